```python
import jax, jax.numpy as jnp
from jax import lax
import numpy as np

D_MODEL = 1024
BATCH = 8
SEQ = 8192
DEPTH = 2

MEM_LEN = 256
D_FF = 2816
N_EVEN = (DEPTH + 1) // 2
N_ODD = DEPTH // 2

CONV_A_CH = 512
CONV_A_WIDTH = 31
SWA_HEADS = 8
SWA_KV_HEADS = 2
SWA_GROUP = SWA_HEADS // SWA_KV_HEADS
HEAD_DIM = 64
WINDOW = 128
BLOCK = 128
EVEN_IN = 2 * CONV_A_CH + (SWA_HEADS + 2 * SWA_KV_HEADS) * HEAD_DIM
EVEN_MIX = CONV_A_CH + SWA_HEADS * HEAD_DIM
SC_CH = 1024
SC_WIDTH = 3
XA_HEADS = 4
XA_HEAD_DIM = D_MODEL // XA_HEADS

RMS_EPS = 1e-6
LN_EPS = 1e-5

kernel_name = "hybrid_conformer_swa_shortconv_macaron"


def rmsnorm(x, g):
    x32 = x.astype(jnp.float32)
    y = x32 * lax.rsqrt(jnp.mean(x32 * x32, axis=-1, keepdims=True) + RMS_EPS)
    return y.astype(x.dtype) * g


def layernorm(x, g, b):
    x32 = x.astype(jnp.float32)
    mu = jnp.mean(x32, axis=-1, keepdims=True)
    var = jnp.mean(jnp.square(x32 - mu), axis=-1, keepdims=True)
    y = (x32 - mu) * lax.rsqrt(var + LN_EPS)
    return y.astype(x.dtype) * g + b


def swiglu(u, w_gu, w_down):
    gu = u @ w_gu
    return (jax.nn.silu(gu[..., :D_FF]) * gu[..., D_FF:]) @ w_down


def causal_depthwise_conv(x, w):
    k_width, ch = w.shape
    return lax.conv_general_dilated(
        x, w[:, None, :].astype(x.dtype), window_strides=(1,), padding=[(k_width - 1, 0)],
        dimension_numbers=("NWC", "WIO", "NWC"), feature_group_count=ch)


def alibi_slopes(n_heads):
    return 2.0 ** (-8.0 * jnp.arange(1, n_heads + 1, dtype=jnp.float32) / n_heads)


def conformer_conv(a_val, a_gate, conv_w, conv_b, ln_g, ln_b):
    a = a_val * jax.nn.sigmoid(a_gate)
    a = causal_depthwise_conv(a, conv_w) + conv_b
    return jax.nn.silu(layernorm(a, ln_g, ln_b))


def sliding_window_gqa(q, k, v, sinks):
    bsz, seq = q.shape[:2]
    nb = seq // BLOCK
    qb = q.reshape(bsz, nb, BLOCK, SWA_KV_HEADS, SWA_GROUP, HEAD_DIM)
    kb = k.reshape(bsz, nb, BLOCK, SWA_KV_HEADS, HEAD_DIM)
    vb = v.reshape(bsz, nb, BLOCK, SWA_KV_HEADS, HEAD_DIM)
    pad = ((0, 0), (1, 0), (0, 0), (0, 0), (0, 0))
    kk = jnp.concatenate([jnp.pad(kb, pad)[:, :-1], kb], axis=2)
    vv = jnp.concatenate([jnp.pad(vb, pad)[:, :-1], vb], axis=2)
    scores = jnp.einsum("bnqkgd,bnskd->bnkgqs", qb, kk).astype(jnp.float32) * (HEAD_DIM ** -0.5)
    dist = jnp.arange(BLOCK)[:, None] + BLOCK - jnp.arange(2 * BLOCK)[None, :]
    key_pos = jnp.arange(nb)[:, None] * BLOCK - BLOCK + jnp.arange(2 * BLOCK)[None, :]
    valid = ((dist >= 0) & (dist < WINDOW))[None] & (key_pos >= 0)[:, None, :]
    slopes = alibi_slopes(SWA_HEADS).reshape(SWA_KV_HEADS, SWA_GROUP)
    scores = scores - slopes[:, :, None, None] * dist.astype(jnp.float32)
    scores = jnp.where(valid[None, :, None, None], scores, -jnp.inf)
    sink = jnp.broadcast_to(
        sinks.astype(jnp.float32).reshape(SWA_KV_HEADS, SWA_GROUP)[None, None, :, :, None, None],
        scores.shape[:-1] + (1,))
    probs = jax.nn.softmax(jnp.concatenate([scores, sink], axis=-1), axis=-1)[..., :-1]
    out = jnp.einsum("bnkgqs,bnskd->bnqkgd", probs.astype(vv.dtype), vv)
    return out.reshape(bsz, seq, SWA_HEADS * HEAD_DIM)


def even_mixer(u, w_in, conv_w, conv_b, ln_g, ln_b, sinks, w_out):
    bsz, seq, _ = u.shape
    z = u @ w_in
    o0 = CONV_A_CH
    o1 = o0 + CONV_A_CH
    o2 = o1 + SWA_HEADS * HEAD_DIM
    o3 = o2 + SWA_KV_HEADS * HEAD_DIM
    a = conformer_conv(z[..., :o0], z[..., o0:o1], conv_w, conv_b, ln_g, ln_b)
    q = z[..., o1:o2].reshape(bsz, seq, SWA_KV_HEADS, SWA_GROUP, HEAD_DIM)
    k = z[..., o2:o3].reshape(bsz, seq, SWA_KV_HEADS, HEAD_DIM)
    v = z[..., o3:].reshape(bsz, seq, SWA_KV_HEADS, HEAD_DIM)
    o = sliding_window_gqa(q, k, v, sinks)
    return jnp.concatenate([a, o], axis=-1) @ w_out


def odd_mixer(u, w_in, conv_w, w_out):
    z = u @ w_in
    gate_b = z[..., :SC_CH]
    gate_c = z[..., SC_CH:2 * SC_CH]
    val = z[..., 2 * SC_CH:]
    y = gate_b * causal_depthwise_conv(gate_c * val, conv_w)
    return y @ w_out


def memory_cross_attention(u, m, wq, wkv, wo):
    bsz, seq, _ = u.shape
    mlen = m.shape[1]
    q = (u @ wq).reshape(bsz, seq, XA_HEADS, XA_HEAD_DIM)
    kv = m @ wkv
    k = kv[..., :D_MODEL].reshape(bsz, mlen, XA_HEADS, XA_HEAD_DIM)
    v = kv[..., D_MODEL:].reshape(bsz, mlen, XA_HEADS, XA_HEAD_DIM)
    s = jnp.einsum("bqhd,bkhd->bhqk", q, k).astype(jnp.float32) * (XA_HEAD_DIM ** -0.5)
    p = jax.nn.softmax(s, axis=-1).astype(v.dtype)
    o = jnp.einsum("bhqk,bkhd->bqhd", p, v).reshape(bsz, seq, D_MODEL)
    return o @ wo


def _normal(key, shape, fan_in):
    return jax.random.normal(key, shape, jnp.float32) * (fan_in ** -0.5)


def _gain(key, shape):
    return 1.0 + 0.05 * jax.random.normal(key, shape, jnp.float32)


def _fwd_setup_inputs(seed: int = 0) -> dict:
    key = jax.random.key(seed)
    ks = jax.random.split(key, 32)
    D, F = D_MODEL, D_FF
    return {
        "x": jax.random.normal(ks[0], (BATCH, SEQ, D), jnp.float32),
        "mem": jax.random.normal(ks[1], (BATCH, MEM_LEN, D), jnp.float32),
        "ffn1_norm": _gain(ks[2], (DEPTH, D)),
        "ffn1_w_gu": _normal(ks[3], (DEPTH, D, 2 * F), D),
        "ffn1_w_down": _normal(ks[4], (DEPTH, F, D), F),
        "mix_norm": _gain(ks[5], (DEPTH, D)),
        "even_w_in": _normal(ks[6], (N_EVEN, D, EVEN_IN), D),
        "conv_a_w": _normal(ks[7], (N_EVEN, CONV_A_WIDTH, CONV_A_CH), CONV_A_WIDTH),
        "conv_a_b": 0.02 * jax.random.normal(ks[8], (N_EVEN, CONV_A_CH), jnp.float32),
        "conv_a_ln_g": _gain(ks[9], (N_EVEN, CONV_A_CH)),
        "conv_a_ln_b": 0.02 * jax.random.normal(ks[10], (N_EVEN, CONV_A_CH), jnp.float32),
        "swa_sinks": 0.5 * jax.random.normal(ks[11], (N_EVEN, SWA_HEADS), jnp.float32),
        "even_w_out": _normal(ks[12], (N_EVEN, EVEN_MIX, D), EVEN_MIX),
        "odd_w_in": _normal(ks[13], (N_ODD, D, 3 * SC_CH), D),
        "sc_conv_w": _normal(ks[14], (N_ODD, SC_WIDTH, SC_CH), SC_WIDTH),
        "odd_w_out": _normal(ks[15], (N_ODD, SC_CH, D), SC_CH),
        "xa_norm": _gain(ks[16], (DEPTH, D)),
        "xa_mem_norm": _gain(ks[17], (DEPTH, D)),
        "xa_wq": _normal(ks[18], (DEPTH, D, D), D),
        "xa_wkv": _normal(ks[19], (DEPTH, D, 2 * D), D),
        "xa_wo": _normal(ks[20], (DEPTH, D, D), D),
        "ffn2_norm": _gain(ks[21], (DEPTH, D)),
        "ffn2_w_gu": _normal(ks[22], (DEPTH, D, 2 * F), D),
        "ffn2_w_down": _normal(ks[23], (DEPTH, F, D), F),
        "final_norm": _gain(ks[24], (D,)),
    }


def _fwd_reference(x, mem, ffn1_norm, ffn1_w_gu, ffn1_w_down, mix_norm, even_w_in, conv_a_w, conv_a_b,
              conv_a_ln_g, conv_a_ln_b, swa_sinks, even_w_out, odd_w_in, sc_conv_w, odd_w_out,
              xa_norm, xa_mem_norm, xa_wq, xa_wkv, xa_wo, ffn2_norm, ffn2_w_gu, ffn2_w_down, final_norm):
    h = x
    for i in range(DEPTH):
        h = h + 0.5 * swiglu(rmsnorm(h, ffn1_norm[i]), ffn1_w_gu[i], ffn1_w_down[i])
        u = rmsnorm(h, mix_norm[i])
        j = i // 2
        if i % 2 == 0:
            h = h + even_mixer(u, even_w_in[j], conv_a_w[j], conv_a_b[j], conv_a_ln_g[j],
                               conv_a_ln_b[j], swa_sinks[j], even_w_out[j])
        else:
            h = h + odd_mixer(u, odd_w_in[j], sc_conv_w[j], odd_w_out[j])
        h = h + memory_cross_attention(rmsnorm(h, xa_norm[i]), rmsnorm(mem, xa_mem_norm[i]),
                                       xa_wq[i], xa_wkv[i], xa_wo[i])
        h = h + 0.5 * swiglu(rmsnorm(h, ffn2_norm[i]), ffn2_w_gu[i], ffn2_w_down[i])
    return rmsnorm(h, final_norm)


import jax as _jax
import jax.numpy as _jnp

TWIN_FORMAT = 'train_step'
FWD_PARAMS = ['x', 'mem', 'ffn1_norm', 'ffn1_w_gu', 'ffn1_w_down', 'mix_norm', 'even_w_in', 'conv_a_w', 'conv_a_b', 'conv_a_ln_g', 'conv_a_ln_b', 'swa_sinks', 'even_w_out', 'odd_w_in', 'sc_conv_w', 'odd_w_out', 'xa_norm', 'xa_mem_norm', 'xa_wq', 'xa_wkv', 'xa_wo', 'ffn2_norm', 'ffn2_w_gu', 'ffn2_w_down', 'final_norm']
TWIN_WEIGHTS = ['ffn1_norm', 'ffn1_w_gu', 'ffn1_w_down', 'mix_norm', 'even_w_in', 'conv_a_w', 'conv_a_b', 'conv_a_ln_g', 'conv_a_ln_b', 'swa_sinks', 'even_w_out', 'odd_w_in', 'sc_conv_w', 'odd_w_out', 'xa_norm', 'xa_mem_norm', 'xa_wq', 'xa_wkv', 'xa_wo', 'ffn2_norm', 'ffn2_w_gu', 'ffn2_w_down', 'final_norm']
TWIN_DIFF_INPUT = 'x'
TWIN_INPUTS = ['x', 'mem', 'ffn1_norm', 'ffn1_w_gu', 'ffn1_w_down', 'mix_norm', 'even_w_in', 'conv_a_w', 'conv_a_b', 'conv_a_ln_g', 'conv_a_ln_b', 'swa_sinks', 'even_w_out', 'odd_w_in', 'sc_conv_w', 'odd_w_out', 'xa_norm', 'xa_mem_norm', 'xa_wq', 'xa_wkv', 'xa_wo', 'ffn2_norm', 'ffn2_w_gu', 'ffn2_w_down', 'final_norm', 'loss_target', 'm_ffn1_norm', 'm_ffn1_w_gu', 'm_ffn1_w_down', 'm_mix_norm', 'm_even_w_in', 'm_conv_a_w', 'm_conv_a_b', 'm_conv_a_ln_g', 'm_conv_a_ln_b', 'm_swa_sinks', 'm_even_w_out', 'm_odd_w_in', 'm_sc_conv_w', 'm_odd_w_out', 'm_xa_norm', 'm_xa_mem_norm', 'm_xa_wq', 'm_xa_wkv', 'm_xa_wo', 'm_ffn2_norm', 'm_ffn2_w_gu', 'm_ffn2_w_down', 'm_final_norm', 'v_ffn1_norm', 'v_ffn1_w_gu', 'v_ffn1_w_down', 'v_mix_norm', 'v_even_w_in', 'v_conv_a_w', 'v_conv_a_b', 'v_conv_a_ln_g', 'v_conv_a_ln_b', 'v_swa_sinks', 'v_even_w_out', 'v_odd_w_in', 'v_sc_conv_w', 'v_odd_w_out', 'v_xa_norm', 'v_xa_mem_norm', 'v_xa_wq', 'v_xa_wkv', 'v_xa_wo', 'v_ffn2_norm', 'v_ffn2_w_gu', 'v_ffn2_w_down', 'v_final_norm']
TWIN_OUTPUTS = ['loss', 'grad_x', 'grad_ffn1_norm', 'grad_ffn1_w_gu', 'grad_ffn1_w_down', 'grad_mix_norm', 'grad_even_w_in', 'grad_conv_a_w', 'grad_conv_a_b', 'grad_conv_a_ln_g', 'grad_conv_a_ln_b', 'grad_swa_sinks', 'grad_even_w_out', 'grad_odd_w_in', 'grad_sc_conv_w', 'grad_odd_w_out', 'grad_xa_norm', 'grad_xa_mem_norm', 'grad_xa_wq', 'grad_xa_wkv', 'grad_xa_wo', 'grad_ffn2_norm', 'grad_ffn2_w_gu', 'grad_ffn2_w_down', 'grad_final_norm', 'delta_ffn1_norm', 'delta_ffn1_w_gu', 'delta_ffn1_w_down', 'delta_mix_norm', 'delta_even_w_in', 'delta_conv_a_w', 'delta_conv_a_b', 'delta_conv_a_ln_g', 'delta_conv_a_ln_b', 'delta_swa_sinks', 'delta_even_w_out', 'delta_odd_w_in', 'delta_sc_conv_w', 'delta_odd_w_out', 'delta_xa_norm', 'delta_xa_mem_norm', 'delta_xa_wq', 'delta_xa_wkv', 'delta_xa_wo', 'delta_ffn2_norm', 'delta_ffn2_w_gu', 'delta_ffn2_w_down', 'delta_final_norm', 'new_m_ffn1_norm', 'new_m_ffn1_w_gu', 'new_m_ffn1_w_down', 'new_m_mix_norm', 'new_m_even_w_in', 'new_m_conv_a_w', 'new_m_conv_a_b', 'new_m_conv_a_ln_g', 'new_m_conv_a_ln_b', 'new_m_swa_sinks', 'new_m_even_w_out', 'new_m_odd_w_in', 'new_m_sc_conv_w', 'new_m_odd_w_out', 'new_m_xa_norm', 'new_m_xa_mem_norm', 'new_m_xa_wq', 'new_m_xa_wkv', 'new_m_xa_wo', 'new_m_ffn2_norm', 'new_m_ffn2_w_gu', 'new_m_ffn2_w_down', 'new_m_final_norm', 'new_v_ffn1_norm', 'new_v_ffn1_w_gu', 'new_v_ffn1_w_down', 'new_v_mix_norm', 'new_v_even_w_in', 'new_v_conv_a_w', 'new_v_conv_a_b', 'new_v_conv_a_ln_g', 'new_v_conv_a_ln_b', 'new_v_swa_sinks', 'new_v_even_w_out', 'new_v_odd_w_in', 'new_v_sc_conv_w', 'new_v_odd_w_out', 'new_v_xa_norm', 'new_v_xa_mem_norm', 'new_v_xa_wq', 'new_v_xa_wkv', 'new_v_xa_wo', 'new_v_ffn2_norm', 'new_v_ffn2_w_gu', 'new_v_ffn2_w_down', 'new_v_final_norm']
TWIN_LEAF_KINDS = {'loss': 'loss', 'grad_x': 'grad_x', 'grad_ffn1_norm': 'grad_w', 'grad_ffn1_w_gu': 'grad_w', 'grad_ffn1_w_down': 'grad_w', 'grad_mix_norm': 'grad_w', 'grad_even_w_in': 'grad_w', 'grad_conv_a_w': 'grad_w', 'grad_conv_a_b': 'grad_w', 'grad_conv_a_ln_g': 'grad_w', 'grad_conv_a_ln_b': 'grad_w', 'grad_swa_sinks': 'grad_w', 'grad_even_w_out': 'grad_w', 'grad_odd_w_in': 'grad_w', 'grad_sc_conv_w': 'grad_w', 'grad_odd_w_out': 'grad_w', 'grad_xa_norm': 'grad_w', 'grad_xa_mem_norm': 'grad_w', 'grad_xa_wq': 'grad_w', 'grad_xa_wkv': 'grad_w', 'grad_xa_wo': 'grad_w', 'grad_ffn2_norm': 'grad_w', 'grad_ffn2_w_gu': 'grad_w', 'grad_ffn2_w_down': 'grad_w', 'grad_final_norm': 'grad_w', 'delta_ffn1_norm': 'delta_w', 'delta_ffn1_w_gu': 'delta_w', 'delta_ffn1_w_down': 'delta_w', 'delta_mix_norm': 'delta_w', 'delta_even_w_in': 'delta_w', 'delta_conv_a_w': 'delta_w', 'delta_conv_a_b': 'delta_w', 'delta_conv_a_ln_g': 'delta_w', 'delta_conv_a_ln_b': 'delta_w', 'delta_swa_sinks': 'delta_w', 'delta_even_w_out': 'delta_w', 'delta_odd_w_in': 'delta_w', 'delta_sc_conv_w': 'delta_w', 'delta_odd_w_out': 'delta_w', 'delta_xa_norm': 'delta_w', 'delta_xa_mem_norm': 'delta_w', 'delta_xa_wq': 'delta_w', 'delta_xa_wkv': 'delta_w', 'delta_xa_wo': 'delta_w', 'delta_ffn2_norm': 'delta_w', 'delta_ffn2_w_gu': 'delta_w', 'delta_ffn2_w_down': 'delta_w', 'delta_final_norm': 'delta_w', 'new_m_ffn1_norm': 'new_m', 'new_m_ffn1_w_gu': 'new_m', 'new_m_ffn1_w_down': 'new_m', 'new_m_mix_norm': 'new_m', 'new_m_even_w_in': 'new_m', 'new_m_conv_a_w': 'new_m', 'new_m_conv_a_b': 'new_m', 'new_m_conv_a_ln_g': 'new_m', 'new_m_conv_a_ln_b': 'new_m', 'new_m_swa_sinks': 'new_m', 'new_m_even_w_out': 'new_m', 'new_m_odd_w_in': 'new_m', 'new_m_sc_conv_w': 'new_m', 'new_m_odd_w_out': 'new_m', 'new_m_xa_norm': 'new_m', 'new_m_xa_mem_norm': 'new_m', 'new_m_xa_wq': 'new_m', 'new_m_xa_wkv': 'new_m', 'new_m_xa_wo': 'new_m', 'new_m_ffn2_norm': 'new_m', 'new_m_ffn2_w_gu': 'new_m', 'new_m_ffn2_w_down': 'new_m', 'new_m_final_norm': 'new_m', 'new_v_ffn1_norm': 'new_v', 'new_v_ffn1_w_gu': 'new_v', 'new_v_ffn1_w_down': 'new_v', 'new_v_mix_norm': 'new_v', 'new_v_even_w_in': 'new_v', 'new_v_conv_a_w': 'new_v', 'new_v_conv_a_b': 'new_v', 'new_v_conv_a_ln_g': 'new_v', 'new_v_conv_a_ln_b': 'new_v', 'new_v_swa_sinks': 'new_v', 'new_v_even_w_out': 'new_v', 'new_v_odd_w_in': 'new_v', 'new_v_sc_conv_w': 'new_v', 'new_v_odd_w_out': 'new_v', 'new_v_xa_norm': 'new_v', 'new_v_xa_mem_norm': 'new_v', 'new_v_xa_wq': 'new_v', 'new_v_xa_wkv': 'new_v', 'new_v_xa_wo': 'new_v', 'new_v_ffn2_norm': 'new_v', 'new_v_ffn2_w_gu': 'new_v', 'new_v_ffn2_w_down': 'new_v', 'new_v_final_norm': 'new_v'}


def _forward(args):
    return _fwd_reference(*[args[k] for k in FWD_PARAMS])


def _output_shape():
    def fwd():
        inp = _fwd_setup_inputs(0)
        return _fwd_reference(*[inp[k] for k in FWD_PARAMS])
    out = _jax.eval_shape(fwd)
    return out.shape, out.dtype

N_MICROBATCH = 1
ADAM_LR = 0.001
ADAM_B1 = 0.9
ADAM_B2 = 0.999
ADAM_EPS = 1e-08
ADAM_WD = 0.01
ADAM_STEP = 10
PER_EXAMPLE_BATCH_AXIS = {'x': 0, 'mem': 0, 'loss_target': 0}
SHARED_INPUTS = []
_WEIGHT_DTYPES = {'ffn1_norm': _jnp.float32, 'ffn1_w_gu': _jnp.float32, 'ffn1_w_down': _jnp.float32, 'mix_norm': _jnp.float32, 'even_w_in': _jnp.float32, 'conv_a_w': _jnp.float32, 'conv_a_b': _jnp.float32, 'conv_a_ln_g': _jnp.float32, 'conv_a_ln_b': _jnp.float32, 'swa_sinks': _jnp.float32, 'even_w_out': _jnp.float32, 'odd_w_in': _jnp.float32, 'sc_conv_w': _jnp.float32, 'odd_w_out': _jnp.float32, 'xa_norm': _jnp.float32, 'xa_mem_norm': _jnp.float32, 'xa_wq': _jnp.float32, 'xa_wkv': _jnp.float32, 'xa_wo': _jnp.float32, 'ffn2_norm': _jnp.float32, 'ffn2_w_gu': _jnp.float32, 'ffn2_w_down': _jnp.float32, 'final_norm': _jnp.float32}
MOMENT_SCALE = {'ffn1_norm': 1.406210e-01, 'ffn1_w_gu': 5.974912e-02, 'ffn1_w_down': 9.757169e-02, 'mix_norm': 2.567209e-01, 'even_w_in': 1.373719e-01, 'conv_a_w': 2.016905e-01, 'conv_a_b': 4.406785e-01, 'conv_a_ln_g': 2.444456e-01, 'conv_a_ln_b': 2.125485e-01, 'swa_sinks': 1.813650e-01, 'even_w_out': 1.523071e-01, 'odd_w_in': 1.621011e-01, 'sc_conv_w': 1.681170e-01, 'odd_w_out': 1.691046e-01, 'xa_norm': 2.871482e-02, 'xa_mem_norm': 3.984389e-02, 'xa_wq': 2.723669e-02, 'xa_wkv': 2.731302e-02, 'xa_wo': 2.749321e-02, 'ffn2_norm': 1.019525e-01, 'ffn2_w_gu': 4.415773e-02, 'ffn2_w_down': 7.240141e-02, 'final_norm': 6.417233e+01}


def _to_microbatches(a, axis):
    t = _jnp.moveaxis(a, axis, 0)
    t = t.reshape((N_MICROBATCH, t.shape[0] // N_MICROBATCH) + t.shape[1:])
    return _jnp.moveaxis(t, 1, axis + 1)


def setup_inputs(seed: int = 0) -> dict:
    inp = _fwd_setup_inputs(seed)
    key = _jax.random.fold_in(_jax.random.key(seed), 7919)
    shape, _ = _output_shape()
    out = dict(inp)
    out["loss_target"] = _jax.random.normal(_jax.random.fold_in(key, 0), shape, _jnp.float32)
    for i, name in enumerate(TWIN_WEIGHTS):
        w = inp[name].astype(_jnp.float32)
        if MOMENT_SCALE is None:
            s = _jnp.sqrt(_jnp.mean(_jnp.square(w)) + 1e-30)
        else:
            s = MOMENT_SCALE[name]
        km, kv = _jax.random.split(_jax.random.fold_in(key, i + 1))
        out[name] = w
        out["m_" + name] = s * _jax.random.normal(km, w.shape, _jnp.float32)
        out["v_" + name] = (s * s) * _jax.random.uniform(kv, w.shape, _jnp.float32, 0.5, 1.5)
    if N_MICROBATCH > 1:
        for name, axis in PER_EXAMPLE_BATCH_AXIS.items():
            out[name] = _to_microbatches(out[name], axis)
    return {'x': out['x'], 'mem': out['mem'], 'ffn1_norm': out['ffn1_norm'], 'ffn1_w_gu': out['ffn1_w_gu'], 'ffn1_w_down': out['ffn1_w_down'], 'mix_norm': out['mix_norm'], 'even_w_in': out['even_w_in'], 'conv_a_w': out['conv_a_w'], 'conv_a_b': out['conv_a_b'], 'conv_a_ln_g': out['conv_a_ln_g'], 'conv_a_ln_b': out['conv_a_ln_b'], 'swa_sinks': out['swa_sinks'], 'even_w_out': out['even_w_out'], 'odd_w_in': out['odd_w_in'], 'sc_conv_w': out['sc_conv_w'], 'odd_w_out': out['odd_w_out'], 'xa_norm': out['xa_norm'], 'xa_mem_norm': out['xa_mem_norm'], 'xa_wq': out['xa_wq'], 'xa_wkv': out['xa_wkv'], 'xa_wo': out['xa_wo'], 'ffn2_norm': out['ffn2_norm'], 'ffn2_w_gu': out['ffn2_w_gu'], 'ffn2_w_down': out['ffn2_w_down'], 'final_norm': out['final_norm'], 'loss_target': out['loss_target'], 'm_ffn1_norm': out['m_ffn1_norm'], 'm_ffn1_w_gu': out['m_ffn1_w_gu'], 'm_ffn1_w_down': out['m_ffn1_w_down'], 'm_mix_norm': out['m_mix_norm'], 'm_even_w_in': out['m_even_w_in'], 'm_conv_a_w': out['m_conv_a_w'], 'm_conv_a_b': out['m_conv_a_b'], 'm_conv_a_ln_g': out['m_conv_a_ln_g'], 'm_conv_a_ln_b': out['m_conv_a_ln_b'], 'm_swa_sinks': out['m_swa_sinks'], 'm_even_w_out': out['m_even_w_out'], 'm_odd_w_in': out['m_odd_w_in'], 'm_sc_conv_w': out['m_sc_conv_w'], 'm_odd_w_out': out['m_odd_w_out'], 'm_xa_norm': out['m_xa_norm'], 'm_xa_mem_norm': out['m_xa_mem_norm'], 'm_xa_wq': out['m_xa_wq'], 'm_xa_wkv': out['m_xa_wkv'], 'm_xa_wo': out['m_xa_wo'], 'm_ffn2_norm': out['m_ffn2_norm'], 'm_ffn2_w_gu': out['m_ffn2_w_gu'], 'm_ffn2_w_down': out['m_ffn2_w_down'], 'm_final_norm': out['m_final_norm'], 'v_ffn1_norm': out['v_ffn1_norm'], 'v_ffn1_w_gu': out['v_ffn1_w_gu'], 'v_ffn1_w_down': out['v_ffn1_w_down'], 'v_mix_norm': out['v_mix_norm'], 'v_even_w_in': out['v_even_w_in'], 'v_conv_a_w': out['v_conv_a_w'], 'v_conv_a_b': out['v_conv_a_b'], 'v_conv_a_ln_g': out['v_conv_a_ln_g'], 'v_conv_a_ln_b': out['v_conv_a_ln_b'], 'v_swa_sinks': out['v_swa_sinks'], 'v_even_w_out': out['v_even_w_out'], 'v_odd_w_in': out['v_odd_w_in'], 'v_sc_conv_w': out['v_sc_conv_w'], 'v_odd_w_out': out['v_odd_w_out'], 'v_xa_norm': out['v_xa_norm'], 'v_xa_mem_norm': out['v_xa_mem_norm'], 'v_xa_wq': out['v_xa_wq'], 'v_xa_wkv': out['v_xa_wkv'], 'v_xa_wo': out['v_xa_wo'], 'v_ffn2_norm': out['v_ffn2_norm'], 'v_ffn2_w_gu': out['v_ffn2_w_gu'], 'v_ffn2_w_down': out['v_ffn2_w_down'], 'v_final_norm': out['v_final_norm']}


def _loss(weights, diff, rest, loss_target):
    with _jax.named_scope("forward"):
        args = {**rest, TWIN_DIFF_INPUT: diff, **{k: w.astype(_WEIGHT_DTYPES[k]) for k, w in weights.items()}}
        y = _forward(args)
    with _jax.named_scope("loss_head"):
        err = _jnp.square(y.astype(_jnp.float32) - loss_target)
        return 0.5 * _jnp.sum(_jnp.mean(err, axis=-1)) if err.ndim else 0.5 * err


def _adamw(w, g, m, v):
    m = ADAM_B1 * m + (1.0 - ADAM_B1) * g
    v = ADAM_B2 * v + (1.0 - ADAM_B2) * _jnp.square(g)
    m_hat = m / (1.0 - ADAM_B1 ** ADAM_STEP)
    v_hat = v / (1.0 - ADAM_B2 ** ADAM_STEP)
    delta = -ADAM_LR * (m_hat / (_jnp.sqrt(v_hat) + ADAM_EPS) + ADAM_WD * w)
    return delta, m, v


def reference(x, mem, ffn1_norm, ffn1_w_gu, ffn1_w_down, mix_norm, even_w_in, conv_a_w, conv_a_b, conv_a_ln_g, conv_a_ln_b, swa_sinks, even_w_out, odd_w_in, sc_conv_w, odd_w_out, xa_norm, xa_mem_norm, xa_wq, xa_wkv, xa_wo, ffn2_norm, ffn2_w_gu, ffn2_w_down, final_norm, loss_target, m_ffn1_norm, m_ffn1_w_gu, m_ffn1_w_down, m_mix_norm, m_even_w_in, m_conv_a_w, m_conv_a_b, m_conv_a_ln_g, m_conv_a_ln_b, m_swa_sinks, m_even_w_out, m_odd_w_in, m_sc_conv_w, m_odd_w_out, m_xa_norm, m_xa_mem_norm, m_xa_wq, m_xa_wkv, m_xa_wo, m_ffn2_norm, m_ffn2_w_gu, m_ffn2_w_down, m_final_norm, v_ffn1_norm, v_ffn1_w_gu, v_ffn1_w_down, v_mix_norm, v_even_w_in, v_conv_a_w, v_conv_a_b, v_conv_a_ln_g, v_conv_a_ln_b, v_swa_sinks, v_even_w_out, v_odd_w_in, v_sc_conv_w, v_odd_w_out, v_xa_norm, v_xa_mem_norm, v_xa_wq, v_xa_wkv, v_xa_wo, v_ffn2_norm, v_ffn2_w_gu, v_ffn2_w_down, v_final_norm):
    given = dict(x=x, mem=mem, ffn1_norm=ffn1_norm, ffn1_w_gu=ffn1_w_gu, ffn1_w_down=ffn1_w_down, mix_norm=mix_norm, even_w_in=even_w_in, conv_a_w=conv_a_w, conv_a_b=conv_a_b, conv_a_ln_g=conv_a_ln_g, conv_a_ln_b=conv_a_ln_b, swa_sinks=swa_sinks, even_w_out=even_w_out, odd_w_in=odd_w_in, sc_conv_w=sc_conv_w, odd_w_out=odd_w_out, xa_norm=xa_norm, xa_mem_norm=xa_mem_norm, xa_wq=xa_wq, xa_wkv=xa_wkv, xa_wo=xa_wo, ffn2_norm=ffn2_norm, ffn2_w_gu=ffn2_w_gu, ffn2_w_down=ffn2_w_down, final_norm=final_norm, loss_target=loss_target, m_ffn1_norm=m_ffn1_norm, m_ffn1_w_gu=m_ffn1_w_gu, m_ffn1_w_down=m_ffn1_w_down, m_mix_norm=m_mix_norm, m_even_w_in=m_even_w_in, m_conv_a_w=m_conv_a_w, m_conv_a_b=m_conv_a_b, m_conv_a_ln_g=m_conv_a_ln_g, m_conv_a_ln_b=m_conv_a_ln_b, m_swa_sinks=m_swa_sinks, m_even_w_out=m_even_w_out, m_odd_w_in=m_odd_w_in, m_sc_conv_w=m_sc_conv_w, m_odd_w_out=m_odd_w_out, m_xa_norm=m_xa_norm, m_xa_mem_norm=m_xa_mem_norm, m_xa_wq=m_xa_wq, m_xa_wkv=m_xa_wkv, m_xa_wo=m_xa_wo, m_ffn2_norm=m_ffn2_norm, m_ffn2_w_gu=m_ffn2_w_gu, m_ffn2_w_down=m_ffn2_w_down, m_final_norm=m_final_norm, v_ffn1_norm=v_ffn1_norm, v_ffn1_w_gu=v_ffn1_w_gu, v_ffn1_w_down=v_ffn1_w_down, v_mix_norm=v_mix_norm, v_even_w_in=v_even_w_in, v_conv_a_w=v_conv_a_w, v_conv_a_b=v_conv_a_b, v_conv_a_ln_g=v_conv_a_ln_g, v_conv_a_ln_b=v_conv_a_ln_b, v_swa_sinks=v_swa_sinks, v_even_w_out=v_even_w_out, v_odd_w_in=v_odd_w_in, v_sc_conv_w=v_sc_conv_w, v_odd_w_out=v_odd_w_out, v_xa_norm=v_xa_norm, v_xa_mem_norm=v_xa_mem_norm, v_xa_wq=v_xa_wq, v_xa_wkv=v_xa_wkv, v_xa_wo=v_xa_wo, v_ffn2_norm=v_ffn2_norm, v_ffn2_w_gu=v_ffn2_w_gu, v_ffn2_w_down=v_ffn2_w_down, v_final_norm=v_final_norm)
    weights = {n: given[n] for n in TWIN_WEIGHTS}
    shared = {n: given[n] for n in SHARED_INPUTS}
    per_example = {n: given[n] for n in ['x', 'mem']}
    grad_fn = _jax.value_and_grad(_loss, argnums=(0, 1))

    def one_microbatch(ex, loss_target):
        ex = dict(ex)
        diff = ex.pop(TWIN_DIFF_INPUT)
        return grad_fn(weights, diff, {**shared, **ex}, loss_target)

    if N_MICROBATCH == 1:
        loss, (grad_w, grad_x) = one_microbatch(per_example, given["loss_target"])
    else:
        def body(carry, xs):
            loss_sum, grad_sum = carry
            l_k, (gw_k, gx_k) = one_microbatch(xs[0], xs[1])
            with _jax.named_scope("update"):
                return (loss_sum + l_k, _jax.tree.map(_jnp.add, grad_sum, gw_k)), gx_k

        init = (_jnp.zeros((), _jnp.float32), _jax.tree.map(_jnp.zeros_like, weights))
        (loss, grad_w), grad_x = _jax.lax.scan(body, init, (per_example, given["loss_target"]))
    with _jax.named_scope("update"):
        delta_w, new_m, new_v = {}, {}, {}
        for n in TWIN_WEIGHTS:
            delta_w[n], new_m[n], new_v[n] = _adamw(weights[n], grad_w[n], given["m_" + n], given["v_" + n])
    return (loss, grad_x, *[grad_w[n] for n in TWIN_WEIGHTS], *[delta_w[n] for n in TWIN_WEIGHTS],
            *[new_m[n] for n in TWIN_WEIGHTS], *[new_v[n] for n in TWIN_WEIGHTS])
```

```python
import functools
import math

import jax
import jax.numpy as jnp
from jax import lax
from jax.experimental import pallas as pl
from jax.experimental.pallas import tpu as pltpu

F32 = jnp.float32
BF16 = jnp.bfloat16

D_MODEL = 1024
D_FF = 2816
CONV_A_CH = 512
CONV_A_WIDTH = 31
SWA_HEADS = 8
SWA_KV_HEADS = 2
SWA_GROUP = 4
HEAD_DIM = 64
WINDOW = 128
SC_CH = 1024
SC_WIDTH = 3
XA_HEADS = 4
XA_HEAD_DIM = 256
RMS_EPS = 1e-6
LN_EPS = 1e-5

ADAM_LR = 0.001
ADAM_B1 = 0.9
ADAM_B2 = 0.999
ADAM_EPS = 1e-08
ADAM_WD = 0.01
ADAM_STEP = 10

N_CHIPS = 4
N_DEV = 8
PACK_COLS = 1024
PACK_ROW_ALIGN = 1024
NEG_BIG = -1e30
VMEM_LIMIT = 56 * 1024 * 1024
MESH = pl.DeviceIdType.MESH

INPUT_NAMES = ['x', 'mem', 'ffn1_norm', 'ffn1_w_gu', 'ffn1_w_down', 'mix_norm', 'even_w_in', 'conv_a_w', 'conv_a_b',
               'conv_a_ln_g', 'conv_a_ln_b', 'swa_sinks', 'even_w_out', 'odd_w_in', 'sc_conv_w', 'odd_w_out', 'xa_norm',
               'xa_mem_norm', 'xa_wq', 'xa_wkv', 'xa_wo', 'ffn2_norm', 'ffn2_w_gu', 'ffn2_w_down', 'final_norm']
WEIGHT_NAMES = INPUT_NAMES[2:]
BIG = [('ffn1_w_gu', 'col'), ('ffn1_w_down', 'row'), ('even_w_in', 'col'), ('conv_a_w', 'col'), ('even_w_out', 'row'),
       ('odd_w_in', 'col'), ('sc_conv_w', 'col'), ('odd_w_out', 'row'), ('xa_wq', 'row'), ('xa_wkv', 'col'),
       ('xa_wo', 'row'), ('ffn2_w_gu', 'col'), ('ffn2_w_down', 'row')]
BIG_NAMES = [n for n, _ in BIG]
SMALL_NAMES = [n for n in WEIGHT_NAMES if n not in BIG_NAMES]


def _cparams(sem=None, vmem=VMEM_LIMIT):
    kw = dict(vmem_limit_bytes=vmem)
    if sem is not None:
        kw['dimension_semantics'] = sem
    return pltpu.CompilerParams(**kw)


def _div_tile(n, want, align=8):
    if n <= want:
        return n
    t = (want // align) * align
    while t >= align:
        if n % t == 0:
            return t
        t -= align
    return n


def _mm(a, b, *, name, ta=False, tb=False, out_dtype=BF16, tm=512, tn=512, tk=512, res=None, scale=1.0):
    if ta:
        K, M = a.shape
    else:
        M, K = a.shape
    if tb:
        N, K2 = b.shape
    else:
        K2, N = b.shape
    assert K == K2, (a.shape, b.shape, ta, tb)
    tm = _div_tile(M, tm, 128 if ta else 16)
    tn = _div_tile(N, tn, 128)
    tk = _div_tile(K, tk, 16 if ta else 128)
    nk = K // tk
    a_spec = pl.BlockSpec((tk, tm), lambda i, j, k: (k, i)) if ta else pl.BlockSpec((tm, tk), lambda i, j, k: (i, k))
    b_spec = pl.BlockSpec((tn, tk), lambda i, j, k: (j, k)) if tb else pl.BlockSpec((tk, tn), lambda i, j, k: (k, j))
    o_spec = pl.BlockSpec((tm, tn), lambda i, j, k: (i, j))
    dims = (((0 if ta else 1,), (1 if tb else 0,)), ((), ()))
    has_res = res is not None

    def body(*refs):
        if has_res:
            a_ref, b_ref, r_ref, o_ref, acc_ref = refs
        else:
            a_ref, b_ref, o_ref, acc_ref = refs
        k = pl.program_id(2)
        p = lax.dot_general(a_ref[...].astype(BF16), b_ref[...].astype(BF16), dims, preferred_element_type=F32)

        @pl.when(k == 0)
        def _():
            acc_ref[...] = p

        @pl.when(k > 0)
        def _():
            acc_ref[...] += p

        @pl.when(k == nk - 1)
        def _():
            r = acc_ref[...] * scale
            if has_res:
                r = r_ref[...] + r
            o_ref[...] = r.astype(out_dtype)

    in_specs = [a_spec, b_spec] + ([o_spec] if has_res else [])
    args = (a, b) + ((res,) if has_res else ())
    return pl.pallas_call(
        body, name=name, grid=(M // tm, N // tn, nk), in_specs=in_specs, out_specs=o_spec,
        out_shape=jax.ShapeDtypeStruct((M, N), out_dtype),
        scratch_shapes=[pltpu.VMEM((tm, tn), F32)],
        compiler_params=_cparams(("parallel", "parallel", "arbitrary")),
    )(*args)


def _rms_fwd(x, g, *, name):
    S, D = x.shape
    ts = _div_tile(S, 512)

    def body(x_ref, g_ref, o_ref):
        xv = x_ref[...]
        r = lax.rsqrt(jnp.mean(xv * xv, axis=-1, keepdims=True) + RMS_EPS)
        o_ref[...] = (xv * r * g_ref[...]).astype(BF16)

    return pl.pallas_call(
        body, name=name, grid=(S // ts,),
        in_specs=[pl.BlockSpec((ts, D), lambda i: (i, 0)), pl.BlockSpec((1, D), lambda i: (0, 0))],
        out_specs=pl.BlockSpec((ts, D), lambda i: (i, 0)),
        out_shape=jax.ShapeDtypeStruct((S, D), BF16),
        compiler_params=_cparams(("parallel",)),
    )(x, g)


def _rms_bwd(x, g, du, dres, *, name):
    S, D = x.shape
    ts = _div_tile(S, 512)
    has_res = dres is not None

    def body(*refs):
        if has_res:
            x_ref, g_ref, du_ref, dr_ref, dx_ref, dg_ref = refs
        else:
            x_ref, g_ref, du_ref, dg_ref = refs
        i = pl.program_id(0)
        xv = x_ref[...]
        duv = du_ref[...].astype(F32)
        r = lax.rsqrt(jnp.mean(xv * xv, axis=-1, keepdims=True) + RMS_EPS)
        xhat = xv * r
        part = jnp.sum(duv * xhat, axis=0, keepdims=True)

        @pl.when(i == 0)
        def _():
            dg_ref[...] = part

        @pl.when(i > 0)
        def _():
            dg_ref[...] += part

        if has_res:
            dxhat = duv * g_ref[...]
            dx = r * (dxhat - xhat * jnp.mean(dxhat * xhat, axis=-1, keepdims=True))
            dx_ref[...] = dr_ref[...] + dx

    row = pl.BlockSpec((ts, D), lambda i: (i, 0))
    vec = pl.BlockSpec((1, D), lambda i: (0, 0))
    if has_res:
        dx, dg = pl.pallas_call(
            body, name=name, grid=(S // ts,), in_specs=[row, vec, row, row], out_specs=[row, vec],
            out_shape=[jax.ShapeDtypeStruct((S, D), F32), jax.ShapeDtypeStruct((1, D), F32)],
            compiler_params=_cparams(("arbitrary",)),
        )(x, g, du, dres)
        return dx, dg
    dg = pl.pallas_call(
        body, name=name, grid=(S // ts,), in_specs=[row, vec, row], out_specs=vec,
        out_shape=jax.ShapeDtypeStruct((1, D), F32),
        compiler_params=_cparams(("arbitrary",)),
    )(x, g, du)
    return None, dg


def _final_loss(h, g, tgt, *, name):
    S, D = h.shape
    ts = _div_tile(S, 512)

    def body(h_ref, g_ref, t_ref, loss_ref, dh_ref, dg_ref):
        i = pl.program_id(0)
        xv = h_ref[...]
        gv = g_ref[...]
        r = lax.rsqrt(jnp.mean(xv * xv, axis=-1, keepdims=True) + RMS_EPS)
        xhat = xv * r
        err = xhat * gv - t_ref[...]
        lpart = 0.5 * jnp.sum(jnp.mean(err * err, axis=-1, keepdims=True), axis=0, keepdims=True)
        dy = err * (1.0 / D)
        gpart = jnp.sum(dy * xhat, axis=0, keepdims=True)

        @pl.when(i == 0)
        def _():
            loss_ref[...] = jnp.broadcast_to(lpart, loss_ref.shape)
            dg_ref[...] = gpart

        @pl.when(i > 0)
        def _():
            loss_ref[...] += jnp.broadcast_to(lpart, loss_ref.shape)
            dg_ref[...] += gpart

        dxhat = dy * gv
        dh_ref[...] = r * (dxhat - xhat * jnp.mean(dxhat * xhat, axis=-1, keepdims=True))

    row = pl.BlockSpec((ts, D), lambda i: (i, 0))
    vec = pl.BlockSpec((1, D), lambda i: (0, 0))
    return pl.pallas_call(
        body, name=name, grid=(S // ts,), in_specs=[row, vec, row],
        out_specs=[pl.BlockSpec((8, 128), lambda i: (0, 0)), row, vec],
        out_shape=[jax.ShapeDtypeStruct((8, 128), F32), jax.ShapeDtypeStruct((S, D), F32),
                   jax.ShapeDtypeStruct((1, D), F32)],
        compiler_params=_cparams(("arbitrary",)),
    )(h, g, tgt)


def _sigmoid(x):
    return 1.0 / (1.0 + jnp.exp(-x))


def _swiglu_fwd(gu, *, name):
    S, F2 = gu.shape
    F = F2 // 2
    ts = _div_tile(S, 256, 16)

    def body(gu_ref, o_ref):
        g = gu_ref[:, pl.ds(0, F)].astype(F32)
        u = gu_ref[:, pl.ds(F, F)].astype(F32)
        o_ref[...] = (g * _sigmoid(g) * u).astype(BF16)

    return pl.pallas_call(
        body, name=name, grid=(S // ts,),
        in_specs=[pl.BlockSpec((ts, F2), lambda i: (i, 0))],
        out_specs=pl.BlockSpec((ts, F), lambda i: (i, 0)),
        out_shape=jax.ShapeDtypeStruct((S, F), BF16),
        compiler_params=_cparams(("parallel",)),
    )(gu)


def _swiglu_bwd(gu, da, *, name):
    S, F2 = gu.shape
    F = F2 // 2
    ts = _div_tile(S, 256, 16)

    def body(gu_ref, da_ref, o_ref):
        g = gu_ref[:, pl.ds(0, F)].astype(F32)
        u = gu_ref[:, pl.ds(F, F)].astype(F32)
        d = da_ref[...].astype(F32)
        sg = _sigmoid(g)
        o_ref[:, pl.ds(0, F)] = (d * u * sg * (1.0 + g * (1.0 - sg))).astype(BF16)
        o_ref[:, pl.ds(F, F)] = (d * g * sg).astype(BF16)

    return pl.pallas_call(
        body, name=name, grid=(S // ts,),
        in_specs=[pl.BlockSpec((ts, F2), lambda i: (i, 0)), pl.BlockSpec((ts, F), lambda i: (i, 0))],
        out_specs=pl.BlockSpec((ts, F2), lambda i: (i, 0)),
        out_shape=jax.ShapeDtypeStruct((S, F2), BF16),
        compiler_params=_cparams(("parallel",)),
    )(gu, da)


CONV_HALO = 32


def _conv_a_fwd(z, w, bias, ln_g, ln_b, *, name):
    S = z.shape[0]
    C = CONV_A_CH
    ts = _div_tile(S, 256, 32)

    def body(val_ref, gate_ref, w_ref, b_ref, g_ref, lb_ref, c_ref, act_ref, win):
        i = pl.program_id(0)

        @pl.when(i == 0)
        def _():
            win[pl.ds(0, CONV_HALO), :] = jnp.zeros((CONV_HALO, C), F32)

        @pl.when(i > 0)
        def _():
            win[pl.ds(0, CONV_HALO), :] = win[pl.ds(ts, CONV_HALO), :]

        a = val_ref[...].astype(F32) * _sigmoid(gate_ref[...].astype(F32))
        win[pl.ds(CONV_HALO, ts), :] = a
        acc = jnp.broadcast_to(b_ref[...], (ts, C))
        for k in range(CONV_A_WIDTH):
            acc = acc + w_ref[pl.ds(k, 1), :] * win[pl.ds(CONV_HALO - (CONV_A_WIDTH - 1) + k, ts), :]
        c_ref[...] = acc
        mu = jnp.mean(acc, axis=-1, keepdims=True)
        xc = acc - mu
        var = jnp.mean(xc * xc, axis=-1, keepdims=True)
        ln = xc * lax.rsqrt(var + LN_EPS) * g_ref[...] + lb_ref[...]
        act_ref[...] = (ln * _sigmoid(ln)).astype(BF16)

    row = lambda col: pl.BlockSpec((ts, C), lambda i, col=col: (i, col))
    vec = pl.BlockSpec((1, C), lambda i: (0, 0))
    return pl.pallas_call(
        body, name=name, grid=(S // ts,),
        in_specs=[row(0), row(1), pl.BlockSpec((32, C), lambda i: (0, 0)), vec, vec, vec],
        out_specs=[row(0), row(0)],
        out_shape=[jax.ShapeDtypeStruct((S, C), F32), jax.ShapeDtypeStruct((S, C), BF16)],
        scratch_shapes=[pltpu.VMEM((ts + CONV_HALO, C), F32)],
        compiler_params=_cparams(("arbitrary",)),
    )(z, z, w, bias, ln_g, ln_b)


def _conv_a_bwd(z, c, dcat, w, ln_g, ln_b, *, name):
    S = z.shape[0]
    C = CONV_A_CH
    ts = _div_tile(S, 256, 32)
    n = S // ts

    def body(val_ref, gate_ref, c_ref, da_ref, w_ref, g_ref, lb_ref, dz_ref, small_ref, win):
        i = pl.program_id(0)

        @pl.when(i == 0)
        def _():
            win[pl.ds(ts, CONV_HALO), :] = jnp.zeros((CONV_HALO, C), F32)
            small_ref[...] = jnp.zeros(small_ref.shape, F32)

        @pl.when(i > 0)
        def _():
            win[pl.ds(ts, CONV_HALO), :] = win[pl.ds(0, CONV_HALO), :]

        cv = c_ref[...]
        gv = g_ref[...]
        mu = jnp.mean(cv, axis=-1, keepdims=True)
        xc = cv - mu
        var = jnp.mean(xc * xc, axis=-1, keepdims=True)
        rstd = lax.rsqrt(var + LN_EPS)
        xhat = xc * rstd
        ln = xhat * gv + lb_ref[...]
        sg = _sigmoid(ln)
        dln = da_ref[...].astype(F32) * (sg * (1.0 + ln * (1.0 - sg)))
        small_ref[pl.ds(33, 1), :] += jnp.sum(dln * xhat, axis=0, keepdims=True)
        small_ref[pl.ds(34, 1), :] += jnp.sum(dln, axis=0, keepdims=True)
        dxhat = dln * gv
        dc = rstd * (dxhat - jnp.mean(dxhat, axis=-1, keepdims=True)
                     - xhat * jnp.mean(dxhat * xhat, axis=-1, keepdims=True))
        small_ref[pl.ds(32, 1), :] += jnp.sum(dc, axis=0, keepdims=True)
        win[pl.ds(0, ts), :] = dc

        val = val_ref[...].astype(F32)
        sgg = _sigmoid(gate_ref[...].astype(F32))
        a = val * sgg
        da = jnp.zeros((ts, C), F32)
        for k in range(CONV_A_WIDTH):
            sh = win[pl.ds(CONV_A_WIDTH - 1 - k, ts), :]
            da = da + w_ref[pl.ds(k, 1), :] * sh
            small_ref[pl.ds(k, 1), :] += jnp.sum(a * sh, axis=0, keepdims=True)
        dz_ref[:, pl.ds(0, C)] = (da * sgg).astype(BF16)
        dz_ref[:, pl.ds(C, C)] = (da * val * sgg * (1.0 - sgg)).astype(BF16)

    row = lambda col: pl.BlockSpec((ts, C), lambda i, col=col: (n - 1 - i, col))
    vec = pl.BlockSpec((1, C), lambda i: (0, 0))
    return pl.pallas_call(
        body, name=name, grid=(n,),
        in_specs=[row(0), row(1), row(0), row(0), pl.BlockSpec((32, C), lambda i: (0, 0)), vec, vec],
        out_specs=[pl.BlockSpec((ts, 2 * C), lambda i: (n - 1 - i, 0)), pl.BlockSpec((40, C), lambda i: (0, 0))],
        out_shape=[jax.ShapeDtypeStruct((S, 2 * C), BF16), jax.ShapeDtypeStruct((40, C), F32)],
        scratch_shapes=[pltpu.VMEM((ts + CONV_HALO, C), F32)],
        compiler_params=_cparams(("arbitrary",)),
    )(z, z, c, dcat, w, ln_g, ln_b)


SC_HALO = 8


def _sconv_fwd(z, w, *, name):
    S = z.shape[0]
    C = SC_CH
    ts = _div_tile(S, 256, 16)

    def body(gb_ref, gc_ref, v_ref, w_ref, y_ref, cc_ref, win):
        i = pl.program_id(0)

        @pl.when(i == 0)
        def _():
            win[pl.ds(0, SC_HALO), :] = jnp.zeros((SC_HALO, C), F32)

        @pl.when(i > 0)
        def _():
            win[pl.ds(0, SC_HALO), :] = win[pl.ds(ts, SC_HALO), :]

        win[pl.ds(SC_HALO, ts), :] = gc_ref[...].astype(F32) * v_ref[...].astype(F32)
        acc = jnp.zeros((ts, C), F32)
        for k in range(SC_WIDTH):
            acc = acc + w_ref[pl.ds(k, 1), :] * win[pl.ds(SC_HALO - (SC_WIDTH - 1) + k, ts), :]
        cc_ref[...] = acc.astype(BF16)
        y_ref[...] = (gb_ref[...].astype(F32) * acc).astype(BF16)

    row = lambda col: pl.BlockSpec((ts, C), lambda i, col=col: (i, col))
    return pl.pallas_call(
        body, name=name, grid=(S // ts,),
        in_specs=[row(0), row(1), row(2), pl.BlockSpec((8, C), lambda i: (0, 0))],
        out_specs=[row(0), row(0)],
        out_shape=[jax.ShapeDtypeStruct((S, C), BF16), jax.ShapeDtypeStruct((S, C), BF16)],
        scratch_shapes=[pltpu.VMEM((ts + SC_HALO, C), F32)],
        compiler_params=_cparams(("arbitrary",)),
    )(z, z, z, w)


def _sconv_bwd(z, cc, dy, w, *, name):
    S = z.shape[0]
    C = SC_CH
    ts = _div_tile(S, 256, 16)
    n = S // ts

    def body(gb_ref, gc_ref, v_ref, cc_ref, dy_ref, w_ref, dz_ref, dw_ref, win):
        i = pl.program_id(0)

        @pl.when(i == 0)
        def _():
            win[pl.ds(ts, SC_HALO), :] = jnp.zeros((SC_HALO, C), F32)
            dw_ref[...] = jnp.zeros(dw_ref.shape, F32)

        @pl.when(i > 0)
        def _():
            win[pl.ds(ts, SC_HALO), :] = win[pl.ds(0, SC_HALO), :]

        dyv = dy_ref[...].astype(F32)
        gb = gb_ref[...].astype(F32)
        gc = gc_ref[...].astype(F32)
        val = v_ref[...].astype(F32)
        dz_ref[:, pl.ds(0, C)] = (dyv * cc_ref[...].astype(F32)).astype(BF16)
        win[pl.ds(0, ts), :] = dyv * gb
        cv = gc * val
        dcv = jnp.zeros((ts, C), F32)
        for k in range(SC_WIDTH):
            sh = win[pl.ds(SC_WIDTH - 1 - k, ts), :]
            dcv = dcv + w_ref[pl.ds(k, 1), :] * sh
            dw_ref[pl.ds(k, 1), :] += jnp.sum(cv * sh, axis=0, keepdims=True)
        dz_ref[:, pl.ds(C, C)] = (dcv * val).astype(BF16)
        dz_ref[:, pl.ds(2 * C, C)] = (dcv * gc).astype(BF16)

    row = lambda col: pl.BlockSpec((ts, C), lambda i, col=col: (n - 1 - i, col))
    return pl.pallas_call(
        body, name=name, grid=(n,),
        in_specs=[row(0), row(1), row(2), row(0), row(0), pl.BlockSpec((8, C), lambda i: (0, 0))],
        out_specs=[pl.BlockSpec((ts, 3 * C), lambda i: (n - 1 - i, 0)), pl.BlockSpec((8, C), lambda i: (0, 0))],
        out_shape=[jax.ShapeDtypeStruct((S, 3 * C), BF16), jax.ShapeDtypeStruct((8, C), F32)],
        scratch_shapes=[pltpu.VMEM((ts + SC_HALO, C), F32)],
        compiler_params=_cparams(("arbitrary",)),
    )(z, z, z, cc, dy, w)


SWA_Q_COL = 2
SWA_SLOPES = [2.0 ** (-8.0 * (h + 1) / SWA_HEADS) for h in range(SWA_HEADS)]
SWA_SCALE = HEAD_DIM ** -0.5


def _swa_masks():
    ii = lax.broadcasted_iota(jnp.int32, (WINDOW, 2 * WINDOW), 0)
    jj = lax.broadcasted_iota(jnp.int32, (WINDOW, 2 * WINDOW), 1)
    dist = ii + WINDOW - jj
    valid = (dist >= 0) & (dist < WINDOW)
    return dist.astype(F32), valid, jj


def _swa_probs(qh, kk, sink, slope, distf, valid):
    s = lax.dot_general(qh, kk, (((1,), (1,)), ((), ())), preferred_element_type=F32) * SWA_SCALE
    s = s - slope * distf
    s = jnp.where(valid, s, NEG_BIG)
    m = jnp.maximum(jnp.max(s, axis=-1, keepdims=True), sink)
    p = jnp.exp(s - m)
    l = jnp.sum(p, axis=-1, keepdims=True) + jnp.exp(sink - m)
    return p, m, l


def _swa_fwd(z, kpad, vpad, sinks, *, name):
    S = z.shape[0]
    tq = _div_tile(S, 256, 128)
    nblk = tq // WINDOW
    W = WINDOW

    def body(sink_ref, q_ref, k_ref, v_ref, o_ref):
        i = pl.program_id(0)
        distf, valid0, jj = _swa_masks()
        for b in range(nblk):
            nb = i * nblk + b
            start = pl.multiple_of(nb * W, W)
            valid = valid0 & ((jj >= W) | (nb > 0))
            for kv in range(SWA_KV_HEADS):
                kk = k_ref[pl.ds(start, 2 * W), pl.ds(HEAD_DIM * kv, HEAD_DIM)]
                vv = v_ref[pl.ds(start, 2 * W), pl.ds(HEAD_DIM * kv, HEAD_DIM)]
                for g in range(SWA_GROUP):
                    h = kv * SWA_GROUP + g
                    qh = q_ref[pl.ds(W * b, W), pl.ds(HEAD_DIM * h, HEAD_DIM)]
                    p, m, l = _swa_probs(qh, kk, sink_ref[h], SWA_SLOPES[h], distf, valid)
                    o = jnp.dot(p.astype(BF16), vv, preferred_element_type=F32) / l
                    o_ref[pl.ds(W * b, W), pl.ds(HEAD_DIM * h, HEAD_DIM)] = o.astype(BF16)

    full = pl.BlockSpec((S + W, 2 * HEAD_DIM), lambda i: (0, 0))
    return pl.pallas_call(
        body, name=name, grid=(S // tq,),
        in_specs=[pl.BlockSpec(memory_space=pltpu.SMEM), pl.BlockSpec((tq, 512), lambda i: (i, SWA_Q_COL)), full, full],
        out_specs=pl.BlockSpec((tq, 512), lambda i: (i, 0)),
        out_shape=jax.ShapeDtypeStruct((S, 512), BF16),
        compiler_params=_cparams(("parallel",)),
    )(sinks, z, kpad, vpad)


def _swa_bwd(z, kpad, vpad, sinks, dcat, *, name):
    S = z.shape[0]
    tq = _div_tile(S, 256, 128)
    nblk = tq // WINDOW
    W = WINDOW

    def body(sink_ref, q_ref, k_ref, v_ref, do_ref, dq_ref, dk_ref, dv_ref, ds_ref):
        i = pl.program_id(0)

        @pl.when(i == 0)
        def _():
            dk_ref[...] = jnp.zeros(dk_ref.shape, F32)
            dv_ref[...] = jnp.zeros(dv_ref.shape, F32)
            ds_ref[...] = jnp.zeros(ds_ref.shape, F32)

        distf, valid0, jj = _swa_masks()
        for b in range(nblk):
            nb = i * nblk + b
            start = pl.multiple_of(nb * W, W)
            valid = valid0 & ((jj >= W) | (nb > 0))
            for kv in range(SWA_KV_HEADS):
                kk = k_ref[pl.ds(start, 2 * W), pl.ds(HEAD_DIM * kv, HEAD_DIM)]
                vv = v_ref[pl.ds(start, 2 * W), pl.ds(HEAD_DIM * kv, HEAD_DIM)]
                dkk = jnp.zeros((2 * W, HEAD_DIM), F32)
                dvv = jnp.zeros((2 * W, HEAD_DIM), F32)
                for g in range(SWA_GROUP):
                    h = kv * SWA_GROUP + g
                    qh = q_ref[pl.ds(W * b, W), pl.ds(HEAD_DIM * h, HEAD_DIM)]
                    doh = do_ref[pl.ds(W * b, W), pl.ds(HEAD_DIM * h, HEAD_DIM)]
                    sink = sink_ref[h]
                    p, m, l = _swa_probs(qh, kk, sink, SWA_SLOPES[h], distf, valid)
                    inv_l = 1.0 / l
                    pn = p * inv_l
                    dp = lax.dot_general(doh, vv, (((1,), (1,)), ((), ())), preferred_element_type=F32)
                    delta = jnp.sum(pn * dp, axis=-1, keepdims=True)
                    dsc = (pn * (dp - delta)).astype(BF16)
                    psink = jnp.exp(sink - m) * inv_l
                    ds_ref[pl.ds(h, 1), :] += jnp.broadcast_to(
                        -jnp.sum(psink * delta, axis=0, keepdims=True), (1, 128))
                    dq = jnp.dot(dsc, kk, preferred_element_type=F32) * SWA_SCALE
                    dq_ref[pl.ds(W * b, W), pl.ds(HEAD_DIM * h, HEAD_DIM)] = dq.astype(BF16)
                    dkk = dkk + lax.dot_general(dsc, qh, (((0,), (0,)), ((), ())),
                                                preferred_element_type=F32) * SWA_SCALE
                    dvv = dvv + lax.dot_general(pn.astype(BF16), doh, (((0,), (0,)), ((), ())),
                                                preferred_element_type=F32)
                dk_ref[pl.ds(start, 2 * W), pl.ds(HEAD_DIM * kv, HEAD_DIM)] += dkk
                dv_ref[pl.ds(start, 2 * W), pl.ds(HEAD_DIM * kv, HEAD_DIM)] += dvv

    full = pl.BlockSpec((S + W, 2 * HEAD_DIM), lambda i: (0, 0))
    return pl.pallas_call(
        body, name=name, grid=(S // tq,),
        in_specs=[pl.BlockSpec(memory_space=pltpu.SMEM), pl.BlockSpec((tq, 512), lambda i: (i, SWA_Q_COL)), full, full,
                  pl.BlockSpec((tq, 512), lambda i: (i, 1))],
        out_specs=[pl.BlockSpec((tq, 512), lambda i: (i, 0)), full, full, pl.BlockSpec((8, 128), lambda i: (0, 0))],
        out_shape=[jax.ShapeDtypeStruct((S, 512), BF16), jax.ShapeDtypeStruct((S + W, 2 * HEAD_DIM), F32),
                   jax.ShapeDtypeStruct((S + W, 2 * HEAD_DIM), F32), jax.ShapeDtypeStruct((8, 128), F32)],
        compiler_params=_cparams(("arbitrary",)),
    )(sinks, z, kpad, vpad, dcat)


XA_SCALE = XA_HEAD_DIM ** -0.5


def _xa_probs(qh, kh):
    s = lax.dot_general(qh, kh, (((1,), (1,)), ((), ())), preferred_element_type=F32) * XA_SCALE
    m = jnp.max(s, axis=-1, keepdims=True)
    p = jnp.exp(s - m)
    return p, jnp.sum(p, axis=-1, keepdims=True)


def _xa_fwd(q, kv, *, name):
    S, D = q.shape
    M = kv.shape[0]
    ts = _div_tile(S, 512, 16)
    HD = XA_HEAD_DIM

    def body(q_ref, k_ref, v_ref, o_ref):
        for h in range(XA_HEADS):
            qh = q_ref[:, pl.ds(HD * h, HD)]
            p, l = _xa_probs(qh, k_ref[:, pl.ds(HD * h, HD)])
            o = jnp.dot(p.astype(BF16), v_ref[:, pl.ds(HD * h, HD)], preferred_element_type=F32) / l
            o_ref[:, pl.ds(HD * h, HD)] = o.astype(BF16)

    return pl.pallas_call(
        body, name=name, grid=(S // ts,),
        in_specs=[pl.BlockSpec((ts, D), lambda i: (i, 0)), pl.BlockSpec((M, D), lambda i: (0, 0)),
                  pl.BlockSpec((M, D), lambda i: (0, 1))],
        out_specs=pl.BlockSpec((ts, D), lambda i: (i, 0)),
        out_shape=jax.ShapeDtypeStruct((S, D), BF16),
        compiler_params=_cparams(("parallel",)),
    )(q, kv, kv)


def _xa_bwd(q, kv, do, *, name):
    S, D = q.shape
    M = kv.shape[0]
    ts = _div_tile(S, 512, 16)
    HD = XA_HEAD_DIM

    def body(q_ref, k_ref, v_ref, do_ref, dq_ref, dkv_ref):
        i = pl.program_id(0)

        @pl.when(i == 0)
        def _():
            dkv_ref[...] = jnp.zeros(dkv_ref.shape, F32)

        for h in range(XA_HEADS):
            qh = q_ref[:, pl.ds(HD * h, HD)]
            kh = k_ref[:, pl.ds(HD * h, HD)]
            vh = v_ref[:, pl.ds(HD * h, HD)]
            doh = do_ref[:, pl.ds(HD * h, HD)]
            p, l = _xa_probs(qh, kh)
            pn = p * (1.0 / l)
            dp = lax.dot_general(doh, vh, (((1,), (1,)), ((), ())), preferred_element_type=F32)
            delta = jnp.sum(pn * dp, axis=-1, keepdims=True)
            dsc = (pn * (dp - delta)).astype(BF16)
            dq_ref[:, pl.ds(HD * h, HD)] = (jnp.dot(dsc, kh, preferred_element_type=F32) * XA_SCALE).astype(BF16)
            dkv_ref[:, pl.ds(HD * h, HD)] += lax.dot_general(
                dsc, qh, (((0,), (0,)), ((), ())), preferred_element_type=F32) * XA_SCALE
            dkv_ref[:, pl.ds(D + HD * h, HD)] += lax.dot_general(
                pn.astype(BF16), doh, (((0,), (0,)), ((), ())), preferred_element_type=F32)

    row = pl.BlockSpec((ts, D), lambda i: (i, 0))
    return pl.pallas_call(
        body, name=name, grid=(S // ts,),
        in_specs=[row, pl.BlockSpec((M, D), lambda i: (0, 0)), pl.BlockSpec((M, D), lambda i: (0, 1)), row],
        out_specs=[row, pl.BlockSpec((M, 2 * D), lambda i: (0, 0))],
        out_shape=[jax.ShapeDtypeStruct((S, D), BF16), jax.ShapeDtypeStruct((M, 2 * D), F32)],
        compiler_params=_cparams(("arbitrary",)),
    )(q, kv, kv, do)


def _adam_math(w, g, m, v):
    m = ADAM_B1 * m + (1.0 - ADAM_B1) * g
    v = ADAM_B2 * v + (1.0 - ADAM_B2) * (g * g)
    m_hat = m / (1.0 - ADAM_B1 ** ADAM_STEP)
    v_hat = v / (1.0 - ADAM_B2 ** ADAM_STEP)
    delta = -ADAM_LR * (m_hat / (jnp.sqrt(v_hat) + ADAM_EPS) + ADAM_WD * w)
    return delta, m, v


def _adamw(w, g, m, v, *, name):
    R, C = w.shape
    tr = _div_tile(R, max(8, (256 * 1024) // C // 8 * 8))

    def body(w_ref, g_ref, m_ref, v_ref, d_ref, nm_ref, nv_ref):
        d, nm, nv = _adam_math(w_ref[...], g_ref[...], m_ref[...], v_ref[...])
        d_ref[...] = d
        nm_ref[...] = nm
        nv_ref[...] = nv

    spec = pl.BlockSpec((tr, C), lambda i: (i, 0))
    sds = jax.ShapeDtypeStruct((R, C), F32)
    return pl.pallas_call(
        body, name=name, grid=(R // tr,), in_specs=[spec] * 4, out_specs=[spec] * 3, out_shape=[sds] * 3,
        compiler_params=_cparams(("parallel",)),
    )(w, g, m, v)


def _adamw_small(w, gparts, m, v, *, name):
    R, C = w.shape

    def body(w_ref, gp_ref, m_ref, v_ref, g_ref, d_ref, nm_ref, nv_ref):
        g = gp_ref[0]
        for k in range(1, N_DEV):
            g = g + gp_ref[k]
        d, nm, nv = _adam_math(w_ref[...], g, m_ref[...], v_ref[...])
        g_ref[...] = g
        d_ref[...] = d
        nm_ref[...] = nm
        nv_ref[...] = nv

    sds = jax.ShapeDtypeStruct((R, C), F32)
    return pl.pallas_call(body, name=name, out_shape=[sds] * 4, compiler_params=_cparams())(w, gparts, m, v)


ANY = pl.BlockSpec(memory_space=pl.ANY)


def _mesh_pos():
    return lax.axis_index("x"), lax.axis_index("y"), lax.axis_index("c")


def _other_chips(x, y):
    return [(1 - x, y), (x, 1 - y), (1 - x, 1 - y)]


def _all_gather_weights(wl):
    R, C = wl.shape
    rh = R // 2

    def body(wl_ref, out_ref, send_sems, recv_sems, local_sem):
        x, y, c = _mesh_pos()
        sibling = (x, y, 1 - c)
        chips = _other_chips(x, y)
        mine_rows = pl.ds(pl.multiple_of(c * rh, 32), rh)
        other_rows = pl.ds(pl.multiple_of((1 - c) * rh, 32), rh)

        def slot(cx, cy):
            return 2 * cx + cy

        mine = pltpu.make_async_copy(wl_ref, out_ref.at[slot(x, y)], local_sem)
        mine.start()

        def ici_copy(j, src_slot_ref):
            cx, cy = chips[j]
            return pltpu.make_async_remote_copy(
                src_ref=src_slot_ref, dst_ref=out_ref.at[slot(x, y), mine_rows],
                send_sem=send_sems.at[j], recv_sem=recv_sems.at[j], device_id=(cx, cy, c), device_id_type=MESH)

        first = [ici_copy(j, wl_ref.at[mine_rows]) for j in range(3)]
        for cp in first:
            cp.start()

        passed = []
        for j in range(3):
            cx, cy = chips[j]
            landed = out_ref.at[slot(cx, cy), mine_rows]
            pltpu.make_async_remote_copy(
                src_ref=landed, dst_ref=landed, send_sem=send_sems.at[j], recv_sem=recv_sems.at[j],
                device_id=(cx, cy, c), device_id_type=MESH).wait_recv()
            fwd = pltpu.make_async_remote_copy(
                src_ref=landed, dst_ref=landed, send_sem=send_sems.at[3 + j], recv_sem=recv_sems.at[3 + j],
                device_id=sibling, device_id_type=MESH)
            fwd.start()
            passed.append(fwd)
        for j in range(3):
            cx, cy = chips[j]
            theirs = out_ref.at[slot(cx, cy), other_rows]
            pltpu.make_async_remote_copy(
                src_ref=theirs, dst_ref=theirs, send_sem=send_sems.at[3 + j], recv_sem=recv_sems.at[3 + j],
                device_id=sibling, device_id_type=MESH).wait_recv()
        for cp in first + passed:
            cp.wait_send()
        mine.wait()

    return pl.pallas_call(
        body, name="ag_weights", in_specs=[ANY], out_specs=ANY,
        out_shape=jax.ShapeDtypeStruct((N_CHIPS, R, C), wl.dtype),
        scratch_shapes=[pltpu.SemaphoreType.DMA((6,)), pltpu.SemaphoreType.DMA((6,)), pltpu.SemaphoreType.DMA],
    )(wl)


def _pair_exchange(gp):
    n, R, C = gp.shape
    rh = R // 2

    def body(gp_ref, out_ref, send_sem, recv_sem):
        x, y, c = _mesh_pos()
        other_rows = pl.ds(pl.multiple_of((1 - c) * rh, 32), rh)
        cp = pltpu.make_async_remote_copy(
            src_ref=gp_ref.at[:, other_rows], dst_ref=out_ref, send_sem=send_sem, recv_sem=recv_sem,
            device_id=(x, y, 1 - c), device_id_type=MESH)
        cp.start()
        cp.wait()

    return pl.pallas_call(
        body, name="rs_pair_exchange", in_specs=[ANY], out_specs=ANY,
        out_shape=jax.ShapeDtypeStruct((n, rh, C), gp.dtype),
        scratch_shapes=[pltpu.SemaphoreType.DMA, pltpu.SemaphoreType.DMA],
    )(gp)


def _pair_sum(gp, got, c_idx):
    n, R, C = gp.shape
    rh = R // 2
    tr = _div_tile(rh, 1024, 32)
    nt = rh // tr

    def body(c_ref, a_ref, b_ref, o_ref):
        o_ref[...] = (a_ref[...].astype(F32) + b_ref[...].astype(F32)).astype(BF16)

    grid_spec = pltpu.PrefetchScalarGridSpec(
        num_scalar_prefetch=1, grid=(n, nt),
        in_specs=[pl.BlockSpec((1, tr, C), lambda s, i, c_ref: (s, c_ref[0] * nt + i, 0)),
                  pl.BlockSpec((1, tr, C), lambda s, i, c_ref: (s, i, 0))],
        out_specs=pl.BlockSpec((1, tr, C), lambda s, i, c_ref: (s, i, 0)))
    return pl.pallas_call(
        body, name="rs_pair_sum", grid_spec=grid_spec, out_shape=jax.ShapeDtypeStruct((n, rh, C), BF16),
        compiler_params=_cparams(("parallel", "parallel")),
    )(c_idx, gp, got)


def _chip_exchange(ps):
    n, rh, C = ps.shape

    def body(ps_ref, out_ref, send_sems, recv_sems):
        x, y, c = _mesh_pos()
        chips = _other_chips(x, y)
        cps = []
        for j in range(3):
            cx, cy = chips[j]
            cp = pltpu.make_async_remote_copy(
                src_ref=ps_ref.at[2 * cx + cy], dst_ref=out_ref.at[j], send_sem=send_sems.at[j],
                recv_sem=recv_sems.at[j], device_id=(cx, cy, c), device_id_type=MESH)
            cp.start()
            cps.append(cp)
        for cp in cps:
            cp.wait()

    return pl.pallas_call(
        body, name="rs_chip_exchange", in_specs=[ANY], out_specs=ANY,
        out_shape=jax.ShapeDtypeStruct((3, rh, C), ps.dtype),
        scratch_shapes=[pltpu.SemaphoreType.DMA((3,)), pltpu.SemaphoreType.DMA((3,))],
    )(ps)


def _chip_sum(ps, got, chip_idx):
    n, rh, C = ps.shape
    tr = _div_tile(rh, 1024, 32)

    def body(s_ref, a_ref, b_ref, o_ref):
        acc = a_ref[0].astype(F32)
        for j in range(3):
            acc = acc + b_ref[j].astype(F32)
        o_ref[...] = acc

    grid_spec = pltpu.PrefetchScalarGridSpec(
        num_scalar_prefetch=1, grid=(rh // tr,),
        in_specs=[pl.BlockSpec((1, tr, C), lambda i, s_ref: (s_ref[0], i, 0)),
                  pl.BlockSpec((3, tr, C), lambda i, s_ref: (0, i, 0))],
        out_specs=pl.BlockSpec((tr, C), lambda i, s_ref: (i, 0)))
    return pl.pallas_call(
        body, name="rs_chip_sum", grid_spec=grid_spec, out_shape=jax.ShapeDtypeStruct((rh, C), F32),
        compiler_params=_cparams(("parallel",)),
    )(chip_idx, ps, got)


def _final_exchange(red, small):
    rh, C = red.shape
    rs, cs = small.shape

    def body(red_ref, small_ref, out_ref, sm_ref, send_sems, recv_sems, local_sems):
        x, y, c = _mesh_pos()
        me = 4 * x + 2 * y + c
        mine_rows = pl.ds(pl.multiple_of(c * rh, 32), rh)
        own = pltpu.make_async_copy(red_ref, out_ref.at[mine_rows], local_sems.at[0])
        own.start()
        own_s = pltpu.make_async_copy(small_ref, sm_ref.at[me], local_sems.at[1])
        own_s.start()
        swap = pltpu.make_async_remote_copy(
            src_ref=red_ref, dst_ref=out_ref.at[mine_rows], send_sem=send_sems.at[0], recv_sem=recv_sems.at[0],
            device_id=(x, y, 1 - c), device_id_type=MESH)
        swap.start()
        cps = [swap]
        for k in range(1, N_DEV):
            fx, fy, fc = (k >> 2) & 1, (k >> 1) & 1, k & 1
            px = 1 - x if fx else x
            py = 1 - y if fy else y
            pc = 1 - c if fc else c
            cp = pltpu.make_async_remote_copy(
                src_ref=small_ref, dst_ref=sm_ref.at[me], send_sem=send_sems.at[k], recv_sem=recv_sems.at[k],
                device_id=(px, py, pc), device_id_type=MESH)
            cp.start()
            cps.append(cp)
        for cp in cps:
            cp.wait()
        own.wait()
        own_s.wait()

    return pl.pallas_call(
        body, name="rs_final_exchange", in_specs=[ANY, ANY], out_specs=[ANY, ANY],
        out_shape=[jax.ShapeDtypeStruct((2 * rh, C), F32), jax.ShapeDtypeStruct((N_DEV, rs, cs), F32)],
        scratch_shapes=[pltpu.SemaphoreType.DMA((N_DEV,)), pltpu.SemaphoreType.DMA((N_DEV,)),
                        pltpu.SemaphoreType.DMA((2,))],
    )(red, small)


def _pad_rows(flat, lead):
    n = flat.shape[-1]
    unit = PACK_COLS * PACK_ROW_ALIGN
    total = -(-n // unit) * unit
    pad = [(0, 0)] * (flat.ndim - 1) + [(0, total - n)]
    return jnp.pad(flat, pad).reshape(lead + (total // PACK_COLS, PACK_COLS))


def _pack_local(shards, dtype):
    flat = jnp.concatenate([shards[n].astype(dtype).reshape(-1) for n in BIG_NAMES])
    return _pad_rows(flat, ())


def _unpack_local(packed, shapes):
    flat = packed.reshape(-1)
    out, off = {}, 0
    for n in BIG_NAMES:
        sz = math.prod(shapes[n])
        out[n] = flat[off:off + sz].reshape(shapes[n])
        off += sz
    return out


def _unpack_gathered(gathered, shapes):
    flat = gathered.reshape(N_CHIPS, -1)
    out, off = {}, 0
    for n, kind in BIG:
        L, A, B = shapes[n]
        sz = L * A * B
        seg = flat[:, off:off + sz].reshape(N_CHIPS, L, A, B)
        off += sz
        if kind == 'col':
            out[n] = seg.transpose(1, 2, 0, 3).reshape(L, A, N_CHIPS * B)
        else:
            out[n] = seg.transpose(1, 0, 2, 3).reshape(L, N_CHIPS * A, B)
    return out


def _pack_grads(grads, shapes):
    parts = []
    for n, kind in BIG:
        L, A, B = shapes[n]
        g = grads[n].astype(BF16)
        if kind == 'col':
            seg = g.reshape(L, A, N_CHIPS, B).transpose(2, 0, 1, 3)
        else:
            seg = g.reshape(L, N_CHIPS, A, B).transpose(1, 0, 2, 3)
        parts.append(seg.reshape(N_CHIPS, -1))
    return _pad_rows(jnp.concatenate(parts, axis=1), (N_CHIPS,))


def _pack_small(d):
    flat = jnp.concatenate([d[n].astype(F32).reshape(-1) for n in SMALL_NAMES])
    n = flat.shape[0]
    total = -(-n // 1024) * 1024
    return jnp.pad(flat, (0, total - n)).reshape(total // 128, 128)


def _unpack_small(packed, shapes):
    flat = packed.reshape(-1)
    out, off = {}, 0
    for n in SMALL_NAMES:
        sz = math.prod(shapes[n])
        out[n] = flat[off:off + sz].reshape(shapes[n])
        off += sz
    return out


def _ffn_fwd(h, g, w_gu, w_down, tag):
    u = _rms_fwd(h, g, name=f"{tag}_norm")
    gu = _mm(u, w_gu, name=f"{tag}_gu", tm=1024, tn=1408, tk=1024)
    a = _swiglu_fwd(gu, name=f"{tag}_act")
    h2 = _mm(a, w_down, name=f"{tag}_down", out_dtype=F32, tm=512, tn=1024, tk=2816, res=h, scale=0.5)
    return h2, (h, u, gu, a)


def _ffn_bwd(dh, saved, g, w_gu, w_down, tag):
    h, u, gu, a = saved
    da = _mm(dh, w_down, name=f"{tag}_b_da", tb=True, tm=512, tn=1408, tk=1024, scale=0.5)
    d_w_down = _mm(a, dh, name=f"{tag}_b_wdown", ta=True, out_dtype=F32, tm=1408, tn=1024, tk=1024, scale=0.5)
    dgu = _swiglu_bwd(gu, da, name=f"{tag}_b_act")
    d_w_gu = _mm(u, dgu, name=f"{tag}_b_wgu", ta=True, out_dtype=F32, tm=1024, tn=1408, tk=1024)
    du = _mm(dgu, w_gu, name=f"{tag}_b_du", tb=True, out_dtype=F32, tm=512, tn=1024, tk=1408)
    dh_in, dg = _rms_bwd(h, g, du, dh, name=f"{tag}_b_norm")
    return dh_in, dg, d_w_gu, d_w_down


def _xa_block_fwd(h, mem, g, gm, wq, wkv, wo, tag):
    u = _rms_fwd(h, g, name=f"{tag}_norm")
    mn = _rms_fwd(mem, gm, name=f"{tag}_mem_norm")
    q = _mm(u, wq, name=f"{tag}_q", tm=1024, tn=1024, tk=1024)
    kv = _mm(mn, wkv, name=f"{tag}_kv", tm=256, tn=1024, tk=1024)
    o = _xa_fwd(q, kv, name=f"{tag}_attn")
    h2 = _mm(o, wo, name=f"{tag}_o", out_dtype=F32, tm=1024, tn=1024, tk=1024, res=h)
    return h2, (h, u, mn, q, kv, o)


def _xa_block_bwd(dh, saved, mem, g, gm, wq, wkv, wo, tag):
    h, u, mn, q, kv, o = saved
    do = _mm(dh, wo, name=f"{tag}_b_do", tb=True, tm=1024, tn=1024, tk=1024)
    d_wo = _mm(o, dh, name=f"{tag}_b_wo", ta=True, out_dtype=F32, tm=1024, tn=1024, tk=1024)
    dq, dkv = _xa_bwd(q, kv, do, name=f"{tag}_b_attn")
    d_wq = _mm(u, dq, name=f"{tag}_b_wq", ta=True, out_dtype=F32, tm=1024, tn=1024, tk=1024)
    du = _mm(dq, wq, name=f"{tag}_b_du", tb=True, out_dtype=F32, tm=1024, tn=1024, tk=1024)
    d_wkv = _mm(mn, dkv, name=f"{tag}_b_wkv", ta=True, out_dtype=F32, tm=1024, tn=1024, tk=256)
    dmn = _mm(dkv, wkv, name=f"{tag}_b_dmn", tb=True, out_dtype=F32, tm=256, tn=1024, tk=1024)
    _, dgm = _rms_bwd(mem, gm, dmn, None, name=f"{tag}_b_mem_norm")
    dh_in, dg = _rms_bwd(h, g, du, dh, name=f"{tag}_b_norm")
    return dh_in, dg, dgm, d_wq, d_wkv, d_wo


def _pad_conv_w(w, rows):
    return jnp.pad(w.astype(F32), ((0, rows - w.shape[0]), (0, 0)))


def _even_fwd(h, g, w_in, conv_w, conv_b, ln_g, ln_b, sinks, w_out, tag):
    u = _rms_fwd(h, g, name=f"{tag}_norm")
    z = _mm(u, w_in, name=f"{tag}_in", tm=1024, tn=1792, tk=1024)
    c, act = _conv_a_fwd(z, conv_w, conv_b, ln_g, ln_b, name=f"{tag}_conv")
    kpad = jnp.pad(z[:, 1536:1664], ((WINDOW, 0), (0, 0)))
    vpad = jnp.pad(z[:, 1664:1792], ((WINDOW, 0), (0, 0)))
    o = _swa_fwd(z, kpad, vpad, sinks, name=f"{tag}_swa")
    cat = jnp.concatenate([act, o], axis=-1)
    h2 = _mm(cat, w_out, name=f"{tag}_out", out_dtype=F32, tm=1024, tn=1024, tk=1024, res=h)
    return h2, (h, u, z, c, kpad, vpad, cat)


def _even_bwd(dh, saved, g, w_in, conv_w, ln_g, ln_b, sinks, w_out, tag):
    h, u, z, c, kpad, vpad, cat = saved
    dcat = _mm(dh, w_out, name=f"{tag}_b_dcat", tb=True, tm=1024, tn=1024, tk=1024)
    d_w_out = _mm(cat, dh, name=f"{tag}_b_wout", ta=True, out_dtype=F32, tm=1024, tn=1024, tk=1024)
    dz_a, small = _conv_a_bwd(z, c, dcat, conv_w, ln_g, ln_b, name=f"{tag}_b_conv")
    dq, dkp, dvp, dsinks = _swa_bwd(z, kpad, vpad, sinks, dcat, name=f"{tag}_b_swa")
    dz = jnp.concatenate([dz_a, dq, dkp[WINDOW:].astype(BF16), dvp[WINDOW:].astype(BF16)], axis=-1)
    d_w_in = _mm(u, dz, name=f"{tag}_b_win", ta=True, out_dtype=F32, tm=1024, tn=1792, tk=1024)
    du = _mm(dz, w_in, name=f"{tag}_b_du", tb=True, out_dtype=F32, tm=1024, tn=1024, tk=1792)
    dh_in, dg = _rms_bwd(h, g, du, dh, name=f"{tag}_b_norm")
    grads = dict(mix=dg, even_w_in=d_w_in, even_w_out=d_w_out, conv_a_w=small[:CONV_A_WIDTH],
                 conv_a_b=small[32:33], conv_a_ln_g=small[33:34], conv_a_ln_b=small[34:35], swa_sinks=dsinks[:, 0])
    return dh_in, grads


def _odd_fwd(h, g, w_in, conv_w, w_out, tag):
    u = _rms_fwd(h, g, name=f"{tag}_norm")
    z = _mm(u, w_in, name=f"{tag}_in", tm=1024, tn=1024, tk=1024)
    y, cc = _sconv_fwd(z, conv_w, name=f"{tag}_conv")
    h2 = _mm(y, w_out, name=f"{tag}_out", out_dtype=F32, tm=1024, tn=1024, tk=1024, res=h)
    return h2, (h, u, z, y, cc)


def _odd_bwd(dh, saved, g, w_in, conv_w, w_out, tag):
    h, u, z, y, cc = saved
    dy = _mm(dh, w_out, name=f"{tag}_b_dy", tb=True, tm=1024, tn=1024, tk=1024)
    d_w_out = _mm(y, dh, name=f"{tag}_b_wout", ta=True, out_dtype=F32, tm=1024, tn=1024, tk=1024)
    dz, dw = _sconv_bwd(z, cc, dy, conv_w, name=f"{tag}_b_conv")
    d_w_in = _mm(u, dz, name=f"{tag}_b_win", ta=True, out_dtype=F32, tm=1024, tn=1024, tk=1024)
    du = _mm(dz, w_in, name=f"{tag}_b_du", tb=True, out_dtype=F32, tm=1024, tn=1024, tk=1024)
    dh_in, dg = _rms_bwd(h, g, du, dh, name=f"{tag}_b_norm")
    return dh_in, dict(mix=dg, odd_w_in=d_w_in, odd_w_out=d_w_out, sc_conv_w=dw[:SC_WIDTH])


def _local_step(x, mem, tgt, W, P):
    row = lambda v: v.reshape(1, -1)
    conv_a_w = _pad_conv_w(W['conv_a_w'][0], 32)
    sc_w = _pad_conv_w(W['sc_conv_w'][0], 8)
    sinks = P['swa_sinks'][0]

    h = x
    saved = []
    for i in range(2):
        t = f"l{i}"
        h, s1 = _ffn_fwd(h, row(P['ffn1_norm'][i]), W['ffn1_w_gu'][i], W['ffn1_w_down'][i], f"{t}_ffn1")
        if i == 0:
            h, s2 = _even_fwd(h, row(P['mix_norm'][i]), W['even_w_in'][0], conv_a_w, P['conv_a_b'], P['conv_a_ln_g'],
                              P['conv_a_ln_b'], sinks, W['even_w_out'][0], f"{t}_even")
        else:
            h, s2 = _odd_fwd(h, row(P['mix_norm'][i]), W['odd_w_in'][0], sc_w, W['odd_w_out'][0], f"{t}_odd")
        h, s3 = _xa_block_fwd(h, mem, row(P['xa_norm'][i]), row(P['xa_mem_norm'][i]), W['xa_wq'][i], W['xa_wkv'][i],
                              W['xa_wo'][i], f"{t}_xa")
        h, s4 = _ffn_fwd(h, row(P['ffn2_norm'][i]), W['ffn2_w_gu'][i], W['ffn2_w_down'][i], f"{t}_ffn2")
        saved.append((s1, s2, s3, s4))

    loss, dh, d_final = _final_loss(h, row(P['final_norm']), tgt, name="final_loss")

    gw = {n: [None, None] for n in ('ffn1_w_gu', 'ffn1_w_down', 'ffn2_w_gu', 'ffn2_w_down', 'xa_wq', 'xa_wkv', 'xa_wo')}
    gp = {n: [None, None] for n in ('ffn1_norm', 'mix_norm', 'xa_norm', 'xa_mem_norm', 'ffn2_norm')}
    single = {}
    for i in (1, 0):
        t = f"l{i}"
        s1, s2, s3, s4 = saved[i]
        dh, gp['ffn2_norm'][i], gw['ffn2_w_gu'][i], gw['ffn2_w_down'][i] = _ffn_bwd(
            dh, s4, row(P['ffn2_norm'][i]), W['ffn2_w_gu'][i], W['ffn2_w_down'][i], f"{t}_ffn2")
        dh, gp['xa_norm'][i], gp['xa_mem_norm'][i], gw['xa_wq'][i], gw['xa_wkv'][i], gw['xa_wo'][i] = _xa_block_bwd(
            dh, s3, mem, row(P['xa_norm'][i]), row(P['xa_mem_norm'][i]), W['xa_wq'][i], W['xa_wkv'][i], W['xa_wo'][i],
            f"{t}_xa")
        if i == 0:
            dh, g2 = _even_bwd(dh, s2, row(P['mix_norm'][i]), W['even_w_in'][0], conv_a_w, P['conv_a_ln_g'],
                               P['conv_a_ln_b'], sinks, W['even_w_out'][0], f"{t}_even")
        else:
            dh, g2 = _odd_bwd(dh, s2, row(P['mix_norm'][i]), W['odd_w_in'][0], sc_w, W['odd_w_out'][0], f"{t}_odd")
        gp['mix_norm'][i] = g2.pop('mix')
        single.update(g2)
        dh, gp['ffn1_norm'][i], gw['ffn1_w_gu'][i], gw['ffn1_w_down'][i] = _ffn_bwd(
            dh, s1, row(P['ffn1_norm'][i]), W['ffn1_w_gu'][i], W['ffn1_w_down'][i], f"{t}_ffn1")

    big = {n: jnp.stack(v) for n, v in gw.items()}
    for n in ('even_w_in', 'even_w_out', 'odd_w_in', 'odd_w_out', 'conv_a_w', 'sc_conv_w'):
        big[n] = single[n][None]
    small = {n: jnp.concatenate(v, axis=0) for n, v in gp.items()}
    small['conv_a_b'] = single['conv_a_b']
    small['conv_a_ln_g'] = single['conv_a_ln_g']
    small['conv_a_ln_b'] = single['conv_a_ln_b']
    small['swa_sinks'] = single['swa_sinks'][None]
    small['final_norm'] = d_final[0]
    return loss[0, 0], dh, big, small


def _as2d(a):
    return a.reshape(-1, a.shape[-1])


def kernel(x, mem, ffn1_norm, ffn1_w_gu, ffn1_w_down, mix_norm, even_w_in, conv_a_w, conv_a_b, conv_a_ln_g, conv_a_ln_b, swa_sinks, even_w_out, odd_w_in, sc_conv_w, odd_w_out, xa_norm, xa_mem_norm, xa_wq, xa_wkv, xa_wo, ffn2_norm, ffn2_w_gu, ffn2_w_down, final_norm, loss_target, m_ffn1_norm, m_ffn1_w_gu, m_ffn1_w_down, m_mix_norm, m_even_w_in, m_conv_a_w, m_conv_a_b, m_conv_a_ln_g, m_conv_a_ln_b, m_swa_sinks, m_even_w_out, m_odd_w_in, m_sc_conv_w, m_odd_w_out, m_xa_norm, m_xa_mem_norm, m_xa_wq, m_xa_wkv, m_xa_wo, m_ffn2_norm, m_ffn2_w_gu, m_ffn2_w_down, m_final_norm, v_ffn1_norm, v_ffn1_w_gu, v_ffn1_w_down, v_mix_norm, v_even_w_in, v_conv_a_w, v_conv_a_b, v_conv_a_ln_g, v_conv_a_ln_b, v_swa_sinks, v_even_w_out, v_odd_w_in, v_sc_conv_w, v_odd_w_out, v_xa_norm, v_xa_mem_norm, v_xa_wq, v_xa_wkv, v_xa_wo, v_ffn2_norm, v_ffn2_w_gu, v_ffn2_w_down, v_final_norm):
    w = dict(zip(WEIGHT_NAMES, (ffn1_norm, ffn1_w_gu, ffn1_w_down, mix_norm, even_w_in, conv_a_w, conv_a_b, conv_a_ln_g, conv_a_ln_b, swa_sinks, even_w_out, odd_w_in, sc_conv_w, odd_w_out, xa_norm, xa_mem_norm, xa_wq, xa_wkv, xa_wo, ffn2_norm, ffn2_w_gu, ffn2_w_down, final_norm)))
    m = dict(zip(WEIGHT_NAMES, (m_ffn1_norm, m_ffn1_w_gu, m_ffn1_w_down, m_mix_norm, m_even_w_in, m_conv_a_w, m_conv_a_b, m_conv_a_ln_g, m_conv_a_ln_b, m_swa_sinks, m_even_w_out, m_odd_w_in, m_sc_conv_w, m_odd_w_out, m_xa_norm, m_xa_mem_norm, m_xa_wq, m_xa_wkv, m_xa_wo, m_ffn2_norm, m_ffn2_w_gu, m_ffn2_w_down, m_final_norm)))
    v = dict(zip(WEIGHT_NAMES, (v_ffn1_norm, v_ffn1_w_gu, v_ffn1_w_down, v_mix_norm, v_even_w_in, v_conv_a_w, v_conv_a_b, v_conv_a_ln_g, v_conv_a_ln_b, v_swa_sinks, v_even_w_out, v_odd_w_in, v_sc_conv_w, v_odd_w_out, v_xa_norm, v_xa_mem_norm, v_xa_wq, v_xa_wkv, v_xa_wo, v_ffn2_norm, v_ffn2_w_gu, v_ffn2_w_down, v_final_norm)))
    shard_shapes = {n: w[n].shape for n in BIG_NAMES}
    small_shapes = {n: w[n].shape for n in SMALL_NAMES}
    cx, cy, cc = lax.axis_index("x"), lax.axis_index("y"), lax.axis_index("c")

    gathered = _all_gather_weights(_pack_local({n: w[n] for n in BIG_NAMES}, BF16))
    W = _unpack_gathered(gathered, shard_shapes)

    loss_part, grad_x, g_big, g_small = _local_step(x[0], mem[0], loss_target[0], W, {n: w[n] for n in SMALL_NAMES})
    loss = lax.psum(loss_part, ("x", "y", "c"))

    gp = _pack_grads(g_big, shard_shapes)
    got = _pair_exchange(gp)
    ps = _pair_sum(gp, got, cc.astype(jnp.int32).reshape(1))
    got2 = _chip_exchange(ps)
    red = _chip_sum(ps, got2, (2 * cx + cy).astype(jnp.int32).reshape(1))
    g_shard, small_parts = _final_exchange(red, _pack_small(g_small))
    g_local = _unpack_local(g_shard, shard_shapes)

    grads, deltas, new_m, new_v = {}, {}, {}, {}
    for n in BIG_NAMES:
        shp = shard_shapes[n]
        d, nm, nv = _adamw(_as2d(w[n]), _as2d(g_local[n]), _as2d(m[n]), _as2d(v[n]), name=f"adamw_{n}")
        grads[n], deltas[n], new_m[n], new_v[n] = g_local[n], d.reshape(shp), nm.reshape(shp), nv.reshape(shp)
    gs, ds, ms, vs = _adamw_small(_pack_small({n: w[n] for n in SMALL_NAMES}), small_parts,
                                  _pack_small({n: m[n] for n in SMALL_NAMES}),
                                  _pack_small({n: v[n] for n in SMALL_NAMES}), name="adamw_small")
    for dst, packed in ((grads, gs), (deltas, ds), (new_m, ms), (new_v, vs)):
        dst.update(_unpack_small(packed, small_shapes))

    return (loss, grad_x[None], *[grads[n] for n in WEIGHT_NAMES], *[deltas[n] for n in WEIGHT_NAMES],
            *[new_m[n] for n in WEIGHT_NAMES], *[new_v[n] for n in WEIGHT_NAMES])
```

```python
import functools
import math

import jax
import jax.numpy as jnp
from jax import lax
from jax.experimental import pallas as pl
from jax.experimental.pallas import tpu as pltpu

F32 = jnp.float32
BF16 = jnp.bfloat16

D_MODEL = 1024
D_FF = 2816
CONV_A_CH = 512
CONV_A_WIDTH = 31
SWA_HEADS = 8
SWA_KV_HEADS = 2
SWA_GROUP = 4
HEAD_DIM = 64
WINDOW = 128
SC_CH = 1024
SC_WIDTH = 3
XA_HEADS = 4
XA_HEAD_DIM = 256
RMS_EPS = 1e-6
LN_EPS = 1e-5

ADAM_LR = 0.001
ADAM_B1 = 0.9
ADAM_B2 = 0.999
ADAM_EPS = 1e-08
ADAM_WD = 0.01
ADAM_STEP = 10

N_CHIPS = 4
N_DEV = 8
NEG_BIG = -1e30
VMEM_LIMIT = 56 * 1024 * 1024
MESH = pl.DeviceIdType.MESH

INPUT_NAMES = ['x', 'mem', 'ffn1_norm', 'ffn1_w_gu', 'ffn1_w_down', 'mix_norm', 'even_w_in', 'conv_a_w', 'conv_a_b',
               'conv_a_ln_g', 'conv_a_ln_b', 'swa_sinks', 'even_w_out', 'odd_w_in', 'sc_conv_w', 'odd_w_out', 'xa_norm',
               'xa_mem_norm', 'xa_wq', 'xa_wkv', 'xa_wo', 'ffn2_norm', 'ffn2_w_gu', 'ffn2_w_down', 'final_norm']
WEIGHT_NAMES = INPUT_NAMES[2:]
BIG = [('ffn1_w_gu', 'col'), ('ffn1_w_down', 'row'), ('even_w_in', 'col'), ('conv_a_w', 'col'), ('even_w_out', 'row'),
       ('odd_w_in', 'col'), ('sc_conv_w', 'col'), ('odd_w_out', 'row'), ('xa_wq', 'row'), ('xa_wkv', 'col'),
       ('xa_wo', 'row'), ('ffn2_w_gu', 'col'), ('ffn2_w_down', 'row')]
BIG_NAMES = [n for n, _ in BIG]
SMALL_NAMES = [n for n in WEIGHT_NAMES if n not in BIG_NAMES]


def _cparams(sem=None, vmem=VMEM_LIMIT):
    kw = dict(vmem_limit_bytes=vmem)
    if sem is not None:
        kw['dimension_semantics'] = sem
    return pltpu.CompilerParams(**kw)


def _div_tile(n, want, align=8):
    if n <= want:
        return n
    t = (want // align) * align
    while t >= align:
        if n % t == 0:
            return t
        t -= align
    return n


def _mm(a, b, *, name, ta=False, tb=False, out_dtype=BF16, tm=512, tn=512, tk=512, res=None, scale=1.0,
        b_layer=None, stack=None):
    if ta:
        K, M = a.shape
    else:
        M, K = a.shape
    if tb:
        N, K2 = b.shape[-2:]
    else:
        K2, N = b.shape[-2:]
    assert K == K2, (a.shape, b.shape, ta, tb)
    tm = _div_tile(M, tm, 128 if ta else 16)
    tn = _div_tile(N, tn, 128)
    tk = _div_tile(K, tk, 16 if ta else 128)
    nk = K // tk
    a_spec = pl.BlockSpec((tk, tm), lambda i, j, k: (k, i)) if ta else pl.BlockSpec((tm, tk), lambda i, j, k: (i, k))
    if b_layer is None:
        b_spec = pl.BlockSpec((tn, tk), lambda i, j, k: (j, k)) if tb else pl.BlockSpec((tk, tn), lambda i, j, k: (k, j))
    elif tb:
        b_spec = pl.BlockSpec((None, tn, tk), lambda i, j, k: (b_layer, j, k))
    else:
        b_spec = pl.BlockSpec((None, tk, tn), lambda i, j, k: (b_layer, k, j))
    o_spec = pl.BlockSpec((tm, tn), lambda i, j, k: (i, j))
    out_shape = jax.ShapeDtypeStruct((M, N), out_dtype)
    out_spec = o_spec
    aliases = {}
    extra_specs, extra_args = [], ()
    if stack is not None:
        n_layers, layer, buf = stack
        out_shape = jax.ShapeDtypeStruct((n_layers, M, N), out_dtype)
        out_spec = pl.BlockSpec((None, tm, tn), lambda i, j, k: (layer, i, j))
        if buf is not None:
            extra_specs, extra_args = [pl.BlockSpec(memory_space=pl.ANY)], (buf,)
            aliases = {2 + (res is not None): 0}
    dims = (((0 if ta else 1,), (1 if tb else 0,)), ((), ()))
    has_res = res is not None
    n_extra = len(extra_args)

    def body(*refs):
        if n_extra:
            refs = refs[:2 + has_res] + refs[2 + has_res + n_extra:]
        if has_res:
            a_ref, b_ref, r_ref, o_ref, acc_ref = refs
        else:
            a_ref, b_ref, o_ref, acc_ref = refs
        k = pl.program_id(2)
        p = lax.dot_general(a_ref[...].astype(BF16), b_ref[...].astype(BF16), dims, preferred_element_type=F32)

        @pl.when(k == 0)
        def _():
            acc_ref[...] = p

        @pl.when(k > 0)
        def _():
            acc_ref[...] += p

        @pl.when(k == nk - 1)
        def _():
            r = acc_ref[...] * scale
            if has_res:
                r = r_ref[...] + r
            o_ref[...] = r.astype(out_dtype)

    in_specs = [a_spec, b_spec] + ([o_spec] if has_res else []) + extra_specs
    args = (a, b) + ((res,) if has_res else ()) + extra_args
    return pl.pallas_call(
        body, name=name, grid=(M // tm, N // tn, nk), in_specs=in_specs, out_specs=out_spec,
        out_shape=out_shape, input_output_aliases=aliases,
        scratch_shapes=[pltpu.VMEM((tm, tn), F32)],
        compiler_params=_cparams(("parallel", "parallel", "arbitrary")),
    )(*args)


def _rms_fwd(x, g, *, name):
    S, D = x.shape
    ts = _div_tile(S, 512)

    def body(x_ref, g_ref, o_ref):
        xv = x_ref[...]
        r = lax.rsqrt(jnp.mean(xv * xv, axis=-1, keepdims=True) + RMS_EPS)
        o_ref[...] = (xv * r * g_ref[...]).astype(BF16)

    return pl.pallas_call(
        body, name=name, grid=(S // ts,),
        in_specs=[pl.BlockSpec((ts, D), lambda i: (i, 0)), pl.BlockSpec((1, D), lambda i: (0, 0))],
        out_specs=pl.BlockSpec((ts, D), lambda i: (i, 0)),
        out_shape=jax.ShapeDtypeStruct((S, D), BF16),
        compiler_params=_cparams(("parallel",)),
    )(x, g)


def _rms_bwd(x, g, du, dres, *, name):
    S, D = x.shape
    ts = _div_tile(S, 512)
    has_res = dres is not None

    def body(*refs):
        if has_res:
            x_ref, g_ref, du_ref, dr_ref, dx_ref, dg_ref = refs
        else:
            x_ref, g_ref, du_ref, dg_ref = refs
        i = pl.program_id(0)
        xv = x_ref[...]
        duv = du_ref[...].astype(F32)
        r = lax.rsqrt(jnp.mean(xv * xv, axis=-1, keepdims=True) + RMS_EPS)
        xhat = xv * r
        part = jnp.sum(duv * xhat, axis=0, keepdims=True)

        @pl.when(i == 0)
        def _():
            dg_ref[...] = part

        @pl.when(i > 0)
        def _():
            dg_ref[...] += part

        if has_res:
            dxhat = duv * g_ref[...]
            dx = r * (dxhat - xhat * jnp.mean(dxhat * xhat, axis=-1, keepdims=True))
            dx_ref[...] = dr_ref[...] + dx

    row = pl.BlockSpec((ts, D), lambda i: (i, 0))
    vec = pl.BlockSpec((1, D), lambda i: (0, 0))
    if has_res:
        dx, dg = pl.pallas_call(
            body, name=name, grid=(S // ts,), in_specs=[row, vec, row, row], out_specs=[row, vec],
            out_shape=[jax.ShapeDtypeStruct((S, D), F32), jax.ShapeDtypeStruct((1, D), F32)],
            compiler_params=_cparams(("arbitrary",)),
        )(x, g, du, dres)
        return dx, dg
    dg = pl.pallas_call(
        body, name=name, grid=(S // ts,), in_specs=[row, vec, row], out_specs=vec,
        out_shape=jax.ShapeDtypeStruct((1, D), F32),
        compiler_params=_cparams(("arbitrary",)),
    )(x, g, du)
    return None, dg


def _final_loss(h, g, tgt, *, name):
    S, D = h.shape
    ts = _div_tile(S, 512)

    def body(h_ref, g_ref, t_ref, loss_ref, dh_ref, dg_ref):
        i = pl.program_id(0)
        xv = h_ref[...]
        gv = g_ref[...]
        r = lax.rsqrt(jnp.mean(xv * xv, axis=-1, keepdims=True) + RMS_EPS)
        xhat = xv * r
        err = xhat * gv - t_ref[...]
        lpart = 0.5 * jnp.sum(jnp.mean(err * err, axis=-1, keepdims=True), axis=0, keepdims=True)
        dy = err * (1.0 / D)
        gpart = jnp.sum(dy * xhat, axis=0, keepdims=True)

        @pl.when(i == 0)
        def _():
            loss_ref[...] = jnp.broadcast_to(lpart, loss_ref.shape)
            dg_ref[...] = gpart

        @pl.when(i > 0)
        def _():
            loss_ref[...] += jnp.broadcast_to(lpart, loss_ref.shape)
            dg_ref[...] += gpart

        dxhat = dy * gv
        dh_ref[...] = r * (dxhat - xhat * jnp.mean(dxhat * xhat, axis=-1, keepdims=True))

    row = pl.BlockSpec((ts, D), lambda i: (i, 0))
    vec = pl.BlockSpec((1, D), lambda i: (0, 0))
    return pl.pallas_call(
        body, name=name, grid=(S // ts,), in_specs=[row, vec, row],
        out_specs=[pl.BlockSpec((8, 128), lambda i: (0, 0)), row, vec],
        out_shape=[jax.ShapeDtypeStruct((8, 128), F32), jax.ShapeDtypeStruct((S, D), F32),
                   jax.ShapeDtypeStruct((1, D), F32)],
        compiler_params=_cparams(("arbitrary",)),
    )(h, g, tgt)


def _sigmoid(x):
    return 1.0 / (1.0 + jnp.exp(-x))


def _swiglu_fwd(gu, *, name):
    S, F2 = gu.shape
    F = F2 // 2
    ts = _div_tile(S, 256, 16)

    def body(gu_ref, o_ref):
        g = gu_ref[:, pl.ds(0, F)].astype(F32)
        u = gu_ref[:, pl.ds(F, F)].astype(F32)
        o_ref[...] = (g * _sigmoid(g) * u).astype(BF16)

    return pl.pallas_call(
        body, name=name, grid=(S // ts,),
        in_specs=[pl.BlockSpec((ts, F2), lambda i: (i, 0))],
        out_specs=pl.BlockSpec((ts, F), lambda i: (i, 0)),
        out_shape=jax.ShapeDtypeStruct((S, F), BF16),
        compiler_params=_cparams(("parallel",)),
    )(gu)


def _swiglu_bwd(gu, da, *, name):
    S, F2 = gu.shape
    F = F2 // 2
    ts = _div_tile(S, 256, 16)

    def body(gu_ref, da_ref, o_ref):
        g = gu_ref[:, pl.ds(0, F)].astype(F32)
        u = gu_ref[:, pl.ds(F, F)].astype(F32)
        d = da_ref[...].astype(F32)
        sg = _sigmoid(g)
        o_ref[:, pl.ds(0, F)] = (d * u * sg * (1.0 + g * (1.0 - sg))).astype(BF16)
        o_ref[:, pl.ds(F, F)] = (d * g * sg).astype(BF16)

    return pl.pallas_call(
        body, name=name, grid=(S // ts,),
        in_specs=[pl.BlockSpec((ts, F2), lambda i: (i, 0)), pl.BlockSpec((ts, F), lambda i: (i, 0))],
        out_specs=pl.BlockSpec((ts, F2), lambda i: (i, 0)),
        out_shape=jax.ShapeDtypeStruct((S, F2), BF16),
        compiler_params=_cparams(("parallel",)),
    )(gu, da)


CONV_HALO = 32


def _conv_a_fwd(z, w, bias, ln_g, ln_b, *, name):
    S = z.shape[0]
    C = CONV_A_CH
    ts = _div_tile(S, 256, 32)

    def body(val_ref, gate_ref, w_ref, b_ref, g_ref, lb_ref, c_ref, act_ref, win):
        i = pl.program_id(0)

        @pl.when(i == 0)
        def _():
            win[pl.ds(0, CONV_HALO), :] = jnp.zeros((CONV_HALO, C), F32)

        @pl.when(i > 0)
        def _():
            win[pl.ds(0, CONV_HALO), :] = win[pl.ds(ts, CONV_HALO), :]

        a = val_ref[...].astype(F32) * _sigmoid(gate_ref[...].astype(F32))
        win[pl.ds(CONV_HALO, ts), :] = a
        acc = jnp.broadcast_to(b_ref[...], (ts, C))
        for k in range(CONV_A_WIDTH):
            acc = acc + w_ref[pl.ds(k, 1), :] * win[pl.ds(CONV_HALO - (CONV_A_WIDTH - 1) + k, ts), :]
        c_ref[...] = acc
        mu = jnp.mean(acc, axis=-1, keepdims=True)
        xc = acc - mu
        var = jnp.mean(xc * xc, axis=-1, keepdims=True)
        ln = xc * lax.rsqrt(var + LN_EPS) * g_ref[...] + lb_ref[...]
        act_ref[...] = (ln * _sigmoid(ln)).astype(BF16)

    row = lambda col: pl.BlockSpec((ts, C), lambda i, col=col: (i, col))
    vec = pl.BlockSpec((1, C), lambda i: (0, 0))
    return pl.pallas_call(
        body, name=name, grid=(S // ts,),
        in_specs=[row(0), row(1), pl.BlockSpec((32, C), lambda i: (0, 0)), vec, vec, vec],
        out_specs=[row(0), row(0)],
        out_shape=[jax.ShapeDtypeStruct((S, C), F32), jax.ShapeDtypeStruct((S, C), BF16)],
        scratch_shapes=[pltpu.VMEM((ts + CONV_HALO, C), F32)],
        compiler_params=_cparams(("arbitrary",)),
    )(z, z, w, bias, ln_g, ln_b)


def _conv_a_bwd(z, c, dcat, w, ln_g, ln_b, *, name):
    S = z.shape[0]
    C = CONV_A_CH
    ts = _div_tile(S, 256, 32)
    n = S // ts

    def body(val_ref, gate_ref, c_ref, da_ref, w_ref, g_ref, lb_ref, dz_ref, small_ref, win):
        i = pl.program_id(0)

        @pl.when(i == 0)
        def _():
            win[pl.ds(ts, CONV_HALO), :] = jnp.zeros((CONV_HALO, C), F32)
            small_ref[...] = jnp.zeros(small_ref.shape, F32)

        @pl.when(i > 0)
        def _():
            win[pl.ds(ts, CONV_HALO), :] = win[pl.ds(0, CONV_HALO), :]

        cv = c_ref[...]
        gv = g_ref[...]
        mu = jnp.mean(cv, axis=-1, keepdims=True)
        xc = cv - mu
        var = jnp.mean(xc * xc, axis=-1, keepdims=True)
        rstd = lax.rsqrt(var + LN_EPS)
        xhat = xc * rstd
        ln = xhat * gv + lb_ref[...]
        sg = _sigmoid(ln)
        dln = da_ref[...].astype(F32) * (sg * (1.0 + ln * (1.0 - sg)))
        small_ref[pl.ds(33, 1), :] += jnp.sum(dln * xhat, axis=0, keepdims=True)
        small_ref[pl.ds(34, 1), :] += jnp.sum(dln, axis=0, keepdims=True)
        dxhat = dln * gv
        dc = rstd * (dxhat - jnp.mean(dxhat, axis=-1, keepdims=True)
                     - xhat * jnp.mean(dxhat * xhat, axis=-1, keepdims=True))
        small_ref[pl.ds(32, 1), :] += jnp.sum(dc, axis=0, keepdims=True)
        win[pl.ds(0, ts), :] = dc

        val = val_ref[...].astype(F32)
        sgg = _sigmoid(gate_ref[...].astype(F32))
        a = val * sgg
        da = jnp.zeros((ts, C), F32)
        for k in range(CONV_A_WIDTH):
            sh = win[pl.ds(CONV_A_WIDTH - 1 - k, ts), :]
            da = da + w_ref[pl.ds(k, 1), :] * sh
            small_ref[pl.ds(k, 1), :] += jnp.sum(a * sh, axis=0, keepdims=True)
        dz_ref[:, pl.ds(0, C)] = (da * sgg).astype(BF16)
        dz_ref[:, pl.ds(C, C)] = (da * val * sgg * (1.0 - sgg)).astype(BF16)

    row = lambda col: pl.BlockSpec((ts, C), lambda i, col=col: (n - 1 - i, col))
    vec = pl.BlockSpec((1, C), lambda i: (0, 0))
    return pl.pallas_call(
        body, name=name, grid=(n,),
        in_specs=[row(0), row(1), row(0), row(0), pl.BlockSpec((32, C), lambda i: (0, 0)), vec, vec],
        out_specs=[pl.BlockSpec((ts, 2 * C), lambda i: (n - 1 - i, 0)), pl.BlockSpec((40, C), lambda i: (0, 0))],
        out_shape=[jax.ShapeDtypeStruct((S, 2 * C), BF16), jax.ShapeDtypeStruct((40, C), F32)],
        scratch_shapes=[pltpu.VMEM((ts + CONV_HALO, C), F32)],
        compiler_params=_cparams(("arbitrary",)),
    )(z, z, c, dcat, w, ln_g, ln_b)


SC_HALO = 8


def _sconv_fwd(z, w, *, name):
    S = z.shape[0]
    C = SC_CH
    ts = _div_tile(S, 256, 16)

    def body(gb_ref, gc_ref, v_ref, w_ref, y_ref, cc_ref, win):
        i = pl.program_id(0)

        @pl.when(i == 0)
        def _():
            win[pl.ds(0, SC_HALO), :] = jnp.zeros((SC_HALO, C), F32)

        @pl.when(i > 0)
        def _():
            win[pl.ds(0, SC_HALO), :] = win[pl.ds(ts, SC_HALO), :]

        win[pl.ds(SC_HALO, ts), :] = gc_ref[...].astype(F32) * v_ref[...].astype(F32)
        acc = jnp.zeros((ts, C), F32)
        for k in range(SC_WIDTH):
            acc = acc + w_ref[pl.ds(k, 1), :] * win[pl.ds(SC_HALO - (SC_WIDTH - 1) + k, ts), :]
        cc_ref[...] = acc.astype(BF16)
        y_ref[...] = (gb_ref[...].astype(F32) * acc).astype(BF16)

    row = lambda col: pl.BlockSpec((ts, C), lambda i, col=col: (i, col))
    return pl.pallas_call(
        body, name=name, grid=(S // ts,),
        in_specs=[row(0), row(1), row(2), pl.BlockSpec((8, C), lambda i: (0, 0))],
        out_specs=[row(0), row(0)],
        out_shape=[jax.ShapeDtypeStruct((S, C), BF16), jax.ShapeDtypeStruct((S, C), BF16)],
        scratch_shapes=[pltpu.VMEM((ts + SC_HALO, C), F32)],
        compiler_params=_cparams(("arbitrary",)),
    )(z, z, z, w)


def _sconv_bwd(z, cc, dy, w, *, name):
    S = z.shape[0]
    C = SC_CH
    ts = _div_tile(S, 256, 16)
    n = S // ts

    def body(gb_ref, gc_ref, v_ref, cc_ref, dy_ref, w_ref, dz_ref, dw_ref, win):
        i = pl.program_id(0)

        @pl.when(i == 0)
        def _():
            win[pl.ds(ts, SC_HALO), :] = jnp.zeros((SC_HALO, C), F32)
            dw_ref[...] = jnp.zeros(dw_ref.shape, F32)

        @pl.when(i > 0)
        def _():
            win[pl.ds(ts, SC_HALO), :] = win[pl.ds(0, SC_HALO), :]

        dyv = dy_ref[...].astype(F32)
        gb = gb_ref[...].astype(F32)
        gc = gc_ref[...].astype(F32)
        val = v_ref[...].astype(F32)
        dz_ref[:, pl.ds(0, C)] = (dyv * cc_ref[...].astype(F32)).astype(BF16)
        win[pl.ds(0, ts), :] = dyv * gb
        cv = gc * val
        dcv = jnp.zeros((ts, C), F32)
        for k in range(SC_WIDTH):
            sh = win[pl.ds(SC_WIDTH - 1 - k, ts), :]
            dcv = dcv + w_ref[pl.ds(k, 1), :] * sh
            dw_ref[pl.ds(k, 1), :] += jnp.sum(cv * sh, axis=0, keepdims=True)
        dz_ref[:, pl.ds(C, C)] = (dcv * val).astype(BF16)
        dz_ref[:, pl.ds(2 * C, C)] = (dcv * gc).astype(BF16)

    row = lambda col: pl.BlockSpec((ts, C), lambda i, col=col: (n - 1 - i, col))
    return pl.pallas_call(
        body, name=name, grid=(n,),
        in_specs=[row(0), row(1), row(2), row(0), row(0), pl.BlockSpec((8, C), lambda i: (0, 0))],
        out_specs=[pl.BlockSpec((ts, 3 * C), lambda i: (n - 1 - i, 0)), pl.BlockSpec((8, C), lambda i: (0, 0))],
        out_shape=[jax.ShapeDtypeStruct((S, 3 * C), BF16), jax.ShapeDtypeStruct((8, C), F32)],
        scratch_shapes=[pltpu.VMEM((ts + SC_HALO, C), F32)],
        compiler_params=_cparams(("arbitrary",)),
    )(z, z, z, cc, dy, w)


SWA_Q_COL = 2
SWA_SLOPES = [2.0 ** (-8.0 * (h + 1) / SWA_HEADS) for h in range(SWA_HEADS)]
SWA_SCALE = HEAD_DIM ** -0.5


def _swa_masks():
    ii = lax.broadcasted_iota(jnp.int32, (WINDOW, 2 * WINDOW), 0)
    jj = lax.broadcasted_iota(jnp.int32, (WINDOW, 2 * WINDOW), 1)
    dist = ii + WINDOW - jj
    valid = (dist >= 0) & (dist < WINDOW)
    return dist.astype(F32), valid, jj


def _swa_probs(qh, kk, sink, slope, distf, valid):
    s = lax.dot_general(qh, kk, (((1,), (1,)), ((), ())), preferred_element_type=F32) * SWA_SCALE
    s = s - slope * distf
    s = jnp.where(valid, s, NEG_BIG)
    m = jnp.maximum(jnp.max(s, axis=-1, keepdims=True), sink)
    p = jnp.exp(s - m)
    l = jnp.sum(p, axis=-1, keepdims=True) + jnp.exp(sink - m)
    return p, m, l


def _swa_fwd(z, kpad, vpad, sinks, *, name):
    S = z.shape[0]
    tq = _div_tile(S, 256, 128)
    nblk = tq // WINDOW
    W = WINDOW

    def body(sink_ref, q_ref, k_ref, v_ref, o_ref):
        i = pl.program_id(0)
        distf, valid0, jj = _swa_masks()
        for b in range(nblk):
            nb = i * nblk + b
            start = pl.multiple_of(nb * W, W)
            valid = valid0 & ((jj >= W) | (nb > 0))
            for kv in range(SWA_KV_HEADS):
                kk = k_ref[pl.ds(start, 2 * W), pl.ds(HEAD_DIM * kv, HEAD_DIM)]
                vv = v_ref[pl.ds(start, 2 * W), pl.ds(HEAD_DIM * kv, HEAD_DIM)]
                for g in range(SWA_GROUP):
                    h = kv * SWA_GROUP + g
                    qh = q_ref[pl.ds(W * b, W), pl.ds(HEAD_DIM * h, HEAD_DIM)]
                    p, m, l = _swa_probs(qh, kk, sink_ref[h], SWA_SLOPES[h], distf, valid)
                    o = jnp.dot(p.astype(BF16), vv, preferred_element_type=F32) / l
                    o_ref[pl.ds(W * b, W), pl.ds(HEAD_DIM * h, HEAD_DIM)] = o.astype(BF16)

    full = pl.BlockSpec((S + W, 2 * HEAD_DIM), lambda i: (0, 0))
    return pl.pallas_call(
        body, name=name, grid=(S // tq,),
        in_specs=[pl.BlockSpec(memory_space=pltpu.SMEM), pl.BlockSpec((tq, 512), lambda i: (i, SWA_Q_COL)), full, full],
        out_specs=pl.BlockSpec((tq, 512), lambda i: (i, 0)),
        out_shape=jax.ShapeDtypeStruct((S, 512), BF16),
        compiler_params=_cparams(("parallel",)),
    )(sinks, z, kpad, vpad)


def _swa_bwd(z, kpad, vpad, sinks, dcat, *, name):
    S = z.shape[0]
    tq = _div_tile(S, 256, 128)
    nblk = tq // WINDOW
    W = WINDOW

    def body(sink_ref, q_ref, k_ref, v_ref, do_ref, dq_ref, dk_ref, dv_ref, ds_ref):
        i = pl.program_id(0)

        @pl.when(i == 0)
        def _():
            dk_ref[...] = jnp.zeros(dk_ref.shape, F32)
            dv_ref[...] = jnp.zeros(dv_ref.shape, F32)
            ds_ref[...] = jnp.zeros(ds_ref.shape, F32)

        distf, valid0, jj = _swa_masks()
        for b in range(nblk):
            nb = i * nblk + b
            start = pl.multiple_of(nb * W, W)
            valid = valid0 & ((jj >= W) | (nb > 0))
            for kv in range(SWA_KV_HEADS):
                kk = k_ref[pl.ds(start, 2 * W), pl.ds(HEAD_DIM * kv, HEAD_DIM)]
                vv = v_ref[pl.ds(start, 2 * W), pl.ds(HEAD_DIM * kv, HEAD_DIM)]
                dkk = jnp.zeros((2 * W, HEAD_DIM), F32)
                dvv = jnp.zeros((2 * W, HEAD_DIM), F32)
                for g in range(SWA_GROUP):
                    h = kv * SWA_GROUP + g
                    qh = q_ref[pl.ds(W * b, W), pl.ds(HEAD_DIM * h, HEAD_DIM)]
                    doh = do_ref[pl.ds(W * b, W), pl.ds(HEAD_DIM * h, HEAD_DIM)]
                    sink = sink_ref[h]
                    p, m, l = _swa_probs(qh, kk, sink, SWA_SLOPES[h], distf, valid)
                    inv_l = 1.0 / l
                    pn = p * inv_l
                    dp = lax.dot_general(doh, vv, (((1,), (1,)), ((), ())), preferred_element_type=F32)
                    delta = jnp.sum(pn * dp, axis=-1, keepdims=True)
                    dsc = (pn * (dp - delta)).astype(BF16)
                    psink = jnp.exp(sink - m) * inv_l
                    ds_ref[pl.ds(h, 1), :] += jnp.broadcast_to(
                        -jnp.sum(psink * delta, axis=0, keepdims=True), (1, 128))
                    dq = jnp.dot(dsc, kk, preferred_element_type=F32) * SWA_SCALE
                    dq_ref[pl.ds(W * b, W), pl.ds(HEAD_DIM * h, HEAD_DIM)] = dq.astype(BF16)
                    dkk = dkk + lax.dot_general(dsc, qh, (((0,), (0,)), ((), ())),
                                                preferred_element_type=F32) * SWA_SCALE
                    dvv = dvv + lax.dot_general(pn.astype(BF16), doh, (((0,), (0,)), ((), ())),
                                                preferred_element_type=F32)
                dk_ref[pl.ds(start, 2 * W), pl.ds(HEAD_DIM * kv, HEAD_DIM)] += dkk
                dv_ref[pl.ds(start, 2 * W), pl.ds(HEAD_DIM * kv, HEAD_DIM)] += dvv

    full = pl.BlockSpec((S + W, 2 * HEAD_DIM), lambda i: (0, 0))
    return pl.pallas_call(
        body, name=name, grid=(S // tq,),
        in_specs=[pl.BlockSpec(memory_space=pltpu.SMEM), pl.BlockSpec((tq, 512), lambda i: (i, SWA_Q_COL)), full, full,
                  pl.BlockSpec((tq, 512), lambda i: (i, 1))],
        out_specs=[pl.BlockSpec((tq, 512), lambda i: (i, 0)), full, full, pl.BlockSpec((8, 128), lambda i: (0, 0))],
        out_shape=[jax.ShapeDtypeStruct((S, 512), BF16), jax.ShapeDtypeStruct((S + W, 2 * HEAD_DIM), F32),
                   jax.ShapeDtypeStruct((S + W, 2 * HEAD_DIM), F32), jax.ShapeDtypeStruct((8, 128), F32)],
        compiler_params=_cparams(("arbitrary",)),
    )(sinks, z, kpad, vpad, dcat)


XA_SCALE = XA_HEAD_DIM ** -0.5


def _xa_probs(qh, kh):
    s = lax.dot_general(qh, kh, (((1,), (1,)), ((), ())), preferred_element_type=F32) * XA_SCALE
    m = jnp.max(s, axis=-1, keepdims=True)
    p = jnp.exp(s - m)
    return p, jnp.sum(p, axis=-1, keepdims=True)


def _xa_fwd(q, kv, *, name):
    S, D = q.shape
    M = kv.shape[0]
    ts = _div_tile(S, 512, 16)
    HD = XA_HEAD_DIM

    def body(q_ref, k_ref, v_ref, o_ref):
        for h in range(XA_HEADS):
            qh = q_ref[:, pl.ds(HD * h, HD)]
            p, l = _xa_probs(qh, k_ref[:, pl.ds(HD * h, HD)])
            o = jnp.dot(p.astype(BF16), v_ref[:, pl.ds(HD * h, HD)], preferred_element_type=F32) / l
            o_ref[:, pl.ds(HD * h, HD)] = o.astype(BF16)

    return pl.pallas_call(
        body, name=name, grid=(S // ts,),
        in_specs=[pl.BlockSpec((ts, D), lambda i: (i, 0)), pl.BlockSpec((M, D), lambda i: (0, 0)),
                  pl.BlockSpec((M, D), lambda i: (0, 1))],
        out_specs=pl.BlockSpec((ts, D), lambda i: (i, 0)),
        out_shape=jax.ShapeDtypeStruct((S, D), BF16),
        compiler_params=_cparams(("parallel",)),
    )(q, kv, kv)


def _xa_bwd(q, kv, do, *, name):
    S, D = q.shape
    M = kv.shape[0]
    ts = _div_tile(S, 512, 16)
    HD = XA_HEAD_DIM

    def body(q_ref, k_ref, v_ref, do_ref, dq_ref, dkv_ref):
        i = pl.program_id(0)

        @pl.when(i == 0)
        def _():
            dkv_ref[...] = jnp.zeros(dkv_ref.shape, F32)

        for h in range(XA_HEADS):
            qh = q_ref[:, pl.ds(HD * h, HD)]
            kh = k_ref[:, pl.ds(HD * h, HD)]
            vh = v_ref[:, pl.ds(HD * h, HD)]
            doh = do_ref[:, pl.ds(HD * h, HD)]
            p, l = _xa_probs(qh, kh)
            pn = p * (1.0 / l)
            dp = lax.dot_general(doh, vh, (((1,), (1,)), ((), ())), preferred_element_type=F32)
            delta = jnp.sum(pn * dp, axis=-1, keepdims=True)
            dsc = (pn * (dp - delta)).astype(BF16)
            dq_ref[:, pl.ds(HD * h, HD)] = (jnp.dot(dsc, kh, preferred_element_type=F32) * XA_SCALE).astype(BF16)
            dkv_ref[:, pl.ds(HD * h, HD)] += lax.dot_general(
                dsc, qh, (((0,), (0,)), ((), ())), preferred_element_type=F32) * XA_SCALE
            dkv_ref[:, pl.ds(D + HD * h, HD)] += lax.dot_general(
                pn.astype(BF16), doh, (((0,), (0,)), ((), ())), preferred_element_type=F32)

    row = pl.BlockSpec((ts, D), lambda i: (i, 0))
    return pl.pallas_call(
        body, name=name, grid=(S // ts,),
        in_specs=[row, pl.BlockSpec((M, D), lambda i: (0, 0)), pl.BlockSpec((M, D), lambda i: (0, 1)), row],
        out_specs=[row, pl.BlockSpec((M, 2 * D), lambda i: (0, 0))],
        out_shape=[jax.ShapeDtypeStruct((S, D), BF16), jax.ShapeDtypeStruct((M, 2 * D), F32)],
        compiler_params=_cparams(("arbitrary",)),
    )(q, kv, kv, do)


def _adam_math(w, g, m, v):
    m = ADAM_B1 * m + (1.0 - ADAM_B1) * g
    v = ADAM_B2 * v + (1.0 - ADAM_B2) * (g * g)
    m_hat = m / (1.0 - ADAM_B1 ** ADAM_STEP)
    v_hat = v / (1.0 - ADAM_B2 ** ADAM_STEP)
    delta = -ADAM_LR * (m_hat / (jnp.sqrt(v_hat) + ADAM_EPS) + ADAM_WD * w)
    return delta, m, v


def _adamw(w, g, m, v, *, name):
    R, C = w.shape
    tr = _div_tile(R, max(8, (256 * 1024) // C // 8 * 8))

    def body(w_ref, g_ref, m_ref, v_ref, d_ref, nm_ref, nv_ref):
        d, nm, nv = _adam_math(w_ref[...], g_ref[...], m_ref[...], v_ref[...])
        d_ref[...] = d
        nm_ref[...] = nm
        nv_ref[...] = nv

    spec = pl.BlockSpec((tr, C), lambda i: (i, 0))
    sds = jax.ShapeDtypeStruct((R, C), F32)
    return pl.pallas_call(
        body, name=name, grid=(R // tr,), in_specs=[spec] * 4, out_specs=[spec] * 3, out_shape=[sds] * 3,
        compiler_params=_cparams(("parallel",)),
    )(w, g, m, v)


def _adamw_small(w, gparts, m, v, *, name):
    R, C = w.shape

    def body(w_ref, gp_ref, m_ref, v_ref, g_ref, d_ref, nm_ref, nv_ref):
        g = gp_ref[0]
        for k in range(1, N_DEV):
            g = g + gp_ref[k]
        d, nm, nv = _adam_math(w_ref[...], g, m_ref[...], v_ref[...])
        g_ref[...] = g
        d_ref[...] = d
        nm_ref[...] = nm
        nv_ref[...] = nv

    sds = jax.ShapeDtypeStruct((R, C), F32)
    return pl.pallas_call(body, name=name, out_shape=[sds] * 4, compiler_params=_cparams())(w, gparts, m, v)


ANY = pl.BlockSpec(memory_space=pl.ANY)


def _mesh_pos():
    return lax.axis_index("x"), lax.axis_index("y"), lax.axis_index("c")


def _other_chips(x, y):
    return [(1 - x, y), (x, 1 - y), (1 - x, 1 - y)]


LAYOUT = {'ffn1_w_gu': 'col', 'ffn1_w_down': 'stk', 'even_w_in': 'stk', 'even_w_out': 'stk', 'odd_w_in': 'col',
          'odd_w_out': 'stk', 'xa_wq': 'stk', 'xa_wkv': 'col', 'xa_wo': 'stk', 'ffn2_w_gu': 'col',
          'ffn2_w_down': 'stk', 'tiny': 'stk'}
COMM_NAMES = list(LAYOUT)
TINY_ROWS = 48


def _piece_rows(L, A):
    return A if L == 2 else A // 2


def _shard_piece(ref, L, A, h):
    if L == 2:
        return ref.at[h]
    return ref.at[0, pl.ds(pl.multiple_of(h * (A // 2), 8), A // 2)]


def _gathered_piece(ref, kind, L, A, h):
    if L == 2:
        return ref.at[h]
    rows = pl.ds(pl.multiple_of(h * (A // 2), 8), A // 2)
    return ref.at[0, rows] if kind == 'col' else ref.at[0, :, rows]


def _chip_part(piece, kind, B, s):
    if kind == 'col':
        return piece.at[:, pl.ds(pl.multiple_of(s * B, 128), B)]
    return piece.at[s]


def _place(shard, kind, chip_idx, out_dtype, *, name):
    L, A, B = shard.shape
    ta = _div_tile(A, 256, 16)

    def body(s_ref, x_ref, o_ref):
        o_ref[...] = x_ref[...].astype(out_dtype)

    if kind == 'col':
        shape = (L, A, N_CHIPS * B)
        out_spec = pl.BlockSpec((None, ta, B), lambda l, i, s: (l, i, s[0]))
    else:
        shape = (L, N_CHIPS, A, B)
        out_spec = pl.BlockSpec((None, None, ta, B), lambda l, i, s: (l, s[0], i, 0))
    grid_spec = pltpu.PrefetchScalarGridSpec(
        num_scalar_prefetch=1, grid=(L, A // ta),
        in_specs=[pl.BlockSpec((None, ta, B), lambda l, i, s: (l, i, 0))], out_specs=out_spec)
    return pl.pallas_call(
        body, name=name, grid_spec=grid_spec, out_shape=jax.ShapeDtypeStruct(shape, out_dtype),
        compiler_params=_cparams(("parallel", "parallel")),
    )(chip_idx, shard)


def _all_gather(fulls, meta):
    n = len(fulls)

    def body(*refs):
        outs = refs[n:2 * n]
        send_sems, recv_sems = refs[2 * n:]
        x, y, c = _mesh_pos()
        sibling = (x, y, 1 - c)
        chips = _other_chips(x, y)

        def part(k, s, h):
            kind, L, A, B = meta[k]
            return _chip_part(_gathered_piece(outs[k], kind, L, A, h), kind, B, s)

        def copy(ref, sem, to):
            return pltpu.make_async_remote_copy(src_ref=ref, dst_ref=ref, send_sem=send_sems.at[sem],
                                                recv_sem=recv_sems.at[sem], device_id=to, device_id_type=MESH)

        started = []
        for k in range(n):
            for j, (cx, cy) in enumerate(chips):
                cp = copy(part(k, 2 * x + y, c), 3 * k + j, (cx, cy, c))
                cp.start()
                started.append(cp)
        for j, (cx, cy) in enumerate(chips):
            for k in range(n):
                landed = part(k, 2 * cx + cy, c)
                copy(landed, 3 * k + j, (cx, cy, c)).wait_recv()
                fwd = copy(landed, 3 * n + 3 * k + j, sibling)
                fwd.start()
                started.append(fwd)
        for j, (cx, cy) in enumerate(chips):
            for k in range(n):
                copy(part(k, 2 * cx + cy, 1 - c), 3 * n + 3 * k + j, sibling).wait_recv()
        for cp in started:
            cp.wait_send()

    return pl.pallas_call(
        body, name="ag_weights", in_specs=[ANY] * n, out_specs=[ANY] * n,
        out_shape=[jax.ShapeDtypeStruct(f.shape, f.dtype) for f in fulls],
        input_output_aliases={k: k for k in range(n)},
        scratch_shapes=[pltpu.SemaphoreType.DMA((6 * n,)), pltpu.SemaphoreType.DMA((6 * n,))],
    )(*fulls)


def _pair_exchange(gs, meta):
    n = len(gs)

    def body(*refs):
        g_refs, out_refs = refs[:n], refs[n:2 * n]
        send_sems, recv_sems = refs[2 * n:]
        x, y, c = _mesh_pos()
        cps = []
        for k in range(n):
            kind, L, A, B = meta[k]
            cp = pltpu.make_async_remote_copy(
                src_ref=_gathered_piece(g_refs[k], kind, L, A, 1 - c), dst_ref=out_refs[k],
                send_sem=send_sems.at[k], recv_sem=recv_sems.at[k], device_id=(x, y, 1 - c), device_id_type=MESH)
            cp.start()
            cps.append(cp)
        for cp in cps:
            cp.wait()

    shapes = []
    for g, (kind, L, A, B) in zip(gs, meta):
        ap = _piece_rows(L, A)
        shapes.append(jax.ShapeDtypeStruct((ap, N_CHIPS * B) if kind == 'col' else (N_CHIPS, ap, B), g.dtype))
    return pl.pallas_call(
        body, name="rs_pair_exchange", in_specs=[ANY] * n, out_specs=[ANY] * n, out_shape=shapes,
        scratch_shapes=[pltpu.SemaphoreType.DMA((n,)), pltpu.SemaphoreType.DMA((n,))],
    )(*gs)


def _pair_sum(g, got, m, c_idx, *, name):
    kind, L, A, B = m
    ap = _piece_rows(L, A)
    ta = _div_tile(ap, 256, 16)
    nt = ap // ta
    dt = g.dtype

    def body(c_ref, a_ref, b_ref, o_ref):
        o_ref[...] = (a_ref[...].astype(F32) + b_ref[...].astype(F32)).astype(dt)

    if kind == 'col':
        grid = (nt,)
        gmap = (lambda i, c: (c[0], i, 0)) if L == 2 else (lambda i, c: (0, c[0] * nt + i, 0))
        g_spec = pl.BlockSpec((None, ta, N_CHIPS * B), gmap)
        r_spec = pl.BlockSpec((ta, N_CHIPS * B), lambda i, c: (i, 0))
        shape = (ap, N_CHIPS * B)
        sem = ("parallel",)
    else:
        grid = (N_CHIPS, nt)
        gmap = (lambda s, i, c: (c[0], s, i, 0)) if L == 2 else (lambda s, i, c: (0, s, c[0] * nt + i, 0))
        g_spec = pl.BlockSpec((None, None, ta, B), gmap)
        r_spec = pl.BlockSpec((None, ta, B), lambda s, i, c: (s, i, 0))
        shape = (N_CHIPS, ap, B)
        sem = ("parallel", "parallel")
    grid_spec = pltpu.PrefetchScalarGridSpec(num_scalar_prefetch=1, grid=grid, in_specs=[g_spec, r_spec],
                                             out_specs=r_spec)
    return pl.pallas_call(
        body, name=name, grid_spec=grid_spec, out_shape=jax.ShapeDtypeStruct(shape, dt), compiler_params=_cparams(sem),
    )(c_idx, g, got)


def _chip_exchange(ps, meta):
    n = len(ps)

    def body(*refs):
        p_refs, out_refs = refs[:n], refs[n:2 * n]
        send_sems, recv_sems = refs[2 * n:]
        x, y, c = _mesh_pos()
        cps = []
        for k in range(n):
            kind, L, A, B = meta[k]
            for j, (cx, cy) in enumerate(_other_chips(x, y)):
                cp = pltpu.make_async_remote_copy(
                    src_ref=_chip_part(p_refs[k], kind, B, 2 * cx + cy), dst_ref=out_refs[k].at[j],
                    send_sem=send_sems.at[3 * k + j], recv_sem=recv_sems.at[3 * k + j], device_id=(cx, cy, c),
                    device_id_type=MESH)
                cp.start()
                cps.append(cp)
        for cp in cps:
            cp.wait()

    shapes = [jax.ShapeDtypeStruct((3, _piece_rows(L, A), B), p.dtype) for p, (kind, L, A, B) in zip(ps, meta)]
    return pl.pallas_call(
        body, name="rs_chip_exchange", in_specs=[ANY] * n, out_specs=[ANY] * n, out_shape=shapes,
        scratch_shapes=[pltpu.SemaphoreType.DMA((3 * n,)), pltpu.SemaphoreType.DMA((3 * n,))],
    )(*ps)


def _chip_sum(p, got, m, sc_idx, *, name):
    kind, L, A, B = m
    ap = _piece_rows(L, A)
    ta = _div_tile(ap, 256, 16)
    nt = ap // ta

    def body(r_ref, a_ref, b_ref, o_ref):
        acc = a_ref[...].astype(F32)
        for j in range(3):
            acc = acc + b_ref[j].astype(F32)
        o_ref[...] = acc

    if kind == 'col':
        p_spec = pl.BlockSpec((ta, B), lambda i, r: (i, r[0]))
    else:
        p_spec = pl.BlockSpec((None, ta, B), lambda i, r: (r[0], i, 0))
    omap = (lambda i, r: (r[1], i, 0)) if L == 2 else (lambda i, r: (0, r[1] * nt + i, 0))
    grid_spec = pltpu.PrefetchScalarGridSpec(
        num_scalar_prefetch=1, grid=(nt,),
        in_specs=[p_spec, pl.BlockSpec((3, ta, B), lambda i, r: (0, i, 0))],
        out_specs=pl.BlockSpec((None, ta, B), omap))
    return pl.pallas_call(
        body, name=name, grid_spec=grid_spec, out_shape=jax.ShapeDtypeStruct((L, A, B), F32),
        compiler_params=_cparams(("parallel",)),
    )(sc_idx, p, got)


def _final_exchange(gls, meta, small):
    n = len(gls)
    rs, cs = small.shape

    def body(*refs):
        small_ref = refs[n]
        outs = refs[n + 1:2 * n + 1]
        sm_ref = refs[2 * n + 1]
        send_sems, recv_sems, local_sem = refs[2 * n + 2:]
        x, y, c = _mesh_pos()
        me = 4 * x + 2 * y + c
        own_s = pltpu.make_async_copy(small_ref, sm_ref.at[me], local_sem)
        own_s.start()
        cps = []
        for k in range(n):
            kind, L, A, B = meta[k]
            half = _shard_piece(outs[k], L, A, c)
            cp = pltpu.make_async_remote_copy(src_ref=half, dst_ref=half, send_sem=send_sems.at[k],
                                              recv_sem=recv_sems.at[k], device_id=(x, y, 1 - c), device_id_type=MESH)
            cp.start()
            cps.append(cp)
        for r in range(1, N_DEV):
            fx, fy, fc = (r >> 2) & 1, (r >> 1) & 1, r & 1
            peer = (1 - x if fx else x, 1 - y if fy else y, 1 - c if fc else c)
            cp = pltpu.make_async_remote_copy(
                src_ref=small_ref, dst_ref=sm_ref.at[me], send_sem=send_sems.at[n + r], recv_sem=recv_sems.at[n + r],
                device_id=peer, device_id_type=MESH)
            cp.start()
            cps.append(cp)
        for cp in cps:
            cp.wait()
        own_s.wait()

    res = pl.pallas_call(
        body, name="rs_final_exchange", in_specs=[ANY] * (n + 1), out_specs=[ANY] * (n + 1),
        out_shape=[jax.ShapeDtypeStruct(g.shape, g.dtype) for g in gls] + [jax.ShapeDtypeStruct((N_DEV, rs, cs), F32)],
        input_output_aliases={k: k for k in range(n)},
        scratch_shapes=[pltpu.SemaphoreType.DMA((n + N_DEV,)), pltpu.SemaphoreType.DMA((n + N_DEV,)),
                        pltpu.SemaphoreType.DMA],
    )(*gls, small)
    return res[:n], res[n]


def _tiny_pack(conv_a_w, sc_conv_w):
    lead = conv_a_w.shape[:-2]
    sc = sc_conv_w.reshape(lead + (2 * SC_WIDTH, 128))
    z = lambda r: jnp.zeros(lead + (r, 128), F32)
    return jnp.concatenate([conv_a_w, z(32 - CONV_A_WIDTH), sc, z(TINY_ROWS - 32 - 2 * SC_WIDTH)], axis=-2)


def _tiny_unpack(t):
    lead = t.shape[:-2]
    return t[..., :CONV_A_WIDTH, :], t[..., 32:32 + 2 * SC_WIDTH, :].reshape(lead + (SC_WIDTH, 256))


def _pack_small(d):
    flat = jnp.concatenate([d[n].astype(F32).reshape(-1) for n in SMALL_NAMES])
    n = flat.shape[0]
    total = -(-n // 1024) * 1024
    return jnp.pad(flat, (0, total - n)).reshape(total // 128, 128)


def _unpack_small(packed, shapes):
    flat = packed.reshape(-1)
    out, off = {}, 0
    for n in SMALL_NAMES:
        sz = math.prod(shapes[n])
        out[n] = flat[off:off + sz].reshape(shapes[n])
        off += sz
    return out


def _ffn_fwd(h, g, W, n_gu, n_down, i, tag):
    u = _rms_fwd(h, g, name=f"{tag}_norm")
    gu = _mm(u, W[n_gu], b_layer=i, name=f"{tag}_gu", tm=1024, tn=1408, tk=1024)
    a = _swiglu_fwd(gu, name=f"{tag}_act")
    h2 = _mm(a, W[n_down], b_layer=i, name=f"{tag}_down", out_dtype=F32, tm=512, tn=1024, tk=2816, res=h, scale=0.5)
    return h2, (h, u, gu, a)


def _ffn_bwd(dh, saved, g, W, n_gu, n_down, i, G, tag):
    h, u, gu, a = saved
    da = _mm(dh, W[n_down], b_layer=i, name=f"{tag}_b_da", tb=True, tm=512, tn=1408, tk=1024, scale=0.5)
    G[n_down] = _mm(a, dh, name=f"{tag}_b_wdown", ta=True, tm=1408, tn=1024, tk=1024, scale=0.5,
                    stack=(2, i, G.get(n_down)))
    dgu = _swiglu_bwd(gu, da, name=f"{tag}_b_act")
    G[n_gu] = _mm(u, dgu, name=f"{tag}_b_wgu", ta=True, tm=1024, tn=1408, tk=1024, stack=(2, i, G.get(n_gu)))
    du = _mm(dgu, W[n_gu], b_layer=i, name=f"{tag}_b_du", tb=True, out_dtype=F32, tm=512, tn=1024, tk=1408)
    return _rms_bwd(h, g, du, dh, name=f"{tag}_b_norm")


def _xa_block_fwd(h, mem, g, gm, W, i, tag):
    u = _rms_fwd(h, g, name=f"{tag}_norm")
    mn = _rms_fwd(mem, gm, name=f"{tag}_mem_norm")
    q = _mm(u, W['xa_wq'], b_layer=i, name=f"{tag}_q", tm=1024, tn=1024, tk=1024)
    kv = _mm(mn, W['xa_wkv'], b_layer=i, name=f"{tag}_kv", tm=256, tn=1024, tk=1024)
    o = _xa_fwd(q, kv, name=f"{tag}_attn")
    h2 = _mm(o, W['xa_wo'], b_layer=i, name=f"{tag}_o", out_dtype=F32, tm=1024, tn=1024, tk=1024, res=h)
    return h2, (h, u, mn, q, kv, o)


def _xa_block_bwd(dh, saved, mem, g, gm, W, i, G, tag):
    h, u, mn, q, kv, o = saved
    do = _mm(dh, W['xa_wo'], b_layer=i, name=f"{tag}_b_do", tb=True, tm=1024, tn=1024, tk=1024)
    G['xa_wo'] = _mm(o, dh, name=f"{tag}_b_wo", ta=True, tm=1024, tn=1024, tk=1024, stack=(2, i, G.get('xa_wo')))
    dq, dkv = _xa_bwd(q, kv, do, name=f"{tag}_b_attn")
    G['xa_wq'] = _mm(u, dq, name=f"{tag}_b_wq", ta=True, tm=1024, tn=1024, tk=1024, stack=(2, i, G.get('xa_wq')))
    du = _mm(dq, W['xa_wq'], b_layer=i, name=f"{tag}_b_du", tb=True, out_dtype=F32, tm=1024, tn=1024, tk=1024)
    G['xa_wkv'] = _mm(mn, dkv, name=f"{tag}_b_wkv", ta=True, tm=1024, tn=1024, tk=256,
                      stack=(2, i, G.get('xa_wkv')))
    dmn = _mm(dkv, W['xa_wkv'], b_layer=i, name=f"{tag}_b_dmn", tb=True, out_dtype=F32, tm=256, tn=1024, tk=1024)
    _, dgm = _rms_bwd(mem, gm, dmn, None, name=f"{tag}_b_mem_norm")
    dh_in, dg = _rms_bwd(h, g, du, dh, name=f"{tag}_b_norm")
    return dh_in, dg, dgm


def _pad_conv_w(w, rows):
    return jnp.pad(w.astype(F32), ((0, rows - w.shape[0]), (0, 0)))


def _even_fwd(h, g, W, conv_w, conv_b, ln_g, ln_b, sinks, tag):
    u = _rms_fwd(h, g, name=f"{tag}_norm")
    z = _mm(u, W['even_w_in'], b_layer=0, name=f"{tag}_in", tm=1024, tn=1792, tk=1024)
    c, act = _conv_a_fwd(z, conv_w, conv_b, ln_g, ln_b, name=f"{tag}_conv")
    kpad = jnp.pad(z[:, 1536:1664], ((WINDOW, 0), (0, 0)))
    vpad = jnp.pad(z[:, 1664:1792], ((WINDOW, 0), (0, 0)))
    o = _swa_fwd(z, kpad, vpad, sinks, name=f"{tag}_swa")
    cat = jnp.concatenate([act, o], axis=-1)
    h2 = _mm(cat, W['even_w_out'], b_layer=0, name=f"{tag}_out", out_dtype=F32, tm=1024, tn=1024, tk=1024, res=h)
    return h2, (h, u, z, c, kpad, vpad, cat)


def _even_bwd(dh, saved, g, W, conv_w, ln_g, ln_b, sinks, G, tag):
    h, u, z, c, kpad, vpad, cat = saved
    dcat = _mm(dh, W['even_w_out'], b_layer=0, name=f"{tag}_b_dcat", tb=True, tm=1024, tn=1024, tk=1024)
    G['even_w_out'] = _mm(cat, dh, name=f"{tag}_b_wout", ta=True, tm=1024, tn=1024, tk=1024)
    dz_a, small = _conv_a_bwd(z, c, dcat, conv_w, ln_g, ln_b, name=f"{tag}_b_conv")
    dq, dkp, dvp, dsinks = _swa_bwd(z, kpad, vpad, sinks, dcat, name=f"{tag}_b_swa")
    dz = jnp.concatenate([dz_a, dq, dkp[WINDOW:].astype(BF16), dvp[WINDOW:].astype(BF16)], axis=-1)
    G['even_w_in'] = _mm(u, dz, name=f"{tag}_b_win", ta=True, tm=1024, tn=1792, tk=1024)
    du = _mm(dz, W['even_w_in'], b_layer=0, name=f"{tag}_b_du", tb=True, out_dtype=F32, tm=1024, tn=1024, tk=1792)
    dh_in, dg = _rms_bwd(h, g, du, dh, name=f"{tag}_b_norm")
    grads = dict(mix=dg, conv_a_w=small[:CONV_A_WIDTH], conv_a_b=small[32:33], conv_a_ln_g=small[33:34],
                 conv_a_ln_b=small[34:35], swa_sinks=dsinks[:, 0])
    return dh_in, grads


def _odd_fwd(h, g, W, conv_w, tag):
    u = _rms_fwd(h, g, name=f"{tag}_norm")
    z = _mm(u, W['odd_w_in'], b_layer=0, name=f"{tag}_in", tm=1024, tn=1024, tk=1024)
    y, cc = _sconv_fwd(z, conv_w, name=f"{tag}_conv")
    h2 = _mm(y, W['odd_w_out'], b_layer=0, name=f"{tag}_out", out_dtype=F32, tm=1024, tn=1024, tk=1024, res=h)
    return h2, (h, u, z, y, cc)


def _odd_bwd(dh, saved, g, W, conv_w, G, tag):
    h, u, z, y, cc = saved
    dy = _mm(dh, W['odd_w_out'], b_layer=0, name=f"{tag}_b_dy", tb=True, tm=1024, tn=1024, tk=1024)
    G['odd_w_out'] = _mm(y, dh, name=f"{tag}_b_wout", ta=True, tm=1024, tn=1024, tk=1024)
    dz, dw = _sconv_bwd(z, cc, dy, conv_w, name=f"{tag}_b_conv")
    G['odd_w_in'] = _mm(u, dz, name=f"{tag}_b_win", ta=True, tm=1024, tn=1024, tk=1024)
    du = _mm(dz, W['odd_w_in'], b_layer=0, name=f"{tag}_b_du", tb=True, out_dtype=F32, tm=1024, tn=1024, tk=1024)
    dh_in, dg = _rms_bwd(h, g, du, dh, name=f"{tag}_b_norm")
    return dh_in, dict(mix=dg, sc_conv_w=dw[:SC_WIDTH])


def _local_step(x, mem, tgt, W, conv_a_w, sc_conv_w, P):
    row = lambda v: v.reshape(1, -1)
    conv_a_w = _pad_conv_w(conv_a_w, 32)
    sc_w = _pad_conv_w(sc_conv_w, 8)
    sinks = P['swa_sinks'][0]

    h = x
    saved = []
    for i in range(2):
        t = f"l{i}"
        h, s1 = _ffn_fwd(h, row(P['ffn1_norm'][i]), W, 'ffn1_w_gu', 'ffn1_w_down', i, f"{t}_ffn1")
        if i == 0:
            h, s2 = _even_fwd(h, row(P['mix_norm'][i]), W, conv_a_w, P['conv_a_b'], P['conv_a_ln_g'],
                              P['conv_a_ln_b'], sinks, f"{t}_even")
        else:
            h, s2 = _odd_fwd(h, row(P['mix_norm'][i]), W, sc_w, f"{t}_odd")
        h, s3 = _xa_block_fwd(h, mem, row(P['xa_norm'][i]), row(P['xa_mem_norm'][i]), W, i, f"{t}_xa")
        h, s4 = _ffn_fwd(h, row(P['ffn2_norm'][i]), W, 'ffn2_w_gu', 'ffn2_w_down', i, f"{t}_ffn2")
        saved.append((s1, s2, s3, s4))

    loss, dh, d_final = _final_loss(h, row(P['final_norm']), tgt, name="final_loss")

    G = {}
    gp = {n: [None, None] for n in ('ffn1_norm', 'mix_norm', 'xa_norm', 'xa_mem_norm', 'ffn2_norm')}
    single = {}
    for i in (1, 0):
        t = f"l{i}"
        s1, s2, s3, s4 = saved[i]
        dh, gp['ffn2_norm'][i] = _ffn_bwd(dh, s4, row(P['ffn2_norm'][i]), W, 'ffn2_w_gu', 'ffn2_w_down', i, G,
                                          f"{t}_ffn2")
        dh, gp['xa_norm'][i], gp['xa_mem_norm'][i] = _xa_block_bwd(
            dh, s3, mem, row(P['xa_norm'][i]), row(P['xa_mem_norm'][i]), W, i, G, f"{t}_xa")
        if i == 0:
            dh, g2 = _even_bwd(dh, s2, row(P['mix_norm'][i]), W, conv_a_w, P['conv_a_ln_g'], P['conv_a_ln_b'], sinks,
                               G, f"{t}_even")
        else:
            dh, g2 = _odd_bwd(dh, s2, row(P['mix_norm'][i]), W, sc_w, G, f"{t}_odd")
        gp['mix_norm'][i] = g2.pop('mix')
        single.update(g2)
        dh, gp['ffn1_norm'][i] = _ffn_bwd(dh, s1, row(P['ffn1_norm'][i]), W, 'ffn1_w_gu', 'ffn1_w_down', i, G,
                                          f"{t}_ffn1")

    small = {n: jnp.concatenate(v, axis=0) for n, v in gp.items()}
    small['conv_a_b'] = single['conv_a_b']
    small['conv_a_ln_g'] = single['conv_a_ln_g']
    small['conv_a_ln_b'] = single['conv_a_ln_b']
    small['swa_sinks'] = single['swa_sinks'][None]
    small['final_norm'] = d_final[0]
    small['conv_a_w'] = single['conv_a_w']
    small['sc_conv_w'] = single['sc_conv_w']
    return loss[0, 0], dh, G, small


def _as2d(a):
    return a.reshape(-1, a.shape[-1])


def kernel(x, mem, ffn1_norm, ffn1_w_gu, ffn1_w_down, mix_norm, even_w_in, conv_a_w, conv_a_b, conv_a_ln_g, conv_a_ln_b, swa_sinks, even_w_out, odd_w_in, sc_conv_w, odd_w_out, xa_norm, xa_mem_norm, xa_wq, xa_wkv, xa_wo, ffn2_norm, ffn2_w_gu, ffn2_w_down, final_norm, loss_target, m_ffn1_norm, m_ffn1_w_gu, m_ffn1_w_down, m_mix_norm, m_even_w_in, m_conv_a_w, m_conv_a_b, m_conv_a_ln_g, m_conv_a_ln_b, m_swa_sinks, m_even_w_out, m_odd_w_in, m_sc_conv_w, m_odd_w_out, m_xa_norm, m_xa_mem_norm, m_xa_wq, m_xa_wkv, m_xa_wo, m_ffn2_norm, m_ffn2_w_gu, m_ffn2_w_down, m_final_norm, v_ffn1_norm, v_ffn1_w_gu, v_ffn1_w_down, v_mix_norm, v_even_w_in, v_conv_a_w, v_conv_a_b, v_conv_a_ln_g, v_conv_a_ln_b, v_swa_sinks, v_even_w_out, v_odd_w_in, v_sc_conv_w, v_odd_w_out, v_xa_norm, v_xa_mem_norm, v_xa_wq, v_xa_wkv, v_xa_wo, v_ffn2_norm, v_ffn2_w_gu, v_ffn2_w_down, v_final_norm):
    w = dict(zip(WEIGHT_NAMES, (ffn1_norm, ffn1_w_gu, ffn1_w_down, mix_norm, even_w_in, conv_a_w, conv_a_b, conv_a_ln_g, conv_a_ln_b, swa_sinks, even_w_out, odd_w_in, sc_conv_w, odd_w_out, xa_norm, xa_mem_norm, xa_wq, xa_wkv, xa_wo, ffn2_norm, ffn2_w_gu, ffn2_w_down, final_norm)))
    m = dict(zip(WEIGHT_NAMES, (m_ffn1_norm, m_ffn1_w_gu, m_ffn1_w_down, m_mix_norm, m_even_w_in, m_conv_a_w, m_conv_a_b, m_conv_a_ln_g, m_conv_a_ln_b, m_swa_sinks, m_even_w_out, m_odd_w_in, m_sc_conv_w, m_odd_w_out, m_xa_norm, m_xa_mem_norm, m_xa_wq, m_xa_wkv, m_xa_wo, m_ffn2_norm, m_ffn2_w_gu, m_ffn2_w_down, m_final_norm)))
    v = dict(zip(WEIGHT_NAMES, (v_ffn1_norm, v_ffn1_w_gu, v_ffn1_w_down, v_mix_norm, v_even_w_in, v_conv_a_w, v_conv_a_b, v_conv_a_ln_g, v_conv_a_ln_b, v_swa_sinks, v_even_w_out, v_odd_w_in, v_sc_conv_w, v_odd_w_out, v_xa_norm, v_xa_mem_norm, v_xa_wq, v_xa_wkv, v_xa_wo, v_ffn2_norm, v_ffn2_w_gu, v_ffn2_w_down, v_final_norm)))
    small_shapes = {n: w[n].shape for n in SMALL_NAMES}
    cx, cy, cc = lax.axis_index("x"), lax.axis_index("y"), lax.axis_index("c")
    chip_idx = (2 * cx + cy).astype(jnp.int32).reshape(1)
    core_idx = cc.astype(jnp.int32).reshape(1)
    chip_core_idx = jnp.concatenate([chip_idx, core_idx])

    shards = {n: w[n] for n in COMM_NAMES if n != 'tiny'}
    shards['tiny'] = _tiny_pack(conv_a_w, sc_conv_w)
    meta = [(LAYOUT[n],) + shards[n].shape for n in COMM_NAMES]
    placed = [_place(shards[n], LAYOUT[n], chip_idx, F32 if n == 'tiny' else BF16, name=f"place_{n}")
              for n in COMM_NAMES]
    gathered = dict(zip(COMM_NAMES, _all_gather(placed, meta)))
    W = {}
    for n in COMM_NAMES:
        a = gathered[n]
        if n == 'tiny':
            continue
        if n == 'even_w_in':
            W[n] = a.transpose(0, 2, 1, 3).reshape(1, D_MODEL, -1)
        else:
            W[n] = a if LAYOUT[n] == 'col' else a.reshape(a.shape[0], N_CHIPS * a.shape[2], a.shape[3])
    ca, sc = _tiny_unpack(gathered['tiny'][0])
    conv_a_full = ca.transpose(1, 0, 2).reshape(CONV_A_WIDTH, CONV_A_CH)
    sc_full = sc.transpose(1, 0, 2).reshape(SC_WIDTH, SC_CH)

    loss_part, grad_x, G, g_small = _local_step(x[0], mem[0], loss_target[0], W, conv_a_full, sc_full,
                                                {n: w[n] for n in SMALL_NAMES})
    loss = lax.psum(loss_part, ("x", "y", "c"))

    gs = []
    for n, (kind, L, A, B) in zip(COMM_NAMES, meta):
        if n == 'tiny':
            g = _tiny_pack(g_small['conv_a_w'].reshape(CONV_A_WIDTH, N_CHIPS, 128).transpose(1, 0, 2),
                           g_small['sc_conv_w'].reshape(SC_WIDTH, N_CHIPS, 256).transpose(1, 0, 2))[None]
        elif n == 'even_w_in':
            g = G[n].reshape(A, N_CHIPS, B).transpose(1, 0, 2)[None]
        elif kind == 'col':
            g = G[n].reshape(L, A, N_CHIPS * B)
        else:
            g = G[n].reshape(L, N_CHIPS, A, B)
        gs.append(g)
    got = _pair_exchange(gs, meta)
    ps = [_pair_sum(g, r, m_, core_idx, name=f"rs_pair_sum_{n}") for n, g, r, m_ in zip(COMM_NAMES, gs, got, meta)]
    got2 = _chip_exchange(ps, meta)
    red = [_chip_sum(p, r, m_, chip_core_idx, name=f"rs_chip_sum_{n}")
           for n, p, r, m_ in zip(COMM_NAMES, ps, got2, meta)]
    g_shards, small_parts = _final_exchange(red, meta, _pack_small(g_small))
    g_local = dict(zip(COMM_NAMES, g_shards))
    g_local['conv_a_w'], g_local['sc_conv_w'] = _tiny_unpack(g_local.pop('tiny'))

    grads, deltas, new_m, new_v = {}, {}, {}, {}
    for n in BIG_NAMES:
        shp = w[n].shape
        d, nm, nv = _adamw(_as2d(w[n]), _as2d(g_local[n]), _as2d(m[n]), _as2d(v[n]), name=f"adamw_{n}")
        grads[n], deltas[n], new_m[n], new_v[n] = g_local[n], d.reshape(shp), nm.reshape(shp), nv.reshape(shp)
    gs, ds, ms, vs = _adamw_small(_pack_small({n: w[n] for n in SMALL_NAMES}), small_parts,
                                  _pack_small({n: m[n] for n in SMALL_NAMES}),
                                  _pack_small({n: v[n] for n in SMALL_NAMES}), name="adamw_small")
    for dst, packed in ((grads, gs), (deltas, ds), (new_m, ms), (new_v, vs)):
        dst.update(_unpack_small(packed, small_shapes))

    return (loss, grad_x[None], *[grads[n] for n in WEIGHT_NAMES], *[deltas[n] for n in WEIGHT_NAMES],
            *[new_m[n] for n in WEIGHT_NAMES], *[new_v[n] for n in WEIGHT_NAMES])
```

```python
import functools
import math

import jax
import jax.numpy as jnp
from jax import lax
from jax.experimental import pallas as pl
from jax.experimental.pallas import tpu as pltpu

F32 = jnp.float32
BF16 = jnp.bfloat16

D_MODEL = 1024
D_FF = 2816
CONV_A_CH = 512
CONV_A_WIDTH = 31
SWA_HEADS = 8
SWA_KV_HEADS = 2
SWA_GROUP = 4
HEAD_DIM = 64
WINDOW = 128
SC_CH = 1024
SC_WIDTH = 3
XA_HEADS = 4
XA_HEAD_DIM = 256
RMS_EPS = 1e-6
LN_EPS = 1e-5

ADAM_LR = 0.001
ADAM_B1 = 0.9
ADAM_B2 = 0.999
ADAM_EPS = 1e-08
ADAM_WD = 0.01
ADAM_STEP = 10

N_CHIPS = 4
N_DEV = 8
NEG_BIG = -1e30
VMEM_LIMIT = 56 * 1024 * 1024
MESH = pl.DeviceIdType.MESH

INPUT_NAMES = ['x', 'mem', 'ffn1_norm', 'ffn1_w_gu', 'ffn1_w_down', 'mix_norm', 'even_w_in', 'conv_a_w', 'conv_a_b',
               'conv_a_ln_g', 'conv_a_ln_b', 'swa_sinks', 'even_w_out', 'odd_w_in', 'sc_conv_w', 'odd_w_out', 'xa_norm',
               'xa_mem_norm', 'xa_wq', 'xa_wkv', 'xa_wo', 'ffn2_norm', 'ffn2_w_gu', 'ffn2_w_down', 'final_norm']
WEIGHT_NAMES = INPUT_NAMES[2:]
BIG = [('ffn1_w_gu', 'col'), ('ffn1_w_down', 'row'), ('even_w_in', 'col'), ('conv_a_w', 'col'), ('even_w_out', 'row'),
       ('odd_w_in', 'col'), ('sc_conv_w', 'col'), ('odd_w_out', 'row'), ('xa_wq', 'row'), ('xa_wkv', 'col'),
       ('xa_wo', 'row'), ('ffn2_w_gu', 'col'), ('ffn2_w_down', 'row')]
BIG_NAMES = [n for n, _ in BIG]
SMALL_NAMES = [n for n in WEIGHT_NAMES if n not in BIG_NAMES]


def _cparams(sem=None, vmem=VMEM_LIMIT):
    kw = dict(vmem_limit_bytes=vmem)
    if sem is not None:
        kw['dimension_semantics'] = sem
    return pltpu.CompilerParams(**kw)


def _div_tile(n, want, align=8):
    if n <= want:
        return n
    t = (want // align) * align
    while t >= align:
        if n % t == 0:
            return t
        t -= align
    return n


def _mm(a, b, *, name, ta=False, tb=False, out_dtype=BF16, tm=512, tn=512, tk=512, res=None, scale=1.0,
        b_layer=None, stack=None, n_map=None):
    n_map = n_map or (lambda j: j)
    if ta:
        K, M = a.shape
    else:
        M, K = a.shape
    if tb:
        N, K2 = b.shape[-2:]
    else:
        K2, N = b.shape[-2:]
    assert K == K2, (a.shape, b.shape, ta, tb)
    tm = _div_tile(M, tm, 128 if ta else 16)
    tn = _div_tile(N, tn, 128)
    tk = _div_tile(K, tk, 16 if ta else 128)
    nk = K // tk
    a_spec = pl.BlockSpec((tk, tm), lambda i, j, k: (k, i)) if ta else pl.BlockSpec((tm, tk), lambda i, j, k: (i, k))
    if b_layer is None:
        b_spec = pl.BlockSpec((tn, tk), lambda i, j, k: (j, k)) if tb else pl.BlockSpec((tk, tn), lambda i, j, k: (k, j))
    elif tb:
        b_spec = pl.BlockSpec((None, tn, tk), lambda i, j, k: (b_layer, j, k))
    else:
        b_spec = pl.BlockSpec((None, tk, tn), lambda i, j, k: (b_layer, k, j))
    o_spec = pl.BlockSpec((tm, tn), lambda i, j, k: (i, j))
    out_shape = jax.ShapeDtypeStruct((M, N), out_dtype)
    out_spec = o_spec
    aliases = {}
    extra_specs, extra_args = [], ()
    if stack is not None:
        n_layers, layer, buf = stack
        out_shape = jax.ShapeDtypeStruct((n_layers, M, N), out_dtype)
        out_spec = pl.BlockSpec((None, tm, tn), lambda i, j, k: (layer, i, n_map(j)))
        if buf is not None:
            extra_specs, extra_args = [pl.BlockSpec(memory_space=pl.ANY)], (buf,)
            aliases = {2 + (res is not None): 0}
    dims = (((0 if ta else 1,), (1 if tb else 0,)), ((), ()))
    has_res = res is not None
    n_extra = len(extra_args)

    def body(*refs):
        if n_extra:
            refs = refs[:2 + has_res] + refs[2 + has_res + n_extra:]
        if has_res:
            a_ref, b_ref, r_ref, o_ref, acc_ref = refs
        else:
            a_ref, b_ref, o_ref, acc_ref = refs
        k = pl.program_id(2)
        p = lax.dot_general(a_ref[...].astype(BF16), b_ref[...].astype(BF16), dims, preferred_element_type=F32)

        @pl.when(k == 0)
        def _():
            acc_ref[...] = p

        @pl.when(k > 0)
        def _():
            acc_ref[...] += p

        @pl.when(k == nk - 1)
        def _():
            r = acc_ref[...] * scale
            if has_res:
                r = r_ref[...] + r
            o_ref[...] = r.astype(out_dtype)

    in_specs = [a_spec, b_spec] + ([o_spec] if has_res else []) + extra_specs
    args = (a, b) + ((res,) if has_res else ()) + extra_args
    return pl.pallas_call(
        body, name=name, grid=(M // tm, N // tn, nk), in_specs=in_specs, out_specs=out_spec,
        out_shape=out_shape, input_output_aliases=aliases,
        scratch_shapes=[pltpu.VMEM((tm, tn), F32)],
        compiler_params=_cparams(("parallel", "parallel", "arbitrary")),
    )(*args)


def _rms_fwd(x, g, *, name):
    S, D = x.shape
    ts = _div_tile(S, 512)

    def body(x_ref, g_ref, o_ref):
        xv = x_ref[...]
        r = lax.rsqrt(jnp.mean(xv * xv, axis=-1, keepdims=True) + RMS_EPS)
        o_ref[...] = (xv * r * g_ref[...]).astype(BF16)

    return pl.pallas_call(
        body, name=name, grid=(S // ts,),
        in_specs=[pl.BlockSpec((ts, D), lambda i: (i, 0)), pl.BlockSpec((1, D), lambda i: (0, 0))],
        out_specs=pl.BlockSpec((ts, D), lambda i: (i, 0)),
        out_shape=jax.ShapeDtypeStruct((S, D), BF16),
        compiler_params=_cparams(("parallel",)),
    )(x, g)


def _norm_mm(h, g, w, layer, *, name, tn):
    S, D = h.shape
    N = w.shape[-1]
    tm = _div_tile(S, 1024, 16)
    tn = _div_tile(N, tn, 128)

    def body(h_ref, g_ref, w_ref, u_ref, z_ref, u_s):
        @pl.when(pl.program_id(1) == 0)
        def _():
            xv = h_ref[...]
            r = lax.rsqrt(jnp.mean(xv * xv, axis=-1, keepdims=True) + RMS_EPS)
            u = (xv * r * g_ref[...]).astype(BF16)
            u_s[...] = u
            u_ref[...] = u

        z_ref[...] = jnp.dot(u_s[...], w_ref[...], preferred_element_type=F32).astype(BF16)

    row = pl.BlockSpec((tm, D), lambda i, j: (i, 0))
    return pl.pallas_call(
        body, name=name, grid=(S // tm, N // tn),
        in_specs=[row, pl.BlockSpec((1, D), lambda i, j: (0, 0)), pl.BlockSpec((None, D, tn), lambda i, j: (layer, 0, j))],
        out_specs=[row, pl.BlockSpec((tm, tn), lambda i, j: (i, j))],
        out_shape=[jax.ShapeDtypeStruct((S, D), BF16), jax.ShapeDtypeStruct((S, N), BF16)],
        scratch_shapes=[pltpu.VMEM((tm, D), BF16)],
        compiler_params=_cparams(("parallel", "arbitrary")),
    )(h, g, w)


def _mm_norm_bwd(dz, w, layer, h, g, dres, *, name, tk):
    S, K = dz.shape
    D = h.shape[1]
    tm = _div_tile(S, 512, 16)
    tk = _div_tile(K, tk, 128)
    nk = K // tk

    def body(dz_ref, w_ref, h_ref, g_ref, dr_ref, dx_ref, dg_ref, acc):
        i = pl.program_id(0)
        k = pl.program_id(1)
        p = lax.dot_general(dz_ref[...], w_ref[...], (((1,), (1,)), ((), ())), preferred_element_type=F32)

        @pl.when(k == 0)
        def _():
            acc[...] = p

        @pl.when(k > 0)
        def _():
            acc[...] += p

        @pl.when(k == nk - 1)
        def _():
            xv = h_ref[...]
            du = acc[...]
            r = lax.rsqrt(jnp.mean(xv * xv, axis=-1, keepdims=True) + RMS_EPS)
            xhat = xv * r
            part = jnp.sum(du * xhat, axis=0, keepdims=True)

            @pl.when(i == 0)
            def _():
                dg_ref[...] = part

            @pl.when(i > 0)
            def _():
                dg_ref[...] += part

            dxhat = du * g_ref[...]
            dx_ref[...] = dr_ref[...] + r * (dxhat - xhat * jnp.mean(dxhat * xhat, axis=-1, keepdims=True))

    row = pl.BlockSpec((tm, D), lambda i, k: (i, 0))
    vec = pl.BlockSpec((1, D), lambda i, k: (0, 0))
    return pl.pallas_call(
        body, name=name, grid=(S // tm, nk),
        in_specs=[pl.BlockSpec((tm, tk), lambda i, k: (i, k)), pl.BlockSpec((None, D, tk), lambda i, k: (layer, 0, k)),
                  row, vec, row],
        out_specs=[row, vec],
        out_shape=[jax.ShapeDtypeStruct((S, D), F32), jax.ShapeDtypeStruct((1, D), F32)],
        scratch_shapes=[pltpu.VMEM((tm, D), F32)],
        compiler_params=_cparams(("arbitrary", "arbitrary")),
    )(dz, w, h, g, dres)


def _rms_bwd(x, g, du, dres, *, name):
    S, D = x.shape
    ts = _div_tile(S, 512)
    has_res = dres is not None

    def body(*refs):
        if has_res:
            x_ref, g_ref, du_ref, dr_ref, dx_ref, dg_ref = refs
        else:
            x_ref, g_ref, du_ref, dg_ref = refs
        i = pl.program_id(0)
        xv = x_ref[...]
        duv = du_ref[...].astype(F32)
        r = lax.rsqrt(jnp.mean(xv * xv, axis=-1, keepdims=True) + RMS_EPS)
        xhat = xv * r
        part = jnp.sum(duv * xhat, axis=0, keepdims=True)

        @pl.when(i == 0)
        def _():
            dg_ref[...] = part

        @pl.when(i > 0)
        def _():
            dg_ref[...] += part

        if has_res:
            dxhat = duv * g_ref[...]
            dx = r * (dxhat - xhat * jnp.mean(dxhat * xhat, axis=-1, keepdims=True))
            dx_ref[...] = dr_ref[...] + dx

    row = pl.BlockSpec((ts, D), lambda i: (i, 0))
    vec = pl.BlockSpec((1, D), lambda i: (0, 0))
    if has_res:
        dx, dg = pl.pallas_call(
            body, name=name, grid=(S // ts,), in_specs=[row, vec, row, row], out_specs=[row, vec],
            out_shape=[jax.ShapeDtypeStruct((S, D), F32), jax.ShapeDtypeStruct((1, D), F32)],
            compiler_params=_cparams(("arbitrary",)),
        )(x, g, du, dres)
        return dx, dg
    dg = pl.pallas_call(
        body, name=name, grid=(S // ts,), in_specs=[row, vec, row], out_specs=vec,
        out_shape=jax.ShapeDtypeStruct((1, D), F32),
        compiler_params=_cparams(("arbitrary",)),
    )(x, g, du)
    return None, dg


def _final_loss(h, g, tgt, *, name):
    S, D = h.shape
    ts = _div_tile(S, 512)

    def body(h_ref, g_ref, t_ref, loss_ref, dh_ref, dg_ref):
        i = pl.program_id(0)
        xv = h_ref[...]
        gv = g_ref[...]
        r = lax.rsqrt(jnp.mean(xv * xv, axis=-1, keepdims=True) + RMS_EPS)
        xhat = xv * r
        err = xhat * gv - t_ref[...]
        lpart = 0.5 * jnp.sum(jnp.mean(err * err, axis=-1, keepdims=True), axis=0, keepdims=True)
        dy = err * (1.0 / D)
        gpart = jnp.sum(dy * xhat, axis=0, keepdims=True)

        @pl.when(i == 0)
        def _():
            loss_ref[...] = jnp.broadcast_to(lpart, loss_ref.shape)
            dg_ref[...] = gpart

        @pl.when(i > 0)
        def _():
            loss_ref[...] += jnp.broadcast_to(lpart, loss_ref.shape)
            dg_ref[...] += gpart

        dxhat = dy * gv
        dh_ref[...] = r * (dxhat - xhat * jnp.mean(dxhat * xhat, axis=-1, keepdims=True))

    row = pl.BlockSpec((ts, D), lambda i: (i, 0))
    vec = pl.BlockSpec((1, D), lambda i: (0, 0))
    return pl.pallas_call(
        body, name=name, grid=(S // ts,), in_specs=[row, vec, row],
        out_specs=[pl.BlockSpec((8, 128), lambda i: (0, 0)), row, vec],
        out_shape=[jax.ShapeDtypeStruct((8, 128), F32), jax.ShapeDtypeStruct((S, D), F32),
                   jax.ShapeDtypeStruct((1, D), F32)],
        compiler_params=_cparams(("arbitrary",)),
    )(h, g, tgt)


def _sigmoid(x):
    return 1.0 / (1.0 + jnp.exp(-x))


FFN_CHUNK = 1408
FFN_CHUNKS = D_FF // FFN_CHUNK


def _gu_col_block(jb):
    return (jb % 2) * FFN_CHUNKS + jb // 2


def _ffn_fwd_fused(h, g, w_gu, w_down, layer, *, name):
    S, D = h.shape
    tm = _div_tile(S, 512, 16)
    tf, nj = FFN_CHUNK, FFN_CHUNKS

    def body(h_ref, g_ref, wg_ref, wu_ref, wd_ref, h2_ref, u_ref, gu_ref, a_ref, u_s, acc):
        j = pl.program_id(1)

        @pl.when(j == 0)
        def _():
            xv = h_ref[...]
            r = lax.rsqrt(jnp.mean(xv * xv, axis=-1, keepdims=True) + RMS_EPS)
            u = (xv * r * g_ref[...]).astype(BF16)
            u_s[...] = u
            u_ref[...] = u

        u = u_s[...]
        gate = jnp.dot(u, wg_ref[...], preferred_element_type=F32)
        up = jnp.dot(u, wu_ref[...], preferred_element_type=F32)
        gu_ref[:, pl.ds(0, tf)] = gate.astype(BF16)
        gu_ref[:, pl.ds(tf, tf)] = up.astype(BF16)
        a = (gate * _sigmoid(gate) * up).astype(BF16)
        a_ref[...] = a
        p = jnp.dot(a, wd_ref[...], preferred_element_type=F32)

        @pl.when(j == 0)
        def _():
            acc[...] = p

        @pl.when(j > 0)
        def _():
            acc[...] += p

        @pl.when(j == nj - 1)
        def _():
            h2_ref[...] = h_ref[...] + 0.5 * acc[...]

    row = pl.BlockSpec((tm, D), lambda i, j: (i, 0))
    return pl.pallas_call(
        body, name=name, grid=(S // tm, nj),
        in_specs=[row, pl.BlockSpec((1, D), lambda i, j: (0, 0)),
                  pl.BlockSpec((None, D, tf), lambda i, j: (layer, 0, j)),
                  pl.BlockSpec((None, D, tf), lambda i, j: (layer, 0, nj + j)),
                  pl.BlockSpec((None, tf, D), lambda i, j: (layer, j, 0))],
        out_specs=[row, row, pl.BlockSpec((tm, 2 * tf), lambda i, j: (i, j)), pl.BlockSpec((tm, tf), lambda i, j: (i, j))],
        out_shape=[jax.ShapeDtypeStruct((S, D), F32), jax.ShapeDtypeStruct((S, D), BF16),
                   jax.ShapeDtypeStruct((S, 2 * D_FF), BF16), jax.ShapeDtypeStruct((S, D_FF), BF16)],
        scratch_shapes=[pltpu.VMEM((tm, D), BF16), pltpu.VMEM((tm, D), F32)],
        compiler_params=_cparams(("parallel", "arbitrary")),
    )(h, g, w_gu, w_gu, w_down)


def _ffn_bwd_fused(dh, h, g, gu, w_gu, w_down, layer, *, name):
    S, D = h.shape
    tm = _div_tile(S, 256, 16)
    tf, nj = FFN_CHUNK, FFN_CHUNKS
    nt = (((1,), (1,)), ((), ()))

    def body(dh_ref, h_ref, g_ref, gu_ref, wg_ref, wu_ref, wd_ref, dx_ref, dg_ref, dgu_ref, dy_s, acc):
        i = pl.program_id(0)
        j = pl.program_id(1)

        @pl.when(j == 0)
        def _():
            dy_s[...] = (0.5 * dh_ref[...]).astype(BF16)

        da = lax.dot_general(dy_s[...], wd_ref[...], nt, preferred_element_type=F32)
        gate = gu_ref[:, pl.ds(0, tf)].astype(F32)
        up = gu_ref[:, pl.ds(tf, tf)].astype(F32)
        sg = _sigmoid(gate)
        dgate = (da * up * sg * (1.0 + gate * (1.0 - sg))).astype(BF16)
        dup = (da * gate * sg).astype(BF16)
        dgu_ref[:, pl.ds(0, tf)] = dgate
        dgu_ref[:, pl.ds(tf, tf)] = dup
        p = (lax.dot_general(dgate, wg_ref[...], nt, preferred_element_type=F32)
             + lax.dot_general(dup, wu_ref[...], nt, preferred_element_type=F32))

        @pl.when(j == 0)
        def _():
            acc[...] = p

        @pl.when(j > 0)
        def _():
            acc[...] += p

        @pl.when(j == nj - 1)
        def _():
            xv = h_ref[...]
            du = acc[...]
            r = lax.rsqrt(jnp.mean(xv * xv, axis=-1, keepdims=True) + RMS_EPS)
            xhat = xv * r
            part = jnp.sum(du * xhat, axis=0, keepdims=True)

            @pl.when(i == 0)
            def _():
                dg_ref[...] = part

            @pl.when(i > 0)
            def _():
                dg_ref[...] += part

            dxhat = du * g_ref[...]
            dx_ref[...] = dh_ref[...] + r * (dxhat - xhat * jnp.mean(dxhat * xhat, axis=-1, keepdims=True))

    row = pl.BlockSpec((tm, D), lambda i, j: (i, 0))
    vec = pl.BlockSpec((1, D), lambda i, j: (0, 0))
    chunk = pl.BlockSpec((tm, 2 * tf), lambda i, j: (i, j))
    return pl.pallas_call(
        body, name=name, grid=(S // tm, nj),
        in_specs=[row, row, vec, chunk,
                  pl.BlockSpec((None, D, tf), lambda i, j: (layer, 0, j)),
                  pl.BlockSpec((None, D, tf), lambda i, j: (layer, 0, nj + j)),
                  pl.BlockSpec((None, tf, D), lambda i, j: (layer, j, 0))],
        out_specs=[row, vec, chunk],
        out_shape=[jax.ShapeDtypeStruct((S, D), F32), jax.ShapeDtypeStruct((1, D), F32),
                   jax.ShapeDtypeStruct((S, 2 * D_FF), BF16)],
        scratch_shapes=[pltpu.VMEM((tm, D), BF16), pltpu.VMEM((tm, D), F32)],
        compiler_params=_cparams(("arbitrary", "arbitrary")),
    )(dh, h, g, gu, w_gu, w_gu, w_down)


CONV_HALO = 32


def _conv_a_fwd(z, w, bias, ln_g, ln_b, *, name):
    S = z.shape[0]
    C = CONV_A_CH
    ts = _div_tile(S, 256, 32)

    def body(val_ref, gate_ref, w_ref, b_ref, g_ref, lb_ref, c_ref, act_ref, win):
        i = pl.program_id(0)

        @pl.when(i == 0)
        def _():
            win[pl.ds(0, CONV_HALO), :] = jnp.zeros((CONV_HALO, C), F32)

        @pl.when(i > 0)
        def _():
            win[pl.ds(0, CONV_HALO), :] = win[pl.ds(ts, CONV_HALO), :]

        a = val_ref[...].astype(F32) * _sigmoid(gate_ref[...].astype(F32))
        win[pl.ds(CONV_HALO, ts), :] = a
        acc = jnp.broadcast_to(b_ref[...], (ts, C))
        for k in range(CONV_A_WIDTH):
            acc = acc + w_ref[pl.ds(k, 1), :] * win[pl.ds(CONV_HALO - (CONV_A_WIDTH - 1) + k, ts), :]
        c_ref[...] = acc
        mu = jnp.mean(acc, axis=-1, keepdims=True)
        xc = acc - mu
        var = jnp.mean(xc * xc, axis=-1, keepdims=True)
        ln = xc * lax.rsqrt(var + LN_EPS) * g_ref[...] + lb_ref[...]
        act_ref[...] = (ln * _sigmoid(ln)).astype(BF16)

    row = lambda col: pl.BlockSpec((ts, C), lambda i, col=col: (i, col))
    vec = pl.BlockSpec((1, C), lambda i: (0, 0))
    return pl.pallas_call(
        body, name=name, grid=(S // ts,),
        in_specs=[row(0), row(1), pl.BlockSpec((32, C), lambda i: (0, 0)), vec, vec, vec],
        out_specs=[row(0), row(0)],
        out_shape=[jax.ShapeDtypeStruct((S, C), F32), jax.ShapeDtypeStruct((S, C), BF16)],
        scratch_shapes=[pltpu.VMEM((ts + CONV_HALO, C), F32)],
        compiler_params=_cparams(("arbitrary",)),
    )(z, z, w, bias, ln_g, ln_b)


def _conv_a_bwd(z, c, dcat, w, ln_g, ln_b, *, name):
    S = z.shape[0]
    C = CONV_A_CH
    ts = _div_tile(S, 256, 32)
    n = S // ts

    def body(val_ref, gate_ref, c_ref, da_ref, w_ref, g_ref, lb_ref, dz_ref, small_ref, win):
        i = pl.program_id(0)

        @pl.when(i == 0)
        def _():
            win[pl.ds(ts, CONV_HALO), :] = jnp.zeros((CONV_HALO, C), F32)
            small_ref[...] = jnp.zeros(small_ref.shape, F32)

        @pl.when(i > 0)
        def _():
            win[pl.ds(ts, CONV_HALO), :] = win[pl.ds(0, CONV_HALO), :]

        cv = c_ref[...]
        gv = g_ref[...]
        mu = jnp.mean(cv, axis=-1, keepdims=True)
        xc = cv - mu
        var = jnp.mean(xc * xc, axis=-1, keepdims=True)
        rstd = lax.rsqrt(var + LN_EPS)
        xhat = xc * rstd
        ln = xhat * gv + lb_ref[...]
        sg = _sigmoid(ln)
        dln = da_ref[...].astype(F32) * (sg * (1.0 + ln * (1.0 - sg)))
        small_ref[pl.ds(33, 1), :] += jnp.sum(dln * xhat, axis=0, keepdims=True)
        small_ref[pl.ds(34, 1), :] += jnp.sum(dln, axis=0, keepdims=True)
        dxhat = dln * gv
        dc = rstd * (dxhat - jnp.mean(dxhat, axis=-1, keepdims=True)
                     - xhat * jnp.mean(dxhat * xhat, axis=-1, keepdims=True))
        small_ref[pl.ds(32, 1), :] += jnp.sum(dc, axis=0, keepdims=True)
        win[pl.ds(0, ts), :] = dc

        val = val_ref[...].astype(F32)
        sgg = _sigmoid(gate_ref[...].astype(F32))
        a = val * sgg
        da = jnp.zeros((ts, C), F32)
        for k in range(CONV_A_WIDTH):
            sh = win[pl.ds(CONV_A_WIDTH - 1 - k, ts), :]
            da = da + w_ref[pl.ds(k, 1), :] * sh
            small_ref[pl.ds(k, 1), :] += jnp.sum(a * sh, axis=0, keepdims=True)
        dz_ref[:, pl.ds(0, C)] = (da * sgg).astype(BF16)
        dz_ref[:, pl.ds(C, C)] = (da * val * sgg * (1.0 - sgg)).astype(BF16)

    row = lambda col: pl.BlockSpec((ts, C), lambda i, col=col: (n - 1 - i, col))
    vec = pl.BlockSpec((1, C), lambda i: (0, 0))
    return pl.pallas_call(
        body, name=name, grid=(n,),
        in_specs=[row(0), row(1), row(0), row(0), pl.BlockSpec((32, C), lambda i: (0, 0)), vec, vec],
        out_specs=[pl.BlockSpec((ts, 2 * C), lambda i: (n - 1 - i, 0)), pl.BlockSpec((40, C), lambda i: (0, 0))],
        out_shape=[jax.ShapeDtypeStruct((S, 2 * C), BF16), jax.ShapeDtypeStruct((40, C), F32)],
        scratch_shapes=[pltpu.VMEM((ts + CONV_HALO, C), F32)],
        compiler_params=_cparams(("arbitrary",)),
    )(z, z, c, dcat, w, ln_g, ln_b)


SC_HALO = 8


def _sconv_fwd(z, w, *, name):
    S = z.shape[0]
    C = SC_CH
    ts = _div_tile(S, 256, 16)

    def body(gb_ref, gc_ref, v_ref, w_ref, y_ref, cc_ref, win):
        i = pl.program_id(0)

        @pl.when(i == 0)
        def _():
            win[pl.ds(0, SC_HALO), :] = jnp.zeros((SC_HALO, C), F32)

        @pl.when(i > 0)
        def _():
            win[pl.ds(0, SC_HALO), :] = win[pl.ds(ts, SC_HALO), :]

        win[pl.ds(SC_HALO, ts), :] = gc_ref[...].astype(F32) * v_ref[...].astype(F32)
        acc = jnp.zeros((ts, C), F32)
        for k in range(SC_WIDTH):
            acc = acc + w_ref[pl.ds(k, 1), :] * win[pl.ds(SC_HALO - (SC_WIDTH - 1) + k, ts), :]
        cc_ref[...] = acc.astype(BF16)
        y_ref[...] = (gb_ref[...].astype(F32) * acc).astype(BF16)

    row = lambda col: pl.BlockSpec((ts, C), lambda i, col=col: (i, col))
    return pl.pallas_call(
        body, name=name, grid=(S // ts,),
        in_specs=[row(0), row(1), row(2), pl.BlockSpec((8, C), lambda i: (0, 0))],
        out_specs=[row(0), row(0)],
        out_shape=[jax.ShapeDtypeStruct((S, C), BF16), jax.ShapeDtypeStruct((S, C), BF16)],
        scratch_shapes=[pltpu.VMEM((ts + SC_HALO, C), F32)],
        compiler_params=_cparams(("arbitrary",)),
    )(z, z, z, w)


def _sconv_bwd(z, cc, dy, w, *, name):
    S = z.shape[0]
    C = SC_CH
    ts = _div_tile(S, 256, 16)
    n = S // ts

    def body(gb_ref, gc_ref, v_ref, cc_ref, dy_ref, w_ref, dz_ref, dw_ref, win):
        i = pl.program_id(0)

        @pl.when(i == 0)
        def _():
            win[pl.ds(ts, SC_HALO), :] = jnp.zeros((SC_HALO, C), F32)
            dw_ref[...] = jnp.zeros(dw_ref.shape, F32)

        @pl.when(i > 0)
        def _():
            win[pl.ds(ts, SC_HALO), :] = win[pl.ds(0, SC_HALO), :]

        dyv = dy_ref[...].astype(F32)
        gb = gb_ref[...].astype(F32)
        gc = gc_ref[...].astype(F32)
        val = v_ref[...].astype(F32)
        dz_ref[:, pl.ds(0, C)] = (dyv * cc_ref[...].astype(F32)).astype(BF16)
        win[pl.ds(0, ts), :] = dyv * gb
        cv = gc * val
        dcv = jnp.zeros((ts, C), F32)
        for k in range(SC_WIDTH):
            sh = win[pl.ds(SC_WIDTH - 1 - k, ts), :]
            dcv = dcv + w_ref[pl.ds(k, 1), :] * sh
            dw_ref[pl.ds(k, 1), :] += jnp.sum(cv * sh, axis=0, keepdims=True)
        dz_ref[:, pl.ds(C, C)] = (dcv * val).astype(BF16)
        dz_ref[:, pl.ds(2 * C, C)] = (dcv * gc).astype(BF16)

    row = lambda col: pl.BlockSpec((ts, C), lambda i, col=col: (n - 1 - i, col))
    return pl.pallas_call(
        body, name=name, grid=(n,),
        in_specs=[row(0), row(1), row(2), row(0), row(0), pl.BlockSpec((8, C), lambda i: (0, 0))],
        out_specs=[pl.BlockSpec((ts, 3 * C), lambda i: (n - 1 - i, 0)), pl.BlockSpec((8, C), lambda i: (0, 0))],
        out_shape=[jax.ShapeDtypeStruct((S, 3 * C), BF16), jax.ShapeDtypeStruct((8, C), F32)],
        scratch_shapes=[pltpu.VMEM((ts + SC_HALO, C), F32)],
        compiler_params=_cparams(("arbitrary",)),
    )(z, z, z, cc, dy, w)


SWA_Q_COL = 2
SWA_SLOPES = [2.0 ** (-8.0 * (h + 1) / SWA_HEADS) for h in range(SWA_HEADS)]
SWA_SCALE = HEAD_DIM ** -0.5


def _swa_masks():
    ii = lax.broadcasted_iota(jnp.int32, (WINDOW, 2 * WINDOW), 0)
    jj = lax.broadcasted_iota(jnp.int32, (WINDOW, 2 * WINDOW), 1)
    dist = ii + WINDOW - jj
    valid = (dist >= 0) & (dist < WINDOW)
    return dist.astype(F32), valid, jj


def _swa_probs(qh, kk, sink, slope, distf, valid):
    s = lax.dot_general(qh, kk, (((1,), (1,)), ((), ())), preferred_element_type=F32) * SWA_SCALE
    s = s - slope * distf
    s = jnp.where(valid, s, NEG_BIG)
    m = jnp.maximum(jnp.max(s, axis=-1, keepdims=True), sink)
    p = jnp.exp(s - m)
    l = jnp.sum(p, axis=-1, keepdims=True) + jnp.exp(sink - m)
    return p, m, l


def _swa_fwd(z, kpad, vpad, sinks, *, name):
    S = z.shape[0]
    tq = _div_tile(S, 256, 128)
    nblk = tq // WINDOW
    W = WINDOW

    def body(sink_ref, q_ref, k_ref, v_ref, o_ref):
        i = pl.program_id(0)
        distf, valid0, jj = _swa_masks()
        for b in range(nblk):
            nb = i * nblk + b
            start = pl.multiple_of(nb * W, W)
            valid = valid0 & ((jj >= W) | (nb > 0))
            for kv in range(SWA_KV_HEADS):
                kk = k_ref[pl.ds(start, 2 * W), pl.ds(HEAD_DIM * kv, HEAD_DIM)]
                vv = v_ref[pl.ds(start, 2 * W), pl.ds(HEAD_DIM * kv, HEAD_DIM)]
                for g in range(SWA_GROUP):
                    h = kv * SWA_GROUP + g
                    qh = q_ref[pl.ds(W * b, W), pl.ds(HEAD_DIM * h, HEAD_DIM)]
                    p, m, l = _swa_probs(qh, kk, sink_ref[h], SWA_SLOPES[h], distf, valid)
                    o = jnp.dot(p.astype(BF16), vv, preferred_element_type=F32) / l
                    o_ref[pl.ds(W * b, W), pl.ds(HEAD_DIM * h, HEAD_DIM)] = o.astype(BF16)

    full = pl.BlockSpec((S + W, 2 * HEAD_DIM), lambda i: (0, 0))
    return pl.pallas_call(
        body, name=name, grid=(S // tq,),
        in_specs=[pl.BlockSpec(memory_space=pltpu.SMEM), pl.BlockSpec((tq, 512), lambda i: (i, SWA_Q_COL)), full, full],
        out_specs=pl.BlockSpec((tq, 512), lambda i: (i, 0)),
        out_shape=jax.ShapeDtypeStruct((S, 512), BF16),
        compiler_params=_cparams(("parallel",)),
    )(sinks, z, kpad, vpad)


def _swa_bwd(z, kpad, vpad, sinks, dcat, *, name):
    S = z.shape[0]
    tq = _div_tile(S, 256, 128)
    nblk = tq // WINDOW
    W = WINDOW

    def body(sink_ref, q_ref, k_ref, v_ref, do_ref, dq_ref, dk_ref, dv_ref, ds_ref):
        i = pl.program_id(0)

        @pl.when(i == 0)
        def _():
            dk_ref[...] = jnp.zeros(dk_ref.shape, F32)
            dv_ref[...] = jnp.zeros(dv_ref.shape, F32)
            ds_ref[...] = jnp.zeros(ds_ref.shape, F32)

        distf, valid0, jj = _swa_masks()
        for b in range(nblk):
            nb = i * nblk + b
            start = pl.multiple_of(nb * W, W)
            valid = valid0 & ((jj >= W) | (nb > 0))
            for kv in range(SWA_KV_HEADS):
                kk = k_ref[pl.ds(start, 2 * W), pl.ds(HEAD_DIM * kv, HEAD_DIM)]
                vv = v_ref[pl.ds(start, 2 * W), pl.ds(HEAD_DIM * kv, HEAD_DIM)]
                dkk = jnp.zeros((2 * W, HEAD_DIM), F32)
                dvv = jnp.zeros((2 * W, HEAD_DIM), F32)
                for g in range(SWA_GROUP):
                    h = kv * SWA_GROUP + g
                    qh = q_ref[pl.ds(W * b, W), pl.ds(HEAD_DIM * h, HEAD_DIM)]
                    doh = do_ref[pl.ds(W * b, W), pl.ds(HEAD_DIM * h, HEAD_DIM)]
                    sink = sink_ref[h]
                    p, m, l = _swa_probs(qh, kk, sink, SWA_SLOPES[h], distf, valid)
                    inv_l = 1.0 / l
                    pn = p * inv_l
                    dp = lax.dot_general(doh, vv, (((1,), (1,)), ((), ())), preferred_element_type=F32)
                    delta = jnp.sum(pn * dp, axis=-1, keepdims=True)
                    dsc = (pn * (dp - delta)).astype(BF16)
                    psink = jnp.exp(sink - m) * inv_l
                    ds_ref[pl.ds(h, 1), :] += jnp.broadcast_to(
                        -jnp.sum(psink * delta, axis=0, keepdims=True), (1, 128))
                    dq = jnp.dot(dsc, kk, preferred_element_type=F32) * SWA_SCALE
                    dq_ref[pl.ds(W * b, W), pl.ds(HEAD_DIM * h, HEAD_DIM)] = dq.astype(BF16)
                    dkk = dkk + lax.dot_general(dsc, qh, (((0,), (0,)), ((), ())),
                                                preferred_element_type=F32) * SWA_SCALE
                    dvv = dvv + lax.dot_general(pn.astype(BF16), doh, (((0,), (0,)), ((), ())),
                                                preferred_element_type=F32)
                dk_ref[pl.ds(start, 2 * W), pl.ds(HEAD_DIM * kv, HEAD_DIM)] += dkk
                dv_ref[pl.ds(start, 2 * W), pl.ds(HEAD_DIM * kv, HEAD_DIM)] += dvv

    full = pl.BlockSpec((S + W, 2 * HEAD_DIM), lambda i: (0, 0))
    return pl.pallas_call(
        body, name=name, grid=(S // tq,),
        in_specs=[pl.BlockSpec(memory_space=pltpu.SMEM), pl.BlockSpec((tq, 512), lambda i: (i, SWA_Q_COL)), full, full,
                  pl.BlockSpec((tq, 512), lambda i: (i, 1))],
        out_specs=[pl.BlockSpec((tq, 512), lambda i: (i, 0)), full, full, pl.BlockSpec((8, 128), lambda i: (0, 0))],
        out_shape=[jax.ShapeDtypeStruct((S, 512), BF16), jax.ShapeDtypeStruct((S + W, 2 * HEAD_DIM), F32),
                   jax.ShapeDtypeStruct((S + W, 2 * HEAD_DIM), F32), jax.ShapeDtypeStruct((8, 128), F32)],
        compiler_params=_cparams(("arbitrary",)),
    )(sinks, z, kpad, vpad, dcat)


XA_SCALE = XA_HEAD_DIM ** -0.5


def _xa_probs(qh, kh):
    s = lax.dot_general(qh, kh, (((1,), (1,)), ((), ())), preferred_element_type=F32) * XA_SCALE
    m = jnp.max(s, axis=-1, keepdims=True)
    p = jnp.exp(s - m)
    return p, jnp.sum(p, axis=-1, keepdims=True)


def _xa_fwd(q, kv, *, name):
    S, D = q.shape
    M = kv.shape[0]
    ts = _div_tile(S, 512, 16)
    HD = XA_HEAD_DIM

    def body(q_ref, k_ref, v_ref, o_ref):
        for h in range(XA_HEADS):
            qh = q_ref[:, pl.ds(HD * h, HD)]
            p, l = _xa_probs(qh, k_ref[:, pl.ds(HD * h, HD)])
            o = jnp.dot(p.astype(BF16), v_ref[:, pl.ds(HD * h, HD)], preferred_element_type=F32) / l
            o_ref[:, pl.ds(HD * h, HD)] = o.astype(BF16)

    return pl.pallas_call(
        body, name=name, grid=(S // ts,),
        in_specs=[pl.BlockSpec((ts, D), lambda i: (i, 0)), pl.BlockSpec((M, D), lambda i: (0, 0)),
                  pl.BlockSpec((M, D), lambda i: (0, 1))],
        out_specs=pl.BlockSpec((ts, D), lambda i: (i, 0)),
        out_shape=jax.ShapeDtypeStruct((S, D), BF16),
        compiler_params=_cparams(("parallel",)),
    )(q, kv, kv)


def _xa_bwd(q, kv, do, *, name):
    S, D = q.shape
    M = kv.shape[0]
    ts = _div_tile(S, 512, 16)
    HD = XA_HEAD_DIM

    def body(q_ref, k_ref, v_ref, do_ref, dq_ref, dkv_ref):
        i = pl.program_id(0)

        @pl.when(i == 0)
        def _():
            dkv_ref[...] = jnp.zeros(dkv_ref.shape, F32)

        for h in range(XA_HEADS):
            qh = q_ref[:, pl.ds(HD * h, HD)]
            kh = k_ref[:, pl.ds(HD * h, HD)]
            vh = v_ref[:, pl.ds(HD * h, HD)]
            doh = do_ref[:, pl.ds(HD * h, HD)]
            p, l = _xa_probs(qh, kh)
            pn = p * (1.0 / l)
            dp = lax.dot_general(doh, vh, (((1,), (1,)), ((), ())), preferred_element_type=F32)
            delta = jnp.sum(pn * dp, axis=-1, keepdims=True)
            dsc = (pn * (dp - delta)).astype(BF16)
            dq_ref[:, pl.ds(HD * h, HD)] = (jnp.dot(dsc, kh, preferred_element_type=F32) * XA_SCALE).astype(BF16)
            dkv_ref[:, pl.ds(HD * h, HD)] += lax.dot_general(
                dsc, qh, (((0,), (0,)), ((), ())), preferred_element_type=F32) * XA_SCALE
            dkv_ref[:, pl.ds(D + HD * h, HD)] += lax.dot_general(
                pn.astype(BF16), doh, (((0,), (0,)), ((), ())), preferred_element_type=F32)

    row = pl.BlockSpec((ts, D), lambda i: (i, 0))
    return pl.pallas_call(
        body, name=name, grid=(S // ts,),
        in_specs=[row, pl.BlockSpec((M, D), lambda i: (0, 0)), pl.BlockSpec((M, D), lambda i: (0, 1)), row],
        out_specs=[row, pl.BlockSpec((M, 2 * D), lambda i: (0, 0))],
        out_shape=[jax.ShapeDtypeStruct((S, D), BF16), jax.ShapeDtypeStruct((M, 2 * D), F32)],
        compiler_params=_cparams(("arbitrary",)),
    )(q, kv, kv, do)


def _adam_math(w, g, m, v):
    m = ADAM_B1 * m + (1.0 - ADAM_B1) * g
    v = ADAM_B2 * v + (1.0 - ADAM_B2) * (g * g)
    m_hat = m / (1.0 - ADAM_B1 ** ADAM_STEP)
    v_hat = v / (1.0 - ADAM_B2 ** ADAM_STEP)
    delta = -ADAM_LR * (m_hat / (jnp.sqrt(v_hat) + ADAM_EPS) + ADAM_WD * w)
    return delta, m, v


def _adamw(w, g, m, v, *, name):
    R, C = w.shape
    tr = _div_tile(R, max(8, (256 * 1024) // C // 8 * 8))

    def body(w_ref, g_ref, m_ref, v_ref, d_ref, nm_ref, nv_ref):
        d, nm, nv = _adam_math(w_ref[...], g_ref[...], m_ref[...], v_ref[...])
        d_ref[...] = d
        nm_ref[...] = nm
        nv_ref[...] = nv

    spec = pl.BlockSpec((tr, C), lambda i: (i, 0))
    sds = jax.ShapeDtypeStruct((R, C), F32)
    return pl.pallas_call(
        body, name=name, grid=(R // tr,), in_specs=[spec] * 4, out_specs=[spec] * 3, out_shape=[sds] * 3,
        compiler_params=_cparams(("parallel",)),
    )(w, g, m, v)


def _adamw_small(w, gparts, m, v, *, name):
    R, C = w.shape

    def body(w_ref, gp_ref, m_ref, v_ref, g_ref, d_ref, nm_ref, nv_ref):
        g = gp_ref[0]
        for k in range(1, N_DEV):
            g = g + gp_ref[k]
        d, nm, nv = _adam_math(w_ref[...], g, m_ref[...], v_ref[...])
        g_ref[...] = g
        d_ref[...] = d
        nm_ref[...] = nm
        nv_ref[...] = nv

    sds = jax.ShapeDtypeStruct((R, C), F32)
    return pl.pallas_call(body, name=name, out_shape=[sds] * 4, compiler_params=_cparams())(w, gparts, m, v)


ANY = pl.BlockSpec(memory_space=pl.ANY)


def _mesh_pos():
    return lax.axis_index("x"), lax.axis_index("y"), lax.axis_index("c")


def _other_chips(x, y):
    return [(1 - x, y), (x, 1 - y), (1 - x, 1 - y)]


LAYOUT = {'ffn1_w_gu': 'col', 'ffn1_w_down': 'stk', 'even_w_in': 'stk', 'even_w_out': 'stk', 'odd_w_in': 'col',
          'odd_w_out': 'stk', 'xa_wq': 'stk', 'xa_wkv': 'col', 'xa_wo': 'stk', 'ffn2_w_gu': 'col',
          'ffn2_w_down': 'stk', 'tiny': 'stk'}
COMM_NAMES = list(LAYOUT)
TINY_ROWS = 48


def _piece_rows(L, A):
    return A if L == 2 else A // 2


def _shard_piece(ref, L, A, h):
    if L == 2:
        return ref.at[h]
    return ref.at[0, pl.ds(pl.multiple_of(h * (A // 2), 8), A // 2)]


def _gathered_piece(ref, kind, L, A, h):
    if L == 2:
        return ref.at[h]
    rows = pl.ds(pl.multiple_of(h * (A // 2), 8), A // 2)
    return ref.at[0, rows] if kind == 'col' else ref.at[0, :, rows]


def _chip_part(piece, kind, B, s):
    if kind == 'col':
        return piece.at[:, pl.ds(pl.multiple_of(s * B, 128), B)]
    return piece.at[s]


def _place(shard, kind, chip_idx, out_dtype, *, name):
    L, A, B = shard.shape
    ta = _div_tile(A, 256, 16)

    def body(s_ref, x_ref, o_ref):
        o_ref[...] = x_ref[...].astype(out_dtype)

    if kind == 'col':
        shape = (L, A, N_CHIPS * B)
        out_spec = pl.BlockSpec((None, ta, B), lambda l, i, s: (l, i, s[0]))
    else:
        shape = (L, N_CHIPS, A, B)
        out_spec = pl.BlockSpec((None, None, ta, B), lambda l, i, s: (l, s[0], i, 0))
    grid_spec = pltpu.PrefetchScalarGridSpec(
        num_scalar_prefetch=1, grid=(L, A // ta),
        in_specs=[pl.BlockSpec((None, ta, B), lambda l, i, s: (l, i, 0))], out_specs=out_spec)
    return pl.pallas_call(
        body, name=name, grid_spec=grid_spec, out_shape=jax.ShapeDtypeStruct(shape, out_dtype),
        compiler_params=_cparams(("parallel", "parallel")),
    )(chip_idx, shard)


def _all_gather(fulls, meta):
    n = len(fulls)

    def body(*refs):
        outs = refs[n:2 * n]
        send_sems, recv_sems = refs[2 * n:]
        x, y, c = _mesh_pos()
        sibling = (x, y, 1 - c)
        chips = _other_chips(x, y)

        def part(k, s, h):
            kind, L, A, B = meta[k]
            return _chip_part(_gathered_piece(outs[k], kind, L, A, h), kind, B, s)

        def copy(ref, sem, to):
            return pltpu.make_async_remote_copy(src_ref=ref, dst_ref=ref, send_sem=send_sems.at[sem],
                                                recv_sem=recv_sems.at[sem], device_id=to, device_id_type=MESH)

        started = []
        for k in range(n):
            for j, (cx, cy) in enumerate(chips):
                cp = copy(part(k, 2 * x + y, c), 3 * k + j, (cx, cy, c))
                cp.start()
                started.append(cp)
        for j, (cx, cy) in enumerate(chips):
            for k in range(n):
                landed = part(k, 2 * cx + cy, c)
                copy(landed, 3 * k + j, (cx, cy, c)).wait_recv()
                fwd = copy(landed, 3 * n + 3 * k + j, sibling)
                fwd.start()
                started.append(fwd)
        for j, (cx, cy) in enumerate(chips):
            for k in range(n):
                copy(part(k, 2 * cx + cy, 1 - c), 3 * n + 3 * k + j, sibling).wait_recv()
        for cp in started:
            cp.wait_send()

    return pl.pallas_call(
        body, name="ag_weights", in_specs=[ANY] * n, out_specs=[ANY] * n,
        out_shape=[jax.ShapeDtypeStruct(f.shape, f.dtype) for f in fulls],
        input_output_aliases={k: k for k in range(n)},
        scratch_shapes=[pltpu.SemaphoreType.DMA((6 * n,)), pltpu.SemaphoreType.DMA((6 * n,))],
    )(*fulls)


def _pair_exchange(gs, meta):
    n = len(gs)

    def body(*refs):
        g_refs, out_refs = refs[:n], refs[n:2 * n]
        send_sems, recv_sems = refs[2 * n:]
        x, y, c = _mesh_pos()
        cps = []
        for k in range(n):
            kind, L, A, B = meta[k]
            cp = pltpu.make_async_remote_copy(
                src_ref=_gathered_piece(g_refs[k], kind, L, A, 1 - c), dst_ref=out_refs[k],
                send_sem=send_sems.at[k], recv_sem=recv_sems.at[k], device_id=(x, y, 1 - c), device_id_type=MESH)
            cp.start()
            cps.append(cp)
        for cp in cps:
            cp.wait()

    shapes = []
    for g, (kind, L, A, B) in zip(gs, meta):
        ap = _piece_rows(L, A)
        shapes.append(jax.ShapeDtypeStruct((ap, N_CHIPS * B) if kind == 'col' else (N_CHIPS, ap, B), g.dtype))
    return pl.pallas_call(
        body, name="rs_pair_exchange", in_specs=[ANY] * n, out_specs=[ANY] * n, out_shape=shapes,
        scratch_shapes=[pltpu.SemaphoreType.DMA((n,)), pltpu.SemaphoreType.DMA((n,))],
    )(*gs)


def _pair_sum(g, got, m, c_idx, *, name):
    kind, L, A, B = m
    ap = _piece_rows(L, A)
    ta = _div_tile(ap, 256, 16)
    nt = ap // ta
    dt = g.dtype

    def body(c_ref, a_ref, b_ref, o_ref):
        o_ref[...] = (a_ref[...].astype(F32) + b_ref[...].astype(F32)).astype(dt)

    if kind == 'col':
        grid = (nt,)
        gmap = (lambda i, c: (c[0], i, 0)) if L == 2 else (lambda i, c: (0, c[0] * nt + i, 0))
        g_spec = pl.BlockSpec((None, ta, N_CHIPS * B), gmap)
        r_spec = pl.BlockSpec((ta, N_CHIPS * B), lambda i, c: (i, 0))
        shape = (ap, N_CHIPS * B)
        sem = ("parallel",)
    else:
        grid = (N_CHIPS, nt)
        gmap = (lambda s, i, c: (c[0], s, i, 0)) if L == 2 else (lambda s, i, c: (0, s, c[0] * nt + i, 0))
        g_spec = pl.BlockSpec((None, None, ta, B), gmap)
        r_spec = pl.BlockSpec((None, ta, B), lambda s, i, c: (s, i, 0))
        shape = (N_CHIPS, ap, B)
        sem = ("parallel", "parallel")
    grid_spec = pltpu.PrefetchScalarGridSpec(num_scalar_prefetch=1, grid=grid, in_specs=[g_spec, r_spec],
                                             out_specs=r_spec)
    return pl.pallas_call(
        body, name=name, grid_spec=grid_spec, out_shape=jax.ShapeDtypeStruct(shape, dt), compiler_params=_cparams(sem),
    )(c_idx, g, got)


def _chip_exchange(ps, meta):
    n = len(ps)

    def body(*refs):
        p_refs, out_refs = refs[:n], refs[n:2 * n]
        send_sems, recv_sems = refs[2 * n:]
        x, y, c = _mesh_pos()
        cps = []
        for k in range(n):
            kind, L, A, B = meta[k]
            for j, (cx, cy) in enumerate(_other_chips(x, y)):
                cp = pltpu.make_async_remote_copy(
                    src_ref=_chip_part(p_refs[k], kind, B, 2 * cx + cy), dst_ref=out_refs[k].at[j],
                    send_sem=send_sems.at[3 * k + j], recv_sem=recv_sems.at[3 * k + j], device_id=(cx, cy, c),
                    device_id_type=MESH)
                cp.start()
                cps.append(cp)
        for cp in cps:
            cp.wait()

    shapes = [jax.ShapeDtypeStruct((3, _piece_rows(L, A), B), p.dtype) for p, (kind, L, A, B) in zip(ps, meta)]
    return pl.pallas_call(
        body, name="rs_chip_exchange", in_specs=[ANY] * n, out_specs=[ANY] * n, out_shape=shapes,
        scratch_shapes=[pltpu.SemaphoreType.DMA((3 * n,)), pltpu.SemaphoreType.DMA((3 * n,))],
    )(*ps)


def _chip_sum(p, got, m, sc_idx, *, name):
    kind, L, A, B = m
    ap = _piece_rows(L, A)
    ta = _div_tile(ap, 256, 16)
    nt = ap // ta

    def body(r_ref, a_ref, b_ref, o_ref):
        acc = a_ref[...].astype(F32)
        for j in range(3):
            acc = acc + b_ref[j].astype(F32)
        o_ref[...] = acc

    if kind == 'col':
        p_spec = pl.BlockSpec((ta, B), lambda i, r: (i, r[0]))
    else:
        p_spec = pl.BlockSpec((None, ta, B), lambda i, r: (r[0], i, 0))
    omap = (lambda i, r: (r[1], i, 0)) if L == 2 else (lambda i, r: (0, r[1] * nt + i, 0))
    grid_spec = pltpu.PrefetchScalarGridSpec(
        num_scalar_prefetch=1, grid=(nt,),
        in_specs=[p_spec, pl.BlockSpec((3, ta, B), lambda i, r: (0, i, 0))],
        out_specs=pl.BlockSpec((None, ta, B), omap))
    return pl.pallas_call(
        body, name=name, grid_spec=grid_spec, out_shape=jax.ShapeDtypeStruct((L, A, B), F32),
        compiler_params=_cparams(("parallel",)),
    )(sc_idx, p, got)


def _final_exchange(gls, meta, small):
    n = len(gls)
    rs, cs = small.shape

    def body(*refs):
        small_ref = refs[n]
        outs = refs[n + 1:2 * n + 1]
        sm_ref = refs[2 * n + 1]
        send_sems, recv_sems, local_sem = refs[2 * n + 2:]
        x, y, c = _mesh_pos()
        me = 4 * x + 2 * y + c
        own_s = pltpu.make_async_copy(small_ref, sm_ref.at[me], local_sem)
        own_s.start()
        cps = []
        for k in range(n):
            kind, L, A, B = meta[k]
            half = _shard_piece(outs[k], L, A, c)
            cp = pltpu.make_async_remote_copy(src_ref=half, dst_ref=half, send_sem=send_sems.at[k],
                                              recv_sem=recv_sems.at[k], device_id=(x, y, 1 - c), device_id_type=MESH)
            cp.start()
            cps.append(cp)
        for r in range(1, N_DEV):
            fx, fy, fc = (r >> 2) & 1, (r >> 1) & 1, r & 1
            peer = (1 - x if fx else x, 1 - y if fy else y, 1 - c if fc else c)
            cp = pltpu.make_async_remote_copy(
                src_ref=small_ref, dst_ref=sm_ref.at[me], send_sem=send_sems.at[n + r], recv_sem=recv_sems.at[n + r],
                device_id=peer, device_id_type=MESH)
            cp.start()
            cps.append(cp)
        for cp in cps:
            cp.wait()
        own_s.wait()

    res = pl.pallas_call(
        body, name="rs_final_exchange", in_specs=[ANY] * (n + 1), out_specs=[ANY] * (n + 1),
        out_shape=[jax.ShapeDtypeStruct(g.shape, g.dtype) for g in gls] + [jax.ShapeDtypeStruct((N_DEV, rs, cs), F32)],
        input_output_aliases={k: k for k in range(n)},
        scratch_shapes=[pltpu.SemaphoreType.DMA((n + N_DEV,)), pltpu.SemaphoreType.DMA((n + N_DEV,)),
                        pltpu.SemaphoreType.DMA],
    )(*gls, small)
    return res[:n], res[n]


def _tiny_pack(conv_a_w, sc_conv_w):
    lead = conv_a_w.shape[:-2]
    sc = sc_conv_w.reshape(lead + (2 * SC_WIDTH, 128))
    z = lambda r: jnp.zeros(lead + (r, 128), F32)
    return jnp.concatenate([conv_a_w, z(32 - CONV_A_WIDTH), sc, z(TINY_ROWS - 32 - 2 * SC_WIDTH)], axis=-2)


def _tiny_unpack(t):
    lead = t.shape[:-2]
    return t[..., :CONV_A_WIDTH, :], t[..., 32:32 + 2 * SC_WIDTH, :].reshape(lead + (SC_WIDTH, 256))


def _pack_small(d):
    flat = jnp.concatenate([d[n].astype(F32).reshape(-1) for n in SMALL_NAMES])
    n = flat.shape[0]
    total = -(-n // 1024) * 1024
    return jnp.pad(flat, (0, total - n)).reshape(total // 128, 128)


def _unpack_small(packed, shapes):
    flat = packed.reshape(-1)
    out, off = {}, 0
    for n in SMALL_NAMES:
        sz = math.prod(shapes[n])
        out[n] = flat[off:off + sz].reshape(shapes[n])
        off += sz
    return out


def _ffn_fwd(h, g, W, n_gu, n_down, i, tag):
    h2, u, gu, a = _ffn_fwd_fused(h, g, W[n_gu], W[n_down], i, name=f"{tag}_fwd")
    return h2, (h, u, gu, a)


def _ffn_bwd(dh, saved, g, W, n_gu, n_down, i, G, tag):
    h, u, gu, a = saved
    dh_in, dg, dgu = _ffn_bwd_fused(dh, h, g, gu, W[n_gu], W[n_down], i, name=f"{tag}_bwd")
    G[n_down] = _mm(a, dh, name=f"{tag}_b_wdown", ta=True, tm=1408, tn=1024, tk=1024, scale=0.5,
                    stack=(2, i, G.get(n_down)))
    G[n_gu] = _mm(u, dgu, name=f"{tag}_b_wgu", ta=True, tm=1024, tn=FFN_CHUNK, tk=2048, stack=(2, i, G.get(n_gu)),
                  n_map=_gu_col_block)
    return dh_in, dg


def _xa_block_fwd(h, mem, g, gm, W, i, tag):
    mn = _rms_fwd(mem, gm, name=f"{tag}_mem_norm")
    u, q = _norm_mm(h, g, W['xa_wq'], i, name=f"{tag}_q", tn=1024)
    kv = _mm(mn, W['xa_wkv'], b_layer=i, name=f"{tag}_kv", tm=256, tn=1024, tk=1024)
    o = _xa_fwd(q, kv, name=f"{tag}_attn")
    h2 = _mm(o, W['xa_wo'], b_layer=i, name=f"{tag}_o", out_dtype=F32, tm=1024, tn=1024, tk=1024, res=h)
    return h2, (h, u, mn, q, kv, o)


def _xa_block_bwd(dh, saved, mem, g, gm, W, i, G, tag):
    h, u, mn, q, kv, o = saved
    do = _mm(dh, W['xa_wo'], b_layer=i, name=f"{tag}_b_do", tb=True, tm=1024, tn=1024, tk=1024)
    G['xa_wo'] = _mm(o, dh, name=f"{tag}_b_wo", ta=True, tm=1024, tn=1024, tk=1024, stack=(2, i, G.get('xa_wo')))
    dq, dkv = _xa_bwd(q, kv, do, name=f"{tag}_b_attn")
    G['xa_wq'] = _mm(u, dq, name=f"{tag}_b_wq", ta=True, tm=1024, tn=1024, tk=1024, stack=(2, i, G.get('xa_wq')))
    dh_in, dg = _mm_norm_bwd(dq, W['xa_wq'], i, h, g, dh, name=f"{tag}_b_du", tk=1024)
    G['xa_wkv'] = _mm(mn, dkv, name=f"{tag}_b_wkv", ta=True, tm=1024, tn=1024, tk=256,
                      stack=(2, i, G.get('xa_wkv')))
    dmn = _mm(dkv, W['xa_wkv'], b_layer=i, name=f"{tag}_b_dmn", tb=True, out_dtype=F32, tm=256, tn=1024, tk=1024)
    _, dgm = _rms_bwd(mem, gm, dmn, None, name=f"{tag}_b_mem_norm")
    return dh_in, dg, dgm


def _pad_conv_w(w, rows):
    return jnp.pad(w.astype(F32), ((0, rows - w.shape[0]), (0, 0)))


def _even_fwd(h, g, W, conv_w, conv_b, ln_g, ln_b, sinks, tag):
    u, z = _norm_mm(h, g, W['even_w_in'], 0, name=f"{tag}_in", tn=1792)
    c, act = _conv_a_fwd(z, conv_w, conv_b, ln_g, ln_b, name=f"{tag}_conv")
    kpad = jnp.pad(z[:, 1536:1664], ((WINDOW, 0), (0, 0)))
    vpad = jnp.pad(z[:, 1664:1792], ((WINDOW, 0), (0, 0)))
    o = _swa_fwd(z, kpad, vpad, sinks, name=f"{tag}_swa")
    cat = jnp.concatenate([act, o], axis=-1)
    h2 = _mm(cat, W['even_w_out'], b_layer=0, name=f"{tag}_out", out_dtype=F32, tm=1024, tn=1024, tk=1024, res=h)
    return h2, (h, u, z, c, kpad, vpad, cat)


def _even_bwd(dh, saved, g, W, conv_w, ln_g, ln_b, sinks, G, tag):
    h, u, z, c, kpad, vpad, cat = saved
    dcat = _mm(dh, W['even_w_out'], b_layer=0, name=f"{tag}_b_dcat", tb=True, tm=1024, tn=1024, tk=1024)
    G['even_w_out'] = _mm(cat, dh, name=f"{tag}_b_wout", ta=True, tm=1024, tn=1024, tk=1024)
    dz_a, small = _conv_a_bwd(z, c, dcat, conv_w, ln_g, ln_b, name=f"{tag}_b_conv")
    dq, dkp, dvp, dsinks = _swa_bwd(z, kpad, vpad, sinks, dcat, name=f"{tag}_b_swa")
    dz = jnp.concatenate([dz_a, dq, dkp[WINDOW:].astype(BF16), dvp[WINDOW:].astype(BF16)], axis=-1)
    G['even_w_in'] = _mm(u, dz, name=f"{tag}_b_win", ta=True, tm=1024, tn=1792, tk=1024)
    dh_in, dg = _mm_norm_bwd(dz, W['even_w_in'], 0, h, g, dh, name=f"{tag}_b_du", tk=1792)
    grads = dict(mix=dg, conv_a_w=small[:CONV_A_WIDTH], conv_a_b=small[32:33], conv_a_ln_g=small[33:34],
                 conv_a_ln_b=small[34:35], swa_sinks=dsinks[:, 0])
    return dh_in, grads


def _odd_fwd(h, g, W, conv_w, tag):
    u, z = _norm_mm(h, g, W['odd_w_in'], 0, name=f"{tag}_in", tn=1024)
    y, cc = _sconv_fwd(z, conv_w, name=f"{tag}_conv")
    h2 = _mm(y, W['odd_w_out'], b_layer=0, name=f"{tag}_out", out_dtype=F32, tm=1024, tn=1024, tk=1024, res=h)
    return h2, (h, u, z, y, cc)


def _odd_bwd(dh, saved, g, W, conv_w, G, tag):
    h, u, z, y, cc = saved
    dy = _mm(dh, W['odd_w_out'], b_layer=0, name=f"{tag}_b_dy", tb=True, tm=1024, tn=1024, tk=1024)
    G['odd_w_out'] = _mm(y, dh, name=f"{tag}_b_wout", ta=True, tm=1024, tn=1024, tk=1024)
    dz, dw = _sconv_bwd(z, cc, dy, conv_w, name=f"{tag}_b_conv")
    G['odd_w_in'] = _mm(u, dz, name=f"{tag}_b_win", ta=True, tm=1024, tn=1024, tk=1024)
    dh_in, dg = _mm_norm_bwd(dz, W['odd_w_in'], 0, h, g, dh, name=f"{tag}_b_du", tk=1024)
    return dh_in, dict(mix=dg, sc_conv_w=dw[:SC_WIDTH])


def _local_step(x, mem, tgt, W, conv_a_w, sc_conv_w, P):
    row = lambda v: v.reshape(1, -1)
    conv_a_w = _pad_conv_w(conv_a_w, 32)
    sc_w = _pad_conv_w(sc_conv_w, 8)
    sinks = P['swa_sinks'][0]

    h = x
    saved = []
    for i in range(2):
        t = f"l{i}"
        h, s1 = _ffn_fwd(h, row(P['ffn1_norm'][i]), W, 'ffn1_w_gu', 'ffn1_w_down', i, f"{t}_ffn1")
        if i == 0:
            h, s2 = _even_fwd(h, row(P['mix_norm'][i]), W, conv_a_w, P['conv_a_b'], P['conv_a_ln_g'],
                              P['conv_a_ln_b'], sinks, f"{t}_even")
        else:
            h, s2 = _odd_fwd(h, row(P['mix_norm'][i]), W, sc_w, f"{t}_odd")
        h, s3 = _xa_block_fwd(h, mem, row(P['xa_norm'][i]), row(P['xa_mem_norm'][i]), W, i, f"{t}_xa")
        h, s4 = _ffn_fwd(h, row(P['ffn2_norm'][i]), W, 'ffn2_w_gu', 'ffn2_w_down', i, f"{t}_ffn2")
        saved.append((s1, s2, s3, s4))

    loss, dh, d_final = _final_loss(h, row(P['final_norm']), tgt, name="final_loss")

    G = {}
    gp = {n: [None, None] for n in ('ffn1_norm', 'mix_norm', 'xa_norm', 'xa_mem_norm', 'ffn2_norm')}
    single = {}
    for i in (1, 0):
        t = f"l{i}"
        s1, s2, s3, s4 = saved[i]
        dh, gp['ffn2_norm'][i] = _ffn_bwd(dh, s4, row(P['ffn2_norm'][i]), W, 'ffn2_w_gu', 'ffn2_w_down', i, G,
                                          f"{t}_ffn2")
        dh, gp['xa_norm'][i], gp['xa_mem_norm'][i] = _xa_block_bwd(
            dh, s3, mem, row(P['xa_norm'][i]), row(P['xa_mem_norm'][i]), W, i, G, f"{t}_xa")
        if i == 0:
            dh, g2 = _even_bwd(dh, s2, row(P['mix_norm'][i]), W, conv_a_w, P['conv_a_ln_g'], P['conv_a_ln_b'], sinks,
                               G, f"{t}_even")
        else:
            dh, g2 = _odd_bwd(dh, s2, row(P['mix_norm'][i]), W, sc_w, G, f"{t}_odd")
        gp['mix_norm'][i] = g2.pop('mix')
        single.update(g2)
        dh, gp['ffn1_norm'][i] = _ffn_bwd(dh, s1, row(P['ffn1_norm'][i]), W, 'ffn1_w_gu', 'ffn1_w_down', i, G,
                                          f"{t}_ffn1")

    small = {n: jnp.concatenate(v, axis=0) for n, v in gp.items()}
    small['conv_a_b'] = single['conv_a_b']
    small['conv_a_ln_g'] = single['conv_a_ln_g']
    small['conv_a_ln_b'] = single['conv_a_ln_b']
    small['swa_sinks'] = single['swa_sinks'][None]
    small['final_norm'] = d_final[0]
    small['conv_a_w'] = single['conv_a_w']
    small['sc_conv_w'] = single['sc_conv_w']
    return loss[0, 0], dh, G, small


def _as2d(a):
    return a.reshape(-1, a.shape[-1])


def kernel(x, mem, ffn1_norm, ffn1_w_gu, ffn1_w_down, mix_norm, even_w_in, conv_a_w, conv_a_b, conv_a_ln_g, conv_a_ln_b, swa_sinks, even_w_out, odd_w_in, sc_conv_w, odd_w_out, xa_norm, xa_mem_norm, xa_wq, xa_wkv, xa_wo, ffn2_norm, ffn2_w_gu, ffn2_w_down, final_norm, loss_target, m_ffn1_norm, m_ffn1_w_gu, m_ffn1_w_down, m_mix_norm, m_even_w_in, m_conv_a_w, m_conv_a_b, m_conv_a_ln_g, m_conv_a_ln_b, m_swa_sinks, m_even_w_out, m_odd_w_in, m_sc_conv_w, m_odd_w_out, m_xa_norm, m_xa_mem_norm, m_xa_wq, m_xa_wkv, m_xa_wo, m_ffn2_norm, m_ffn2_w_gu, m_ffn2_w_down, m_final_norm, v_ffn1_norm, v_ffn1_w_gu, v_ffn1_w_down, v_mix_norm, v_even_w_in, v_conv_a_w, v_conv_a_b, v_conv_a_ln_g, v_conv_a_ln_b, v_swa_sinks, v_even_w_out, v_odd_w_in, v_sc_conv_w, v_odd_w_out, v_xa_norm, v_xa_mem_norm, v_xa_wq, v_xa_wkv, v_xa_wo, v_ffn2_norm, v_ffn2_w_gu, v_ffn2_w_down, v_final_norm):
    w = dict(zip(WEIGHT_NAMES, (ffn1_norm, ffn1_w_gu, ffn1_w_down, mix_norm, even_w_in, conv_a_w, conv_a_b, conv_a_ln_g, conv_a_ln_b, swa_sinks, even_w_out, odd_w_in, sc_conv_w, odd_w_out, xa_norm, xa_mem_norm, xa_wq, xa_wkv, xa_wo, ffn2_norm, ffn2_w_gu, ffn2_w_down, final_norm)))
    m = dict(zip(WEIGHT_NAMES, (m_ffn1_norm, m_ffn1_w_gu, m_ffn1_w_down, m_mix_norm, m_even_w_in, m_conv_a_w, m_conv_a_b, m_conv_a_ln_g, m_conv_a_ln_b, m_swa_sinks, m_even_w_out, m_odd_w_in, m_sc_conv_w, m_odd_w_out, m_xa_norm, m_xa_mem_norm, m_xa_wq, m_xa_wkv, m_xa_wo, m_ffn2_norm, m_ffn2_w_gu, m_ffn2_w_down, m_final_norm)))
    v = dict(zip(WEIGHT_NAMES, (v_ffn1_norm, v_ffn1_w_gu, v_ffn1_w_down, v_mix_norm, v_even_w_in, v_conv_a_w, v_conv_a_b, v_conv_a_ln_g, v_conv_a_ln_b, v_swa_sinks, v_even_w_out, v_odd_w_in, v_sc_conv_w, v_odd_w_out, v_xa_norm, v_xa_mem_norm, v_xa_wq, v_xa_wkv, v_xa_wo, v_ffn2_norm, v_ffn2_w_gu, v_ffn2_w_down, v_final_norm)))
    small_shapes = {n: w[n].shape for n in SMALL_NAMES}
    cx, cy, cc = lax.axis_index("x"), lax.axis_index("y"), lax.axis_index("c")
    chip_idx = (2 * cx + cy).astype(jnp.int32).reshape(1)
    core_idx = cc.astype(jnp.int32).reshape(1)
    chip_core_idx = jnp.concatenate([chip_idx, core_idx])

    shards = {n: w[n] for n in COMM_NAMES if n != 'tiny'}
    shards['tiny'] = _tiny_pack(conv_a_w, sc_conv_w)
    meta = [(LAYOUT[n],) + shards[n].shape for n in COMM_NAMES]
    placed = [_place(shards[n], LAYOUT[n], chip_idx, F32 if n == 'tiny' else BF16, name=f"place_{n}")
              for n in COMM_NAMES]
    gathered = dict(zip(COMM_NAMES, _all_gather(placed, meta)))
    W = {}
    for n in COMM_NAMES:
        a = gathered[n]
        if n == 'tiny':
            continue
        if n == 'even_w_in':
            W[n] = a.transpose(0, 2, 1, 3).reshape(1, D_MODEL, -1)
        else:
            W[n] = a if LAYOUT[n] == 'col' else a.reshape(a.shape[0], N_CHIPS * a.shape[2], a.shape[3])
    ca, sc = _tiny_unpack(gathered['tiny'][0])
    conv_a_full = ca.transpose(1, 0, 2).reshape(CONV_A_WIDTH, CONV_A_CH)
    sc_full = sc.transpose(1, 0, 2).reshape(SC_WIDTH, SC_CH)

    loss_part, grad_x, G, g_small = _local_step(x[0], mem[0], loss_target[0], W, conv_a_full, sc_full,
                                                {n: w[n] for n in SMALL_NAMES})
    loss = lax.psum(loss_part, ("x", "y", "c"))

    gs = []
    for n, (kind, L, A, B) in zip(COMM_NAMES, meta):
        if n == 'tiny':
            g = _tiny_pack(g_small['conv_a_w'].reshape(CONV_A_WIDTH, N_CHIPS, 128).transpose(1, 0, 2),
                           g_small['sc_conv_w'].reshape(SC_WIDTH, N_CHIPS, 256).transpose(1, 0, 2))[None]
        elif n == 'even_w_in':
            g = G[n].reshape(A, N_CHIPS, B).transpose(1, 0, 2)[None]
        elif kind == 'col':
            g = G[n].reshape(L, A, N_CHIPS * B)
        else:
            g = G[n].reshape(L, N_CHIPS, A, B)
        gs.append(g)
    got = _pair_exchange(gs, meta)
    ps = [_pair_sum(g, r, m_, core_idx, name=f"rs_pair_sum_{n}") for n, g, r, m_ in zip(COMM_NAMES, gs, got, meta)]
    got2 = _chip_exchange(ps, meta)
    red = [_chip_sum(p, r, m_, chip_core_idx, name=f"rs_chip_sum_{n}")
           for n, p, r, m_ in zip(COMM_NAMES, ps, got2, meta)]
    g_shards, small_parts = _final_exchange(red, meta, _pack_small(g_small))
    g_local = dict(zip(COMM_NAMES, g_shards))
    g_local['conv_a_w'], g_local['sc_conv_w'] = _tiny_unpack(g_local.pop('tiny'))

    grads, deltas, new_m, new_v = {}, {}, {}, {}
    for n in BIG_NAMES:
        shp = w[n].shape
        d, nm, nv = _adamw(_as2d(w[n]), _as2d(g_local[n]), _as2d(m[n]), _as2d(v[n]), name=f"adamw_{n}")
        grads[n], deltas[n], new_m[n], new_v[n] = g_local[n], d.reshape(shp), nm.reshape(shp), nv.reshape(shp)
    gs, ds, ms, vs = _adamw_small(_pack_small({n: w[n] for n in SMALL_NAMES}), small_parts,
                                  _pack_small({n: m[n] for n in SMALL_NAMES}),
                                  _pack_small({n: v[n] for n in SMALL_NAMES}), name="adamw_small")
    for dst, packed in ((grads, gs), (deltas, ds), (new_m, ms), (new_v, vs)):
        dst.update(_unpack_small(packed, small_shapes))

    return (loss, grad_x[None], *[grads[n] for n in WEIGHT_NAMES], *[deltas[n] for n in WEIGHT_NAMES],
            *[new_m[n] for n in WEIGHT_NAMES], *[new_v[n] for n in WEIGHT_NAMES])
```

```python
import functools
import math

import jax
import jax.numpy as jnp
from jax import lax
from jax.experimental import pallas as pl
from jax.experimental.pallas import tpu as pltpu

F32 = jnp.float32
BF16 = jnp.bfloat16

D_MODEL = 1024
D_FF = 2816
CONV_A_CH = 512
CONV_A_WIDTH = 31
SWA_HEADS = 8
SWA_KV_HEADS = 2
SWA_GROUP = 4
HEAD_DIM = 64
WINDOW = 128
SC_CH = 1024
SC_WIDTH = 3
XA_HEADS = 4
XA_HEAD_DIM = 256
RMS_EPS = 1e-6
LN_EPS = 1e-5

ADAM_LR = 0.001
ADAM_B1 = 0.9
ADAM_B2 = 0.999
ADAM_EPS = 1e-08
ADAM_WD = 0.01
ADAM_STEP = 10

N_CHIPS = 4
N_DEV = 8
NEG_BIG = -1e30
VMEM_LIMIT = 56 * 1024 * 1024
MESH = pl.DeviceIdType.MESH

INPUT_NAMES = ['x', 'mem', 'ffn1_norm', 'ffn1_w_gu', 'ffn1_w_down', 'mix_norm', 'even_w_in', 'conv_a_w', 'conv_a_b',
               'conv_a_ln_g', 'conv_a_ln_b', 'swa_sinks', 'even_w_out', 'odd_w_in', 'sc_conv_w', 'odd_w_out', 'xa_norm',
               'xa_mem_norm', 'xa_wq', 'xa_wkv', 'xa_wo', 'ffn2_norm', 'ffn2_w_gu', 'ffn2_w_down', 'final_norm']
WEIGHT_NAMES = INPUT_NAMES[2:]
BIG = [('ffn1_w_gu', 'col'), ('ffn1_w_down', 'row'), ('even_w_in', 'col'), ('conv_a_w', 'col'), ('even_w_out', 'row'),
       ('odd_w_in', 'col'), ('sc_conv_w', 'col'), ('odd_w_out', 'row'), ('xa_wq', 'row'), ('xa_wkv', 'col'),
       ('xa_wo', 'row'), ('ffn2_w_gu', 'col'), ('ffn2_w_down', 'row')]
BIG_NAMES = [n for n, _ in BIG]
SMALL_NAMES = [n for n in WEIGHT_NAMES if n not in BIG_NAMES]


def _cparams(sem=None, vmem=VMEM_LIMIT):
    kw = dict(vmem_limit_bytes=vmem)
    if sem is not None:
        kw['dimension_semantics'] = sem
    return pltpu.CompilerParams(**kw)


def _div_tile(n, want, align=8):
    if n <= want:
        return n
    t = (want // align) * align
    while t >= align:
        if n % t == 0:
            return t
        t -= align
    return n


def _mm(a, b, *, name, ta=False, tb=False, out_dtype=BF16, tm=512, tn=512, tk=512, res=None, scale=1.0,
        b_layer=None, stack=None, n_map=None):
    n_map = n_map or (lambda j: j)
    if ta:
        K, M = a.shape
    else:
        M, K = a.shape
    if tb:
        N, K2 = b.shape[-2:]
    else:
        K2, N = b.shape[-2:]
    assert K == K2, (a.shape, b.shape, ta, tb)
    tm = _div_tile(M, tm, 128 if ta else 16)
    tn = _div_tile(N, tn, 128)
    tk = _div_tile(K, tk, 16 if ta else 128)
    nk = K // tk
    a_spec = pl.BlockSpec((tk, tm), lambda i, j, k: (k, i)) if ta else pl.BlockSpec((tm, tk), lambda i, j, k: (i, k))
    if b_layer is None:
        b_spec = pl.BlockSpec((tn, tk), lambda i, j, k: (j, k)) if tb else pl.BlockSpec((tk, tn), lambda i, j, k: (k, j))
    elif tb:
        b_spec = pl.BlockSpec((None, tn, tk), lambda i, j, k: (b_layer, j, k))
    else:
        b_spec = pl.BlockSpec((None, tk, tn), lambda i, j, k: (b_layer, k, j))
    o_spec = pl.BlockSpec((tm, tn), lambda i, j, k: (i, j))
    out_shape = jax.ShapeDtypeStruct((M, N), out_dtype)
    out_spec = o_spec
    aliases = {}
    extra_specs, extra_args = [], ()
    if stack is not None:
        n_layers, layer, buf = stack[:3]
        n_total = stack[3] if len(stack) > 3 else N
        out_shape = jax.ShapeDtypeStruct((n_layers, M, n_total), out_dtype)
        out_spec = pl.BlockSpec((None, tm, tn), lambda i, j, k: (layer, i, n_map(j)))
        if buf is not None:
            extra_specs, extra_args = [pl.BlockSpec(memory_space=pl.ANY)], (buf,)
            aliases = {2 + (res is not None): 0}
    dims = (((0 if ta else 1,), (1 if tb else 0,)), ((), ()))
    has_res = res is not None
    n_extra = len(extra_args)

    def body(*refs):
        if n_extra:
            refs = refs[:2 + has_res] + refs[2 + has_res + n_extra:]
        if has_res:
            a_ref, b_ref, r_ref, o_ref, acc_ref = refs
        else:
            a_ref, b_ref, o_ref, acc_ref = refs
        k = pl.program_id(2)
        p = lax.dot_general(a_ref[...].astype(BF16), b_ref[...].astype(BF16), dims, preferred_element_type=F32)

        @pl.when(k == 0)
        def _():
            acc_ref[...] = p

        @pl.when(k > 0)
        def _():
            acc_ref[...] += p

        @pl.when(k == nk - 1)
        def _():
            r = acc_ref[...] * scale
            if has_res:
                r = r_ref[...] + r
            o_ref[...] = r.astype(out_dtype)

    in_specs = [a_spec, b_spec] + ([o_spec] if has_res else []) + extra_specs
    args = (a, b) + ((res,) if has_res else ()) + extra_args
    return pl.pallas_call(
        body, name=name, grid=(M // tm, N // tn, nk), in_specs=in_specs, out_specs=out_spec,
        out_shape=out_shape, input_output_aliases=aliases,
        scratch_shapes=[pltpu.VMEM((tm, tn), F32)],
        compiler_params=_cparams(("parallel", "parallel", "arbitrary")),
    )(*args)


def _rms_fwd(x, g, *, name):
    S, D = x.shape
    ts = _div_tile(S, 512)

    def body(x_ref, g_ref, o_ref):
        xv = x_ref[...]
        r = lax.rsqrt(jnp.mean(xv * xv, axis=-1, keepdims=True) + RMS_EPS)
        o_ref[...] = (xv * r * g_ref[...]).astype(BF16)

    return pl.pallas_call(
        body, name=name, grid=(S // ts,),
        in_specs=[pl.BlockSpec((ts, D), lambda i: (i, 0)), pl.BlockSpec((1, D), lambda i: (0, 0))],
        out_specs=pl.BlockSpec((ts, D), lambda i: (i, 0)),
        out_shape=jax.ShapeDtypeStruct((S, D), BF16),
        compiler_params=_cparams(("parallel",)),
    )(x, g)


def _norm_mm(h, g, w, layer, *, name, tn):
    S, D = h.shape
    N = w.shape[-1]
    tm = _div_tile(S, 1024, 16)
    tn = _div_tile(N, tn, 128)

    def body(h_ref, g_ref, w_ref, u_ref, z_ref, u_s):
        @pl.when(pl.program_id(1) == 0)
        def _():
            xv = h_ref[...]
            r = lax.rsqrt(jnp.mean(xv * xv, axis=-1, keepdims=True) + RMS_EPS)
            u = (xv * r * g_ref[...]).astype(BF16)
            u_s[...] = u
            u_ref[...] = u

        z_ref[...] = jnp.dot(u_s[...], w_ref[...], preferred_element_type=F32).astype(BF16)

    row = pl.BlockSpec((tm, D), lambda i, j: (i, 0))
    return pl.pallas_call(
        body, name=name, grid=(S // tm, N // tn),
        in_specs=[row, pl.BlockSpec((1, D), lambda i, j: (0, 0)), pl.BlockSpec((None, D, tn), lambda i, j: (layer, 0, j))],
        out_specs=[row, pl.BlockSpec((tm, tn), lambda i, j: (i, j))],
        out_shape=[jax.ShapeDtypeStruct((S, D), BF16), jax.ShapeDtypeStruct((S, N), BF16)],
        scratch_shapes=[pltpu.VMEM((tm, D), BF16)],
        compiler_params=_cparams(("parallel", "arbitrary")),
    )(h, g, w)


def _mm_norm_bwd(dz, w, layer, h, g, dres, *, name, tk):
    S, K = dz.shape
    D = h.shape[1]
    tm = _div_tile(S, 512, 16)
    tk = _div_tile(K, tk, 128)
    nk = K // tk

    def body(dz_ref, w_ref, h_ref, g_ref, dr_ref, dx_ref, dg_ref, acc):
        i = pl.program_id(0)
        k = pl.program_id(1)
        p = lax.dot_general(dz_ref[...], w_ref[...], (((1,), (1,)), ((), ())), preferred_element_type=F32)

        @pl.when(k == 0)
        def _():
            acc[...] = p

        @pl.when(k > 0)
        def _():
            acc[...] += p

        @pl.when(k == nk - 1)
        def _():
            xv = h_ref[...]
            du = acc[...]
            r = lax.rsqrt(jnp.mean(xv * xv, axis=-1, keepdims=True) + RMS_EPS)
            xhat = xv * r
            part = jnp.sum(du * xhat, axis=0, keepdims=True)

            @pl.when(i == 0)
            def _():
                dg_ref[...] = part

            @pl.when(i > 0)
            def _():
                dg_ref[...] += part

            dxhat = du * g_ref[...]
            dx_ref[...] = dr_ref[...] + r * (dxhat - xhat * jnp.mean(dxhat * xhat, axis=-1, keepdims=True))

    row = pl.BlockSpec((tm, D), lambda i, k: (i, 0))
    vec = pl.BlockSpec((1, D), lambda i, k: (0, 0))
    return pl.pallas_call(
        body, name=name, grid=(S // tm, nk),
        in_specs=[pl.BlockSpec((tm, tk), lambda i, k: (i, k)), pl.BlockSpec((None, D, tk), lambda i, k: (layer, 0, k)),
                  row, vec, row],
        out_specs=[row, vec],
        out_shape=[jax.ShapeDtypeStruct((S, D), F32), jax.ShapeDtypeStruct((1, D), F32)],
        scratch_shapes=[pltpu.VMEM((tm, D), F32)],
        compiler_params=_cparams(("arbitrary", "arbitrary")),
    )(dz, w, h, g, dres)


def _rms_bwd(x, g, du, dres, *, name):
    S, D = x.shape
    ts = _div_tile(S, 512)
    has_res = dres is not None

    def body(*refs):
        if has_res:
            x_ref, g_ref, du_ref, dr_ref, dx_ref, dg_ref = refs
        else:
            x_ref, g_ref, du_ref, dg_ref = refs
        i = pl.program_id(0)
        xv = x_ref[...]
        duv = du_ref[...].astype(F32)
        r = lax.rsqrt(jnp.mean(xv * xv, axis=-1, keepdims=True) + RMS_EPS)
        xhat = xv * r
        part = jnp.sum(duv * xhat, axis=0, keepdims=True)

        @pl.when(i == 0)
        def _():
            dg_ref[...] = part

        @pl.when(i > 0)
        def _():
            dg_ref[...] += part

        if has_res:
            dxhat = duv * g_ref[...]
            dx = r * (dxhat - xhat * jnp.mean(dxhat * xhat, axis=-1, keepdims=True))
            dx_ref[...] = dr_ref[...] + dx

    row = pl.BlockSpec((ts, D), lambda i: (i, 0))
    vec = pl.BlockSpec((1, D), lambda i: (0, 0))
    if has_res:
        dx, dg = pl.pallas_call(
            body, name=name, grid=(S // ts,), in_specs=[row, vec, row, row], out_specs=[row, vec],
            out_shape=[jax.ShapeDtypeStruct((S, D), F32), jax.ShapeDtypeStruct((1, D), F32)],
            compiler_params=_cparams(("arbitrary",)),
        )(x, g, du, dres)
        return dx, dg
    dg = pl.pallas_call(
        body, name=name, grid=(S // ts,), in_specs=[row, vec, row], out_specs=vec,
        out_shape=jax.ShapeDtypeStruct((1, D), F32),
        compiler_params=_cparams(("arbitrary",)),
    )(x, g, du)
    return None, dg


def _final_loss(h, g, tgt, *, name):
    S, D = h.shape
    ts = _div_tile(S, 512)

    def body(h_ref, g_ref, t_ref, loss_ref, dh_ref, dg_ref):
        i = pl.program_id(0)
        xv = h_ref[...]
        gv = g_ref[...]
        r = lax.rsqrt(jnp.mean(xv * xv, axis=-1, keepdims=True) + RMS_EPS)
        xhat = xv * r
        err = xhat * gv - t_ref[...]
        lpart = 0.5 * jnp.sum(jnp.mean(err * err, axis=-1, keepdims=True), axis=0, keepdims=True)
        dy = err * (1.0 / D)
        gpart = jnp.sum(dy * xhat, axis=0, keepdims=True)

        @pl.when(i == 0)
        def _():
            loss_ref[...] = jnp.broadcast_to(lpart, loss_ref.shape)
            dg_ref[...] = gpart

        @pl.when(i > 0)
        def _():
            loss_ref[...] += jnp.broadcast_to(lpart, loss_ref.shape)
            dg_ref[...] += gpart

        dxhat = dy * gv
        dh_ref[...] = r * (dxhat - xhat * jnp.mean(dxhat * xhat, axis=-1, keepdims=True))

    row = pl.BlockSpec((ts, D), lambda i: (i, 0))
    vec = pl.BlockSpec((1, D), lambda i: (0, 0))
    return pl.pallas_call(
        body, name=name, grid=(S // ts,), in_specs=[row, vec, row],
        out_specs=[pl.BlockSpec((8, 128), lambda i: (0, 0)), row, vec],
        out_shape=[jax.ShapeDtypeStruct((8, 128), F32), jax.ShapeDtypeStruct((S, D), F32),
                   jax.ShapeDtypeStruct((1, D), F32)],
        compiler_params=_cparams(("arbitrary",)),
    )(h, g, tgt)


def _sigmoid(x):
    return 1.0 / (1.0 + jnp.exp(-x))


FFN_CHUNK = 1408
FFN_CHUNKS = D_FF // FFN_CHUNK
FFN_BWD_PIECE = 768
FFN_BWD_SLAB = 256


def _ffn_fwd_fused(h, g, w_gu, w_down, layer, *, name):
    S, D = h.shape
    tm = _div_tile(S, 512, 16)
    tf, nj = FFN_CHUNK, FFN_CHUNKS

    def body(h_ref, g_ref, wg_ref, wu_ref, wd_ref, h2_ref, u_ref, gate_ref, up_ref, a_ref, u_s, acc):
        j = pl.program_id(1)

        @pl.when(j == 0)
        def _():
            xv = h_ref[...]
            r = lax.rsqrt(jnp.mean(xv * xv, axis=-1, keepdims=True) + RMS_EPS)
            u = (xv * r * g_ref[...]).astype(BF16)
            u_s[...] = u
            u_ref[...] = u

        u = u_s[...]
        gate = jnp.dot(u, wg_ref[...], preferred_element_type=F32)
        up = jnp.dot(u, wu_ref[...], preferred_element_type=F32)
        gate_ref[...] = gate.astype(BF16)
        up_ref[...] = up.astype(BF16)
        a = (gate * _sigmoid(gate) * up).astype(BF16)
        a_ref[...] = a
        p = jnp.dot(a, wd_ref[...], preferred_element_type=F32)

        @pl.when(j == 0)
        def _():
            acc[...] = p

        @pl.when(j > 0)
        def _():
            acc[...] += p

        @pl.when(j == nj - 1)
        def _():
            h2_ref[...] = h_ref[...] + 0.5 * acc[...]

    row = pl.BlockSpec((tm, D), lambda i, j: (i, 0))
    chunk = pl.BlockSpec((tm, tf), lambda i, j: (i, j))
    hidden = jax.ShapeDtypeStruct((S, D_FF), BF16)
    return pl.pallas_call(
        body, name=name, grid=(S // tm, nj),
        in_specs=[row, pl.BlockSpec((1, D), lambda i, j: (0, 0)),
                  pl.BlockSpec((None, D, tf), lambda i, j: (layer, 0, j)),
                  pl.BlockSpec((None, D, tf), lambda i, j: (layer, 0, nj + j)),
                  pl.BlockSpec((None, tf, D), lambda i, j: (layer, j, 0))],
        out_specs=[row, row, chunk, chunk, chunk],
        out_shape=[jax.ShapeDtypeStruct((S, D), F32), jax.ShapeDtypeStruct((S, D), BF16), hidden, hidden, hidden],
        scratch_shapes=[pltpu.VMEM((tm, D), BF16), pltpu.VMEM((tm, D), F32)],
        compiler_params=_cparams(("parallel", "arbitrary")),
    )(h, g, w_gu, w_gu, w_down)


def _ffn_bwd_fused(dh, h, g, gate, up, w_gu, w_down, layer, *, name):
    S, D = h.shape
    tm = _div_tile(S, 512, FFN_BWD_SLAB)
    tf = FFN_CHUNK
    nj = D_FF // tf
    slab = min(FFN_BWD_SLAB, tm)
    nt = (((1,), (1,)), ((), ()))
    pieces = [(c0, min(FFN_BWD_PIECE, tf - c0)) for c0 in range(0, tf, FFN_BWD_PIECE)]

    def body(dh_ref, h_ref, g_ref, gate_ref, up_ref, wg_ref, wu_ref, wd_ref, dx_ref, dg_ref, dgate_ref, dup_ref,
             dy_s, acc):
        i = pl.program_id(0)
        j = pl.program_id(1)

        @pl.when(j == 0)
        def _():
            for r0 in range(0, tm, slab):
                rows = pl.ds(r0, slab)
                dy_s[rows, :] = (0.5 * dh_ref[rows, :]).astype(BF16)

        p = None
        for c0, cw in pieces:
            cols = pl.ds(c0, cw)
            da = lax.dot_general(dy_s[...], wd_ref[cols, :], nt, preferred_element_type=F32)
            gt = gate_ref[:, cols].astype(F32)
            sg = _sigmoid(gt)
            dgate = (da * up_ref[:, cols].astype(F32) * sg * (1.0 + gt * (1.0 - sg))).astype(BF16)
            dup = (da * gt * sg).astype(BF16)
            dgate_ref[:, cols] = dgate
            dup_ref[:, cols] = dup
            q = (lax.dot_general(dgate, wg_ref[:, cols], nt, preferred_element_type=F32)
                 + lax.dot_general(dup, wu_ref[:, cols], nt, preferred_element_type=F32))
            p = q if p is None else p + q

        @pl.when(j == 0)
        def _():
            acc[...] = p

        @pl.when(j > 0)
        def _():
            acc[...] += p

        @pl.when(j == nj - 1)
        def _():
            part = jnp.zeros((1, D), F32)
            for r0 in range(0, tm, slab):
                rows = pl.ds(r0, slab)
                xv = h_ref[rows, :]
                du = acc[rows, :]
                r = lax.rsqrt(jnp.mean(xv * xv, axis=-1, keepdims=True) + RMS_EPS)
                xhat = xv * r
                part = part + jnp.sum(du * xhat, axis=0, keepdims=True)
                dxhat = du * g_ref[...]
                dx_ref[rows, :] = dh_ref[rows, :] + r * (
                    dxhat - xhat * jnp.mean(dxhat * xhat, axis=-1, keepdims=True))

            @pl.when(i == 0)
            def _():
                dg_ref[...] = part

            @pl.when(i > 0)
            def _():
                dg_ref[...] += part

    row = pl.BlockSpec((tm, D), lambda i, j: (i, 0))
    vec = pl.BlockSpec((1, D), lambda i, j: (0, 0))
    chunk = pl.BlockSpec((tm, tf), lambda i, j: (i, j))
    hidden = jax.ShapeDtypeStruct((S, D_FF), BF16)
    return pl.pallas_call(
        body, name=name, grid=(S // tm, nj),
        in_specs=[row, row, vec, chunk, chunk,
                  pl.BlockSpec((None, D, tf), lambda i, j: (layer, 0, j)),
                  pl.BlockSpec((None, D, tf), lambda i, j: (layer, 0, nj + j)),
                  pl.BlockSpec((None, tf, D), lambda i, j: (layer, j, 0))],
        out_specs=[row, vec, chunk, chunk],
        out_shape=[jax.ShapeDtypeStruct((S, D), F32), jax.ShapeDtypeStruct((1, D), F32), hidden, hidden],
        scratch_shapes=[pltpu.VMEM((tm, D), BF16), pltpu.VMEM((tm, D), F32)],
        compiler_params=_cparams(("arbitrary", "arbitrary")),
    )(dh, h, g, gate, up, w_gu, w_gu, w_down)


CONV_HALO = 32
CONV_SUB_ROWS = 128


def _conv_a_fwd(z, w, bias, ln_g, ln_b, *, name):
    S = z.shape[0]
    C = CONV_A_CH
    ts = _div_tile(S, 256, 32)

    def body(val_ref, gate_ref, w_ref, b_ref, g_ref, lb_ref, c_ref, act_ref, win):
        i = pl.program_id(0)

        @pl.when(i == 0)
        def _():
            win[pl.ds(0, CONV_HALO), :] = jnp.zeros((CONV_HALO, C), F32)

        @pl.when(i > 0)
        def _():
            win[pl.ds(0, CONV_HALO), :] = win[pl.ds(ts, CONV_HALO), :]

        a = val_ref[...].astype(F32) * _sigmoid(gate_ref[...].astype(F32))
        win[pl.ds(CONV_HALO, ts), :] = a
        rs = min(CONV_SUB_ROWS, ts)
        for cb in range(C // 128):
            lanes = pl.ds(128 * cb, 128)
            for rt in range(ts // rs):
                sub = jnp.broadcast_to(b_ref[:, lanes], (rs, 128))
                for k in range(CONV_A_WIDTH):
                    sub = sub + w_ref[pl.ds(k, 1), lanes] * win[
                        pl.ds(CONV_HALO - (CONV_A_WIDTH - 1) + k + rs * rt, rs), lanes]
                c_ref[pl.ds(rs * rt, rs), lanes] = sub
        acc = c_ref[...]
        mu = jnp.mean(acc, axis=-1, keepdims=True)
        xc = acc - mu
        var = jnp.mean(xc * xc, axis=-1, keepdims=True)
        ln = xc * lax.rsqrt(var + LN_EPS) * g_ref[...] + lb_ref[...]
        act_ref[...] = (ln * _sigmoid(ln)).astype(BF16)

    row = lambda col: pl.BlockSpec((ts, C), lambda i, col=col: (i, col))
    vec = pl.BlockSpec((1, C), lambda i: (0, 0))
    return pl.pallas_call(
        body, name=name, grid=(S // ts,),
        in_specs=[row(0), row(1), pl.BlockSpec((32, C), lambda i: (0, 0)), vec, vec, vec],
        out_specs=[row(0), row(0)],
        out_shape=[jax.ShapeDtypeStruct((S, C), F32), jax.ShapeDtypeStruct((S, C), BF16)],
        scratch_shapes=[pltpu.VMEM((ts + CONV_HALO, C), F32)],
        compiler_params=_cparams(("arbitrary",)),
    )(z, z, w, bias, ln_g, ln_b)


def _conv_a_bwd(z, c, dcat, w, ln_g, ln_b, *, name):
    S = z.shape[0]
    C = CONV_A_CH
    ts = _div_tile(S, 256, 32)
    n = S // ts

    def body(val_ref, gate_ref, c_ref, da_ref, w_ref, g_ref, lb_ref, dz_ref, small_ref, win, a_s, da_s, dw8):
        i = pl.program_id(0)

        @pl.when(i == 0)
        def _():
            win[pl.ds(ts, CONV_HALO), :] = jnp.zeros((CONV_HALO, C), F32)
            small_ref[...] = jnp.zeros(small_ref.shape, F32)
            dw8[...] = jnp.zeros(dw8.shape, F32)

        @pl.when(i > 0)
        def _():
            win[pl.ds(ts, CONV_HALO), :] = win[pl.ds(0, CONV_HALO), :]

        cv = c_ref[...]
        gv = g_ref[...]
        mu = jnp.mean(cv, axis=-1, keepdims=True)
        xc = cv - mu
        var = jnp.mean(xc * xc, axis=-1, keepdims=True)
        rstd = lax.rsqrt(var + LN_EPS)
        xhat = xc * rstd
        ln = xhat * gv + lb_ref[...]
        sg = _sigmoid(ln)
        dln = da_ref[...].astype(F32) * (sg * (1.0 + ln * (1.0 - sg)))
        small_ref[pl.ds(33, 1), :] += jnp.sum(dln * xhat, axis=0, keepdims=True)
        small_ref[pl.ds(34, 1), :] += jnp.sum(dln, axis=0, keepdims=True)
        dxhat = dln * gv
        dc = rstd * (dxhat - jnp.mean(dxhat, axis=-1, keepdims=True)
                     - xhat * jnp.mean(dxhat * xhat, axis=-1, keepdims=True))
        small_ref[pl.ds(32, 1), :] += jnp.sum(dc, axis=0, keepdims=True)
        win[pl.ds(0, ts), :] = dc

        val = val_ref[...].astype(F32)
        sgg = _sigmoid(gate_ref[...].astype(F32))
        a_s[...] = val * sgg
        rs = min(CONV_SUB_ROWS, ts)
        for cb in range(C // 128):
            lanes = pl.ds(128 * cb, 128)
            for rt in range(ts // rs):
                a_sub = a_s[pl.ds(rs * rt, rs), lanes]
                da = jnp.zeros((rs, 128), F32)
                for k in range(CONV_A_WIDTH):
                    sh = win[pl.ds(CONV_A_WIDTH - 1 - k + rs * rt, rs), lanes]
                    da = da + w_ref[pl.ds(k, 1), lanes] * sh
                    prod = a_sub * sh
                    part = prod[0:8]
                    for r in range(1, rs // 8):
                        part = part + prod[8 * r:8 * r + 8]
                    dw8[pl.ds(8 * k, 8), lanes] += part
                da_s[pl.ds(rs * rt, rs), lanes] = da
        da = da_s[...]
        dz_ref[:, pl.ds(0, C)] = (da * sgg).astype(BF16)
        dz_ref[:, pl.ds(C, C)] = (da * val * sgg * (1.0 - sgg)).astype(BF16)

        @pl.when(i == n - 1)
        def _():
            for k in range(CONV_A_WIDTH):
                small_ref[pl.ds(k, 1), :] = jnp.sum(dw8[pl.ds(8 * k, 8), :], axis=0, keepdims=True)

    row = lambda col: pl.BlockSpec((ts, C), lambda i, col=col: (n - 1 - i, col))
    vec = pl.BlockSpec((1, C), lambda i: (0, 0))
    return pl.pallas_call(
        body, name=name, grid=(n,),
        in_specs=[row(0), row(1), row(0), row(0), pl.BlockSpec((32, C), lambda i: (0, 0)), vec, vec],
        out_specs=[pl.BlockSpec((ts, 2 * C), lambda i: (n - 1 - i, 0)), pl.BlockSpec((40, C), lambda i: (0, 0))],
        out_shape=[jax.ShapeDtypeStruct((S, 2 * C), BF16), jax.ShapeDtypeStruct((40, C), F32)],
        scratch_shapes=[pltpu.VMEM((ts + CONV_HALO, C), F32), pltpu.VMEM((ts, C), F32), pltpu.VMEM((ts, C), F32),
                        pltpu.VMEM((8 * 32, C), F32)],
        compiler_params=_cparams(("arbitrary",)),
    )(z, z, c, dcat, w, ln_g, ln_b)


SC_HALO = 8


def _sconv_fwd(z, w, *, name):
    S = z.shape[0]
    C = SC_CH
    ts = _div_tile(S, 256, 16)

    def body(gb_ref, gc_ref, v_ref, w_ref, y_ref, cc_ref, win):
        i = pl.program_id(0)

        @pl.when(i == 0)
        def _():
            win[pl.ds(0, SC_HALO), :] = jnp.zeros((SC_HALO, C), F32)

        @pl.when(i > 0)
        def _():
            win[pl.ds(0, SC_HALO), :] = win[pl.ds(ts, SC_HALO), :]

        win[pl.ds(SC_HALO, ts), :] = gc_ref[...].astype(F32) * v_ref[...].astype(F32)
        acc = jnp.zeros((ts, C), F32)
        for k in range(SC_WIDTH):
            acc = acc + w_ref[pl.ds(k, 1), :] * win[pl.ds(SC_HALO - (SC_WIDTH - 1) + k, ts), :]
        cc_ref[...] = acc.astype(BF16)
        y_ref[...] = (gb_ref[...].astype(F32) * acc).astype(BF16)

    row = lambda col: pl.BlockSpec((ts, C), lambda i, col=col: (i, col))
    return pl.pallas_call(
        body, name=name, grid=(S // ts,),
        in_specs=[row(0), row(1), row(2), pl.BlockSpec((8, C), lambda i: (0, 0))],
        out_specs=[row(0), row(0)],
        out_shape=[jax.ShapeDtypeStruct((S, C), BF16), jax.ShapeDtypeStruct((S, C), BF16)],
        scratch_shapes=[pltpu.VMEM((ts + SC_HALO, C), F32)],
        compiler_params=_cparams(("arbitrary",)),
    )(z, z, z, w)


def _sconv_bwd(z, cc, dy, w, *, name):
    S = z.shape[0]
    C = SC_CH
    ts = _div_tile(S, 256, 16)
    n = S // ts

    def body(gb_ref, gc_ref, v_ref, cc_ref, dy_ref, w_ref, dz_ref, dw_ref, win):
        i = pl.program_id(0)

        @pl.when(i == 0)
        def _():
            win[pl.ds(ts, SC_HALO), :] = jnp.zeros((SC_HALO, C), F32)
            dw_ref[...] = jnp.zeros(dw_ref.shape, F32)

        @pl.when(i > 0)
        def _():
            win[pl.ds(ts, SC_HALO), :] = win[pl.ds(0, SC_HALO), :]

        dyv = dy_ref[...].astype(F32)
        gb = gb_ref[...].astype(F32)
        gc = gc_ref[...].astype(F32)
        val = v_ref[...].astype(F32)
        dz_ref[:, pl.ds(0, C)] = (dyv * cc_ref[...].astype(F32)).astype(BF16)
        win[pl.ds(0, ts), :] = dyv * gb
        cv = gc * val
        dcv = jnp.zeros((ts, C), F32)
        for k in range(SC_WIDTH):
            sh = win[pl.ds(SC_WIDTH - 1 - k, ts), :]
            dcv = dcv + w_ref[pl.ds(k, 1), :] * sh
            dw_ref[pl.ds(k, 1), :] += jnp.sum(cv * sh, axis=0, keepdims=True)
        dz_ref[:, pl.ds(C, C)] = (dcv * val).astype(BF16)
        dz_ref[:, pl.ds(2 * C, C)] = (dcv * gc).astype(BF16)

    row = lambda col: pl.BlockSpec((ts, C), lambda i, col=col: (n - 1 - i, col))
    return pl.pallas_call(
        body, name=name, grid=(n,),
        in_specs=[row(0), row(1), row(2), row(0), row(0), pl.BlockSpec((8, C), lambda i: (0, 0))],
        out_specs=[pl.BlockSpec((ts, 3 * C), lambda i: (n - 1 - i, 0)), pl.BlockSpec((8, C), lambda i: (0, 0))],
        out_shape=[jax.ShapeDtypeStruct((S, 3 * C), BF16), jax.ShapeDtypeStruct((8, C), F32)],
        scratch_shapes=[pltpu.VMEM((ts + SC_HALO, C), F32)],
        compiler_params=_cparams(("arbitrary",)),
    )(z, z, z, cc, dy, w)


SWA_Q_COL = 2
SWA_SLOPES = [2.0 ** (-8.0 * (h + 1) / SWA_HEADS) for h in range(SWA_HEADS)]
SWA_SCALE = HEAD_DIM ** -0.5


SWA_GROUP_ROWS = SWA_GROUP * WINDOW


def _swa_masks():
    shape = (SWA_GROUP_ROWS, 2 * WINDOW)
    ii = lax.broadcasted_iota(jnp.int32, shape, 0)
    jj = lax.broadcasted_iota(jnp.int32, shape, 1)
    dist = (ii & (WINDOW - 1)) + WINDOW - jj
    valid = (dist >= 0) & (dist < WINDOW)
    grp = lax.broadcasted_iota(jnp.int32, (SWA_GROUP_ROWS, 1), 0) // WINDOW
    return dist.astype(F32), valid, jj, grp


def _by_group(grp, vals):
    out = jnp.full(grp.shape, vals[SWA_GROUP - 1], F32)
    for g in range(SWA_GROUP - 2, -1, -1):
        out = jnp.where(grp == g, vals[g], out)
    return out


def _stack_heads(ref, rows, kv):
    return jnp.concatenate([ref[rows, pl.ds(HEAD_DIM * (kv * SWA_GROUP + g), HEAD_DIM)] for g in range(SWA_GROUP)],
                           axis=0)


def _swa_probs(qg, kk, sink, slope, distf, valid):
    s = lax.dot_general(qg, kk, (((1,), (1,)), ((), ())), preferred_element_type=F32) * SWA_SCALE
    s = s - slope * distf
    s = jnp.where(valid, s, NEG_BIG)
    m = jnp.maximum(jnp.max(s, axis=-1, keepdims=True), sink)
    p = jnp.exp(s - m)
    l = jnp.sum(p, axis=-1, keepdims=True) + jnp.exp(sink - m)
    return p, m, l


def _swa_fwd(z, kpad, vpad, sinks, *, name):
    S = z.shape[0]
    tq = _div_tile(S, 256, 128)
    nblk = tq // WINDOW
    W = WINDOW

    def body(sink_ref, q_ref, k_ref, v_ref, o_ref):
        i = pl.program_id(0)
        distf, valid0, jj, grp = _swa_masks()
        for kv in range(SWA_KV_HEADS):
            heads = range(kv * SWA_GROUP, (kv + 1) * SWA_GROUP)
            sink = _by_group(grp, [sink_ref[h] for h in heads])
            slope = _by_group(grp, [SWA_SLOPES[h] for h in heads])
            for b in range(nblk):
                nb = i * nblk + b
                start = pl.multiple_of(nb * W, W)
                rows = pl.ds(W * b, W)
                valid = valid0 & ((jj >= W) | (nb > 0))
                kk = k_ref[pl.ds(start, 2 * W), pl.ds(HEAD_DIM * kv, HEAD_DIM)]
                vv = v_ref[pl.ds(start, 2 * W), pl.ds(HEAD_DIM * kv, HEAD_DIM)]
                p, m, l = _swa_probs(_stack_heads(q_ref, rows, kv), kk, sink, slope, distf, valid)
                o = (jnp.dot(p.astype(BF16), vv, preferred_element_type=F32) / l).astype(BF16)
                for g, h in enumerate(heads):
                    o_ref[rows, pl.ds(HEAD_DIM * h, HEAD_DIM)] = o[W * g:W * (g + 1)]

    full = pl.BlockSpec((S + W, 2 * HEAD_DIM), lambda i: (0, 0))
    return pl.pallas_call(
        body, name=name, grid=(S // tq,),
        in_specs=[pl.BlockSpec(memory_space=pltpu.SMEM), pl.BlockSpec((tq, 512), lambda i: (i, SWA_Q_COL)), full, full],
        out_specs=pl.BlockSpec((tq, 512), lambda i: (i, 0)),
        out_shape=jax.ShapeDtypeStruct((S, 512), BF16),
        compiler_params=_cparams(("parallel",)),
    )(sinks, z, kpad, vpad)


def _swa_bwd(z, kpad, vpad, sinks, dcat, *, name):
    S = z.shape[0]
    tq = _div_tile(S, 256, 128)
    nblk = tq // WINDOW
    W = WINDOW

    def body(sink_ref, q_ref, k_ref, v_ref, do_ref, dq_ref, dk_ref, dv_ref, ds_ref):
        i = pl.program_id(0)

        @pl.when(i == 0)
        def _():
            dk_ref[...] = jnp.zeros(dk_ref.shape, F32)
            dv_ref[...] = jnp.zeros(dv_ref.shape, F32)
            ds_ref[...] = jnp.zeros(ds_ref.shape, F32)

        distf, valid0, jj, grp = _swa_masks()
        tn = (((0,), (0,)), ((), ()))
        for kv in range(SWA_KV_HEADS):
            heads = range(kv * SWA_GROUP, (kv + 1) * SWA_GROUP)
            sink = _by_group(grp, [sink_ref[h] for h in heads])
            slope = _by_group(grp, [SWA_SLOPES[h] for h in heads])
            for b in range(nblk):
                nb = i * nblk + b
                start = pl.multiple_of(nb * W, W)
                rows = pl.ds(W * b, W)
                valid = valid0 & ((jj >= W) | (nb > 0))
                kk = k_ref[pl.ds(start, 2 * W), pl.ds(HEAD_DIM * kv, HEAD_DIM)]
                vv = v_ref[pl.ds(start, 2 * W), pl.ds(HEAD_DIM * kv, HEAD_DIM)]
                qg = _stack_heads(q_ref, rows, kv)
                dog = _stack_heads(do_ref, rows, kv)
                p, m, l = _swa_probs(qg, kk, sink, slope, distf, valid)
                inv_l = 1.0 / l
                pn = p * inv_l
                dp = lax.dot_general(dog, vv, (((1,), (1,)), ((), ())), preferred_element_type=F32)
                delta = jnp.sum(pn * dp, axis=-1, keepdims=True)
                dsc = (pn * (dp - delta)).astype(BF16)
                dsink = jnp.exp(sink - m) * inv_l * delta
                dq = (jnp.dot(dsc, kk, preferred_element_type=F32) * SWA_SCALE).astype(BF16)
                for g, h in enumerate(heads):
                    ds_ref[pl.ds(h, 1), :] += jnp.broadcast_to(
                        -jnp.sum(dsink[W * g:W * (g + 1)], axis=0, keepdims=True), (1, 128))
                    dq_ref[rows, pl.ds(HEAD_DIM * h, HEAD_DIM)] = dq[W * g:W * (g + 1)]
                dk_ref[pl.ds(start, 2 * W), pl.ds(HEAD_DIM * kv, HEAD_DIM)] += lax.dot_general(
                    dsc, qg, tn, preferred_element_type=F32) * SWA_SCALE
                dv_ref[pl.ds(start, 2 * W), pl.ds(HEAD_DIM * kv, HEAD_DIM)] += lax.dot_general(
                    pn.astype(BF16), dog, tn, preferred_element_type=F32)

    full = pl.BlockSpec((S + W, 2 * HEAD_DIM), lambda i: (0, 0))
    return pl.pallas_call(
        body, name=name, grid=(S // tq,),
        in_specs=[pl.BlockSpec(memory_space=pltpu.SMEM), pl.BlockSpec((tq, 512), lambda i: (i, SWA_Q_COL)), full, full,
                  pl.BlockSpec((tq, 512), lambda i: (i, 1))],
        out_specs=[pl.BlockSpec((tq, 512), lambda i: (i, 0)), full, full, pl.BlockSpec((8, 128), lambda i: (0, 0))],
        out_shape=[jax.ShapeDtypeStruct((S, 512), BF16), jax.ShapeDtypeStruct((S + W, 2 * HEAD_DIM), F32),
                   jax.ShapeDtypeStruct((S + W, 2 * HEAD_DIM), F32), jax.ShapeDtypeStruct((8, 128), F32)],
        compiler_params=_cparams(("arbitrary",)),
    )(sinks, z, kpad, vpad, dcat)


XA_SCALE = XA_HEAD_DIM ** -0.5


def _xa_probs(qh, kh):
    s = lax.dot_general(qh, kh, (((1,), (1,)), ((), ())), preferred_element_type=F32) * XA_SCALE
    m = jnp.max(s, axis=-1, keepdims=True)
    p = jnp.exp(s - m)
    return p, jnp.sum(p, axis=-1, keepdims=True)


def _xa_fwd(q, kv, *, name):
    S, D = q.shape
    M = kv.shape[0]
    ts = _div_tile(S, 512, 16)
    HD = XA_HEAD_DIM

    def body(q_ref, k_ref, v_ref, o_ref):
        for h in range(XA_HEADS):
            qh = q_ref[:, pl.ds(HD * h, HD)]
            p, l = _xa_probs(qh, k_ref[:, pl.ds(HD * h, HD)])
            o = jnp.dot(p.astype(BF16), v_ref[:, pl.ds(HD * h, HD)], preferred_element_type=F32) / l
            o_ref[:, pl.ds(HD * h, HD)] = o.astype(BF16)

    return pl.pallas_call(
        body, name=name, grid=(S // ts,),
        in_specs=[pl.BlockSpec((ts, D), lambda i: (i, 0)), pl.BlockSpec((M, D), lambda i: (0, 0)),
                  pl.BlockSpec((M, D), lambda i: (0, 1))],
        out_specs=pl.BlockSpec((ts, D), lambda i: (i, 0)),
        out_shape=jax.ShapeDtypeStruct((S, D), BF16),
        compiler_params=_cparams(("parallel",)),
    )(q, kv, kv)


def _xa_bwd(q, kv, do, *, name):
    S, D = q.shape
    M = kv.shape[0]
    ts = _div_tile(S, 512, 16)
    HD = XA_HEAD_DIM

    def body(q_ref, k_ref, v_ref, do_ref, dq_ref, dkv_ref):
        i = pl.program_id(0)

        @pl.when(i == 0)
        def _():
            dkv_ref[...] = jnp.zeros(dkv_ref.shape, F32)

        for h in range(XA_HEADS):
            qh = q_ref[:, pl.ds(HD * h, HD)]
            kh = k_ref[:, pl.ds(HD * h, HD)]
            vh = v_ref[:, pl.ds(HD * h, HD)]
            doh = do_ref[:, pl.ds(HD * h, HD)]
            p, l = _xa_probs(qh, kh)
            pn = p * (1.0 / l)
            dp = lax.dot_general(doh, vh, (((1,), (1,)), ((), ())), preferred_element_type=F32)
            delta = jnp.sum(pn * dp, axis=-1, keepdims=True)
            dsc = (pn * (dp - delta)).astype(BF16)
            dq_ref[:, pl.ds(HD * h, HD)] = (jnp.dot(dsc, kh, preferred_element_type=F32) * XA_SCALE).astype(BF16)
            dkv_ref[:, pl.ds(HD * h, HD)] += lax.dot_general(
                dsc, qh, (((0,), (0,)), ((), ())), preferred_element_type=F32) * XA_SCALE
            dkv_ref[:, pl.ds(D + HD * h, HD)] += lax.dot_general(
                pn.astype(BF16), doh, (((0,), (0,)), ((), ())), preferred_element_type=F32)

    row = pl.BlockSpec((ts, D), lambda i: (i, 0))
    return pl.pallas_call(
        body, name=name, grid=(S // ts,),
        in_specs=[row, pl.BlockSpec((M, D), lambda i: (0, 0)), pl.BlockSpec((M, D), lambda i: (0, 1)), row],
        out_specs=[row, pl.BlockSpec((M, 2 * D), lambda i: (0, 0))],
        out_shape=[jax.ShapeDtypeStruct((S, D), BF16), jax.ShapeDtypeStruct((M, 2 * D), F32)],
        compiler_params=_cparams(("arbitrary",)),
    )(q, kv, kv, do)


def _adam_math(w, g, m, v):
    m = ADAM_B1 * m + (1.0 - ADAM_B1) * g
    v = ADAM_B2 * v + (1.0 - ADAM_B2) * (g * g)
    m_hat = m / (1.0 - ADAM_B1 ** ADAM_STEP)
    v_hat = v / (1.0 - ADAM_B2 ** ADAM_STEP)
    delta = -ADAM_LR * (m_hat / (jnp.sqrt(v_hat) + ADAM_EPS) + ADAM_WD * w)
    return delta, m, v


def _adamw(w, g, m, v, *, name):
    R, C = w.shape
    tr = _div_tile(R, max(8, (256 * 1024) // C // 8 * 8))

    def body(w_ref, g_ref, m_ref, v_ref, d_ref, nm_ref, nv_ref):
        d, nm, nv = _adam_math(w_ref[...], g_ref[...], m_ref[...], v_ref[...])
        d_ref[...] = d
        nm_ref[...] = nm
        nv_ref[...] = nv

    spec = pl.BlockSpec((tr, C), lambda i: (i, 0))
    sds = jax.ShapeDtypeStruct((R, C), F32)
    return pl.pallas_call(
        body, name=name, grid=(R // tr,), in_specs=[spec] * 4, out_specs=[spec] * 3, out_shape=[sds] * 3,
        compiler_params=_cparams(("parallel",)),
    )(w, g, m, v)


def _adamw_small(w, gparts, m, v, *, name):
    R, C = w.shape

    def body(w_ref, gp_ref, m_ref, v_ref, g_ref, d_ref, nm_ref, nv_ref):
        g = gp_ref[0]
        for k in range(1, N_DEV):
            g = g + gp_ref[k]
        d, nm, nv = _adam_math(w_ref[...], g, m_ref[...], v_ref[...])
        g_ref[...] = g
        d_ref[...] = d
        nm_ref[...] = nm
        nv_ref[...] = nv

    sds = jax.ShapeDtypeStruct((R, C), F32)
    return pl.pallas_call(body, name=name, out_shape=[sds] * 4, compiler_params=_cparams())(w, gparts, m, v)


ANY = pl.BlockSpec(memory_space=pl.ANY)


def _mesh_pos():
    return lax.axis_index("x"), lax.axis_index("y"), lax.axis_index("c")


def _other_chips(x, y):
    return [(1 - x, y), (x, 1 - y), (1 - x, 1 - y)]


LAYOUT = {'ffn1_w_gu': 'col', 'ffn1_w_down': 'stk', 'even_w_in': 'stk', 'even_w_out': 'stk', 'odd_w_in': 'col',
          'odd_w_out': 'stk', 'xa_wq': 'stk', 'xa_wkv': 'col', 'xa_wo': 'stk', 'ffn2_w_gu': 'col',
          'ffn2_w_down': 'stk', 'tiny': 'stk'}
COMM_NAMES = list(LAYOUT)
TINY_ROWS = 48


def _piece_rows(L, A):
    return A if L == 2 else A // 2


def _shard_piece(ref, L, A, h):
    if L == 2:
        return ref.at[h]
    return ref.at[0, pl.ds(pl.multiple_of(h * (A // 2), 8), A // 2)]


def _gathered_piece(ref, kind, L, A, h):
    if L == 2:
        return ref.at[h]
    rows = pl.ds(pl.multiple_of(h * (A // 2), 8), A // 2)
    return ref.at[0, rows] if kind == 'col' else ref.at[0, :, rows]


def _chip_part(piece, kind, B, s):
    if kind == 'col':
        return piece.at[:, pl.ds(pl.multiple_of(s * B, 128), B)]
    return piece.at[s]


def _place(shard, kind, chip_idx, out_dtype, *, name):
    L, A, B = shard.shape
    ta = _div_tile(A, 256, 16)

    def body(s_ref, x_ref, o_ref):
        o_ref[...] = x_ref[...].astype(out_dtype)

    if kind == 'col':
        shape = (L, A, N_CHIPS * B)
        out_spec = pl.BlockSpec((None, ta, B), lambda l, i, s: (l, i, s[0]))
    else:
        shape = (L, N_CHIPS, A, B)
        out_spec = pl.BlockSpec((None, None, ta, B), lambda l, i, s: (l, s[0], i, 0))
    grid_spec = pltpu.PrefetchScalarGridSpec(
        num_scalar_prefetch=1, grid=(L, A // ta),
        in_specs=[pl.BlockSpec((None, ta, B), lambda l, i, s: (l, i, 0))], out_specs=out_spec)
    return pl.pallas_call(
        body, name=name, grid_spec=grid_spec, out_shape=jax.ShapeDtypeStruct(shape, out_dtype),
        compiler_params=_cparams(("parallel", "parallel")),
    )(chip_idx, shard)


def _all_gather(fulls, meta):
    n = len(fulls)

    def body(*refs):
        outs = refs[n:2 * n]
        send_sems, recv_sems = refs[2 * n:]
        x, y, c = _mesh_pos()
        sibling = (x, y, 1 - c)
        chips = _other_chips(x, y)

        def part(k, s, h):
            kind, L, A, B = meta[k]
            return _chip_part(_gathered_piece(outs[k], kind, L, A, h), kind, B, s)

        def copy(ref, sem, to):
            return pltpu.make_async_remote_copy(src_ref=ref, dst_ref=ref, send_sem=send_sems.at[sem],
                                                recv_sem=recv_sems.at[sem], device_id=to, device_id_type=MESH)

        started = []
        for k in range(n):
            for j, (cx, cy) in enumerate(chips):
                cp = copy(part(k, 2 * x + y, c), 3 * k + j, (cx, cy, c))
                cp.start()
                started.append(cp)
        for j, (cx, cy) in enumerate(chips):
            for k in range(n):
                landed = part(k, 2 * cx + cy, c)
                copy(landed, 3 * k + j, (cx, cy, c)).wait_recv()
                fwd = copy(landed, 3 * n + 3 * k + j, sibling)
                fwd.start()
                started.append(fwd)
        for j, (cx, cy) in enumerate(chips):
            for k in range(n):
                copy(part(k, 2 * cx + cy, 1 - c), 3 * n + 3 * k + j, sibling).wait_recv()
        for cp in started:
            cp.wait_send()

    return pl.pallas_call(
        body, name="ag_weights", in_specs=[ANY] * n, out_specs=[ANY] * n,
        out_shape=[jax.ShapeDtypeStruct(f.shape, f.dtype) for f in fulls],
        input_output_aliases={k: k for k in range(n)},
        scratch_shapes=[pltpu.SemaphoreType.DMA((6 * n,)), pltpu.SemaphoreType.DMA((6 * n,))],
    )(*fulls)


def _pair_exchange(gs, meta):
    n = len(gs)

    def body(*refs):
        g_refs, out_refs = refs[:n], refs[n:2 * n]
        send_sems, recv_sems = refs[2 * n:]
        x, y, c = _mesh_pos()
        cps = []
        for k in range(n):
            kind, L, A, B = meta[k]
            cp = pltpu.make_async_remote_copy(
                src_ref=_gathered_piece(g_refs[k], kind, L, A, 1 - c), dst_ref=out_refs[k],
                send_sem=send_sems.at[k], recv_sem=recv_sems.at[k], device_id=(x, y, 1 - c), device_id_type=MESH)
            cp.start()
            cps.append(cp)
        for cp in cps:
            cp.wait()

    shapes = []
    for g, (kind, L, A, B) in zip(gs, meta):
        ap = _piece_rows(L, A)
        shapes.append(jax.ShapeDtypeStruct((ap, N_CHIPS * B) if kind == 'col' else (N_CHIPS, ap, B), g.dtype))
    return pl.pallas_call(
        body, name="rs_pair_exchange", in_specs=[ANY] * n, out_specs=[ANY] * n, out_shape=shapes,
        scratch_shapes=[pltpu.SemaphoreType.DMA((n,)), pltpu.SemaphoreType.DMA((n,))],
    )(*gs)


def _pair_sum(g, got, m, c_idx, *, name):
    kind, L, A, B = m
    ap = _piece_rows(L, A)
    ta = _div_tile(ap, 256, 16)
    nt = ap // ta
    dt = g.dtype

    def body(c_ref, a_ref, b_ref, o_ref):
        o_ref[...] = (a_ref[...].astype(F32) + b_ref[...].astype(F32)).astype(dt)

    if kind == 'col':
        grid = (nt,)
        gmap = (lambda i, c: (c[0], i, 0)) if L == 2 else (lambda i, c: (0, c[0] * nt + i, 0))
        g_spec = pl.BlockSpec((None, ta, N_CHIPS * B), gmap)
        r_spec = pl.BlockSpec((ta, N_CHIPS * B), lambda i, c: (i, 0))
        shape = (ap, N_CHIPS * B)
        sem = ("parallel",)
    else:
        grid = (N_CHIPS, nt)
        gmap = (lambda s, i, c: (c[0], s, i, 0)) if L == 2 else (lambda s, i, c: (0, s, c[0] * nt + i, 0))
        g_spec = pl.BlockSpec((None, None, ta, B), gmap)
        r_spec = pl.BlockSpec((None, ta, B), lambda s, i, c: (s, i, 0))
        shape = (N_CHIPS, ap, B)
        sem = ("parallel", "parallel")
    grid_spec = pltpu.PrefetchScalarGridSpec(num_scalar_prefetch=1, grid=grid, in_specs=[g_spec, r_spec],
                                             out_specs=r_spec)
    return pl.pallas_call(
        body, name=name, grid_spec=grid_spec, out_shape=jax.ShapeDtypeStruct(shape, dt), compiler_params=_cparams(sem),
    )(c_idx, g, got)


def _chip_exchange(ps, meta):
    n = len(ps)

    def body(*refs):
        p_refs, out_refs = refs[:n], refs[n:2 * n]
        send_sems, recv_sems = refs[2 * n:]
        x, y, c = _mesh_pos()
        cps = []
        for k in range(n):
            kind, L, A, B = meta[k]
            for j, (cx, cy) in enumerate(_other_chips(x, y)):
                cp = pltpu.make_async_remote_copy(
                    src_ref=_chip_part(p_refs[k], kind, B, 2 * cx + cy), dst_ref=out_refs[k].at[j],
                    send_sem=send_sems.at[3 * k + j], recv_sem=recv_sems.at[3 * k + j], device_id=(cx, cy, c),
                    device_id_type=MESH)
                cp.start()
                cps.append(cp)
        for cp in cps:
            cp.wait()

    shapes = [jax.ShapeDtypeStruct((3, _piece_rows(L, A), B), p.dtype) for p, (kind, L, A, B) in zip(ps, meta)]
    return pl.pallas_call(
        body, name="rs_chip_exchange", in_specs=[ANY] * n, out_specs=[ANY] * n, out_shape=shapes,
        scratch_shapes=[pltpu.SemaphoreType.DMA((3 * n,)), pltpu.SemaphoreType.DMA((3 * n,))],
    )(*ps)


def _chip_sum(p, got, m, sc_idx, *, name):
    kind, L, A, B = m
    ap = _piece_rows(L, A)
    ta = _div_tile(ap, 256, 16)
    nt = ap // ta

    def body(r_ref, a_ref, b_ref, o_ref):
        acc = a_ref[...].astype(F32)
        for j in range(3):
            acc = acc + b_ref[j].astype(F32)
        o_ref[...] = acc

    if kind == 'col':
        p_spec = pl.BlockSpec((ta, B), lambda i, r: (i, r[0]))
    else:
        p_spec = pl.BlockSpec((None, ta, B), lambda i, r: (r[0], i, 0))
    omap = (lambda i, r: (r[1], i, 0)) if L == 2 else (lambda i, r: (0, r[1] * nt + i, 0))
    grid_spec = pltpu.PrefetchScalarGridSpec(
        num_scalar_prefetch=1, grid=(nt,),
        in_specs=[p_spec, pl.BlockSpec((3, ta, B), lambda i, r: (0, i, 0))],
        out_specs=pl.BlockSpec((None, ta, B), omap))
    return pl.pallas_call(
        body, name=name, grid_spec=grid_spec, out_shape=jax.ShapeDtypeStruct((L, A, B), F32),
        compiler_params=_cparams(("parallel",)),
    )(sc_idx, p, got)


def _final_exchange(gls, meta, small):
    n = len(gls)
    rs, cs = small.shape

    def body(*refs):
        small_ref = refs[n]
        outs = refs[n + 1:2 * n + 1]
        sm_ref = refs[2 * n + 1]
        send_sems, recv_sems, local_sem = refs[2 * n + 2:]
        x, y, c = _mesh_pos()
        me = 4 * x + 2 * y + c
        own_s = pltpu.make_async_copy(small_ref, sm_ref.at[me], local_sem)
        own_s.start()
        cps = []
        for k in range(n):
            kind, L, A, B = meta[k]
            half = _shard_piece(outs[k], L, A, c)
            cp = pltpu.make_async_remote_copy(src_ref=half, dst_ref=half, send_sem=send_sems.at[k],
                                              recv_sem=recv_sems.at[k], device_id=(x, y, 1 - c), device_id_type=MESH)
            cp.start()
            cps.append(cp)
        for r in range(1, N_DEV):
            fx, fy, fc = (r >> 2) & 1, (r >> 1) & 1, r & 1
            peer = (1 - x if fx else x, 1 - y if fy else y, 1 - c if fc else c)
            cp = pltpu.make_async_remote_copy(
                src_ref=small_ref, dst_ref=sm_ref.at[me], send_sem=send_sems.at[n + r], recv_sem=recv_sems.at[n + r],
                device_id=peer, device_id_type=MESH)
            cp.start()
            cps.append(cp)
        for cp in cps:
            cp.wait()
        own_s.wait()

    res = pl.pallas_call(
        body, name="rs_final_exchange", in_specs=[ANY] * (n + 1), out_specs=[ANY] * (n + 1),
        out_shape=[jax.ShapeDtypeStruct(g.shape, g.dtype) for g in gls] + [jax.ShapeDtypeStruct((N_DEV, rs, cs), F32)],
        input_output_aliases={k: k for k in range(n)},
        scratch_shapes=[pltpu.SemaphoreType.DMA((n + N_DEV,)), pltpu.SemaphoreType.DMA((n + N_DEV,)),
                        pltpu.SemaphoreType.DMA],
    )(*gls, small)
    return res[:n], res[n]


def _tiny_pack(conv_a_w, sc_conv_w):
    lead = conv_a_w.shape[:-2]
    sc = sc_conv_w.reshape(lead + (2 * SC_WIDTH, 128))
    z = lambda r: jnp.zeros(lead + (r, 128), F32)
    return jnp.concatenate([conv_a_w, z(32 - CONV_A_WIDTH), sc, z(TINY_ROWS - 32 - 2 * SC_WIDTH)], axis=-2)


def _tiny_unpack(t):
    lead = t.shape[:-2]
    return t[..., :CONV_A_WIDTH, :], t[..., 32:32 + 2 * SC_WIDTH, :].reshape(lead + (SC_WIDTH, 256))


def _pack_small(d):
    flat = jnp.concatenate([d[n].astype(F32).reshape(-1) for n in SMALL_NAMES])
    n = flat.shape[0]
    total = -(-n // 1024) * 1024
    return jnp.pad(flat, (0, total - n)).reshape(total // 128, 128)


def _unpack_small(packed, shapes):
    flat = packed.reshape(-1)
    out, off = {}, 0
    for n in SMALL_NAMES:
        sz = math.prod(shapes[n])
        out[n] = flat[off:off + sz].reshape(shapes[n])
        off += sz
    return out


def _ffn_fwd(h, g, W, n_gu, n_down, i, tag):
    h2, u, gate, up, a = _ffn_fwd_fused(h, g, W[n_gu], W[n_down], i, name=f"{tag}_fwd")
    return h2, (h, u, gate, up, a)


def _ffn_bwd(dh, saved, g, W, n_gu, n_down, i, G, tag):
    h, u, gate, up, a = saved
    dh_in, dg, dgate, dup = _ffn_bwd_fused(dh, h, g, gate, up, W[n_gu], W[n_down], i, name=f"{tag}_bwd")
    G[n_down] = _mm(a, dh, name=f"{tag}_b_wdown", ta=True, tm=1408, tn=1024, tk=1024, scale=0.5,
                    stack=(2, i, G.get(n_down)))
    tn = FFN_CHUNK
    half = _mm(u, dgate, name=f"{tag}_b_wg", ta=True, tm=1024, tn=tn, tk=2048, stack=(2, i, G.get(n_gu), 2 * D_FF))
    G[n_gu] = _mm(u, dup, name=f"{tag}_b_wu", ta=True, tm=1024, tn=tn, tk=2048, stack=(2, i, half, 2 * D_FF),
                  n_map=lambda j: j + D_FF // tn)
    return dh_in, dg


def _xa_block_fwd(h, mem, g, gm, W, i, tag):
    mn = _rms_fwd(mem, gm, name=f"{tag}_mem_norm")
    u, q = _norm_mm(h, g, W['xa_wq'], i, name=f"{tag}_q", tn=1024)
    kv = _mm(mn, W['xa_wkv'], b_layer=i, name=f"{tag}_kv", tm=256, tn=1024, tk=1024)
    o = _xa_fwd(q, kv, name=f"{tag}_attn")
    h2 = _mm(o, W['xa_wo'], b_layer=i, name=f"{tag}_o", out_dtype=F32, tm=1024, tn=1024, tk=1024, res=h)
    return h2, (h, u, mn, q, kv, o)


def _xa_block_bwd(dh, saved, mem, g, gm, W, i, G, tag):
    h, u, mn, q, kv, o = saved
    do = _mm(dh, W['xa_wo'], b_layer=i, name=f"{tag}_b_do", tb=True, tm=1024, tn=1024, tk=1024)
    G['xa_wo'] = _mm(o, dh, name=f"{tag}_b_wo", ta=True, tm=1024, tn=1024, tk=1024, stack=(2, i, G.get('xa_wo')))
    dq, dkv = _xa_bwd(q, kv, do, name=f"{tag}_b_attn")
    G['xa_wq'] = _mm(u, dq, name=f"{tag}_b_wq", ta=True, tm=1024, tn=1024, tk=1024, stack=(2, i, G.get('xa_wq')))
    dh_in, dg = _mm_norm_bwd(dq, W['xa_wq'], i, h, g, dh, name=f"{tag}_b_du", tk=1024)
    G['xa_wkv'] = _mm(mn, dkv, name=f"{tag}_b_wkv", ta=True, tm=1024, tn=1024, tk=256,
                      stack=(2, i, G.get('xa_wkv')))
    dmn = _mm(dkv, W['xa_wkv'], b_layer=i, name=f"{tag}_b_dmn", tb=True, out_dtype=F32, tm=256, tn=1024, tk=1024)
    _, dgm = _rms_bwd(mem, gm, dmn, None, name=f"{tag}_b_mem_norm")
    return dh_in, dg, dgm


def _pad_conv_w(w, rows):
    return jnp.pad(w.astype(F32), ((0, rows - w.shape[0]), (0, 0)))


def _even_fwd(h, g, W, conv_w, conv_b, ln_g, ln_b, sinks, tag):
    u, z = _norm_mm(h, g, W['even_w_in'], 0, name=f"{tag}_in", tn=1792)
    c, act = _conv_a_fwd(z, conv_w, conv_b, ln_g, ln_b, name=f"{tag}_conv")
    kpad = jnp.pad(z[:, 1536:1664], ((WINDOW, 0), (0, 0)))
    vpad = jnp.pad(z[:, 1664:1792], ((WINDOW, 0), (0, 0)))
    o = _swa_fwd(z, kpad, vpad, sinks, name=f"{tag}_swa")
    cat = jnp.concatenate([act, o], axis=-1)
    h2 = _mm(cat, W['even_w_out'], b_layer=0, name=f"{tag}_out", out_dtype=F32, tm=1024, tn=1024, tk=1024, res=h)
    return h2, (h, u, z, c, kpad, vpad, cat)


def _even_bwd(dh, saved, g, W, conv_w, ln_g, ln_b, sinks, G, tag):
    h, u, z, c, kpad, vpad, cat = saved
    dcat = _mm(dh, W['even_w_out'], b_layer=0, name=f"{tag}_b_dcat", tb=True, tm=1024, tn=1024, tk=1024)
    G['even_w_out'] = _mm(cat, dh, name=f"{tag}_b_wout", ta=True, tm=1024, tn=1024, tk=1024)
    dz_a, small = _conv_a_bwd(z, c, dcat, conv_w, ln_g, ln_b, name=f"{tag}_b_conv")
    dq, dkp, dvp, dsinks = _swa_bwd(z, kpad, vpad, sinks, dcat, name=f"{tag}_b_swa")
    dz = jnp.concatenate([dz_a, dq, dkp[WINDOW:].astype(BF16), dvp[WINDOW:].astype(BF16)], axis=-1)
    G['even_w_in'] = _mm(u, dz, name=f"{tag}_b_win", ta=True, tm=1024, tn=1792, tk=1024)
    dh_in, dg = _mm_norm_bwd(dz, W['even_w_in'], 0, h, g, dh, name=f"{tag}_b_du", tk=1792)
    grads = dict(mix=dg, conv_a_w=small[:CONV_A_WIDTH], conv_a_b=small[32:33], conv_a_ln_g=small[33:34],
                 conv_a_ln_b=small[34:35], swa_sinks=dsinks[:, 0])
    return dh_in, grads


def _odd_fwd(h, g, W, conv_w, tag):
    u, z = _norm_mm(h, g, W['odd_w_in'], 0, name=f"{tag}_in", tn=1024)
    y, cc = _sconv_fwd(z, conv_w, name=f"{tag}_conv")
    h2 = _mm(y, W['odd_w_out'], b_layer=0, name=f"{tag}_out", out_dtype=F32, tm=1024, tn=1024, tk=1024, res=h)
    return h2, (h, u, z, y, cc)


def _odd_bwd(dh, saved, g, W, conv_w, G, tag):
    h, u, z, y, cc = saved
    dy = _mm(dh, W['odd_w_out'], b_layer=0, name=f"{tag}_b_dy", tb=True, tm=1024, tn=1024, tk=1024)
    G['odd_w_out'] = _mm(y, dh, name=f"{tag}_b_wout", ta=True, tm=1024, tn=1024, tk=1024)
    dz, dw = _sconv_bwd(z, cc, dy, conv_w, name=f"{tag}_b_conv")
    G['odd_w_in'] = _mm(u, dz, name=f"{tag}_b_win", ta=True, tm=1024, tn=1024, tk=1024)
    dh_in, dg = _mm_norm_bwd(dz, W['odd_w_in'], 0, h, g, dh, name=f"{tag}_b_du", tk=1024)
    return dh_in, dict(mix=dg, sc_conv_w=dw[:SC_WIDTH])


def _local_step(x, mem, tgt, W, conv_a_w, sc_conv_w, P):
    row = lambda v: v.reshape(1, -1)
    conv_a_w = _pad_conv_w(conv_a_w, 32)
    sc_w = _pad_conv_w(sc_conv_w, 8)
    sinks = P['swa_sinks'][0]

    h = x
    saved = []
    for i in range(2):
        t = f"l{i}"
        h, s1 = _ffn_fwd(h, row(P['ffn1_norm'][i]), W, 'ffn1_w_gu', 'ffn1_w_down', i, f"{t}_ffn1")
        if i == 0:
            h, s2 = _even_fwd(h, row(P['mix_norm'][i]), W, conv_a_w, P['conv_a_b'], P['conv_a_ln_g'],
                              P['conv_a_ln_b'], sinks, f"{t}_even")
        else:
            h, s2 = _odd_fwd(h, row(P['mix_norm'][i]), W, sc_w, f"{t}_odd")
        h, s3 = _xa_block_fwd(h, mem, row(P['xa_norm'][i]), row(P['xa_mem_norm'][i]), W, i, f"{t}_xa")
        h, s4 = _ffn_fwd(h, row(P['ffn2_norm'][i]), W, 'ffn2_w_gu', 'ffn2_w_down', i, f"{t}_ffn2")
        saved.append((s1, s2, s3, s4))

    loss, dh, d_final = _final_loss(h, row(P['final_norm']), tgt, name="final_loss")

    G = {}
    gp = {n: [None, None] for n in ('ffn1_norm', 'mix_norm', 'xa_norm', 'xa_mem_norm', 'ffn2_norm')}
    single = {}
    for i in (1, 0):
        t = f"l{i}"
        s1, s2, s3, s4 = saved[i]
        dh, gp['ffn2_norm'][i] = _ffn_bwd(dh, s4, row(P['ffn2_norm'][i]), W, 'ffn2_w_gu', 'ffn2_w_down', i, G,
                                          f"{t}_ffn2")
        dh, gp['xa_norm'][i], gp['xa_mem_norm'][i] = _xa_block_bwd(
            dh, s3, mem, row(P['xa_norm'][i]), row(P['xa_mem_norm'][i]), W, i, G, f"{t}_xa")
        if i == 0:
            dh, g2 = _even_bwd(dh, s2, row(P['mix_norm'][i]), W, conv_a_w, P['conv_a_ln_g'], P['conv_a_ln_b'], sinks,
                               G, f"{t}_even")
        else:
            dh, g2 = _odd_bwd(dh, s2, row(P['mix_norm'][i]), W, sc_w, G, f"{t}_odd")
        gp['mix_norm'][i] = g2.pop('mix')
        single.update(g2)
        dh, gp['ffn1_norm'][i] = _ffn_bwd(dh, s1, row(P['ffn1_norm'][i]), W, 'ffn1_w_gu', 'ffn1_w_down', i, G,
                                          f"{t}_ffn1")

    small = {n: jnp.concatenate(v, axis=0) for n, v in gp.items()}
    small['conv_a_b'] = single['conv_a_b']
    small['conv_a_ln_g'] = single['conv_a_ln_g']
    small['conv_a_ln_b'] = single['conv_a_ln_b']
    small['swa_sinks'] = single['swa_sinks'][None]
    small['final_norm'] = d_final[0]
    small['conv_a_w'] = single['conv_a_w']
    small['sc_conv_w'] = single['sc_conv_w']
    return loss[0, 0], dh, G, small


def _as2d(a):
    return a.reshape(-1, a.shape[-1])


def kernel(x, mem, ffn1_norm, ffn1_w_gu, ffn1_w_down, mix_norm, even_w_in, conv_a_w, conv_a_b, conv_a_ln_g, conv_a_ln_b, swa_sinks, even_w_out, odd_w_in, sc_conv_w, odd_w_out, xa_norm, xa_mem_norm, xa_wq, xa_wkv, xa_wo, ffn2_norm, ffn2_w_gu, ffn2_w_down, final_norm, loss_target, m_ffn1_norm, m_ffn1_w_gu, m_ffn1_w_down, m_mix_norm, m_even_w_in, m_conv_a_w, m_conv_a_b, m_conv_a_ln_g, m_conv_a_ln_b, m_swa_sinks, m_even_w_out, m_odd_w_in, m_sc_conv_w, m_odd_w_out, m_xa_norm, m_xa_mem_norm, m_xa_wq, m_xa_wkv, m_xa_wo, m_ffn2_norm, m_ffn2_w_gu, m_ffn2_w_down, m_final_norm, v_ffn1_norm, v_ffn1_w_gu, v_ffn1_w_down, v_mix_norm, v_even_w_in, v_conv_a_w, v_conv_a_b, v_conv_a_ln_g, v_conv_a_ln_b, v_swa_sinks, v_even_w_out, v_odd_w_in, v_sc_conv_w, v_odd_w_out, v_xa_norm, v_xa_mem_norm, v_xa_wq, v_xa_wkv, v_xa_wo, v_ffn2_norm, v_ffn2_w_gu, v_ffn2_w_down, v_final_norm):
    w = dict(zip(WEIGHT_NAMES, (ffn1_norm, ffn1_w_gu, ffn1_w_down, mix_norm, even_w_in, conv_a_w, conv_a_b, conv_a_ln_g, conv_a_ln_b, swa_sinks, even_w_out, odd_w_in, sc_conv_w, odd_w_out, xa_norm, xa_mem_norm, xa_wq, xa_wkv, xa_wo, ffn2_norm, ffn2_w_gu, ffn2_w_down, final_norm)))
    m = dict(zip(WEIGHT_NAMES, (m_ffn1_norm, m_ffn1_w_gu, m_ffn1_w_down, m_mix_norm, m_even_w_in, m_conv_a_w, m_conv_a_b, m_conv_a_ln_g, m_conv_a_ln_b, m_swa_sinks, m_even_w_out, m_odd_w_in, m_sc_conv_w, m_odd_w_out, m_xa_norm, m_xa_mem_norm, m_xa_wq, m_xa_wkv, m_xa_wo, m_ffn2_norm, m_ffn2_w_gu, m_ffn2_w_down, m_final_norm)))
    v = dict(zip(WEIGHT_NAMES, (v_ffn1_norm, v_ffn1_w_gu, v_ffn1_w_down, v_mix_norm, v_even_w_in, v_conv_a_w, v_conv_a_b, v_conv_a_ln_g, v_conv_a_ln_b, v_swa_sinks, v_even_w_out, v_odd_w_in, v_sc_conv_w, v_odd_w_out, v_xa_norm, v_xa_mem_norm, v_xa_wq, v_xa_wkv, v_xa_wo, v_ffn2_norm, v_ffn2_w_gu, v_ffn2_w_down, v_final_norm)))
    small_shapes = {n: w[n].shape for n in SMALL_NAMES}
    cx, cy, cc = lax.axis_index("x"), lax.axis_index("y"), lax.axis_index("c")
    chip_idx = (2 * cx + cy).astype(jnp.int32).reshape(1)
    core_idx = cc.astype(jnp.int32).reshape(1)
    chip_core_idx = jnp.concatenate([chip_idx, core_idx])

    shards = {n: w[n] for n in COMM_NAMES if n != 'tiny'}
    shards['tiny'] = _tiny_pack(conv_a_w, sc_conv_w)
    meta = [(LAYOUT[n],) + shards[n].shape for n in COMM_NAMES]
    placed = [_place(shards[n], LAYOUT[n], chip_idx, F32 if n == 'tiny' else BF16, name=f"place_{n}")
              for n in COMM_NAMES]
    gathered = dict(zip(COMM_NAMES, _all_gather(placed, meta)))
    W = {}
    for n in COMM_NAMES:
        a = gathered[n]
        if n == 'tiny':
            continue
        if n == 'even_w_in':
            W[n] = a.transpose(0, 2, 1, 3).reshape(1, D_MODEL, -1)
        else:
            W[n] = a if LAYOUT[n] == 'col' else a.reshape(a.shape[0], N_CHIPS * a.shape[2], a.shape[3])
    ca, sc = _tiny_unpack(gathered['tiny'][0])
    conv_a_full = ca.transpose(1, 0, 2).reshape(CONV_A_WIDTH, CONV_A_CH)
    sc_full = sc.transpose(1, 0, 2).reshape(SC_WIDTH, SC_CH)

    loss_part, grad_x, G, g_small = _local_step(x[0], mem[0], loss_target[0], W, conv_a_full, sc_full,
                                                {n: w[n] for n in SMALL_NAMES})
    loss = lax.psum(loss_part, ("x", "y", "c"))

    gs = []
    for n, (kind, L, A, B) in zip(COMM_NAMES, meta):
        if n == 'tiny':
            g = _tiny_pack(g_small['conv_a_w'].reshape(CONV_A_WIDTH, N_CHIPS, 128).transpose(1, 0, 2),
                           g_small['sc_conv_w'].reshape(SC_WIDTH, N_CHIPS, 256).transpose(1, 0, 2))[None]
        elif n == 'even_w_in':
            g = G[n].reshape(A, N_CHIPS, B).transpose(1, 0, 2)[None]
        elif kind == 'col':
            g = G[n].reshape(L, A, N_CHIPS * B)
        else:
            g = G[n].reshape(L, N_CHIPS, A, B)
        gs.append(g)
    got = _pair_exchange(gs, meta)
    ps = [_pair_sum(g, r, m_, core_idx, name=f"rs_pair_sum_{n}") for n, g, r, m_ in zip(COMM_NAMES, gs, got, meta)]
    got2 = _chip_exchange(ps, meta)
    red = [_chip_sum(p, r, m_, chip_core_idx, name=f"rs_chip_sum_{n}")
           for n, p, r, m_ in zip(COMM_NAMES, ps, got2, meta)]
    g_shards, small_parts = _final_exchange(red, meta, _pack_small(g_small))
    g_local = dict(zip(COMM_NAMES, g_shards))
    g_local['conv_a_w'], g_local['sc_conv_w'] = _tiny_unpack(g_local.pop('tiny'))

    grads, deltas, new_m, new_v = {}, {}, {}, {}
    for n in BIG_NAMES:
        shp = w[n].shape
        d, nm, nv = _adamw(_as2d(w[n]), _as2d(g_local[n]), _as2d(m[n]), _as2d(v[n]), name=f"adamw_{n}")
        grads[n], deltas[n], new_m[n], new_v[n] = g_local[n], d.reshape(shp), nm.reshape(shp), nv.reshape(shp)
    gs, ds, ms, vs = _adamw_small(_pack_small({n: w[n] for n in SMALL_NAMES}), small_parts,
                                  _pack_small({n: m[n] for n in SMALL_NAMES}),
                                  _pack_small({n: v[n] for n in SMALL_NAMES}), name="adamw_small")
    for dst, packed in ((grads, gs), (deltas, ds), (new_m, ms), (new_v, vs)):
        dst.update(_unpack_small(packed, small_shapes))

    return (loss, grad_x[None], *[grads[n] for n in WEIGHT_NAMES], *[deltas[n] for n in WEIGHT_NAMES],
            *[new_m[n] for n in WEIGHT_NAMES], *[new_v[n] for n in WEIGHT_NAMES])
```

```python
import functools
import math

import jax
import jax.numpy as jnp
from jax import lax
from jax.experimental import pallas as pl
from jax.experimental.pallas import tpu as pltpu

F32 = jnp.float32
BF16 = jnp.bfloat16

D_MODEL = 1024
D_FF = 2816
CONV_A_CH = 512
CONV_A_WIDTH = 31
SWA_HEADS = 8
SWA_KV_HEADS = 2
SWA_GROUP = 4
HEAD_DIM = 64
WINDOW = 128
SC_CH = 1024
SC_WIDTH = 3
XA_HEADS = 4
XA_HEAD_DIM = 256
RMS_EPS = 1e-6
LN_EPS = 1e-5

ADAM_LR = 0.001
ADAM_B1 = 0.9
ADAM_B2 = 0.999
ADAM_EPS = 1e-08
ADAM_WD = 0.01
ADAM_STEP = 10

N_CHIPS = 4
N_DEV = 8
NEG_BIG = -1e30
VMEM_LIMIT = 56 * 1024 * 1024
MESH = pl.DeviceIdType.MESH

INPUT_NAMES = ['x', 'mem', 'ffn1_norm', 'ffn1_w_gu', 'ffn1_w_down', 'mix_norm', 'even_w_in', 'conv_a_w', 'conv_a_b',
               'conv_a_ln_g', 'conv_a_ln_b', 'swa_sinks', 'even_w_out', 'odd_w_in', 'sc_conv_w', 'odd_w_out', 'xa_norm',
               'xa_mem_norm', 'xa_wq', 'xa_wkv', 'xa_wo', 'ffn2_norm', 'ffn2_w_gu', 'ffn2_w_down', 'final_norm']
WEIGHT_NAMES = INPUT_NAMES[2:]
BIG = [('ffn1_w_gu', 'col'), ('ffn1_w_down', 'row'), ('even_w_in', 'col'), ('conv_a_w', 'col'), ('even_w_out', 'row'),
       ('odd_w_in', 'col'), ('sc_conv_w', 'col'), ('odd_w_out', 'row'), ('xa_wq', 'row'), ('xa_wkv', 'col'),
       ('xa_wo', 'row'), ('ffn2_w_gu', 'col'), ('ffn2_w_down', 'row')]
BIG_NAMES = [n for n, _ in BIG]
SMALL_NAMES = [n for n in WEIGHT_NAMES if n not in BIG_NAMES]


def _cparams(sem=None, vmem=VMEM_LIMIT):
    kw = dict(vmem_limit_bytes=vmem)
    if sem is not None:
        kw['dimension_semantics'] = sem
    return pltpu.CompilerParams(**kw)


def _div_tile(n, want, align=8):
    if n <= want:
        return n
    t = (want // align) * align
    while t >= align:
        if n % t == 0:
            return t
        t -= align
    return n


def _mm(a, b, *, name, ta=False, tb=False, out_dtype=BF16, tm=512, tn=512, tk=512, res=None, scale=1.0,
        b_layer=None, stack=None, n_map=None):
    n_map = n_map or (lambda j: j)
    if ta:
        K, M = a.shape
    else:
        M, K = a.shape
    if tb:
        N, K2 = b.shape[-2:]
    else:
        K2, N = b.shape[-2:]
    assert K == K2, (a.shape, b.shape, ta, tb)
    tm = _div_tile(M, tm, 128 if ta else 16)
    tn = _div_tile(N, tn, 128)
    tk = _div_tile(K, tk, 16 if ta else 128)
    nk = K // tk
    a_spec = pl.BlockSpec((tk, tm), lambda i, j, k: (k, i)) if ta else pl.BlockSpec((tm, tk), lambda i, j, k: (i, k))
    if b_layer is None:
        b_spec = pl.BlockSpec((tn, tk), lambda i, j, k: (j, k)) if tb else pl.BlockSpec((tk, tn), lambda i, j, k: (k, j))
    elif tb:
        b_spec = pl.BlockSpec((None, tn, tk), lambda i, j, k: (b_layer, j, k))
    else:
        b_spec = pl.BlockSpec((None, tk, tn), lambda i, j, k: (b_layer, k, j))
    o_spec = pl.BlockSpec((tm, tn), lambda i, j, k: (i, j))
    out_shape = jax.ShapeDtypeStruct((M, N), out_dtype)
    out_spec = o_spec
    aliases = {}
    extra_specs, extra_args = [], ()
    if stack is not None:
        n_layers, layer, buf = stack[:3]
        n_total = stack[3] if len(stack) > 3 else N
        out_shape = jax.ShapeDtypeStruct((n_layers, M, n_total), out_dtype)
        out_spec = pl.BlockSpec((None, tm, tn), lambda i, j, k: (layer, i, n_map(j)))
        if buf is not None:
            extra_specs, extra_args = [pl.BlockSpec(memory_space=pl.ANY)], (buf,)
            aliases = {2 + (res is not None): 0}
    dims = (((0 if ta else 1,), (1 if tb else 0,)), ((), ()))
    has_res = res is not None
    n_extra = len(extra_args)

    def body(*refs):
        if n_extra:
            refs = refs[:2 + has_res] + refs[2 + has_res + n_extra:]
        if has_res:
            a_ref, b_ref, r_ref, o_ref, acc_ref = refs
        else:
            a_ref, b_ref, o_ref, acc_ref = refs
        k = pl.program_id(2)
        p = lax.dot_general(a_ref[...].astype(BF16), b_ref[...].astype(BF16), dims, preferred_element_type=F32)

        @pl.when(k == 0)
        def _():
            acc_ref[...] = p

        @pl.when(k > 0)
        def _():
            acc_ref[...] += p

        @pl.when(k == nk - 1)
        def _():
            r = acc_ref[...] * scale
            if has_res:
                r = r_ref[...] + r
            o_ref[...] = r.astype(out_dtype)

    in_specs = [a_spec, b_spec] + ([o_spec] if has_res else []) + extra_specs
    args = (a, b) + ((res,) if has_res else ()) + extra_args
    return pl.pallas_call(
        body, name=name, grid=(M // tm, N // tn, nk), in_specs=in_specs, out_specs=out_spec,
        out_shape=out_shape, input_output_aliases=aliases,
        scratch_shapes=[pltpu.VMEM((tm, tn), F32)],
        compiler_params=_cparams(("parallel", "parallel", "arbitrary")),
    )(*args)


def _rms_fwd(x, g, *, name):
    S, D = x.shape
    ts = _div_tile(S, 512)

    def body(x_ref, g_ref, o_ref):
        xv = x_ref[...]
        r = lax.rsqrt(jnp.mean(xv * xv, axis=-1, keepdims=True) + RMS_EPS)
        o_ref[...] = (xv * r * g_ref[...]).astype(BF16)

    return pl.pallas_call(
        body, name=name, grid=(S // ts,),
        in_specs=[pl.BlockSpec((ts, D), lambda i: (i, 0)), pl.BlockSpec((1, D), lambda i: (0, 0))],
        out_specs=pl.BlockSpec((ts, D), lambda i: (i, 0)),
        out_shape=jax.ShapeDtypeStruct((S, D), BF16),
        compiler_params=_cparams(("parallel",)),
    )(x, g)


def _norm_mm(h, g, w, layer, *, name, tn):
    S, D = h.shape
    N = w.shape[-1]
    tm = _div_tile(S, 1024, 16)
    tn = _div_tile(N, tn, 128)

    def body(h_ref, g_ref, w_ref, u_ref, z_ref, u_s):
        @pl.when(pl.program_id(1) == 0)
        def _():
            xv = h_ref[...]
            r = lax.rsqrt(jnp.mean(xv * xv, axis=-1, keepdims=True) + RMS_EPS)
            u = (xv * r * g_ref[...]).astype(BF16)
            u_s[...] = u
            u_ref[...] = u

        z_ref[...] = jnp.dot(u_s[...], w_ref[...], preferred_element_type=F32).astype(BF16)

    row = pl.BlockSpec((tm, D), lambda i, j: (i, 0))
    return pl.pallas_call(
        body, name=name, grid=(S // tm, N // tn),
        in_specs=[row, pl.BlockSpec((1, D), lambda i, j: (0, 0)), pl.BlockSpec((None, D, tn), lambda i, j: (layer, 0, j))],
        out_specs=[row, pl.BlockSpec((tm, tn), lambda i, j: (i, j))],
        out_shape=[jax.ShapeDtypeStruct((S, D), BF16), jax.ShapeDtypeStruct((S, N), BF16)],
        scratch_shapes=[pltpu.VMEM((tm, D), BF16)],
        compiler_params=_cparams(("parallel", "arbitrary")),
    )(h, g, w)


def _mm_norm_bwd(dz, w, layer, h, g, dres, *, name, tk):
    S, K = dz.shape
    D = h.shape[1]
    tm = _div_tile(S, 512, 16)
    tk = _div_tile(K, tk, 128)
    nk = K // tk

    def body(dz_ref, w_ref, h_ref, g_ref, dr_ref, dx_ref, dg_ref, acc):
        i = pl.program_id(0)
        k = pl.program_id(1)
        p = lax.dot_general(dz_ref[...], w_ref[...], (((1,), (1,)), ((), ())), preferred_element_type=F32)

        @pl.when(k == 0)
        def _():
            acc[...] = p

        @pl.when(k > 0)
        def _():
            acc[...] += p

        @pl.when(k == nk - 1)
        def _():
            xv = h_ref[...]
            du = acc[...]
            r = lax.rsqrt(jnp.mean(xv * xv, axis=-1, keepdims=True) + RMS_EPS)
            xhat = xv * r
            part = jnp.sum(du * xhat, axis=0, keepdims=True)

            @pl.when(i == 0)
            def _():
                dg_ref[...] = part

            @pl.when(i > 0)
            def _():
                dg_ref[...] += part

            dxhat = du * g_ref[...]
            dx_ref[...] = dr_ref[...] + r * (dxhat - xhat * jnp.mean(dxhat * xhat, axis=-1, keepdims=True))

    row = pl.BlockSpec((tm, D), lambda i, k: (i, 0))
    vec = pl.BlockSpec((1, D), lambda i, k: (0, 0))
    return pl.pallas_call(
        body, name=name, grid=(S // tm, nk),
        in_specs=[pl.BlockSpec((tm, tk), lambda i, k: (i, k)), pl.BlockSpec((None, D, tk), lambda i, k: (layer, 0, k)),
                  row, vec, row],
        out_specs=[row, vec],
        out_shape=[jax.ShapeDtypeStruct((S, D), F32), jax.ShapeDtypeStruct((1, D), F32)],
        scratch_shapes=[pltpu.VMEM((tm, D), F32)],
        compiler_params=_cparams(("arbitrary", "arbitrary")),
    )(dz, w, h, g, dres)


def _rms_bwd(x, g, du, dres, *, name):
    S, D = x.shape
    ts = _div_tile(S, 512)
    has_res = dres is not None

    def body(*refs):
        if has_res:
            x_ref, g_ref, du_ref, dr_ref, dx_ref, dg_ref = refs
        else:
            x_ref, g_ref, du_ref, dg_ref = refs
        i = pl.program_id(0)
        xv = x_ref[...]
        duv = du_ref[...].astype(F32)
        r = lax.rsqrt(jnp.mean(xv * xv, axis=-1, keepdims=True) + RMS_EPS)
        xhat = xv * r
        part = jnp.sum(duv * xhat, axis=0, keepdims=True)

        @pl.when(i == 0)
        def _():
            dg_ref[...] = part

        @pl.when(i > 0)
        def _():
            dg_ref[...] += part

        if has_res:
            dxhat = duv * g_ref[...]
            dx = r * (dxhat - xhat * jnp.mean(dxhat * xhat, axis=-1, keepdims=True))
            dx_ref[...] = dr_ref[...] + dx

    row = pl.BlockSpec((ts, D), lambda i: (i, 0))
    vec = pl.BlockSpec((1, D), lambda i: (0, 0))
    if has_res:
        dx, dg = pl.pallas_call(
            body, name=name, grid=(S // ts,), in_specs=[row, vec, row, row], out_specs=[row, vec],
            out_shape=[jax.ShapeDtypeStruct((S, D), F32), jax.ShapeDtypeStruct((1, D), F32)],
            compiler_params=_cparams(("arbitrary",)),
        )(x, g, du, dres)
        return dx, dg
    dg = pl.pallas_call(
        body, name=name, grid=(S // ts,), in_specs=[row, vec, row], out_specs=vec,
        out_shape=jax.ShapeDtypeStruct((1, D), F32),
        compiler_params=_cparams(("arbitrary",)),
    )(x, g, du)
    return None, dg


def _final_loss(h, g, tgt, *, name):
    S, D = h.shape
    ts = _div_tile(S, 512)

    def body(h_ref, g_ref, t_ref, loss_ref, dh_ref, dg_ref):
        i = pl.program_id(0)
        xv = h_ref[...]
        gv = g_ref[...]
        r = lax.rsqrt(jnp.mean(xv * xv, axis=-1, keepdims=True) + RMS_EPS)
        xhat = xv * r
        err = xhat * gv - t_ref[...]
        lpart = 0.5 * jnp.sum(jnp.mean(err * err, axis=-1, keepdims=True), axis=0, keepdims=True)
        dy = err * (1.0 / D)
        gpart = jnp.sum(dy * xhat, axis=0, keepdims=True)

        @pl.when(i == 0)
        def _():
            loss_ref[...] = jnp.broadcast_to(lpart, loss_ref.shape)
            dg_ref[...] = gpart

        @pl.when(i > 0)
        def _():
            loss_ref[...] += jnp.broadcast_to(lpart, loss_ref.shape)
            dg_ref[...] += gpart

        dxhat = dy * gv
        dh_ref[...] = r * (dxhat - xhat * jnp.mean(dxhat * xhat, axis=-1, keepdims=True))

    row = pl.BlockSpec((ts, D), lambda i: (i, 0))
    vec = pl.BlockSpec((1, D), lambda i: (0, 0))
    return pl.pallas_call(
        body, name=name, grid=(S // ts,), in_specs=[row, vec, row],
        out_specs=[pl.BlockSpec((8, 128), lambda i: (0, 0)), row, vec],
        out_shape=[jax.ShapeDtypeStruct((8, 128), F32), jax.ShapeDtypeStruct((S, D), F32),
                   jax.ShapeDtypeStruct((1, D), F32)],
        compiler_params=_cparams(("arbitrary",)),
    )(h, g, tgt)


def _sigmoid(x):
    return 1.0 / (1.0 + jnp.exp(-x))


FFN_CHUNK = 1408
FFN_CHUNKS = D_FF // FFN_CHUNK
FFN_BWD_PIECE = 768
FFN_BWD_SLAB = 256


def _ffn_fwd_fused(h, g, w_gu, w_down, layer, *, name):
    S, D = h.shape
    tm = _div_tile(S, 512, 16)
    tf, nj = FFN_CHUNK, FFN_CHUNKS

    def body(h_ref, g_ref, wg_ref, wu_ref, wd_ref, h2_ref, u_ref, gate_ref, up_ref, a_ref, u_s, acc):
        j = pl.program_id(1)

        @pl.when(j == 0)
        def _():
            xv = h_ref[...]
            r = lax.rsqrt(jnp.mean(xv * xv, axis=-1, keepdims=True) + RMS_EPS)
            u = (xv * r * g_ref[...]).astype(BF16)
            u_s[...] = u
            u_ref[...] = u

        u = u_s[...]
        gate = jnp.dot(u, wg_ref[...], preferred_element_type=F32)
        up = jnp.dot(u, wu_ref[...], preferred_element_type=F32)
        gate_ref[...] = gate.astype(BF16)
        up_ref[...] = up.astype(BF16)
        a = (gate * _sigmoid(gate) * up).astype(BF16)
        a_ref[...] = a
        p = jnp.dot(a, wd_ref[...], preferred_element_type=F32)

        @pl.when(j == 0)
        def _():
            acc[...] = p

        @pl.when(j > 0)
        def _():
            acc[...] += p

        @pl.when(j == nj - 1)
        def _():
            h2_ref[...] = h_ref[...] + 0.5 * acc[...]

    row = pl.BlockSpec((tm, D), lambda i, j: (i, 0))
    chunk = pl.BlockSpec((tm, tf), lambda i, j: (i, j))
    hidden = jax.ShapeDtypeStruct((S, D_FF), BF16)
    return pl.pallas_call(
        body, name=name, grid=(S // tm, nj),
        in_specs=[row, pl.BlockSpec((1, D), lambda i, j: (0, 0)),
                  pl.BlockSpec((None, D, tf), lambda i, j: (layer, 0, j)),
                  pl.BlockSpec((None, D, tf), lambda i, j: (layer, 0, nj + j)),
                  pl.BlockSpec((None, tf, D), lambda i, j: (layer, j, 0))],
        out_specs=[row, row, chunk, chunk, chunk],
        out_shape=[jax.ShapeDtypeStruct((S, D), F32), jax.ShapeDtypeStruct((S, D), BF16), hidden, hidden, hidden],
        scratch_shapes=[pltpu.VMEM((tm, D), BF16), pltpu.VMEM((tm, D), F32)],
        compiler_params=_cparams(("parallel", "arbitrary")),
    )(h, g, w_gu, w_gu, w_down)


def _ffn_bwd_fused(dh, h, g, gate, up, w_gu, w_down, layer, *, name):
    S, D = h.shape
    tm = _div_tile(S, 512, FFN_BWD_SLAB)
    tf = FFN_CHUNK
    nj = D_FF // tf
    slab = min(FFN_BWD_SLAB, tm)
    nt = (((1,), (1,)), ((), ()))
    pieces = [(c0, min(FFN_BWD_PIECE, tf - c0)) for c0 in range(0, tf, FFN_BWD_PIECE)]

    def body(dh_ref, h_ref, g_ref, gate_ref, up_ref, wg_ref, wu_ref, wd_ref, dx_ref, dg_ref, dgate_ref, dup_ref,
             dy_s, acc):
        i = pl.program_id(0)
        j = pl.program_id(1)

        @pl.when(j == 0)
        def _():
            for r0 in range(0, tm, slab):
                rows = pl.ds(r0, slab)
                dy_s[rows, :] = (0.5 * dh_ref[rows, :]).astype(BF16)

        p = None
        for c0, cw in pieces:
            cols = pl.ds(c0, cw)
            da = lax.dot_general(dy_s[...], wd_ref[cols, :], nt, preferred_element_type=F32)
            gt = gate_ref[:, cols].astype(F32)
            sg = _sigmoid(gt)
            dgate = (da * up_ref[:, cols].astype(F32) * sg * (1.0 + gt * (1.0 - sg))).astype(BF16)
            dup = (da * gt * sg).astype(BF16)
            dgate_ref[:, cols] = dgate
            dup_ref[:, cols] = dup
            q = (lax.dot_general(dgate, wg_ref[:, cols], nt, preferred_element_type=F32)
                 + lax.dot_general(dup, wu_ref[:, cols], nt, preferred_element_type=F32))
            p = q if p is None else p + q

        @pl.when(j == 0)
        def _():
            acc[...] = p

        @pl.when(j > 0)
        def _():
            acc[...] += p

        @pl.when(j == nj - 1)
        def _():
            part = jnp.zeros((1, D), F32)
            for r0 in range(0, tm, slab):
                rows = pl.ds(r0, slab)
                xv = h_ref[rows, :]
                du = acc[rows, :]
                r = lax.rsqrt(jnp.mean(xv * xv, axis=-1, keepdims=True) + RMS_EPS)
                xhat = xv * r
                part = part + jnp.sum(du * xhat, axis=0, keepdims=True)
                dxhat = du * g_ref[...]
                dx_ref[rows, :] = dh_ref[rows, :] + r * (
                    dxhat - xhat * jnp.mean(dxhat * xhat, axis=-1, keepdims=True))

            @pl.when(i == 0)
            def _():
                dg_ref[...] = part

            @pl.when(i > 0)
            def _():
                dg_ref[...] += part

    row = pl.BlockSpec((tm, D), lambda i, j: (i, 0))
    vec = pl.BlockSpec((1, D), lambda i, j: (0, 0))
    chunk = pl.BlockSpec((tm, tf), lambda i, j: (i, j))
    hidden = jax.ShapeDtypeStruct((S, D_FF), BF16)
    return pl.pallas_call(
        body, name=name, grid=(S // tm, nj),
        in_specs=[row, row, vec, chunk, chunk,
                  pl.BlockSpec((None, D, tf), lambda i, j: (layer, 0, j)),
                  pl.BlockSpec((None, D, tf), lambda i, j: (layer, 0, nj + j)),
                  pl.BlockSpec((None, tf, D), lambda i, j: (layer, j, 0))],
        out_specs=[row, vec, chunk, chunk],
        out_shape=[jax.ShapeDtypeStruct((S, D), F32), jax.ShapeDtypeStruct((1, D), F32), hidden, hidden],
        scratch_shapes=[pltpu.VMEM((tm, D), BF16), pltpu.VMEM((tm, D), F32)],
        compiler_params=_cparams(("arbitrary", "arbitrary")),
    )(dh, h, g, gate, up, w_gu, w_gu, w_down)


CONV_HALO = 32
CONV_SUB_ROWS = 128


def _conv_a_fwd(z, w, bias, ln_g, ln_b, *, name):
    S = z.shape[0]
    C = CONV_A_CH
    ts = _div_tile(S, 256, 32)

    def body(val_ref, gate_ref, w_ref, b_ref, g_ref, lb_ref, c_ref, act_ref, win):
        i = pl.program_id(0)

        @pl.when(i == 0)
        def _():
            win[pl.ds(0, CONV_HALO), :] = jnp.zeros((CONV_HALO, C), F32)

        @pl.when(i > 0)
        def _():
            win[pl.ds(0, CONV_HALO), :] = win[pl.ds(ts, CONV_HALO), :]

        a = val_ref[...].astype(F32) * _sigmoid(gate_ref[...].astype(F32))
        win[pl.ds(CONV_HALO, ts), :] = a
        rs = min(CONV_SUB_ROWS, ts)
        for cb in range(C // 128):
            lanes = pl.ds(128 * cb, 128)
            for rt in range(ts // rs):
                sub = jnp.broadcast_to(b_ref[:, lanes], (rs, 128))
                for k in range(CONV_A_WIDTH):
                    sub = sub + w_ref[pl.ds(k, 1), lanes] * win[
                        pl.ds(CONV_HALO - (CONV_A_WIDTH - 1) + k + rs * rt, rs), lanes]
                c_ref[pl.ds(rs * rt, rs), lanes] = sub
        acc = c_ref[...]
        mu = jnp.mean(acc, axis=-1, keepdims=True)
        xc = acc - mu
        var = jnp.mean(xc * xc, axis=-1, keepdims=True)
        ln = xc * lax.rsqrt(var + LN_EPS) * g_ref[...] + lb_ref[...]
        act_ref[...] = (ln * _sigmoid(ln)).astype(BF16)

    row = lambda col: pl.BlockSpec((ts, C), lambda i, col=col: (i, col))
    vec = pl.BlockSpec((1, C), lambda i: (0, 0))
    return pl.pallas_call(
        body, name=name, grid=(S // ts,),
        in_specs=[row(0), row(1), pl.BlockSpec((32, C), lambda i: (0, 0)), vec, vec, vec],
        out_specs=[row(0), row(0)],
        out_shape=[jax.ShapeDtypeStruct((S, C), F32), jax.ShapeDtypeStruct((S, C), BF16)],
        scratch_shapes=[pltpu.VMEM((ts + CONV_HALO, C), F32)],
        compiler_params=_cparams(("arbitrary",)),
    )(z, z, w, bias, ln_g, ln_b)


def _conv_a_bwd(z, c, dcat, w, ln_g, ln_b, *, name):
    S = z.shape[0]
    C = CONV_A_CH
    ts = _div_tile(S, 256, 32)
    n = S // ts

    def body(val_ref, gate_ref, c_ref, da_ref, w_ref, g_ref, lb_ref, dz_ref, small_ref, win, a_s, da_s, dw8):
        i = pl.program_id(0)

        @pl.when(i == 0)
        def _():
            win[pl.ds(ts, CONV_HALO), :] = jnp.zeros((CONV_HALO, C), F32)
            small_ref[...] = jnp.zeros(small_ref.shape, F32)
            dw8[...] = jnp.zeros(dw8.shape, F32)

        @pl.when(i > 0)
        def _():
            win[pl.ds(ts, CONV_HALO), :] = win[pl.ds(0, CONV_HALO), :]

        cv = c_ref[...]
        gv = g_ref[...]
        mu = jnp.mean(cv, axis=-1, keepdims=True)
        xc = cv - mu
        var = jnp.mean(xc * xc, axis=-1, keepdims=True)
        rstd = lax.rsqrt(var + LN_EPS)
        xhat = xc * rstd
        ln = xhat * gv + lb_ref[...]
        sg = _sigmoid(ln)
        dln = da_ref[...].astype(F32) * (sg * (1.0 + ln * (1.0 - sg)))
        small_ref[pl.ds(33, 1), :] += jnp.sum(dln * xhat, axis=0, keepdims=True)
        small_ref[pl.ds(34, 1), :] += jnp.sum(dln, axis=0, keepdims=True)
        dxhat = dln * gv
        dc = rstd * (dxhat - jnp.mean(dxhat, axis=-1, keepdims=True)
                     - xhat * jnp.mean(dxhat * xhat, axis=-1, keepdims=True))
        small_ref[pl.ds(32, 1), :] += jnp.sum(dc, axis=0, keepdims=True)
        win[pl.ds(0, ts), :] = dc

        val = val_ref[...].astype(F32)
        sgg = _sigmoid(gate_ref[...].astype(F32))
        a_s[...] = val * sgg
        rs = min(CONV_SUB_ROWS, ts)
        for cb in range(C // 128):
            lanes = pl.ds(128 * cb, 128)
            for rt in range(ts // rs):
                a_sub = a_s[pl.ds(rs * rt, rs), lanes]
                da = jnp.zeros((rs, 128), F32)
                for k in range(CONV_A_WIDTH):
                    sh = win[pl.ds(CONV_A_WIDTH - 1 - k + rs * rt, rs), lanes]
                    da = da + w_ref[pl.ds(k, 1), lanes] * sh
                    prod = a_sub * sh
                    part = prod[0:8]
                    for r in range(1, rs // 8):
                        part = part + prod[8 * r:8 * r + 8]
                    dw8[pl.ds(8 * k, 8), lanes] += part
                da_s[pl.ds(rs * rt, rs), lanes] = da
        da = da_s[...]
        dz_ref[:, pl.ds(0, C)] = (da * sgg).astype(BF16)
        dz_ref[:, pl.ds(C, C)] = (da * val * sgg * (1.0 - sgg)).astype(BF16)

        @pl.when(i == n - 1)
        def _():
            for k in range(CONV_A_WIDTH):
                small_ref[pl.ds(k, 1), :] = jnp.sum(dw8[pl.ds(8 * k, 8), :], axis=0, keepdims=True)

    row = lambda col: pl.BlockSpec((ts, C), lambda i, col=col: (n - 1 - i, col))
    vec = pl.BlockSpec((1, C), lambda i: (0, 0))
    return pl.pallas_call(
        body, name=name, grid=(n,),
        in_specs=[row(0), row(1), row(0), row(0), pl.BlockSpec((32, C), lambda i: (0, 0)), vec, vec],
        out_specs=[pl.BlockSpec((ts, 2 * C), lambda i: (n - 1 - i, 0)), pl.BlockSpec((40, C), lambda i: (0, 0))],
        out_shape=[jax.ShapeDtypeStruct((S, 2 * C), BF16), jax.ShapeDtypeStruct((40, C), F32)],
        scratch_shapes=[pltpu.VMEM((ts + CONV_HALO, C), F32), pltpu.VMEM((ts, C), F32), pltpu.VMEM((ts, C), F32),
                        pltpu.VMEM((8 * 32, C), F32)],
        compiler_params=_cparams(("arbitrary",)),
    )(z, z, c, dcat, w, ln_g, ln_b)


SC_HALO = 8


def _sconv_fwd(z, w, *, name):
    S = z.shape[0]
    C = SC_CH
    ts = _div_tile(S, 256, 16)

    def body(gb_ref, gc_ref, v_ref, w_ref, y_ref, cc_ref, win):
        i = pl.program_id(0)

        @pl.when(i == 0)
        def _():
            win[pl.ds(0, SC_HALO), :] = jnp.zeros((SC_HALO, C), F32)

        @pl.when(i > 0)
        def _():
            win[pl.ds(0, SC_HALO), :] = win[pl.ds(ts, SC_HALO), :]

        win[pl.ds(SC_HALO, ts), :] = gc_ref[...].astype(F32) * v_ref[...].astype(F32)
        acc = jnp.zeros((ts, C), F32)
        for k in range(SC_WIDTH):
            acc = acc + w_ref[pl.ds(k, 1), :] * win[pl.ds(SC_HALO - (SC_WIDTH - 1) + k, ts), :]
        cc_ref[...] = acc.astype(BF16)
        y_ref[...] = (gb_ref[...].astype(F32) * acc).astype(BF16)

    row = lambda col: pl.BlockSpec((ts, C), lambda i, col=col: (i, col))
    return pl.pallas_call(
        body, name=name, grid=(S // ts,),
        in_specs=[row(0), row(1), row(2), pl.BlockSpec((8, C), lambda i: (0, 0))],
        out_specs=[row(0), row(0)],
        out_shape=[jax.ShapeDtypeStruct((S, C), BF16), jax.ShapeDtypeStruct((S, C), BF16)],
        scratch_shapes=[pltpu.VMEM((ts + SC_HALO, C), F32)],
        compiler_params=_cparams(("arbitrary",)),
    )(z, z, z, w)


def _sconv_bwd(z, cc, dy, w, *, name):
    S = z.shape[0]
    C = SC_CH
    ts = _div_tile(S, 256, 16)
    n = S // ts

    def body(gb_ref, gc_ref, v_ref, cc_ref, dy_ref, w_ref, dz_ref, dw_ref, win):
        i = pl.program_id(0)

        @pl.when(i == 0)
        def _():
            win[pl.ds(ts, SC_HALO), :] = jnp.zeros((SC_HALO, C), F32)
            dw_ref[...] = jnp.zeros(dw_ref.shape, F32)

        @pl.when(i > 0)
        def _():
            win[pl.ds(ts, SC_HALO), :] = win[pl.ds(0, SC_HALO), :]

        dyv = dy_ref[...].astype(F32)
        gb = gb_ref[...].astype(F32)
        gc = gc_ref[...].astype(F32)
        val = v_ref[...].astype(F32)
        dz_ref[:, pl.ds(0, C)] = (dyv * cc_ref[...].astype(F32)).astype(BF16)
        win[pl.ds(0, ts), :] = dyv * gb
        cv = gc * val
        dcv = jnp.zeros((ts, C), F32)
        for k in range(SC_WIDTH):
            sh = win[pl.ds(SC_WIDTH - 1 - k, ts), :]
            dcv = dcv + w_ref[pl.ds(k, 1), :] * sh
            dw_ref[pl.ds(k, 1), :] += jnp.sum(cv * sh, axis=0, keepdims=True)
        dz_ref[:, pl.ds(C, C)] = (dcv * val).astype(BF16)
        dz_ref[:, pl.ds(2 * C, C)] = (dcv * gc).astype(BF16)

    row = lambda col: pl.BlockSpec((ts, C), lambda i, col=col: (n - 1 - i, col))
    return pl.pallas_call(
        body, name=name, grid=(n,),
        in_specs=[row(0), row(1), row(2), row(0), row(0), pl.BlockSpec((8, C), lambda i: (0, 0))],
        out_specs=[pl.BlockSpec((ts, 3 * C), lambda i: (n - 1 - i, 0)), pl.BlockSpec((8, C), lambda i: (0, 0))],
        out_shape=[jax.ShapeDtypeStruct((S, 3 * C), BF16), jax.ShapeDtypeStruct((8, C), F32)],
        scratch_shapes=[pltpu.VMEM((ts + SC_HALO, C), F32)],
        compiler_params=_cparams(("arbitrary",)),
    )(z, z, z, cc, dy, w)


SWA_Q_COL = 2
SWA_SLOPES = [2.0 ** (-8.0 * (h + 1) / SWA_HEADS) for h in range(SWA_HEADS)]
SWA_SCALE = HEAD_DIM ** -0.5


SWA_GROUP_ROWS = SWA_GROUP * WINDOW


def _swa_masks():
    shape = (SWA_GROUP_ROWS, 2 * WINDOW)
    ii = lax.broadcasted_iota(jnp.int32, shape, 0)
    jj = lax.broadcasted_iota(jnp.int32, shape, 1)
    dist = (ii & (WINDOW - 1)) + WINDOW - jj
    valid = (dist >= 0) & (dist < WINDOW)
    grp = lax.broadcasted_iota(jnp.int32, (SWA_GROUP_ROWS, 1), 0) // WINDOW
    return dist.astype(F32), valid, jj, grp


def _by_group(grp, vals):
    out = jnp.full(grp.shape, vals[SWA_GROUP - 1], F32)
    for g in range(SWA_GROUP - 2, -1, -1):
        out = jnp.where(grp == g, vals[g], out)
    return out


def _stack_heads(ref, rows, kv):
    return jnp.concatenate([ref[rows, pl.ds(HEAD_DIM * (kv * SWA_GROUP + g), HEAD_DIM)] for g in range(SWA_GROUP)],
                           axis=0)


def _swa_probs(qg, kk, sink, slope, distf, valid):
    s = lax.dot_general(qg, kk, (((1,), (1,)), ((), ())), preferred_element_type=F32) * SWA_SCALE
    s = s - slope * distf
    s = jnp.where(valid, s, NEG_BIG)
    m = jnp.maximum(jnp.max(s, axis=-1, keepdims=True), sink)
    p = jnp.exp(s - m)
    l = jnp.sum(p, axis=-1, keepdims=True) + jnp.exp(sink - m)
    return p, m, l


def _swa_fwd(z, kpad, vpad, sinks, *, name):
    S = z.shape[0]
    tq = _div_tile(S, 256, 128)
    nblk = tq // WINDOW
    W = WINDOW

    def body(sink_ref, q_ref, k_ref, v_ref, o_ref):
        i = pl.program_id(0)
        distf, valid0, jj, grp = _swa_masks()
        for kv in range(SWA_KV_HEADS):
            heads = range(kv * SWA_GROUP, (kv + 1) * SWA_GROUP)
            sink = _by_group(grp, [sink_ref[h] for h in heads])
            slope = _by_group(grp, [SWA_SLOPES[h] for h in heads])
            for b in range(nblk):
                nb = i * nblk + b
                start = pl.multiple_of(nb * W, W)
                rows = pl.ds(W * b, W)
                valid = valid0 & ((jj >= W) | (nb > 0))
                kk = k_ref[pl.ds(start, 2 * W), pl.ds(HEAD_DIM * kv, HEAD_DIM)]
                vv = v_ref[pl.ds(start, 2 * W), pl.ds(HEAD_DIM * kv, HEAD_DIM)]
                p, m, l = _swa_probs(_stack_heads(q_ref, rows, kv), kk, sink, slope, distf, valid)
                o = (jnp.dot(p.astype(BF16), vv, preferred_element_type=F32) / l).astype(BF16)
                for g, h in enumerate(heads):
                    o_ref[rows, pl.ds(HEAD_DIM * h, HEAD_DIM)] = o[W * g:W * (g + 1)]

    full = pl.BlockSpec((S + W, 2 * HEAD_DIM), lambda i: (0, 0))
    return pl.pallas_call(
        body, name=name, grid=(S // tq,),
        in_specs=[pl.BlockSpec(memory_space=pltpu.SMEM), pl.BlockSpec((tq, 512), lambda i: (i, SWA_Q_COL)), full, full],
        out_specs=pl.BlockSpec((tq, 512), lambda i: (i, 0)),
        out_shape=jax.ShapeDtypeStruct((S, 512), BF16),
        compiler_params=_cparams(("parallel",)),
    )(sinks, z, kpad, vpad)


def _swa_bwd(z, kpad, vpad, sinks, dcat, *, name):
    S = z.shape[0]
    tq = _div_tile(S, 256, 128)
    nblk = tq // WINDOW
    W = WINDOW

    def body(sink_ref, q_ref, k_ref, v_ref, do_ref, dq_ref, dk_ref, dv_ref, ds_ref):
        i = pl.program_id(0)

        @pl.when(i == 0)
        def _():
            dk_ref[...] = jnp.zeros(dk_ref.shape, F32)
            dv_ref[...] = jnp.zeros(dv_ref.shape, F32)
            ds_ref[...] = jnp.zeros(ds_ref.shape, F32)

        distf, valid0, jj, grp = _swa_masks()
        tn = (((0,), (0,)), ((), ()))
        for kv in range(SWA_KV_HEADS):
            heads = range(kv * SWA_GROUP, (kv + 1) * SWA_GROUP)
            sink = _by_group(grp, [sink_ref[h] for h in heads])
            slope = _by_group(grp, [SWA_SLOPES[h] for h in heads])
            for b in range(nblk):
                nb = i * nblk + b
                start = pl.multiple_of(nb * W, W)
                rows = pl.ds(W * b, W)
                valid = valid0 & ((jj >= W) | (nb > 0))
                kk = k_ref[pl.ds(start, 2 * W), pl.ds(HEAD_DIM * kv, HEAD_DIM)]
                vv = v_ref[pl.ds(start, 2 * W), pl.ds(HEAD_DIM * kv, HEAD_DIM)]
                qg = _stack_heads(q_ref, rows, kv)
                dog = _stack_heads(do_ref, rows, kv)
                p, m, l = _swa_probs(qg, kk, sink, slope, distf, valid)
                inv_l = 1.0 / l
                pn = p * inv_l
                dp = lax.dot_general(dog, vv, (((1,), (1,)), ((), ())), preferred_element_type=F32)
                delta = jnp.sum(pn * dp, axis=-1, keepdims=True)
                dsc = (pn * (dp - delta)).astype(BF16)
                dsink = jnp.exp(sink - m) * inv_l * delta
                dq = (jnp.dot(dsc, kk, preferred_element_type=F32) * SWA_SCALE).astype(BF16)
                for g, h in enumerate(heads):
                    ds_ref[pl.ds(h, 1), :] += jnp.broadcast_to(
                        -jnp.sum(dsink[W * g:W * (g + 1)], axis=0, keepdims=True), (1, 128))
                    dq_ref[rows, pl.ds(HEAD_DIM * h, HEAD_DIM)] = dq[W * g:W * (g + 1)]
                dk_ref[pl.ds(start, 2 * W), pl.ds(HEAD_DIM * kv, HEAD_DIM)] += lax.dot_general(
                    dsc, qg, tn, preferred_element_type=F32) * SWA_SCALE
                dv_ref[pl.ds(start, 2 * W), pl.ds(HEAD_DIM * kv, HEAD_DIM)] += lax.dot_general(
                    pn.astype(BF16), dog, tn, preferred_element_type=F32)

    full = pl.BlockSpec((S + W, 2 * HEAD_DIM), lambda i: (0, 0))
    return pl.pallas_call(
        body, name=name, grid=(S // tq,),
        in_specs=[pl.BlockSpec(memory_space=pltpu.SMEM), pl.BlockSpec((tq, 512), lambda i: (i, SWA_Q_COL)), full, full,
                  pl.BlockSpec((tq, 512), lambda i: (i, 1))],
        out_specs=[pl.BlockSpec((tq, 512), lambda i: (i, 0)), full, full, pl.BlockSpec((8, 128), lambda i: (0, 0))],
        out_shape=[jax.ShapeDtypeStruct((S, 512), BF16), jax.ShapeDtypeStruct((S + W, 2 * HEAD_DIM), F32),
                   jax.ShapeDtypeStruct((S + W, 2 * HEAD_DIM), F32), jax.ShapeDtypeStruct((8, 128), F32)],
        compiler_params=_cparams(("arbitrary",)),
    )(sinks, z, kpad, vpad, dcat)


XA_SCALE = XA_HEAD_DIM ** -0.5


def _xa_probs(qh, kh):
    s = lax.dot_general(qh, kh, (((1,), (1,)), ((), ())), preferred_element_type=F32) * XA_SCALE
    m = jnp.max(s, axis=-1, keepdims=True)
    p = jnp.exp(s - m)
    return p, jnp.sum(p, axis=-1, keepdims=True)


def _xa_fwd(q, kv, *, name):
    S, D = q.shape
    M = kv.shape[0]
    ts = _div_tile(S, 512, 16)
    HD = XA_HEAD_DIM

    def body(q_ref, k_ref, v_ref, o_ref):
        for h in range(XA_HEADS):
            qh = q_ref[:, pl.ds(HD * h, HD)]
            p, l = _xa_probs(qh, k_ref[:, pl.ds(HD * h, HD)])
            o = jnp.dot(p.astype(BF16), v_ref[:, pl.ds(HD * h, HD)], preferred_element_type=F32) / l
            o_ref[:, pl.ds(HD * h, HD)] = o.astype(BF16)

    return pl.pallas_call(
        body, name=name, grid=(S // ts,),
        in_specs=[pl.BlockSpec((ts, D), lambda i: (i, 0)), pl.BlockSpec((M, D), lambda i: (0, 0)),
                  pl.BlockSpec((M, D), lambda i: (0, 1))],
        out_specs=pl.BlockSpec((ts, D), lambda i: (i, 0)),
        out_shape=jax.ShapeDtypeStruct((S, D), BF16),
        compiler_params=_cparams(("parallel",)),
    )(q, kv, kv)


def _xa_bwd(q, kv, do, *, name):
    S, D = q.shape
    M = kv.shape[0]
    ts = _div_tile(S, 512, 16)
    HD = XA_HEAD_DIM

    def body(q_ref, k_ref, v_ref, do_ref, dq_ref, dkv_ref):
        i = pl.program_id(0)

        @pl.when(i == 0)
        def _():
            dkv_ref[...] = jnp.zeros(dkv_ref.shape, F32)

        for h in range(XA_HEADS):
            qh = q_ref[:, pl.ds(HD * h, HD)]
            kh = k_ref[:, pl.ds(HD * h, HD)]
            vh = v_ref[:, pl.ds(HD * h, HD)]
            doh = do_ref[:, pl.ds(HD * h, HD)]
            p, l = _xa_probs(qh, kh)
            pn = p * (1.0 / l)
            dp = lax.dot_general(doh, vh, (((1,), (1,)), ((), ())), preferred_element_type=F32)
            delta = jnp.sum(pn * dp, axis=-1, keepdims=True)
            dsc = (pn * (dp - delta)).astype(BF16)
            dq_ref[:, pl.ds(HD * h, HD)] = (jnp.dot(dsc, kh, preferred_element_type=F32) * XA_SCALE).astype(BF16)
            dkv_ref[:, pl.ds(HD * h, HD)] += lax.dot_general(
                dsc, qh, (((0,), (0,)), ((), ())), preferred_element_type=F32) * XA_SCALE
            dkv_ref[:, pl.ds(D + HD * h, HD)] += lax.dot_general(
                pn.astype(BF16), doh, (((0,), (0,)), ((), ())), preferred_element_type=F32)

    row = pl.BlockSpec((ts, D), lambda i: (i, 0))
    return pl.pallas_call(
        body, name=name, grid=(S // ts,),
        in_specs=[row, pl.BlockSpec((M, D), lambda i: (0, 0)), pl.BlockSpec((M, D), lambda i: (0, 1)), row],
        out_specs=[row, pl.BlockSpec((M, 2 * D), lambda i: (0, 0))],
        out_shape=[jax.ShapeDtypeStruct((S, D), BF16), jax.ShapeDtypeStruct((M, 2 * D), F32)],
        compiler_params=_cparams(("arbitrary",)),
    )(q, kv, kv, do)


def _adam_math(w, g, m, v):
    m = ADAM_B1 * m + (1.0 - ADAM_B1) * g
    v = ADAM_B2 * v + (1.0 - ADAM_B2) * (g * g)
    m_hat = m / (1.0 - ADAM_B1 ** ADAM_STEP)
    v_hat = v / (1.0 - ADAM_B2 ** ADAM_STEP)
    delta = -ADAM_LR * (m_hat / (jnp.sqrt(v_hat) + ADAM_EPS) + ADAM_WD * w)
    return delta, m, v


def _adamw(w, g, m, v, *, name):
    R, C = w.shape
    tr = _div_tile(R, max(8, (256 * 1024) // C // 8 * 8))

    def body(w_ref, g_ref, m_ref, v_ref, d_ref, nm_ref, nv_ref):
        d, nm, nv = _adam_math(w_ref[...], g_ref[...], m_ref[...], v_ref[...])
        d_ref[...] = d
        nm_ref[...] = nm
        nv_ref[...] = nv

    spec = pl.BlockSpec((tr, C), lambda i: (i, 0))
    sds = jax.ShapeDtypeStruct((R, C), F32)
    return pl.pallas_call(
        body, name=name, grid=(R // tr,), in_specs=[spec] * 4, out_specs=[spec] * 3, out_shape=[sds] * 3,
        compiler_params=_cparams(("parallel",)),
    )(w, g, m, v)


def _adamw_small(w, gparts, m, v, *, name):
    R, C = w.shape

    def body(w_ref, gp_ref, m_ref, v_ref, g_ref, d_ref, nm_ref, nv_ref):
        g = gp_ref[0]
        for k in range(1, N_DEV):
            g = g + gp_ref[k]
        d, nm, nv = _adam_math(w_ref[...], g, m_ref[...], v_ref[...])
        g_ref[...] = g
        d_ref[...] = d
        nm_ref[...] = nm
        nv_ref[...] = nv

    sds = jax.ShapeDtypeStruct((R, C), F32)
    return pl.pallas_call(body, name=name, out_shape=[sds] * 4, compiler_params=_cparams())(w, gparts, m, v)


ANY = pl.BlockSpec(memory_space=pl.ANY)


def _mesh_pos():
    return lax.axis_index("x"), lax.axis_index("y"), lax.axis_index("c")


def _other_chips(x, y):
    return [(1 - x, y), (x, 1 - y), (1 - x, 1 - y)]


LAYOUT = {'ffn1_w_gu': 'col', 'ffn1_w_down': 'stk', 'even_w_in': 'stk', 'even_w_out': 'stk', 'odd_w_in': 'col',
          'odd_w_out': 'stk', 'xa_wq': 'stk', 'xa_wkv': 'col', 'xa_wo': 'stk', 'ffn2_w_gu': 'col',
          'ffn2_w_down': 'stk', 'tiny': 'stk'}
COMM_NAMES = list(LAYOUT)
TINY_ROWS = 48


def _piece_rows(L, A):
    return A if L == 2 else A // 2


def _shard_piece(ref, L, A, h):
    if L == 2:
        return ref.at[h]
    return ref.at[0, pl.ds(pl.multiple_of(h * (A // 2), 8), A // 2)]


def _gathered_piece(ref, kind, L, A, h):
    if L == 2:
        return ref.at[h]
    rows = pl.ds(pl.multiple_of(h * (A // 2), 8), A // 2)
    return ref.at[0, rows] if kind == 'col' else ref.at[0, :, rows]


def _chip_part(piece, kind, B, s):
    if kind == 'col':
        return piece.at[:, pl.ds(pl.multiple_of(s * B, 128), B)]
    return piece.at[s]


def _place(shard, layer, kind, chip_idx, out_dtype, *, name):
    L, A, B = shard.shape
    ta = _div_tile(A, 256, 16)

    def body(s_ref, x_ref, o_ref):
        o_ref[...] = x_ref[...].astype(out_dtype)

    if kind == 'col':
        shape = (1, A, N_CHIPS * B)
        out_spec = pl.BlockSpec((None, ta, B), lambda i, s: (0, i, s[0]))
    else:
        shape = (1, N_CHIPS, A, B)
        out_spec = pl.BlockSpec((None, None, ta, B), lambda i, s: (0, s[0], i, 0))
    grid_spec = pltpu.PrefetchScalarGridSpec(
        num_scalar_prefetch=1, grid=(A // ta,),
        in_specs=[pl.BlockSpec((None, ta, B), lambda i, s: (layer, i, 0))], out_specs=out_spec)
    return pl.pallas_call(
        body, name=name, grid_spec=grid_spec, out_shape=jax.ShapeDtypeStruct(shape, out_dtype),
        compiler_params=_cparams(("parallel",)),
    )(chip_idx, shard)


HBM = pl.BlockSpec(memory_space=pltpu.HBM)
SEM = pl.BlockSpec(memory_space=pltpu.SEMAPHORE)
DATAFLOW = pltpu.SideEffectType.DATAFLOW_SIDE_EFFECTING


def _own_part_copies(refs, meta, send_sems, recv_sems):
    x, y, c = _mesh_pos()
    cps = []
    for k, (kind, L, A, B) in enumerate(meta):
        for j, (cx, cy) in enumerate(_other_chips(x, y)):
            part = _chip_part(refs[k].at[0], kind, B, 2 * x + y)
            cps.append(pltpu.make_async_remote_copy(
                src_ref=part, dst_ref=part, send_sem=send_sems.at[3 * k + j], recv_sem=recv_sems.at[3 * k + j],
                device_id=(cx, cy, c), device_id_type=MESH))
    return cps


def _gather_start(fulls, meta, after):
    n = len(fulls)

    def body(*refs):
        send_sems, recv_sems = refs[n + 1], refs[n + 2]
        outs = refs[n + 3:2 * n + 3]
        token = refs[2 * n + 3]
        for cp in _own_part_copies(outs, meta, send_sems, recv_sems):
            cp.start()
        token[...] = jnp.zeros_like(token)

    res = pl.pallas_call(
        body, name="ag_start_late", in_specs=[HBM] * n + [pl.BlockSpec(memory_space=pl.ANY)],
        out_specs=(SEM, SEM) + (HBM,) * n + (pl.BlockSpec(memory_space=pltpu.VMEM),),
        out_shape=(pltpu.SemaphoreType.DMA((3 * n,)), pltpu.SemaphoreType.DMA((3 * n,)))
        + tuple(pltpu.HBM(f.shape, f.dtype) for f in fulls) + (jax.ShapeDtypeStruct((8, 128), F32),),
        input_output_aliases={k: 2 + k for k in range(n)},
        compiler_params=pltpu.CompilerParams(has_side_effects=DATAFLOW),
    )(*[pltpu.with_memory_space_constraint(f, pltpu.HBM) for f in fulls], after)
    return res[0], res[1], list(res[2:2 + n]), res[2 + n]


def _gather_wait(send_sems, recv_sems, fulls, meta, after):
    n = len(fulls)

    def body(*refs):
        f_refs = refs[:n]
        send_sems, recv_sems = refs[n], refs[n + 1]
        for cp in _own_part_copies(f_refs, meta, send_sems, recv_sems):
            cp.wait_send()
            cp.wait_recv()

    return pl.pallas_call(
        body, name="ag_wait_late", in_specs=[HBM] * n + [SEM, SEM, pl.BlockSpec(memory_space=pl.ANY)],
        out_specs=[HBM] * n, out_shape=[pltpu.HBM(f.shape, f.dtype) for f in fulls],
        input_output_aliases={k: k for k in range(n)},
        compiler_params=pltpu.CompilerParams(has_side_effects=DATAFLOW),
    )(*fulls, send_sems, recv_sems, after)


def _all_gather(fulls, meta):
    n = len(fulls)

    def body(*refs):
        outs = refs[n:2 * n]
        send_sems, recv_sems = refs[2 * n:]
        x, y, c = _mesh_pos()
        sibling = (x, y, 1 - c)
        chips = _other_chips(x, y)

        def part(k, s, h):
            kind, L, A, B = meta[k]
            return _chip_part(_gathered_piece(outs[k], kind, L, A, h), kind, B, s)

        def copy(ref, sem, to):
            return pltpu.make_async_remote_copy(src_ref=ref, dst_ref=ref, send_sem=send_sems.at[sem],
                                                recv_sem=recv_sems.at[sem], device_id=to, device_id_type=MESH)

        started = []
        for k in range(n):
            for j, (cx, cy) in enumerate(chips):
                cp = copy(part(k, 2 * x + y, c), 3 * k + j, (cx, cy, c))
                cp.start()
                started.append(cp)
        for j, (cx, cy) in enumerate(chips):
            for k in range(n):
                landed = part(k, 2 * cx + cy, c)
                copy(landed, 3 * k + j, (cx, cy, c)).wait_recv()
                fwd = copy(landed, 3 * n + 3 * k + j, sibling)
                fwd.start()
                started.append(fwd)
        for j, (cx, cy) in enumerate(chips):
            for k in range(n):
                copy(part(k, 2 * cx + cy, 1 - c), 3 * n + 3 * k + j, sibling).wait_recv()
        for cp in started:
            cp.wait_send()

    return pl.pallas_call(
        body, name="ag_weights", in_specs=[ANY] * n, out_specs=[ANY] * n,
        out_shape=[jax.ShapeDtypeStruct(f.shape, f.dtype) for f in fulls],
        input_output_aliases={k: k for k in range(n)},
        scratch_shapes=[pltpu.SemaphoreType.DMA((6 * n,)), pltpu.SemaphoreType.DMA((6 * n,))],
    )(*fulls)


def _pair_exchange(gs, meta):
    n = len(gs)

    def body(*refs):
        g_refs, out_refs = refs[:n], refs[n:2 * n]
        send_sems, recv_sems = refs[2 * n:]
        x, y, c = _mesh_pos()
        cps = []
        for k in range(n):
            kind, L, A, B = meta[k]
            cp = pltpu.make_async_remote_copy(
                src_ref=_gathered_piece(g_refs[k], kind, L, A, 1 - c), dst_ref=out_refs[k],
                send_sem=send_sems.at[k], recv_sem=recv_sems.at[k], device_id=(x, y, 1 - c), device_id_type=MESH)
            cp.start()
            cps.append(cp)
        for cp in cps:
            cp.wait()

    shapes = []
    for g, (kind, L, A, B) in zip(gs, meta):
        ap = _piece_rows(L, A)
        shapes.append(jax.ShapeDtypeStruct((ap, N_CHIPS * B) if kind == 'col' else (N_CHIPS, ap, B), g.dtype))
    return pl.pallas_call(
        body, name="rs_pair_exchange", in_specs=[ANY] * n, out_specs=[ANY] * n, out_shape=shapes,
        scratch_shapes=[pltpu.SemaphoreType.DMA((n,)), pltpu.SemaphoreType.DMA((n,))],
    )(*gs)


def _pair_sum(g, got, m, c_idx, *, name):
    kind, L, A, B = m
    ap = _piece_rows(L, A)
    ta = _div_tile(ap, 256, 16)
    nt = ap // ta
    dt = g.dtype

    def body(c_ref, a_ref, b_ref, o_ref):
        o_ref[...] = (a_ref[...].astype(F32) + b_ref[...].astype(F32)).astype(dt)

    if kind == 'col':
        grid = (nt,)
        gmap = (lambda i, c: (c[0], i, 0)) if L == 2 else (lambda i, c: (0, c[0] * nt + i, 0))
        g_spec = pl.BlockSpec((None, ta, N_CHIPS * B), gmap)
        r_spec = pl.BlockSpec((ta, N_CHIPS * B), lambda i, c: (i, 0))
        shape = (ap, N_CHIPS * B)
        sem = ("parallel",)
    else:
        grid = (N_CHIPS, nt)
        gmap = (lambda s, i, c: (c[0], s, i, 0)) if L == 2 else (lambda s, i, c: (0, s, c[0] * nt + i, 0))
        g_spec = pl.BlockSpec((None, None, ta, B), gmap)
        r_spec = pl.BlockSpec((None, ta, B), lambda s, i, c: (s, i, 0))
        shape = (N_CHIPS, ap, B)
        sem = ("parallel", "parallel")
    grid_spec = pltpu.PrefetchScalarGridSpec(num_scalar_prefetch=1, grid=grid, in_specs=[g_spec, r_spec],
                                             out_specs=r_spec)
    return pl.pallas_call(
        body, name=name, grid_spec=grid_spec, out_shape=jax.ShapeDtypeStruct(shape, dt), compiler_params=_cparams(sem),
    )(c_idx, g, got)


def _chip_exchange(ps, meta):
    n = len(ps)

    def body(*refs):
        p_refs, out_refs = refs[:n], refs[n:2 * n]
        send_sems, recv_sems = refs[2 * n:]
        x, y, c = _mesh_pos()
        cps = []
        for k in range(n):
            kind, L, A, B = meta[k]
            for j, (cx, cy) in enumerate(_other_chips(x, y)):
                cp = pltpu.make_async_remote_copy(
                    src_ref=_chip_part(p_refs[k], kind, B, 2 * cx + cy), dst_ref=out_refs[k].at[j],
                    send_sem=send_sems.at[3 * k + j], recv_sem=recv_sems.at[3 * k + j], device_id=(cx, cy, c),
                    device_id_type=MESH)
                cp.start()
                cps.append(cp)
        for cp in cps:
            cp.wait()

    shapes = [jax.ShapeDtypeStruct((3, _piece_rows(L, A), B), p.dtype) for p, (kind, L, A, B) in zip(ps, meta)]
    return pl.pallas_call(
        body, name="rs_chip_exchange", in_specs=[ANY] * n, out_specs=[ANY] * n, out_shape=shapes,
        scratch_shapes=[pltpu.SemaphoreType.DMA((3 * n,)), pltpu.SemaphoreType.DMA((3 * n,))],
    )(*ps)


def _chip_sum(p, got, m, sc_idx, *, name):
    kind, L, A, B = m
    ap = _piece_rows(L, A)
    ta = _div_tile(ap, 256, 16)
    nt = ap // ta

    def body(r_ref, a_ref, b_ref, o_ref):
        acc = a_ref[...].astype(F32)
        for j in range(3):
            acc = acc + b_ref[j].astype(F32)
        o_ref[...] = acc

    if kind == 'col':
        p_spec = pl.BlockSpec((ta, B), lambda i, r: (i, r[0]))
    else:
        p_spec = pl.BlockSpec((None, ta, B), lambda i, r: (r[0], i, 0))
    omap = (lambda i, r: (r[1], i, 0)) if L == 2 else (lambda i, r: (0, r[1] * nt + i, 0))
    grid_spec = pltpu.PrefetchScalarGridSpec(
        num_scalar_prefetch=1, grid=(nt,),
        in_specs=[p_spec, pl.BlockSpec((3, ta, B), lambda i, r: (0, i, 0))],
        out_specs=pl.BlockSpec((None, ta, B), omap))
    return pl.pallas_call(
        body, name=name, grid_spec=grid_spec, out_shape=jax.ShapeDtypeStruct((L, A, B), F32),
        compiler_params=_cparams(("parallel",)),
    )(sc_idx, p, got)


def _final_exchange(gls, meta, small):
    n = len(gls)
    rs, cs = small.shape

    def body(*refs):
        small_ref = refs[n]
        outs = refs[n + 1:2 * n + 1]
        sm_ref = refs[2 * n + 1]
        send_sems, recv_sems, local_sem = refs[2 * n + 2:]
        x, y, c = _mesh_pos()
        me = 4 * x + 2 * y + c
        own_s = pltpu.make_async_copy(small_ref, sm_ref.at[me], local_sem)
        own_s.start()
        cps = []
        for k in range(n):
            kind, L, A, B = meta[k]
            half = _shard_piece(outs[k], L, A, c)
            cp = pltpu.make_async_remote_copy(src_ref=half, dst_ref=half, send_sem=send_sems.at[k],
                                              recv_sem=recv_sems.at[k], device_id=(x, y, 1 - c), device_id_type=MESH)
            cp.start()
            cps.append(cp)
        for r in range(1, N_DEV):
            fx, fy, fc = (r >> 2) & 1, (r >> 1) & 1, r & 1
            peer = (1 - x if fx else x, 1 - y if fy else y, 1 - c if fc else c)
            cp = pltpu.make_async_remote_copy(
                src_ref=small_ref, dst_ref=sm_ref.at[me], send_sem=send_sems.at[n + r], recv_sem=recv_sems.at[n + r],
                device_id=peer, device_id_type=MESH)
            cp.start()
            cps.append(cp)
        for cp in cps:
            cp.wait()
        own_s.wait()

    res = pl.pallas_call(
        body, name="rs_final_exchange", in_specs=[ANY] * (n + 1), out_specs=[ANY] * (n + 1),
        out_shape=[jax.ShapeDtypeStruct(g.shape, g.dtype) for g in gls] + [jax.ShapeDtypeStruct((N_DEV, rs, cs), F32)],
        input_output_aliases={k: k for k in range(n)},
        scratch_shapes=[pltpu.SemaphoreType.DMA((n + N_DEV,)), pltpu.SemaphoreType.DMA((n + N_DEV,)),
                        pltpu.SemaphoreType.DMA],
    )(*gls, small)
    return res[:n], res[n]


def _tiny_pack(conv_a_w, sc_conv_w):
    lead = conv_a_w.shape[:-2]
    sc = sc_conv_w.reshape(lead + (2 * SC_WIDTH, 128))
    z = lambda r: jnp.zeros(lead + (r, 128), F32)
    return jnp.concatenate([conv_a_w, z(32 - CONV_A_WIDTH), sc, z(TINY_ROWS - 32 - 2 * SC_WIDTH)], axis=-2)


def _tiny_unpack(t):
    lead = t.shape[:-2]
    return t[..., :CONV_A_WIDTH, :], t[..., 32:32 + 2 * SC_WIDTH, :].reshape(lead + (SC_WIDTH, 256))


def _pack_small(d):
    flat = jnp.concatenate([d[n].astype(F32).reshape(-1) for n in SMALL_NAMES])
    n = flat.shape[0]
    total = -(-n // 1024) * 1024
    return jnp.pad(flat, (0, total - n)).reshape(total // 128, 128)


def _unpack_small(packed, shapes):
    flat = packed.reshape(-1)
    out, off = {}, 0
    for n in SMALL_NAMES:
        sz = math.prod(shapes[n])
        out[n] = flat[off:off + sz].reshape(shapes[n])
        off += sz
    return out


def _ffn_fwd(h, g, W, n_gu, n_down, i, tag):
    h2, u, gate, up, a = _ffn_fwd_fused(h, g, W[n_gu][i], W[n_down][i], 0, name=f"{tag}_fwd")
    return h2, (h, u, gate, up, a)


def _ffn_bwd(dh, saved, g, W, n_gu, n_down, i, G, tag):
    h, u, gate, up, a = saved
    dh_in, dg, dgate, dup = _ffn_bwd_fused(dh, h, g, gate, up, W[n_gu][i], W[n_down][i], 0, name=f"{tag}_bwd")
    G[n_down] = _mm(a, dh, name=f"{tag}_b_wdown", ta=True, tm=1408, tn=1024, tk=1024, scale=0.5,
                    stack=(2, i, G.get(n_down)))
    tn = FFN_CHUNK
    half = _mm(u, dgate, name=f"{tag}_b_wg", ta=True, tm=1024, tn=tn, tk=2048, stack=(2, i, G.get(n_gu), 2 * D_FF))
    G[n_gu] = _mm(u, dup, name=f"{tag}_b_wu", ta=True, tm=1024, tn=tn, tk=2048, stack=(2, i, half, 2 * D_FF),
                  n_map=lambda j: j + D_FF // tn)
    return dh_in, dg


def _xa_block_fwd(h, mem, g, gm, W, i, tag):
    mn = _rms_fwd(mem, gm, name=f"{tag}_mem_norm")
    u, q = _norm_mm(h, g, W['xa_wq'][i], 0, name=f"{tag}_q", tn=1024)
    kv = _mm(mn, W['xa_wkv'][i], b_layer=0, name=f"{tag}_kv", tm=256, tn=1024, tk=1024)
    o = _xa_fwd(q, kv, name=f"{tag}_attn")
    h2 = _mm(o, W['xa_wo'][i], b_layer=0, name=f"{tag}_o", out_dtype=F32, tm=1024, tn=1024, tk=1024, res=h)
    return h2, (h, u, mn, q, kv, o)


def _xa_block_bwd(dh, saved, mem, g, gm, W, i, G, tag):
    h, u, mn, q, kv, o = saved
    do = _mm(dh, W['xa_wo'][i], b_layer=0, name=f"{tag}_b_do", tb=True, tm=1024, tn=1024, tk=1024)
    G['xa_wo'] = _mm(o, dh, name=f"{tag}_b_wo", ta=True, tm=1024, tn=1024, tk=1024, stack=(2, i, G.get('xa_wo')))
    dq, dkv = _xa_bwd(q, kv, do, name=f"{tag}_b_attn")
    G['xa_wq'] = _mm(u, dq, name=f"{tag}_b_wq", ta=True, tm=1024, tn=1024, tk=1024, stack=(2, i, G.get('xa_wq')))
    dh_in, dg = _mm_norm_bwd(dq, W['xa_wq'][i], 0, h, g, dh, name=f"{tag}_b_du", tk=1024)
    G['xa_wkv'] = _mm(mn, dkv, name=f"{tag}_b_wkv", ta=True, tm=1024, tn=1024, tk=256,
                      stack=(2, i, G.get('xa_wkv')))
    dmn = _mm(dkv, W['xa_wkv'][i], b_layer=0, name=f"{tag}_b_dmn", tb=True, out_dtype=F32, tm=256, tn=1024, tk=1024)
    _, dgm = _rms_bwd(mem, gm, dmn, None, name=f"{tag}_b_mem_norm")
    return dh_in, dg, dgm


def _pad_conv_w(w, rows):
    return jnp.pad(w.astype(F32), ((0, rows - w.shape[0]), (0, 0)))


def _even_fwd(h, g, W, conv_w, conv_b, ln_g, ln_b, sinks, tag):
    u, z = _norm_mm(h, g, W['even_w_in'][0], 0, name=f"{tag}_in", tn=1792)
    c, act = _conv_a_fwd(z, conv_w, conv_b, ln_g, ln_b, name=f"{tag}_conv")
    kpad = jnp.pad(z[:, 1536:1664], ((WINDOW, 0), (0, 0)))
    vpad = jnp.pad(z[:, 1664:1792], ((WINDOW, 0), (0, 0)))
    o = _swa_fwd(z, kpad, vpad, sinks, name=f"{tag}_swa")
    cat = jnp.concatenate([act, o], axis=-1)
    h2 = _mm(cat, W['even_w_out'][0], b_layer=0, name=f"{tag}_out", out_dtype=F32, tm=1024, tn=1024, tk=1024, res=h)
    return h2, (h, u, z, c, kpad, vpad, cat)


def _even_bwd(dh, saved, g, W, conv_w, ln_g, ln_b, sinks, G, tag):
    h, u, z, c, kpad, vpad, cat = saved
    dcat = _mm(dh, W['even_w_out'][0], b_layer=0, name=f"{tag}_b_dcat", tb=True, tm=1024, tn=1024, tk=1024)
    G['even_w_out'] = _mm(cat, dh, name=f"{tag}_b_wout", ta=True, tm=1024, tn=1024, tk=1024)
    dz_a, small = _conv_a_bwd(z, c, dcat, conv_w, ln_g, ln_b, name=f"{tag}_b_conv")
    dq, dkp, dvp, dsinks = _swa_bwd(z, kpad, vpad, sinks, dcat, name=f"{tag}_b_swa")
    dz = jnp.concatenate([dz_a, dq, dkp[WINDOW:].astype(BF16), dvp[WINDOW:].astype(BF16)], axis=-1)
    G['even_w_in'] = _mm(u, dz, name=f"{tag}_b_win", ta=True, tm=1024, tn=1792, tk=1024)
    dh_in, dg = _mm_norm_bwd(dz, W['even_w_in'][0], 0, h, g, dh, name=f"{tag}_b_du", tk=1792)
    grads = dict(mix=dg, conv_a_w=small[:CONV_A_WIDTH], conv_a_b=small[32:33], conv_a_ln_g=small[33:34],
                 conv_a_ln_b=small[34:35], swa_sinks=dsinks[:, 0])
    return dh_in, grads


def _odd_fwd(h, g, W, conv_w, tag):
    u, z = _norm_mm(h, g, W['odd_w_in'][0], 0, name=f"{tag}_in", tn=1024)
    y, cc = _sconv_fwd(z, conv_w, name=f"{tag}_conv")
    h2 = _mm(y, W['odd_w_out'][0], b_layer=0, name=f"{tag}_out", out_dtype=F32, tm=1024, tn=1024, tk=1024, res=h)
    return h2, (h, u, z, y, cc)


def _odd_bwd(dh, saved, g, W, conv_w, G, tag):
    h, u, z, y, cc = saved
    dy = _mm(dh, W['odd_w_out'][0], b_layer=0, name=f"{tag}_b_dy", tb=True, tm=1024, tn=1024, tk=1024)
    G['odd_w_out'] = _mm(y, dh, name=f"{tag}_b_wout", ta=True, tm=1024, tn=1024, tk=1024)
    dz, dw = _sconv_bwd(z, cc, dy, conv_w, name=f"{tag}_b_conv")
    G['odd_w_in'] = _mm(u, dz, name=f"{tag}_b_win", ta=True, tm=1024, tn=1024, tk=1024)
    dh_in, dg = _mm_norm_bwd(dz, W['odd_w_in'][0], 0, h, g, dh, name=f"{tag}_b_du", tk=1024)
    return dh_in, dict(mix=dg, sc_conv_w=dw[:SC_WIDTH])


def _local_step(x, mem, tgt, W, late_weights, token, conv_a_w, sc_conv_w, P):
    row = lambda v: v.reshape(1, -1)
    conv_a_w = _pad_conv_w(conv_a_w, 32)
    sc_w = _pad_conv_w(sc_conv_w, 8)
    sinks = P['swa_sinks'][0]

    h = x
    saved = []
    for i in range(2):
        t = f"l{i}"
        if i == 1:
            for n, ws in late_weights(h).items():
                W[n] = W.get(n, []) + ws
        g1 = row(P['ffn1_norm'][i]) + (token if i == 0 else 0.0)
        h, s1 = _ffn_fwd(h, g1, W, 'ffn1_w_gu', 'ffn1_w_down', i, f"{t}_ffn1")
        if i == 0:
            h, s2 = _even_fwd(h, row(P['mix_norm'][i]), W, conv_a_w, P['conv_a_b'], P['conv_a_ln_g'],
                              P['conv_a_ln_b'], sinks, f"{t}_even")
        else:
            h, s2 = _odd_fwd(h, row(P['mix_norm'][i]), W, sc_w, f"{t}_odd")
        h, s3 = _xa_block_fwd(h, mem, row(P['xa_norm'][i]), row(P['xa_mem_norm'][i]), W, i, f"{t}_xa")
        h, s4 = _ffn_fwd(h, row(P['ffn2_norm'][i]), W, 'ffn2_w_gu', 'ffn2_w_down', i, f"{t}_ffn2")
        saved.append((s1, s2, s3, s4))

    loss, dh, d_final = _final_loss(h, row(P['final_norm']), tgt, name="final_loss")

    G = {}
    gp = {n: [None, None] for n in ('ffn1_norm', 'mix_norm', 'xa_norm', 'xa_mem_norm', 'ffn2_norm')}
    single = {}
    for i in (1, 0):
        t = f"l{i}"
        s1, s2, s3, s4 = saved[i]
        dh, gp['ffn2_norm'][i] = _ffn_bwd(dh, s4, row(P['ffn2_norm'][i]), W, 'ffn2_w_gu', 'ffn2_w_down', i, G,
                                          f"{t}_ffn2")
        dh, gp['xa_norm'][i], gp['xa_mem_norm'][i] = _xa_block_bwd(
            dh, s3, mem, row(P['xa_norm'][i]), row(P['xa_mem_norm'][i]), W, i, G, f"{t}_xa")
        if i == 0:
            dh, g2 = _even_bwd(dh, s2, row(P['mix_norm'][i]), W, conv_a_w, P['conv_a_ln_g'], P['conv_a_ln_b'], sinks,
                               G, f"{t}_even")
        else:
            dh, g2 = _odd_bwd(dh, s2, row(P['mix_norm'][i]), W, sc_w, G, f"{t}_odd")
        gp['mix_norm'][i] = g2.pop('mix')
        single.update(g2)
        dh, gp['ffn1_norm'][i] = _ffn_bwd(dh, s1, row(P['ffn1_norm'][i]), W, 'ffn1_w_gu', 'ffn1_w_down', i, G,
                                          f"{t}_ffn1")

    small = {n: jnp.concatenate(v, axis=0) for n, v in gp.items()}
    small['conv_a_b'] = single['conv_a_b']
    small['conv_a_ln_g'] = single['conv_a_ln_g']
    small['conv_a_ln_b'] = single['conv_a_ln_b']
    small['swa_sinks'] = single['swa_sinks'][None]
    small['final_norm'] = d_final[0]
    small['conv_a_w'] = single['conv_a_w']
    small['sc_conv_w'] = single['sc_conv_w']
    return loss[0, 0], dh, G, small


def _as2d(a):
    return a.reshape(-1, a.shape[-1])


def kernel(x, mem, ffn1_norm, ffn1_w_gu, ffn1_w_down, mix_norm, even_w_in, conv_a_w, conv_a_b, conv_a_ln_g, conv_a_ln_b, swa_sinks, even_w_out, odd_w_in, sc_conv_w, odd_w_out, xa_norm, xa_mem_norm, xa_wq, xa_wkv, xa_wo, ffn2_norm, ffn2_w_gu, ffn2_w_down, final_norm, loss_target, m_ffn1_norm, m_ffn1_w_gu, m_ffn1_w_down, m_mix_norm, m_even_w_in, m_conv_a_w, m_conv_a_b, m_conv_a_ln_g, m_conv_a_ln_b, m_swa_sinks, m_even_w_out, m_odd_w_in, m_sc_conv_w, m_odd_w_out, m_xa_norm, m_xa_mem_norm, m_xa_wq, m_xa_wkv, m_xa_wo, m_ffn2_norm, m_ffn2_w_gu, m_ffn2_w_down, m_final_norm, v_ffn1_norm, v_ffn1_w_gu, v_ffn1_w_down, v_mix_norm, v_even_w_in, v_conv_a_w, v_conv_a_b, v_conv_a_ln_g, v_conv_a_ln_b, v_swa_sinks, v_even_w_out, v_odd_w_in, v_sc_conv_w, v_odd_w_out, v_xa_norm, v_xa_mem_norm, v_xa_wq, v_xa_wkv, v_xa_wo, v_ffn2_norm, v_ffn2_w_gu, v_ffn2_w_down, v_final_norm):
    w = dict(zip(WEIGHT_NAMES, (ffn1_norm, ffn1_w_gu, ffn1_w_down, mix_norm, even_w_in, conv_a_w, conv_a_b, conv_a_ln_g, conv_a_ln_b, swa_sinks, even_w_out, odd_w_in, sc_conv_w, odd_w_out, xa_norm, xa_mem_norm, xa_wq, xa_wkv, xa_wo, ffn2_norm, ffn2_w_gu, ffn2_w_down, final_norm)))
    m = dict(zip(WEIGHT_NAMES, (m_ffn1_norm, m_ffn1_w_gu, m_ffn1_w_down, m_mix_norm, m_even_w_in, m_conv_a_w, m_conv_a_b, m_conv_a_ln_g, m_conv_a_ln_b, m_swa_sinks, m_even_w_out, m_odd_w_in, m_sc_conv_w, m_odd_w_out, m_xa_norm, m_xa_mem_norm, m_xa_wq, m_xa_wkv, m_xa_wo, m_ffn2_norm, m_ffn2_w_gu, m_ffn2_w_down, m_final_norm)))
    v = dict(zip(WEIGHT_NAMES, (v_ffn1_norm, v_ffn1_w_gu, v_ffn1_w_down, v_mix_norm, v_even_w_in, v_conv_a_w, v_conv_a_b, v_conv_a_ln_g, v_conv_a_ln_b, v_swa_sinks, v_even_w_out, v_odd_w_in, v_sc_conv_w, v_odd_w_out, v_xa_norm, v_xa_mem_norm, v_xa_wq, v_xa_wkv, v_xa_wo, v_ffn2_norm, v_ffn2_w_gu, v_ffn2_w_down, v_final_norm)))
    small_shapes = {n: w[n].shape for n in SMALL_NAMES}
    cx, cy, cc = lax.axis_index("x"), lax.axis_index("y"), lax.axis_index("c")
    chip_idx = (2 * cx + cy).astype(jnp.int32).reshape(1)
    core_idx = cc.astype(jnp.int32).reshape(1)
    chip_core_idx = jnp.concatenate([chip_idx, core_idx])

    shards = {n: w[n] for n in COMM_NAMES if n != 'tiny'}
    shards['tiny'] = _tiny_pack(conv_a_w, sc_conv_w)
    meta = [(LAYOUT[n],) + shards[n].shape for n in COMM_NAMES]
    two_layer = [n for n in COMM_NAMES if shards[n].shape[0] == 2]
    early = [(n, 0) for n in two_layer] + [('even_w_in', 0), ('even_w_out', 0), ('tiny', 0)]
    late = [(n, 1) for n in two_layer] + [('odd_w_in', 0), ('odd_w_out', 0)]

    def place(items):
        return [_place(shards[n], l, LAYOUT[n], chip_idx, F32 if n == 'tiny' else BF16, name=f"place_{n}_{l}")
                for n, l in items]

    def item_meta(items):
        return [(LAYOUT[n], 1) + shards[n].shape[1:] for n, l in items]

    def natural(items, arrays):
        out = {}
        for (n, l), a in zip(items, arrays):
            if n == 'tiny':
                continue
            if n == 'even_w_in':
                out[n] = [a.transpose(0, 2, 1, 3).reshape(1, D_MODEL, -1)]
            else:
                out[n] = [a if LAYOUT[n] == 'col' else a.reshape(1, N_CHIPS * a.shape[2], a.shape[3])]
        return out

    early_full = _all_gather(place(early), item_meta(early))
    W = natural(early, early_full)
    ca, sc = _tiny_unpack(early_full[-1][0])
    conv_a_full = ca.transpose(1, 0, 2).reshape(CONV_A_WIDTH, CONV_A_CH)
    sc_full = sc.transpose(1, 0, 2).reshape(SC_WIDTH, SC_CH)
    send_sems, recv_sems, in_flight, token = _gather_start(place(late), item_meta(late), early_full[0])

    def late_weights(h):
        return natural(late, _gather_wait(send_sems, recv_sems, in_flight, item_meta(late), h))

    loss_part, grad_x, G, g_small = _local_step(x[0], mem[0], loss_target[0], W, late_weights, token[:1, :1],
                                                conv_a_full, sc_full, {n: w[n] for n in SMALL_NAMES})
    loss = lax.psum(loss_part, ("x", "y", "c"))

    gs = []
    for n, (kind, L, A, B) in zip(COMM_NAMES, meta):
        if n == 'tiny':
            g = _tiny_pack(g_small['conv_a_w'].reshape(CONV_A_WIDTH, N_CHIPS, 128).transpose(1, 0, 2),
                           g_small['sc_conv_w'].reshape(SC_WIDTH, N_CHIPS, 256).transpose(1, 0, 2))[None]
        elif n == 'even_w_in':
            g = G[n].reshape(A, N_CHIPS, B).transpose(1, 0, 2)[None]
        elif kind == 'col':
            g = G[n].reshape(L, A, N_CHIPS * B)
        else:
            g = G[n].reshape(L, N_CHIPS, A, B)
        gs.append(g)
    got = _pair_exchange(gs, meta)
    ps = [_pair_sum(g, r, m_, core_idx, name=f"rs_pair_sum_{n}") for n, g, r, m_ in zip(COMM_NAMES, gs, got, meta)]
    got2 = _chip_exchange(ps, meta)
    red = [_chip_sum(p, r, m_, chip_core_idx, name=f"rs_chip_sum_{n}")
           for n, p, r, m_ in zip(COMM_NAMES, ps, got2, meta)]
    g_shards, small_parts = _final_exchange(red, meta, _pack_small(g_small))
    g_local = dict(zip(COMM_NAMES, g_shards))
    g_local['conv_a_w'], g_local['sc_conv_w'] = _tiny_unpack(g_local.pop('tiny'))

    grads, deltas, new_m, new_v = {}, {}, {}, {}
    for n in BIG_NAMES:
        shp = w[n].shape
        d, nm, nv = _adamw(_as2d(w[n]), _as2d(g_local[n]), _as2d(m[n]), _as2d(v[n]), name=f"adamw_{n}")
        grads[n], deltas[n], new_m[n], new_v[n] = g_local[n], d.reshape(shp), nm.reshape(shp), nv.reshape(shp)
    gs, ds, ms, vs = _adamw_small(_pack_small({n: w[n] for n in SMALL_NAMES}), small_parts,
                                  _pack_small({n: m[n] for n in SMALL_NAMES}),
                                  _pack_small({n: v[n] for n in SMALL_NAMES}), name="adamw_small")
    for dst, packed in ((grads, gs), (deltas, ds), (new_m, ms), (new_v, vs)):
        dst.update(_unpack_small(packed, small_shapes))

    return (loss, grad_x[None], *[grads[n] for n in WEIGHT_NAMES], *[deltas[n] for n in WEIGHT_NAMES],
            *[new_m[n] for n in WEIGHT_NAMES], *[new_v[n] for n in WEIGHT_NAMES])
```

```python
import functools
import math

import jax
import jax.numpy as jnp
from jax import lax
from jax.experimental import pallas as pl
from jax.experimental.pallas import tpu as pltpu

F32 = jnp.float32
BF16 = jnp.bfloat16

D_MODEL = 1024
D_FF = 2816
CONV_A_CH = 512
CONV_A_WIDTH = 31
SWA_HEADS = 8
SWA_KV_HEADS = 2
SWA_GROUP = 4
HEAD_DIM = 64
WINDOW = 128
SC_CH = 1024
SC_WIDTH = 3
XA_HEADS = 4
XA_HEAD_DIM = 256
RMS_EPS = 1e-6
LN_EPS = 1e-5

ADAM_LR = 0.001
ADAM_B1 = 0.9
ADAM_B2 = 0.999
ADAM_EPS = 1e-08
ADAM_WD = 0.01
ADAM_STEP = 10

N_CHIPS = 4
N_DEV = 8
NEG_BIG = -1e30
VMEM_LIMIT = 56 * 1024 * 1024
MESH = pl.DeviceIdType.MESH

INPUT_NAMES = ['x', 'mem', 'ffn1_norm', 'ffn1_w_gu', 'ffn1_w_down', 'mix_norm', 'even_w_in', 'conv_a_w', 'conv_a_b',
               'conv_a_ln_g', 'conv_a_ln_b', 'swa_sinks', 'even_w_out', 'odd_w_in', 'sc_conv_w', 'odd_w_out', 'xa_norm',
               'xa_mem_norm', 'xa_wq', 'xa_wkv', 'xa_wo', 'ffn2_norm', 'ffn2_w_gu', 'ffn2_w_down', 'final_norm']
WEIGHT_NAMES = INPUT_NAMES[2:]
BIG = [('ffn1_w_gu', 'col'), ('ffn1_w_down', 'row'), ('even_w_in', 'col'), ('conv_a_w', 'col'), ('even_w_out', 'row'),
       ('odd_w_in', 'col'), ('sc_conv_w', 'col'), ('odd_w_out', 'row'), ('xa_wq', 'row'), ('xa_wkv', 'col'),
       ('xa_wo', 'row'), ('ffn2_w_gu', 'col'), ('ffn2_w_down', 'row')]
BIG_NAMES = [n for n, _ in BIG]
SMALL_NAMES = [n for n in WEIGHT_NAMES if n not in BIG_NAMES]


def _cparams(sem=None, vmem=VMEM_LIMIT):
    kw = dict(vmem_limit_bytes=vmem)
    if sem is not None:
        kw['dimension_semantics'] = sem
    return pltpu.CompilerParams(**kw)


def _div_tile(n, want, align=8):
    if n <= want:
        return n
    t = (want // align) * align
    while t >= align:
        if n % t == 0:
            return t
        t -= align
    return n


def _mm(a, b, *, name, ta=False, tb=False, out_dtype=BF16, tm=512, tn=512, tk=512, res=None, scale=1.0,
        b_layer=None, stack=None, n_map=None):
    n_map = n_map or (lambda j: j)
    if ta:
        K, M = a.shape
    else:
        M, K = a.shape
    if tb:
        N, K2 = b.shape[-2:]
    else:
        K2, N = b.shape[-2:]
    assert K == K2, (a.shape, b.shape, ta, tb)
    tm = _div_tile(M, tm, 128 if ta else 16)
    tn = _div_tile(N, tn, 128)
    tk = _div_tile(K, tk, 16 if ta else 128)
    nk = K // tk
    a_spec = pl.BlockSpec((tk, tm), lambda i, j, k: (k, i)) if ta else pl.BlockSpec((tm, tk), lambda i, j, k: (i, k))
    if b_layer is None:
        b_spec = pl.BlockSpec((tn, tk), lambda i, j, k: (j, k)) if tb else pl.BlockSpec((tk, tn), lambda i, j, k: (k, j))
    elif tb:
        b_spec = pl.BlockSpec((None, tn, tk), lambda i, j, k: (b_layer, j, k))
    else:
        b_spec = pl.BlockSpec((None, tk, tn), lambda i, j, k: (b_layer, k, j))
    o_spec = pl.BlockSpec((tm, tn), lambda i, j, k: (i, j))
    out_shape = jax.ShapeDtypeStruct((M, N), out_dtype)
    out_spec = o_spec
    aliases = {}
    extra_specs, extra_args = [], ()
    if stack is not None:
        n_layers, layer, buf = stack[:3]
        n_total = stack[3] if len(stack) > 3 else N
        out_shape = jax.ShapeDtypeStruct((n_layers, M, n_total), out_dtype)
        out_spec = pl.BlockSpec((None, tm, tn), lambda i, j, k: (layer, i, n_map(j)))
        if buf is not None:
            extra_specs, extra_args = [pl.BlockSpec(memory_space=pl.ANY)], (buf,)
            aliases = {2 + (res is not None): 0}
    dims = (((0 if ta else 1,), (1 if tb else 0,)), ((), ()))
    has_res = res is not None
    n_extra = len(extra_args)

    def body(*refs):
        if n_extra:
            refs = refs[:2 + has_res] + refs[2 + has_res + n_extra:]
        if has_res:
            a_ref, b_ref, r_ref, o_ref, acc_ref = refs
        else:
            a_ref, b_ref, o_ref, acc_ref = refs
        k = pl.program_id(2)
        p = lax.dot_general(a_ref[...].astype(BF16), b_ref[...].astype(BF16), dims, preferred_element_type=F32)

        @pl.when(k == 0)
        def _():
            acc_ref[...] = p

        @pl.when(k > 0)
        def _():
            acc_ref[...] += p

        @pl.when(k == nk - 1)
        def _():
            r = acc_ref[...] * scale
            if has_res:
                r = r_ref[...] + r
            o_ref[...] = r.astype(out_dtype)

    in_specs = [a_spec, b_spec] + ([o_spec] if has_res else []) + extra_specs
    args = (a, b) + ((res,) if has_res else ()) + extra_args
    return pl.pallas_call(
        body, name=name, grid=(M // tm, N // tn, nk), in_specs=in_specs, out_specs=out_spec,
        out_shape=out_shape, input_output_aliases=aliases,
        scratch_shapes=[pltpu.VMEM((tm, tn), F32)],
        compiler_params=_cparams(("parallel", "parallel", "arbitrary")),
    )(*args)


def _rms_fwd(x, g, *, name):
    S, D = x.shape
    ts = _div_tile(S, 512)

    def body(x_ref, g_ref, o_ref):
        xv = x_ref[...]
        r = lax.rsqrt(jnp.mean(xv * xv, axis=-1, keepdims=True) + RMS_EPS)
        o_ref[...] = (xv * r * g_ref[...]).astype(BF16)

    return pl.pallas_call(
        body, name=name, grid=(S // ts,),
        in_specs=[pl.BlockSpec((ts, D), lambda i: (i, 0)), pl.BlockSpec((1, D), lambda i: (0, 0))],
        out_specs=pl.BlockSpec((ts, D), lambda i: (i, 0)),
        out_shape=jax.ShapeDtypeStruct((S, D), BF16),
        compiler_params=_cparams(("parallel",)),
    )(x, g)


def _norm_mm(h, g, w, layer, *, name, tn):
    S, D = h.shape
    N = w.shape[-1]
    tm = _div_tile(S, 1024, 16)
    tn = _div_tile(N, tn, 128)

    def body(h_ref, g_ref, w_ref, u_ref, z_ref, u_s):
        @pl.when(pl.program_id(1) == 0)
        def _():
            xv = h_ref[...]
            r = lax.rsqrt(jnp.mean(xv * xv, axis=-1, keepdims=True) + RMS_EPS)
            u = (xv * r * g_ref[...]).astype(BF16)
            u_s[...] = u
            u_ref[...] = u

        z_ref[...] = jnp.dot(u_s[...], w_ref[...], preferred_element_type=F32).astype(BF16)

    row = pl.BlockSpec((tm, D), lambda i, j: (i, 0))
    return pl.pallas_call(
        body, name=name, grid=(S // tm, N // tn),
        in_specs=[row, pl.BlockSpec((1, D), lambda i, j: (0, 0)), pl.BlockSpec((None, D, tn), lambda i, j: (layer, 0, j))],
        out_specs=[row, pl.BlockSpec((tm, tn), lambda i, j: (i, j))],
        out_shape=[jax.ShapeDtypeStruct((S, D), BF16), jax.ShapeDtypeStruct((S, N), BF16)],
        scratch_shapes=[pltpu.VMEM((tm, D), BF16)],
        compiler_params=_cparams(("parallel", "arbitrary")),
    )(h, g, w)


def _mm_norm_bwd(dz, w, layer, h, g, dres, *, name, tk):
    S, K = dz.shape
    D = h.shape[1]
    tm = _div_tile(S, 512, 16)
    tk = _div_tile(K, tk, 128)
    nk = K // tk

    def body(dz_ref, w_ref, h_ref, g_ref, dr_ref, dx_ref, dg_ref, acc):
        i = pl.program_id(0)
        k = pl.program_id(1)
        p = lax.dot_general(dz_ref[...], w_ref[...], (((1,), (1,)), ((), ())), preferred_element_type=F32)

        @pl.when(k == 0)
        def _():
            acc[...] = p

        @pl.when(k > 0)
        def _():
            acc[...] += p

        @pl.when(k == nk - 1)
        def _():
            xv = h_ref[...]
            du = acc[...]
            r = lax.rsqrt(jnp.mean(xv * xv, axis=-1, keepdims=True) + RMS_EPS)
            xhat = xv * r
            part = jnp.sum(du * xhat, axis=0, keepdims=True)

            @pl.when(i == 0)
            def _():
                dg_ref[...] = part

            @pl.when(i > 0)
            def _():
                dg_ref[...] += part

            dxhat = du * g_ref[...]
            dx_ref[...] = dr_ref[...] + r * (dxhat - xhat * jnp.mean(dxhat * xhat, axis=-1, keepdims=True))

    row = pl.BlockSpec((tm, D), lambda i, k: (i, 0))
    vec = pl.BlockSpec((1, D), lambda i, k: (0, 0))
    return pl.pallas_call(
        body, name=name, grid=(S // tm, nk),
        in_specs=[pl.BlockSpec((tm, tk), lambda i, k: (i, k)), pl.BlockSpec((None, D, tk), lambda i, k: (layer, 0, k)),
                  row, vec, row],
        out_specs=[row, vec],
        out_shape=[jax.ShapeDtypeStruct((S, D), F32), jax.ShapeDtypeStruct((1, D), F32)],
        scratch_shapes=[pltpu.VMEM((tm, D), F32)],
        compiler_params=_cparams(("arbitrary", "arbitrary")),
    )(dz, w, h, g, dres)


def _rms_bwd(x, g, du, dres, *, name):
    S, D = x.shape
    ts = _div_tile(S, 512)
    has_res = dres is not None

    def body(*refs):
        if has_res:
            x_ref, g_ref, du_ref, dr_ref, dx_ref, dg_ref = refs
        else:
            x_ref, g_ref, du_ref, dg_ref = refs
        i = pl.program_id(0)
        xv = x_ref[...]
        duv = du_ref[...].astype(F32)
        r = lax.rsqrt(jnp.mean(xv * xv, axis=-1, keepdims=True) + RMS_EPS)
        xhat = xv * r
        part = jnp.sum(duv * xhat, axis=0, keepdims=True)

        @pl.when(i == 0)
        def _():
            dg_ref[...] = part

        @pl.when(i > 0)
        def _():
            dg_ref[...] += part

        if has_res:
            dxhat = duv * g_ref[...]
            dx = r * (dxhat - xhat * jnp.mean(dxhat * xhat, axis=-1, keepdims=True))
            dx_ref[...] = dr_ref[...] + dx

    row = pl.BlockSpec((ts, D), lambda i: (i, 0))
    vec = pl.BlockSpec((1, D), lambda i: (0, 0))
    if has_res:
        dx, dg = pl.pallas_call(
            body, name=name, grid=(S // ts,), in_specs=[row, vec, row, row], out_specs=[row, vec],
            out_shape=[jax.ShapeDtypeStruct((S, D), F32), jax.ShapeDtypeStruct((1, D), F32)],
            compiler_params=_cparams(("arbitrary",)),
        )(x, g, du, dres)
        return dx, dg
    dg = pl.pallas_call(
        body, name=name, grid=(S // ts,), in_specs=[row, vec, row], out_specs=vec,
        out_shape=jax.ShapeDtypeStruct((1, D), F32),
        compiler_params=_cparams(("arbitrary",)),
    )(x, g, du)
    return None, dg


def _final_loss(h, g, tgt, *, name):
    S, D = h.shape
    ts = _div_tile(S, 512)

    def body(h_ref, g_ref, t_ref, loss_ref, dh_ref, dg_ref):
        i = pl.program_id(0)
        xv = h_ref[...]
        gv = g_ref[...]
        r = lax.rsqrt(jnp.mean(xv * xv, axis=-1, keepdims=True) + RMS_EPS)
        xhat = xv * r
        err = xhat * gv - t_ref[...]
        lpart = 0.5 * jnp.sum(jnp.mean(err * err, axis=-1, keepdims=True), axis=0, keepdims=True)
        dy = err * (1.0 / D)
        gpart = jnp.sum(dy * xhat, axis=0, keepdims=True)

        @pl.when(i == 0)
        def _():
            loss_ref[...] = jnp.broadcast_to(lpart, loss_ref.shape)
            dg_ref[...] = gpart

        @pl.when(i > 0)
        def _():
            loss_ref[...] += jnp.broadcast_to(lpart, loss_ref.shape)
            dg_ref[...] += gpart

        dxhat = dy * gv
        dh_ref[...] = r * (dxhat - xhat * jnp.mean(dxhat * xhat, axis=-1, keepdims=True))

    row = pl.BlockSpec((ts, D), lambda i: (i, 0))
    vec = pl.BlockSpec((1, D), lambda i: (0, 0))
    return pl.pallas_call(
        body, name=name, grid=(S // ts,), in_specs=[row, vec, row],
        out_specs=[pl.BlockSpec((8, 128), lambda i: (0, 0)), row, vec],
        out_shape=[jax.ShapeDtypeStruct((8, 128), F32), jax.ShapeDtypeStruct((S, D), F32),
                   jax.ShapeDtypeStruct((1, D), F32)],
        compiler_params=_cparams(("arbitrary",)),
    )(h, g, tgt)


def _sigmoid(x):
    return 1.0 / (1.0 + jnp.exp(-x))


FFN_CHUNK = 1408
FFN_CHUNKS = D_FF // FFN_CHUNK
FFN_BWD_PIECE = 768
FFN_BWD_SLAB = 256


def _ffn_fwd_fused(h, g, w_gu, w_down, layer, *, name):
    S, D = h.shape
    tm = _div_tile(S, 512, 16)
    tf, nj = FFN_CHUNK, FFN_CHUNKS

    def body(h_ref, g_ref, wg_ref, wu_ref, wd_ref, h2_ref, u_ref, gate_ref, up_ref, a_ref, u_s, acc):
        j = pl.program_id(1)

        @pl.when(j == 0)
        def _():
            xv = h_ref[...]
            r = lax.rsqrt(jnp.mean(xv * xv, axis=-1, keepdims=True) + RMS_EPS)
            u = (xv * r * g_ref[...]).astype(BF16)
            u_s[...] = u
            u_ref[...] = u

        u = u_s[...]
        gate = jnp.dot(u, wg_ref[...], preferred_element_type=F32)
        up = jnp.dot(u, wu_ref[...], preferred_element_type=F32)
        gate_ref[...] = gate.astype(BF16)
        up_ref[...] = up.astype(BF16)
        a = (gate * _sigmoid(gate) * up).astype(BF16)
        a_ref[...] = a
        p = jnp.dot(a, wd_ref[...], preferred_element_type=F32)

        @pl.when(j == 0)
        def _():
            acc[...] = p

        @pl.when(j > 0)
        def _():
            acc[...] += p

        @pl.when(j == nj - 1)
        def _():
            h2_ref[...] = h_ref[...] + 0.5 * acc[...]

    row = pl.BlockSpec((tm, D), lambda i, j: (i, 0))
    chunk = pl.BlockSpec((tm, tf), lambda i, j: (i, j))
    hidden = jax.ShapeDtypeStruct((S, D_FF), BF16)
    return pl.pallas_call(
        body, name=name, grid=(S // tm, nj),
        in_specs=[row, pl.BlockSpec((1, D), lambda i, j: (0, 0)),
                  pl.BlockSpec((None, D, tf), lambda i, j: (layer, 0, j)),
                  pl.BlockSpec((None, D, tf), lambda i, j: (layer, 0, nj + j)),
                  pl.BlockSpec((None, tf, D), lambda i, j: (layer, j, 0))],
        out_specs=[row, row, chunk, chunk, chunk],
        out_shape=[jax.ShapeDtypeStruct((S, D), F32), jax.ShapeDtypeStruct((S, D), BF16), hidden, hidden, hidden],
        scratch_shapes=[pltpu.VMEM((tm, D), BF16), pltpu.VMEM((tm, D), F32)],
        compiler_params=_cparams(("parallel", "arbitrary")),
    )(h, g, w_gu, w_gu, w_down)


def _ffn_bwd_fused(dh, h, g, gate, up, w_gu, w_down, layer, *, name):
    S, D = h.shape
    tm = _div_tile(S, 512, FFN_BWD_SLAB)
    tf = FFN_CHUNK
    nj = D_FF // tf
    slab = min(FFN_BWD_SLAB, tm)
    nt = (((1,), (1,)), ((), ()))
    pieces = [(c0, min(FFN_BWD_PIECE, tf - c0)) for c0 in range(0, tf, FFN_BWD_PIECE)]

    def body(dh_ref, h_ref, g_ref, gate_ref, up_ref, wg_ref, wu_ref, wd_ref, dx_ref, dg_ref, dgate_ref, dup_ref,
             dy_s, acc):
        i = pl.program_id(0)
        j = pl.program_id(1)

        @pl.when(j == 0)
        def _():
            for r0 in range(0, tm, slab):
                rows = pl.ds(r0, slab)
                dy_s[rows, :] = (0.5 * dh_ref[rows, :]).astype(BF16)

        p = None
        for c0, cw in pieces:
            cols = pl.ds(c0, cw)
            da = lax.dot_general(dy_s[...], wd_ref[cols, :], nt, preferred_element_type=F32)
            gt = gate_ref[:, cols].astype(F32)
            sg = _sigmoid(gt)
            dgate = (da * up_ref[:, cols].astype(F32) * sg * (1.0 + gt * (1.0 - sg))).astype(BF16)
            dup = (da * gt * sg).astype(BF16)
            dgate_ref[:, cols] = dgate
            dup_ref[:, cols] = dup
            q = (lax.dot_general(dgate, wg_ref[:, cols], nt, preferred_element_type=F32)
                 + lax.dot_general(dup, wu_ref[:, cols], nt, preferred_element_type=F32))
            p = q if p is None else p + q

        @pl.when(j == 0)
        def _():
            acc[...] = p

        @pl.when(j > 0)
        def _():
            acc[...] += p

        @pl.when(j == nj - 1)
        def _():
            part = jnp.zeros((1, D), F32)
            for r0 in range(0, tm, slab):
                rows = pl.ds(r0, slab)
                xv = h_ref[rows, :]
                du = acc[rows, :]
                r = lax.rsqrt(jnp.mean(xv * xv, axis=-1, keepdims=True) + RMS_EPS)
                xhat = xv * r
                part = part + jnp.sum(du * xhat, axis=0, keepdims=True)
                dxhat = du * g_ref[...]
                dx_ref[rows, :] = dh_ref[rows, :] + r * (
                    dxhat - xhat * jnp.mean(dxhat * xhat, axis=-1, keepdims=True))

            @pl.when(i == 0)
            def _():
                dg_ref[...] = part

            @pl.when(i > 0)
            def _():
                dg_ref[...] += part

    row = pl.BlockSpec((tm, D), lambda i, j: (i, 0))
    vec = pl.BlockSpec((1, D), lambda i, j: (0, 0))
    chunk = pl.BlockSpec((tm, tf), lambda i, j: (i, j))
    hidden = jax.ShapeDtypeStruct((S, D_FF), BF16)
    return pl.pallas_call(
        body, name=name, grid=(S // tm, nj),
        in_specs=[row, row, vec, chunk, chunk,
                  pl.BlockSpec((None, D, tf), lambda i, j: (layer, 0, j)),
                  pl.BlockSpec((None, D, tf), lambda i, j: (layer, 0, nj + j)),
                  pl.BlockSpec((None, tf, D), lambda i, j: (layer, j, 0))],
        out_specs=[row, vec, chunk, chunk],
        out_shape=[jax.ShapeDtypeStruct((S, D), F32), jax.ShapeDtypeStruct((1, D), F32), hidden, hidden],
        scratch_shapes=[pltpu.VMEM((tm, D), BF16), pltpu.VMEM((tm, D), F32)],
        compiler_params=_cparams(("arbitrary", "arbitrary")),
    )(dh, h, g, gate, up, w_gu, w_gu, w_down)


CONV_HALO = 32
CONV_SUB_ROWS = 128


def _conv_a_fwd(z, w, bias, ln_g, ln_b, *, name):
    S = z.shape[0]
    C = CONV_A_CH
    ts = _div_tile(S, 256, 32)

    def body(val_ref, gate_ref, w_ref, b_ref, g_ref, lb_ref, c_ref, act_ref, win):
        i = pl.program_id(0)

        @pl.when(i == 0)
        def _():
            win[pl.ds(0, CONV_HALO), :] = jnp.zeros((CONV_HALO, C), F32)

        @pl.when(i > 0)
        def _():
            win[pl.ds(0, CONV_HALO), :] = win[pl.ds(ts, CONV_HALO), :]

        a = val_ref[...].astype(F32) * _sigmoid(gate_ref[...].astype(F32))
        win[pl.ds(CONV_HALO, ts), :] = a
        rs = min(CONV_SUB_ROWS, ts)
        for cb in range(C // 128):
            lanes = pl.ds(128 * cb, 128)
            for rt in range(ts // rs):
                sub = jnp.broadcast_to(b_ref[:, lanes], (rs, 128))
                for k in range(CONV_A_WIDTH):
                    sub = sub + w_ref[pl.ds(k, 1), lanes] * win[
                        pl.ds(CONV_HALO - (CONV_A_WIDTH - 1) + k + rs * rt, rs), lanes]
                c_ref[pl.ds(rs * rt, rs), lanes] = sub
        acc = c_ref[...]
        mu = jnp.mean(acc, axis=-1, keepdims=True)
        xc = acc - mu
        var = jnp.mean(xc * xc, axis=-1, keepdims=True)
        ln = xc * lax.rsqrt(var + LN_EPS) * g_ref[...] + lb_ref[...]
        act_ref[...] = (ln * _sigmoid(ln)).astype(BF16)

    row = lambda col: pl.BlockSpec((ts, C), lambda i, col=col: (i, col))
    vec = pl.BlockSpec((1, C), lambda i: (0, 0))
    return pl.pallas_call(
        body, name=name, grid=(S // ts,),
        in_specs=[row(0), row(1), pl.BlockSpec((32, C), lambda i: (0, 0)), vec, vec, vec],
        out_specs=[row(0), row(0)],
        out_shape=[jax.ShapeDtypeStruct((S, C), F32), jax.ShapeDtypeStruct((S, C), BF16)],
        scratch_shapes=[pltpu.VMEM((ts + CONV_HALO, C), F32)],
        compiler_params=_cparams(("arbitrary",)),
    )(z, z, w, bias, ln_g, ln_b)


def _conv_a_bwd(z, c, dcat, w, ln_g, ln_b, *, name):
    S = z.shape[0]
    C = CONV_A_CH
    ts = _div_tile(S, 256, 32)
    n = S // ts

    def body(val_ref, gate_ref, c_ref, da_ref, w_ref, g_ref, lb_ref, dz_ref, small_ref, win, a_s, da_s, dw8):
        i = pl.program_id(0)

        @pl.when(i == 0)
        def _():
            win[pl.ds(ts, CONV_HALO), :] = jnp.zeros((CONV_HALO, C), F32)
            small_ref[...] = jnp.zeros(small_ref.shape, F32)
            dw8[...] = jnp.zeros(dw8.shape, F32)

        @pl.when(i > 0)
        def _():
            win[pl.ds(ts, CONV_HALO), :] = win[pl.ds(0, CONV_HALO), :]

        cv = c_ref[...]
        gv = g_ref[...]
        mu = jnp.mean(cv, axis=-1, keepdims=True)
        xc = cv - mu
        var = jnp.mean(xc * xc, axis=-1, keepdims=True)
        rstd = lax.rsqrt(var + LN_EPS)
        xhat = xc * rstd
        ln = xhat * gv + lb_ref[...]
        sg = _sigmoid(ln)
        dln = da_ref[...].astype(F32) * (sg * (1.0 + ln * (1.0 - sg)))
        small_ref[pl.ds(33, 1), :] += jnp.sum(dln * xhat, axis=0, keepdims=True)
        small_ref[pl.ds(34, 1), :] += jnp.sum(dln, axis=0, keepdims=True)
        dxhat = dln * gv
        dc = rstd * (dxhat - jnp.mean(dxhat, axis=-1, keepdims=True)
                     - xhat * jnp.mean(dxhat * xhat, axis=-1, keepdims=True))
        small_ref[pl.ds(32, 1), :] += jnp.sum(dc, axis=0, keepdims=True)
        win[pl.ds(0, ts), :] = dc

        val = val_ref[...].astype(F32)
        sgg = _sigmoid(gate_ref[...].astype(F32))
        a_s[...] = val * sgg
        rs = min(CONV_SUB_ROWS, ts)
        for cb in range(C // 128):
            lanes = pl.ds(128 * cb, 128)
            for rt in range(ts // rs):
                a_sub = a_s[pl.ds(rs * rt, rs), lanes]
                da = jnp.zeros((rs, 128), F32)
                for k in range(CONV_A_WIDTH):
                    sh = win[pl.ds(CONV_A_WIDTH - 1 - k + rs * rt, rs), lanes]
                    da = da + w_ref[pl.ds(k, 1), lanes] * sh
                    prod = a_sub * sh
                    part = prod[0:8]
                    for r in range(1, rs // 8):
                        part = part + prod[8 * r:8 * r + 8]
                    dw8[pl.ds(8 * k, 8), lanes] += part
                da_s[pl.ds(rs * rt, rs), lanes] = da
        da = da_s[...]
        dz_ref[:, pl.ds(0, C)] = (da * sgg).astype(BF16)
        dz_ref[:, pl.ds(C, C)] = (da * val * sgg * (1.0 - sgg)).astype(BF16)

        @pl.when(i == n - 1)
        def _():
            for k in range(CONV_A_WIDTH):
                small_ref[pl.ds(k, 1), :] = jnp.sum(dw8[pl.ds(8 * k, 8), :], axis=0, keepdims=True)

    row = lambda col: pl.BlockSpec((ts, C), lambda i, col=col: (n - 1 - i, col))
    vec = pl.BlockSpec((1, C), lambda i: (0, 0))
    return pl.pallas_call(
        body, name=name, grid=(n,),
        in_specs=[row(0), row(1), row(0), row(0), pl.BlockSpec((32, C), lambda i: (0, 0)), vec, vec],
        out_specs=[pl.BlockSpec((ts, 2 * C), lambda i: (n - 1 - i, 0)), pl.BlockSpec((40, C), lambda i: (0, 0))],
        out_shape=[jax.ShapeDtypeStruct((S, 2 * C), BF16), jax.ShapeDtypeStruct((40, C), F32)],
        scratch_shapes=[pltpu.VMEM((ts + CONV_HALO, C), F32), pltpu.VMEM((ts, C), F32), pltpu.VMEM((ts, C), F32),
                        pltpu.VMEM((8 * 32, C), F32)],
        compiler_params=_cparams(("arbitrary",)),
    )(z, z, c, dcat, w, ln_g, ln_b)


SC_HALO = 8


def _sconv_fwd(z, w, *, name):
    S = z.shape[0]
    C = SC_CH
    ts = _div_tile(S, 256, 16)

    def body(gb_ref, gc_ref, v_ref, w_ref, y_ref, cc_ref, win):
        i = pl.program_id(0)

        @pl.when(i == 0)
        def _():
            win[pl.ds(0, SC_HALO), :] = jnp.zeros((SC_HALO, C), F32)

        @pl.when(i > 0)
        def _():
            win[pl.ds(0, SC_HALO), :] = win[pl.ds(ts, SC_HALO), :]

        win[pl.ds(SC_HALO, ts), :] = gc_ref[...].astype(F32) * v_ref[...].astype(F32)
        acc = jnp.zeros((ts, C), F32)
        for k in range(SC_WIDTH):
            acc = acc + w_ref[pl.ds(k, 1), :] * win[pl.ds(SC_HALO - (SC_WIDTH - 1) + k, ts), :]
        cc_ref[...] = acc.astype(BF16)
        y_ref[...] = (gb_ref[...].astype(F32) * acc).astype(BF16)

    row = lambda col: pl.BlockSpec((ts, C), lambda i, col=col: (i, col))
    return pl.pallas_call(
        body, name=name, grid=(S // ts,),
        in_specs=[row(0), row(1), row(2), pl.BlockSpec((8, C), lambda i: (0, 0))],
        out_specs=[row(0), row(0)],
        out_shape=[jax.ShapeDtypeStruct((S, C), BF16), jax.ShapeDtypeStruct((S, C), BF16)],
        scratch_shapes=[pltpu.VMEM((ts + SC_HALO, C), F32)],
        compiler_params=_cparams(("arbitrary",)),
    )(z, z, z, w)


def _sconv_bwd(z, cc, dy, w, *, name):
    S = z.shape[0]
    C = SC_CH
    ts = _div_tile(S, 256, 16)
    n = S // ts

    def body(gb_ref, gc_ref, v_ref, cc_ref, dy_ref, w_ref, dz_ref, dw_ref, win):
        i = pl.program_id(0)

        @pl.when(i == 0)
        def _():
            win[pl.ds(ts, SC_HALO), :] = jnp.zeros((SC_HALO, C), F32)
            dw_ref[...] = jnp.zeros(dw_ref.shape, F32)

        @pl.when(i > 0)
        def _():
            win[pl.ds(ts, SC_HALO), :] = win[pl.ds(0, SC_HALO), :]

        dyv = dy_ref[...].astype(F32)
        gb = gb_ref[...].astype(F32)
        gc = gc_ref[...].astype(F32)
        val = v_ref[...].astype(F32)
        dz_ref[:, pl.ds(0, C)] = (dyv * cc_ref[...].astype(F32)).astype(BF16)
        win[pl.ds(0, ts), :] = dyv * gb
        cv = gc * val
        dcv = jnp.zeros((ts, C), F32)
        for k in range(SC_WIDTH):
            sh = win[pl.ds(SC_WIDTH - 1 - k, ts), :]
            dcv = dcv + w_ref[pl.ds(k, 1), :] * sh
            dw_ref[pl.ds(k, 1), :] += jnp.sum(cv * sh, axis=0, keepdims=True)
        dz_ref[:, pl.ds(C, C)] = (dcv * val).astype(BF16)
        dz_ref[:, pl.ds(2 * C, C)] = (dcv * gc).astype(BF16)

    row = lambda col: pl.BlockSpec((ts, C), lambda i, col=col: (n - 1 - i, col))
    return pl.pallas_call(
        body, name=name, grid=(n,),
        in_specs=[row(0), row(1), row(2), row(0), row(0), pl.BlockSpec((8, C), lambda i: (0, 0))],
        out_specs=[pl.BlockSpec((ts, 3 * C), lambda i: (n - 1 - i, 0)), pl.BlockSpec((8, C), lambda i: (0, 0))],
        out_shape=[jax.ShapeDtypeStruct((S, 3 * C), BF16), jax.ShapeDtypeStruct((8, C), F32)],
        scratch_shapes=[pltpu.VMEM((ts + SC_HALO, C), F32)],
        compiler_params=_cparams(("arbitrary",)),
    )(z, z, z, cc, dy, w)


SWA_Q_COL = 2
SWA_SLOPES = [2.0 ** (-8.0 * (h + 1) / SWA_HEADS) for h in range(SWA_HEADS)]
SWA_SCALE = HEAD_DIM ** -0.5


SWA_GROUP_ROWS = SWA_GROUP * WINDOW


def _swa_masks():
    shape = (SWA_GROUP_ROWS, 2 * WINDOW)
    ii = lax.broadcasted_iota(jnp.int32, shape, 0)
    jj = lax.broadcasted_iota(jnp.int32, shape, 1)
    dist = (ii & (WINDOW - 1)) + WINDOW - jj
    valid = (dist >= 0) & (dist < WINDOW)
    grp = lax.broadcasted_iota(jnp.int32, (SWA_GROUP_ROWS, 1), 0) // WINDOW
    return dist.astype(F32), valid, jj, grp


def _by_group(grp, vals):
    out = jnp.full(grp.shape, vals[SWA_GROUP - 1], F32)
    for g in range(SWA_GROUP - 2, -1, -1):
        out = jnp.where(grp == g, vals[g], out)
    return out


def _stack_heads(ref, rows, kv):
    return jnp.concatenate([ref[rows, pl.ds(HEAD_DIM * (kv * SWA_GROUP + g), HEAD_DIM)] for g in range(SWA_GROUP)],
                           axis=0)


def _swa_probs(qg, kk, sink, slope, distf, valid):
    s = lax.dot_general(qg, kk, (((1,), (1,)), ((), ())), preferred_element_type=F32) * SWA_SCALE
    s = s - slope * distf
    s = jnp.where(valid, s, NEG_BIG)
    m = jnp.maximum(jnp.max(s, axis=-1, keepdims=True), sink)
    p = jnp.exp(s - m)
    l = jnp.sum(p, axis=-1, keepdims=True) + jnp.exp(sink - m)
    return p, m, l


def _swa_fwd(z, kpad, vpad, sinks, *, name):
    S = z.shape[0]
    tq = _div_tile(S, 256, 128)
    nblk = tq // WINDOW
    W = WINDOW

    def body(sink_ref, q_ref, k_ref, v_ref, o_ref):
        i = pl.program_id(0)
        distf, valid0, jj, grp = _swa_masks()
        for kv in range(SWA_KV_HEADS):
            heads = range(kv * SWA_GROUP, (kv + 1) * SWA_GROUP)
            sink = _by_group(grp, [sink_ref[h] for h in heads])
            slope = _by_group(grp, [SWA_SLOPES[h] for h in heads])
            for b in range(nblk):
                nb = i * nblk + b
                start = pl.multiple_of(nb * W, W)
                rows = pl.ds(W * b, W)
                valid = valid0 & ((jj >= W) | (nb > 0))
                kk = k_ref[pl.ds(start, 2 * W), pl.ds(HEAD_DIM * kv, HEAD_DIM)]
                vv = v_ref[pl.ds(start, 2 * W), pl.ds(HEAD_DIM * kv, HEAD_DIM)]
                p, m, l = _swa_probs(_stack_heads(q_ref, rows, kv), kk, sink, slope, distf, valid)
                o = (jnp.dot(p.astype(BF16), vv, preferred_element_type=F32) / l).astype(BF16)
                for g, h in enumerate(heads):
                    o_ref[rows, pl.ds(HEAD_DIM * h, HEAD_DIM)] = o[W * g:W * (g + 1)]

    full = pl.BlockSpec((S + W, 2 * HEAD_DIM), lambda i: (0, 0))
    return pl.pallas_call(
        body, name=name, grid=(S // tq,),
        in_specs=[pl.BlockSpec(memory_space=pltpu.SMEM), pl.BlockSpec((tq, 512), lambda i: (i, SWA_Q_COL)), full, full],
        out_specs=pl.BlockSpec((tq, 512), lambda i: (i, 0)),
        out_shape=jax.ShapeDtypeStruct((S, 512), BF16),
        compiler_params=_cparams(("parallel",)),
    )(sinks, z, kpad, vpad)


def _swa_bwd(z, kpad, vpad, sinks, dcat, *, name):
    S = z.shape[0]
    tq = _div_tile(S, 256, 128)
    nblk = tq // WINDOW
    W = WINDOW

    def body(sink_ref, q_ref, k_ref, v_ref, do_ref, dq_ref, dk_ref, dv_ref, ds_ref):
        i = pl.program_id(0)

        @pl.when(i == 0)
        def _():
            dk_ref[...] = jnp.zeros(dk_ref.shape, F32)
            dv_ref[...] = jnp.zeros(dv_ref.shape, F32)
            ds_ref[...] = jnp.zeros(ds_ref.shape, F32)

        distf, valid0, jj, grp = _swa_masks()
        tn = (((0,), (0,)), ((), ()))
        for kv in range(SWA_KV_HEADS):
            heads = range(kv * SWA_GROUP, (kv + 1) * SWA_GROUP)
            sink = _by_group(grp, [sink_ref[h] for h in heads])
            slope = _by_group(grp, [SWA_SLOPES[h] for h in heads])
            for b in range(nblk):
                nb = i * nblk + b
                start = pl.multiple_of(nb * W, W)
                rows = pl.ds(W * b, W)
                valid = valid0 & ((jj >= W) | (nb > 0))
                kk = k_ref[pl.ds(start, 2 * W), pl.ds(HEAD_DIM * kv, HEAD_DIM)]
                vv = v_ref[pl.ds(start, 2 * W), pl.ds(HEAD_DIM * kv, HEAD_DIM)]
                qg = _stack_heads(q_ref, rows, kv)
                dog = _stack_heads(do_ref, rows, kv)
                p, m, l = _swa_probs(qg, kk, sink, slope, distf, valid)
                inv_l = 1.0 / l
                pn = p * inv_l
                dp = lax.dot_general(dog, vv, (((1,), (1,)), ((), ())), preferred_element_type=F32)
                delta = jnp.sum(pn * dp, axis=-1, keepdims=True)
                dsc = (pn * (dp - delta)).astype(BF16)
                dsink = jnp.exp(sink - m) * inv_l * delta
                dq = (jnp.dot(dsc, kk, preferred_element_type=F32) * SWA_SCALE).astype(BF16)
                for g, h in enumerate(heads):
                    ds_ref[pl.ds(h, 1), :] += jnp.broadcast_to(
                        -jnp.sum(dsink[W * g:W * (g + 1)], axis=0, keepdims=True), (1, 128))
                    dq_ref[rows, pl.ds(HEAD_DIM * h, HEAD_DIM)] = dq[W * g:W * (g + 1)]
                dk_ref[pl.ds(start, 2 * W), pl.ds(HEAD_DIM * kv, HEAD_DIM)] += lax.dot_general(
                    dsc, qg, tn, preferred_element_type=F32) * SWA_SCALE
                dv_ref[pl.ds(start, 2 * W), pl.ds(HEAD_DIM * kv, HEAD_DIM)] += lax.dot_general(
                    pn.astype(BF16), dog, tn, preferred_element_type=F32)

    full = pl.BlockSpec((S + W, 2 * HEAD_DIM), lambda i: (0, 0))
    return pl.pallas_call(
        body, name=name, grid=(S // tq,),
        in_specs=[pl.BlockSpec(memory_space=pltpu.SMEM), pl.BlockSpec((tq, 512), lambda i: (i, SWA_Q_COL)), full, full,
                  pl.BlockSpec((tq, 512), lambda i: (i, 1))],
        out_specs=[pl.BlockSpec((tq, 512), lambda i: (i, 0)), full, full, pl.BlockSpec((8, 128), lambda i: (0, 0))],
        out_shape=[jax.ShapeDtypeStruct((S, 512), BF16), jax.ShapeDtypeStruct((S + W, 2 * HEAD_DIM), F32),
                   jax.ShapeDtypeStruct((S + W, 2 * HEAD_DIM), F32), jax.ShapeDtypeStruct((8, 128), F32)],
        compiler_params=_cparams(("arbitrary",)),
    )(sinks, z, kpad, vpad, dcat)


XA_SCALE = XA_HEAD_DIM ** -0.5


def _xa_probs(qh, kh):
    s = lax.dot_general(qh, kh, (((1,), (1,)), ((), ())), preferred_element_type=F32) * XA_SCALE
    m = jnp.max(s, axis=-1, keepdims=True)
    p = jnp.exp(s - m)
    return p, jnp.sum(p, axis=-1, keepdims=True)


def _xa_fwd(q, kv, *, name):
    S, D = q.shape
    M = kv.shape[0]
    ts = _div_tile(S, 512, 16)
    HD = XA_HEAD_DIM

    def body(q_ref, k_ref, v_ref, o_ref):
        for h in range(XA_HEADS):
            qh = q_ref[:, pl.ds(HD * h, HD)]
            p, l = _xa_probs(qh, k_ref[:, pl.ds(HD * h, HD)])
            o = jnp.dot(p.astype(BF16), v_ref[:, pl.ds(HD * h, HD)], preferred_element_type=F32) / l
            o_ref[:, pl.ds(HD * h, HD)] = o.astype(BF16)

    return pl.pallas_call(
        body, name=name, grid=(S // ts,),
        in_specs=[pl.BlockSpec((ts, D), lambda i: (i, 0)), pl.BlockSpec((M, D), lambda i: (0, 0)),
                  pl.BlockSpec((M, D), lambda i: (0, 1))],
        out_specs=pl.BlockSpec((ts, D), lambda i: (i, 0)),
        out_shape=jax.ShapeDtypeStruct((S, D), BF16),
        compiler_params=_cparams(("parallel",)),
    )(q, kv, kv)


def _xa_bwd(q, kv, do, *, name):
    S, D = q.shape
    M = kv.shape[0]
    ts = _div_tile(S, 512, 16)
    HD = XA_HEAD_DIM

    def body(q_ref, k_ref, v_ref, do_ref, dq_ref, dkv_ref):
        i = pl.program_id(0)

        @pl.when(i == 0)
        def _():
            dkv_ref[...] = jnp.zeros(dkv_ref.shape, F32)

        for h in range(XA_HEADS):
            qh = q_ref[:, pl.ds(HD * h, HD)]
            kh = k_ref[:, pl.ds(HD * h, HD)]
            vh = v_ref[:, pl.ds(HD * h, HD)]
            doh = do_ref[:, pl.ds(HD * h, HD)]
            p, l = _xa_probs(qh, kh)
            pn = p * (1.0 / l)
            dp = lax.dot_general(doh, vh, (((1,), (1,)), ((), ())), preferred_element_type=F32)
            delta = jnp.sum(pn * dp, axis=-1, keepdims=True)
            dsc = (pn * (dp - delta)).astype(BF16)
            dq_ref[:, pl.ds(HD * h, HD)] = (jnp.dot(dsc, kh, preferred_element_type=F32) * XA_SCALE).astype(BF16)
            dkv_ref[:, pl.ds(HD * h, HD)] += lax.dot_general(
                dsc, qh, (((0,), (0,)), ((), ())), preferred_element_type=F32) * XA_SCALE
            dkv_ref[:, pl.ds(D + HD * h, HD)] += lax.dot_general(
                pn.astype(BF16), doh, (((0,), (0,)), ((), ())), preferred_element_type=F32)

    row = pl.BlockSpec((ts, D), lambda i: (i, 0))
    return pl.pallas_call(
        body, name=name, grid=(S // ts,),
        in_specs=[row, pl.BlockSpec((M, D), lambda i: (0, 0)), pl.BlockSpec((M, D), lambda i: (0, 1)), row],
        out_specs=[row, pl.BlockSpec((M, 2 * D), lambda i: (0, 0))],
        out_shape=[jax.ShapeDtypeStruct((S, D), BF16), jax.ShapeDtypeStruct((M, 2 * D), F32)],
        compiler_params=_cparams(("arbitrary",)),
    )(q, kv, kv, do)


def _adam_math(w, g, m, v):
    m = ADAM_B1 * m + (1.0 - ADAM_B1) * g
    v = ADAM_B2 * v + (1.0 - ADAM_B2) * (g * g)
    m_hat = m / (1.0 - ADAM_B1 ** ADAM_STEP)
    v_hat = v / (1.0 - ADAM_B2 ** ADAM_STEP)
    delta = -ADAM_LR * (m_hat / (jnp.sqrt(v_hat) + ADAM_EPS) + ADAM_WD * w)
    return delta, m, v


def _adamw(w, g, m, v, *, name):
    R, C = w.shape
    tr = _div_tile(R, max(8, (256 * 1024) // C // 8 * 8))

    def body(w_ref, g_ref, m_ref, v_ref, d_ref, nm_ref, nv_ref):
        d, nm, nv = _adam_math(w_ref[...], g_ref[...], m_ref[...], v_ref[...])
        d_ref[...] = d
        nm_ref[...] = nm
        nv_ref[...] = nv

    spec = pl.BlockSpec((tr, C), lambda i: (i, 0))
    sds = jax.ShapeDtypeStruct((R, C), F32)
    return pl.pallas_call(
        body, name=name, grid=(R // tr,), in_specs=[spec] * 4, out_specs=[spec] * 3, out_shape=[sds] * 3,
        compiler_params=_cparams(("parallel",)),
    )(w, g, m, v)


def _adamw_layers(w, m, v, gsrc, *, name):
    L, A, B = w.shape
    tr = _div_tile(A, max(8, (256 * 1024) // B // 8 * 8))
    nt = A // tr
    flat = [a for srcs in gsrc for a in srcs]
    owner = [l for l, srcs in enumerate(gsrc) for _ in srcs]
    ng = len(flat)

    def body(*refs):
        w_ref, m_ref, v_ref = refs[:3]
        g_refs = refs[3:3 + ng]
        g_ref, d_ref, nm_ref, nv_ref = refs[3 + ng:]
        layer = pl.program_id(0)
        g = None
        for l in range(L):
            gl = None
            for a_ref, o in zip(g_refs, owner):
                if o == l:
                    gl = a_ref[...] if gl is None else gl + a_ref[...]
            g = gl if g is None else jnp.where(layer == l, gl, g)
        d, nm, nv = _adam_math(w_ref[...], g, m_ref[...], v_ref[...])
        g_ref[...] = g
        d_ref[...] = d
        nm_ref[...] = nm
        nv_ref[...] = nv

    def src_spec(o):
        return pl.BlockSpec((None, tr, B),
                            lambda l, i: (0, jnp.where(l == o, i, jnp.where(l > o, nt - 1, 0)), 0))

    spec = pl.BlockSpec((None, tr, B), lambda l, i: (l, i, 0))
    sds = jax.ShapeDtypeStruct((L, A, B), F32)
    return pl.pallas_call(
        body, name=name, grid=(L, nt), in_specs=[spec] * 3 + [src_spec(o) for o in owner], out_specs=[spec] * 4,
        out_shape=[sds] * 4, compiler_params=_cparams(("arbitrary", "arbitrary")),
    )(w, m, v, *flat)


def _adamw_small(w, gparts, m, v, *, name):
    R, C = w.shape

    def body(w_ref, gp_ref, m_ref, v_ref, g_ref, d_ref, nm_ref, nv_ref):
        g = gp_ref[0]
        for k in range(1, N_DEV):
            g = g + gp_ref[k]
        d, nm, nv = _adam_math(w_ref[...], g, m_ref[...], v_ref[...])
        g_ref[...] = g
        d_ref[...] = d
        nm_ref[...] = nm
        nv_ref[...] = nv

    sds = jax.ShapeDtypeStruct((R, C), F32)
    return pl.pallas_call(body, name=name, out_shape=[sds] * 4, compiler_params=_cparams())(w, gparts, m, v)


ANY = pl.BlockSpec(memory_space=pl.ANY)


def _mesh_pos():
    return lax.axis_index("x"), lax.axis_index("y"), lax.axis_index("c")


def _other_chips(x, y):
    return [(1 - x, y), (x, 1 - y), (1 - x, 1 - y)]


LAYOUT = {'ffn1_w_gu': 'col', 'ffn1_w_down': 'stk', 'even_w_in': 'stk', 'even_w_out': 'stk', 'odd_w_in': 'col',
          'odd_w_out': 'stk', 'xa_wq': 'stk', 'xa_wkv': 'col', 'xa_wo': 'stk', 'ffn2_w_gu': 'col',
          'ffn2_w_down': 'stk', 'tiny': 'stk'}
COMM_NAMES = list(LAYOUT)
TINY_ROWS = 48


def _piece_rows(L, A):
    return A if L == 2 else A // 2


def _shard_piece(ref, L, A, h):
    if L == 2:
        return ref.at[h]
    return ref.at[0, pl.ds(pl.multiple_of(h * (A // 2), 8), A // 2)]


def _gathered_piece(ref, kind, L, A, h):
    if L == 2:
        return ref.at[h]
    rows = pl.ds(pl.multiple_of(h * (A // 2), 8), A // 2)
    return ref.at[0, rows] if kind == 'col' else ref.at[0, :, rows]


def _chip_part(piece, kind, B, s):
    if kind == 'col':
        return piece.at[:, pl.ds(pl.multiple_of(s * B, 128), B)]
    return piece.at[s]


def _place(shard, layer, kind, chip_idx, out_dtype, *, name):
    L, A, B = shard.shape
    ta = _div_tile(A, 256, 16)

    def body(s_ref, x_ref, o_ref):
        o_ref[...] = x_ref[...].astype(out_dtype)

    if kind == 'col':
        shape = (1, A, N_CHIPS * B)
        out_spec = pl.BlockSpec((None, ta, B), lambda i, s: (0, i, s[0]))
    else:
        shape = (1, N_CHIPS, A, B)
        out_spec = pl.BlockSpec((None, None, ta, B), lambda i, s: (0, s[0], i, 0))
    grid_spec = pltpu.PrefetchScalarGridSpec(
        num_scalar_prefetch=1, grid=(A // ta,),
        in_specs=[pl.BlockSpec((None, ta, B), lambda i, s: (layer, i, 0))], out_specs=out_spec)
    return pl.pallas_call(
        body, name=name, grid_spec=grid_spec, out_shape=jax.ShapeDtypeStruct(shape, out_dtype),
        compiler_params=_cparams(("parallel",)),
    )(chip_idx, shard)


HBM = pl.BlockSpec(memory_space=pltpu.HBM)
SEM = pl.BlockSpec(memory_space=pltpu.SEMAPHORE)
DATAFLOW = pltpu.SideEffectType.DATAFLOW_SIDE_EFFECTING


def _own_part_copies(refs, meta, send_sems, recv_sems):
    x, y, c = _mesh_pos()
    cps = []
    for k, (kind, L, A, B) in enumerate(meta):
        for j, (cx, cy) in enumerate(_other_chips(x, y)):
            part = _chip_part(refs[k].at[0], kind, B, 2 * x + y)
            cps.append(pltpu.make_async_remote_copy(
                src_ref=part, dst_ref=part, send_sem=send_sems.at[3 * k + j], recv_sem=recv_sems.at[3 * k + j],
                device_id=(cx, cy, c), device_id_type=MESH))
    return cps


def _gather_start(fulls, meta, after):
    n = len(fulls)

    def body(*refs):
        send_sems, recv_sems = refs[n + 1], refs[n + 2]
        outs = refs[n + 3:2 * n + 3]
        token = refs[2 * n + 3]
        for cp in _own_part_copies(outs, meta, send_sems, recv_sems):
            cp.start()
        token[...] = jnp.zeros_like(token)

    res = pl.pallas_call(
        body, name="ag_start_late", in_specs=[HBM] * n + [pl.BlockSpec(memory_space=pl.ANY)],
        out_specs=(SEM, SEM) + (HBM,) * n + (pl.BlockSpec(memory_space=pltpu.VMEM),),
        out_shape=(pltpu.SemaphoreType.DMA((3 * n,)), pltpu.SemaphoreType.DMA((3 * n,)))
        + tuple(pltpu.HBM(f.shape, f.dtype) for f in fulls) + (jax.ShapeDtypeStruct((8, 128), F32),),
        input_output_aliases={k: 2 + k for k in range(n)},
        compiler_params=pltpu.CompilerParams(has_side_effects=DATAFLOW),
    )(*[pltpu.with_memory_space_constraint(f, pltpu.HBM) for f in fulls], after)
    return res[0], res[1], list(res[2:2 + n]), res[2 + n]


def _gather_wait(send_sems, recv_sems, fulls, meta, after):
    n = len(fulls)

    def body(*refs):
        f_refs = refs[:n]
        send_sems, recv_sems = refs[n], refs[n + 1]
        for cp in _own_part_copies(f_refs, meta, send_sems, recv_sems):
            cp.wait_send()
            cp.wait_recv()

    return pl.pallas_call(
        body, name="ag_wait_late", in_specs=[HBM] * n + [SEM, SEM, pl.BlockSpec(memory_space=pl.ANY)],
        out_specs=[HBM] * n, out_shape=[pltpu.HBM(f.shape, f.dtype) for f in fulls],
        input_output_aliases={k: k for k in range(n)},
        compiler_params=pltpu.CompilerParams(has_side_effects=DATAFLOW),
    )(*fulls, send_sems, recv_sems, after)


def _scatter_copies(g_refs, land_refs, meta, send_sems, recv_sems):
    x, y, c = _mesh_pos()
    cps = []
    for k, (kind, L, A, B) in enumerate(meta):
        for j, (cx, cy) in enumerate(_other_chips(x, y)):
            cps.append(pltpu.make_async_remote_copy(
                src_ref=_chip_part(g_refs[k].at[0], kind, B, 2 * cx + cy), dst_ref=land_refs[k].at[j],
                send_sem=send_sems.at[3 * k + j], recv_sem=recv_sems.at[3 * k + j], device_id=(cx, cy, c),
                device_id_type=MESH))
    return cps


def _scatter_start(gs, meta, after):
    n = len(gs)

    def body(*refs):
        send_sems, recv_sems = refs[2 * n + 1], refs[2 * n + 2]
        g_out = refs[2 * n + 3:3 * n + 3]
        lands = refs[3 * n + 3:4 * n + 3]
        token = refs[4 * n + 3]
        for cp in _scatter_copies(g_out, lands, meta, send_sems, recv_sems):
            cp.start()
        token[...] = jnp.zeros_like(token)

    land_shapes = [(3, A, B) for kind, L, A, B in meta]
    lands = [pltpu.with_memory_space_constraint(lax.empty(s, g.dtype), pltpu.HBM) for s, g in zip(land_shapes, gs)]
    res = pl.pallas_call(
        body, name="rs_start_early", in_specs=[HBM] * (2 * n) + [pl.BlockSpec(memory_space=pl.ANY)],
        out_specs=(SEM, SEM) + (HBM,) * (2 * n) + (pl.BlockSpec(memory_space=pltpu.VMEM),),
        out_shape=(pltpu.SemaphoreType.DMA((3 * n,)), pltpu.SemaphoreType.DMA((3 * n,)))
        + tuple(pltpu.HBM(g.shape, g.dtype) for g in gs)
        + tuple(pltpu.HBM(s, g.dtype) for s, g in zip(land_shapes, gs)) + (jax.ShapeDtypeStruct((8, 128), F32),),
        input_output_aliases={k: 2 + k for k in range(2 * n)},
        compiler_params=pltpu.CompilerParams(has_side_effects=DATAFLOW),
    )(*[pltpu.with_memory_space_constraint(g, pltpu.HBM) for g in gs], *lands, after)
    return res[0], res[1], list(res[2:2 + n]), list(res[2 + n:2 + 2 * n]), res[2 + 2 * n]


def _scatter_wait(send_sems, recv_sems, gs, lands, meta, after):
    n = len(gs)

    def body(*refs):
        g_refs, land_refs = refs[:n], refs[n:2 * n]
        send_sems, recv_sems = refs[2 * n], refs[2 * n + 1]
        for cp in _scatter_copies(g_refs, land_refs, meta, send_sems, recv_sems):
            cp.wait_send()
            cp.wait_recv()

    both = list(gs) + list(lands)
    res = pl.pallas_call(
        body, name="rs_wait_early", in_specs=[HBM] * (2 * n) + [SEM, SEM, pl.BlockSpec(memory_space=pl.ANY)],
        out_specs=[HBM] * (2 * n), out_shape=[pltpu.HBM(a.shape, a.dtype) for a in both],
        input_output_aliases={k: k for k in range(2 * n)},
        compiler_params=pltpu.CompilerParams(has_side_effects=DATAFLOW),
    )(*both, send_sems, recv_sems, after)
    return list(res[:n]), list(res[n:])


def _chip_sum_full(g, got, m, chip_idx, *, name):
    kind, L, A, B = m
    ta = _div_tile(A, 256, 16)

    def body(r_ref, a_ref, b_ref, o_ref):
        acc = a_ref[...].astype(F32)
        for j in range(3):
            acc = acc + b_ref[j].astype(F32)
        o_ref[...] = acc

    if kind == 'col':
        g_spec = pl.BlockSpec((None, ta, B), lambda i, r: (0, i, r[0]))
    else:
        g_spec = pl.BlockSpec((None, None, ta, B), lambda i, r: (0, r[0], i, 0))
    grid_spec = pltpu.PrefetchScalarGridSpec(
        num_scalar_prefetch=1, grid=(A // ta,),
        in_specs=[g_spec, pl.BlockSpec((3, ta, B), lambda i, r: (0, i, 0))],
        out_specs=pl.BlockSpec((None, ta, B), lambda i, r: (0, i, 0)))
    return pl.pallas_call(
        body, name=name, grid_spec=grid_spec, out_shape=jax.ShapeDtypeStruct((1, A, B), F32),
        compiler_params=_cparams(("parallel",)),
    )(chip_idx, g, got)


def _all_gather(fulls, meta):
    n = len(fulls)

    def body(*refs):
        outs = refs[n:2 * n]
        send_sems, recv_sems = refs[2 * n:]
        x, y, c = _mesh_pos()
        sibling = (x, y, 1 - c)
        chips = _other_chips(x, y)

        def part(k, s, h):
            kind, L, A, B = meta[k]
            return _chip_part(_gathered_piece(outs[k], kind, L, A, h), kind, B, s)

        def copy(ref, sem, to):
            return pltpu.make_async_remote_copy(src_ref=ref, dst_ref=ref, send_sem=send_sems.at[sem],
                                                recv_sem=recv_sems.at[sem], device_id=to, device_id_type=MESH)

        started = []
        for k in range(n):
            for j, (cx, cy) in enumerate(chips):
                cp = copy(part(k, 2 * x + y, c), 3 * k + j, (cx, cy, c))
                cp.start()
                started.append(cp)
        for j, (cx, cy) in enumerate(chips):
            for k in range(n):
                landed = part(k, 2 * cx + cy, c)
                copy(landed, 3 * k + j, (cx, cy, c)).wait_recv()
                fwd = copy(landed, 3 * n + 3 * k + j, sibling)
                fwd.start()
                started.append(fwd)
        for j, (cx, cy) in enumerate(chips):
            for k in range(n):
                copy(part(k, 2 * cx + cy, 1 - c), 3 * n + 3 * k + j, sibling).wait_recv()
        for cp in started:
            cp.wait_send()

    return pl.pallas_call(
        body, name="ag_weights", in_specs=[ANY] * n, out_specs=[ANY] * n,
        out_shape=[jax.ShapeDtypeStruct(f.shape, f.dtype) for f in fulls],
        input_output_aliases={k: k for k in range(n)},
        scratch_shapes=[pltpu.SemaphoreType.DMA((6 * n,)), pltpu.SemaphoreType.DMA((6 * n,))],
    )(*fulls)


def _pair_exchange(gs, meta):
    n = len(gs)

    def body(*refs):
        g_refs, out_refs = refs[:n], refs[n:2 * n]
        send_sems, recv_sems = refs[2 * n:]
        x, y, c = _mesh_pos()
        cps = []
        for k in range(n):
            kind, L, A, B = meta[k]
            cp = pltpu.make_async_remote_copy(
                src_ref=_gathered_piece(g_refs[k], kind, L, A, 1 - c), dst_ref=out_refs[k],
                send_sem=send_sems.at[k], recv_sem=recv_sems.at[k], device_id=(x, y, 1 - c), device_id_type=MESH)
            cp.start()
            cps.append(cp)
        for cp in cps:
            cp.wait()

    shapes = []
    for g, (kind, L, A, B) in zip(gs, meta):
        ap = _piece_rows(L, A)
        shapes.append(jax.ShapeDtypeStruct((ap, N_CHIPS * B) if kind == 'col' else (N_CHIPS, ap, B), g.dtype))
    return pl.pallas_call(
        body, name="rs_pair_exchange", in_specs=[ANY] * n, out_specs=[ANY] * n, out_shape=shapes,
        scratch_shapes=[pltpu.SemaphoreType.DMA((n,)), pltpu.SemaphoreType.DMA((n,))],
    )(*gs)


def _pair_sum(g, got, m, c_idx, *, name):
    kind, L, A, B = m
    ap = _piece_rows(L, A)
    ta = _div_tile(ap, 256, 16)
    nt = ap // ta
    dt = g.dtype

    def body(c_ref, a_ref, b_ref, o_ref):
        o_ref[...] = (a_ref[...].astype(F32) + b_ref[...].astype(F32)).astype(dt)

    if kind == 'col':
        grid = (nt,)
        gmap = (lambda i, c: (c[0], i, 0)) if L == 2 else (lambda i, c: (0, c[0] * nt + i, 0))
        g_spec = pl.BlockSpec((None, ta, N_CHIPS * B), gmap)
        r_spec = pl.BlockSpec((ta, N_CHIPS * B), lambda i, c: (i, 0))
        shape = (ap, N_CHIPS * B)
        sem = ("parallel",)
    else:
        grid = (N_CHIPS, nt)
        gmap = (lambda s, i, c: (c[0], s, i, 0)) if L == 2 else (lambda s, i, c: (0, s, c[0] * nt + i, 0))
        g_spec = pl.BlockSpec((None, None, ta, B), gmap)
        r_spec = pl.BlockSpec((None, ta, B), lambda s, i, c: (s, i, 0))
        shape = (N_CHIPS, ap, B)
        sem = ("parallel", "parallel")
    grid_spec = pltpu.PrefetchScalarGridSpec(num_scalar_prefetch=1, grid=grid, in_specs=[g_spec, r_spec],
                                             out_specs=r_spec)
    return pl.pallas_call(
        body, name=name, grid_spec=grid_spec, out_shape=jax.ShapeDtypeStruct(shape, dt), compiler_params=_cparams(sem),
    )(c_idx, g, got)


def _chip_exchange(ps, meta):
    n = len(ps)

    def body(*refs):
        p_refs, out_refs = refs[:n], refs[n:2 * n]
        send_sems, recv_sems = refs[2 * n:]
        x, y, c = _mesh_pos()
        cps = []
        for k in range(n):
            kind, L, A, B = meta[k]
            for j, (cx, cy) in enumerate(_other_chips(x, y)):
                cp = pltpu.make_async_remote_copy(
                    src_ref=_chip_part(p_refs[k], kind, B, 2 * cx + cy), dst_ref=out_refs[k].at[j],
                    send_sem=send_sems.at[3 * k + j], recv_sem=recv_sems.at[3 * k + j], device_id=(cx, cy, c),
                    device_id_type=MESH)
                cp.start()
                cps.append(cp)
        for cp in cps:
            cp.wait()

    shapes = [jax.ShapeDtypeStruct((3, _piece_rows(L, A), B), p.dtype) for p, (kind, L, A, B) in zip(ps, meta)]
    return pl.pallas_call(
        body, name="rs_chip_exchange", in_specs=[ANY] * n, out_specs=[ANY] * n, out_shape=shapes,
        scratch_shapes=[pltpu.SemaphoreType.DMA((3 * n,)), pltpu.SemaphoreType.DMA((3 * n,))],
    )(*ps)


def _chip_sum(p, got, m, sc_idx, *, name):
    kind, L, A, B = m
    ap = _piece_rows(L, A)
    ta = _div_tile(ap, 256, 16)
    nt = ap // ta

    def body(r_ref, a_ref, b_ref, o_ref):
        acc = a_ref[...].astype(F32)
        for j in range(3):
            acc = acc + b_ref[j].astype(F32)
        o_ref[...] = acc

    if kind == 'col':
        p_spec = pl.BlockSpec((ta, B), lambda i, r: (i, r[0]))
    else:
        p_spec = pl.BlockSpec((None, ta, B), lambda i, r: (r[0], i, 0))
    omap = (lambda i, r: (r[1], i, 0)) if L == 2 else (lambda i, r: (0, r[1] * nt + i, 0))
    grid_spec = pltpu.PrefetchScalarGridSpec(
        num_scalar_prefetch=1, grid=(nt,),
        in_specs=[p_spec, pl.BlockSpec((3, ta, B), lambda i, r: (0, i, 0))],
        out_specs=pl.BlockSpec((None, ta, B), omap))
    return pl.pallas_call(
        body, name=name, grid_spec=grid_spec, out_shape=jax.ShapeDtypeStruct((L, A, B), F32),
        compiler_params=_cparams(("parallel",)),
    )(sc_idx, p, got)


def _final_exchange(gls, meta, sums, small):
    n = len(gls)
    ns = len(sums)
    rs, cs = small.shape

    def body(*refs):
        sum_refs = refs[n:n + ns]
        small_ref = refs[n + ns]
        outs = refs[n + ns + 1:2 * n + ns + 1]
        got_refs = refs[2 * n + ns + 1:2 * n + 2 * ns + 1]
        sm_ref = refs[2 * n + 2 * ns + 1]
        send_sems, recv_sems, local_sem = refs[2 * n + 2 * ns + 2:]
        x, y, c = _mesh_pos()
        me = 4 * x + 2 * y + c
        own_s = pltpu.make_async_copy(small_ref, sm_ref.at[me], local_sem)
        own_s.start()
        cps = []
        for k in range(n):
            kind, L, A, B = meta[k]
            half = _shard_piece(outs[k], L, A, c)
            cp = pltpu.make_async_remote_copy(src_ref=half, dst_ref=half, send_sem=send_sems.at[k],
                                              recv_sem=recv_sems.at[k], device_id=(x, y, 1 - c), device_id_type=MESH)
            cp.start()
            cps.append(cp)
        for k in range(ns):
            cp = pltpu.make_async_remote_copy(
                src_ref=sum_refs[k], dst_ref=got_refs[k], send_sem=send_sems.at[n + N_DEV + k],
                recv_sem=recv_sems.at[n + N_DEV + k], device_id=(x, y, 1 - c), device_id_type=MESH)
            cp.start()
            cps.append(cp)
        for r in range(1, N_DEV):
            fx, fy, fc = (r >> 2) & 1, (r >> 1) & 1, r & 1
            peer = (1 - x if fx else x, 1 - y if fy else y, 1 - c if fc else c)
            cp = pltpu.make_async_remote_copy(
                src_ref=small_ref, dst_ref=sm_ref.at[me], send_sem=send_sems.at[n + r], recv_sem=recv_sems.at[n + r],
                device_id=peer, device_id_type=MESH)
            cp.start()
            cps.append(cp)
        for cp in cps:
            cp.wait()
        own_s.wait()

    res = pl.pallas_call(
        body, name="rs_final_exchange", in_specs=[ANY] * (n + ns + 1), out_specs=[ANY] * (n + ns + 1),
        out_shape=[jax.ShapeDtypeStruct(g.shape, g.dtype) for g in list(gls) + list(sums)]
        + [jax.ShapeDtypeStruct((N_DEV, rs, cs), F32)],
        input_output_aliases={k: k for k in range(n)},
        scratch_shapes=[pltpu.SemaphoreType.DMA((n + N_DEV + ns,)), pltpu.SemaphoreType.DMA((n + N_DEV + ns,)),
                        pltpu.SemaphoreType.DMA],
    )(*gls, *sums, small)
    return res[:n], res[n:n + ns], res[n + ns]


def _tiny_pack(conv_a_w, sc_conv_w):
    lead = conv_a_w.shape[:-2]
    sc = sc_conv_w.reshape(lead + (2 * SC_WIDTH, 128))
    z = lambda r: jnp.zeros(lead + (r, 128), F32)
    return jnp.concatenate([conv_a_w, z(32 - CONV_A_WIDTH), sc, z(TINY_ROWS - 32 - 2 * SC_WIDTH)], axis=-2)


def _tiny_unpack(t):
    lead = t.shape[:-2]
    return t[..., :CONV_A_WIDTH, :], t[..., 32:32 + 2 * SC_WIDTH, :].reshape(lead + (SC_WIDTH, 256))


def _pack_small(d):
    flat = jnp.concatenate([d[n].astype(F32).reshape(-1) for n in SMALL_NAMES])
    n = flat.shape[0]
    total = -(-n // 1024) * 1024
    return jnp.pad(flat, (0, total - n)).reshape(total // 128, 128)


def _unpack_small(packed, shapes):
    flat = packed.reshape(-1)
    out, off = {}, 0
    for n in SMALL_NAMES:
        sz = math.prod(shapes[n])
        out[n] = flat[off:off + sz].reshape(shapes[n])
        off += sz
    return out


def _ffn_fwd(h, g, W, n_gu, n_down, i, tag):
    h2, u, gate, up, a = _ffn_fwd_fused(h, g, W[n_gu][i], W[n_down][i], 0, name=f"{tag}_fwd")
    return h2, (h, u, gate, up, a)


def _ffn_bwd(dh, saved, g, W, n_gu, n_down, i, G, tag):
    h, u, gate, up, a = saved
    dh_in, dg, dgate, dup = _ffn_bwd_fused(dh, h, g, gate, up, W[n_gu][i], W[n_down][i], 0, name=f"{tag}_bwd")
    G[(n_down, i)] = _mm(a, dh, name=f"{tag}_b_wdown", ta=True, tm=1408, tn=1024, tk=1024, scale=0.5)
    tn = FFN_CHUNK
    half = _mm(u, dgate, name=f"{tag}_b_wg", ta=True, tm=1024, tn=tn, tk=2048, stack=(1, 0, None, 2 * D_FF))
    G[(n_gu, i)] = _mm(u, dup, name=f"{tag}_b_wu", ta=True, tm=1024, tn=tn, tk=2048, stack=(1, 0, half, 2 * D_FF),
                       n_map=lambda j: j + D_FF // tn)
    return dh_in, dg


def _xa_block_fwd(h, mem, g, gm, W, i, tag):
    mn = _rms_fwd(mem, gm, name=f"{tag}_mem_norm")
    u, q = _norm_mm(h, g, W['xa_wq'][i], 0, name=f"{tag}_q", tn=1024)
    kv = _mm(mn, W['xa_wkv'][i], b_layer=0, name=f"{tag}_kv", tm=256, tn=1024, tk=1024)
    o = _xa_fwd(q, kv, name=f"{tag}_attn")
    h2 = _mm(o, W['xa_wo'][i], b_layer=0, name=f"{tag}_o", out_dtype=F32, tm=1024, tn=1024, tk=1024, res=h)
    return h2, (h, u, mn, q, kv, o)


def _xa_block_bwd(dh, saved, mem, g, gm, W, i, G, tag):
    h, u, mn, q, kv, o = saved
    do = _mm(dh, W['xa_wo'][i], b_layer=0, name=f"{tag}_b_do", tb=True, tm=1024, tn=1024, tk=1024)
    G[('xa_wo', i)] = _mm(o, dh, name=f"{tag}_b_wo", ta=True, tm=1024, tn=1024, tk=1024)
    dq, dkv = _xa_bwd(q, kv, do, name=f"{tag}_b_attn")
    G[('xa_wq', i)] = _mm(u, dq, name=f"{tag}_b_wq", ta=True, tm=1024, tn=1024, tk=1024)
    dh_in, dg = _mm_norm_bwd(dq, W['xa_wq'][i], 0, h, g, dh, name=f"{tag}_b_du", tk=1024)
    G[('xa_wkv', i)] = _mm(mn, dkv, name=f"{tag}_b_wkv", ta=True, tm=1024, tn=1024, tk=256)
    dmn = _mm(dkv, W['xa_wkv'][i], b_layer=0, name=f"{tag}_b_dmn", tb=True, out_dtype=F32, tm=256, tn=1024, tk=1024)
    _, dgm = _rms_bwd(mem, gm, dmn, None, name=f"{tag}_b_mem_norm")
    return dh_in, dg, dgm


def _pad_conv_w(w, rows):
    return jnp.pad(w.astype(F32), ((0, rows - w.shape[0]), (0, 0)))


def _even_fwd(h, g, W, conv_w, conv_b, ln_g, ln_b, sinks, tag):
    u, z = _norm_mm(h, g, W['even_w_in'][0], 0, name=f"{tag}_in", tn=1792)
    c, act = _conv_a_fwd(z, conv_w, conv_b, ln_g, ln_b, name=f"{tag}_conv")
    kpad = jnp.pad(z[:, 1536:1664], ((WINDOW, 0), (0, 0)))
    vpad = jnp.pad(z[:, 1664:1792], ((WINDOW, 0), (0, 0)))
    o = _swa_fwd(z, kpad, vpad, sinks, name=f"{tag}_swa")
    cat = jnp.concatenate([act, o], axis=-1)
    h2 = _mm(cat, W['even_w_out'][0], b_layer=0, name=f"{tag}_out", out_dtype=F32, tm=1024, tn=1024, tk=1024, res=h)
    return h2, (h, u, z, c, kpad, vpad, cat)


def _even_bwd(dh, saved, g, W, conv_w, ln_g, ln_b, sinks, G, tag):
    h, u, z, c, kpad, vpad, cat = saved
    dcat = _mm(dh, W['even_w_out'][0], b_layer=0, name=f"{tag}_b_dcat", tb=True, tm=1024, tn=1024, tk=1024)
    G[('even_w_out', 0)] = _mm(cat, dh, name=f"{tag}_b_wout", ta=True, tm=1024, tn=1024, tk=1024)
    dz_a, small = _conv_a_bwd(z, c, dcat, conv_w, ln_g, ln_b, name=f"{tag}_b_conv")
    dq, dkp, dvp, dsinks = _swa_bwd(z, kpad, vpad, sinks, dcat, name=f"{tag}_b_swa")
    dz = jnp.concatenate([dz_a, dq, dkp[WINDOW:].astype(BF16), dvp[WINDOW:].astype(BF16)], axis=-1)
    G[('even_w_in', 0)] = _mm(u, dz, name=f"{tag}_b_win", ta=True, tm=1024, tn=1792, tk=1024)
    dh_in, dg = _mm_norm_bwd(dz, W['even_w_in'][0], 0, h, g, dh, name=f"{tag}_b_du", tk=1792)
    grads = dict(mix=dg, conv_a_w=small[:CONV_A_WIDTH], conv_a_b=small[32:33], conv_a_ln_g=small[33:34],
                 conv_a_ln_b=small[34:35], swa_sinks=dsinks[:, 0])
    return dh_in, grads


def _odd_fwd(h, g, W, conv_w, tag):
    u, z = _norm_mm(h, g, W['odd_w_in'][0], 0, name=f"{tag}_in", tn=1024)
    y, cc = _sconv_fwd(z, conv_w, name=f"{tag}_conv")
    h2 = _mm(y, W['odd_w_out'][0], b_layer=0, name=f"{tag}_out", out_dtype=F32, tm=1024, tn=1024, tk=1024, res=h)
    return h2, (h, u, z, y, cc)


def _odd_bwd(dh, saved, g, W, conv_w, G, tag):
    h, u, z, y, cc = saved
    dy = _mm(dh, W['odd_w_out'][0], b_layer=0, name=f"{tag}_b_dy", tb=True, tm=1024, tn=1024, tk=1024)
    G[('odd_w_out', 0)] = _mm(y, dh, name=f"{tag}_b_wout", ta=True, tm=1024, tn=1024, tk=1024)
    dz, dw = _sconv_bwd(z, cc, dy, conv_w, name=f"{tag}_b_conv")
    G[('odd_w_in', 0)] = _mm(u, dz, name=f"{tag}_b_win", ta=True, tm=1024, tn=1024, tk=1024)
    dh_in, dg = _mm_norm_bwd(dz, W['odd_w_in'][0], 0, h, g, dh, name=f"{tag}_b_du", tk=1024)
    return dh_in, dict(mix=dg, sc_conv_w=dw[:SC_WIDTH])


def _local_step(x, mem, tgt, W, late_weights, token, layer1_grads_done, conv_a_w, sc_conv_w, P):
    row = lambda v: v.reshape(1, -1)
    conv_a_w = _pad_conv_w(conv_a_w, 32)
    sc_w = _pad_conv_w(sc_conv_w, 8)
    sinks = P['swa_sinks'][0]

    h = x
    saved = []
    for i in range(2):
        t = f"l{i}"
        if i == 1:
            for n, ws in late_weights(h).items():
                W[n] = W.get(n, []) + ws
        g1 = row(P['ffn1_norm'][i]) + (token if i == 0 else 0.0)
        h, s1 = _ffn_fwd(h, g1, W, 'ffn1_w_gu', 'ffn1_w_down', i, f"{t}_ffn1")
        if i == 0:
            h, s2 = _even_fwd(h, row(P['mix_norm'][i]), W, conv_a_w, P['conv_a_b'], P['conv_a_ln_g'],
                              P['conv_a_ln_b'], sinks, f"{t}_even")
        else:
            h, s2 = _odd_fwd(h, row(P['mix_norm'][i]), W, sc_w, f"{t}_odd")
        h, s3 = _xa_block_fwd(h, mem, row(P['xa_norm'][i]), row(P['xa_mem_norm'][i]), W, i, f"{t}_xa")
        h, s4 = _ffn_fwd(h, row(P['ffn2_norm'][i]), W, 'ffn2_w_gu', 'ffn2_w_down', i, f"{t}_ffn2")
        saved.append((s1, s2, s3, s4))

    loss, dh, d_final = _final_loss(h, row(P['final_norm']), tgt, name="final_loss")

    G = {}
    gp = {n: [None, None] for n in ('ffn1_norm', 'mix_norm', 'xa_norm', 'xa_mem_norm', 'ffn2_norm')}
    single = {}
    for i in (1, 0):
        t = f"l{i}"
        s1, s2, s3, s4 = saved[i]
        g4 = row(P['ffn2_norm'][i]) + (layer1_grads_done(G) if i == 0 else 0.0)
        dh, gp['ffn2_norm'][i] = _ffn_bwd(dh, s4, g4, W, 'ffn2_w_gu', 'ffn2_w_down', i, G, f"{t}_ffn2")
        dh, gp['xa_norm'][i], gp['xa_mem_norm'][i] = _xa_block_bwd(
            dh, s3, mem, row(P['xa_norm'][i]), row(P['xa_mem_norm'][i]), W, i, G, f"{t}_xa")
        if i == 0:
            dh, g2 = _even_bwd(dh, s2, row(P['mix_norm'][i]), W, conv_a_w, P['conv_a_ln_g'], P['conv_a_ln_b'], sinks,
                               G, f"{t}_even")
        else:
            dh, g2 = _odd_bwd(dh, s2, row(P['mix_norm'][i]), W, sc_w, G, f"{t}_odd")
        gp['mix_norm'][i] = g2.pop('mix')
        single.update(g2)
        dh, gp['ffn1_norm'][i] = _ffn_bwd(dh, s1, row(P['ffn1_norm'][i]), W, 'ffn1_w_gu', 'ffn1_w_down', i, G,
                                          f"{t}_ffn1")

    small = {n: jnp.concatenate(v, axis=0) for n, v in gp.items()}
    small['conv_a_b'] = single['conv_a_b']
    small['conv_a_ln_g'] = single['conv_a_ln_g']
    small['conv_a_ln_b'] = single['conv_a_ln_b']
    small['swa_sinks'] = single['swa_sinks'][None]
    small['final_norm'] = d_final[0]
    small['conv_a_w'] = single['conv_a_w']
    small['sc_conv_w'] = single['sc_conv_w']
    return loss[0, 0], dh, G, small


def _as2d(a):
    return a.reshape(-1, a.shape[-1])


def kernel(x, mem, ffn1_norm, ffn1_w_gu, ffn1_w_down, mix_norm, even_w_in, conv_a_w, conv_a_b, conv_a_ln_g, conv_a_ln_b, swa_sinks, even_w_out, odd_w_in, sc_conv_w, odd_w_out, xa_norm, xa_mem_norm, xa_wq, xa_wkv, xa_wo, ffn2_norm, ffn2_w_gu, ffn2_w_down, final_norm, loss_target, m_ffn1_norm, m_ffn1_w_gu, m_ffn1_w_down, m_mix_norm, m_even_w_in, m_conv_a_w, m_conv_a_b, m_conv_a_ln_g, m_conv_a_ln_b, m_swa_sinks, m_even_w_out, m_odd_w_in, m_sc_conv_w, m_odd_w_out, m_xa_norm, m_xa_mem_norm, m_xa_wq, m_xa_wkv, m_xa_wo, m_ffn2_norm, m_ffn2_w_gu, m_ffn2_w_down, m_final_norm, v_ffn1_norm, v_ffn1_w_gu, v_ffn1_w_down, v_mix_norm, v_even_w_in, v_conv_a_w, v_conv_a_b, v_conv_a_ln_g, v_conv_a_ln_b, v_swa_sinks, v_even_w_out, v_odd_w_in, v_sc_conv_w, v_odd_w_out, v_xa_norm, v_xa_mem_norm, v_xa_wq, v_xa_wkv, v_xa_wo, v_ffn2_norm, v_ffn2_w_gu, v_ffn2_w_down, v_final_norm):
    w = dict(zip(WEIGHT_NAMES, (ffn1_norm, ffn1_w_gu, ffn1_w_down, mix_norm, even_w_in, conv_a_w, conv_a_b, conv_a_ln_g, conv_a_ln_b, swa_sinks, even_w_out, odd_w_in, sc_conv_w, odd_w_out, xa_norm, xa_mem_norm, xa_wq, xa_wkv, xa_wo, ffn2_norm, ffn2_w_gu, ffn2_w_down, final_norm)))
    m = dict(zip(WEIGHT_NAMES, (m_ffn1_norm, m_ffn1_w_gu, m_ffn1_w_down, m_mix_norm, m_even_w_in, m_conv_a_w, m_conv_a_b, m_conv_a_ln_g, m_conv_a_ln_b, m_swa_sinks, m_even_w_out, m_odd_w_in, m_sc_conv_w, m_odd_w_out, m_xa_norm, m_xa_mem_norm, m_xa_wq, m_xa_wkv, m_xa_wo, m_ffn2_norm, m_ffn2_w_gu, m_ffn2_w_down, m_final_norm)))
    v = dict(zip(WEIGHT_NAMES, (v_ffn1_norm, v_ffn1_w_gu, v_ffn1_w_down, v_mix_norm, v_even_w_in, v_conv_a_w, v_conv_a_b, v_conv_a_ln_g, v_conv_a_ln_b, v_swa_sinks, v_even_w_out, v_odd_w_in, v_sc_conv_w, v_odd_w_out, v_xa_norm, v_xa_mem_norm, v_xa_wq, v_xa_wkv, v_xa_wo, v_ffn2_norm, v_ffn2_w_gu, v_ffn2_w_down, v_final_norm)))
    small_shapes = {n: w[n].shape for n in SMALL_NAMES}
    cx, cy, cc = lax.axis_index("x"), lax.axis_index("y"), lax.axis_index("c")
    chip_idx = (2 * cx + cy).astype(jnp.int32).reshape(1)
    core_idx = cc.astype(jnp.int32).reshape(1)
    chip_core_idx = jnp.concatenate([chip_idx, core_idx])

    shards = {n: w[n] for n in COMM_NAMES if n != 'tiny'}
    shards['tiny'] = _tiny_pack(conv_a_w, sc_conv_w)
    two_layer = [n for n in COMM_NAMES if shards[n].shape[0] == 2]
    early = [(n, 0) for n in two_layer] + [('even_w_in', 0), ('even_w_out', 0), ('tiny', 0)]
    late = [(n, 1) for n in two_layer] + [('odd_w_in', 0), ('odd_w_out', 0)]

    def place(items):
        return [_place(shards[n], l, LAYOUT[n], chip_idx, F32 if n == 'tiny' else BF16, name=f"place_{n}_{l}")
                for n, l in items]

    def item_meta(items):
        return [(LAYOUT[n], 1) + shards[n].shape[1:] for n, l in items]

    def natural(items, arrays):
        out = {}
        for (n, l), a in zip(items, arrays):
            if n == 'tiny':
                continue
            if n == 'even_w_in':
                out[n] = [a.transpose(0, 2, 1, 3).reshape(1, D_MODEL, -1)]
            else:
                out[n] = [a if LAYOUT[n] == 'col' else a.reshape(1, N_CHIPS * a.shape[2], a.shape[3])]
        return out

    early_full = _all_gather(place(early), item_meta(early))
    W = natural(early, early_full)
    ca, sc = _tiny_unpack(early_full[-1][0])
    conv_a_full = ca.transpose(1, 0, 2).reshape(CONV_A_WIDTH, CONV_A_CH)
    sc_full = sc.transpose(1, 0, 2).reshape(SC_WIDTH, SC_CH)
    send_sems, recv_sems, in_flight, token = _gather_start(place(late), item_meta(late), early_full[0])

    def late_weights(h):
        return natural(late, _gather_wait(send_sems, recv_sems, in_flight, item_meta(late), h))

    def gathered_layout(G, item):
        n, l = item
        A, B = shards[n].shape[1:]
        g = G[item]
        if n == 'tiny':
            return g
        if n == 'even_w_in':
            return g.reshape(A, N_CHIPS, B).transpose(1, 0, 2)[None]
        return g.reshape(1, A, N_CHIPS * B) if LAYOUT[n] == 'col' else g.reshape(1, N_CHIPS, A, B)

    scatter = {}

    def layer1_grads_done(G):
        send, recv, gs1, lands, tok = _scatter_start([gathered_layout(G, it) for it in late], item_meta(late),
                                                     chip_idx)
        scatter.update(sems=(send, recv), gs=gs1, lands=lands)
        return tok[:1, :1]

    loss_part, grad_x, G, g_small = _local_step(x[0], mem[0], loss_target[0], W, late_weights, token[:1, :1],
                                                layer1_grads_done, conv_a_full, sc_full,
                                                {n: w[n] for n in SMALL_NAMES})
    G[('tiny', 0)] = _tiny_pack(g_small['conv_a_w'].reshape(CONV_A_WIDTH, N_CHIPS, 128).transpose(1, 0, 2),
                                g_small['sc_conv_w'].reshape(SC_WIDTH, N_CHIPS, 256).transpose(1, 0, 2))[None]
    loss = lax.psum(loss_part, ("x", "y", "c"))

    gs = [gathered_layout(G, it) for it in early]
    meta0 = item_meta(early)
    got = _pair_exchange(gs, meta0)
    ps = [_pair_sum(g, r, m_, core_idx, name=f"rs_pair_sum_{n}") for (n, l), g, r, m_ in zip(early, gs, got, meta0)]
    got2 = _chip_exchange(ps, meta0)
    red = [_chip_sum(p, r, m_, chip_core_idx, name=f"rs_chip_sum_{n}")
           for (n, l), p, r, m_ in zip(early, ps, got2, meta0)]
    sent1, landed1 = _scatter_wait(*scatter['sems'], scatter['gs'], scatter['lands'], item_meta(late), grad_x)
    sums1 = [_chip_sum_full(g, r, m_, chip_idx, name=f"rs_chip_sum_{n}_1")
             for (n, l), g, r, m_ in zip(late, sent1, landed1, item_meta(late))]
    g_shards, sib_sums1, small_parts = _final_exchange(red, meta0, sums1, _pack_small(g_small))
    g0 = dict(zip(early, g_shards))
    g1 = {it: [a, b] for it, a, b in zip(late, sums1, sib_sums1)}

    grads, deltas, new_m, new_v = {}, {}, {}, {}
    for n in BIG_NAMES:
        if n in ('conv_a_w', 'sc_conv_w'):
            continue
        gsrc = [[g0[(n, l)]] if (n, l) in g0 else g1[(n, l)] for l in range(w[n].shape[0])]
        grads[n], deltas[n], new_m[n], new_v[n] = _adamw_layers(w[n], m[n], v[n], gsrc, name=f"adamw_{n}")
    tiny_g = dict(zip(('conv_a_w', 'sc_conv_w'), _tiny_unpack(g0[('tiny', 0)])))
    for n in ('conv_a_w', 'sc_conv_w'):
        shp = w[n].shape
        d, nm, nv = _adamw(_as2d(w[n]), _as2d(tiny_g[n]), _as2d(m[n]), _as2d(v[n]), name=f"adamw_{n}")
        grads[n], deltas[n], new_m[n], new_v[n] = tiny_g[n], d.reshape(shp), nm.reshape(shp), nv.reshape(shp)
    gs, ds, ms, vs = _adamw_small(_pack_small({n: w[n] for n in SMALL_NAMES}), small_parts,
                                  _pack_small({n: m[n] for n in SMALL_NAMES}),
                                  _pack_small({n: v[n] for n in SMALL_NAMES}), name="adamw_small")
    for dst, packed in ((grads, gs), (deltas, ds), (new_m, ms), (new_v, vs)):
        dst.update(_unpack_small(packed, small_shapes))

    return (loss, grad_x[None], *[grads[n] for n in WEIGHT_NAMES], *[deltas[n] for n in WEIGHT_NAMES],
            *[new_m[n] for n in WEIGHT_NAMES], *[new_v[n] for n in WEIGHT_NAMES])
```

```python
import functools
import math

import jax
import jax.numpy as jnp
from jax import lax
from jax.experimental import pallas as pl
from jax.experimental.pallas import tpu as pltpu

F32 = jnp.float32
BF16 = jnp.bfloat16

D_MODEL = 1024
D_FF = 2816
CONV_A_CH = 512
CONV_A_WIDTH = 31
SWA_HEADS = 8
SWA_KV_HEADS = 2
SWA_GROUP = 4
HEAD_DIM = 64
WINDOW = 128
SC_CH = 1024
SC_WIDTH = 3
XA_HEADS = 4
XA_HEAD_DIM = 256
RMS_EPS = 1e-6
LN_EPS = 1e-5

ADAM_LR = 0.001
ADAM_B1 = 0.9
ADAM_B2 = 0.999
ADAM_EPS = 1e-08
ADAM_WD = 0.01
ADAM_STEP = 10

N_CHIPS = 4
N_DEV = 8
NEG_BIG = -1e30
VMEM_LIMIT = 56 * 1024 * 1024
MESH = pl.DeviceIdType.MESH

INPUT_NAMES = ['x', 'mem', 'ffn1_norm', 'ffn1_w_gu', 'ffn1_w_down', 'mix_norm', 'even_w_in', 'conv_a_w', 'conv_a_b',
               'conv_a_ln_g', 'conv_a_ln_b', 'swa_sinks', 'even_w_out', 'odd_w_in', 'sc_conv_w', 'odd_w_out', 'xa_norm',
               'xa_mem_norm', 'xa_wq', 'xa_wkv', 'xa_wo', 'ffn2_norm', 'ffn2_w_gu', 'ffn2_w_down', 'final_norm']
WEIGHT_NAMES = INPUT_NAMES[2:]
BIG = [('ffn1_w_gu', 'col'), ('ffn1_w_down', 'row'), ('even_w_in', 'col'), ('conv_a_w', 'col'), ('even_w_out', 'row'),
       ('odd_w_in', 'col'), ('sc_conv_w', 'col'), ('odd_w_out', 'row'), ('xa_wq', 'row'), ('xa_wkv', 'col'),
       ('xa_wo', 'row'), ('ffn2_w_gu', 'col'), ('ffn2_w_down', 'row')]
BIG_NAMES = [n for n, _ in BIG]
SMALL_NAMES = [n for n in WEIGHT_NAMES if n not in BIG_NAMES]


def _cparams(sem=None, vmem=VMEM_LIMIT):
    kw = dict(vmem_limit_bytes=vmem)
    if sem is not None:
        kw['dimension_semantics'] = sem
    return pltpu.CompilerParams(**kw)


def _div_tile(n, want, align=8):
    if n <= want:
        return n
    t = (want // align) * align
    while t >= align:
        if n % t == 0:
            return t
        t -= align
    return n


def _mm(a, b, *, name, ta=False, tb=False, out_dtype=BF16, tm=512, tn=512, tk=512, res=None, scale=1.0,
        b_layer=None, stack=None, n_map=None):
    n_map = n_map or (lambda j: j)
    if ta:
        K, M = a.shape
    else:
        M, K = a.shape
    if tb:
        N, K2 = b.shape[-2:]
    else:
        K2, N = b.shape[-2:]
    assert K == K2, (a.shape, b.shape, ta, tb)
    tm = _div_tile(M, tm, 128 if ta else 16)
    tn = _div_tile(N, tn, 128)
    tk = _div_tile(K, tk, 16 if ta else 128)
    nk = K // tk
    a_spec = pl.BlockSpec((tk, tm), lambda i, j, k: (k, i)) if ta else pl.BlockSpec((tm, tk), lambda i, j, k: (i, k))
    if b_layer is None:
        b_spec = pl.BlockSpec((tn, tk), lambda i, j, k: (j, k)) if tb else pl.BlockSpec((tk, tn), lambda i, j, k: (k, j))
    elif tb:
        b_spec = pl.BlockSpec((None, tn, tk), lambda i, j, k: (b_layer, j, k))
    else:
        b_spec = pl.BlockSpec((None, tk, tn), lambda i, j, k: (b_layer, k, j))
    o_spec = pl.BlockSpec((tm, tn), lambda i, j, k: (i, j))
    out_shape = jax.ShapeDtypeStruct((M, N), out_dtype)
    out_spec = o_spec
    aliases = {}
    extra_specs, extra_args = [], ()
    if stack is not None:
        n_layers, layer, buf = stack[:3]
        n_total = stack[3] if len(stack) > 3 else N
        out_shape = jax.ShapeDtypeStruct((n_layers, M, n_total), out_dtype)
        out_spec = pl.BlockSpec((None, tm, tn), lambda i, j, k: (layer, i, n_map(j)))
        if buf is not None:
            extra_specs, extra_args = [pl.BlockSpec(memory_space=pl.ANY)], (buf,)
            aliases = {2 + (res is not None): 0}
    dims = (((0 if ta else 1,), (1 if tb else 0,)), ((), ()))
    has_res = res is not None
    n_extra = len(extra_args)

    def body(*refs):
        if n_extra:
            refs = refs[:2 + has_res] + refs[2 + has_res + n_extra:]
        if has_res:
            a_ref, b_ref, r_ref, o_ref, acc_ref = refs
        else:
            a_ref, b_ref, o_ref, acc_ref = refs
        k = pl.program_id(2)
        p = lax.dot_general(a_ref[...].astype(BF16), b_ref[...].astype(BF16), dims, preferred_element_type=F32)

        @pl.when(k == 0)
        def _():
            acc_ref[...] = p

        @pl.when(k > 0)
        def _():
            acc_ref[...] += p

        @pl.when(k == nk - 1)
        def _():
            r = acc_ref[...] * scale
            if has_res:
                r = r_ref[...] + r
            o_ref[...] = r.astype(out_dtype)

    in_specs = [a_spec, b_spec] + ([o_spec] if has_res else []) + extra_specs
    args = (a, b) + ((res,) if has_res else ()) + extra_args
    return pl.pallas_call(
        body, name=name, grid=(M // tm, N // tn, nk), in_specs=in_specs, out_specs=out_spec,
        out_shape=out_shape, input_output_aliases=aliases,
        scratch_shapes=[pltpu.VMEM((tm, tn), F32)],
        compiler_params=_cparams(("parallel", "parallel", "arbitrary")),
    )(*args)


def _rms_fwd(x, g, *, name):
    S, D = x.shape
    ts = _div_tile(S, 512)

    def body(x_ref, g_ref, o_ref):
        xv = x_ref[...]
        r = lax.rsqrt(jnp.mean(xv * xv, axis=-1, keepdims=True) + RMS_EPS)
        o_ref[...] = (xv * r * g_ref[...]).astype(BF16)

    return pl.pallas_call(
        body, name=name, grid=(S // ts,),
        in_specs=[pl.BlockSpec((ts, D), lambda i: (i, 0)), pl.BlockSpec((1, D), lambda i: (0, 0))],
        out_specs=pl.BlockSpec((ts, D), lambda i: (i, 0)),
        out_shape=jax.ShapeDtypeStruct((S, D), BF16),
        compiler_params=_cparams(("parallel",)),
    )(x, g)


def _norm_mm(h, g, w, layer, *, name, tn):
    S, D = h.shape
    N = w.shape[-1]
    tm = _div_tile(S, 1024, 16)
    tn = _div_tile(N, tn, 128)

    def body(h_ref, g_ref, w_ref, u_ref, z_ref, u_s):
        @pl.when(pl.program_id(1) == 0)
        def _():
            xv = h_ref[...]
            r = lax.rsqrt(jnp.mean(xv * xv, axis=-1, keepdims=True) + RMS_EPS)
            u = (xv * r * g_ref[...]).astype(BF16)
            u_s[...] = u
            u_ref[...] = u

        z_ref[...] = jnp.dot(u_s[...], w_ref[...], preferred_element_type=F32).astype(BF16)

    row = pl.BlockSpec((tm, D), lambda i, j: (i, 0))
    return pl.pallas_call(
        body, name=name, grid=(S // tm, N // tn),
        in_specs=[row, pl.BlockSpec((1, D), lambda i, j: (0, 0)), pl.BlockSpec((None, D, tn), lambda i, j: (layer, 0, j))],
        out_specs=[row, pl.BlockSpec((tm, tn), lambda i, j: (i, j))],
        out_shape=[jax.ShapeDtypeStruct((S, D), BF16), jax.ShapeDtypeStruct((S, N), BF16)],
        scratch_shapes=[pltpu.VMEM((tm, D), BF16)],
        compiler_params=_cparams(("parallel", "arbitrary")),
    )(h, g, w)


def _mm_norm_bwd(dz, w, layer, h, g, dres, *, name, tk):
    S, K = dz.shape
    D = h.shape[1]
    tm = _div_tile(S, 512, 16)
    tk = _div_tile(K, tk, 128)
    nk = K // tk

    def body(dz_ref, w_ref, h_ref, g_ref, dr_ref, dx_ref, dg_ref, acc):
        i = pl.program_id(0)
        k = pl.program_id(1)
        p = lax.dot_general(dz_ref[...], w_ref[...], (((1,), (1,)), ((), ())), preferred_element_type=F32)

        @pl.when(k == 0)
        def _():
            acc[...] = p

        @pl.when(k > 0)
        def _():
            acc[...] += p

        @pl.when(k == nk - 1)
        def _():
            xv = h_ref[...]
            du = acc[...]
            r = lax.rsqrt(jnp.mean(xv * xv, axis=-1, keepdims=True) + RMS_EPS)
            xhat = xv * r
            part = jnp.sum(du * xhat, axis=0, keepdims=True)

            @pl.when(i == 0)
            def _():
                dg_ref[...] = part

            @pl.when(i > 0)
            def _():
                dg_ref[...] += part

            dxhat = du * g_ref[...]
            dx_ref[...] = dr_ref[...] + r * (dxhat - xhat * jnp.mean(dxhat * xhat, axis=-1, keepdims=True))

    row = pl.BlockSpec((tm, D), lambda i, k: (i, 0))
    vec = pl.BlockSpec((1, D), lambda i, k: (0, 0))
    return pl.pallas_call(
        body, name=name, grid=(S // tm, nk),
        in_specs=[pl.BlockSpec((tm, tk), lambda i, k: (i, k)), pl.BlockSpec((None, D, tk), lambda i, k: (layer, 0, k)),
                  row, vec, row],
        out_specs=[row, vec],
        out_shape=[jax.ShapeDtypeStruct((S, D), F32), jax.ShapeDtypeStruct((1, D), F32)],
        scratch_shapes=[pltpu.VMEM((tm, D), F32)],
        compiler_params=_cparams(("arbitrary", "arbitrary")),
    )(dz, w, h, g, dres)


def _rms_bwd(x, g, du, dres, *, name):
    S, D = x.shape
    ts = _div_tile(S, 512)
    has_res = dres is not None

    def body(*refs):
        if has_res:
            x_ref, g_ref, du_ref, dr_ref, dx_ref, dg_ref = refs
        else:
            x_ref, g_ref, du_ref, dg_ref = refs
        i = pl.program_id(0)
        xv = x_ref[...]
        duv = du_ref[...].astype(F32)
        r = lax.rsqrt(jnp.mean(xv * xv, axis=-1, keepdims=True) + RMS_EPS)
        xhat = xv * r
        part = jnp.sum(duv * xhat, axis=0, keepdims=True)

        @pl.when(i == 0)
        def _():
            dg_ref[...] = part

        @pl.when(i > 0)
        def _():
            dg_ref[...] += part

        if has_res:
            dxhat = duv * g_ref[...]
            dx = r * (dxhat - xhat * jnp.mean(dxhat * xhat, axis=-1, keepdims=True))
            dx_ref[...] = dr_ref[...] + dx

    row = pl.BlockSpec((ts, D), lambda i: (i, 0))
    vec = pl.BlockSpec((1, D), lambda i: (0, 0))
    if has_res:
        dx, dg = pl.pallas_call(
            body, name=name, grid=(S // ts,), in_specs=[row, vec, row, row], out_specs=[row, vec],
            out_shape=[jax.ShapeDtypeStruct((S, D), F32), jax.ShapeDtypeStruct((1, D), F32)],
            compiler_params=_cparams(("arbitrary",)),
        )(x, g, du, dres)
        return dx, dg
    dg = pl.pallas_call(
        body, name=name, grid=(S // ts,), in_specs=[row, vec, row], out_specs=vec,
        out_shape=jax.ShapeDtypeStruct((1, D), F32),
        compiler_params=_cparams(("arbitrary",)),
    )(x, g, du)
    return None, dg


def _final_loss(h, g, tgt, *, name):
    S, D = h.shape
    ts = _div_tile(S, 512)

    def body(h_ref, g_ref, t_ref, loss_ref, dh_ref, dg_ref):
        i = pl.program_id(0)
        xv = h_ref[...]
        gv = g_ref[...]
        r = lax.rsqrt(jnp.mean(xv * xv, axis=-1, keepdims=True) + RMS_EPS)
        xhat = xv * r
        err = xhat * gv - t_ref[...]
        lpart = 0.5 * jnp.sum(jnp.mean(err * err, axis=-1, keepdims=True), axis=0, keepdims=True)
        dy = err * (1.0 / D)
        gpart = jnp.sum(dy * xhat, axis=0, keepdims=True)

        @pl.when(i == 0)
        def _():
            loss_ref[...] = jnp.broadcast_to(lpart, loss_ref.shape)
            dg_ref[...] = gpart

        @pl.when(i > 0)
        def _():
            loss_ref[...] += jnp.broadcast_to(lpart, loss_ref.shape)
            dg_ref[...] += gpart

        dxhat = dy * gv
        dh_ref[...] = r * (dxhat - xhat * jnp.mean(dxhat * xhat, axis=-1, keepdims=True))

    row = pl.BlockSpec((ts, D), lambda i: (i, 0))
    vec = pl.BlockSpec((1, D), lambda i: (0, 0))
    return pl.pallas_call(
        body, name=name, grid=(S // ts,), in_specs=[row, vec, row],
        out_specs=[pl.BlockSpec((8, 128), lambda i: (0, 0)), row, vec],
        out_shape=[jax.ShapeDtypeStruct((8, 128), F32), jax.ShapeDtypeStruct((S, D), F32),
                   jax.ShapeDtypeStruct((1, D), F32)],
        compiler_params=_cparams(("arbitrary",)),
    )(h, g, tgt)


def _sigmoid(x):
    return 1.0 / (1.0 + jnp.exp(-x))


FFN_CHUNK = 1408
FFN_CHUNKS = D_FF // FFN_CHUNK
FFN_BWD_PIECE = 768
FFN_BWD_SLAB = 256


def _ffn_fwd_fused(h, g, w_gu, w_down, layer, *, name):
    S, D = h.shape
    tm = _div_tile(S, 512, 16)
    tf, nj = FFN_CHUNK, FFN_CHUNKS

    def body(h_ref, g_ref, wg_ref, wu_ref, wd_ref, h2_ref, u_ref, gate_ref, up_ref, a_ref, u_s, acc):
        j = pl.program_id(1)

        @pl.when(j == 0)
        def _():
            xv = h_ref[...]
            r = lax.rsqrt(jnp.mean(xv * xv, axis=-1, keepdims=True) + RMS_EPS)
            u = (xv * r * g_ref[...]).astype(BF16)
            u_s[...] = u
            u_ref[...] = u

        u = u_s[...]
        gate = jnp.dot(u, wg_ref[...], preferred_element_type=F32)
        up = jnp.dot(u, wu_ref[...], preferred_element_type=F32)
        gate_ref[...] = gate.astype(BF16)
        up_ref[...] = up.astype(BF16)
        a = (gate * _sigmoid(gate) * up).astype(BF16)
        a_ref[...] = a
        p = jnp.dot(a, wd_ref[...], preferred_element_type=F32)

        @pl.when(j == 0)
        def _():
            acc[...] = p

        @pl.when(j > 0)
        def _():
            acc[...] += p

        @pl.when(j == nj - 1)
        def _():
            h2_ref[...] = h_ref[...] + 0.5 * acc[...]

    row = pl.BlockSpec((tm, D), lambda i, j: (i, 0))
    chunk = pl.BlockSpec((tm, tf), lambda i, j: (i, j))
    hidden = jax.ShapeDtypeStruct((S, D_FF), BF16)
    return pl.pallas_call(
        body, name=name, grid=(S // tm, nj),
        in_specs=[row, pl.BlockSpec((1, D), lambda i, j: (0, 0)),
                  pl.BlockSpec((None, D, tf), lambda i, j: (layer, 0, j)),
                  pl.BlockSpec((None, D, tf), lambda i, j: (layer, 0, nj + j)),
                  pl.BlockSpec((None, tf, D), lambda i, j: (layer, j, 0))],
        out_specs=[row, row, chunk, chunk, chunk],
        out_shape=[jax.ShapeDtypeStruct((S, D), F32), jax.ShapeDtypeStruct((S, D), BF16), hidden, hidden, hidden],
        scratch_shapes=[pltpu.VMEM((tm, D), BF16), pltpu.VMEM((tm, D), F32)],
        compiler_params=_cparams(("parallel", "arbitrary")),
    )(h, g, w_gu, w_gu, w_down)


def _ffn_bwd_fused(dh, h, g, gate, up, w_gu, w_down, layer, *, name):
    S, D = h.shape
    tm = _div_tile(S, 512, FFN_BWD_SLAB)
    tf = FFN_CHUNK
    nj = D_FF // tf
    slab = min(FFN_BWD_SLAB, tm)
    nt = (((1,), (1,)), ((), ()))
    pieces = [(c0, min(FFN_BWD_PIECE, tf - c0)) for c0 in range(0, tf, FFN_BWD_PIECE)]

    def body(dh_ref, h_ref, g_ref, gate_ref, up_ref, wg_ref, wu_ref, wd_ref, dx_ref, dg_ref, dgate_ref, dup_ref,
             dy_s, acc):
        i = pl.program_id(0)
        j = pl.program_id(1)

        @pl.when(j == 0)
        def _():
            for r0 in range(0, tm, slab):
                rows = pl.ds(r0, slab)
                dy_s[rows, :] = (0.5 * dh_ref[rows, :]).astype(BF16)

        p = None
        for c0, cw in pieces:
            cols = pl.ds(c0, cw)
            da = lax.dot_general(dy_s[...], wd_ref[cols, :], nt, preferred_element_type=F32)
            gt = gate_ref[:, cols].astype(F32)
            sg = _sigmoid(gt)
            dgate = (da * up_ref[:, cols].astype(F32) * sg * (1.0 + gt * (1.0 - sg))).astype(BF16)
            dup = (da * gt * sg).astype(BF16)
            dgate_ref[:, cols] = dgate
            dup_ref[:, cols] = dup
            q = (lax.dot_general(dgate, wg_ref[:, cols], nt, preferred_element_type=F32)
                 + lax.dot_general(dup, wu_ref[:, cols], nt, preferred_element_type=F32))
            p = q if p is None else p + q

        @pl.when(j == 0)
        def _():
            acc[...] = p

        @pl.when(j > 0)
        def _():
            acc[...] += p

        @pl.when(j == nj - 1)
        def _():
            part = jnp.zeros((1, D), F32)
            for r0 in range(0, tm, slab):
                rows = pl.ds(r0, slab)
                xv = h_ref[rows, :]
                du = acc[rows, :]
                r = lax.rsqrt(jnp.mean(xv * xv, axis=-1, keepdims=True) + RMS_EPS)
                xhat = xv * r
                part = part + jnp.sum(du * xhat, axis=0, keepdims=True)
                dxhat = du * g_ref[...]
                dx_ref[rows, :] = dh_ref[rows, :] + r * (
                    dxhat - xhat * jnp.mean(dxhat * xhat, axis=-1, keepdims=True))

            @pl.when(i == 0)
            def _():
                dg_ref[...] = part

            @pl.when(i > 0)
            def _():
                dg_ref[...] += part

    row = pl.BlockSpec((tm, D), lambda i, j: (i, 0))
    vec = pl.BlockSpec((1, D), lambda i, j: (0, 0))
    chunk = pl.BlockSpec((tm, tf), lambda i, j: (i, j))
    hidden = jax.ShapeDtypeStruct((S, D_FF), BF16)
    return pl.pallas_call(
        body, name=name, grid=(S // tm, nj),
        in_specs=[row, row, vec, chunk, chunk,
                  pl.BlockSpec((None, D, tf), lambda i, j: (layer, 0, j)),
                  pl.BlockSpec((None, D, tf), lambda i, j: (layer, 0, nj + j)),
                  pl.BlockSpec((None, tf, D), lambda i, j: (layer, j, 0))],
        out_specs=[row, vec, chunk, chunk],
        out_shape=[jax.ShapeDtypeStruct((S, D), F32), jax.ShapeDtypeStruct((1, D), F32), hidden, hidden],
        scratch_shapes=[pltpu.VMEM((tm, D), BF16), pltpu.VMEM((tm, D), F32)],
        compiler_params=_cparams(("arbitrary", "arbitrary")),
    )(dh, h, g, gate, up, w_gu, w_gu, w_down)


CONV_HALO = 32
CONV_SUB_ROWS = 128


def _conv_a_fwd(z, w, bias, ln_g, ln_b, *, name):
    S = z.shape[0]
    C = CONV_A_CH
    ts = _div_tile(S, 256, 32)

    def body(val_ref, gate_ref, w_ref, b_ref, g_ref, lb_ref, c_ref, act_ref, win):
        i = pl.program_id(0)

        @pl.when(i == 0)
        def _():
            win[pl.ds(0, CONV_HALO), :] = jnp.zeros((CONV_HALO, C), F32)

        @pl.when(i > 0)
        def _():
            win[pl.ds(0, CONV_HALO), :] = win[pl.ds(ts, CONV_HALO), :]

        a = val_ref[...].astype(F32) * _sigmoid(gate_ref[...].astype(F32))
        win[pl.ds(CONV_HALO, ts), :] = a
        rs = min(CONV_SUB_ROWS, ts)
        for cb in range(C // 128):
            lanes = pl.ds(128 * cb, 128)
            for rt in range(ts // rs):
                sub = jnp.broadcast_to(b_ref[:, lanes], (rs, 128))
                for k in range(CONV_A_WIDTH):
                    sub = sub + w_ref[pl.ds(k, 1), lanes] * win[
                        pl.ds(CONV_HALO - (CONV_A_WIDTH - 1) + k + rs * rt, rs), lanes]
                c_ref[pl.ds(rs * rt, rs), lanes] = sub
        acc = c_ref[...]
        mu = jnp.mean(acc, axis=-1, keepdims=True)
        xc = acc - mu
        var = jnp.mean(xc * xc, axis=-1, keepdims=True)
        ln = xc * lax.rsqrt(var + LN_EPS) * g_ref[...] + lb_ref[...]
        act_ref[...] = (ln * _sigmoid(ln)).astype(BF16)

    row = lambda col: pl.BlockSpec((ts, C), lambda i, col=col: (i, col))
    vec = pl.BlockSpec((1, C), lambda i: (0, 0))
    return pl.pallas_call(
        body, name=name, grid=(S // ts,),
        in_specs=[row(0), row(1), pl.BlockSpec((32, C), lambda i: (0, 0)), vec, vec, vec],
        out_specs=[row(0), row(0)],
        out_shape=[jax.ShapeDtypeStruct((S, C), F32), jax.ShapeDtypeStruct((S, C), BF16)],
        scratch_shapes=[pltpu.VMEM((ts + CONV_HALO, C), F32)],
        compiler_params=_cparams(("arbitrary",)),
    )(z, z, w, bias, ln_g, ln_b)


def _conv_a_bwd(z, c, dcat, w, ln_g, ln_b, *, name):
    S = z.shape[0]
    C = CONV_A_CH
    ts = _div_tile(S, 256, 32)
    n = S // ts

    def body(val_ref, gate_ref, c_ref, da_ref, w_ref, g_ref, lb_ref, dz_ref, small_ref, win, a_s, da_s, dw8):
        i = pl.program_id(0)

        @pl.when(i == 0)
        def _():
            win[pl.ds(ts, CONV_HALO), :] = jnp.zeros((CONV_HALO, C), F32)
            small_ref[...] = jnp.zeros(small_ref.shape, F32)
            dw8[...] = jnp.zeros(dw8.shape, F32)

        @pl.when(i > 0)
        def _():
            win[pl.ds(ts, CONV_HALO), :] = win[pl.ds(0, CONV_HALO), :]

        cv = c_ref[...]
        gv = g_ref[...]
        mu = jnp.mean(cv, axis=-1, keepdims=True)
        xc = cv - mu
        var = jnp.mean(xc * xc, axis=-1, keepdims=True)
        rstd = lax.rsqrt(var + LN_EPS)
        xhat = xc * rstd
        ln = xhat * gv + lb_ref[...]
        sg = _sigmoid(ln)
        dln = da_ref[...].astype(F32) * (sg * (1.0 + ln * (1.0 - sg)))
        small_ref[pl.ds(33, 1), :] += jnp.sum(dln * xhat, axis=0, keepdims=True)
        small_ref[pl.ds(34, 1), :] += jnp.sum(dln, axis=0, keepdims=True)
        dxhat = dln * gv
        dc = rstd * (dxhat - jnp.mean(dxhat, axis=-1, keepdims=True)
                     - xhat * jnp.mean(dxhat * xhat, axis=-1, keepdims=True))
        small_ref[pl.ds(32, 1), :] += jnp.sum(dc, axis=0, keepdims=True)
        win[pl.ds(0, ts), :] = dc

        val = val_ref[...].astype(F32)
        sgg = _sigmoid(gate_ref[...].astype(F32))
        a_s[...] = val * sgg
        rs = min(CONV_SUB_ROWS, ts)
        for cb in range(C // 128):
            lanes = pl.ds(128 * cb, 128)
            for rt in range(ts // rs):
                a_sub = a_s[pl.ds(rs * rt, rs), lanes]
                da = jnp.zeros((rs, 128), F32)
                for k in range(CONV_A_WIDTH):
                    sh = win[pl.ds(CONV_A_WIDTH - 1 - k + rs * rt, rs), lanes]
                    da = da + w_ref[pl.ds(k, 1), lanes] * sh
                    prod = a_sub * sh
                    part = prod[0:8]
                    for r in range(1, rs // 8):
                        part = part + prod[8 * r:8 * r + 8]
                    dw8[pl.ds(8 * k, 8), lanes] += part
                da_s[pl.ds(rs * rt, rs), lanes] = da
        da = da_s[...]
        dz_ref[:, pl.ds(0, C)] = (da * sgg).astype(BF16)
        dz_ref[:, pl.ds(C, C)] = (da * val * sgg * (1.0 - sgg)).astype(BF16)

        @pl.when(i == n - 1)
        def _():
            for k in range(CONV_A_WIDTH):
                small_ref[pl.ds(k, 1), :] = jnp.sum(dw8[pl.ds(8 * k, 8), :], axis=0, keepdims=True)

    row = lambda col: pl.BlockSpec((ts, C), lambda i, col=col: (n - 1 - i, col))
    vec = pl.BlockSpec((1, C), lambda i: (0, 0))
    return pl.pallas_call(
        body, name=name, grid=(n,),
        in_specs=[row(0), row(1), row(0), row(0), pl.BlockSpec((32, C), lambda i: (0, 0)), vec, vec],
        out_specs=[pl.BlockSpec((ts, 2 * C), lambda i: (n - 1 - i, 0)), pl.BlockSpec((40, C), lambda i: (0, 0))],
        out_shape=[jax.ShapeDtypeStruct((S, 2 * C), BF16), jax.ShapeDtypeStruct((40, C), F32)],
        scratch_shapes=[pltpu.VMEM((ts + CONV_HALO, C), F32), pltpu.VMEM((ts, C), F32), pltpu.VMEM((ts, C), F32),
                        pltpu.VMEM((8 * 32, C), F32)],
        compiler_params=_cparams(("arbitrary",)),
    )(z, z, c, dcat, w, ln_g, ln_b)


SC_HALO = 8


def _sconv_fwd(z, w, *, name):
    S = z.shape[0]
    C = SC_CH
    ts = _div_tile(S, 256, 16)

    def body(gb_ref, gc_ref, v_ref, w_ref, y_ref, cc_ref, win):
        i = pl.program_id(0)

        @pl.when(i == 0)
        def _():
            win[pl.ds(0, SC_HALO), :] = jnp.zeros((SC_HALO, C), F32)

        @pl.when(i > 0)
        def _():
            win[pl.ds(0, SC_HALO), :] = win[pl.ds(ts, SC_HALO), :]

        win[pl.ds(SC_HALO, ts), :] = gc_ref[...].astype(F32) * v_ref[...].astype(F32)
        acc = jnp.zeros((ts, C), F32)
        for k in range(SC_WIDTH):
            acc = acc + w_ref[pl.ds(k, 1), :] * win[pl.ds(SC_HALO - (SC_WIDTH - 1) + k, ts), :]
        cc_ref[...] = acc.astype(BF16)
        y_ref[...] = (gb_ref[...].astype(F32) * acc).astype(BF16)

    row = lambda col: pl.BlockSpec((ts, C), lambda i, col=col: (i, col))
    return pl.pallas_call(
        body, name=name, grid=(S // ts,),
        in_specs=[row(0), row(1), row(2), pl.BlockSpec((8, C), lambda i: (0, 0))],
        out_specs=[row(0), row(0)],
        out_shape=[jax.ShapeDtypeStruct((S, C), BF16), jax.ShapeDtypeStruct((S, C), BF16)],
        scratch_shapes=[pltpu.VMEM((ts + SC_HALO, C), F32)],
        compiler_params=_cparams(("arbitrary",)),
    )(z, z, z, w)


def _sconv_bwd(z, cc, dy, w, *, name):
    S = z.shape[0]
    C = SC_CH
    ts = _div_tile(S, 256, 16)
    n = S // ts

    def body(gb_ref, gc_ref, v_ref, cc_ref, dy_ref, w_ref, dz_ref, dw_ref, win):
        i = pl.program_id(0)

        @pl.when(i == 0)
        def _():
            win[pl.ds(ts, SC_HALO), :] = jnp.zeros((SC_HALO, C), F32)
            dw_ref[...] = jnp.zeros(dw_ref.shape, F32)

        @pl.when(i > 0)
        def _():
            win[pl.ds(ts, SC_HALO), :] = win[pl.ds(0, SC_HALO), :]

        dyv = dy_ref[...].astype(F32)
        gb = gb_ref[...].astype(F32)
        gc = gc_ref[...].astype(F32)
        val = v_ref[...].astype(F32)
        dz_ref[:, pl.ds(0, C)] = (dyv * cc_ref[...].astype(F32)).astype(BF16)
        win[pl.ds(0, ts), :] = dyv * gb
        cv = gc * val
        dcv = jnp.zeros((ts, C), F32)
        for k in range(SC_WIDTH):
            sh = win[pl.ds(SC_WIDTH - 1 - k, ts), :]
            dcv = dcv + w_ref[pl.ds(k, 1), :] * sh
            dw_ref[pl.ds(k, 1), :] += jnp.sum(cv * sh, axis=0, keepdims=True)
        dz_ref[:, pl.ds(C, C)] = (dcv * val).astype(BF16)
        dz_ref[:, pl.ds(2 * C, C)] = (dcv * gc).astype(BF16)

    row = lambda col: pl.BlockSpec((ts, C), lambda i, col=col: (n - 1 - i, col))
    return pl.pallas_call(
        body, name=name, grid=(n,),
        in_specs=[row(0), row(1), row(2), row(0), row(0), pl.BlockSpec((8, C), lambda i: (0, 0))],
        out_specs=[pl.BlockSpec((ts, 3 * C), lambda i: (n - 1 - i, 0)), pl.BlockSpec((8, C), lambda i: (0, 0))],
        out_shape=[jax.ShapeDtypeStruct((S, 3 * C), BF16), jax.ShapeDtypeStruct((8, C), F32)],
        scratch_shapes=[pltpu.VMEM((ts + SC_HALO, C), F32)],
        compiler_params=_cparams(("arbitrary",)),
    )(z, z, z, cc, dy, w)


SWA_Q_COL = 2
SWA_SLOPES = [2.0 ** (-8.0 * (h + 1) / SWA_HEADS) for h in range(SWA_HEADS)]
SWA_SCALE = HEAD_DIM ** -0.5


SWA_GROUP_ROWS = SWA_GROUP * WINDOW


def _swa_masks():
    shape = (SWA_GROUP_ROWS, 2 * WINDOW)
    ii = lax.broadcasted_iota(jnp.int32, shape, 0)
    jj = lax.broadcasted_iota(jnp.int32, shape, 1)
    dist = (ii & (WINDOW - 1)) + WINDOW - jj
    valid = (dist >= 0) & (dist < WINDOW)
    grp = lax.broadcasted_iota(jnp.int32, (SWA_GROUP_ROWS, 1), 0) // WINDOW
    return dist.astype(F32), valid, jj, grp


def _by_group(grp, vals):
    out = jnp.full(grp.shape, vals[SWA_GROUP - 1], F32)
    for g in range(SWA_GROUP - 2, -1, -1):
        out = jnp.where(grp == g, vals[g], out)
    return out


def _stack_heads(ref, rows, kv):
    return jnp.concatenate([ref[rows, pl.ds(HEAD_DIM * (kv * SWA_GROUP + g), HEAD_DIM)] for g in range(SWA_GROUP)],
                           axis=0)


def _swa_probs(qg, kk, sink, slope, distf, valid):
    s = lax.dot_general(qg, kk, (((1,), (1,)), ((), ())), preferred_element_type=F32) * SWA_SCALE
    s = s - slope * distf
    s = jnp.where(valid, s, NEG_BIG)
    m = jnp.maximum(jnp.max(s, axis=-1, keepdims=True), sink)
    p = jnp.exp(s - m)
    l = jnp.sum(p, axis=-1, keepdims=True) + jnp.exp(sink - m)
    return p, m, l


def _swa_fwd(z, kpad, vpad, sinks, *, name):
    S = z.shape[0]
    tq = _div_tile(S, 256, 128)
    nblk = tq // WINDOW
    W = WINDOW

    def body(sink_ref, q_ref, k_ref, v_ref, o_ref):
        i = pl.program_id(0)
        distf, valid0, jj, grp = _swa_masks()
        for kv in range(SWA_KV_HEADS):
            heads = range(kv * SWA_GROUP, (kv + 1) * SWA_GROUP)
            sink = _by_group(grp, [sink_ref[h] for h in heads])
            slope = _by_group(grp, [SWA_SLOPES[h] for h in heads])
            for b in range(nblk):
                nb = i * nblk + b
                start = pl.multiple_of(nb * W, W)
                rows = pl.ds(W * b, W)
                valid = valid0 & ((jj >= W) | (nb > 0))
                kk = k_ref[pl.ds(start, 2 * W), pl.ds(HEAD_DIM * kv, HEAD_DIM)]
                vv = v_ref[pl.ds(start, 2 * W), pl.ds(HEAD_DIM * kv, HEAD_DIM)]
                p, m, l = _swa_probs(_stack_heads(q_ref, rows, kv), kk, sink, slope, distf, valid)
                o = (jnp.dot(p.astype(BF16), vv, preferred_element_type=F32) / l).astype(BF16)
                for g, h in enumerate(heads):
                    o_ref[rows, pl.ds(HEAD_DIM * h, HEAD_DIM)] = o[W * g:W * (g + 1)]

    full = pl.BlockSpec((S + W, 2 * HEAD_DIM), lambda i: (0, 0))
    return pl.pallas_call(
        body, name=name, grid=(S // tq,),
        in_specs=[pl.BlockSpec(memory_space=pltpu.SMEM), pl.BlockSpec((tq, 512), lambda i: (i, SWA_Q_COL)), full, full],
        out_specs=pl.BlockSpec((tq, 512), lambda i: (i, 0)),
        out_shape=jax.ShapeDtypeStruct((S, 512), BF16),
        compiler_params=_cparams(("parallel",)),
    )(sinks, z, kpad, vpad)


def _swa_bwd(z, kpad, vpad, sinks, dcat, *, name):
    S = z.shape[0]
    tq = _div_tile(S, 256, 128)
    nblk = tq // WINDOW
    W = WINDOW

    def body(sink_ref, q_ref, k_ref, v_ref, do_ref, dq_ref, dk_ref, dv_ref, ds_ref):
        i = pl.program_id(0)

        @pl.when(i == 0)
        def _():
            dk_ref[...] = jnp.zeros(dk_ref.shape, F32)
            dv_ref[...] = jnp.zeros(dv_ref.shape, F32)
            ds_ref[...] = jnp.zeros(ds_ref.shape, F32)

        distf, valid0, jj, grp = _swa_masks()
        tn = (((0,), (0,)), ((), ()))
        for kv in range(SWA_KV_HEADS):
            heads = range(kv * SWA_GROUP, (kv + 1) * SWA_GROUP)
            sink = _by_group(grp, [sink_ref[h] for h in heads])
            slope = _by_group(grp, [SWA_SLOPES[h] for h in heads])
            for b in range(nblk):
                nb = i * nblk + b
                start = pl.multiple_of(nb * W, W)
                rows = pl.ds(W * b, W)
                valid = valid0 & ((jj >= W) | (nb > 0))
                kk = k_ref[pl.ds(start, 2 * W), pl.ds(HEAD_DIM * kv, HEAD_DIM)]
                vv = v_ref[pl.ds(start, 2 * W), pl.ds(HEAD_DIM * kv, HEAD_DIM)]
                qg = _stack_heads(q_ref, rows, kv)
                dog = _stack_heads(do_ref, rows, kv)
                p, m, l = _swa_probs(qg, kk, sink, slope, distf, valid)
                inv_l = 1.0 / l
                pn = p * inv_l
                dp = lax.dot_general(dog, vv, (((1,), (1,)), ((), ())), preferred_element_type=F32)
                delta = jnp.sum(pn * dp, axis=-1, keepdims=True)
                dsc = (pn * (dp - delta)).astype(BF16)
                dsink = jnp.exp(sink - m) * inv_l * delta
                dq = (jnp.dot(dsc, kk, preferred_element_type=F32) * SWA_SCALE).astype(BF16)
                for g, h in enumerate(heads):
                    ds_ref[pl.ds(h, 1), :] += jnp.broadcast_to(
                        -jnp.sum(dsink[W * g:W * (g + 1)], axis=0, keepdims=True), (1, 128))
                    dq_ref[rows, pl.ds(HEAD_DIM * h, HEAD_DIM)] = dq[W * g:W * (g + 1)]
                dk_ref[pl.ds(start, 2 * W), pl.ds(HEAD_DIM * kv, HEAD_DIM)] += lax.dot_general(
                    dsc, qg, tn, preferred_element_type=F32) * SWA_SCALE
                dv_ref[pl.ds(start, 2 * W), pl.ds(HEAD_DIM * kv, HEAD_DIM)] += lax.dot_general(
                    pn.astype(BF16), dog, tn, preferred_element_type=F32)

    full = pl.BlockSpec((S + W, 2 * HEAD_DIM), lambda i: (0, 0))
    return pl.pallas_call(
        body, name=name, grid=(S // tq,),
        in_specs=[pl.BlockSpec(memory_space=pltpu.SMEM), pl.BlockSpec((tq, 512), lambda i: (i, SWA_Q_COL)), full, full,
                  pl.BlockSpec((tq, 512), lambda i: (i, 1))],
        out_specs=[pl.BlockSpec((tq, 512), lambda i: (i, 0)), full, full, pl.BlockSpec((8, 128), lambda i: (0, 0))],
        out_shape=[jax.ShapeDtypeStruct((S, 512), BF16), jax.ShapeDtypeStruct((S + W, 2 * HEAD_DIM), F32),
                   jax.ShapeDtypeStruct((S + W, 2 * HEAD_DIM), F32), jax.ShapeDtypeStruct((8, 128), F32)],
        compiler_params=_cparams(("arbitrary",)),
    )(sinks, z, kpad, vpad, dcat)


XA_SCALE = XA_HEAD_DIM ** -0.5


def _xa_probs(qh, kh):
    s = lax.dot_general(qh, kh, (((1,), (1,)), ((), ())), preferred_element_type=F32) * XA_SCALE
    m = jnp.max(s, axis=-1, keepdims=True)
    p = jnp.exp(s - m)
    return p, jnp.sum(p, axis=-1, keepdims=True)


def _xa_fwd(q, kv, *, name):
    S, D = q.shape
    M = kv.shape[0]
    ts = _div_tile(S, 512, 16)
    HD = XA_HEAD_DIM

    def body(q_ref, k_ref, v_ref, o_ref):
        for h in range(XA_HEADS):
            qh = q_ref[:, pl.ds(HD * h, HD)]
            p, l = _xa_probs(qh, k_ref[:, pl.ds(HD * h, HD)])
            o = jnp.dot(p.astype(BF16), v_ref[:, pl.ds(HD * h, HD)], preferred_element_type=F32) / l
            o_ref[:, pl.ds(HD * h, HD)] = o.astype(BF16)

    return pl.pallas_call(
        body, name=name, grid=(S // ts,),
        in_specs=[pl.BlockSpec((ts, D), lambda i: (i, 0)), pl.BlockSpec((M, D), lambda i: (0, 0)),
                  pl.BlockSpec((M, D), lambda i: (0, 1))],
        out_specs=pl.BlockSpec((ts, D), lambda i: (i, 0)),
        out_shape=jax.ShapeDtypeStruct((S, D), BF16),
        compiler_params=_cparams(("parallel",)),
    )(q, kv, kv)


def _xa_bwd(q, kv, do, *, name):
    S, D = q.shape
    M = kv.shape[0]
    ts = _div_tile(S, 512, 16)
    HD = XA_HEAD_DIM

    def body(q_ref, k_ref, v_ref, do_ref, dq_ref, dkv_ref):
        i = pl.program_id(0)

        @pl.when(i == 0)
        def _():
            dkv_ref[...] = jnp.zeros(dkv_ref.shape, F32)

        for h in range(XA_HEADS):
            qh = q_ref[:, pl.ds(HD * h, HD)]
            kh = k_ref[:, pl.ds(HD * h, HD)]
            vh = v_ref[:, pl.ds(HD * h, HD)]
            doh = do_ref[:, pl.ds(HD * h, HD)]
            p, l = _xa_probs(qh, kh)
            pn = p * (1.0 / l)
            dp = lax.dot_general(doh, vh, (((1,), (1,)), ((), ())), preferred_element_type=F32)
            delta = jnp.sum(pn * dp, axis=-1, keepdims=True)
            dsc = (pn * (dp - delta)).astype(BF16)
            dq_ref[:, pl.ds(HD * h, HD)] = (jnp.dot(dsc, kh, preferred_element_type=F32) * XA_SCALE).astype(BF16)
            dkv_ref[:, pl.ds(HD * h, HD)] += lax.dot_general(
                dsc, qh, (((0,), (0,)), ((), ())), preferred_element_type=F32) * XA_SCALE
            dkv_ref[:, pl.ds(D + HD * h, HD)] += lax.dot_general(
                pn.astype(BF16), doh, (((0,), (0,)), ((), ())), preferred_element_type=F32)

    row = pl.BlockSpec((ts, D), lambda i: (i, 0))
    return pl.pallas_call(
        body, name=name, grid=(S // ts,),
        in_specs=[row, pl.BlockSpec((M, D), lambda i: (0, 0)), pl.BlockSpec((M, D), lambda i: (0, 1)), row],
        out_specs=[row, pl.BlockSpec((M, 2 * D), lambda i: (0, 0))],
        out_shape=[jax.ShapeDtypeStruct((S, D), BF16), jax.ShapeDtypeStruct((M, 2 * D), F32)],
        compiler_params=_cparams(("arbitrary",)),
    )(q, kv, kv, do)


def _adam_math(w, g, m, v):
    m = ADAM_B1 * m + (1.0 - ADAM_B1) * g
    v = ADAM_B2 * v + (1.0 - ADAM_B2) * (g * g)
    m_hat = m / (1.0 - ADAM_B1 ** ADAM_STEP)
    v_hat = v / (1.0 - ADAM_B2 ** ADAM_STEP)
    delta = -ADAM_LR * (m_hat / (jnp.sqrt(v_hat) + ADAM_EPS) + ADAM_WD * w)
    return delta, m, v


def _adamw(w, g, m, v, *, name):
    R, C = w.shape
    tr = _div_tile(R, max(8, (256 * 1024) // C // 8 * 8))

    def body(w_ref, g_ref, m_ref, v_ref, d_ref, nm_ref, nv_ref):
        d, nm, nv = _adam_math(w_ref[...], g_ref[...], m_ref[...], v_ref[...])
        d_ref[...] = d
        nm_ref[...] = nm
        nv_ref[...] = nv

    spec = pl.BlockSpec((tr, C), lambda i: (i, 0))
    sds = jax.ShapeDtypeStruct((R, C), F32)
    return pl.pallas_call(
        body, name=name, grid=(R // tr,), in_specs=[spec] * 4, out_specs=[spec] * 3, out_shape=[sds] * 3,
        compiler_params=_cparams(("parallel",)),
    )(w, g, m, v)


def _adamw_layers(w, m, v, gsrc, *, name):
    L, A, B = w.shape
    tr = _div_tile(A, max(8, (256 * 1024) // B // 8 * 8))
    nt = A // tr
    flat = [a for srcs in gsrc for a in srcs]
    owner = [l for l, srcs in enumerate(gsrc) for _ in srcs]
    ng = len(flat)

    def body(*refs):
        w_ref, m_ref, v_ref = refs[:3]
        g_refs = refs[3:3 + ng]
        g_ref, d_ref, nm_ref, nv_ref = refs[3 + ng:]
        layer = pl.program_id(0)
        g = None
        for l in range(L):
            gl = None
            for a_ref, o in zip(g_refs, owner):
                if o == l:
                    gl = a_ref[...] if gl is None else gl + a_ref[...]
            g = gl if g is None else jnp.where(layer == l, gl, g)
        d, nm, nv = _adam_math(w_ref[...], g, m_ref[...], v_ref[...])
        g_ref[...] = g
        d_ref[...] = d
        nm_ref[...] = nm
        nv_ref[...] = nv

    def src_spec(o):
        return pl.BlockSpec((None, tr, B),
                            lambda l, i: (0, jnp.where(l == o, i, jnp.where(l > o, nt - 1, 0)), 0))

    spec = pl.BlockSpec((None, tr, B), lambda l, i: (l, i, 0))
    sds = jax.ShapeDtypeStruct((L, A, B), F32)
    return pl.pallas_call(
        body, name=name, grid=(L, nt), in_specs=[spec] * 3 + [src_spec(o) for o in owner], out_specs=[spec] * 4,
        out_shape=[sds] * 4, compiler_params=_cparams(("arbitrary", "arbitrary")),
    )(w, m, v, *flat)


def _adamw_small(w, gparts, m, v, *, name):
    R, C = w.shape

    def body(w_ref, gp_ref, m_ref, v_ref, g_ref, d_ref, nm_ref, nv_ref):
        g = gp_ref[0]
        for k in range(1, N_DEV):
            g = g + gp_ref[k]
        d, nm, nv = _adam_math(w_ref[...], g, m_ref[...], v_ref[...])
        g_ref[...] = g
        d_ref[...] = d
        nm_ref[...] = nm
        nv_ref[...] = nv

    sds = jax.ShapeDtypeStruct((R, C), F32)
    return pl.pallas_call(body, name=name, out_shape=[sds] * 4, compiler_params=_cparams())(w, gparts, m, v)


ANY = pl.BlockSpec(memory_space=pl.ANY)


def _mesh_pos():
    return lax.axis_index("x"), lax.axis_index("y"), lax.axis_index("c")


def _other_chips(x, y):
    return [(1 - x, y), (x, 1 - y), (1 - x, 1 - y)]


LAYOUT = {'ffn1_w_gu': 'col', 'ffn1_w_down': 'stk', 'even_w_in': 'stk', 'even_w_out': 'stk', 'odd_w_in': 'col',
          'odd_w_out': 'stk', 'xa_wq': 'stk', 'xa_wkv': 'col', 'xa_wo': 'stk', 'ffn2_w_gu': 'col',
          'ffn2_w_down': 'stk', 'tiny': 'stk'}
COMM_NAMES = list(LAYOUT)
TINY_ROWS = 48


def _piece_rows(L, A):
    return A if L == 2 else A // 2


def _shard_piece(ref, L, A, h):
    if L == 2:
        return ref.at[h]
    return ref.at[0, pl.ds(pl.multiple_of(h * (A // 2), 8), A // 2)]


def _gathered_piece(ref, kind, L, A, h):
    if L == 2:
        return ref.at[h]
    rows = pl.ds(pl.multiple_of(h * (A // 2), 8), A // 2)
    return ref.at[0, rows] if kind == 'col' else ref.at[0, :, rows]


def _chip_part(piece, kind, B, s):
    if kind == 'col':
        return piece.at[:, pl.ds(pl.multiple_of(s * B, 128), B)]
    return piece.at[s]


def _place(shard, layer, kind, chip_idx, out_dtype, *, name):
    L, A, B = shard.shape
    ta = _div_tile(A, 256, 16)

    def body(s_ref, x_ref, o_ref):
        o_ref[...] = x_ref[...].astype(out_dtype)

    if kind == 'col':
        shape = (1, A, N_CHIPS * B)
        out_spec = pl.BlockSpec((None, ta, B), lambda i, s: (0, i, s[0]))
    else:
        shape = (1, N_CHIPS, A, B)
        out_spec = pl.BlockSpec((None, None, ta, B), lambda i, s: (0, s[0], i, 0))
    grid_spec = pltpu.PrefetchScalarGridSpec(
        num_scalar_prefetch=1, grid=(A // ta,),
        in_specs=[pl.BlockSpec((None, ta, B), lambda i, s: (layer, i, 0))], out_specs=out_spec)
    return pl.pallas_call(
        body, name=name, grid_spec=grid_spec, out_shape=jax.ShapeDtypeStruct(shape, out_dtype),
        compiler_params=_cparams(("parallel",)),
    )(chip_idx, shard)


HBM = pl.BlockSpec(memory_space=pltpu.HBM)
SEM = pl.BlockSpec(memory_space=pltpu.SEMAPHORE)
DATAFLOW = pltpu.SideEffectType.DATAFLOW_SIDE_EFFECTING


def _own_part_copies(refs, meta, send_sems, recv_sems):
    x, y, c = _mesh_pos()
    cps = []
    for k, (kind, L, A, B) in enumerate(meta):
        for j, (cx, cy) in enumerate(_other_chips(x, y)):
            part = _chip_part(refs[k].at[0], kind, B, 2 * x + y)
            cps.append(pltpu.make_async_remote_copy(
                src_ref=part, dst_ref=part, send_sem=send_sems.at[3 * k + j], recv_sem=recv_sems.at[3 * k + j],
                device_id=(cx, cy, c), device_id_type=MESH))
    return cps


def _gather_start(fulls, meta, after, tag):
    n = len(fulls)

    def body(*refs):
        send_sems, recv_sems = refs[n + 1], refs[n + 2]
        outs = refs[n + 3:2 * n + 3]
        token = refs[2 * n + 3]
        for cp in _own_part_copies(outs, meta, send_sems, recv_sems):
            cp.start()
        token[...] = jnp.zeros_like(token)

    res = pl.pallas_call(
        body, name=f"ag_start_{tag}", in_specs=[HBM] * n + [pl.BlockSpec(memory_space=pl.ANY)],
        out_specs=(SEM, SEM) + (HBM,) * n + (pl.BlockSpec(memory_space=pltpu.VMEM),),
        out_shape=(pltpu.SemaphoreType.DMA((3 * n,)), pltpu.SemaphoreType.DMA((3 * n,)))
        + tuple(pltpu.HBM(f.shape, f.dtype) for f in fulls) + (jax.ShapeDtypeStruct((8, 128), F32),),
        input_output_aliases={k: 2 + k for k in range(n)},
        compiler_params=pltpu.CompilerParams(has_side_effects=DATAFLOW),
    )(*[pltpu.with_memory_space_constraint(f, pltpu.HBM) for f in fulls], after)
    return res[0], res[1], list(res[2:2 + n]), res[2 + n]


def _gather_wait(send_sems, recv_sems, fulls, meta, after, tag):
    n = len(fulls)

    def body(*refs):
        f_refs = refs[:n]
        send_sems, recv_sems = refs[n], refs[n + 1]
        for cp in _own_part_copies(f_refs, meta, send_sems, recv_sems):
            cp.wait_send()
            cp.wait_recv()

    return pl.pallas_call(
        body, name=f"ag_wait_{tag}", in_specs=[HBM] * n + [SEM, SEM, pl.BlockSpec(memory_space=pl.ANY)],
        out_specs=[HBM] * n, out_shape=[pltpu.HBM(f.shape, f.dtype) for f in fulls],
        input_output_aliases={k: k for k in range(n)},
        compiler_params=pltpu.CompilerParams(has_side_effects=DATAFLOW),
    )(*fulls, send_sems, recv_sems, after)


def _scatter_copies(g_refs, land_refs, meta, send_sems, recv_sems):
    x, y, c = _mesh_pos()
    cps = []
    for k, (kind, L, A, B) in enumerate(meta):
        for j, (cx, cy) in enumerate(_other_chips(x, y)):
            cps.append(pltpu.make_async_remote_copy(
                src_ref=_chip_part(g_refs[k].at[0], kind, B, 2 * cx + cy), dst_ref=land_refs[k].at[j],
                send_sem=send_sems.at[3 * k + j], recv_sem=recv_sems.at[3 * k + j], device_id=(cx, cy, c),
                device_id_type=MESH))
    return cps


def _scatter_start(gs, meta, after, tag):
    n = len(gs)

    def body(*refs):
        send_sems, recv_sems = refs[2 * n + 1], refs[2 * n + 2]
        g_out = refs[2 * n + 3:3 * n + 3]
        lands = refs[3 * n + 3:4 * n + 3]
        token = refs[4 * n + 3]
        for cp in _scatter_copies(g_out, lands, meta, send_sems, recv_sems):
            cp.start()
        token[...] = jnp.zeros_like(token)

    land_shapes = [(3, A, B) for kind, L, A, B in meta]
    lands = [pltpu.with_memory_space_constraint(lax.empty(s, g.dtype), pltpu.HBM) for s, g in zip(land_shapes, gs)]
    res = pl.pallas_call(
        body, name=f"rs_start_{tag}", in_specs=[HBM] * (2 * n) + [pl.BlockSpec(memory_space=pl.ANY)],
        out_specs=(SEM, SEM) + (HBM,) * (2 * n) + (pl.BlockSpec(memory_space=pltpu.VMEM),),
        out_shape=(pltpu.SemaphoreType.DMA((3 * n,)), pltpu.SemaphoreType.DMA((3 * n,)))
        + tuple(pltpu.HBM(g.shape, g.dtype) for g in gs)
        + tuple(pltpu.HBM(s, g.dtype) for s, g in zip(land_shapes, gs)) + (jax.ShapeDtypeStruct((8, 128), F32),),
        input_output_aliases={k: 2 + k for k in range(2 * n)},
        compiler_params=pltpu.CompilerParams(has_side_effects=DATAFLOW),
    )(*[pltpu.with_memory_space_constraint(g, pltpu.HBM) for g in gs], *lands, after)
    return res[0], res[1], list(res[2:2 + n]), list(res[2 + n:2 + 2 * n]), res[2 + 2 * n]


def _scatter_wait(send_sems, recv_sems, gs, lands, meta, after, tag):
    n = len(gs)

    def body(*refs):
        g_refs, land_refs = refs[:n], refs[n:2 * n]
        send_sems, recv_sems = refs[2 * n], refs[2 * n + 1]
        for cp in _scatter_copies(g_refs, land_refs, meta, send_sems, recv_sems):
            cp.wait_send()
            cp.wait_recv()

    both = list(gs) + list(lands)
    res = pl.pallas_call(
        body, name=f"rs_wait_{tag}", in_specs=[HBM] * (2 * n) + [SEM, SEM, pl.BlockSpec(memory_space=pl.ANY)],
        out_specs=[HBM] * (2 * n), out_shape=[pltpu.HBM(a.shape, a.dtype) for a in both],
        input_output_aliases={k: k for k in range(2 * n)},
        compiler_params=pltpu.CompilerParams(has_side_effects=DATAFLOW),
    )(*both, send_sems, recv_sems, after)
    return list(res[:n]), list(res[n:])


def _chip_sum_full(g, got, m, chip_idx, *, name):
    kind, L, A, B = m
    ta = _div_tile(A, 256, 16)

    def body(r_ref, a_ref, b_ref, o_ref):
        acc = a_ref[...].astype(F32)
        for j in range(3):
            acc = acc + b_ref[j].astype(F32)
        o_ref[...] = acc

    if kind == 'col':
        g_spec = pl.BlockSpec((None, ta, B), lambda i, r: (0, i, r[0]))
    else:
        g_spec = pl.BlockSpec((None, None, ta, B), lambda i, r: (0, r[0], i, 0))
    grid_spec = pltpu.PrefetchScalarGridSpec(
        num_scalar_prefetch=1, grid=(A // ta,),
        in_specs=[g_spec, pl.BlockSpec((3, ta, B), lambda i, r: (0, i, 0))],
        out_specs=pl.BlockSpec((None, ta, B), lambda i, r: (0, i, 0)))
    return pl.pallas_call(
        body, name=name, grid_spec=grid_spec, out_shape=jax.ShapeDtypeStruct((1, A, B), F32),
        compiler_params=_cparams(("parallel",)),
    )(chip_idx, g, got)


def _all_gather(fulls, meta):
    n = len(fulls)

    def body(*refs):
        outs = refs[n:2 * n]
        send_sems, recv_sems = refs[2 * n:]
        x, y, c = _mesh_pos()
        sibling = (x, y, 1 - c)
        chips = _other_chips(x, y)

        def part(k, s, h):
            kind, L, A, B = meta[k]
            return _chip_part(_gathered_piece(outs[k], kind, L, A, h), kind, B, s)

        def copy(ref, sem, to):
            return pltpu.make_async_remote_copy(src_ref=ref, dst_ref=ref, send_sem=send_sems.at[sem],
                                                recv_sem=recv_sems.at[sem], device_id=to, device_id_type=MESH)

        started = []
        for k in range(n):
            for j, (cx, cy) in enumerate(chips):
                cp = copy(part(k, 2 * x + y, c), 3 * k + j, (cx, cy, c))
                cp.start()
                started.append(cp)
        for j, (cx, cy) in enumerate(chips):
            for k in range(n):
                landed = part(k, 2 * cx + cy, c)
                copy(landed, 3 * k + j, (cx, cy, c)).wait_recv()
                fwd = copy(landed, 3 * n + 3 * k + j, sibling)
                fwd.start()
                started.append(fwd)
        for j, (cx, cy) in enumerate(chips):
            for k in range(n):
                copy(part(k, 2 * cx + cy, 1 - c), 3 * n + 3 * k + j, sibling).wait_recv()
        for cp in started:
            cp.wait_send()

    return pl.pallas_call(
        body, name="ag_weights", in_specs=[ANY] * n, out_specs=[ANY] * n,
        out_shape=[jax.ShapeDtypeStruct(f.shape, f.dtype) for f in fulls],
        input_output_aliases={k: k for k in range(n)},
        scratch_shapes=[pltpu.SemaphoreType.DMA((6 * n,)), pltpu.SemaphoreType.DMA((6 * n,))],
    )(*fulls)


def _pair_exchange(gs, meta):
    n = len(gs)

    def body(*refs):
        g_refs, out_refs = refs[:n], refs[n:2 * n]
        send_sems, recv_sems = refs[2 * n:]
        x, y, c = _mesh_pos()
        cps = []
        for k in range(n):
            kind, L, A, B = meta[k]
            cp = pltpu.make_async_remote_copy(
                src_ref=_gathered_piece(g_refs[k], kind, L, A, 1 - c), dst_ref=out_refs[k],
                send_sem=send_sems.at[k], recv_sem=recv_sems.at[k], device_id=(x, y, 1 - c), device_id_type=MESH)
            cp.start()
            cps.append(cp)
        for cp in cps:
            cp.wait()

    shapes = []
    for g, (kind, L, A, B) in zip(gs, meta):
        ap = _piece_rows(L, A)
        shapes.append(jax.ShapeDtypeStruct((ap, N_CHIPS * B) if kind == 'col' else (N_CHIPS, ap, B), g.dtype))
    return pl.pallas_call(
        body, name="rs_pair_exchange", in_specs=[ANY] * n, out_specs=[ANY] * n, out_shape=shapes,
        scratch_shapes=[pltpu.SemaphoreType.DMA((n,)), pltpu.SemaphoreType.DMA((n,))],
    )(*gs)


def _pair_sum(g, got, m, c_idx, *, name):
    kind, L, A, B = m
    ap = _piece_rows(L, A)
    ta = _div_tile(ap, 256, 16)
    nt = ap // ta
    dt = g.dtype

    def body(c_ref, a_ref, b_ref, o_ref):
        o_ref[...] = (a_ref[...].astype(F32) + b_ref[...].astype(F32)).astype(dt)

    if kind == 'col':
        grid = (nt,)
        gmap = (lambda i, c: (c[0], i, 0)) if L == 2 else (lambda i, c: (0, c[0] * nt + i, 0))
        g_spec = pl.BlockSpec((None, ta, N_CHIPS * B), gmap)
        r_spec = pl.BlockSpec((ta, N_CHIPS * B), lambda i, c: (i, 0))
        shape = (ap, N_CHIPS * B)
        sem = ("parallel",)
    else:
        grid = (N_CHIPS, nt)
        gmap = (lambda s, i, c: (c[0], s, i, 0)) if L == 2 else (lambda s, i, c: (0, s, c[0] * nt + i, 0))
        g_spec = pl.BlockSpec((None, None, ta, B), gmap)
        r_spec = pl.BlockSpec((None, ta, B), lambda s, i, c: (s, i, 0))
        shape = (N_CHIPS, ap, B)
        sem = ("parallel", "parallel")
    grid_spec = pltpu.PrefetchScalarGridSpec(num_scalar_prefetch=1, grid=grid, in_specs=[g_spec, r_spec],
                                             out_specs=r_spec)
    return pl.pallas_call(
        body, name=name, grid_spec=grid_spec, out_shape=jax.ShapeDtypeStruct(shape, dt), compiler_params=_cparams(sem),
    )(c_idx, g, got)


def _chip_exchange(ps, meta):
    n = len(ps)

    def body(*refs):
        p_refs, out_refs = refs[:n], refs[n:2 * n]
        send_sems, recv_sems = refs[2 * n:]
        x, y, c = _mesh_pos()
        cps = []
        for k in range(n):
            kind, L, A, B = meta[k]
            for j, (cx, cy) in enumerate(_other_chips(x, y)):
                cp = pltpu.make_async_remote_copy(
                    src_ref=_chip_part(p_refs[k], kind, B, 2 * cx + cy), dst_ref=out_refs[k].at[j],
                    send_sem=send_sems.at[3 * k + j], recv_sem=recv_sems.at[3 * k + j], device_id=(cx, cy, c),
                    device_id_type=MESH)
                cp.start()
                cps.append(cp)
        for cp in cps:
            cp.wait()

    shapes = [jax.ShapeDtypeStruct((3, _piece_rows(L, A), B), p.dtype) for p, (kind, L, A, B) in zip(ps, meta)]
    return pl.pallas_call(
        body, name="rs_chip_exchange", in_specs=[ANY] * n, out_specs=[ANY] * n, out_shape=shapes,
        scratch_shapes=[pltpu.SemaphoreType.DMA((3 * n,)), pltpu.SemaphoreType.DMA((3 * n,))],
    )(*ps)


def _chip_sum(p, got, m, sc_idx, *, name):
    kind, L, A, B = m
    ap = _piece_rows(L, A)
    ta = _div_tile(ap, 256, 16)
    nt = ap // ta

    def body(r_ref, a_ref, b_ref, o_ref):
        acc = a_ref[...].astype(F32)
        for j in range(3):
            acc = acc + b_ref[j].astype(F32)
        o_ref[...] = acc

    if kind == 'col':
        p_spec = pl.BlockSpec((ta, B), lambda i, r: (i, r[0]))
    else:
        p_spec = pl.BlockSpec((None, ta, B), lambda i, r: (r[0], i, 0))
    omap = (lambda i, r: (r[1], i, 0)) if L == 2 else (lambda i, r: (0, r[1] * nt + i, 0))
    grid_spec = pltpu.PrefetchScalarGridSpec(
        num_scalar_prefetch=1, grid=(nt,),
        in_specs=[p_spec, pl.BlockSpec((3, ta, B), lambda i, r: (0, i, 0))],
        out_specs=pl.BlockSpec((None, ta, B), omap))
    return pl.pallas_call(
        body, name=name, grid_spec=grid_spec, out_shape=jax.ShapeDtypeStruct((L, A, B), F32),
        compiler_params=_cparams(("parallel",)),
    )(sc_idx, p, got)


def _final_exchange(gls, meta, sums, small):
    n = len(gls)
    ns = len(sums)
    rs, cs = small.shape

    def body(*refs):
        sum_refs = refs[n:n + ns]
        small_ref = refs[n + ns]
        outs = refs[n + ns + 1:2 * n + ns + 1]
        got_refs = refs[2 * n + ns + 1:2 * n + 2 * ns + 1]
        sm_ref = refs[2 * n + 2 * ns + 1]
        send_sems, recv_sems, local_sem = refs[2 * n + 2 * ns + 2:]
        x, y, c = _mesh_pos()
        me = 4 * x + 2 * y + c
        own_s = pltpu.make_async_copy(small_ref, sm_ref.at[me], local_sem)
        own_s.start()
        cps = []
        for k in range(n):
            kind, L, A, B = meta[k]
            half = _shard_piece(outs[k], L, A, c)
            cp = pltpu.make_async_remote_copy(src_ref=half, dst_ref=half, send_sem=send_sems.at[k],
                                              recv_sem=recv_sems.at[k], device_id=(x, y, 1 - c), device_id_type=MESH)
            cp.start()
            cps.append(cp)
        for k in range(ns):
            cp = pltpu.make_async_remote_copy(
                src_ref=sum_refs[k], dst_ref=got_refs[k], send_sem=send_sems.at[n + N_DEV + k],
                recv_sem=recv_sems.at[n + N_DEV + k], device_id=(x, y, 1 - c), device_id_type=MESH)
            cp.start()
            cps.append(cp)
        for r in range(1, N_DEV):
            fx, fy, fc = (r >> 2) & 1, (r >> 1) & 1, r & 1
            peer = (1 - x if fx else x, 1 - y if fy else y, 1 - c if fc else c)
            cp = pltpu.make_async_remote_copy(
                src_ref=small_ref, dst_ref=sm_ref.at[me], send_sem=send_sems.at[n + r], recv_sem=recv_sems.at[n + r],
                device_id=peer, device_id_type=MESH)
            cp.start()
            cps.append(cp)
        for cp in cps:
            cp.wait()
        own_s.wait()

    res = pl.pallas_call(
        body, name="rs_final_exchange", in_specs=[ANY] * (n + ns + 1), out_specs=[ANY] * (n + ns + 1),
        out_shape=[jax.ShapeDtypeStruct(g.shape, g.dtype) for g in list(gls) + list(sums)]
        + [jax.ShapeDtypeStruct((N_DEV, rs, cs), F32)],
        input_output_aliases={k: k for k in range(n)},
        scratch_shapes=[pltpu.SemaphoreType.DMA((n + N_DEV + ns,)), pltpu.SemaphoreType.DMA((n + N_DEV + ns,)),
                        pltpu.SemaphoreType.DMA],
    )(*gls, *sums, small)
    return res[:n], res[n:n + ns], res[n + ns]


def _tiny_pack(conv_a_w, sc_conv_w):
    lead = conv_a_w.shape[:-2]
    sc = sc_conv_w.reshape(lead + (2 * SC_WIDTH, 128))
    z = lambda r: jnp.zeros(lead + (r, 128), F32)
    return jnp.concatenate([conv_a_w, z(32 - CONV_A_WIDTH), sc, z(TINY_ROWS - 32 - 2 * SC_WIDTH)], axis=-2)


def _tiny_unpack(t):
    lead = t.shape[:-2]
    return t[..., :CONV_A_WIDTH, :], t[..., 32:32 + 2 * SC_WIDTH, :].reshape(lead + (SC_WIDTH, 256))


def _pack_small(d):
    flat = jnp.concatenate([d[n].astype(F32).reshape(-1) for n in SMALL_NAMES])
    n = flat.shape[0]
    total = -(-n // 1024) * 1024
    return jnp.pad(flat, (0, total - n)).reshape(total // 128, 128)


def _unpack_small(packed, shapes):
    flat = packed.reshape(-1)
    out, off = {}, 0
    for n in SMALL_NAMES:
        sz = math.prod(shapes[n])
        out[n] = flat[off:off + sz].reshape(shapes[n])
        off += sz
    return out


def _ffn_fwd(h, g, W, n_gu, n_down, i, tag):
    h2, u, gate, up, a = _ffn_fwd_fused(h, g, W[n_gu][i], W[n_down][i], 0, name=f"{tag}_fwd")
    return h2, (h, u, gate, up, a)


def _ffn_bwd(dh, saved, g, W, n_gu, n_down, i, G, tag):
    h, u, gate, up, a = saved
    dh_in, dg, dgate, dup = _ffn_bwd_fused(dh, h, g, gate, up, W[n_gu][i], W[n_down][i], 0, name=f"{tag}_bwd")
    G[(n_down, i)] = _mm(a, dh, name=f"{tag}_b_wdown", ta=True, tm=1408, tn=1024, tk=1024, scale=0.5)
    tn = FFN_CHUNK
    half = _mm(u, dgate, name=f"{tag}_b_wg", ta=True, tm=1024, tn=tn, tk=2048, stack=(1, 0, None, 2 * D_FF))
    G[(n_gu, i)] = _mm(u, dup, name=f"{tag}_b_wu", ta=True, tm=1024, tn=tn, tk=2048, stack=(1, 0, half, 2 * D_FF),
                       n_map=lambda j: j + D_FF // tn)
    return dh_in, dg


def _xa_block_fwd(h, mem, g, gm, W, i, tag):
    mn = _rms_fwd(mem, gm, name=f"{tag}_mem_norm")
    u, q = _norm_mm(h, g, W['xa_wq'][i], 0, name=f"{tag}_q", tn=1024)
    kv = _mm(mn, W['xa_wkv'][i], b_layer=0, name=f"{tag}_kv", tm=256, tn=1024, tk=1024)
    o = _xa_fwd(q, kv, name=f"{tag}_attn")
    h2 = _mm(o, W['xa_wo'][i], b_layer=0, name=f"{tag}_o", out_dtype=F32, tm=1024, tn=1024, tk=1024, res=h)
    return h2, (h, u, mn, q, kv, o)


def _xa_block_bwd(dh, saved, mem, g, gm, W, i, G, tag):
    h, u, mn, q, kv, o = saved
    do = _mm(dh, W['xa_wo'][i], b_layer=0, name=f"{tag}_b_do", tb=True, tm=1024, tn=1024, tk=1024)
    G[('xa_wo', i)] = _mm(o, dh, name=f"{tag}_b_wo", ta=True, tm=1024, tn=1024, tk=1024)
    dq, dkv = _xa_bwd(q, kv, do, name=f"{tag}_b_attn")
    G[('xa_wq', i)] = _mm(u, dq, name=f"{tag}_b_wq", ta=True, tm=1024, tn=1024, tk=1024)
    dh_in, dg = _mm_norm_bwd(dq, W['xa_wq'][i], 0, h, g, dh, name=f"{tag}_b_du", tk=1024)
    G[('xa_wkv', i)] = _mm(mn, dkv, name=f"{tag}_b_wkv", ta=True, tm=1024, tn=1024, tk=256)
    dmn = _mm(dkv, W['xa_wkv'][i], b_layer=0, name=f"{tag}_b_dmn", tb=True, out_dtype=F32, tm=256, tn=1024, tk=1024)
    _, dgm = _rms_bwd(mem, gm, dmn, None, name=f"{tag}_b_mem_norm")
    return dh_in, dg, dgm


def _pad_conv_w(w, rows):
    return jnp.pad(w.astype(F32), ((0, rows - w.shape[0]), (0, 0)))


def _even_fwd(h, g, W, conv_w, conv_b, ln_g, ln_b, sinks, tag):
    u, z = _norm_mm(h, g, W['even_w_in'][0], 0, name=f"{tag}_in", tn=1792)
    c, act = _conv_a_fwd(z, conv_w, conv_b, ln_g, ln_b, name=f"{tag}_conv")
    kpad = jnp.pad(z[:, 1536:1664], ((WINDOW, 0), (0, 0)))
    vpad = jnp.pad(z[:, 1664:1792], ((WINDOW, 0), (0, 0)))
    o = _swa_fwd(z, kpad, vpad, sinks, name=f"{tag}_swa")
    cat = jnp.concatenate([act, o], axis=-1)
    h2 = _mm(cat, W['even_w_out'][0], b_layer=0, name=f"{tag}_out", out_dtype=F32, tm=1024, tn=1024, tk=1024, res=h)
    return h2, (h, u, z, c, kpad, vpad, cat)


def _even_bwd(dh, saved, g, W, conv_w, ln_g, ln_b, sinks, G, tag):
    h, u, z, c, kpad, vpad, cat = saved
    dcat = _mm(dh, W['even_w_out'][0], b_layer=0, name=f"{tag}_b_dcat", tb=True, tm=1024, tn=1024, tk=1024)
    G[('even_w_out', 0)] = _mm(cat, dh, name=f"{tag}_b_wout", ta=True, tm=1024, tn=1024, tk=1024)
    dz_a, small = _conv_a_bwd(z, c, dcat, conv_w, ln_g, ln_b, name=f"{tag}_b_conv")
    dq, dkp, dvp, dsinks = _swa_bwd(z, kpad, vpad, sinks, dcat, name=f"{tag}_b_swa")
    dz = jnp.concatenate([dz_a, dq, dkp[WINDOW:].astype(BF16), dvp[WINDOW:].astype(BF16)], axis=-1)
    G[('even_w_in', 0)] = _mm(u, dz, name=f"{tag}_b_win", ta=True, tm=1024, tn=1792, tk=1024)
    dh_in, dg = _mm_norm_bwd(dz, W['even_w_in'][0], 0, h, g, dh, name=f"{tag}_b_du", tk=1792)
    grads = dict(mix=dg, conv_a_w=small[:CONV_A_WIDTH], conv_a_b=small[32:33], conv_a_ln_g=small[33:34],
                 conv_a_ln_b=small[34:35], swa_sinks=dsinks[:, 0])
    return dh_in, grads


def _odd_fwd(h, g, W, conv_w, tag):
    u, z = _norm_mm(h, g, W['odd_w_in'][0], 0, name=f"{tag}_in", tn=1024)
    y, cc = _sconv_fwd(z, conv_w, name=f"{tag}_conv")
    h2 = _mm(y, W['odd_w_out'][0], b_layer=0, name=f"{tag}_out", out_dtype=F32, tm=1024, tn=1024, tk=1024, res=h)
    return h2, (h, u, z, y, cc)


def _odd_bwd(dh, saved, g, W, conv_w, G, tag):
    h, u, z, y, cc = saved
    dy = _mm(dh, W['odd_w_out'][0], b_layer=0, name=f"{tag}_b_dy", tb=True, tm=1024, tn=1024, tk=1024)
    G[('odd_w_out', 0)] = _mm(y, dh, name=f"{tag}_b_wout", ta=True, tm=1024, tn=1024, tk=1024)
    dz, dw = _sconv_bwd(z, cc, dy, conv_w, name=f"{tag}_b_conv")
    G[('odd_w_in', 0)] = _mm(u, dz, name=f"{tag}_b_win", ta=True, tm=1024, tn=1024, tk=1024)
    dh_in, dg = _mm_norm_bwd(dz, W['odd_w_in'][0], 0, h, g, dh, name=f"{tag}_b_du", tk=1024)
    return dh_in, dict(mix=dg, sc_conv_w=dw[:SC_WIDTH])


def _local_step(x, mem, tgt, W, need, token, ready, conv_a_w, sc_conv_w, P):
    row = lambda v: v.reshape(1, -1)
    conv_a_w = _pad_conv_w(conv_a_w, 32)
    sc_w = _pad_conv_w(sc_conv_w, 8)
    sinks = P['swa_sinks'][0]

    def arrive(stage, h):
        for n, ws in need(stage, h).items():
            W[n] = W.get(n, []) + ws

    h = x
    saved = []
    for i in range(2):
        t = f"l{i}"
        if i == 1:
            arrive('l1_ffn1', h)
        g1 = row(P['ffn1_norm'][i]) + (token if i == 0 else 0.0)
        h, s1 = _ffn_fwd(h, g1, W, 'ffn1_w_gu', 'ffn1_w_down', i, f"{t}_ffn1")
        arrive(f"{t}_mix", h)
        if i == 0:
            h, s2 = _even_fwd(h, row(P['mix_norm'][i]), W, conv_a_w, P['conv_a_b'], P['conv_a_ln_g'],
                              P['conv_a_ln_b'], sinks, f"{t}_even")
        else:
            h, s2 = _odd_fwd(h, row(P['mix_norm'][i]), W, sc_w, f"{t}_odd")
        h, s3 = _xa_block_fwd(h, mem, row(P['xa_norm'][i]), row(P['xa_mem_norm'][i]), W, i, f"{t}_xa")
        arrive(f"{t}_ffn2", h)
        h, s4 = _ffn_fwd(h, row(P['ffn2_norm'][i]), W, 'ffn2_w_gu', 'ffn2_w_down', i, f"{t}_ffn2")
        saved.append((s1, s2, s3, s4))

    loss, dh, d_final = _final_loss(h, row(P['final_norm']), tgt, name="final_loss")

    G = {}
    gp = {n: [None, None] for n in ('ffn1_norm', 'mix_norm', 'xa_norm', 'xa_mem_norm', 'ffn2_norm')}
    single = {}
    for i in (1, 0):
        t = f"l{i}"
        s1, s2, s3, s4 = saved[i]
        g4 = row(P['ffn2_norm'][i]) + (ready('l1', G) if i == 0 else 0.0)
        dh, gp['ffn2_norm'][i] = _ffn_bwd(dh, s4, g4, W, 'ffn2_w_gu', 'ffn2_w_down', i, G, f"{t}_ffn2")
        dh, gp['xa_norm'][i], gp['xa_mem_norm'][i] = _xa_block_bwd(
            dh, s3, mem, row(P['xa_norm'][i]), row(P['xa_mem_norm'][i]), W, i, G, f"{t}_xa")
        if i == 0:
            dh, g2 = _even_bwd(dh, s2, row(P['mix_norm'][i]), W, conv_a_w, P['conv_a_ln_g'], P['conv_a_ln_b'], sinks,
                               G, f"{t}_even")
        else:
            dh, g2 = _odd_bwd(dh, s2, row(P['mix_norm'][i]), W, sc_w, G, f"{t}_odd")
        gp['mix_norm'][i] = g2.pop('mix')
        single.update(g2)
        g1 = row(P['ffn1_norm'][i]) + (ready('l0_rest', G) if i == 0 else 0.0)
        dh, gp['ffn1_norm'][i] = _ffn_bwd(dh, s1, g1, W, 'ffn1_w_gu', 'ffn1_w_down', i, G, f"{t}_ffn1")

    small = {n: jnp.concatenate(v, axis=0) for n, v in gp.items()}
    small['conv_a_b'] = single['conv_a_b']
    small['conv_a_ln_g'] = single['conv_a_ln_g']
    small['conv_a_ln_b'] = single['conv_a_ln_b']
    small['swa_sinks'] = single['swa_sinks'][None]
    small['final_norm'] = d_final[0]
    small['conv_a_w'] = single['conv_a_w']
    small['sc_conv_w'] = single['sc_conv_w']
    return loss[0, 0], dh, G, small


def _as2d(a):
    return a.reshape(-1, a.shape[-1])


def kernel(x, mem, ffn1_norm, ffn1_w_gu, ffn1_w_down, mix_norm, even_w_in, conv_a_w, conv_a_b, conv_a_ln_g, conv_a_ln_b, swa_sinks, even_w_out, odd_w_in, sc_conv_w, odd_w_out, xa_norm, xa_mem_norm, xa_wq, xa_wkv, xa_wo, ffn2_norm, ffn2_w_gu, ffn2_w_down, final_norm, loss_target, m_ffn1_norm, m_ffn1_w_gu, m_ffn1_w_down, m_mix_norm, m_even_w_in, m_conv_a_w, m_conv_a_b, m_conv_a_ln_g, m_conv_a_ln_b, m_swa_sinks, m_even_w_out, m_odd_w_in, m_sc_conv_w, m_odd_w_out, m_xa_norm, m_xa_mem_norm, m_xa_wq, m_xa_wkv, m_xa_wo, m_ffn2_norm, m_ffn2_w_gu, m_ffn2_w_down, m_final_norm, v_ffn1_norm, v_ffn1_w_gu, v_ffn1_w_down, v_mix_norm, v_even_w_in, v_conv_a_w, v_conv_a_b, v_conv_a_ln_g, v_conv_a_ln_b, v_swa_sinks, v_even_w_out, v_odd_w_in, v_sc_conv_w, v_odd_w_out, v_xa_norm, v_xa_mem_norm, v_xa_wq, v_xa_wkv, v_xa_wo, v_ffn2_norm, v_ffn2_w_gu, v_ffn2_w_down, v_final_norm):
    w = dict(zip(WEIGHT_NAMES, (ffn1_norm, ffn1_w_gu, ffn1_w_down, mix_norm, even_w_in, conv_a_w, conv_a_b, conv_a_ln_g, conv_a_ln_b, swa_sinks, even_w_out, odd_w_in, sc_conv_w, odd_w_out, xa_norm, xa_mem_norm, xa_wq, xa_wkv, xa_wo, ffn2_norm, ffn2_w_gu, ffn2_w_down, final_norm)))
    m = dict(zip(WEIGHT_NAMES, (m_ffn1_norm, m_ffn1_w_gu, m_ffn1_w_down, m_mix_norm, m_even_w_in, m_conv_a_w, m_conv_a_b, m_conv_a_ln_g, m_conv_a_ln_b, m_swa_sinks, m_even_w_out, m_odd_w_in, m_sc_conv_w, m_odd_w_out, m_xa_norm, m_xa_mem_norm, m_xa_wq, m_xa_wkv, m_xa_wo, m_ffn2_norm, m_ffn2_w_gu, m_ffn2_w_down, m_final_norm)))
    v = dict(zip(WEIGHT_NAMES, (v_ffn1_norm, v_ffn1_w_gu, v_ffn1_w_down, v_mix_norm, v_even_w_in, v_conv_a_w, v_conv_a_b, v_conv_a_ln_g, v_conv_a_ln_b, v_swa_sinks, v_even_w_out, v_odd_w_in, v_sc_conv_w, v_odd_w_out, v_xa_norm, v_xa_mem_norm, v_xa_wq, v_xa_wkv, v_xa_wo, v_ffn2_norm, v_ffn2_w_gu, v_ffn2_w_down, v_final_norm)))
    small_shapes = {n: w[n].shape for n in SMALL_NAMES}
    cx, cy, cc = lax.axis_index("x"), lax.axis_index("y"), lax.axis_index("c")
    chip_idx = (2 * cx + cy).astype(jnp.int32).reshape(1)
    core_idx = cc.astype(jnp.int32).reshape(1)
    chip_core_idx = jnp.concatenate([chip_idx, core_idx])

    shards = {n: w[n] for n in COMM_NAMES if n != 'tiny'}
    shards['tiny'] = _tiny_pack(conv_a_w, sc_conv_w)
    first =[('ffn1_w_gu', 0), ('ffn1_w_down', 0), ('tiny', 0)]
    stages = {
        'l0_mix': [('even_w_in', 0), ('even_w_out', 0), ('xa_wq', 0), ('xa_wkv', 0), ('xa_wo', 0)],
        'l0_ffn2': [('ffn2_w_gu', 0), ('ffn2_w_down', 0)],
        'l1_ffn1': [('ffn1_w_gu', 1), ('ffn1_w_down', 1)],
        'l1_mix': [('odd_w_in', 0), ('odd_w_out', 0), ('xa_wq', 1), ('xa_wkv', 1), ('xa_wo', 1)],
        'l1_ffn2': [('ffn2_w_gu', 1), ('ffn2_w_down', 1)],
    }
    grad_stages = {'l1': stages['l1_ffn1'] + stages['l1_mix'] + stages['l1_ffn2'],
                   'l0_rest': stages['l0_mix'] + stages['l0_ffn2']}

    def place(items):
        return [_place(shards[n], l, LAYOUT[n], chip_idx, F32 if n == 'tiny' else BF16, name=f"place_{n}_{l}")
                for n, l in items]

    def item_meta(items):
        return [(LAYOUT[n], 1) + shards[n].shape[1:] for n, l in items]

    def natural(items, arrays):
        out = {}
        for (n, l), a in zip(items, arrays):
            if n == 'tiny':
                continue
            if n == 'even_w_in':
                out[n] = [a.transpose(0, 2, 1, 3).reshape(1, D_MODEL, -1)]
            else:
                out[n] = [a if LAYOUT[n] == 'col' else a.reshape(1, N_CHIPS * a.shape[2], a.shape[3])]
        return out

    first_full = _all_gather(place(first), item_meta(first))
    W = natural(first, first_full)
    ca, sc = _tiny_unpack(first_full[-1][0])
    conv_a_full = ca.transpose(1, 0, 2).reshape(CONV_A_WIDTH, CONV_A_CH)
    sc_full = sc.transpose(1, 0, 2).reshape(SC_WIDTH, SC_CH)
    gathers = {}
    token = first_full[0]
    for stage, items in stages.items():
        send, recv, in_flight, token = _gather_start(place(items), item_meta(items), token, stage)
        gathers[stage] = (send, recv, in_flight)

    def need(stage, h):
        send, recv, in_flight = gathers[stage]
        items = stages[stage]
        return natural(items, _gather_wait(send, recv, in_flight, item_meta(items), h, stage))

    def gathered_layout(G, item):
        n, l = item
        A, B = shards[n].shape[1:]
        g = G[item]
        if n == 'tiny':
            return g
        if n == 'even_w_in':
            return g.reshape(A, N_CHIPS, B).transpose(1, 0, 2)[None]
        return g.reshape(1, A, N_CHIPS * B) if LAYOUT[n] == 'col' else g.reshape(1, N_CHIPS, A, B)

    scatters = {}

    def ready(stage, G):
        items = grad_stages[stage]
        send, recv, gs1, lands, tok = _scatter_start([gathered_layout(G, it) for it in items], item_meta(items),
                                                     chip_idx, stage)
        scatters[stage] = (send, recv, gs1, lands)
        return tok[:1, :1]

    loss_part, grad_x, G, g_small = _local_step(x[0], mem[0], loss_target[0], W, need, token[:1, :1], ready,
                                                conv_a_full, sc_full, {n: w[n] for n in SMALL_NAMES})
    G[('tiny', 0)] = _tiny_pack(g_small['conv_a_w'].reshape(CONV_A_WIDTH, N_CHIPS, 128).transpose(1, 0, 2),
                                g_small['sc_conv_w'].reshape(SC_WIDTH, N_CHIPS, 256).transpose(1, 0, 2))[None]
    loss = lax.psum(loss_part, ("x", "y", "c"))

    gs = [gathered_layout(G, it) for it in first]
    meta0 = item_meta(first)
    got = _pair_exchange(gs, meta0)
    ps = [_pair_sum(g, r, m_, core_idx, name=f"rs_pair_sum_{n}") for (n, l), g, r, m_ in zip(first, gs, got, meta0)]
    got2 = _chip_exchange(ps, meta0)
    red = [_chip_sum(p, r, m_, chip_core_idx, name=f"rs_chip_sum_{n}")
           for (n, l), p, r, m_ in zip(first, ps, got2, meta0)]
    sent_items, sums = [], []
    for stage, (send, recv, gs1, lands) in scatters.items():
        items = grad_stages[stage]
        sent, landed = _scatter_wait(send, recv, gs1, lands, item_meta(items), grad_x, stage)
        sums += [_chip_sum_full(g, r, m_, chip_idx, name=f"rs_chip_sum_{n}_{l}")
                 for (n, l), g, r, m_ in zip(items, sent, landed, item_meta(items))]
        sent_items += items
    g_shards, sib_sums, small_parts = _final_exchange(red, meta0, sums, _pack_small(g_small))
    g0 = dict(zip(first, g_shards))
    g1 = {it: [a, b] for it, a, b in zip(sent_items, sums, sib_sums)}

    grads, deltas, new_m, new_v = {}, {}, {}, {}
    for n in BIG_NAMES:
        if n in ('conv_a_w', 'sc_conv_w'):
            continue
        gsrc = [[g0[(n, l)]] if (n, l) in g0 else g1[(n, l)] for l in range(w[n].shape[0])]
        grads[n], deltas[n], new_m[n], new_v[n] = _adamw_layers(w[n], m[n], v[n], gsrc, name=f"adamw_{n}")
    tiny_g = dict(zip(('conv_a_w', 'sc_conv_w'), _tiny_unpack(g0[('tiny', 0)])))
    for n in ('conv_a_w', 'sc_conv_w'):
        shp = w[n].shape
        d, nm, nv = _adamw(_as2d(w[n]), _as2d(tiny_g[n]), _as2d(m[n]), _as2d(v[n]), name=f"adamw_{n}")
        grads[n], deltas[n], new_m[n], new_v[n] = tiny_g[n], d.reshape(shp), nm.reshape(shp), nv.reshape(shp)
    gs, ds, ms, vs = _adamw_small(_pack_small({n: w[n] for n in SMALL_NAMES}), small_parts,
                                  _pack_small({n: m[n] for n in SMALL_NAMES}),
                                  _pack_small({n: v[n] for n in SMALL_NAMES}), name="adamw_small")
    for dst, packed in ((grads, gs), (deltas, ds), (new_m, ms), (new_v, vs)):
        dst.update(_unpack_small(packed, small_shapes))

    return (loss, grad_x[None], *[grads[n] for n in WEIGHT_NAMES], *[deltas[n] for n in WEIGHT_NAMES],
            *[new_m[n] for n in WEIGHT_NAMES], *[new_v[n] for n in WEIGHT_NAMES])
```

```python
import functools
import math

import jax
import jax.numpy as jnp
from jax import lax
from jax.experimental import pallas as pl
from jax.experimental.pallas import tpu as pltpu

F32 = jnp.float32
BF16 = jnp.bfloat16

D_MODEL = 1024
D_FF = 2816
CONV_A_CH = 512
CONV_A_WIDTH = 31
SWA_HEADS = 8
SWA_KV_HEADS = 2
SWA_GROUP = 4
HEAD_DIM = 64
WINDOW = 128
SC_CH = 1024
SC_WIDTH = 3
XA_HEADS = 4
XA_HEAD_DIM = 256
RMS_EPS = 1e-6
LN_EPS = 1e-5

ADAM_LR = 0.001
ADAM_B1 = 0.9
ADAM_B2 = 0.999
ADAM_EPS = 1e-08
ADAM_WD = 0.01
ADAM_STEP = 10

N_CHIPS = 4
N_DEV = 8
NEG_BIG = -1e30
VMEM_LIMIT = 56 * 1024 * 1024
MESH = pl.DeviceIdType.MESH

INPUT_NAMES = ['x', 'mem', 'ffn1_norm', 'ffn1_w_gu', 'ffn1_w_down', 'mix_norm', 'even_w_in', 'conv_a_w', 'conv_a_b',
               'conv_a_ln_g', 'conv_a_ln_b', 'swa_sinks', 'even_w_out', 'odd_w_in', 'sc_conv_w', 'odd_w_out', 'xa_norm',
               'xa_mem_norm', 'xa_wq', 'xa_wkv', 'xa_wo', 'ffn2_norm', 'ffn2_w_gu', 'ffn2_w_down', 'final_norm']
WEIGHT_NAMES = INPUT_NAMES[2:]
BIG = [('ffn1_w_gu', 'col'), ('ffn1_w_down', 'row'), ('even_w_in', 'col'), ('conv_a_w', 'col'), ('even_w_out', 'row'),
       ('odd_w_in', 'col'), ('sc_conv_w', 'col'), ('odd_w_out', 'row'), ('xa_wq', 'row'), ('xa_wkv', 'col'),
       ('xa_wo', 'row'), ('ffn2_w_gu', 'col'), ('ffn2_w_down', 'row')]
BIG_NAMES = [n for n, _ in BIG]
SMALL_NAMES = [n for n in WEIGHT_NAMES if n not in BIG_NAMES]


def _cparams(sem=None, vmem=VMEM_LIMIT):
    kw = dict(vmem_limit_bytes=vmem)
    if sem is not None:
        kw['dimension_semantics'] = sem
    return pltpu.CompilerParams(**kw)


def _div_tile(n, want, align=8):
    if n <= want:
        return n
    t = (want // align) * align
    while t >= align:
        if n % t == 0:
            return t
        t -= align
    return n


def _mm(a, b, *, name, ta=False, tb=False, out_dtype=BF16, tm=512, tn=512, tk=512, res=None, scale=1.0,
        b_layer=None, stack=None, n_map=None):
    n_map = n_map or (lambda j: j)
    if ta:
        K, M = a.shape
    else:
        M, K = a.shape
    if tb:
        N, K2 = b.shape[-2:]
    else:
        K2, N = b.shape[-2:]
    assert K == K2, (a.shape, b.shape, ta, tb)
    tm = _div_tile(M, tm, 128 if ta else 16)
    tn = _div_tile(N, tn, 128)
    tk = _div_tile(K, tk, 16 if ta else 128)
    nk = K // tk
    a_spec = pl.BlockSpec((tk, tm), lambda i, j, k: (k, i)) if ta else pl.BlockSpec((tm, tk), lambda i, j, k: (i, k))
    if b_layer is None:
        b_spec = pl.BlockSpec((tn, tk), lambda i, j, k: (j, k)) if tb else pl.BlockSpec((tk, tn), lambda i, j, k: (k, j))
    elif tb:
        b_spec = pl.BlockSpec((None, tn, tk), lambda i, j, k: (b_layer, j, k))
    else:
        b_spec = pl.BlockSpec((None, tk, tn), lambda i, j, k: (b_layer, k, j))
    o_spec = pl.BlockSpec((tm, tn), lambda i, j, k: (i, j))
    out_shape = jax.ShapeDtypeStruct((M, N), out_dtype)
    out_spec = o_spec
    aliases = {}
    extra_specs, extra_args = [], ()
    if stack is not None:
        n_layers, layer, buf = stack[:3]
        n_total = stack[3] if len(stack) > 3 else N
        out_shape = jax.ShapeDtypeStruct((n_layers, M, n_total), out_dtype)
        out_spec = pl.BlockSpec((None, tm, tn), lambda i, j, k: (layer, i, n_map(j)))
        if buf is not None:
            extra_specs, extra_args = [pl.BlockSpec(memory_space=pl.ANY)], (buf,)
            aliases = {2 + (res is not None): 0}
    dims = (((0 if ta else 1,), (1 if tb else 0,)), ((), ()))
    has_res = res is not None
    n_extra = len(extra_args)

    def body(*refs):
        if n_extra:
            refs = refs[:2 + has_res] + refs[2 + has_res + n_extra:]
        if has_res:
            a_ref, b_ref, r_ref, o_ref, acc_ref = refs
        else:
            a_ref, b_ref, o_ref, acc_ref = refs
        k = pl.program_id(2)
        p = lax.dot_general(a_ref[...].astype(BF16), b_ref[...].astype(BF16), dims, preferred_element_type=F32)

        @pl.when(k == 0)
        def _():
            acc_ref[...] = p

        @pl.when(k > 0)
        def _():
            acc_ref[...] += p

        @pl.when(k == nk - 1)
        def _():
            r = acc_ref[...] * scale
            if has_res:
                r = r_ref[...] + r
            o_ref[...] = r.astype(out_dtype)

    in_specs = [a_spec, b_spec] + ([o_spec] if has_res else []) + extra_specs
    args = (a, b) + ((res,) if has_res else ()) + extra_args
    return pl.pallas_call(
        body, name=name, grid=(M // tm, N // tn, nk), in_specs=in_specs, out_specs=out_spec,
        out_shape=out_shape, input_output_aliases=aliases,
        scratch_shapes=[pltpu.VMEM((tm, tn), F32)],
        compiler_params=_cparams(("parallel", "parallel", "arbitrary")),
    )(*args)


def _rms_fwd(x, g, *, name):
    S, D = x.shape
    ts = _div_tile(S, 512)

    def body(x_ref, g_ref, o_ref):
        xv = x_ref[...]
        r = lax.rsqrt(jnp.mean(xv * xv, axis=-1, keepdims=True) + RMS_EPS)
        o_ref[...] = (xv * r * g_ref[...]).astype(BF16)

    return pl.pallas_call(
        body, name=name, grid=(S // ts,),
        in_specs=[pl.BlockSpec((ts, D), lambda i: (i, 0)), pl.BlockSpec((1, D), lambda i: (0, 0))],
        out_specs=pl.BlockSpec((ts, D), lambda i: (i, 0)),
        out_shape=jax.ShapeDtypeStruct((S, D), BF16),
        compiler_params=_cparams(("parallel",)),
    )(x, g)


def _norm_mm(h, g, w, layer, *, name, tn):
    S, D = h.shape
    N = w.shape[-1]
    tm = _div_tile(S, 1024, 16)
    tn = _div_tile(N, tn, 128)

    def body(h_ref, g_ref, w_ref, u_ref, z_ref, u_s):
        @pl.when(pl.program_id(1) == 0)
        def _():
            xv = h_ref[...]
            r = lax.rsqrt(jnp.mean(xv * xv, axis=-1, keepdims=True) + RMS_EPS)
            u = (xv * r * g_ref[...]).astype(BF16)
            u_s[...] = u
            u_ref[...] = u

        z_ref[...] = jnp.dot(u_s[...], w_ref[...], preferred_element_type=F32).astype(BF16)

    row = pl.BlockSpec((tm, D), lambda i, j: (i, 0))
    return pl.pallas_call(
        body, name=name, grid=(S // tm, N // tn),
        in_specs=[row, pl.BlockSpec((1, D), lambda i, j: (0, 0)), pl.BlockSpec((None, D, tn), lambda i, j: (layer, 0, j))],
        out_specs=[row, pl.BlockSpec((tm, tn), lambda i, j: (i, j))],
        out_shape=[jax.ShapeDtypeStruct((S, D), BF16), jax.ShapeDtypeStruct((S, N), BF16)],
        scratch_shapes=[pltpu.VMEM((tm, D), BF16)],
        compiler_params=_cparams(("parallel", "arbitrary")),
    )(h, g, w)


def _mm_norm_bwd(dz, w, layer, h, g, dres, *, name, tk):
    S, K = dz.shape
    D = h.shape[1]
    tm = _div_tile(S, 512, 16)
    tk = _div_tile(K, tk, 128)
    nk = K // tk

    def body(dz_ref, w_ref, h_ref, g_ref, dr_ref, dx_ref, dg_ref, acc):
        i = pl.program_id(0)
        k = pl.program_id(1)
        p = lax.dot_general(dz_ref[...], w_ref[...], (((1,), (1,)), ((), ())), preferred_element_type=F32)

        @pl.when(k == 0)
        def _():
            acc[...] = p

        @pl.when(k > 0)
        def _():
            acc[...] += p

        @pl.when(k == nk - 1)
        def _():
            xv = h_ref[...]
            du = acc[...]
            r = lax.rsqrt(jnp.mean(xv * xv, axis=-1, keepdims=True) + RMS_EPS)
            xhat = xv * r
            part = jnp.sum(du * xhat, axis=0, keepdims=True)

            @pl.when(i == 0)
            def _():
                dg_ref[...] = part

            @pl.when(i > 0)
            def _():
                dg_ref[...] += part

            dxhat = du * g_ref[...]
            dx_ref[...] = dr_ref[...] + r * (dxhat - xhat * jnp.mean(dxhat * xhat, axis=-1, keepdims=True))

    row = pl.BlockSpec((tm, D), lambda i, k: (i, 0))
    vec = pl.BlockSpec((1, D), lambda i, k: (0, 0))
    return pl.pallas_call(
        body, name=name, grid=(S // tm, nk),
        in_specs=[pl.BlockSpec((tm, tk), lambda i, k: (i, k)), pl.BlockSpec((None, D, tk), lambda i, k: (layer, 0, k)),
                  row, vec, row],
        out_specs=[row, vec],
        out_shape=[jax.ShapeDtypeStruct((S, D), F32), jax.ShapeDtypeStruct((1, D), F32)],
        scratch_shapes=[pltpu.VMEM((tm, D), F32)],
        compiler_params=_cparams(("arbitrary", "arbitrary")),
    )(dz, w, h, g, dres)


def _rms_bwd(x, g, du, dres, *, name):
    S, D = x.shape
    ts = _div_tile(S, 512)
    has_res = dres is not None

    def body(*refs):
        if has_res:
            x_ref, g_ref, du_ref, dr_ref, dx_ref, dg_ref = refs
        else:
            x_ref, g_ref, du_ref, dg_ref = refs
        i = pl.program_id(0)
        xv = x_ref[...]
        duv = du_ref[...].astype(F32)
        r = lax.rsqrt(jnp.mean(xv * xv, axis=-1, keepdims=True) + RMS_EPS)
        xhat = xv * r
        part = jnp.sum(duv * xhat, axis=0, keepdims=True)

        @pl.when(i == 0)
        def _():
            dg_ref[...] = part

        @pl.when(i > 0)
        def _():
            dg_ref[...] += part

        if has_res:
            dxhat = duv * g_ref[...]
            dx = r * (dxhat - xhat * jnp.mean(dxhat * xhat, axis=-1, keepdims=True))
            dx_ref[...] = dr_ref[...] + dx

    row = pl.BlockSpec((ts, D), lambda i: (i, 0))
    vec = pl.BlockSpec((1, D), lambda i: (0, 0))
    if has_res:
        dx, dg = pl.pallas_call(
            body, name=name, grid=(S // ts,), in_specs=[row, vec, row, row], out_specs=[row, vec],
            out_shape=[jax.ShapeDtypeStruct((S, D), F32), jax.ShapeDtypeStruct((1, D), F32)],
            compiler_params=_cparams(("arbitrary",)),
        )(x, g, du, dres)
        return dx, dg
    dg = pl.pallas_call(
        body, name=name, grid=(S // ts,), in_specs=[row, vec, row], out_specs=vec,
        out_shape=jax.ShapeDtypeStruct((1, D), F32),
        compiler_params=_cparams(("arbitrary",)),
    )(x, g, du)
    return None, dg


def _final_loss(h, g, tgt, *, name):
    S, D = h.shape
    ts = _div_tile(S, 512)

    def body(h_ref, g_ref, t_ref, loss_ref, dh_ref, dg_ref):
        i = pl.program_id(0)
        xv = h_ref[...]
        gv = g_ref[...]
        r = lax.rsqrt(jnp.mean(xv * xv, axis=-1, keepdims=True) + RMS_EPS)
        xhat = xv * r
        err = xhat * gv - t_ref[...]
        lpart = 0.5 * jnp.sum(jnp.mean(err * err, axis=-1, keepdims=True), axis=0, keepdims=True)
        dy = err * (1.0 / D)
        gpart = jnp.sum(dy * xhat, axis=0, keepdims=True)

        @pl.when(i == 0)
        def _():
            loss_ref[...] = jnp.broadcast_to(lpart, loss_ref.shape)
            dg_ref[...] = gpart

        @pl.when(i > 0)
        def _():
            loss_ref[...] += jnp.broadcast_to(lpart, loss_ref.shape)
            dg_ref[...] += gpart

        dxhat = dy * gv
        dh_ref[...] = r * (dxhat - xhat * jnp.mean(dxhat * xhat, axis=-1, keepdims=True))

    row = pl.BlockSpec((ts, D), lambda i: (i, 0))
    vec = pl.BlockSpec((1, D), lambda i: (0, 0))
    return pl.pallas_call(
        body, name=name, grid=(S // ts,), in_specs=[row, vec, row],
        out_specs=[pl.BlockSpec((8, 128), lambda i: (0, 0)), row, vec],
        out_shape=[jax.ShapeDtypeStruct((8, 128), F32), jax.ShapeDtypeStruct((S, D), F32),
                   jax.ShapeDtypeStruct((1, D), F32)],
        compiler_params=_cparams(("arbitrary",)),
    )(h, g, tgt)


def _sigmoid(x):
    return 1.0 / (1.0 + jnp.exp(-x))


FFN_CHUNK = 1408
FFN_CHUNKS = D_FF // FFN_CHUNK
FFN_BWD_PIECE = 384
FFN_BWD_SLAB = 256


def _ffn_fwd_fused(h, g, w_gu, w_down, layer, *, name):
    S, D = h.shape
    tm = _div_tile(S, 512, 16)
    tf, nj = FFN_CHUNK, FFN_CHUNKS

    def body(h_ref, g_ref, wg_ref, wu_ref, wd_ref, h2_ref, u_ref, gate_ref, up_ref, a_ref, u_s, acc):
        j = pl.program_id(1)

        @pl.when(j == 0)
        def _():
            xv = h_ref[...]
            r = lax.rsqrt(jnp.mean(xv * xv, axis=-1, keepdims=True) + RMS_EPS)
            u = (xv * r * g_ref[...]).astype(BF16)
            u_s[...] = u
            u_ref[...] = u

        u = u_s[...]
        gate = jnp.dot(u, wg_ref[...], preferred_element_type=F32)
        up = jnp.dot(u, wu_ref[...], preferred_element_type=F32)
        gate_ref[...] = gate.astype(BF16)
        up_ref[...] = up.astype(BF16)
        a = (gate * _sigmoid(gate) * up).astype(BF16)
        a_ref[...] = a
        p = jnp.dot(a, wd_ref[...], preferred_element_type=F32)

        @pl.when(j == 0)
        def _():
            acc[...] = p

        @pl.when(j > 0)
        def _():
            acc[...] += p

        @pl.when(j == nj - 1)
        def _():
            h2_ref[...] = h_ref[...] + 0.5 * acc[...]

    row = pl.BlockSpec((tm, D), lambda i, j: (i, 0))
    chunk = pl.BlockSpec((tm, tf), lambda i, j: (i, j))
    hidden = jax.ShapeDtypeStruct((S, D_FF), BF16)
    return pl.pallas_call(
        body, name=name, grid=(S // tm, nj),
        in_specs=[row, pl.BlockSpec((1, D), lambda i, j: (0, 0)),
                  pl.BlockSpec((None, D, tf), lambda i, j: (layer, 0, j)),
                  pl.BlockSpec((None, D, tf), lambda i, j: (layer, 0, nj + j)),
                  pl.BlockSpec((None, tf, D), lambda i, j: (layer, j, 0))],
        out_specs=[row, row, chunk, chunk, chunk],
        out_shape=[jax.ShapeDtypeStruct((S, D), F32), jax.ShapeDtypeStruct((S, D), BF16), hidden, hidden, hidden],
        scratch_shapes=[pltpu.VMEM((tm, D), BF16), pltpu.VMEM((tm, D), F32)],
        compiler_params=_cparams(("parallel", "arbitrary")),
    )(h, g, w_gu, w_gu, w_down)


def _ffn_bwd_fused(dh, h, g, gate, up, w_gu, w_down, layer, *, name):
    S, D = h.shape
    tm = _div_tile(S, 512, FFN_BWD_SLAB)
    tf = FFN_CHUNK
    nj = D_FF // tf
    slab = min(FFN_BWD_SLAB, tm)
    nt = (((1,), (1,)), ((), ()))
    pieces = [(c0, min(FFN_BWD_PIECE, tf - c0)) for c0 in range(0, tf, FFN_BWD_PIECE)]

    def body(dh_ref, h_ref, g_ref, gate_ref, up_ref, wg_ref, wu_ref, wd_ref, dx_ref, dg_ref, dgate_ref, dup_ref,
             dy_s, acc):
        i = pl.program_id(0)
        j = pl.program_id(1)

        @pl.when(j == 0)
        def _():
            for r0 in range(0, tm, slab):
                rows = pl.ds(r0, slab)
                dy_s[rows, :] = (0.5 * dh_ref[rows, :]).astype(BF16)

        p = None
        for c0, cw in pieces:
            cols = pl.ds(c0, cw)
            da = lax.dot_general(dy_s[...], wd_ref[cols, :], nt, preferred_element_type=F32)
            gt = gate_ref[:, cols].astype(F32)
            sg = _sigmoid(gt)
            dgate = (da * up_ref[:, cols].astype(F32) * sg * (1.0 + gt * (1.0 - sg))).astype(BF16)
            dup = (da * gt * sg).astype(BF16)
            dgate_ref[:, cols] = dgate
            dup_ref[:, cols] = dup
            q = (lax.dot_general(dgate, wg_ref[:, cols], nt, preferred_element_type=F32)
                 + lax.dot_general(dup, wu_ref[:, cols], nt, preferred_element_type=F32))
            p = q if p is None else p + q

        @pl.when(j == 0)
        def _():
            acc[...] = p

        @pl.when(j > 0)
        def _():
            acc[...] += p

        @pl.when(j == nj - 1)
        def _():
            part = jnp.zeros((1, D), F32)
            for r0 in range(0, tm, slab):
                rows = pl.ds(r0, slab)
                xv = h_ref[rows, :]
                du = acc[rows, :]
                r = lax.rsqrt(jnp.mean(xv * xv, axis=-1, keepdims=True) + RMS_EPS)
                xhat = xv * r
                part = part + jnp.sum(du * xhat, axis=0, keepdims=True)
                dxhat = du * g_ref[...]
                dx_ref[rows, :] = dh_ref[rows, :] + r * (
                    dxhat - xhat * jnp.mean(dxhat * xhat, axis=-1, keepdims=True))

            @pl.when(i == 0)
            def _():
                dg_ref[...] = part

            @pl.when(i > 0)
            def _():
                dg_ref[...] += part

    row = pl.BlockSpec((tm, D), lambda i, j: (i, 0))
    vec = pl.BlockSpec((1, D), lambda i, j: (0, 0))
    chunk = pl.BlockSpec((tm, tf), lambda i, j: (i, j))
    hidden = jax.ShapeDtypeStruct((S, D_FF), BF16)
    return pl.pallas_call(
        body, name=name, grid=(S // tm, nj),
        in_specs=[row, row, vec, chunk, chunk,
                  pl.BlockSpec((None, D, tf), lambda i, j: (layer, 0, j)),
                  pl.BlockSpec((None, D, tf), lambda i, j: (layer, 0, nj + j)),
                  pl.BlockSpec((None, tf, D), lambda i, j: (layer, j, 0))],
        out_specs=[row, vec, chunk, chunk],
        out_shape=[jax.ShapeDtypeStruct((S, D), F32), jax.ShapeDtypeStruct((1, D), F32), hidden, hidden],
        scratch_shapes=[pltpu.VMEM((tm, D), BF16), pltpu.VMEM((tm, D), F32)],
        compiler_params=_cparams(("arbitrary", "arbitrary")),
    )(dh, h, g, gate, up, w_gu, w_gu, w_down)


CONV_HALO = 32
CONV_SUB_ROWS = 128


def _conv_a_fwd(z, w, bias, ln_g, ln_b, *, name):
    S = z.shape[0]
    C = CONV_A_CH
    ts = _div_tile(S, 256, 32)

    def body(val_ref, gate_ref, w_ref, b_ref, g_ref, lb_ref, c_ref, act_ref, win):
        i = pl.program_id(0)

        @pl.when(i == 0)
        def _():
            win[pl.ds(0, CONV_HALO), :] = jnp.zeros((CONV_HALO, C), F32)

        @pl.when(i > 0)
        def _():
            win[pl.ds(0, CONV_HALO), :] = win[pl.ds(ts, CONV_HALO), :]

        a = val_ref[...].astype(F32) * _sigmoid(gate_ref[...].astype(F32))
        win[pl.ds(CONV_HALO, ts), :] = a
        rs = min(CONV_SUB_ROWS, ts)
        for cb in range(C // 128):
            lanes = pl.ds(128 * cb, 128)
            for rt in range(ts // rs):
                sub = jnp.broadcast_to(b_ref[:, lanes], (rs, 128))
                for k in range(CONV_A_WIDTH):
                    sub = sub + w_ref[pl.ds(k, 1), lanes] * win[
                        pl.ds(CONV_HALO - (CONV_A_WIDTH - 1) + k + rs * rt, rs), lanes]
                c_ref[pl.ds(rs * rt, rs), lanes] = sub
        acc = c_ref[...]
        mu = jnp.mean(acc, axis=-1, keepdims=True)
        xc = acc - mu
        var = jnp.mean(xc * xc, axis=-1, keepdims=True)
        ln = xc * lax.rsqrt(var + LN_EPS) * g_ref[...] + lb_ref[...]
        act_ref[...] = (ln * _sigmoid(ln)).astype(BF16)

    row = lambda col: pl.BlockSpec((ts, C), lambda i, col=col: (i, col))
    vec = pl.BlockSpec((1, C), lambda i: (0, 0))
    return pl.pallas_call(
        body, name=name, grid=(S // ts,),
        in_specs=[row(0), row(1), pl.BlockSpec((32, C), lambda i: (0, 0)), vec, vec, vec],
        out_specs=[row(0), row(0)],
        out_shape=[jax.ShapeDtypeStruct((S, C), F32), jax.ShapeDtypeStruct((S, C), BF16)],
        scratch_shapes=[pltpu.VMEM((ts + CONV_HALO, C), F32)],
        compiler_params=_cparams(("arbitrary",)),
    )(z, z, w, bias, ln_g, ln_b)


def _conv_a_bwd(z, c, dcat, w, ln_g, ln_b, *, name):
    S = z.shape[0]
    C = CONV_A_CH
    ts = _div_tile(S, 256, 32)
    n = S // ts

    def body(val_ref, gate_ref, c_ref, da_ref, w_ref, g_ref, lb_ref, dz_ref, small_ref, win, a_s, da_s, dw8):
        i = pl.program_id(0)

        @pl.when(i == 0)
        def _():
            win[pl.ds(ts, CONV_HALO), :] = jnp.zeros((CONV_HALO, C), F32)
            small_ref[...] = jnp.zeros(small_ref.shape, F32)
            dw8[...] = jnp.zeros(dw8.shape, F32)

        @pl.when(i > 0)
        def _():
            win[pl.ds(ts, CONV_HALO), :] = win[pl.ds(0, CONV_HALO), :]

        cv = c_ref[...]
        gv = g_ref[...]
        mu = jnp.mean(cv, axis=-1, keepdims=True)
        xc = cv - mu
        var = jnp.mean(xc * xc, axis=-1, keepdims=True)
        rstd = lax.rsqrt(var + LN_EPS)
        xhat = xc * rstd
        ln = xhat * gv + lb_ref[...]
        sg = _sigmoid(ln)
        dln = da_ref[...].astype(F32) * (sg * (1.0 + ln * (1.0 - sg)))
        small_ref[pl.ds(33, 1), :] += jnp.sum(dln * xhat, axis=0, keepdims=True)
        small_ref[pl.ds(34, 1), :] += jnp.sum(dln, axis=0, keepdims=True)
        dxhat = dln * gv
        dc = rstd * (dxhat - jnp.mean(dxhat, axis=-1, keepdims=True)
                     - xhat * jnp.mean(dxhat * xhat, axis=-1, keepdims=True))
        small_ref[pl.ds(32, 1), :] += jnp.sum(dc, axis=0, keepdims=True)
        win[pl.ds(0, ts), :] = dc

        val = val_ref[...].astype(F32)
        sgg = _sigmoid(gate_ref[...].astype(F32))
        a_s[...] = val * sgg
        rs = min(CONV_SUB_ROWS, ts)
        for cb in range(C // 128):
            lanes = pl.ds(128 * cb, 128)
            for rt in range(ts // rs):
                a_sub = a_s[pl.ds(rs * rt, rs), lanes]
                da = jnp.zeros((rs, 128), F32)
                for k in range(CONV_A_WIDTH):
                    sh = win[pl.ds(CONV_A_WIDTH - 1 - k + rs * rt, rs), lanes]
                    da = da + w_ref[pl.ds(k, 1), lanes] * sh
                    prod = a_sub * sh
                    part = prod[0:8]
                    for r in range(1, rs // 8):
                        part = part + prod[8 * r:8 * r + 8]
                    dw8[pl.ds(8 * k, 8), lanes] += part
                da_s[pl.ds(rs * rt, rs), lanes] = da
        da = da_s[...]
        dz_ref[:, pl.ds(0, C)] = (da * sgg).astype(BF16)
        dz_ref[:, pl.ds(C, C)] = (da * val * sgg * (1.0 - sgg)).astype(BF16)

        @pl.when(i == n - 1)
        def _():
            for k in range(CONV_A_WIDTH):
                small_ref[pl.ds(k, 1), :] = jnp.sum(dw8[pl.ds(8 * k, 8), :], axis=0, keepdims=True)

    row = lambda col: pl.BlockSpec((ts, C), lambda i, col=col: (n - 1 - i, col))
    vec = pl.BlockSpec((1, C), lambda i: (0, 0))
    return pl.pallas_call(
        body, name=name, grid=(n,),
        in_specs=[row(0), row(1), row(0), row(0), pl.BlockSpec((32, C), lambda i: (0, 0)), vec, vec],
        out_specs=[pl.BlockSpec((ts, 2 * C), lambda i: (n - 1 - i, 0)), pl.BlockSpec((40, C), lambda i: (0, 0))],
        out_shape=[jax.ShapeDtypeStruct((S, 2 * C), BF16), jax.ShapeDtypeStruct((40, C), F32)],
        scratch_shapes=[pltpu.VMEM((ts + CONV_HALO, C), F32), pltpu.VMEM((ts, C), F32), pltpu.VMEM((ts, C), F32),
                        pltpu.VMEM((8 * 32, C), F32)],
        compiler_params=_cparams(("arbitrary",)),
    )(z, z, c, dcat, w, ln_g, ln_b)


SC_HALO = 8


def _sconv_fwd(z, w, *, name):
    S = z.shape[0]
    C = SC_CH
    ts = _div_tile(S, 256, 16)

    def body(gb_ref, gc_ref, v_ref, w_ref, y_ref, cc_ref, win):
        i = pl.program_id(0)

        @pl.when(i == 0)
        def _():
            win[pl.ds(0, SC_HALO), :] = jnp.zeros((SC_HALO, C), F32)

        @pl.when(i > 0)
        def _():
            win[pl.ds(0, SC_HALO), :] = win[pl.ds(ts, SC_HALO), :]

        win[pl.ds(SC_HALO, ts), :] = gc_ref[...].astype(F32) * v_ref[...].astype(F32)
        acc = jnp.zeros((ts, C), F32)
        for k in range(SC_WIDTH):
            acc = acc + w_ref[pl.ds(k, 1), :] * win[pl.ds(SC_HALO - (SC_WIDTH - 1) + k, ts), :]
        cc_ref[...] = acc.astype(BF16)
        y_ref[...] = (gb_ref[...].astype(F32) * acc).astype(BF16)

    row = lambda col: pl.BlockSpec((ts, C), lambda i, col=col: (i, col))
    return pl.pallas_call(
        body, name=name, grid=(S // ts,),
        in_specs=[row(0), row(1), row(2), pl.BlockSpec((8, C), lambda i: (0, 0))],
        out_specs=[row(0), row(0)],
        out_shape=[jax.ShapeDtypeStruct((S, C), BF16), jax.ShapeDtypeStruct((S, C), BF16)],
        scratch_shapes=[pltpu.VMEM((ts + SC_HALO, C), F32)],
        compiler_params=_cparams(("arbitrary",)),
    )(z, z, z, w)


def _sconv_bwd(z, cc, dy, w, *, name):
    S = z.shape[0]
    C = SC_CH
    ts = _div_tile(S, 256, 16)
    n = S // ts

    def body(gb_ref, gc_ref, v_ref, cc_ref, dy_ref, w_ref, dz_ref, dw_ref, win):
        i = pl.program_id(0)

        @pl.when(i == 0)
        def _():
            win[pl.ds(ts, SC_HALO), :] = jnp.zeros((SC_HALO, C), F32)
            dw_ref[...] = jnp.zeros(dw_ref.shape, F32)

        @pl.when(i > 0)
        def _():
            win[pl.ds(ts, SC_HALO), :] = win[pl.ds(0, SC_HALO), :]

        dyv = dy_ref[...].astype(F32)
        gb = gb_ref[...].astype(F32)
        gc = gc_ref[...].astype(F32)
        val = v_ref[...].astype(F32)
        dz_ref[:, pl.ds(0, C)] = (dyv * cc_ref[...].astype(F32)).astype(BF16)
        win[pl.ds(0, ts), :] = dyv * gb
        cv = gc * val
        dcv = jnp.zeros((ts, C), F32)
        for k in range(SC_WIDTH):
            sh = win[pl.ds(SC_WIDTH - 1 - k, ts), :]
            dcv = dcv + w_ref[pl.ds(k, 1), :] * sh
            dw_ref[pl.ds(k, 1), :] += jnp.sum(cv * sh, axis=0, keepdims=True)
        dz_ref[:, pl.ds(C, C)] = (dcv * val).astype(BF16)
        dz_ref[:, pl.ds(2 * C, C)] = (dcv * gc).astype(BF16)

    row = lambda col: pl.BlockSpec((ts, C), lambda i, col=col: (n - 1 - i, col))
    return pl.pallas_call(
        body, name=name, grid=(n,),
        in_specs=[row(0), row(1), row(2), row(0), row(0), pl.BlockSpec((8, C), lambda i: (0, 0))],
        out_specs=[pl.BlockSpec((ts, 3 * C), lambda i: (n - 1 - i, 0)), pl.BlockSpec((8, C), lambda i: (0, 0))],
        out_shape=[jax.ShapeDtypeStruct((S, 3 * C), BF16), jax.ShapeDtypeStruct((8, C), F32)],
        scratch_shapes=[pltpu.VMEM((ts + SC_HALO, C), F32)],
        compiler_params=_cparams(("arbitrary",)),
    )(z, z, z, cc, dy, w)


SWA_Q_COL = 2
SWA_SLOPES = [2.0 ** (-8.0 * (h + 1) / SWA_HEADS) for h in range(SWA_HEADS)]
SWA_SCALE = HEAD_DIM ** -0.5


SWA_GROUP_ROWS = SWA_GROUP * WINDOW


def _swa_masks():
    shape = (SWA_GROUP_ROWS, 2 * WINDOW)
    ii = lax.broadcasted_iota(jnp.int32, shape, 0)
    jj = lax.broadcasted_iota(jnp.int32, shape, 1)
    dist = (ii & (WINDOW - 1)) + WINDOW - jj
    valid = (dist >= 0) & (dist < WINDOW)
    grp = lax.broadcasted_iota(jnp.int32, (SWA_GROUP_ROWS, 1), 0) // WINDOW
    return dist.astype(F32), valid, jj, grp


def _by_group(grp, vals):
    out = jnp.full(grp.shape, vals[SWA_GROUP - 1], F32)
    for g in range(SWA_GROUP - 2, -1, -1):
        out = jnp.where(grp == g, vals[g], out)
    return out


def _stack_heads(ref, rows, kv):
    return jnp.concatenate([ref[rows, pl.ds(HEAD_DIM * (kv * SWA_GROUP + g), HEAD_DIM)] for g in range(SWA_GROUP)],
                           axis=0)


def _swa_probs(qg, kk, sink, slope, distf, valid):
    s = lax.dot_general(qg, kk, (((1,), (1,)), ((), ())), preferred_element_type=F32) * SWA_SCALE
    s = s - slope * distf
    s = jnp.where(valid, s, NEG_BIG)
    m = jnp.maximum(jnp.max(s, axis=-1, keepdims=True), sink)
    p = jnp.exp(s - m)
    l = jnp.sum(p, axis=-1, keepdims=True) + jnp.exp(sink - m)
    return p, m, l


def _swa_fwd(z, kpad, vpad, sinks, *, name):
    S = z.shape[0]
    tq = _div_tile(S, 256, 128)
    nblk = tq // WINDOW
    W = WINDOW

    def body(sink_ref, q_ref, k_ref, v_ref, o_ref):
        i = pl.program_id(0)
        distf, valid0, jj, grp = _swa_masks()
        for kv in range(SWA_KV_HEADS):
            heads = range(kv * SWA_GROUP, (kv + 1) * SWA_GROUP)
            sink = _by_group(grp, [sink_ref[h] for h in heads])
            slope = _by_group(grp, [SWA_SLOPES[h] for h in heads])
            for b in range(nblk):
                nb = i * nblk + b
                start = pl.multiple_of(nb * W, W)
                rows = pl.ds(W * b, W)
                valid = valid0 & ((jj >= W) | (nb > 0))
                kk = k_ref[pl.ds(start, 2 * W), pl.ds(HEAD_DIM * kv, HEAD_DIM)]
                vv = v_ref[pl.ds(start, 2 * W), pl.ds(HEAD_DIM * kv, HEAD_DIM)]
                p, m, l = _swa_probs(_stack_heads(q_ref, rows, kv), kk, sink, slope, distf, valid)
                o = (jnp.dot(p.astype(BF16), vv, preferred_element_type=F32) / l).astype(BF16)
                for g, h in enumerate(heads):
                    o_ref[rows, pl.ds(HEAD_DIM * h, HEAD_DIM)] = o[W * g:W * (g + 1)]

    full = pl.BlockSpec((S + W, 2 * HEAD_DIM), lambda i: (0, 0))
    return pl.pallas_call(
        body, name=name, grid=(S // tq,),
        in_specs=[pl.BlockSpec(memory_space=pltpu.SMEM), pl.BlockSpec((tq, 512), lambda i: (i, SWA_Q_COL)), full, full],
        out_specs=pl.BlockSpec((tq, 512), lambda i: (i, 0)),
        out_shape=jax.ShapeDtypeStruct((S, 512), BF16),
        compiler_params=_cparams(("parallel",)),
    )(sinks, z, kpad, vpad)


def _swa_bwd(z, kpad, vpad, sinks, dcat, *, name):
    S = z.shape[0]
    tq = _div_tile(S, 256, 128)
    nblk = tq // WINDOW
    W = WINDOW

    def body(sink_ref, q_ref, k_ref, v_ref, do_ref, dq_ref, dk_ref, dv_ref, ds_ref):
        i = pl.program_id(0)

        @pl.when(i == 0)
        def _():
            dk_ref[...] = jnp.zeros(dk_ref.shape, F32)
            dv_ref[...] = jnp.zeros(dv_ref.shape, F32)
            ds_ref[...] = jnp.zeros(ds_ref.shape, F32)

        distf, valid0, jj, grp = _swa_masks()
        tn = (((0,), (0,)), ((), ()))
        for kv in range(SWA_KV_HEADS):
            heads = range(kv * SWA_GROUP, (kv + 1) * SWA_GROUP)
            sink = _by_group(grp, [sink_ref[h] for h in heads])
            slope = _by_group(grp, [SWA_SLOPES[h] for h in heads])
            for b in range(nblk):
                nb = i * nblk + b
                start = pl.multiple_of(nb * W, W)
                rows = pl.ds(W * b, W)
                valid = valid0 & ((jj >= W) | (nb > 0))
                kk = k_ref[pl.ds(start, 2 * W), pl.ds(HEAD_DIM * kv, HEAD_DIM)]
                vv = v_ref[pl.ds(start, 2 * W), pl.ds(HEAD_DIM * kv, HEAD_DIM)]
                qg = _stack_heads(q_ref, rows, kv)
                dog = _stack_heads(do_ref, rows, kv)
                p, m, l = _swa_probs(qg, kk, sink, slope, distf, valid)
                inv_l = 1.0 / l
                pn = p * inv_l
                dp = lax.dot_general(dog, vv, (((1,), (1,)), ((), ())), preferred_element_type=F32)
                delta = jnp.sum(pn * dp, axis=-1, keepdims=True)
                dsc = (pn * (dp - delta)).astype(BF16)
                dsink = jnp.exp(sink - m) * inv_l * delta
                dq = (jnp.dot(dsc, kk, preferred_element_type=F32) * SWA_SCALE).astype(BF16)
                for g, h in enumerate(heads):
                    ds_ref[pl.ds(h, 1), :] += jnp.broadcast_to(
                        -jnp.sum(dsink[W * g:W * (g + 1)], axis=0, keepdims=True), (1, 128))
                    dq_ref[rows, pl.ds(HEAD_DIM * h, HEAD_DIM)] = dq[W * g:W * (g + 1)]
                dk_ref[pl.ds(start, 2 * W), pl.ds(HEAD_DIM * kv, HEAD_DIM)] += lax.dot_general(
                    dsc, qg, tn, preferred_element_type=F32) * SWA_SCALE
                dv_ref[pl.ds(start, 2 * W), pl.ds(HEAD_DIM * kv, HEAD_DIM)] += lax.dot_general(
                    pn.astype(BF16), dog, tn, preferred_element_type=F32)

    full = pl.BlockSpec((S + W, 2 * HEAD_DIM), lambda i: (0, 0))
    return pl.pallas_call(
        body, name=name, grid=(S // tq,),
        in_specs=[pl.BlockSpec(memory_space=pltpu.SMEM), pl.BlockSpec((tq, 512), lambda i: (i, SWA_Q_COL)), full, full,
                  pl.BlockSpec((tq, 512), lambda i: (i, 1))],
        out_specs=[pl.BlockSpec((tq, 512), lambda i: (i, 0)), full, full, pl.BlockSpec((8, 128), lambda i: (0, 0))],
        out_shape=[jax.ShapeDtypeStruct((S, 512), BF16), jax.ShapeDtypeStruct((S + W, 2 * HEAD_DIM), F32),
                   jax.ShapeDtypeStruct((S + W, 2 * HEAD_DIM), F32), jax.ShapeDtypeStruct((8, 128), F32)],
        compiler_params=_cparams(("arbitrary",)),
    )(sinks, z, kpad, vpad, dcat)


XA_SCALE = XA_HEAD_DIM ** -0.5


def _xa_probs(qh, kh):
    s = lax.dot_general(qh, kh, (((1,), (1,)), ((), ())), preferred_element_type=F32) * XA_SCALE
    m = jnp.max(s, axis=-1, keepdims=True)
    p = jnp.exp(s - m)
    return p, jnp.sum(p, axis=-1, keepdims=True)


def _xa_fwd(q, kv, *, name):
    S, D = q.shape
    M = kv.shape[0]
    ts = _div_tile(S, 512, 16)
    HD = XA_HEAD_DIM

    def body(q_ref, k_ref, v_ref, o_ref):
        for h in range(XA_HEADS):
            qh = q_ref[:, pl.ds(HD * h, HD)]
            p, l = _xa_probs(qh, k_ref[:, pl.ds(HD * h, HD)])
            o = jnp.dot(p.astype(BF16), v_ref[:, pl.ds(HD * h, HD)], preferred_element_type=F32) / l
            o_ref[:, pl.ds(HD * h, HD)] = o.astype(BF16)

    return pl.pallas_call(
        body, name=name, grid=(S // ts,),
        in_specs=[pl.BlockSpec((ts, D), lambda i: (i, 0)), pl.BlockSpec((M, D), lambda i: (0, 0)),
                  pl.BlockSpec((M, D), lambda i: (0, 1))],
        out_specs=pl.BlockSpec((ts, D), lambda i: (i, 0)),
        out_shape=jax.ShapeDtypeStruct((S, D), BF16),
        compiler_params=_cparams(("parallel",)),
    )(q, kv, kv)


def _xa_bwd(q, kv, do, *, name):
    S, D = q.shape
    M = kv.shape[0]
    ts = _div_tile(S, 512, 16)
    HD = XA_HEAD_DIM

    def body(q_ref, k_ref, v_ref, do_ref, dq_ref, dkv_ref):
        i = pl.program_id(0)

        @pl.when(i == 0)
        def _():
            dkv_ref[...] = jnp.zeros(dkv_ref.shape, F32)

        for h in range(XA_HEADS):
            qh = q_ref[:, pl.ds(HD * h, HD)]
            kh = k_ref[:, pl.ds(HD * h, HD)]
            vh = v_ref[:, pl.ds(HD * h, HD)]
            doh = do_ref[:, pl.ds(HD * h, HD)]
            p, l = _xa_probs(qh, kh)
            pn = p * (1.0 / l)
            dp = lax.dot_general(doh, vh, (((1,), (1,)), ((), ())), preferred_element_type=F32)
            delta = jnp.sum(pn * dp, axis=-1, keepdims=True)
            dsc = (pn * (dp - delta)).astype(BF16)
            dq_ref[:, pl.ds(HD * h, HD)] = (jnp.dot(dsc, kh, preferred_element_type=F32) * XA_SCALE).astype(BF16)
            dkv_ref[:, pl.ds(HD * h, HD)] += lax.dot_general(
                dsc, qh, (((0,), (0,)), ((), ())), preferred_element_type=F32) * XA_SCALE
            dkv_ref[:, pl.ds(D + HD * h, HD)] += lax.dot_general(
                pn.astype(BF16), doh, (((0,), (0,)), ((), ())), preferred_element_type=F32)

    row = pl.BlockSpec((ts, D), lambda i: (i, 0))
    return pl.pallas_call(
        body, name=name, grid=(S // ts,),
        in_specs=[row, pl.BlockSpec((M, D), lambda i: (0, 0)), pl.BlockSpec((M, D), lambda i: (0, 1)), row],
        out_specs=[row, pl.BlockSpec((M, 2 * D), lambda i: (0, 0))],
        out_shape=[jax.ShapeDtypeStruct((S, D), BF16), jax.ShapeDtypeStruct((M, 2 * D), F32)],
        compiler_params=_cparams(("arbitrary",)),
    )(q, kv, kv, do)


def _adam_math(w, g, m, v):
    m = ADAM_B1 * m + (1.0 - ADAM_B1) * g
    v = ADAM_B2 * v + (1.0 - ADAM_B2) * (g * g)
    m_hat = m / (1.0 - ADAM_B1 ** ADAM_STEP)
    v_hat = v / (1.0 - ADAM_B2 ** ADAM_STEP)
    delta = -ADAM_LR * (m_hat / (jnp.sqrt(v_hat) + ADAM_EPS) + ADAM_WD * w)
    return delta, m, v


def _adamw_layers(w, m, v, gsrc, *, name):
    L, A, B = w.shape
    tr = _div_tile(A, max(8, (256 * 1024) // B // 8 * 8))
    nt = A // tr
    flat = [a for srcs in gsrc for a in srcs]
    owner = [l for l, srcs in enumerate(gsrc) for _ in srcs]
    ng = len(flat)

    def body(*refs):
        w_ref, m_ref, v_ref = refs[:3]
        g_refs = refs[3:3 + ng]
        g_ref, d_ref, nm_ref, nv_ref = refs[3 + ng:]
        layer = pl.program_id(0)
        g = None
        for l in range(L):
            gl = None
            for a_ref, o in zip(g_refs, owner):
                if o == l:
                    gl = a_ref[...] if gl is None else gl + a_ref[...]
            g = gl if g is None else jnp.where(layer == l, gl, g)
        d, nm, nv = _adam_math(w_ref[...], g, m_ref[...], v_ref[...])
        g_ref[...] = g
        d_ref[...] = d
        nm_ref[...] = nm
        nv_ref[...] = nv

    def src_spec(o):
        return pl.BlockSpec((None, tr, B),
                            lambda l, i: (0, jnp.where(l == o, i, jnp.where(l > o, nt - 1, 0)), 0))

    spec = pl.BlockSpec((None, tr, B), lambda l, i: (l, i, 0))
    sds = jax.ShapeDtypeStruct((L, A, B), F32)
    return pl.pallas_call(
        body, name=name, grid=(L, nt), in_specs=[spec] * 3 + [src_spec(o) for o in owner], out_specs=[spec] * 4,
        out_shape=[sds] * 4, compiler_params=_cparams(("arbitrary", "arbitrary")),
    )(w, m, v, *flat)


def _adamw_small(w, gparts, m, v, *, name):
    R, C = w.shape

    def body(w_ref, gp_ref, m_ref, v_ref, g_ref, d_ref, nm_ref, nv_ref):
        g = gp_ref[0]
        for k in range(1, N_DEV):
            g = g + gp_ref[k]
        d, nm, nv = _adam_math(w_ref[...], g, m_ref[...], v_ref[...])
        g_ref[...] = g
        d_ref[...] = d
        nm_ref[...] = nm
        nv_ref[...] = nv

    sds = jax.ShapeDtypeStruct((R, C), F32)
    return pl.pallas_call(body, name=name, out_shape=[sds] * 4, compiler_params=_cparams())(w, gparts, m, v)


ANY = pl.BlockSpec(memory_space=pl.ANY)


def _mesh_pos():
    return lax.axis_index("x"), lax.axis_index("y"), lax.axis_index("c")


def _other_chips(x, y):
    return [(1 - x, y), (x, 1 - y), (1 - x, 1 - y)]


LAYOUT = {'ffn1_w_gu': 'col', 'ffn1_w_down': 'stk', 'even_w_in': 'stk', 'even_w_out': 'stk', 'odd_w_in': 'col',
          'odd_w_out': 'stk', 'xa_wq': 'stk', 'xa_wkv': 'col', 'xa_wo': 'stk', 'ffn2_w_gu': 'col',
          'ffn2_w_down': 'stk', 'tiny': 'stk'}
COMM_NAMES = list(LAYOUT)
TINY_ROWS = 48


def _gathered_piece(ref, kind, L, A, h):
    if L == 2:
        return ref.at[h]
    rows = pl.ds(pl.multiple_of(h * (A // 2), 8), A // 2)
    return ref.at[0, rows] if kind == 'col' else ref.at[0, :, rows]


def _chip_part(piece, kind, B, s):
    if kind == 'col':
        return piece.at[:, pl.ds(pl.multiple_of(s * B, 128), B)]
    return piece.at[s]


def _place(shard, layer, kind, chip_idx, out_dtype, *, name):
    L, A, B = shard.shape
    ta = _div_tile(A, 256, 16)

    def body(s_ref, x_ref, o_ref):
        o_ref[...] = x_ref[...].astype(out_dtype)

    if kind == 'col':
        shape = (1, A, N_CHIPS * B)
        out_spec = pl.BlockSpec((None, ta, B), lambda i, s: (0, i, s[0]))
    else:
        shape = (1, N_CHIPS, A, B)
        out_spec = pl.BlockSpec((None, None, ta, B), lambda i, s: (0, s[0], i, 0))
    grid_spec = pltpu.PrefetchScalarGridSpec(
        num_scalar_prefetch=1, grid=(A // ta,),
        in_specs=[pl.BlockSpec((None, ta, B), lambda i, s: (layer, i, 0))], out_specs=out_spec)
    return pl.pallas_call(
        body, name=name, grid_spec=grid_spec, out_shape=jax.ShapeDtypeStruct(shape, out_dtype),
        compiler_params=_cparams(("parallel",)),
    )(chip_idx, shard)


HBM = pl.BlockSpec(memory_space=pltpu.HBM)
SEM = pl.BlockSpec(memory_space=pltpu.SEMAPHORE)
DATAFLOW = pltpu.SideEffectType.DATAFLOW_SIDE_EFFECTING


def _own_part_copies(refs, meta, send_sems, recv_sems):
    x, y, c = _mesh_pos()
    cps = []
    for k, (kind, L, A, B) in enumerate(meta):
        for j, (cx, cy) in enumerate(_other_chips(x, y)):
            part = _chip_part(refs[k].at[0], kind, B, 2 * x + y)
            cps.append(pltpu.make_async_remote_copy(
                src_ref=part, dst_ref=part, send_sem=send_sems.at[3 * k + j], recv_sem=recv_sems.at[3 * k + j],
                device_id=(cx, cy, c), device_id_type=MESH))
    return cps


def _gather_start(fulls, meta, after, tag):
    n = len(fulls)

    def body(*refs):
        send_sems, recv_sems = refs[n + 1], refs[n + 2]
        outs = refs[n + 3:2 * n + 3]
        token = refs[2 * n + 3]
        for cp in _own_part_copies(outs, meta, send_sems, recv_sems):
            cp.start()
        token[...] = jnp.zeros_like(token)

    res = pl.pallas_call(
        body, name=f"ag_start_{tag}", in_specs=[HBM] * n + [pl.BlockSpec(memory_space=pl.ANY)],
        out_specs=(SEM, SEM) + (HBM,) * n + (pl.BlockSpec(memory_space=pltpu.VMEM),),
        out_shape=(pltpu.SemaphoreType.DMA((3 * n,)), pltpu.SemaphoreType.DMA((3 * n,)))
        + tuple(pltpu.HBM(f.shape, f.dtype) for f in fulls) + (jax.ShapeDtypeStruct((8, 128), F32),),
        input_output_aliases={k: 2 + k for k in range(n)},
        compiler_params=pltpu.CompilerParams(has_side_effects=DATAFLOW),
    )(*[pltpu.with_memory_space_constraint(f, pltpu.HBM) for f in fulls], after)
    return res[0], res[1], list(res[2:2 + n]), res[2 + n]


def _gather_wait(send_sems, recv_sems, fulls, meta, after, tag):
    n = len(fulls)

    def body(*refs):
        f_refs = refs[:n]
        send_sems, recv_sems = refs[n], refs[n + 1]
        for cp in _own_part_copies(f_refs, meta, send_sems, recv_sems):
            cp.wait_send()
            cp.wait_recv()

    return pl.pallas_call(
        body, name=f"ag_wait_{tag}", in_specs=[HBM] * n + [SEM, SEM, pl.BlockSpec(memory_space=pl.ANY)],
        out_specs=[HBM] * n, out_shape=[pltpu.HBM(f.shape, f.dtype) for f in fulls],
        input_output_aliases={k: k for k in range(n)},
        compiler_params=pltpu.CompilerParams(has_side_effects=DATAFLOW),
    )(*fulls, send_sems, recv_sems, after)


def _scatter_copies(g_refs, land_refs, meta, send_sems, recv_sems):
    x, y, c = _mesh_pos()
    cps = []
    for k, (kind, L, A, B) in enumerate(meta):
        for j, (cx, cy) in enumerate(_other_chips(x, y)):
            cps.append(pltpu.make_async_remote_copy(
                src_ref=_chip_part(g_refs[k].at[0], kind, B, 2 * cx + cy), dst_ref=land_refs[k].at[j],
                send_sem=send_sems.at[3 * k + j], recv_sem=recv_sems.at[3 * k + j], device_id=(cx, cy, c),
                device_id_type=MESH))
    return cps


def _scatter_start(gs, meta, after, tag):
    n = len(gs)

    def body(*refs):
        send_sems, recv_sems = refs[2 * n + 1], refs[2 * n + 2]
        g_out = refs[2 * n + 3:3 * n + 3]
        lands = refs[3 * n + 3:4 * n + 3]
        token = refs[4 * n + 3]
        for cp in _scatter_copies(g_out, lands, meta, send_sems, recv_sems):
            cp.start()
        token[...] = jnp.zeros_like(token)

    land_shapes = [(3, A, B) for kind, L, A, B in meta]
    lands = [pltpu.with_memory_space_constraint(lax.empty(s, g.dtype), pltpu.HBM) for s, g in zip(land_shapes, gs)]
    res = pl.pallas_call(
        body, name=f"rs_start_{tag}", in_specs=[HBM] * (2 * n) + [pl.BlockSpec(memory_space=pl.ANY)],
        out_specs=(SEM, SEM) + (HBM,) * (2 * n) + (pl.BlockSpec(memory_space=pltpu.VMEM),),
        out_shape=(pltpu.SemaphoreType.DMA((3 * n,)), pltpu.SemaphoreType.DMA((3 * n,)))
        + tuple(pltpu.HBM(g.shape, g.dtype) for g in gs)
        + tuple(pltpu.HBM(s, g.dtype) for s, g in zip(land_shapes, gs)) + (jax.ShapeDtypeStruct((8, 128), F32),),
        input_output_aliases={k: 2 + k for k in range(2 * n)},
        compiler_params=pltpu.CompilerParams(has_side_effects=DATAFLOW),
    )(*[pltpu.with_memory_space_constraint(g, pltpu.HBM) for g in gs], *lands, after)
    return res[0], res[1], list(res[2:2 + n]), list(res[2 + n:2 + 2 * n]), res[2 + 2 * n]


def _scatter_wait(send_sems, recv_sems, gs, lands, meta, after, tag):
    n = len(gs)

    def body(*refs):
        g_refs, land_refs = refs[:n], refs[n:2 * n]
        send_sems, recv_sems = refs[2 * n], refs[2 * n + 1]
        for cp in _scatter_copies(g_refs, land_refs, meta, send_sems, recv_sems):
            cp.wait_send()
            cp.wait_recv()

    both = list(gs) + list(lands)
    res = pl.pallas_call(
        body, name=f"rs_wait_{tag}", in_specs=[HBM] * (2 * n) + [SEM, SEM, pl.BlockSpec(memory_space=pl.ANY)],
        out_specs=[HBM] * (2 * n), out_shape=[pltpu.HBM(a.shape, a.dtype) for a in both],
        input_output_aliases={k: k for k in range(2 * n)},
        compiler_params=pltpu.CompilerParams(has_side_effects=DATAFLOW),
    )(*both, send_sems, recv_sems, after)
    return list(res[:n]), list(res[n:])


def _chip_sum_full(g, got, m, chip_idx, *, name):
    kind, L, A, B = m
    ta = _div_tile(A, 256, 16)

    def body(r_ref, a_ref, b_ref, o_ref):
        acc = a_ref[...].astype(F32)
        for j in range(3):
            acc = acc + b_ref[j].astype(F32)
        o_ref[...] = acc

    if kind == 'col':
        g_spec = pl.BlockSpec((None, ta, B), lambda i, r: (0, i, r[0]))
    else:
        g_spec = pl.BlockSpec((None, None, ta, B), lambda i, r: (0, r[0], i, 0))
    grid_spec = pltpu.PrefetchScalarGridSpec(
        num_scalar_prefetch=1, grid=(A // ta,),
        in_specs=[g_spec, pl.BlockSpec((3, ta, B), lambda i, r: (0, i, 0))],
        out_specs=pl.BlockSpec((None, ta, B), lambda i, r: (0, i, 0)))
    return pl.pallas_call(
        body, name=name, grid_spec=grid_spec, out_shape=jax.ShapeDtypeStruct((1, A, B), F32),
        compiler_params=_cparams(("parallel",)),
    )(chip_idx, g, got)


def _all_gather(fulls, meta):
    n = len(fulls)

    def body(*refs):
        outs = refs[n:2 * n]
        send_sems, recv_sems = refs[2 * n:]
        x, y, c = _mesh_pos()
        sibling = (x, y, 1 - c)
        chips = _other_chips(x, y)

        def part(k, s, h):
            kind, L, A, B = meta[k]
            return _chip_part(_gathered_piece(outs[k], kind, L, A, h), kind, B, s)

        def copy(ref, sem, to):
            return pltpu.make_async_remote_copy(src_ref=ref, dst_ref=ref, send_sem=send_sems.at[sem],
                                                recv_sem=recv_sems.at[sem], device_id=to, device_id_type=MESH)

        started = []
        for k in range(n):
            for j, (cx, cy) in enumerate(chips):
                cp = copy(part(k, 2 * x + y, c), 3 * k + j, (cx, cy, c))
                cp.start()
                started.append(cp)
        for j, (cx, cy) in enumerate(chips):
            for k in range(n):
                landed = part(k, 2 * cx + cy, c)
                copy(landed, 3 * k + j, (cx, cy, c)).wait_recv()
                fwd = copy(landed, 3 * n + 3 * k + j, sibling)
                fwd.start()
                started.append(fwd)
        for j, (cx, cy) in enumerate(chips):
            for k in range(n):
                copy(part(k, 2 * cx + cy, 1 - c), 3 * n + 3 * k + j, sibling).wait_recv()
        for cp in started:
            cp.wait_send()

    return pl.pallas_call(
        body, name="ag_weights", in_specs=[ANY] * n, out_specs=[ANY] * n,
        out_shape=[jax.ShapeDtypeStruct(f.shape, f.dtype) for f in fulls],
        input_output_aliases={k: k for k in range(n)},
        scratch_shapes=[pltpu.SemaphoreType.DMA((6 * n,)), pltpu.SemaphoreType.DMA((6 * n,))],
    )(*fulls)


def _pair_swap(sums, small, *, tag):
    ns = len(sums)
    with_small = small is not None
    n_in = ns + with_small

    def body(*refs):
        sum_refs = refs[:ns]
        got_refs = refs[n_in:n_in + ns]
        send_sems, recv_sems = refs[2 * n_in], refs[2 * n_in + 1]
        x, y, c = _mesh_pos()
        cps = []
        for k in range(ns):
            cp = pltpu.make_async_remote_copy(
                src_ref=sum_refs[k], dst_ref=got_refs[k], send_sem=send_sems.at[k], recv_sem=recv_sems.at[k],
                device_id=(x, y, 1 - c), device_id_type=MESH)
            cp.start()
            cps.append(cp)
        if with_small:
            small_ref, sm_ref, local_sem = refs[ns], refs[n_in + ns], refs[2 * n_in + 2]
            me = 4 * x + 2 * y + c
            own = pltpu.make_async_copy(small_ref, sm_ref.at[me], local_sem)
            own.start()
            for r in range(1, N_DEV):
                fx, fy, fc = (r >> 2) & 1, (r >> 1) & 1, r & 1
                peer = (1 - x if fx else x, 1 - y if fy else y, 1 - c if fc else c)
                cp = pltpu.make_async_remote_copy(
                    src_ref=small_ref, dst_ref=sm_ref.at[me], send_sem=send_sems.at[ns + r],
                    recv_sem=recv_sems.at[ns + r], device_id=peer, device_id_type=MESH)
                cp.start()
                cps.append(cp)
        for cp in cps:
            cp.wait()
        if with_small:
            own.wait()

    out_shape = [jax.ShapeDtypeStruct(s.shape, s.dtype) for s in sums]
    scratch = [pltpu.SemaphoreType.DMA((ns + N_DEV,)), pltpu.SemaphoreType.DMA((ns + N_DEV,))]
    args = list(sums)
    if with_small:
        out_shape.append(jax.ShapeDtypeStruct((N_DEV,) + small.shape, F32))
        scratch.append(pltpu.SemaphoreType.DMA)
        args.append(small)
    res = pl.pallas_call(
        body, name=f"rs_pair_swap_{tag}", in_specs=[ANY] * n_in, out_specs=[ANY] * n_in, out_shape=out_shape,
        scratch_shapes=scratch,
    )(*args)
    return list(res[:ns]), (res[ns] if with_small else None)


def _tiny_pack(conv_a_w, sc_conv_w):
    lead = conv_a_w.shape[:-2]
    sc = sc_conv_w.reshape(lead + (2 * SC_WIDTH, 128))
    z = lambda r: jnp.zeros(lead + (r, 128), F32)
    return jnp.concatenate([conv_a_w, z(32 - CONV_A_WIDTH), sc, z(TINY_ROWS - 32 - 2 * SC_WIDTH)], axis=-2)


def _tiny_unpack(t):
    lead = t.shape[:-2]
    return t[..., :CONV_A_WIDTH, :], t[..., 32:32 + 2 * SC_WIDTH, :].reshape(lead + (SC_WIDTH, 256))


def _pack_small(d):
    flat = jnp.concatenate([d[n].astype(F32).reshape(-1) for n in SMALL_NAMES])
    n = flat.shape[0]
    total = -(-n // 1024) * 1024
    return jnp.pad(flat, (0, total - n)).reshape(total // 128, 128)


def _unpack_small(packed, shapes):
    flat = packed.reshape(-1)
    out, off = {}, 0
    for n in SMALL_NAMES:
        sz = math.prod(shapes[n])
        out[n] = flat[off:off + sz].reshape(shapes[n])
        off += sz
    return out


def _ffn_fwd(h, g, W, n_gu, n_down, i, tag):
    h2, u, gate, up, a = _ffn_fwd_fused(h, g, W[n_gu][i], W[n_down][i], 0, name=f"{tag}_fwd")
    return h2, (h, u, gate, up, a)


def _ffn_bwd(dh, saved, g, W, n_gu, n_down, i, G, tag):
    h, u, gate, up, a = saved
    dh_in, dg, dgate, dup = _ffn_bwd_fused(dh, h, g, gate, up, W[n_gu][i], W[n_down][i], 0, name=f"{tag}_bwd")
    G[(n_down, i)] = _mm(a, dh, name=f"{tag}_b_wdown", ta=True, tm=1408, tn=1024, tk=1024, scale=0.5)
    tn = FFN_CHUNK
    half = _mm(u, dgate, name=f"{tag}_b_wg", ta=True, tm=1024, tn=tn, tk=2048, stack=(1, 0, None, 2 * D_FF))
    G[(n_gu, i)] = _mm(u, dup, name=f"{tag}_b_wu", ta=True, tm=1024, tn=tn, tk=2048, stack=(1, 0, half, 2 * D_FF),
                       n_map=lambda j: j + D_FF // tn)
    return dh_in, dg


def _xa_block_fwd(h, mem, g, gm, W, i, tag):
    mn = _rms_fwd(mem, gm, name=f"{tag}_mem_norm")
    u, q = _norm_mm(h, g, W['xa_wq'][i], 0, name=f"{tag}_q", tn=1024)
    kv = _mm(mn, W['xa_wkv'][i], b_layer=0, name=f"{tag}_kv", tm=256, tn=1024, tk=1024)
    o = _xa_fwd(q, kv, name=f"{tag}_attn")
    h2 = _mm(o, W['xa_wo'][i], b_layer=0, name=f"{tag}_o", out_dtype=F32, tm=1024, tn=1024, tk=1024, res=h)
    return h2, (h, u, mn, q, kv, o)


def _xa_block_bwd(dh, saved, mem, g, gm, W, i, G, tag):
    h, u, mn, q, kv, o = saved
    do = _mm(dh, W['xa_wo'][i], b_layer=0, name=f"{tag}_b_do", tb=True, tm=1024, tn=1024, tk=1024)
    G[('xa_wo', i)] = _mm(o, dh, name=f"{tag}_b_wo", ta=True, tm=1024, tn=1024, tk=1024)
    dq, dkv = _xa_bwd(q, kv, do, name=f"{tag}_b_attn")
    G[('xa_wq', i)] = _mm(u, dq, name=f"{tag}_b_wq", ta=True, tm=1024, tn=1024, tk=1024)
    dh_in, dg = _mm_norm_bwd(dq, W['xa_wq'][i], 0, h, g, dh, name=f"{tag}_b_du", tk=1024)
    G[('xa_wkv', i)] = _mm(mn, dkv, name=f"{tag}_b_wkv", ta=True, tm=1024, tn=1024, tk=256)
    dmn = _mm(dkv, W['xa_wkv'][i], b_layer=0, name=f"{tag}_b_dmn", tb=True, out_dtype=F32, tm=256, tn=1024, tk=1024)
    _, dgm = _rms_bwd(mem, gm, dmn, None, name=f"{tag}_b_mem_norm")
    return dh_in, dg, dgm


def _pad_conv_w(w, rows):
    return jnp.pad(w.astype(F32), ((0, rows - w.shape[0]), (0, 0)))


def _even_fwd(h, g, W, conv_w, conv_b, ln_g, ln_b, sinks, tag):
    u, z = _norm_mm(h, g, W['even_w_in'][0], 0, name=f"{tag}_in", tn=1792)
    c, act = _conv_a_fwd(z, conv_w, conv_b, ln_g, ln_b, name=f"{tag}_conv")
    kpad = jnp.pad(z[:, 1536:1664], ((WINDOW, 0), (0, 0)))
    vpad = jnp.pad(z[:, 1664:1792], ((WINDOW, 0), (0, 0)))
    o = _swa_fwd(z, kpad, vpad, sinks, name=f"{tag}_swa")
    cat = jnp.concatenate([act, o], axis=-1)
    h2 = _mm(cat, W['even_w_out'][0], b_layer=0, name=f"{tag}_out", out_dtype=F32, tm=1024, tn=1024, tk=1024, res=h)
    return h2, (h, u, z, c, kpad, vpad, cat)


def _even_bwd(dh, saved, g, W, conv_w, ln_g, ln_b, sinks, G, tag):
    h, u, z, c, kpad, vpad, cat = saved
    dcat = _mm(dh, W['even_w_out'][0], b_layer=0, name=f"{tag}_b_dcat", tb=True, tm=1024, tn=1024, tk=1024)
    G[('even_w_out', 0)] = _mm(cat, dh, name=f"{tag}_b_wout", ta=True, tm=1024, tn=1024, tk=1024)
    dz_a, small = _conv_a_bwd(z, c, dcat, conv_w, ln_g, ln_b, name=f"{tag}_b_conv")
    dq, dkp, dvp, dsinks = _swa_bwd(z, kpad, vpad, sinks, dcat, name=f"{tag}_b_swa")
    dz = jnp.concatenate([dz_a, dq, dkp[WINDOW:].astype(BF16), dvp[WINDOW:].astype(BF16)], axis=-1)
    G[('even_w_in', 0)] = _mm(u, dz, name=f"{tag}_b_win", ta=True, tm=1024, tn=1792, tk=1024)
    dh_in, dg = _mm_norm_bwd(dz, W['even_w_in'][0], 0, h, g, dh, name=f"{tag}_b_du", tk=1792)
    grads = dict(mix=dg, conv_a_w=small[:CONV_A_WIDTH], conv_a_b=small[32:33], conv_a_ln_g=small[33:34],
                 conv_a_ln_b=small[34:35], swa_sinks=dsinks[:, 0])
    return dh_in, grads


def _odd_fwd(h, g, W, conv_w, tag):
    u, z = _norm_mm(h, g, W['odd_w_in'][0], 0, name=f"{tag}_in", tn=1024)
    y, cc = _sconv_fwd(z, conv_w, name=f"{tag}_conv")
    h2 = _mm(y, W['odd_w_out'][0], b_layer=0, name=f"{tag}_out", out_dtype=F32, tm=1024, tn=1024, tk=1024, res=h)
    return h2, (h, u, z, y, cc)


def _odd_bwd(dh, saved, g, W, conv_w, G, tag):
    h, u, z, y, cc = saved
    dy = _mm(dh, W['odd_w_out'][0], b_layer=0, name=f"{tag}_b_dy", tb=True, tm=1024, tn=1024, tk=1024)
    G[('odd_w_out', 0)] = _mm(y, dh, name=f"{tag}_b_wout", ta=True, tm=1024, tn=1024, tk=1024)
    dz, dw = _sconv_bwd(z, cc, dy, conv_w, name=f"{tag}_b_conv")
    G[('odd_w_in', 0)] = _mm(u, dz, name=f"{tag}_b_win", ta=True, tm=1024, tn=1024, tk=1024)
    dh_in, dg = _mm_norm_bwd(dz, W['odd_w_in'][0], 0, h, g, dh, name=f"{tag}_b_du", tk=1024)
    return dh_in, dict(mix=dg, sc_conv_w=dw[:SC_WIDTH])


def _local_step(x, mem, tgt, W, need, token, ready, conv_a_w, sc_conv_w, P):
    row = lambda v: v.reshape(1, -1)
    conv_a_w = _pad_conv_w(conv_a_w, 32)
    sc_w = _pad_conv_w(sc_conv_w, 8)
    sinks = P['swa_sinks'][0]

    def arrive(stage, h):
        for n, ws in need(stage, h).items():
            W[n] = W.get(n, []) + ws

    h = x
    saved = []
    for i in range(2):
        t = f"l{i}"
        if i == 1:
            arrive('l1_ffn1', h)
        g1 = row(P['ffn1_norm'][i]) + (token if i == 0 else 0.0)
        h, s1 = _ffn_fwd(h, g1, W, 'ffn1_w_gu', 'ffn1_w_down', i, f"{t}_ffn1")
        arrive(f"{t}_mix", h)
        if i == 0:
            h, s2 = _even_fwd(h, row(P['mix_norm'][i]), W, conv_a_w, P['conv_a_b'], P['conv_a_ln_g'],
                              P['conv_a_ln_b'], sinks, f"{t}_even")
        else:
            h, s2 = _odd_fwd(h, row(P['mix_norm'][i]), W, sc_w, f"{t}_odd")
        h, s3 = _xa_block_fwd(h, mem, row(P['xa_norm'][i]), row(P['xa_mem_norm'][i]), W, i, f"{t}_xa")
        arrive(f"{t}_ffn2", h)
        h, s4 = _ffn_fwd(h, row(P['ffn2_norm'][i]), W, 'ffn2_w_gu', 'ffn2_w_down', i, f"{t}_ffn2")
        saved.append((s1, s2, s3, s4))

    loss, dh, d_final = _final_loss(h, row(P['final_norm']), tgt, name="final_loss")

    G = {}
    gp = {n: [None, None] for n in ('ffn1_norm', 'mix_norm', 'xa_norm', 'xa_mem_norm', 'ffn2_norm')}
    single = {}
    for i in (1, 0):
        t = f"l{i}"
        s1, s2, s3, s4 = saved[i]
        g4 = row(P['ffn2_norm'][i]) + (ready('l1', G) if i == 0 else 0.0)
        dh, gp['ffn2_norm'][i] = _ffn_bwd(dh, s4, g4, W, 'ffn2_w_gu', 'ffn2_w_down', i, G, f"{t}_ffn2")
        dh, gp['xa_norm'][i], gp['xa_mem_norm'][i] = _xa_block_bwd(
            dh, s3, mem, row(P['xa_norm'][i]), row(P['xa_mem_norm'][i]), W, i, G, f"{t}_xa")
        if i == 0:
            dh, g2 = _even_bwd(dh, s2, row(P['mix_norm'][i]), W, conv_a_w, P['conv_a_ln_g'], P['conv_a_ln_b'], sinks,
                               G, f"{t}_even")
        else:
            dh, g2 = _odd_bwd(dh, s2, row(P['mix_norm'][i]), W, sc_w, G, f"{t}_odd")
        gp['mix_norm'][i] = g2.pop('mix')
        single.update(g2)
        g1 = row(P['ffn1_norm'][i]) + (ready('l0_rest', G) if i == 0 else 0.0)
        dh, gp['ffn1_norm'][i] = _ffn_bwd(dh, s1, g1, W, 'ffn1_w_gu', 'ffn1_w_down', i, G, f"{t}_ffn1")

    small = {n: jnp.concatenate(v, axis=0) for n, v in gp.items()}
    small['conv_a_b'] = single['conv_a_b']
    small['conv_a_ln_g'] = single['conv_a_ln_g']
    small['conv_a_ln_b'] = single['conv_a_ln_b']
    small['swa_sinks'] = single['swa_sinks'][None]
    small['final_norm'] = d_final[0]
    small['conv_a_w'] = single['conv_a_w']
    small['sc_conv_w'] = single['sc_conv_w']
    return loss[0, 0], dh, G, small


def kernel(x, mem, ffn1_norm, ffn1_w_gu, ffn1_w_down, mix_norm, even_w_in, conv_a_w, conv_a_b, conv_a_ln_g, conv_a_ln_b, swa_sinks, even_w_out, odd_w_in, sc_conv_w, odd_w_out, xa_norm, xa_mem_norm, xa_wq, xa_wkv, xa_wo, ffn2_norm, ffn2_w_gu, ffn2_w_down, final_norm, loss_target, m_ffn1_norm, m_ffn1_w_gu, m_ffn1_w_down, m_mix_norm, m_even_w_in, m_conv_a_w, m_conv_a_b, m_conv_a_ln_g, m_conv_a_ln_b, m_swa_sinks, m_even_w_out, m_odd_w_in, m_sc_conv_w, m_odd_w_out, m_xa_norm, m_xa_mem_norm, m_xa_wq, m_xa_wkv, m_xa_wo, m_ffn2_norm, m_ffn2_w_gu, m_ffn2_w_down, m_final_norm, v_ffn1_norm, v_ffn1_w_gu, v_ffn1_w_down, v_mix_norm, v_even_w_in, v_conv_a_w, v_conv_a_b, v_conv_a_ln_g, v_conv_a_ln_b, v_swa_sinks, v_even_w_out, v_odd_w_in, v_sc_conv_w, v_odd_w_out, v_xa_norm, v_xa_mem_norm, v_xa_wq, v_xa_wkv, v_xa_wo, v_ffn2_norm, v_ffn2_w_gu, v_ffn2_w_down, v_final_norm):
    w = dict(zip(WEIGHT_NAMES, (ffn1_norm, ffn1_w_gu, ffn1_w_down, mix_norm, even_w_in, conv_a_w, conv_a_b, conv_a_ln_g, conv_a_ln_b, swa_sinks, even_w_out, odd_w_in, sc_conv_w, odd_w_out, xa_norm, xa_mem_norm, xa_wq, xa_wkv, xa_wo, ffn2_norm, ffn2_w_gu, ffn2_w_down, final_norm)))
    m = dict(zip(WEIGHT_NAMES, (m_ffn1_norm, m_ffn1_w_gu, m_ffn1_w_down, m_mix_norm, m_even_w_in, m_conv_a_w, m_conv_a_b, m_conv_a_ln_g, m_conv_a_ln_b, m_swa_sinks, m_even_w_out, m_odd_w_in, m_sc_conv_w, m_odd_w_out, m_xa_norm, m_xa_mem_norm, m_xa_wq, m_xa_wkv, m_xa_wo, m_ffn2_norm, m_ffn2_w_gu, m_ffn2_w_down, m_final_norm)))
    v = dict(zip(WEIGHT_NAMES, (v_ffn1_norm, v_ffn1_w_gu, v_ffn1_w_down, v_mix_norm, v_even_w_in, v_conv_a_w, v_conv_a_b, v_conv_a_ln_g, v_conv_a_ln_b, v_swa_sinks, v_even_w_out, v_odd_w_in, v_sc_conv_w, v_odd_w_out, v_xa_norm, v_xa_mem_norm, v_xa_wq, v_xa_wkv, v_xa_wo, v_ffn2_norm, v_ffn2_w_gu, v_ffn2_w_down, v_final_norm)))
    small_shapes = {n: w[n].shape for n in SMALL_NAMES}
    cx, cy, cc = lax.axis_index("x"), lax.axis_index("y"), lax.axis_index("c")
    chip_idx = (2 * cx + cy).astype(jnp.int32).reshape(1)

    shards = {n: w[n] for n in COMM_NAMES if n != 'tiny'}
    shards['tiny'] = _tiny_pack(conv_a_w, sc_conv_w)
    first =[('ffn1_w_gu', 0), ('ffn1_w_down', 0), ('tiny', 0)]
    stages = {
        'l0_mix': [('even_w_in', 0), ('even_w_out', 0), ('xa_wq', 0), ('xa_wkv', 0), ('xa_wo', 0)],
        'l0_ffn2': [('ffn2_w_gu', 0), ('ffn2_w_down', 0)],
        'l1_ffn1': [('ffn1_w_gu', 1), ('ffn1_w_down', 1)],
        'l1_mix': [('odd_w_in', 0), ('odd_w_out', 0), ('xa_wq', 1), ('xa_wkv', 1), ('xa_wo', 1)],
        'l1_ffn2': [('ffn2_w_gu', 1), ('ffn2_w_down', 1)],
    }
    grad_stages = {'l1': stages['l1_ffn1'] + stages['l1_mix'] + stages['l1_ffn2'],
                   'l0_rest': stages['l0_mix'] + stages['l0_ffn2'], 'l0_ffn1': first}

    def place(items):
        return [_place(shards[n], l, LAYOUT[n], chip_idx, F32 if n == 'tiny' else BF16, name=f"place_{n}_{l}")
                for n, l in items]

    def item_meta(items):
        return [(LAYOUT[n], 1) + shards[n].shape[1:] for n, l in items]

    def natural(items, arrays):
        out = {}
        for (n, l), a in zip(items, arrays):
            if n == 'tiny':
                continue
            if n == 'even_w_in':
                out[n] = [a.transpose(0, 2, 1, 3).reshape(1, D_MODEL, -1)]
            else:
                out[n] = [a if LAYOUT[n] == 'col' else a.reshape(1, N_CHIPS * a.shape[2], a.shape[3])]
        return out

    first_full = _all_gather(place(first), item_meta(first))
    W = natural(first, first_full)
    ca, sc = _tiny_unpack(first_full[-1][0])
    conv_a_full = ca.transpose(1, 0, 2).reshape(CONV_A_WIDTH, CONV_A_CH)
    sc_full = sc.transpose(1, 0, 2).reshape(SC_WIDTH, SC_CH)
    gathers = {}
    token = first_full[0]
    for stage, items in stages.items():
        send, recv, in_flight, token = _gather_start(place(items), item_meta(items), token, stage)
        gathers[stage] = (send, recv, in_flight)

    def need(stage, h):
        send, recv, in_flight = gathers[stage]
        items = stages[stage]
        return natural(items, _gather_wait(send, recv, in_flight, item_meta(items), h, stage))

    def gathered_layout(G, item):
        n, l = item
        A, B = shards[n].shape[1:]
        g = G[item]
        if n == 'tiny':
            return g
        if n == 'even_w_in':
            return g.reshape(A, N_CHIPS, B).transpose(1, 0, 2)[None]
        return g.reshape(1, A, N_CHIPS * B) if LAYOUT[n] == 'col' else g.reshape(1, N_CHIPS, A, B)

    scatters, tokens = {}, {}

    def ready(stage, G):
        items = grad_stages[stage]
        send, recv, gs1, lands, tok = _scatter_start([gathered_layout(G, it) for it in items], item_meta(items),
                                                     chip_idx, stage)
        scatters[stage] = (send, recv, gs1, lands)
        tokens[stage] = tok
        return tok[:1, :1]

    loss_part, grad_x, G, g_small = _local_step(x[0], mem[0], loss_target[0], W, need, token[:1, :1], ready,
                                                conv_a_full, sc_full, {n: w[n] for n in SMALL_NAMES})
    G[('tiny', 0)] = _tiny_pack(g_small['conv_a_w'].reshape(CONV_A_WIDTH, N_CHIPS, 128).transpose(1, 0, 2),
                                g_small['sc_conv_w'].reshape(SC_WIDTH, N_CHIPS, 256).transpose(1, 0, 2))[None]
    loss = lax.psum(loss_part, ("x", "y", "c"))

    ready('l0_ffn1', G)
    started = tokens['l0_ffn1']

    def summed(stage, after):
        send, recv, gs1, lands = scatters[stage]
        items = grad_stages[stage]
        sent, landed = _scatter_wait(send, recv, gs1, lands, item_meta(items), after, stage)
        return items, [_chip_sum_full(g, r, m_, chip_idx, name=f"rs_chip_sum_{n}_{l}")
                       for (n, l), g, r, m_ in zip(items, sent, landed, item_meta(items))]

    def adamw(n, g1):
        if n == 'tiny':
            pk = lambda d: _tiny_pack(d['conv_a_w'], d['sc_conv_w'])
            res = [_tiny_unpack(a) for a in _adamw_layers(pk(w), pk(m), pk(v), [g1[('tiny', 0)]], name="adamw_tiny")]
            for k, nn in enumerate(('conv_a_w', 'sc_conv_w')):
                grads[nn], deltas[nn], new_m[nn], new_v[nn] = (r[k] for r in res)
        else:
            gsrc = [g1[(n, l)] for l in range(w[n].shape[0])]
            grads[n], deltas[n], new_m[n], new_v[n] = _adamw_layers(w[n], m[n], v[n], gsrc, name=f"adamw_{n}")

    grads, deltas, new_m, new_v = {}, {}, {}, {}
    items, sums = [], []
    for stage in ('l1', 'l0_rest'):
        its, ss = summed(stage, started)
        items += its
        sums += ss
    sib_sums, small_parts = _pair_swap(sums, _pack_small(g_small), tag="main")
    g1 = {it: [a, b] for it, a, b in zip(items, sums, sib_sums)}
    last_names = [n for n, l in grad_stages['l0_ffn1']]
    for n in COMM_NAMES:
        if n not in last_names:
            adamw(n, g1)

    its, ss = summed('l0_ffn1', deltas['ffn2_w_down'])
    sib, _ = _pair_swap(ss, None, tag="last")
    g1.update({it: [a, b] for it, a, b in zip(its, ss, sib)})
    for n in last_names:
        adamw(n, g1)
    gs, ds, ms, vs = _adamw_small(_pack_small({n: w[n] for n in SMALL_NAMES}), small_parts,
                                  _pack_small({n: m[n] for n in SMALL_NAMES}),
                                  _pack_small({n: v[n] for n in SMALL_NAMES}), name="adamw_small")
    for dst, packed in ((grads, gs), (deltas, ds), (new_m, ms), (new_v, vs)):
        dst.update(_unpack_small(packed, small_shapes))

    return (loss, grad_x[None], *[grads[n] for n in WEIGHT_NAMES], *[deltas[n] for n in WEIGHT_NAMES],
            *[new_m[n] for n in WEIGHT_NAMES], *[new_v[n] for n in WEIGHT_NAMES])
```

```python
import jax
import jax.numpy as jnp
from jax import lax
from jax.experimental import pallas as pl
from jax.experimental.pallas import tpu as pltpu

F32 = jnp.float32
BF16 = jnp.bfloat16

D_MODEL = 1024
D_FF = 2816
CONV_A_CH = 512
CONV_A_WIDTH = 31
SWA_HEADS = 8
SWA_KV_HEADS = 2
SWA_GROUP = 4
HEAD_DIM = 64
WINDOW = 128
SC_CH = 1024
SC_WIDTH = 3
XA_HEADS = 4
XA_HEAD_DIM = 256
RMS_EPS = 1e-6
LN_EPS = 1e-5

ADAM_LR = 0.001
ADAM_B1 = 0.9
ADAM_B2 = 0.999
ADAM_EPS = 1e-08
ADAM_WD = 0.01
ADAM_STEP = 10
ADAM_TILE_ELEMS = 384 * 1024

N_CHIPS = 4
N_DEV = 8
NEG_BIG = -1e30
VMEM_LIMIT = 56 * 1024 * 1024
MESH = pl.DeviceIdType.MESH

INPUT_NAMES = ['x', 'mem', 'ffn1_norm', 'ffn1_w_gu', 'ffn1_w_down', 'mix_norm', 'even_w_in', 'conv_a_w', 'conv_a_b',
               'conv_a_ln_g', 'conv_a_ln_b', 'swa_sinks', 'even_w_out', 'odd_w_in', 'sc_conv_w', 'odd_w_out', 'xa_norm',
               'xa_mem_norm', 'xa_wq', 'xa_wkv', 'xa_wo', 'ffn2_norm', 'ffn2_w_gu', 'ffn2_w_down', 'final_norm']
WEIGHT_NAMES = INPUT_NAMES[2:]
BIG = [('ffn1_w_gu', 'col'), ('ffn1_w_down', 'row'), ('even_w_in', 'col'), ('conv_a_w', 'col'), ('even_w_out', 'row'),
       ('odd_w_in', 'col'), ('sc_conv_w', 'col'), ('odd_w_out', 'row'), ('xa_wq', 'row'), ('xa_wkv', 'col'),
       ('xa_wo', 'row'), ('ffn2_w_gu', 'col'), ('ffn2_w_down', 'row')]
BIG_NAMES = [n for n, _ in BIG]
SMALL_NAMES = [n for n in WEIGHT_NAMES if n not in BIG_NAMES]


def _cparams(sem=None, vmem=VMEM_LIMIT):
    kw = dict(vmem_limit_bytes=vmem)
    if sem is not None:
        kw['dimension_semantics'] = sem
    return pltpu.CompilerParams(**kw)


def _div_tile(n, want, align=8):
    if n <= want:
        return n
    t = (want // align) * align
    while t >= align:
        if n % t == 0:
            return t
        t -= align
    return n


def _mm(a, b, *, name, ta=False, tb=False, out_dtype=BF16, tm=512, tn=512, tk=512, res=None, scale=1.0,
        b_layer=None, stack=None, n_map=None):
    n_map = n_map or (lambda j: j)
    if ta:
        K, M = a.shape
    else:
        M, K = a.shape
    if tb:
        N, K2 = b.shape[-2:]
    else:
        K2, N = b.shape[-2:]
    assert K == K2, (a.shape, b.shape, ta, tb)
    tm = _div_tile(M, tm, 128 if ta else 16)
    tn = _div_tile(N, tn, 128)
    tk = _div_tile(K, tk, 16 if ta else 128)
    nk = K // tk
    a_spec = pl.BlockSpec((tk, tm), lambda i, j, k: (k, i)) if ta else pl.BlockSpec((tm, tk), lambda i, j, k: (i, k))
    if b_layer is None:
        b_spec = pl.BlockSpec((tn, tk), lambda i, j, k: (j, k)) if tb else pl.BlockSpec((tk, tn), lambda i, j, k: (k, j))
    elif tb:
        b_spec = pl.BlockSpec((None, tn, tk), lambda i, j, k: (b_layer, j, k))
    else:
        b_spec = pl.BlockSpec((None, tk, tn), lambda i, j, k: (b_layer, k, j))
    o_spec = pl.BlockSpec((tm, tn), lambda i, j, k: (i, j))
    out_shape = jax.ShapeDtypeStruct((M, N), out_dtype)
    out_spec = o_spec
    aliases = {}
    extra_specs, extra_args = [], ()
    if stack is not None:
        n_layers, layer, buf = stack[:3]
        n_total = stack[3] if len(stack) > 3 else N
        out_shape = jax.ShapeDtypeStruct((n_layers, M, n_total), out_dtype)
        out_spec = pl.BlockSpec((None, tm, tn), lambda i, j, k: (layer, i, n_map(j)))
        if buf is not None:
            extra_specs, extra_args = [pl.BlockSpec(memory_space=pl.ANY)], (buf,)
            aliases = {2 + (res is not None): 0}
    dims = (((0 if ta else 1,), (1 if tb else 0,)), ((), ()))
    has_res = res is not None
    n_extra = len(extra_args)

    def body(*refs):
        if n_extra:
            refs = refs[:2 + has_res] + refs[2 + has_res + n_extra:]
        if has_res:
            a_ref, b_ref, r_ref, o_ref, acc_ref = refs
        else:
            a_ref, b_ref, o_ref, acc_ref = refs
        k = pl.program_id(2)
        p = lax.dot_general(a_ref[...].astype(BF16), b_ref[...].astype(BF16), dims, preferred_element_type=F32)

        @pl.when(k == 0)
        def _():
            acc_ref[...] = p

        @pl.when(k > 0)
        def _():
            acc_ref[...] += p

        @pl.when(k == nk - 1)
        def _():
            r = acc_ref[...] * scale
            if has_res:
                r = r_ref[...] + r
            o_ref[...] = r.astype(out_dtype)

    in_specs = [a_spec, b_spec] + ([o_spec] if has_res else []) + extra_specs
    args = (a, b) + ((res,) if has_res else ()) + extra_args
    return pl.pallas_call(
        body, name=name, grid=(M // tm, N // tn, nk), in_specs=in_specs, out_specs=out_spec,
        out_shape=out_shape, input_output_aliases=aliases,
        scratch_shapes=[pltpu.VMEM((tm, tn), F32)],
        compiler_params=_cparams(("parallel", "parallel", "arbitrary")),
    )(*args)


def _rms_fwd(x, g, *, name):
    S, D = x.shape
    ts = _div_tile(S, 512)

    def body(x_ref, g_ref, o_ref):
        xv = x_ref[...]
        r = lax.rsqrt(jnp.mean(xv * xv, axis=-1, keepdims=True) + RMS_EPS)
        o_ref[...] = (xv * r * g_ref[...]).astype(BF16)

    return pl.pallas_call(
        body, name=name, grid=(S // ts,),
        in_specs=[pl.BlockSpec((ts, D), lambda i: (i, 0)), pl.BlockSpec((1, D), lambda i: (0, 0))],
        out_specs=pl.BlockSpec((ts, D), lambda i: (i, 0)),
        out_shape=jax.ShapeDtypeStruct((S, D), BF16),
        compiler_params=_cparams(("parallel",)),
    )(x, g)


NORM_SLAB = 256


def _norm_mm(h, g, w, layer, *, name):
    S, D = h.shape
    N = w.shape[-1]
    tm = _div_tile(S, 1024, NORM_SLAB)
    slab = min(NORM_SLAB, tm)

    def body(h_ref, g_ref, w_ref, u_ref, z_ref):
        for r0 in range(0, tm, slab):
            rows = pl.ds(r0, slab)
            xv = h_ref[rows, :]
            r = lax.rsqrt(jnp.mean(xv * xv, axis=-1, keepdims=True) + RMS_EPS)
            u = (xv * r * g_ref[...]).astype(BF16)
            u_ref[rows, :] = u
            z_ref[rows, :] = jnp.dot(u, w_ref[...], preferred_element_type=F32).astype(BF16)

    row = pl.BlockSpec((tm, D), lambda i: (i, 0))
    return pl.pallas_call(
        body, name=name, grid=(S // tm,),
        in_specs=[row, pl.BlockSpec((1, D), lambda i: (0, 0)), pl.BlockSpec((None, D, N), lambda i: (layer, 0, 0))],
        out_specs=[row, pl.BlockSpec((tm, N), lambda i: (i, 0))],
        out_shape=[jax.ShapeDtypeStruct((S, D), BF16), jax.ShapeDtypeStruct((S, N), BF16)],
        compiler_params=_cparams(("parallel",)),
    )(h, g, w)


def _mm_norm_bwd(dz, w, layer, h, g, dres, *, name, tk):
    S, K = dz.shape
    D = h.shape[1]
    tm = _div_tile(S, 1024, NORM_SLAB)
    slab = min(NORM_SLAB, tm)
    tk = _div_tile(K, tk, 128)
    nk = K // tk
    nt = (((1,), (1,)), ((), ()))

    def body(dz_ref, w_ref, h_ref, g_ref, dr_ref, dx_ref, dg_ref, acc):
        i = pl.program_id(0)
        k = pl.program_id(1)

        def norm_bwd(du_of):
            part = jnp.zeros((1, D), F32)
            for r0 in range(0, tm, slab):
                rows = pl.ds(r0, slab)
                du = du_of(rows)
                xv = h_ref[rows, :]
                r = lax.rsqrt(jnp.mean(xv * xv, axis=-1, keepdims=True) + RMS_EPS)
                xhat = xv * r
                part = part + jnp.sum(du * xhat, axis=0, keepdims=True)
                dxhat = du * g_ref[...]
                dx_ref[rows, :] = dr_ref[rows, :] + r * (
                    dxhat - xhat * jnp.mean(dxhat * xhat, axis=-1, keepdims=True))

            @pl.when(i == 0)
            def _():
                dg_ref[...] = part

            @pl.when(i > 0)
            def _():
                dg_ref[...] += part

        if nk == 1:
            norm_bwd(lambda rows: lax.dot_general(dz_ref[rows, :], w_ref[...], nt, preferred_element_type=F32))
        else:
            p = lax.dot_general(dz_ref[...], w_ref[...], nt, preferred_element_type=F32)

            @pl.when(k == 0)
            def _():
                acc[...] = p

            @pl.when(k > 0)
            def _():
                acc[...] += p

            @pl.when(k == nk - 1)
            def _():
                norm_bwd(lambda rows: acc[rows, :])

    row = pl.BlockSpec((tm, D), lambda i, k: (i, 0))
    vec = pl.BlockSpec((1, D), lambda i, k: (0, 0))
    return pl.pallas_call(
        body, name=name, grid=(S // tm, nk),
        in_specs=[pl.BlockSpec((tm, tk), lambda i, k: (i, k)), pl.BlockSpec((None, D, tk), lambda i, k: (layer, 0, k)),
                  row, vec, row],
        out_specs=[row, vec],
        out_shape=[jax.ShapeDtypeStruct((S, D), F32), jax.ShapeDtypeStruct((1, D), F32)],
        scratch_shapes=[pltpu.VMEM((tm, D), F32)],
        compiler_params=_cparams(("arbitrary", "arbitrary")),
    )(dz, w, h, g, dres)


def _rms_bwd(x, g, du, dres, *, name):
    S, D = x.shape
    ts = _div_tile(S, 512)
    has_res = dres is not None

    def body(*refs):
        if has_res:
            x_ref, g_ref, du_ref, dr_ref, dx_ref, dg_ref = refs
        else:
            x_ref, g_ref, du_ref, dg_ref = refs
        i = pl.program_id(0)
        xv = x_ref[...]
        duv = du_ref[...].astype(F32)
        r = lax.rsqrt(jnp.mean(xv * xv, axis=-1, keepdims=True) + RMS_EPS)
        xhat = xv * r
        part = jnp.sum(duv * xhat, axis=0, keepdims=True)

        @pl.when(i == 0)
        def _():
            dg_ref[...] = part

        @pl.when(i > 0)
        def _():
            dg_ref[...] += part

        if has_res:
            dxhat = duv * g_ref[...]
            dx = r * (dxhat - xhat * jnp.mean(dxhat * xhat, axis=-1, keepdims=True))
            dx_ref[...] = dr_ref[...] + dx

    row = pl.BlockSpec((ts, D), lambda i: (i, 0))
    vec = pl.BlockSpec((1, D), lambda i: (0, 0))
    if has_res:
        dx, dg = pl.pallas_call(
            body, name=name, grid=(S // ts,), in_specs=[row, vec, row, row], out_specs=[row, vec],
            out_shape=[jax.ShapeDtypeStruct((S, D), F32), jax.ShapeDtypeStruct((1, D), F32)],
            compiler_params=_cparams(("arbitrary",)),
        )(x, g, du, dres)
        return dx, dg
    dg = pl.pallas_call(
        body, name=name, grid=(S // ts,), in_specs=[row, vec, row], out_specs=vec,
        out_shape=jax.ShapeDtypeStruct((1, D), F32),
        compiler_params=_cparams(("arbitrary",)),
    )(x, g, du)
    return None, dg


def _final_loss(h, g, tgt, *, name):
    S, D = h.shape
    ts = _div_tile(S, 512)

    def body(h_ref, g_ref, t_ref, loss_ref, dh_ref, dg_ref):
        i = pl.program_id(0)
        xv = h_ref[...]
        gv = g_ref[...]
        r = lax.rsqrt(jnp.mean(xv * xv, axis=-1, keepdims=True) + RMS_EPS)
        xhat = xv * r
        err = xhat * gv - t_ref[...]
        lpart = 0.5 * jnp.sum(jnp.mean(err * err, axis=-1, keepdims=True), axis=0, keepdims=True)
        dy = err * (1.0 / D)
        gpart = jnp.sum(dy * xhat, axis=0, keepdims=True)

        @pl.when(i == 0)
        def _():
            loss_ref[...] = jnp.broadcast_to(lpart, loss_ref.shape)
            dg_ref[...] = gpart

        @pl.when(i > 0)
        def _():
            loss_ref[...] += jnp.broadcast_to(lpart, loss_ref.shape)
            dg_ref[...] += gpart

        dxhat = dy * gv
        dh_ref[...] = r * (dxhat - xhat * jnp.mean(dxhat * xhat, axis=-1, keepdims=True))

    row = pl.BlockSpec((ts, D), lambda i: (i, 0))
    vec = pl.BlockSpec((1, D), lambda i: (0, 0))
    return pl.pallas_call(
        body, name=name, grid=(S // ts,), in_specs=[row, vec, row],
        out_specs=[pl.BlockSpec((8, 128), lambda i: (0, 0)), row, vec],
        out_shape=[jax.ShapeDtypeStruct((8, 128), F32), jax.ShapeDtypeStruct((S, D), F32),
                   jax.ShapeDtypeStruct((1, D), F32)],
        compiler_params=_cparams(("arbitrary",)),
    )(h, g, tgt)


def _sigmoid(x):
    return 1.0 / (1.0 + jnp.exp(-x))


FFN_CHUNK = 1408
FFN_CHUNKS = D_FF // FFN_CHUNK
FFN_BWD_PIECE = 384
FFN_BWD_SLAB = 256


def _ffn_fwd_fused(h, g, w_gu, w_down, layer, *, name):
    S, D = h.shape
    tm = _div_tile(S, 512, 16)
    tf, nj = FFN_CHUNK, FFN_CHUNKS

    def body(h_ref, g_ref, wg_ref, wu_ref, wd_ref, h2_ref, u_ref, gate_ref, up_ref, a_ref, u_s, acc):
        j = pl.program_id(1)

        @pl.when(j == 0)
        def _():
            xv = h_ref[...]
            r = lax.rsqrt(jnp.mean(xv * xv, axis=-1, keepdims=True) + RMS_EPS)
            u = (xv * r * g_ref[...]).astype(BF16)
            u_s[...] = u
            u_ref[...] = u

        u = u_s[...]
        gate = jnp.dot(u, wg_ref[...], preferred_element_type=F32)
        up = jnp.dot(u, wu_ref[...], preferred_element_type=F32)
        gate_ref[...] = gate.astype(BF16)
        up_ref[...] = up.astype(BF16)
        a = (gate * _sigmoid(gate) * up).astype(BF16)
        a_ref[...] = a
        p = jnp.dot(a, wd_ref[...], preferred_element_type=F32)

        @pl.when(j == 0)
        def _():
            acc[...] = p

        @pl.when(j > 0)
        def _():
            acc[...] += p

        @pl.when(j == nj - 1)
        def _():
            h2_ref[...] = h_ref[...] + 0.5 * acc[...]

    row = pl.BlockSpec((tm, D), lambda i, j: (i, 0))
    chunk = pl.BlockSpec((tm, tf), lambda i, j: (i, j))
    hidden = jax.ShapeDtypeStruct((S, D_FF), BF16)
    return pl.pallas_call(
        body, name=name, grid=(S // tm, nj),
        in_specs=[row, pl.BlockSpec((1, D), lambda i, j: (0, 0)),
                  pl.BlockSpec((None, D, tf), lambda i, j: (layer, 0, j)),
                  pl.BlockSpec((None, D, tf), lambda i, j: (layer, 0, nj + j)),
                  pl.BlockSpec((None, tf, D), lambda i, j: (layer, j, 0))],
        out_specs=[row, row, chunk, chunk, chunk],
        out_shape=[jax.ShapeDtypeStruct((S, D), F32), jax.ShapeDtypeStruct((S, D), BF16), hidden, hidden, hidden],
        scratch_shapes=[pltpu.VMEM((tm, D), BF16), pltpu.VMEM((tm, D), F32)],
        compiler_params=_cparams(("parallel", "arbitrary")),
    )(h, g, w_gu, w_gu, w_down)


def _ffn_bwd_fused(dh, h, g, gate, up, w_gu, w_down, layer, *, name):
    S, D = h.shape
    tm = _div_tile(S, 512, FFN_BWD_SLAB)
    tf = FFN_CHUNK
    nj = D_FF // tf
    slab = min(FFN_BWD_SLAB, tm)
    nt = (((1,), (1,)), ((), ()))
    pieces = [(c0, min(FFN_BWD_PIECE, tf - c0)) for c0 in range(0, tf, FFN_BWD_PIECE)]

    def body(dh_ref, h_ref, g_ref, gate_ref, up_ref, wg_ref, wu_ref, wd_ref, dx_ref, dg_ref, dgate_ref, dup_ref,
             dy_s, acc):
        i = pl.program_id(0)
        j = pl.program_id(1)

        @pl.when(j == 0)
        def _():
            for r0 in range(0, tm, slab):
                rows = pl.ds(r0, slab)
                dy_s[rows, :] = (0.5 * dh_ref[rows, :]).astype(BF16)

        p = None
        for c0, cw in pieces:
            cols = pl.ds(c0, cw)
            da = lax.dot_general(dy_s[...], wd_ref[cols, :], nt, preferred_element_type=F32)
            gt = gate_ref[:, cols].astype(F32)
            sg = _sigmoid(gt)
            dgate = (da * up_ref[:, cols].astype(F32) * sg * (1.0 + gt * (1.0 - sg))).astype(BF16)
            dup = (da * gt * sg).astype(BF16)
            dgate_ref[:, cols] = dgate
            dup_ref[:, cols] = dup
            q = (lax.dot_general(dgate, wg_ref[:, cols], nt, preferred_element_type=F32)
                 + lax.dot_general(dup, wu_ref[:, cols], nt, preferred_element_type=F32))
            p = q if p is None else p + q

        @pl.when(j == 0)
        def _():
            acc[...] = p

        @pl.when(j > 0)
        def _():
            acc[...] += p

        @pl.when(j == nj - 1)
        def _():
            part = jnp.zeros((1, D), F32)
            for r0 in range(0, tm, slab):
                rows = pl.ds(r0, slab)
                xv = h_ref[rows, :]
                du = acc[rows, :]
                r = lax.rsqrt(jnp.mean(xv * xv, axis=-1, keepdims=True) + RMS_EPS)
                xhat = xv * r
                part = part + jnp.sum(du * xhat, axis=0, keepdims=True)
                dxhat = du * g_ref[...]
                dx_ref[rows, :] = dh_ref[rows, :] + r * (
                    dxhat - xhat * jnp.mean(dxhat * xhat, axis=-1, keepdims=True))

            @pl.when(i == 0)
            def _():
                dg_ref[...] = part

            @pl.when(i > 0)
            def _():
                dg_ref[...] += part

    row = pl.BlockSpec((tm, D), lambda i, j: (i, 0))
    vec = pl.BlockSpec((1, D), lambda i, j: (0, 0))
    chunk = pl.BlockSpec((tm, tf), lambda i, j: (i, j))
    hidden = jax.ShapeDtypeStruct((S, D_FF), BF16)
    return pl.pallas_call(
        body, name=name, grid=(S // tm, nj),
        in_specs=[row, row, vec, chunk, chunk,
                  pl.BlockSpec((None, D, tf), lambda i, j: (layer, 0, j)),
                  pl.BlockSpec((None, D, tf), lambda i, j: (layer, 0, nj + j)),
                  pl.BlockSpec((None, tf, D), lambda i, j: (layer, j, 0))],
        out_specs=[row, vec, chunk, chunk],
        out_shape=[jax.ShapeDtypeStruct((S, D), F32), jax.ShapeDtypeStruct((1, D), F32), hidden, hidden],
        scratch_shapes=[pltpu.VMEM((tm, D), BF16), pltpu.VMEM((tm, D), F32)],
        compiler_params=_cparams(("arbitrary", "arbitrary")),
    )(dh, h, g, gate, up, w_gu, w_gu, w_down)


CONV_HALO = 32
CONV_SUB_ROWS = 128


def _conv_a_fwd(z, w, bias, ln_g, ln_b, *, name):
    S = z.shape[0]
    C = CONV_A_CH
    ts = _div_tile(S, 256, 32)

    def body(val_ref, gate_ref, w_ref, b_ref, g_ref, lb_ref, c_ref, act_ref, win):
        i = pl.program_id(0)

        @pl.when(i == 0)
        def _():
            win[pl.ds(0, CONV_HALO), :] = jnp.zeros((CONV_HALO, C), F32)

        @pl.when(i > 0)
        def _():
            win[pl.ds(0, CONV_HALO), :] = win[pl.ds(ts, CONV_HALO), :]

        a = val_ref[...].astype(F32) * _sigmoid(gate_ref[...].astype(F32))
        win[pl.ds(CONV_HALO, ts), :] = a
        rs = min(CONV_SUB_ROWS, ts)
        for cb in range(C // 128):
            lanes = pl.ds(128 * cb, 128)
            for rt in range(ts // rs):
                sub = jnp.broadcast_to(b_ref[:, lanes], (rs, 128))
                for k in range(CONV_A_WIDTH):
                    sub = sub + w_ref[pl.ds(k, 1), lanes] * win[
                        pl.ds(CONV_HALO - (CONV_A_WIDTH - 1) + k + rs * rt, rs), lanes]
                c_ref[pl.ds(rs * rt, rs), lanes] = sub
        acc = c_ref[...]
        mu = jnp.mean(acc, axis=-1, keepdims=True)
        xc = acc - mu
        var = jnp.mean(xc * xc, axis=-1, keepdims=True)
        ln = xc * lax.rsqrt(var + LN_EPS) * g_ref[...] + lb_ref[...]
        act_ref[...] = (ln * _sigmoid(ln)).astype(BF16)

    row = lambda col: pl.BlockSpec((ts, C), lambda i, col=col: (i, col))
    vec = pl.BlockSpec((1, C), lambda i: (0, 0))
    return pl.pallas_call(
        body, name=name, grid=(S // ts,),
        in_specs=[row(0), row(1), pl.BlockSpec((32, C), lambda i: (0, 0)), vec, vec, vec],
        out_specs=[row(0), row(0)],
        out_shape=[jax.ShapeDtypeStruct((S, C), F32), jax.ShapeDtypeStruct((S, C), BF16)],
        scratch_shapes=[pltpu.VMEM((ts + CONV_HALO, C), F32)],
        compiler_params=_cparams(("arbitrary",)),
    )(z, z, w, bias, ln_g, ln_b)


def _conv_a_bwd(z, c, dcat, w, ln_g, ln_b, *, name):
    S = z.shape[0]
    C = CONV_A_CH
    ts = _div_tile(S, 256, 32)
    n = S // ts

    def body(val_ref, gate_ref, c_ref, da_ref, w_ref, g_ref, lb_ref, dz_ref, small_ref, win, a_s, da_s, dw8):
        i = pl.program_id(0)

        @pl.when(i == 0)
        def _():
            win[pl.ds(ts, CONV_HALO), :] = jnp.zeros((CONV_HALO, C), F32)
            small_ref[...] = jnp.zeros(small_ref.shape, F32)
            dw8[...] = jnp.zeros(dw8.shape, F32)

        @pl.when(i > 0)
        def _():
            win[pl.ds(ts, CONV_HALO), :] = win[pl.ds(0, CONV_HALO), :]

        cv = c_ref[...]
        gv = g_ref[...]
        mu = jnp.mean(cv, axis=-1, keepdims=True)
        xc = cv - mu
        var = jnp.mean(xc * xc, axis=-1, keepdims=True)
        rstd = lax.rsqrt(var + LN_EPS)
        xhat = xc * rstd
        ln = xhat * gv + lb_ref[...]
        sg = _sigmoid(ln)
        dln = da_ref[...].astype(F32) * (sg * (1.0 + ln * (1.0 - sg)))
        small_ref[pl.ds(33, 1), :] += jnp.sum(dln * xhat, axis=0, keepdims=True)
        small_ref[pl.ds(34, 1), :] += jnp.sum(dln, axis=0, keepdims=True)
        dxhat = dln * gv
        dc = rstd * (dxhat - jnp.mean(dxhat, axis=-1, keepdims=True)
                     - xhat * jnp.mean(dxhat * xhat, axis=-1, keepdims=True))
        small_ref[pl.ds(32, 1), :] += jnp.sum(dc, axis=0, keepdims=True)
        win[pl.ds(0, ts), :] = dc

        val = val_ref[...].astype(F32)
        sgg = _sigmoid(gate_ref[...].astype(F32))
        a_s[...] = val * sgg
        rs = min(CONV_SUB_ROWS, ts)
        for cb in range(C // 128):
            lanes = pl.ds(128 * cb, 128)
            for rt in range(ts // rs):
                a_sub = a_s[pl.ds(rs * rt, rs), lanes]
                da = jnp.zeros((rs, 128), F32)
                for k in range(CONV_A_WIDTH):
                    sh = win[pl.ds(CONV_A_WIDTH - 1 - k + rs * rt, rs), lanes]
                    da = da + w_ref[pl.ds(k, 1), lanes] * sh
                    prod = a_sub * sh
                    part = prod[0:8]
                    for r in range(1, rs // 8):
                        part = part + prod[8 * r:8 * r + 8]
                    dw8[pl.ds(8 * k, 8), lanes] += part
                da_s[pl.ds(rs * rt, rs), lanes] = da
        da = da_s[...]
        dz_ref[:, pl.ds(0, C)] = (da * sgg).astype(BF16)
        dz_ref[:, pl.ds(C, C)] = (da * val * sgg * (1.0 - sgg)).astype(BF16)

        @pl.when(i == n - 1)
        def _():
            for k in range(CONV_A_WIDTH):
                small_ref[pl.ds(k, 1), :] = jnp.sum(dw8[pl.ds(8 * k, 8), :], axis=0, keepdims=True)

    row = lambda col: pl.BlockSpec((ts, C), lambda i, col=col: (n - 1 - i, col))
    vec = pl.BlockSpec((1, C), lambda i: (0, 0))
    return pl.pallas_call(
        body, name=name, grid=(n,),
        in_specs=[row(0), row(1), row(0), row(0), pl.BlockSpec((32, C), lambda i: (0, 0)), vec, vec],
        out_specs=[pl.BlockSpec((ts, 2 * C), lambda i: (n - 1 - i, 0)), pl.BlockSpec((40, C), lambda i: (0, 0))],
        out_shape=[jax.ShapeDtypeStruct((S, 2 * C), BF16), jax.ShapeDtypeStruct((40, C), F32)],
        scratch_shapes=[pltpu.VMEM((ts + CONV_HALO, C), F32), pltpu.VMEM((ts, C), F32), pltpu.VMEM((ts, C), F32),
                        pltpu.VMEM((8 * 32, C), F32)],
        compiler_params=_cparams(("arbitrary",)),
    )(z, z, c, dcat, w, ln_g, ln_b)


SC_HALO = 8


def _sconv_fwd(z, w, *, name):
    S = z.shape[0]
    C = SC_CH
    ts = _div_tile(S, 256, 16)

    def body(gb_ref, gc_ref, v_ref, w_ref, y_ref, cc_ref, win):
        i = pl.program_id(0)

        @pl.when(i == 0)
        def _():
            win[pl.ds(0, SC_HALO), :] = jnp.zeros((SC_HALO, C), F32)

        @pl.when(i > 0)
        def _():
            win[pl.ds(0, SC_HALO), :] = win[pl.ds(ts, SC_HALO), :]

        win[pl.ds(SC_HALO, ts), :] = gc_ref[...].astype(F32) * v_ref[...].astype(F32)
        acc = jnp.zeros((ts, C), F32)
        for k in range(SC_WIDTH):
            acc = acc + w_ref[pl.ds(k, 1), :] * win[pl.ds(SC_HALO - (SC_WIDTH - 1) + k, ts), :]
        cc_ref[...] = acc.astype(BF16)
        y_ref[...] = (gb_ref[...].astype(F32) * acc).astype(BF16)

    row = lambda col: pl.BlockSpec((ts, C), lambda i, col=col: (i, col))
    return pl.pallas_call(
        body, name=name, grid=(S // ts,),
        in_specs=[row(0), row(1), row(2), pl.BlockSpec((8, C), lambda i: (0, 0))],
        out_specs=[row(0), row(0)],
        out_shape=[jax.ShapeDtypeStruct((S, C), BF16), jax.ShapeDtypeStruct((S, C), BF16)],
        scratch_shapes=[pltpu.VMEM((ts + SC_HALO, C), F32)],
        compiler_params=_cparams(("arbitrary",)),
    )(z, z, z, w)


def _sconv_bwd(z, cc, dy, w, *, name):
    S = z.shape[0]
    C = SC_CH
    ts = _div_tile(S, 256, 16)
    n = S // ts

    def body(gb_ref, gc_ref, v_ref, cc_ref, dy_ref, w_ref, dz_ref, dw_ref, win):
        i = pl.program_id(0)

        @pl.when(i == 0)
        def _():
            win[pl.ds(ts, SC_HALO), :] = jnp.zeros((SC_HALO, C), F32)
            dw_ref[...] = jnp.zeros(dw_ref.shape, F32)

        @pl.when(i > 0)
        def _():
            win[pl.ds(ts, SC_HALO), :] = win[pl.ds(0, SC_HALO), :]

        dyv = dy_ref[...].astype(F32)
        gb = gb_ref[...].astype(F32)
        gc = gc_ref[...].astype(F32)
        val = v_ref[...].astype(F32)
        dz_ref[:, pl.ds(0, C)] = (dyv * cc_ref[...].astype(F32)).astype(BF16)
        win[pl.ds(0, ts), :] = dyv * gb
        cv = gc * val
        dcv = jnp.zeros((ts, C), F32)
        for k in range(SC_WIDTH):
            sh = win[pl.ds(SC_WIDTH - 1 - k, ts), :]
            dcv = dcv + w_ref[pl.ds(k, 1), :] * sh
            dw_ref[pl.ds(k, 1), :] += jnp.sum(cv * sh, axis=0, keepdims=True)
        dz_ref[:, pl.ds(C, C)] = (dcv * val).astype(BF16)
        dz_ref[:, pl.ds(2 * C, C)] = (dcv * gc).astype(BF16)

    row = lambda col: pl.BlockSpec((ts, C), lambda i, col=col: (n - 1 - i, col))
    return pl.pallas_call(
        body, name=name, grid=(n,),
        in_specs=[row(0), row(1), row(2), row(0), row(0), pl.BlockSpec((8, C), lambda i: (0, 0))],
        out_specs=[pl.BlockSpec((ts, 3 * C), lambda i: (n - 1 - i, 0)), pl.BlockSpec((8, C), lambda i: (0, 0))],
        out_shape=[jax.ShapeDtypeStruct((S, 3 * C), BF16), jax.ShapeDtypeStruct((8, C), F32)],
        scratch_shapes=[pltpu.VMEM((ts + SC_HALO, C), F32)],
        compiler_params=_cparams(("arbitrary",)),
    )(z, z, z, cc, dy, w)


SWA_Q_COL = 2
SWA_SLOPES = [2.0 ** (-8.0 * (h + 1) / SWA_HEADS) for h in range(SWA_HEADS)]
SWA_SCALE = HEAD_DIM ** -0.5


SWA_GROUP_ROWS = SWA_GROUP * WINDOW


def _swa_masks():
    shape = (SWA_GROUP_ROWS, 2 * WINDOW)
    ii = lax.broadcasted_iota(jnp.int32, shape, 0)
    jj = lax.broadcasted_iota(jnp.int32, shape, 1)
    dist = (ii & (WINDOW - 1)) + WINDOW - jj
    valid = (dist >= 0) & (dist < WINDOW)
    grp = lax.broadcasted_iota(jnp.int32, (SWA_GROUP_ROWS, 1), 0) // WINDOW
    return dist.astype(F32), valid, jj, grp


def _by_group(grp, vals):
    out = jnp.full(grp.shape, vals[SWA_GROUP - 1], F32)
    for g in range(SWA_GROUP - 2, -1, -1):
        out = jnp.where(grp == g, vals[g], out)
    return out


def _stack_heads(ref, rows, kv):
    return jnp.concatenate([ref[rows, pl.ds(HEAD_DIM * (kv * SWA_GROUP + g), HEAD_DIM)] for g in range(SWA_GROUP)],
                           axis=0)


def _swa_probs(qg, kk, sink, slope, distf, valid):
    s = lax.dot_general(qg, kk, (((1,), (1,)), ((), ())), preferred_element_type=F32) * SWA_SCALE
    s = s - slope * distf
    s = jnp.where(valid, s, NEG_BIG)
    m = jnp.maximum(jnp.max(s, axis=-1, keepdims=True), sink)
    p = jnp.exp(s - m)
    l = jnp.sum(p, axis=-1, keepdims=True) + jnp.exp(sink - m)
    return p, m, l


def _swa_fwd(z, kpad, vpad, sinks, *, name):
    S = z.shape[0]
    tq = _div_tile(S, 256, 128)
    nblk = tq // WINDOW
    W = WINDOW

    def body(sink_ref, q_ref, k_ref, v_ref, o_ref):
        i = pl.program_id(0)
        distf, valid0, jj, grp = _swa_masks()
        for kv in range(SWA_KV_HEADS):
            heads = range(kv * SWA_GROUP, (kv + 1) * SWA_GROUP)
            sink = _by_group(grp, [sink_ref[h] for h in heads])
            slope = _by_group(grp, [SWA_SLOPES[h] for h in heads])
            for b in range(nblk):
                nb = i * nblk + b
                start = pl.multiple_of(nb * W, W)
                rows = pl.ds(W * b, W)
                valid = valid0 & ((jj >= W) | (nb > 0))
                kk = k_ref[pl.ds(start, 2 * W), pl.ds(HEAD_DIM * kv, HEAD_DIM)]
                vv = v_ref[pl.ds(start, 2 * W), pl.ds(HEAD_DIM * kv, HEAD_DIM)]
                p, m, l = _swa_probs(_stack_heads(q_ref, rows, kv), kk, sink, slope, distf, valid)
                o = (jnp.dot(p.astype(BF16), vv, preferred_element_type=F32) / l).astype(BF16)
                for g, h in enumerate(heads):
                    o_ref[rows, pl.ds(HEAD_DIM * h, HEAD_DIM)] = o[W * g:W * (g + 1)]

    full = pl.BlockSpec((S + W, 2 * HEAD_DIM), lambda i: (0, 0))
    return pl.pallas_call(
        body, name=name, grid=(S // tq,),
        in_specs=[pl.BlockSpec(memory_space=pltpu.SMEM), pl.BlockSpec((tq, 512), lambda i: (i, SWA_Q_COL)), full, full],
        out_specs=pl.BlockSpec((tq, 512), lambda i: (i, 0)),
        out_shape=jax.ShapeDtypeStruct((S, 512), BF16),
        compiler_params=_cparams(("parallel",)),
    )(sinks, z, kpad, vpad)


def _swa_bwd(z, kpad, vpad, sinks, dcat, *, name):
    S = z.shape[0]
    tq = _div_tile(S, 256, 128)
    nblk = tq // WINDOW
    W = WINDOW

    def body(sink_ref, q_ref, k_ref, v_ref, do_ref, dq_ref, dk_ref, dv_ref, ds_ref):
        i = pl.program_id(0)

        @pl.when(i == 0)
        def _():
            dk_ref[...] = jnp.zeros(dk_ref.shape, F32)
            dv_ref[...] = jnp.zeros(dv_ref.shape, F32)
            ds_ref[...] = jnp.zeros(ds_ref.shape, F32)

        distf, valid0, jj, grp = _swa_masks()
        tn = (((0,), (0,)), ((), ()))
        for kv in range(SWA_KV_HEADS):
            heads = range(kv * SWA_GROUP, (kv + 1) * SWA_GROUP)
            sink = _by_group(grp, [sink_ref[h] for h in heads])
            slope = _by_group(grp, [SWA_SLOPES[h] for h in heads])
            for b in range(nblk):
                nb = i * nblk + b
                start = pl.multiple_of(nb * W, W)
                rows = pl.ds(W * b, W)
                valid = valid0 & ((jj >= W) | (nb > 0))
                kk = k_ref[pl.ds(start, 2 * W), pl.ds(HEAD_DIM * kv, HEAD_DIM)]
                vv = v_ref[pl.ds(start, 2 * W), pl.ds(HEAD_DIM * kv, HEAD_DIM)]
                qg = _stack_heads(q_ref, rows, kv)
                dog = _stack_heads(do_ref, rows, kv)
                p, m, l = _swa_probs(qg, kk, sink, slope, distf, valid)
                inv_l = 1.0 / l
                pn = p * inv_l
                dp = lax.dot_general(dog, vv, (((1,), (1,)), ((), ())), preferred_element_type=F32)
                delta = jnp.sum(pn * dp, axis=-1, keepdims=True)
                dsc = (pn * (dp - delta)).astype(BF16)
                dsink = jnp.exp(sink - m) * inv_l * delta
                dq = (jnp.dot(dsc, kk, preferred_element_type=F32) * SWA_SCALE).astype(BF16)
                for g, h in enumerate(heads):
                    ds_ref[pl.ds(h, 1), :] += jnp.broadcast_to(
                        -jnp.sum(dsink[W * g:W * (g + 1)], axis=0, keepdims=True), (1, 128))
                    dq_ref[rows, pl.ds(HEAD_DIM * h, HEAD_DIM)] = dq[W * g:W * (g + 1)]
                dk_ref[pl.ds(start, 2 * W), pl.ds(HEAD_DIM * kv, HEAD_DIM)] += lax.dot_general(
                    dsc, qg, tn, preferred_element_type=F32) * SWA_SCALE
                dv_ref[pl.ds(start, 2 * W), pl.ds(HEAD_DIM * kv, HEAD_DIM)] += lax.dot_general(
                    pn.astype(BF16), dog, tn, preferred_element_type=F32)

    full = pl.BlockSpec((S + W, 2 * HEAD_DIM), lambda i: (0, 0))
    return pl.pallas_call(
        body, name=name, grid=(S // tq,),
        in_specs=[pl.BlockSpec(memory_space=pltpu.SMEM), pl.BlockSpec((tq, 512), lambda i: (i, SWA_Q_COL)), full, full,
                  pl.BlockSpec((tq, 512), lambda i: (i, 1))],
        out_specs=[pl.BlockSpec((tq, 512), lambda i: (i, 0)), full, full, pl.BlockSpec((8, 128), lambda i: (0, 0))],
        out_shape=[jax.ShapeDtypeStruct((S, 512), BF16), jax.ShapeDtypeStruct((S + W, 2 * HEAD_DIM), F32),
                   jax.ShapeDtypeStruct((S + W, 2 * HEAD_DIM), F32), jax.ShapeDtypeStruct((8, 128), F32)],
        compiler_params=_cparams(("arbitrary",)),
    )(sinks, z, kpad, vpad, dcat)


XA_SCALE = XA_HEAD_DIM ** -0.5


def _xa_probs(qh, kh):
    s = lax.dot_general(qh, kh, (((1,), (1,)), ((), ())), preferred_element_type=F32) * XA_SCALE
    m = jnp.max(s, axis=-1, keepdims=True)
    p = jnp.exp(s - m)
    return p, jnp.sum(p, axis=-1, keepdims=True)


def _xa_fwd(q, kv, *, name):
    S, D = q.shape
    M = kv.shape[0]
    ts = _div_tile(S, 512, 16)
    HD = XA_HEAD_DIM

    def body(q_ref, k_ref, v_ref, o_ref):
        for h in range(XA_HEADS):
            qh = q_ref[:, pl.ds(HD * h, HD)]
            p, l = _xa_probs(qh, k_ref[:, pl.ds(HD * h, HD)])
            o = jnp.dot(p.astype(BF16), v_ref[:, pl.ds(HD * h, HD)], preferred_element_type=F32) / l
            o_ref[:, pl.ds(HD * h, HD)] = o.astype(BF16)

    return pl.pallas_call(
        body, name=name, grid=(S // ts,),
        in_specs=[pl.BlockSpec((ts, D), lambda i: (i, 0)), pl.BlockSpec((M, D), lambda i: (0, 0)),
                  pl.BlockSpec((M, D), lambda i: (0, 1))],
        out_specs=pl.BlockSpec((ts, D), lambda i: (i, 0)),
        out_shape=jax.ShapeDtypeStruct((S, D), BF16),
        compiler_params=_cparams(("parallel",)),
    )(q, kv, kv)


def _xa_bwd(q, kv, do, *, name):
    S, D = q.shape
    M = kv.shape[0]
    ts = _div_tile(S, 512, 16)
    HD = XA_HEAD_DIM

    def body(q_ref, k_ref, v_ref, do_ref, dq_ref, dkv_ref):
        i = pl.program_id(0)

        @pl.when(i == 0)
        def _():
            dkv_ref[...] = jnp.zeros(dkv_ref.shape, F32)

        for h in range(XA_HEADS):
            qh = q_ref[:, pl.ds(HD * h, HD)]
            kh = k_ref[:, pl.ds(HD * h, HD)]
            vh = v_ref[:, pl.ds(HD * h, HD)]
            doh = do_ref[:, pl.ds(HD * h, HD)]
            p, l = _xa_probs(qh, kh)
            pn = p * (1.0 / l)
            dp = lax.dot_general(doh, vh, (((1,), (1,)), ((), ())), preferred_element_type=F32)
            delta = jnp.sum(pn * dp, axis=-1, keepdims=True)
            dsc = (pn * (dp - delta)).astype(BF16)
            dq_ref[:, pl.ds(HD * h, HD)] = (jnp.dot(dsc, kh, preferred_element_type=F32) * XA_SCALE).astype(BF16)
            dkv_ref[:, pl.ds(HD * h, HD)] += lax.dot_general(
                dsc, qh, (((0,), (0,)), ((), ())), preferred_element_type=F32) * XA_SCALE
            dkv_ref[:, pl.ds(D + HD * h, HD)] += lax.dot_general(
                pn.astype(BF16), doh, (((0,), (0,)), ((), ())), preferred_element_type=F32)

    row = pl.BlockSpec((ts, D), lambda i: (i, 0))
    return pl.pallas_call(
        body, name=name, grid=(S // ts,),
        in_specs=[row, pl.BlockSpec((M, D), lambda i: (0, 0)), pl.BlockSpec((M, D), lambda i: (0, 1)), row],
        out_specs=[row, pl.BlockSpec((M, 2 * D), lambda i: (0, 0))],
        out_shape=[jax.ShapeDtypeStruct((S, D), BF16), jax.ShapeDtypeStruct((M, 2 * D), F32)],
        compiler_params=_cparams(("arbitrary",)),
    )(q, kv, kv, do)


def _adam_math(w, g, m, v):
    m = ADAM_B1 * m + (1.0 - ADAM_B1) * g
    v = ADAM_B2 * v + (1.0 - ADAM_B2) * (g * g)
    m_hat = m / (1.0 - ADAM_B1 ** ADAM_STEP)
    v_hat = v / (1.0 - ADAM_B2 ** ADAM_STEP)
    delta = -ADAM_LR * (m_hat / (jnp.sqrt(v_hat) + ADAM_EPS) + ADAM_WD * w)
    return delta, m, v


def _adamw_layers(w, m, v, gsrc, *, name):
    L, A, B = w.shape
    tr = _div_tile(A, max(8, ADAM_TILE_ELEMS // B // 8 * 8))
    nt = A // tr
    flat = [a for srcs in gsrc for a in srcs]
    owner = [l for l, srcs in enumerate(gsrc) for _ in srcs]
    ng = len(flat)

    def body(*refs):
        w_ref, m_ref, v_ref = refs[:3]
        g_refs = refs[3:3 + ng]
        g_ref, d_ref, nm_ref, nv_ref = refs[3 + ng:]
        layer = pl.program_id(0)
        g = None
        for l in range(L):
            gl = None
            for a_ref, o in zip(g_refs, owner):
                if o == l:
                    gl = a_ref[...] if gl is None else gl + a_ref[...]
            g = gl if g is None else jnp.where(layer == l, gl, g)
        d, nm, nv = _adam_math(w_ref[...], g, m_ref[...], v_ref[...])
        g_ref[...] = g
        d_ref[...] = d
        nm_ref[...] = nm
        nv_ref[...] = nv

    def src_spec(o):
        return pl.BlockSpec((None, tr, B),
                            lambda l, i: (0, jnp.where(l == o, i, jnp.where(l > o, nt - 1, 0)), 0))

    spec = pl.BlockSpec((None, tr, B), lambda l, i: (l, i, 0))
    sds = jax.ShapeDtypeStruct((L, A, B), F32)
    return pl.pallas_call(
        body, name=name, grid=(L, nt), in_specs=[spec] * 3 + [src_spec(o) for o in owner], out_specs=[spec] * 4,
        out_shape=[sds] * 4, compiler_params=_cparams(("arbitrary", "arbitrary")),
    )(w, m, v, *flat)


def _adamw_small(ws, ms, vs, gparts, *, name):
    n = len(ws)
    R = gparts.shape[1]

    def body(*refs):
        w_refs, m_refs, v_refs = refs[:n], refs[n:2 * n], refs[2 * n:3 * n]
        gp_ref = refs[3 * n]
        outs = refs[3 * n + 1:7 * n + 1]
        packed = refs[7 * n + 1]
        g = gp_ref[0]
        for k in range(1, N_DEV):
            g = g + gp_ref[k]
        packed[...] = g
        row = 0
        for p in range(n):
            r, c = ws[p].shape
            per, lanes = max(c // 128, 1), min(c, 128)
            g_ref = outs[p]
            for i in range(r):
                for k in range(per):
                    g_ref[pl.ds(i, 1), pl.ds(128 * k, lanes)] = packed[pl.ds(row, 1), pl.ds(0, lanes)]
                    row += 1
            d, nm, nv = _adam_math(w_refs[p][...], g_ref[...], m_refs[p][...], v_refs[p][...])
            outs[n + p][...] = d
            outs[2 * n + p][...] = nm
            outs[3 * n + p][...] = nv

    shapes = [jax.ShapeDtypeStruct(a.shape, F32) for a in ws]
    res = pl.pallas_call(body, name=name, out_shape=shapes * 4, scratch_shapes=[pltpu.VMEM((R, 128), F32)],
                         compiler_params=_cparams())(*ws, *ms, *vs, gparts)
    return res[:n], res[n:2 * n], res[2 * n:3 * n], res[3 * n:]


ANY = pl.BlockSpec(memory_space=pl.ANY)


def _mesh_pos():
    return lax.axis_index("x"), lax.axis_index("y"), lax.axis_index("c")


def _other_chips(x, y):
    return [(1 - x, y), (x, 1 - y), (1 - x, 1 - y)]


LAYOUT = {'ffn1_w_gu': 'col', 'ffn1_w_down': 'stk', 'even_w_in': 'stk', 'even_w_out': 'stk', 'odd_w_in': 'col',
          'odd_w_out': 'stk', 'xa_wq': 'stk', 'xa_wkv': 'col', 'xa_wo': 'stk', 'ffn2_w_gu': 'col',
          'ffn2_w_down': 'stk', 'tiny': 'stk'}
COMM_NAMES = list(LAYOUT)
TINY_ROWS = 48


def _gathered_piece(ref, kind, L, A, h):
    if L == 2:
        return ref.at[h]
    rows = pl.ds(pl.multiple_of(h * (A // 2), 8), A // 2)
    return ref.at[0, rows] if kind == 'col' else ref.at[0, :, rows]


def _chip_part(piece, kind, B, s):
    if kind == 'col':
        return piece.at[:, pl.ds(pl.multiple_of(s * B, 128), B)]
    return piece.at[s]


def _place(shard, layer, kind, chip_idx, out_dtype, *, name):
    L, A, B = shard.shape
    ta = _div_tile(A, 256, 16)

    def body(s_ref, x_ref, o_ref):
        o_ref[...] = x_ref[...].astype(out_dtype)

    if kind == 'col':
        shape = (1, A, N_CHIPS * B)
        out_spec = pl.BlockSpec((None, ta, B), lambda i, s: (0, i, s[0]))
    else:
        shape = (1, N_CHIPS, A, B)
        out_spec = pl.BlockSpec((None, None, ta, B), lambda i, s: (0, s[0], i, 0))
    grid_spec = pltpu.PrefetchScalarGridSpec(
        num_scalar_prefetch=1, grid=(A // ta,),
        in_specs=[pl.BlockSpec((None, ta, B), lambda i, s: (layer, i, 0))], out_specs=out_spec)
    return pl.pallas_call(
        body, name=name, grid_spec=grid_spec, out_shape=jax.ShapeDtypeStruct(shape, out_dtype),
        compiler_params=_cparams(("parallel",)),
    )(chip_idx, shard)


HBM = pl.BlockSpec(memory_space=pltpu.HBM)
SEM = pl.BlockSpec(memory_space=pltpu.SEMAPHORE)
DATAFLOW = pltpu.SideEffectType.DATAFLOW_SIDE_EFFECTING


def _own_part_copies(refs, meta, send_sems, recv_sems):
    x, y, c = _mesh_pos()
    cps = []
    for k, (kind, L, A, B) in enumerate(meta):
        for j, (cx, cy) in enumerate(_other_chips(x, y)):
            part = _chip_part(refs[k].at[0], kind, B, 2 * x + y)
            cps.append(pltpu.make_async_remote_copy(
                src_ref=part, dst_ref=part, send_sem=send_sems.at[3 * k + j], recv_sem=recv_sems.at[3 * k + j],
                device_id=(cx, cy, c), device_id_type=MESH))
    return cps


def _gather_start(fulls, meta, after, tag):
    n = len(fulls)

    def body(*refs):
        send_sems, recv_sems = refs[n + 1], refs[n + 2]
        outs = refs[n + 3:2 * n + 3]
        token = refs[2 * n + 3]
        for cp in _own_part_copies(outs, meta, send_sems, recv_sems):
            cp.start()
        token[...] = jnp.zeros_like(token)

    res = pl.pallas_call(
        body, name=f"ag_start_{tag}", in_specs=[HBM] * n + [pl.BlockSpec(memory_space=pl.ANY)],
        out_specs=(SEM, SEM) + (HBM,) * n + (pl.BlockSpec(memory_space=pltpu.VMEM),),
        out_shape=(pltpu.SemaphoreType.DMA((3 * n,)), pltpu.SemaphoreType.DMA((3 * n,)))
        + tuple(pltpu.HBM(f.shape, f.dtype) for f in fulls) + (jax.ShapeDtypeStruct((8, 128), F32),),
        input_output_aliases={k: 2 + k for k in range(n)},
        compiler_params=pltpu.CompilerParams(has_side_effects=DATAFLOW),
    )(*[pltpu.with_memory_space_constraint(f, pltpu.HBM) for f in fulls], after)
    return res[0], res[1], list(res[2:2 + n]), res[2 + n]


def _gather_wait(send_sems, recv_sems, fulls, meta, after, tag):
    n = len(fulls)

    def body(*refs):
        f_refs = refs[:n]
        send_sems, recv_sems = refs[n], refs[n + 1]
        for cp in _own_part_copies(f_refs, meta, send_sems, recv_sems):
            cp.wait_send()
            cp.wait_recv()

    return pl.pallas_call(
        body, name=f"ag_wait_{tag}", in_specs=[HBM] * n + [SEM, SEM, pl.BlockSpec(memory_space=pl.ANY)],
        out_specs=[HBM] * n, out_shape=[pltpu.HBM(f.shape, f.dtype) for f in fulls],
        input_output_aliases={k: k for k in range(n)},
        compiler_params=pltpu.CompilerParams(has_side_effects=DATAFLOW),
    )(*fulls, send_sems, recv_sems, after)


def _scatter_copies(g_refs, land_refs, meta, send_sems, recv_sems):
    x, y, c = _mesh_pos()
    cps = []
    for k, (kind, L, A, B) in enumerate(meta):
        for j, (cx, cy) in enumerate(_other_chips(x, y)):
            cps.append(pltpu.make_async_remote_copy(
                src_ref=_chip_part(g_refs[k].at[0], kind, B, 2 * cx + cy), dst_ref=land_refs[k].at[j],
                send_sem=send_sems.at[3 * k + j], recv_sem=recv_sems.at[3 * k + j], device_id=(cx, cy, c),
                device_id_type=MESH))
    return cps


def _scatter_start(gs, meta, after, tag):
    n = len(gs)

    def body(*refs):
        send_sems, recv_sems = refs[2 * n + 1], refs[2 * n + 2]
        g_out = refs[2 * n + 3:3 * n + 3]
        lands = refs[3 * n + 3:4 * n + 3]
        token = refs[4 * n + 3]
        for cp in _scatter_copies(g_out, lands, meta, send_sems, recv_sems):
            cp.start()
        token[...] = jnp.zeros_like(token)

    land_shapes = [(3, A, B) for kind, L, A, B in meta]
    lands = [pltpu.with_memory_space_constraint(lax.empty(s, g.dtype), pltpu.HBM) for s, g in zip(land_shapes, gs)]
    res = pl.pallas_call(
        body, name=f"rs_start_{tag}", in_specs=[HBM] * (2 * n) + [pl.BlockSpec(memory_space=pl.ANY)],
        out_specs=(SEM, SEM) + (HBM,) * (2 * n) + (pl.BlockSpec(memory_space=pltpu.VMEM),),
        out_shape=(pltpu.SemaphoreType.DMA((3 * n,)), pltpu.SemaphoreType.DMA((3 * n,)))
        + tuple(pltpu.HBM(g.shape, g.dtype) for g in gs)
        + tuple(pltpu.HBM(s, g.dtype) for s, g in zip(land_shapes, gs)) + (jax.ShapeDtypeStruct((8, 128), F32),),
        input_output_aliases={k: 2 + k for k in range(2 * n)},
        compiler_params=pltpu.CompilerParams(has_side_effects=DATAFLOW),
    )(*[pltpu.with_memory_space_constraint(g, pltpu.HBM) for g in gs], *lands, after)
    return res[0], res[1], list(res[2:2 + n]), list(res[2 + n:2 + 2 * n]), res[2 + 2 * n]


def _scatter_wait(send_sems, recv_sems, gs, lands, meta, after, tag):
    n = len(gs)

    def body(*refs):
        g_refs, land_refs = refs[:n], refs[n:2 * n]
        send_sems, recv_sems = refs[2 * n], refs[2 * n + 1]
        for cp in _scatter_copies(g_refs, land_refs, meta, send_sems, recv_sems):
            cp.wait_send()
            cp.wait_recv()

    both = list(gs) + list(lands)
    res = pl.pallas_call(
        body, name=f"rs_wait_{tag}", in_specs=[HBM] * (2 * n) + [SEM, SEM, pl.BlockSpec(memory_space=pl.ANY)],
        out_specs=[HBM] * (2 * n), out_shape=[pltpu.HBM(a.shape, a.dtype) for a in both],
        input_output_aliases={k: k for k in range(2 * n)},
        compiler_params=pltpu.CompilerParams(has_side_effects=DATAFLOW),
    )(*both, send_sems, recv_sems, after)
    return list(res[:n]), list(res[n:])


def _chip_sum_full(g, got, m, chip_idx, *, name):
    kind, L, A, B = m
    ta = _div_tile(A, 256, 16)

    def body(r_ref, a_ref, b_ref, o_ref):
        acc = a_ref[...].astype(F32)
        for j in range(3):
            acc = acc + b_ref[j].astype(F32)
        o_ref[...] = acc

    if kind == 'col':
        g_spec = pl.BlockSpec((None, ta, B), lambda i, r: (0, i, r[0]))
    else:
        g_spec = pl.BlockSpec((None, None, ta, B), lambda i, r: (0, r[0], i, 0))
    grid_spec = pltpu.PrefetchScalarGridSpec(
        num_scalar_prefetch=1, grid=(A // ta,),
        in_specs=[g_spec, pl.BlockSpec((3, ta, B), lambda i, r: (0, i, 0))],
        out_specs=pl.BlockSpec((None, ta, B), lambda i, r: (0, i, 0)))
    return pl.pallas_call(
        body, name=name, grid_spec=grid_spec, out_shape=jax.ShapeDtypeStruct((1, A, B), F32),
        compiler_params=_cparams(("parallel",)),
    )(chip_idx, g, got)


def _all_gather(fulls, meta):
    n = len(fulls)

    def body(*refs):
        outs = refs[n:2 * n]
        send_sems, recv_sems = refs[2 * n:]
        x, y, c = _mesh_pos()
        sibling = (x, y, 1 - c)
        chips = _other_chips(x, y)

        def part(k, s, h):
            kind, L, A, B = meta[k]
            return _chip_part(_gathered_piece(outs[k], kind, L, A, h), kind, B, s)

        def copy(ref, sem, to):
            return pltpu.make_async_remote_copy(src_ref=ref, dst_ref=ref, send_sem=send_sems.at[sem],
                                                recv_sem=recv_sems.at[sem], device_id=to, device_id_type=MESH)

        started = []
        for k in range(n):
            for j, (cx, cy) in enumerate(chips):
                cp = copy(part(k, 2 * x + y, c), 3 * k + j, (cx, cy, c))
                cp.start()
                started.append(cp)
        for j, (cx, cy) in enumerate(chips):
            for k in range(n):
                landed = part(k, 2 * cx + cy, c)
                copy(landed, 3 * k + j, (cx, cy, c)).wait_recv()
                fwd = copy(landed, 3 * n + 3 * k + j, sibling)
                fwd.start()
                started.append(fwd)
        for j, (cx, cy) in enumerate(chips):
            for k in range(n):
                copy(part(k, 2 * cx + cy, 1 - c), 3 * n + 3 * k + j, sibling).wait_recv()
        for cp in started:
            cp.wait_send()

    return pl.pallas_call(
        body, name="ag_weights", in_specs=[ANY] * n, out_specs=[ANY] * n,
        out_shape=[jax.ShapeDtypeStruct(f.shape, f.dtype) for f in fulls],
        input_output_aliases={k: k for k in range(n)},
        scratch_shapes=[pltpu.SemaphoreType.DMA((6 * n,)), pltpu.SemaphoreType.DMA((6 * n,))],
    )(*fulls)


def _swap_copies(src_refs, land_refs, send_sems, recv_sems):
    x, y, c = _mesh_pos()
    return [pltpu.make_async_remote_copy(src_ref=s, dst_ref=d, send_sem=send_sems.at[k], recv_sem=recv_sems.at[k],
                                         device_id=(x, y, 1 - c), device_id_type=MESH)
            for k, (s, d) in enumerate(zip(src_refs, land_refs))]


def _swap_start(sums, after, tag):
    n = len(sums)

    def body(*refs):
        send_sems, recv_sems = refs[2 * n + 1], refs[2 * n + 2]
        s_out = refs[2 * n + 3:3 * n + 3]
        lands = refs[3 * n + 3:4 * n + 3]
        token = refs[4 * n + 3]
        for cp in _swap_copies(s_out, lands, send_sems, recv_sems):
            cp.start()
        token[...] = jnp.zeros_like(token)

    lands = [pltpu.with_memory_space_constraint(lax.empty(s.shape, s.dtype), pltpu.HBM) for s in sums]
    res = pl.pallas_call(
        body, name=f"rs_swap_start_{tag}", in_specs=[HBM] * (2 * n) + [pl.BlockSpec(memory_space=pl.ANY)],
        out_specs=(SEM, SEM) + (HBM,) * (2 * n) + (pl.BlockSpec(memory_space=pltpu.VMEM),),
        out_shape=(pltpu.SemaphoreType.DMA((n,)), pltpu.SemaphoreType.DMA((n,)))
        + tuple(pltpu.HBM(s.shape, s.dtype) for s in sums) * 2 + (jax.ShapeDtypeStruct((8, 128), F32),),
        input_output_aliases={k: 2 + k for k in range(2 * n)},
        compiler_params=pltpu.CompilerParams(has_side_effects=DATAFLOW),
    )(*[pltpu.with_memory_space_constraint(s, pltpu.HBM) for s in sums], *lands, after)
    return res[0], res[1], list(res[2:2 + n]), list(res[2 + n:2 + 2 * n]), res[2 + 2 * n]


def _swap_wait(send_sems, recv_sems, sums, lands, after, tag):
    n = len(sums)

    def body(*refs):
        send_sems, recv_sems = refs[2 * n], refs[2 * n + 1]
        for cp in _swap_copies(refs[:n], refs[n:2 * n], send_sems, recv_sems):
            cp.wait_send()
            cp.wait_recv()

    both = list(sums) + list(lands)
    res = pl.pallas_call(
        body, name=f"rs_swap_wait_{tag}", in_specs=[HBM] * (2 * n) + [SEM, SEM, pl.BlockSpec(memory_space=pl.ANY)],
        out_specs=[HBM] * (2 * n), out_shape=[pltpu.HBM(a.shape, a.dtype) for a in both],
        input_output_aliases={k: k for k in range(2 * n)},
        compiler_params=pltpu.CompilerParams(has_side_effects=DATAFLOW),
    )(*both, send_sems, recv_sems, after)
    return list(res[:n]), list(res[n:])


def _pair_swap(sums, small, *, tag):
    ns = len(sums)
    with_small = small is not None
    n_in = ns + with_small

    def body(*refs):
        sum_refs = refs[:ns]
        got_refs = refs[n_in:n_in + ns]
        send_sems, recv_sems = refs[2 * n_in], refs[2 * n_in + 1]
        x, y, c = _mesh_pos()
        cps = []
        for k in range(ns):
            cp = pltpu.make_async_remote_copy(
                src_ref=sum_refs[k], dst_ref=got_refs[k], send_sem=send_sems.at[k], recv_sem=recv_sems.at[k],
                device_id=(x, y, 1 - c), device_id_type=MESH)
            cp.start()
            cps.append(cp)
        if with_small:
            small_ref, sm_ref, local_sem = refs[ns], refs[n_in + ns], refs[2 * n_in + 2]
            me = 4 * x + 2 * y + c
            own = pltpu.make_async_copy(small_ref, sm_ref.at[me], local_sem)
            own.start()
            for r in range(1, N_DEV):
                fx, fy, fc = (r >> 2) & 1, (r >> 1) & 1, r & 1
                peer = (1 - x if fx else x, 1 - y if fy else y, 1 - c if fc else c)
                cp = pltpu.make_async_remote_copy(
                    src_ref=small_ref, dst_ref=sm_ref.at[me], send_sem=send_sems.at[ns + r],
                    recv_sem=recv_sems.at[ns + r], device_id=peer, device_id_type=MESH)
                cp.start()
                cps.append(cp)
        for cp in cps:
            cp.wait()
        if with_small:
            own.wait()

    out_shape = [jax.ShapeDtypeStruct(s.shape, s.dtype) for s in sums]
    scratch = [pltpu.SemaphoreType.DMA((ns + N_DEV,)), pltpu.SemaphoreType.DMA((ns + N_DEV,))]
    args = list(sums)
    if with_small:
        out_shape.append(jax.ShapeDtypeStruct((N_DEV,) + small.shape, F32))
        scratch.append(pltpu.SemaphoreType.DMA)
        args.append(small)
    res = pl.pallas_call(
        body, name=f"rs_pair_swap_{tag}", in_specs=[ANY] * n_in, out_specs=[ANY] * n_in, out_shape=out_shape,
        scratch_shapes=scratch,
    )(*args)
    return list(res[:ns]), (res[ns] if with_small else None)


def _tiny_pack(conv_a_w, sc_conv_w):
    lead = conv_a_w.shape[:-2]
    sc = sc_conv_w.reshape(lead + (2 * SC_WIDTH, 128))
    z = lambda r: jnp.zeros(lead + (r, 128), F32)
    return jnp.concatenate([conv_a_w, z(32 - CONV_A_WIDTH), sc, z(TINY_ROWS - 32 - 2 * SC_WIDTH)], axis=-2)


def _tiny_unpack(t):
    lead = t.shape[:-2]
    return t[..., :CONV_A_WIDTH, :], t[..., 32:32 + 2 * SC_WIDTH, :].reshape(lead + (SC_WIDTH, 256))


def _pack_small(d):
    parts = []
    for n in SMALL_NAMES:
        flat = d[n].astype(F32).reshape(-1)
        parts.append(jnp.pad(flat, (0, -flat.shape[0] % 128)))
    flat = jnp.concatenate(parts)
    return jnp.pad(flat, (0, -flat.shape[0] % 1024)).reshape(-1, 128)


def _ffn_fwd(h, g, W, n_gu, n_down, i, tag):
    h2, u, gate, up, a = _ffn_fwd_fused(h, g, W[n_gu][i], W[n_down][i], 0, name=f"{tag}_fwd")
    return h2, (h, u, gate, up, a)


def _ffn_bwd(dh, saved, g, W, n_gu, n_down, i, G, tag):
    h, u, gate, up, a = saved
    dh_in, dg, dgate, dup = _ffn_bwd_fused(dh, h, g, gate, up, W[n_gu][i], W[n_down][i], 0, name=f"{tag}_bwd")
    G[(n_down, i)] = _mm(a, dh, name=f"{tag}_b_wdown", ta=True, tm=1408, tn=1024, tk=1024, scale=0.5)
    tn = FFN_CHUNK
    half = _mm(u, dgate, name=f"{tag}_b_wg", ta=True, tm=1024, tn=tn, tk=2048, stack=(1, 0, None, 2 * D_FF))
    G[(n_gu, i)] = _mm(u, dup, name=f"{tag}_b_wu", ta=True, tm=1024, tn=tn, tk=2048, stack=(1, 0, half, 2 * D_FF),
                       n_map=lambda j: j + D_FF // tn)
    return dh_in, dg


def _xa_block_fwd(h, mem, g, gm, W, i, tag):
    mn = _rms_fwd(mem, gm, name=f"{tag}_mem_norm")
    u, q = _norm_mm(h, g, W['xa_wq'][i], 0, name=f"{tag}_q")
    kv = _mm(mn, W['xa_wkv'][i], b_layer=0, name=f"{tag}_kv", tm=256, tn=1024, tk=1024)
    o = _xa_fwd(q, kv, name=f"{tag}_attn")
    h2 = _mm(o, W['xa_wo'][i], b_layer=0, name=f"{tag}_o", out_dtype=F32, tm=1024, tn=1024, tk=1024, res=h)
    return h2, (h, u, mn, q, kv, o)


def _xa_block_bwd(dh, saved, mem, g, gm, W, i, G, tag):
    h, u, mn, q, kv, o = saved
    do = _mm(dh, W['xa_wo'][i], b_layer=0, name=f"{tag}_b_do", tb=True, tm=1024, tn=1024, tk=1024)
    G[('xa_wo', i)] = _mm(o, dh, name=f"{tag}_b_wo", ta=True, tm=1024, tn=1024, tk=1024)
    dq, dkv = _xa_bwd(q, kv, do, name=f"{tag}_b_attn")
    G[('xa_wq', i)] = _mm(u, dq, name=f"{tag}_b_wq", ta=True, tm=1024, tn=1024, tk=1024)
    dh_in, dg = _mm_norm_bwd(dq, W['xa_wq'][i], 0, h, g, dh, name=f"{tag}_b_du", tk=1024)
    G[('xa_wkv', i)] = _mm(mn, dkv, name=f"{tag}_b_wkv", ta=True, tm=1024, tn=1024, tk=256)
    dmn = _mm(dkv, W['xa_wkv'][i], b_layer=0, name=f"{tag}_b_dmn", tb=True, out_dtype=F32, tm=256, tn=1024, tk=1024)
    _, dgm = _rms_bwd(mem, gm, dmn, None, name=f"{tag}_b_mem_norm")
    return dh_in, dg, dgm


def _pad_conv_w(w, rows):
    return jnp.pad(w.astype(F32), ((0, rows - w.shape[0]), (0, 0)))


def _even_fwd(h, g, W, conv_w, conv_b, ln_g, ln_b, sinks, tag):
    u, z = _norm_mm(h, g, W['even_w_in'][0], 0, name=f"{tag}_in")
    c, act = _conv_a_fwd(z, conv_w, conv_b, ln_g, ln_b, name=f"{tag}_conv")
    kpad = jnp.pad(z[:, 1536:1664], ((WINDOW, 0), (0, 0)))
    vpad = jnp.pad(z[:, 1664:1792], ((WINDOW, 0), (0, 0)))
    o = _swa_fwd(z, kpad, vpad, sinks, name=f"{tag}_swa")
    cat = jnp.concatenate([act, o], axis=-1)
    h2 = _mm(cat, W['even_w_out'][0], b_layer=0, name=f"{tag}_out", out_dtype=F32, tm=1024, tn=1024, tk=1024, res=h)
    return h2, (h, u, z, c, kpad, vpad, cat)


def _even_bwd(dh, saved, g, W, conv_w, ln_g, ln_b, sinks, G, tag):
    h, u, z, c, kpad, vpad, cat = saved
    dcat = _mm(dh, W['even_w_out'][0], b_layer=0, name=f"{tag}_b_dcat", tb=True, tm=1024, tn=1024, tk=1024)
    G[('even_w_out', 0)] = _mm(cat, dh, name=f"{tag}_b_wout", ta=True, tm=1024, tn=1024, tk=1024)
    dz_a, small = _conv_a_bwd(z, c, dcat, conv_w, ln_g, ln_b, name=f"{tag}_b_conv")
    dq, dkp, dvp, dsinks = _swa_bwd(z, kpad, vpad, sinks, dcat, name=f"{tag}_b_swa")
    dz = jnp.concatenate([dz_a, dq, dkp[WINDOW:].astype(BF16), dvp[WINDOW:].astype(BF16)], axis=-1)
    G[('even_w_in', 0)] = _mm(u, dz, name=f"{tag}_b_win", ta=True, tm=1024, tn=1792, tk=1024)
    dh_in, dg = _mm_norm_bwd(dz, W['even_w_in'][0], 0, h, g, dh, name=f"{tag}_b_du", tk=1792)
    grads = dict(mix=dg, conv_a_w=small[:CONV_A_WIDTH], conv_a_b=small[32:33], conv_a_ln_g=small[33:34],
                 conv_a_ln_b=small[34:35], swa_sinks=dsinks[:, 0])
    return dh_in, grads


def _odd_fwd(h, g, W, conv_w, tag):
    u, z = _norm_mm(h, g, W['odd_w_in'][0], 0, name=f"{tag}_in")
    y, cc = _sconv_fwd(z, conv_w, name=f"{tag}_conv")
    h2 = _mm(y, W['odd_w_out'][0], b_layer=0, name=f"{tag}_out", out_dtype=F32, tm=1024, tn=1024, tk=1024, res=h)
    return h2, (h, u, z, y, cc)


def _odd_bwd(dh, saved, g, W, conv_w, G, tag):
    h, u, z, y, cc = saved
    dy = _mm(dh, W['odd_w_out'][0], b_layer=0, name=f"{tag}_b_dy", tb=True, tm=1024, tn=1024, tk=1024)
    G[('odd_w_out', 0)] = _mm(y, dh, name=f"{tag}_b_wout", ta=True, tm=1024, tn=1024, tk=1024)
    dz, dw = _sconv_bwd(z, cc, dy, conv_w, name=f"{tag}_b_conv")
    G[('odd_w_in', 0)] = _mm(u, dz, name=f"{tag}_b_win", ta=True, tm=1024, tn=1024, tk=1024)
    dh_in, dg = _mm_norm_bwd(dz, W['odd_w_in'][0], 0, h, g, dh, name=f"{tag}_b_du", tk=1024)
    return dh_in, dict(mix=dg, sc_conv_w=dw[:SC_WIDTH])


def _local_step(x, mem, tgt, W, need, token, ready, conv_a_w, sc_conv_w, P):
    row = lambda v: v.reshape(1, -1)
    conv_a_w = _pad_conv_w(conv_a_w, 32)
    sc_w = _pad_conv_w(sc_conv_w, 8)
    sinks = P['swa_sinks'][0]

    def arrive(stage, h):
        for n, ws in need(stage, h).items():
            W[n] = W.get(n, []) + ws

    h = x
    saved = []
    for i in range(2):
        t = f"l{i}"
        if i == 1:
            arrive('l1_ffn1', h)
        g1 = row(P['ffn1_norm'][i]) + (token if i == 0 else 0.0)
        h, s1 = _ffn_fwd(h, g1, W, 'ffn1_w_gu', 'ffn1_w_down', i, f"{t}_ffn1")
        arrive(f"{t}_mix", h)
        if i == 0:
            h, s2 = _even_fwd(h, row(P['mix_norm'][i]), W, conv_a_w, P['conv_a_b'], P['conv_a_ln_g'],
                              P['conv_a_ln_b'], sinks, f"{t}_even")
        else:
            h, s2 = _odd_fwd(h, row(P['mix_norm'][i]), W, sc_w, f"{t}_odd")
        h, s3 = _xa_block_fwd(h, mem, row(P['xa_norm'][i]), row(P['xa_mem_norm'][i]), W, i, f"{t}_xa")
        arrive(f"{t}_ffn2", h)
        h, s4 = _ffn_fwd(h, row(P['ffn2_norm'][i]), W, 'ffn2_w_gu', 'ffn2_w_down', i, f"{t}_ffn2")
        saved.append((s1, s2, s3, s4))

    loss, dh, d_final = _final_loss(h, row(P['final_norm']), tgt, name="final_loss")

    G = {}
    gp = {n: [None, None] for n in ('ffn1_norm', 'mix_norm', 'xa_norm', 'xa_mem_norm', 'ffn2_norm')}
    single = {}
    for i in (1, 0):
        t = f"l{i}"
        s1, s2, s3, s4 = saved[i]
        g4 = row(P['ffn2_norm'][i]) + (ready('l1', G) if i == 0 else 0.0)
        dh, gp['ffn2_norm'][i] = _ffn_bwd(dh, s4, g4, W, 'ffn2_w_gu', 'ffn2_w_down', i, G, f"{t}_ffn2")
        dh, gp['xa_norm'][i], gp['xa_mem_norm'][i] = _xa_block_bwd(
            dh, s3, mem, row(P['xa_norm'][i]), row(P['xa_mem_norm'][i]), W, i, G, f"{t}_xa")
        if i == 0:
            dh, g2 = _even_bwd(dh, s2, row(P['mix_norm'][i]), W, conv_a_w, P['conv_a_ln_g'], P['conv_a_ln_b'], sinks,
                               G, f"{t}_even")
        else:
            dh, g2 = _odd_bwd(dh, s2, row(P['mix_norm'][i]), W, sc_w, G, f"{t}_odd")
        gp['mix_norm'][i] = g2.pop('mix')
        single.update(g2)
        g1 = row(P['ffn1_norm'][i]) + (ready('l0_rest', G) if i == 0 else 0.0)
        dh, gp['ffn1_norm'][i] = _ffn_bwd(dh, s1, g1, W, 'ffn1_w_gu', 'ffn1_w_down', i, G, f"{t}_ffn1")

    small = {n: jnp.concatenate(v, axis=0) for n, v in gp.items()}
    small['conv_a_b'] = single['conv_a_b']
    small['conv_a_ln_g'] = single['conv_a_ln_g']
    small['conv_a_ln_b'] = single['conv_a_ln_b']
    small['swa_sinks'] = single['swa_sinks'][None]
    small['final_norm'] = d_final[0]
    small['conv_a_w'] = single['conv_a_w']
    small['sc_conv_w'] = single['sc_conv_w']
    return loss[0, 0], dh, G, small


def kernel(x, mem, ffn1_norm, ffn1_w_gu, ffn1_w_down, mix_norm, even_w_in, conv_a_w, conv_a_b, conv_a_ln_g, conv_a_ln_b, swa_sinks, even_w_out, odd_w_in, sc_conv_w, odd_w_out, xa_norm, xa_mem_norm, xa_wq, xa_wkv, xa_wo, ffn2_norm, ffn2_w_gu, ffn2_w_down, final_norm, loss_target, m_ffn1_norm, m_ffn1_w_gu, m_ffn1_w_down, m_mix_norm, m_even_w_in, m_conv_a_w, m_conv_a_b, m_conv_a_ln_g, m_conv_a_ln_b, m_swa_sinks, m_even_w_out, m_odd_w_in, m_sc_conv_w, m_odd_w_out, m_xa_norm, m_xa_mem_norm, m_xa_wq, m_xa_wkv, m_xa_wo, m_ffn2_norm, m_ffn2_w_gu, m_ffn2_w_down, m_final_norm, v_ffn1_norm, v_ffn1_w_gu, v_ffn1_w_down, v_mix_norm, v_even_w_in, v_conv_a_w, v_conv_a_b, v_conv_a_ln_g, v_conv_a_ln_b, v_swa_sinks, v_even_w_out, v_odd_w_in, v_sc_conv_w, v_odd_w_out, v_xa_norm, v_xa_mem_norm, v_xa_wq, v_xa_wkv, v_xa_wo, v_ffn2_norm, v_ffn2_w_gu, v_ffn2_w_down, v_final_norm):
    w = dict(zip(WEIGHT_NAMES, (ffn1_norm, ffn1_w_gu, ffn1_w_down, mix_norm, even_w_in, conv_a_w, conv_a_b, conv_a_ln_g, conv_a_ln_b, swa_sinks, even_w_out, odd_w_in, sc_conv_w, odd_w_out, xa_norm, xa_mem_norm, xa_wq, xa_wkv, xa_wo, ffn2_norm, ffn2_w_gu, ffn2_w_down, final_norm)))
    m = dict(zip(WEIGHT_NAMES, (m_ffn1_norm, m_ffn1_w_gu, m_ffn1_w_down, m_mix_norm, m_even_w_in, m_conv_a_w, m_conv_a_b, m_conv_a_ln_g, m_conv_a_ln_b, m_swa_sinks, m_even_w_out, m_odd_w_in, m_sc_conv_w, m_odd_w_out, m_xa_norm, m_xa_mem_norm, m_xa_wq, m_xa_wkv, m_xa_wo, m_ffn2_norm, m_ffn2_w_gu, m_ffn2_w_down, m_final_norm)))
    v = dict(zip(WEIGHT_NAMES, (v_ffn1_norm, v_ffn1_w_gu, v_ffn1_w_down, v_mix_norm, v_even_w_in, v_conv_a_w, v_conv_a_b, v_conv_a_ln_g, v_conv_a_ln_b, v_swa_sinks, v_even_w_out, v_odd_w_in, v_sc_conv_w, v_odd_w_out, v_xa_norm, v_xa_mem_norm, v_xa_wq, v_xa_wkv, v_xa_wo, v_ffn2_norm, v_ffn2_w_gu, v_ffn2_w_down, v_final_norm)))
    cx, cy, cc = lax.axis_index("x"), lax.axis_index("y"), lax.axis_index("c")
    chip_idx = (2 * cx + cy).astype(jnp.int32).reshape(1)

    shards = {n: w[n] for n in COMM_NAMES if n != 'tiny'}
    shards['tiny'] = _tiny_pack(conv_a_w, sc_conv_w)
    first =[('ffn1_w_gu', 0), ('ffn1_w_down', 0), ('tiny', 0)]
    stages = {
        'l0_mix': [('even_w_in', 0), ('even_w_out', 0), ('xa_wq', 0), ('xa_wkv', 0), ('xa_wo', 0)],
        'l0_ffn2': [('ffn2_w_gu', 0), ('ffn2_w_down', 0)],
        'l1_ffn1': [('ffn1_w_gu', 1), ('ffn1_w_down', 1)],
        'l1_mix': [('odd_w_in', 0), ('odd_w_out', 0), ('xa_wq', 1), ('xa_wkv', 1), ('xa_wo', 1)],
        'l1_ffn2': [('ffn2_w_gu', 1), ('ffn2_w_down', 1)],
    }
    grad_stages = {'l1': stages['l1_ffn1'] + stages['l1_mix'] + stages['l1_ffn2'],
                   'l0_rest': stages['l0_mix'] + stages['l0_ffn2'], 'l0_ffn1': first}

    def place(items):
        return [_place(shards[n], l, LAYOUT[n], chip_idx, F32 if n == 'tiny' else BF16, name=f"place_{n}_{l}")
                for n, l in items]

    def item_meta(items):
        return [(LAYOUT[n], 1) + shards[n].shape[1:] for n, l in items]

    def natural(items, arrays):
        out = {}
        for (n, l), a in zip(items, arrays):
            if n == 'tiny':
                continue
            if n == 'even_w_in':
                out[n] = [a.transpose(0, 2, 1, 3).reshape(1, D_MODEL, -1)]
            else:
                out[n] = [a if LAYOUT[n] == 'col' else a.reshape(1, N_CHIPS * a.shape[2], a.shape[3])]
        return out

    first_full = _all_gather(place(first), item_meta(first))
    W = natural(first, first_full)
    ca, sc = _tiny_unpack(first_full[-1][0])
    conv_a_full = ca.transpose(1, 0, 2).reshape(CONV_A_WIDTH, CONV_A_CH)
    sc_full = sc.transpose(1, 0, 2).reshape(SC_WIDTH, SC_CH)
    gathers = {}
    token = first_full[0]
    for stage, items in stages.items():
        send, recv, in_flight, token = _gather_start(place(items), item_meta(items), token, stage)
        gathers[stage] = (send, recv, in_flight)

    def need(stage, h):
        send, recv, in_flight = gathers[stage]
        items = stages[stage]
        return natural(items, _gather_wait(send, recv, in_flight, item_meta(items), h, stage))

    def gathered_layout(G, item):
        n, l = item
        A, B = shards[n].shape[1:]
        g = G[item]
        if n == 'tiny':
            return g
        if n == 'even_w_in':
            return g.reshape(A, N_CHIPS, B).transpose(1, 0, 2)[None]
        return g.reshape(1, A, N_CHIPS * B) if LAYOUT[n] == 'col' else g.reshape(1, N_CHIPS, A, B)

    scatters, tokens = {}, {}

    def ready(stage, G):
        items = grad_stages[stage]
        send, recv, gs1, lands, tok = _scatter_start([gathered_layout(G, it) for it in items], item_meta(items),
                                                     chip_idx, stage)
        scatters[stage] = (send, recv, gs1, lands)
        tokens[stage] = tok
        return tok[:1, :1]

    loss_part, grad_x, G, g_small = _local_step(x[0], mem[0], loss_target[0], W, need, token[:1, :1], ready,
                                                conv_a_full, sc_full, {n: w[n] for n in SMALL_NAMES})
    G[('tiny', 0)] = _tiny_pack(g_small['conv_a_w'].reshape(CONV_A_WIDTH, N_CHIPS, 128).transpose(1, 0, 2),
                                g_small['sc_conv_w'].reshape(SC_WIDTH, N_CHIPS, 256).transpose(1, 0, 2))[None]
    loss = lax.psum(loss_part, ("x", "y", "c"))

    ready('l0_ffn1', G)
    started = tokens['l0_ffn1']

    def summed(stage, after):
        send, recv, gs1, lands = scatters[stage]
        items = grad_stages[stage]
        sent, landed = _scatter_wait(send, recv, gs1, lands, item_meta(items), after, stage)
        return items, [_chip_sum_full(g, r, m_, chip_idx, name=f"rs_chip_sum_{n}_{l}")
                       for (n, l), g, r, m_ in zip(items, sent, landed, item_meta(items))]

    def adamw(n, g1):
        if n == 'tiny':
            pk = lambda d: _tiny_pack(d['conv_a_w'], d['sc_conv_w'])
            res = [_tiny_unpack(a) for a in _adamw_layers(pk(w), pk(m), pk(v), [g1[('tiny', 0)]], name="adamw_tiny")]
            for k, nn in enumerate(('conv_a_w', 'sc_conv_w')):
                grads[nn], deltas[nn], new_m[nn], new_v[nn] = (r[k] for r in res)
        else:
            gsrc = [g1[(n, l)] for l in range(w[n].shape[0])]
            grads[n], deltas[n], new_m[n], new_v[n] = _adamw_layers(w[n], m[n], v[n], gsrc, name=f"adamw_{n}")

    grads, deltas, new_m, new_v = {}, {}, {}, {}
    sum_of = {}
    for stage in ('l1', 'l0_rest'):
        its, ss = summed(stage, started)
        sum_of.update(zip(its, ss))
    swap_groups = [['even_w_in', 'even_w_out', 'odd_w_in', 'odd_w_out', 'xa_wq', 'xa_wkv', 'xa_wo'],
                   ['ffn2_w_gu', 'ffn2_w_down'], ['ffn1_w_gu', 'ffn1_w_down']]
    swaps, after = [], started
    for gi, names in enumerate(swap_groups):
        its = [it for it in sum_of if it[0] in names]
        send, recv, own, lands, after = _swap_start([sum_of[it] for it in its], after, f"g{gi}")
        swaps.append((its, send, recv, own, lands))
    _, small_parts = _pair_swap([], _pack_small(g_small), tag="small")
    g1 = {}

    def swapped(gi, after):
        its, send, recv, own, lands = swaps[gi]
        mine, theirs = _swap_wait(send, recv, own, lands, after, f"g{gi}")
        g1.update({it: [a, b] for it, a, b in zip(its, mine, theirs)})

    for gi in (0, 1):
        swapped(gi, after)
        for n in swap_groups[gi]:
            adamw(n, g1)
        after = deltas[swap_groups[gi][-1]]

    swapped(2, after)
    its, ss = summed('l0_ffn1', after)
    sib, _ = _pair_swap(ss, None, tag="last")
    g1.update({it: [a, b] for it, a, b in zip(its, ss, sib)})
    for n in ('ffn1_w_gu', 'ffn1_w_down', 'tiny'):
        adamw(n, g1)
    rows2d = lambda d: [d[n].reshape(-1, d[n].shape[-1]) for n in SMALL_NAMES]
    for dst, arrs in zip((grads, deltas, new_m, new_v),
                         _adamw_small(rows2d(w), rows2d(m), rows2d(v), small_parts, name="adamw_small")):
        dst.update({n: a.reshape(w[n].shape) for n, a in zip(SMALL_NAMES, arrs)})

    return (loss, grad_x[None], *[grads[n] for n in WEIGHT_NAMES], *[deltas[n] for n in WEIGHT_NAMES],
            *[new_m[n] for n in WEIGHT_NAMES], *[new_v[n] for n in WEIGHT_NAMES])
```

```python
import jax
import jax.numpy as jnp
from jax import lax
from jax.experimental import pallas as pl
from jax.experimental.pallas import tpu as pltpu

F32 = jnp.float32
BF16 = jnp.bfloat16

D_MODEL = 1024
D_FF = 2816
CONV_A_CH = 512
CONV_A_WIDTH = 31
SWA_HEADS = 8
SWA_KV_HEADS = 2
SWA_GROUP = 4
HEAD_DIM = 64
WINDOW = 128
SC_CH = 1024
SC_WIDTH = 3
XA_HEADS = 4
XA_HEAD_DIM = 256
RMS_EPS = 1e-6
LN_EPS = 1e-5

ADAM_LR = 0.001
ADAM_B1 = 0.9
ADAM_B2 = 0.999
ADAM_EPS = 1e-08
ADAM_WD = 0.01
ADAM_STEP = 10
ADAM_TILE_ELEMS = 384 * 1024

N_CHIPS = 4
N_DEV = 8
NEG_BIG = -1e30
VMEM_LIMIT = 56 * 1024 * 1024
MESH = pl.DeviceIdType.MESH

INPUT_NAMES = ['x', 'mem', 'ffn1_norm', 'ffn1_w_gu', 'ffn1_w_down', 'mix_norm', 'even_w_in', 'conv_a_w', 'conv_a_b',
               'conv_a_ln_g', 'conv_a_ln_b', 'swa_sinks', 'even_w_out', 'odd_w_in', 'sc_conv_w', 'odd_w_out', 'xa_norm',
               'xa_mem_norm', 'xa_wq', 'xa_wkv', 'xa_wo', 'ffn2_norm', 'ffn2_w_gu', 'ffn2_w_down', 'final_norm']
WEIGHT_NAMES = INPUT_NAMES[2:]
BIG = [('ffn1_w_gu', 'col'), ('ffn1_w_down', 'row'), ('even_w_in', 'col'), ('conv_a_w', 'col'), ('even_w_out', 'row'),
       ('odd_w_in', 'col'), ('sc_conv_w', 'col'), ('odd_w_out', 'row'), ('xa_wq', 'row'), ('xa_wkv', 'col'),
       ('xa_wo', 'row'), ('ffn2_w_gu', 'col'), ('ffn2_w_down', 'row')]
BIG_NAMES = [n for n, _ in BIG]
SMALL_NAMES = [n for n in WEIGHT_NAMES if n not in BIG_NAMES]


def _cparams(sem=None, vmem=VMEM_LIMIT):
    kw = dict(vmem_limit_bytes=vmem)
    if sem is not None:
        kw['dimension_semantics'] = sem
    return pltpu.CompilerParams(**kw)


def _div_tile(n, want, align=8):
    if n <= want:
        return n
    t = (want // align) * align
    while t >= align:
        if n % t == 0:
            return t
        t -= align
    return n


def _mm(a, b, *, name, ta=False, tb=False, out_dtype=BF16, tm=512, tn=512, tk=512, res=None, scale=1.0,
        b_layer=None, stack=None, n_map=None):
    n_map = n_map or (lambda j: j)
    if ta:
        K, M = a.shape
    else:
        M, K = a.shape
    if tb:
        N, K2 = b.shape[-2:]
    else:
        K2, N = b.shape[-2:]
    assert K == K2, (a.shape, b.shape, ta, tb)
    tm = _div_tile(M, tm, 128 if ta else 16)
    tn = _div_tile(N, tn, 128)
    tk = _div_tile(K, tk, 16 if ta else 128)
    nk = K // tk
    a_spec = pl.BlockSpec((tk, tm), lambda i, j, k: (k, i)) if ta else pl.BlockSpec((tm, tk), lambda i, j, k: (i, k))
    if b_layer is None:
        b_spec = pl.BlockSpec((tn, tk), lambda i, j, k: (j, k)) if tb else pl.BlockSpec((tk, tn), lambda i, j, k: (k, j))
    elif tb:
        b_spec = pl.BlockSpec((None, tn, tk), lambda i, j, k: (b_layer, j, k))
    else:
        b_spec = pl.BlockSpec((None, tk, tn), lambda i, j, k: (b_layer, k, j))
    o_spec = pl.BlockSpec((tm, tn), lambda i, j, k: (i, j))
    out_shape = jax.ShapeDtypeStruct((M, N), out_dtype)
    out_spec = o_spec
    aliases = {}
    extra_specs, extra_args = [], ()
    if stack is not None:
        n_layers, layer, buf = stack[:3]
        n_total = stack[3] if len(stack) > 3 else N
        out_shape = jax.ShapeDtypeStruct((n_layers, M, n_total), out_dtype)
        out_spec = pl.BlockSpec((None, tm, tn), lambda i, j, k: (layer, i, n_map(j)))
        if buf is not None:
            extra_specs, extra_args = [pl.BlockSpec(memory_space=pl.ANY)], (buf,)
            aliases = {2 + (res is not None): 0}
    dims = (((0 if ta else 1,), (1 if tb else 0,)), ((), ()))
    has_res = res is not None
    n_extra = len(extra_args)

    def body(*refs):
        if n_extra:
            refs = refs[:2 + has_res] + refs[2 + has_res + n_extra:]
        if has_res:
            a_ref, b_ref, r_ref, o_ref, acc_ref = refs
        else:
            a_ref, b_ref, o_ref, acc_ref = refs
        k = pl.program_id(2)
        p = lax.dot_general(a_ref[...].astype(BF16), b_ref[...].astype(BF16), dims, preferred_element_type=F32)

        @pl.when(k == 0)
        def _():
            acc_ref[...] = p

        @pl.when(k > 0)
        def _():
            acc_ref[...] += p

        @pl.when(k == nk - 1)
        def _():
            r = acc_ref[...] * scale
            if has_res:
                r = r_ref[...] + r
            o_ref[...] = r.astype(out_dtype)

    in_specs = [a_spec, b_spec] + ([o_spec] if has_res else []) + extra_specs
    args = (a, b) + ((res,) if has_res else ()) + extra_args
    return pl.pallas_call(
        body, name=name, grid=(M // tm, N // tn, nk), in_specs=in_specs, out_specs=out_spec,
        out_shape=out_shape, input_output_aliases=aliases,
        scratch_shapes=[pltpu.VMEM((tm, tn), F32)],
        compiler_params=_cparams(("parallel", "parallel", "arbitrary")),
    )(*args)


def _rms_fwd(x, g, *, name):
    S, D = x.shape
    ts = _div_tile(S, 512)

    def body(x_ref, g_ref, o_ref):
        xv = x_ref[...]
        r = lax.rsqrt(jnp.mean(xv * xv, axis=-1, keepdims=True) + RMS_EPS)
        o_ref[...] = (xv * r * g_ref[...]).astype(BF16)

    return pl.pallas_call(
        body, name=name, grid=(S // ts,),
        in_specs=[pl.BlockSpec((ts, D), lambda i: (i, 0)), pl.BlockSpec((1, D), lambda i: (0, 0))],
        out_specs=pl.BlockSpec((ts, D), lambda i: (i, 0)),
        out_shape=jax.ShapeDtypeStruct((S, D), BF16),
        compiler_params=_cparams(("parallel",)),
    )(x, g)


NORM_SLAB = 256


def _norm_mm(h, g, w, layer, *, name):
    S, D = h.shape
    N = w.shape[-1]
    tm = _div_tile(S, 1024, NORM_SLAB)
    slab = min(NORM_SLAB, tm)

    def body(h_ref, g_ref, w_ref, u_ref, z_ref):
        for r0 in range(0, tm, slab):
            rows = pl.ds(r0, slab)
            xv = h_ref[rows, :]
            r = lax.rsqrt(jnp.mean(xv * xv, axis=-1, keepdims=True) + RMS_EPS)
            u = (xv * r * g_ref[...]).astype(BF16)
            u_ref[rows, :] = u
            z_ref[rows, :] = jnp.dot(u, w_ref[...], preferred_element_type=F32).astype(BF16)

    row = pl.BlockSpec((tm, D), lambda i: (i, 0))
    return pl.pallas_call(
        body, name=name, grid=(S // tm,),
        in_specs=[row, pl.BlockSpec((1, D), lambda i: (0, 0)), pl.BlockSpec((None, D, N), lambda i: (layer, 0, 0))],
        out_specs=[row, pl.BlockSpec((tm, N), lambda i: (i, 0))],
        out_shape=[jax.ShapeDtypeStruct((S, D), BF16), jax.ShapeDtypeStruct((S, N), BF16)],
        compiler_params=_cparams(("parallel",)),
    )(h, g, w)


def _mm_norm_bwd(dz, w, layer, h, g, dres, *, name, tk):
    S, K = dz.shape
    D = h.shape[1]
    tm = _div_tile(S, 1024, NORM_SLAB)
    slab = min(NORM_SLAB, tm)
    tk = _div_tile(K, tk, 128)
    nk = K // tk
    nt = (((1,), (1,)), ((), ()))

    def body(dz_ref, w_ref, h_ref, g_ref, dr_ref, dx_ref, dg_ref, acc):
        i = pl.program_id(0)
        k = pl.program_id(1)

        def norm_bwd(du_of):
            part = jnp.zeros((1, D), F32)
            for r0 in range(0, tm, slab):
                rows = pl.ds(r0, slab)
                du = du_of(rows)
                xv = h_ref[rows, :]
                r = lax.rsqrt(jnp.mean(xv * xv, axis=-1, keepdims=True) + RMS_EPS)
                xhat = xv * r
                part = part + jnp.sum(du * xhat, axis=0, keepdims=True)
                dxhat = du * g_ref[...]
                dx_ref[rows, :] = dr_ref[rows, :] + r * (
                    dxhat - xhat * jnp.mean(dxhat * xhat, axis=-1, keepdims=True))

            @pl.when(i == 0)
            def _():
                dg_ref[...] = part

            @pl.when(i > 0)
            def _():
                dg_ref[...] += part

        if nk == 1:
            norm_bwd(lambda rows: lax.dot_general(dz_ref[rows, :], w_ref[...], nt, preferred_element_type=F32))
        else:
            p = lax.dot_general(dz_ref[...], w_ref[...], nt, preferred_element_type=F32)

            @pl.when(k == 0)
            def _():
                acc[...] = p

            @pl.when(k > 0)
            def _():
                acc[...] += p

            @pl.when(k == nk - 1)
            def _():
                norm_bwd(lambda rows: acc[rows, :])

    row = pl.BlockSpec((tm, D), lambda i, k: (i, 0))
    vec = pl.BlockSpec((1, D), lambda i, k: (0, 0))
    return pl.pallas_call(
        body, name=name, grid=(S // tm, nk),
        in_specs=[pl.BlockSpec((tm, tk), lambda i, k: (i, k)), pl.BlockSpec((None, D, tk), lambda i, k: (layer, 0, k)),
                  row, vec, row],
        out_specs=[row, vec],
        out_shape=[jax.ShapeDtypeStruct((S, D), F32), jax.ShapeDtypeStruct((1, D), F32)],
        scratch_shapes=[pltpu.VMEM((tm, D), F32)],
        compiler_params=_cparams(("arbitrary", "arbitrary")),
    )(dz, w, h, g, dres)


def _rms_bwd(x, g, du, dres, *, name):
    S, D = x.shape
    ts = _div_tile(S, 512)
    has_res = dres is not None

    def body(*refs):
        if has_res:
            x_ref, g_ref, du_ref, dr_ref, dx_ref, dg_ref = refs
        else:
            x_ref, g_ref, du_ref, dg_ref = refs
        i = pl.program_id(0)
        xv = x_ref[...]
        duv = du_ref[...].astype(F32)
        r = lax.rsqrt(jnp.mean(xv * xv, axis=-1, keepdims=True) + RMS_EPS)
        xhat = xv * r
        part = jnp.sum(duv * xhat, axis=0, keepdims=True)

        @pl.when(i == 0)
        def _():
            dg_ref[...] = part

        @pl.when(i > 0)
        def _():
            dg_ref[...] += part

        if has_res:
            dxhat = duv * g_ref[...]
            dx = r * (dxhat - xhat * jnp.mean(dxhat * xhat, axis=-1, keepdims=True))
            dx_ref[...] = dr_ref[...] + dx

    row = pl.BlockSpec((ts, D), lambda i: (i, 0))
    vec = pl.BlockSpec((1, D), lambda i: (0, 0))
    if has_res:
        dx, dg = pl.pallas_call(
            body, name=name, grid=(S // ts,), in_specs=[row, vec, row, row], out_specs=[row, vec],
            out_shape=[jax.ShapeDtypeStruct((S, D), F32), jax.ShapeDtypeStruct((1, D), F32)],
            compiler_params=_cparams(("arbitrary",)),
        )(x, g, du, dres)
        return dx, dg
    dg = pl.pallas_call(
        body, name=name, grid=(S // ts,), in_specs=[row, vec, row], out_specs=vec,
        out_shape=jax.ShapeDtypeStruct((1, D), F32),
        compiler_params=_cparams(("arbitrary",)),
    )(x, g, du)
    return None, dg


def _final_loss(h, g, tgt, *, name):
    S, D = h.shape
    ts = _div_tile(S, 512)

    def body(h_ref, g_ref, t_ref, loss_ref, dh_ref, dg_ref):
        i = pl.program_id(0)
        xv = h_ref[...]
        gv = g_ref[...]
        r = lax.rsqrt(jnp.mean(xv * xv, axis=-1, keepdims=True) + RMS_EPS)
        xhat = xv * r
        err = xhat * gv - t_ref[...]
        lpart = 0.5 * jnp.sum(jnp.mean(err * err, axis=-1, keepdims=True), axis=0, keepdims=True)
        dy = err * (1.0 / D)
        gpart = jnp.sum(dy * xhat, axis=0, keepdims=True)

        @pl.when(i == 0)
        def _():
            loss_ref[...] = jnp.broadcast_to(lpart, loss_ref.shape)
            dg_ref[...] = gpart

        @pl.when(i > 0)
        def _():
            loss_ref[...] += jnp.broadcast_to(lpart, loss_ref.shape)
            dg_ref[...] += gpart

        dxhat = dy * gv
        dh_ref[...] = r * (dxhat - xhat * jnp.mean(dxhat * xhat, axis=-1, keepdims=True))

    row = pl.BlockSpec((ts, D), lambda i: (i, 0))
    vec = pl.BlockSpec((1, D), lambda i: (0, 0))
    return pl.pallas_call(
        body, name=name, grid=(S // ts,), in_specs=[row, vec, row],
        out_specs=[pl.BlockSpec((8, 128), lambda i: (0, 0)), row, vec],
        out_shape=[jax.ShapeDtypeStruct((8, 128), F32), jax.ShapeDtypeStruct((S, D), F32),
                   jax.ShapeDtypeStruct((1, D), F32)],
        compiler_params=_cparams(("arbitrary",)),
    )(h, g, tgt)


def _sigmoid(x):
    return 1.0 / (1.0 + jnp.exp(-x))


FFN_CHUNK = 1408
FFN_CHUNKS = D_FF // FFN_CHUNK
FFN_BWD_PIECE = 384
FFN_BWD_SLAB = 256


def _ffn_fwd_fused(h, g, w_gu, w_down, layer, *, name):
    S, D = h.shape
    tm = _div_tile(S, 512, 16)
    tf, nj = FFN_CHUNK, FFN_CHUNKS

    def body(h_ref, g_ref, wg_ref, wu_ref, wd_ref, h2_ref, u_ref, gate_ref, up_ref, a_ref, u_s, acc):
        j = pl.program_id(1)

        @pl.when(j == 0)
        def _():
            xv = h_ref[...]
            r = lax.rsqrt(jnp.mean(xv * xv, axis=-1, keepdims=True) + RMS_EPS)
            u = (xv * r * g_ref[...]).astype(BF16)
            u_s[...] = u
            u_ref[...] = u

        u = u_s[...]
        gate = jnp.dot(u, wg_ref[...], preferred_element_type=F32)
        up = jnp.dot(u, wu_ref[...], preferred_element_type=F32)
        gate_ref[...] = gate.astype(BF16)
        up_ref[...] = up.astype(BF16)
        a = (gate * _sigmoid(gate) * up).astype(BF16)
        a_ref[...] = a
        p = jnp.dot(a, wd_ref[...], preferred_element_type=F32)

        @pl.when(j == 0)
        def _():
            acc[...] = p

        @pl.when(j > 0)
        def _():
            acc[...] += p

        @pl.when(j == nj - 1)
        def _():
            h2_ref[...] = h_ref[...] + 0.5 * acc[...]

    row = pl.BlockSpec((tm, D), lambda i, j: (i, 0))
    chunk = pl.BlockSpec((tm, tf), lambda i, j: (i, j))
    hidden = jax.ShapeDtypeStruct((S, D_FF), BF16)
    return pl.pallas_call(
        body, name=name, grid=(S // tm, nj),
        in_specs=[row, pl.BlockSpec((1, D), lambda i, j: (0, 0)),
                  pl.BlockSpec((None, D, tf), lambda i, j: (layer, 0, j)),
                  pl.BlockSpec((None, D, tf), lambda i, j: (layer, 0, nj + j)),
                  pl.BlockSpec((None, tf, D), lambda i, j: (layer, j, 0))],
        out_specs=[row, row, chunk, chunk, chunk],
        out_shape=[jax.ShapeDtypeStruct((S, D), F32), jax.ShapeDtypeStruct((S, D), BF16), hidden, hidden, hidden],
        scratch_shapes=[pltpu.VMEM((tm, D), BF16), pltpu.VMEM((tm, D), F32)],
        compiler_params=_cparams(("parallel", "arbitrary")),
    )(h, g, w_gu, w_gu, w_down)


def _ffn_bwd_fused(dh, h, g, gate, up, w_gu, w_down, layer, *, name):
    S, D = h.shape
    tm = _div_tile(S, 512, FFN_BWD_SLAB)
    tf = FFN_CHUNK
    nj = D_FF // tf
    slab = min(FFN_BWD_SLAB, tm)
    nt = (((1,), (1,)), ((), ()))
    pieces = [(c0, min(FFN_BWD_PIECE, tf - c0)) for c0 in range(0, tf, FFN_BWD_PIECE)]

    def body(dh_ref, h_ref, g_ref, gate_ref, up_ref, wg_ref, wu_ref, wd_ref, dx_ref, dg_ref, dgate_ref, dup_ref,
             dy_s, acc):
        i = pl.program_id(0)
        j = pl.program_id(1)

        @pl.when(j == 0)
        def _():
            for r0 in range(0, tm, slab):
                rows = pl.ds(r0, slab)
                dy_s[rows, :] = (0.5 * dh_ref[rows, :]).astype(BF16)

        p = None
        for c0, cw in pieces:
            cols = pl.ds(c0, cw)
            da = lax.dot_general(dy_s[...], wd_ref[cols, :], nt, preferred_element_type=F32)
            gt = gate_ref[:, cols].astype(F32)
            sg = _sigmoid(gt)
            dgate = (da * up_ref[:, cols].astype(F32) * sg * (1.0 + gt * (1.0 - sg))).astype(BF16)
            dup = (da * gt * sg).astype(BF16)
            dgate_ref[:, cols] = dgate
            dup_ref[:, cols] = dup
            q = (lax.dot_general(dgate, wg_ref[:, cols], nt, preferred_element_type=F32)
                 + lax.dot_general(dup, wu_ref[:, cols], nt, preferred_element_type=F32))
            p = q if p is None else p + q

        @pl.when(j == 0)
        def _():
            acc[...] = p

        @pl.when(j > 0)
        def _():
            acc[...] += p

        @pl.when(j == nj - 1)
        def _():
            part = jnp.zeros((1, D), F32)
            for r0 in range(0, tm, slab):
                rows = pl.ds(r0, slab)
                xv = h_ref[rows, :]
                du = acc[rows, :]
                r = lax.rsqrt(jnp.mean(xv * xv, axis=-1, keepdims=True) + RMS_EPS)
                xhat = xv * r
                part = part + jnp.sum(du * xhat, axis=0, keepdims=True)
                dxhat = du * g_ref[...]
                dx_ref[rows, :] = dh_ref[rows, :] + r * (
                    dxhat - xhat * jnp.mean(dxhat * xhat, axis=-1, keepdims=True))

            @pl.when(i == 0)
            def _():
                dg_ref[...] = part

            @pl.when(i > 0)
            def _():
                dg_ref[...] += part

    row = pl.BlockSpec((tm, D), lambda i, j: (i, 0))
    vec = pl.BlockSpec((1, D), lambda i, j: (0, 0))
    chunk = pl.BlockSpec((tm, tf), lambda i, j: (i, j))
    hidden = jax.ShapeDtypeStruct((S, D_FF), BF16)
    return pl.pallas_call(
        body, name=name, grid=(S // tm, nj),
        in_specs=[row, row, vec, chunk, chunk,
                  pl.BlockSpec((None, D, tf), lambda i, j: (layer, 0, j)),
                  pl.BlockSpec((None, D, tf), lambda i, j: (layer, 0, nj + j)),
                  pl.BlockSpec((None, tf, D), lambda i, j: (layer, j, 0))],
        out_specs=[row, vec, chunk, chunk],
        out_shape=[jax.ShapeDtypeStruct((S, D), F32), jax.ShapeDtypeStruct((1, D), F32), hidden, hidden],
        scratch_shapes=[pltpu.VMEM((tm, D), BF16), pltpu.VMEM((tm, D), F32)],
        compiler_params=_cparams(("arbitrary", "arbitrary")),
    )(dh, h, g, gate, up, w_gu, w_gu, w_down)


CONV_HALO = 32
CONV_SUB_ROWS = 128


def _conv_a_fwd(z, w, bias, ln_g, ln_b, *, name):
    S = z.shape[0]
    C = CONV_A_CH
    ts = _div_tile(S, 256, 32)

    def body(val_ref, gate_ref, w_ref, b_ref, g_ref, lb_ref, c_ref, act_ref, win):
        i = pl.program_id(0)

        @pl.when(i == 0)
        def _():
            win[pl.ds(0, CONV_HALO), :] = jnp.zeros((CONV_HALO, C), F32)

        @pl.when(i > 0)
        def _():
            win[pl.ds(0, CONV_HALO), :] = win[pl.ds(ts, CONV_HALO), :]

        a = val_ref[...].astype(F32) * _sigmoid(gate_ref[...].astype(F32))
        win[pl.ds(CONV_HALO, ts), :] = a
        rs = min(CONV_SUB_ROWS, ts)
        for cb in range(C // 128):
            lanes = pl.ds(128 * cb, 128)
            for rt in range(ts // rs):
                sub = jnp.broadcast_to(b_ref[:, lanes], (rs, 128))
                for k in range(CONV_A_WIDTH):
                    sub = sub + w_ref[pl.ds(k, 1), lanes] * win[
                        pl.ds(CONV_HALO - (CONV_A_WIDTH - 1) + k + rs * rt, rs), lanes]
                c_ref[pl.ds(rs * rt, rs), lanes] = sub
        acc = c_ref[...]
        mu = jnp.mean(acc, axis=-1, keepdims=True)
        xc = acc - mu
        var = jnp.mean(xc * xc, axis=-1, keepdims=True)
        ln = xc * lax.rsqrt(var + LN_EPS) * g_ref[...] + lb_ref[...]
        act_ref[...] = (ln * _sigmoid(ln)).astype(BF16)

    row = lambda col: pl.BlockSpec((ts, C), lambda i, col=col: (i, col))
    vec = pl.BlockSpec((1, C), lambda i: (0, 0))
    return pl.pallas_call(
        body, name=name, grid=(S // ts,),
        in_specs=[row(0), row(1), pl.BlockSpec((32, C), lambda i: (0, 0)), vec, vec, vec],
        out_specs=[row(0), row(0)],
        out_shape=[jax.ShapeDtypeStruct((S, C), F32), jax.ShapeDtypeStruct((S, C), BF16)],
        scratch_shapes=[pltpu.VMEM((ts + CONV_HALO, C), F32)],
        compiler_params=_cparams(("arbitrary",)),
    )(z, z, w, bias, ln_g, ln_b)


def _conv_a_bwd(z, c, dcat, w, ln_g, ln_b, *, name):
    S = z.shape[0]
    C = CONV_A_CH
    ts = _div_tile(S, 256, 32)
    n = S // ts

    def body(val_ref, gate_ref, c_ref, da_ref, w_ref, g_ref, lb_ref, dz_ref, small_ref, win, a_s, da_s, dw8):
        i = pl.program_id(0)

        @pl.when(i == 0)
        def _():
            win[pl.ds(ts, CONV_HALO), :] = jnp.zeros((CONV_HALO, C), F32)
            small_ref[...] = jnp.zeros(small_ref.shape, F32)
            dw8[...] = jnp.zeros(dw8.shape, F32)

        @pl.when(i > 0)
        def _():
            win[pl.ds(ts, CONV_HALO), :] = win[pl.ds(0, CONV_HALO), :]

        cv = c_ref[...]
        gv = g_ref[...]
        mu = jnp.mean(cv, axis=-1, keepdims=True)
        xc = cv - mu
        var = jnp.mean(xc * xc, axis=-1, keepdims=True)
        rstd = lax.rsqrt(var + LN_EPS)
        xhat = xc * rstd
        ln = xhat * gv + lb_ref[...]
        sg = _sigmoid(ln)
        dln = da_ref[...].astype(F32) * (sg * (1.0 + ln * (1.0 - sg)))
        small_ref[pl.ds(33, 1), :] += jnp.sum(dln * xhat, axis=0, keepdims=True)
        small_ref[pl.ds(34, 1), :] += jnp.sum(dln, axis=0, keepdims=True)
        dxhat = dln * gv
        dc = rstd * (dxhat - jnp.mean(dxhat, axis=-1, keepdims=True)
                     - xhat * jnp.mean(dxhat * xhat, axis=-1, keepdims=True))
        small_ref[pl.ds(32, 1), :] += jnp.sum(dc, axis=0, keepdims=True)
        win[pl.ds(0, ts), :] = dc

        val = val_ref[...].astype(F32)
        sgg = _sigmoid(gate_ref[...].astype(F32))
        a_s[...] = val * sgg
        rs = min(CONV_SUB_ROWS, ts)
        for cb in range(C // 128):
            lanes = pl.ds(128 * cb, 128)
            for rt in range(ts // rs):
                a_sub = a_s[pl.ds(rs * rt, rs), lanes]
                da = jnp.zeros((rs, 128), F32)
                for k in range(CONV_A_WIDTH):
                    sh = win[pl.ds(CONV_A_WIDTH - 1 - k + rs * rt, rs), lanes]
                    da = da + w_ref[pl.ds(k, 1), lanes] * sh
                    prod = a_sub * sh
                    part = prod[0:8]
                    for r in range(1, rs // 8):
                        part = part + prod[8 * r:8 * r + 8]
                    dw8[pl.ds(8 * k, 8), lanes] += part
                da_s[pl.ds(rs * rt, rs), lanes] = da
        da = da_s[...]
        dz_ref[:, pl.ds(0, C)] = (da * sgg).astype(BF16)
        dz_ref[:, pl.ds(C, C)] = (da * val * sgg * (1.0 - sgg)).astype(BF16)

        @pl.when(i == n - 1)
        def _():
            for k in range(CONV_A_WIDTH):
                small_ref[pl.ds(k, 1), :] = jnp.sum(dw8[pl.ds(8 * k, 8), :], axis=0, keepdims=True)

    row = lambda col: pl.BlockSpec((ts, C), lambda i, col=col: (n - 1 - i, col))
    vec = pl.BlockSpec((1, C), lambda i: (0, 0))
    return pl.pallas_call(
        body, name=name, grid=(n,),
        in_specs=[row(0), row(1), row(0), row(0), pl.BlockSpec((32, C), lambda i: (0, 0)), vec, vec],
        out_specs=[pl.BlockSpec((ts, 2 * C), lambda i: (n - 1 - i, 0)), pl.BlockSpec((40, C), lambda i: (0, 0))],
        out_shape=[jax.ShapeDtypeStruct((S, 2 * C), BF16), jax.ShapeDtypeStruct((40, C), F32)],
        scratch_shapes=[pltpu.VMEM((ts + CONV_HALO, C), F32), pltpu.VMEM((ts, C), F32), pltpu.VMEM((ts, C), F32),
                        pltpu.VMEM((8 * 32, C), F32)],
        compiler_params=_cparams(("arbitrary",)),
    )(z, z, c, dcat, w, ln_g, ln_b)


SC_HALO = 8


def _sconv_fwd(z, w, *, name):
    S = z.shape[0]
    C = SC_CH
    ts = _div_tile(S, 256, 16)

    def body(gb_ref, gc_ref, v_ref, w_ref, y_ref, cc_ref, win):
        i = pl.program_id(0)

        @pl.when(i == 0)
        def _():
            win[pl.ds(0, SC_HALO), :] = jnp.zeros((SC_HALO, C), F32)

        @pl.when(i > 0)
        def _():
            win[pl.ds(0, SC_HALO), :] = win[pl.ds(ts, SC_HALO), :]

        win[pl.ds(SC_HALO, ts), :] = gc_ref[...].astype(F32) * v_ref[...].astype(F32)
        acc = jnp.zeros((ts, C), F32)
        for k in range(SC_WIDTH):
            acc = acc + w_ref[pl.ds(k, 1), :] * win[pl.ds(SC_HALO - (SC_WIDTH - 1) + k, ts), :]
        cc_ref[...] = acc.astype(BF16)
        y_ref[...] = (gb_ref[...].astype(F32) * acc).astype(BF16)

    row = lambda col: pl.BlockSpec((ts, C), lambda i, col=col: (i, col))
    return pl.pallas_call(
        body, name=name, grid=(S // ts,),
        in_specs=[row(0), row(1), row(2), pl.BlockSpec((8, C), lambda i: (0, 0))],
        out_specs=[row(0), row(0)],
        out_shape=[jax.ShapeDtypeStruct((S, C), BF16), jax.ShapeDtypeStruct((S, C), BF16)],
        scratch_shapes=[pltpu.VMEM((ts + SC_HALO, C), F32)],
        compiler_params=_cparams(("arbitrary",)),
    )(z, z, z, w)


def _sconv_bwd(z, cc, dy, w, *, name):
    S = z.shape[0]
    C = SC_CH
    ts = _div_tile(S, 256, 16)
    n = S // ts

    def body(gb_ref, gc_ref, v_ref, cc_ref, dy_ref, w_ref, dz_ref, dw_ref, win):
        i = pl.program_id(0)

        @pl.when(i == 0)
        def _():
            win[pl.ds(ts, SC_HALO), :] = jnp.zeros((SC_HALO, C), F32)
            dw_ref[...] = jnp.zeros(dw_ref.shape, F32)

        @pl.when(i > 0)
        def _():
            win[pl.ds(ts, SC_HALO), :] = win[pl.ds(0, SC_HALO), :]

        dyv = dy_ref[...].astype(F32)
        gb = gb_ref[...].astype(F32)
        gc = gc_ref[...].astype(F32)
        val = v_ref[...].astype(F32)
        dz_ref[:, pl.ds(0, C)] = (dyv * cc_ref[...].astype(F32)).astype(BF16)
        win[pl.ds(0, ts), :] = dyv * gb
        cv = gc * val
        dcv = jnp.zeros((ts, C), F32)
        for k in range(SC_WIDTH):
            sh = win[pl.ds(SC_WIDTH - 1 - k, ts), :]
            dcv = dcv + w_ref[pl.ds(k, 1), :] * sh
            dw_ref[pl.ds(k, 1), :] += jnp.sum(cv * sh, axis=0, keepdims=True)
        dz_ref[:, pl.ds(C, C)] = (dcv * val).astype(BF16)
        dz_ref[:, pl.ds(2 * C, C)] = (dcv * gc).astype(BF16)

    row = lambda col: pl.BlockSpec((ts, C), lambda i, col=col: (n - 1 - i, col))
    return pl.pallas_call(
        body, name=name, grid=(n,),
        in_specs=[row(0), row(1), row(2), row(0), row(0), pl.BlockSpec((8, C), lambda i: (0, 0))],
        out_specs=[pl.BlockSpec((ts, 3 * C), lambda i: (n - 1 - i, 0)), pl.BlockSpec((8, C), lambda i: (0, 0))],
        out_shape=[jax.ShapeDtypeStruct((S, 3 * C), BF16), jax.ShapeDtypeStruct((8, C), F32)],
        scratch_shapes=[pltpu.VMEM((ts + SC_HALO, C), F32)],
        compiler_params=_cparams(("arbitrary",)),
    )(z, z, z, cc, dy, w)


SWA_Q_COL = 2
SWA_SLOPES = [2.0 ** (-8.0 * (h + 1) / SWA_HEADS) for h in range(SWA_HEADS)]
SWA_SCALE = HEAD_DIM ** -0.5


SWA_GROUP_ROWS = SWA_GROUP * WINDOW


def _swa_masks():
    shape = (SWA_GROUP_ROWS, 2 * WINDOW)
    ii = lax.broadcasted_iota(jnp.int32, shape, 0)
    jj = lax.broadcasted_iota(jnp.int32, shape, 1)
    dist = (ii & (WINDOW - 1)) + WINDOW - jj
    valid = (dist >= 0) & (dist < WINDOW)
    grp = lax.broadcasted_iota(jnp.int32, (SWA_GROUP_ROWS, 1), 0) // WINDOW
    return dist.astype(F32), valid, jj, grp


def _by_group(grp, vals):
    out = jnp.full(grp.shape, vals[SWA_GROUP - 1], F32)
    for g in range(SWA_GROUP - 2, -1, -1):
        out = jnp.where(grp == g, vals[g], out)
    return out


def _stack_heads(ref, rows, kv):
    return jnp.concatenate([ref[rows, pl.ds(HEAD_DIM * (kv * SWA_GROUP + g), HEAD_DIM)] for g in range(SWA_GROUP)],
                           axis=0)


def _swa_probs(qg, kk, sink, slope, distf, valid):
    s = lax.dot_general(qg, kk, (((1,), (1,)), ((), ())), preferred_element_type=F32) * SWA_SCALE
    s = s - slope * distf
    s = jnp.where(valid, s, NEG_BIG)
    m = jnp.maximum(jnp.max(s, axis=-1, keepdims=True), sink)
    p = jnp.exp(s - m)
    l = jnp.sum(p, axis=-1, keepdims=True) + jnp.exp(sink - m)
    return p, m, l


def _swa_fwd(z, kpad, vpad, sinks, *, name):
    S = z.shape[0]
    tq = _div_tile(S, 256, 128)
    nblk = tq // WINDOW
    W = WINDOW

    def body(sink_ref, q_ref, k_ref, v_ref, o_ref):
        i = pl.program_id(0)
        distf, valid0, jj, grp = _swa_masks()
        for kv in range(SWA_KV_HEADS):
            heads = range(kv * SWA_GROUP, (kv + 1) * SWA_GROUP)
            sink = _by_group(grp, [sink_ref[h] for h in heads])
            slope = _by_group(grp, [SWA_SLOPES[h] for h in heads])
            for b in range(nblk):
                nb = i * nblk + b
                start = pl.multiple_of(nb * W, W)
                rows = pl.ds(W * b, W)
                valid = valid0 & ((jj >= W) | (nb > 0))
                kk = k_ref[pl.ds(start, 2 * W), pl.ds(HEAD_DIM * kv, HEAD_DIM)]
                vv = v_ref[pl.ds(start, 2 * W), pl.ds(HEAD_DIM * kv, HEAD_DIM)]
                p, m, l = _swa_probs(_stack_heads(q_ref, rows, kv), kk, sink, slope, distf, valid)
                o = (jnp.dot(p.astype(BF16), vv, preferred_element_type=F32) / l).astype(BF16)
                for g, h in enumerate(heads):
                    o_ref[rows, pl.ds(HEAD_DIM * h, HEAD_DIM)] = o[W * g:W * (g + 1)]

    full = pl.BlockSpec((S + W, 2 * HEAD_DIM), lambda i: (0, 0))
    return pl.pallas_call(
        body, name=name, grid=(S // tq,),
        in_specs=[pl.BlockSpec(memory_space=pltpu.SMEM), pl.BlockSpec((tq, 512), lambda i: (i, SWA_Q_COL)), full, full],
        out_specs=pl.BlockSpec((tq, 512), lambda i: (i, 0)),
        out_shape=jax.ShapeDtypeStruct((S, 512), BF16),
        compiler_params=_cparams(("parallel",)),
    )(sinks, z, kpad, vpad)


def _swa_bwd(z, kpad, vpad, sinks, dcat, *, name):
    S = z.shape[0]
    tq = _div_tile(S, 256, 128)
    nblk = tq // WINDOW
    W = WINDOW

    def body(sink_ref, q_ref, k_ref, v_ref, do_ref, dq_ref, dk_ref, dv_ref, ds_ref):
        i = pl.program_id(0)

        @pl.when(i == 0)
        def _():
            dk_ref[...] = jnp.zeros(dk_ref.shape, F32)
            dv_ref[...] = jnp.zeros(dv_ref.shape, F32)
            ds_ref[...] = jnp.zeros(ds_ref.shape, F32)

        distf, valid0, jj, grp = _swa_masks()
        tn = (((0,), (0,)), ((), ()))
        for kv in range(SWA_KV_HEADS):
            heads = range(kv * SWA_GROUP, (kv + 1) * SWA_GROUP)
            sink = _by_group(grp, [sink_ref[h] for h in heads])
            slope = _by_group(grp, [SWA_SLOPES[h] for h in heads])
            for b in range(nblk):
                nb = i * nblk + b
                start = pl.multiple_of(nb * W, W)
                rows = pl.ds(W * b, W)
                valid = valid0 & ((jj >= W) | (nb > 0))
                kk = k_ref[pl.ds(start, 2 * W), pl.ds(HEAD_DIM * kv, HEAD_DIM)]
                vv = v_ref[pl.ds(start, 2 * W), pl.ds(HEAD_DIM * kv, HEAD_DIM)]
                qg = _stack_heads(q_ref, rows, kv)
                dog = _stack_heads(do_ref, rows, kv)
                p, m, l = _swa_probs(qg, kk, sink, slope, distf, valid)
                inv_l = 1.0 / l
                pn = p * inv_l
                dp = lax.dot_general(dog, vv, (((1,), (1,)), ((), ())), preferred_element_type=F32)
                delta = jnp.sum(pn * dp, axis=-1, keepdims=True)
                dsc = (pn * (dp - delta)).astype(BF16)
                dsink = jnp.exp(sink - m) * inv_l * delta
                dq = (jnp.dot(dsc, kk, preferred_element_type=F32) * SWA_SCALE).astype(BF16)
                for g, h in enumerate(heads):
                    ds_ref[pl.ds(h, 1), :] += jnp.broadcast_to(
                        -jnp.sum(dsink[W * g:W * (g + 1)], axis=0, keepdims=True), (1, 128))
                    dq_ref[rows, pl.ds(HEAD_DIM * h, HEAD_DIM)] = dq[W * g:W * (g + 1)]
                dk_ref[pl.ds(start, 2 * W), pl.ds(HEAD_DIM * kv, HEAD_DIM)] += lax.dot_general(
                    dsc, qg, tn, preferred_element_type=F32) * SWA_SCALE
                dv_ref[pl.ds(start, 2 * W), pl.ds(HEAD_DIM * kv, HEAD_DIM)] += lax.dot_general(
                    pn.astype(BF16), dog, tn, preferred_element_type=F32)

    full = pl.BlockSpec((S + W, 2 * HEAD_DIM), lambda i: (0, 0))
    return pl.pallas_call(
        body, name=name, grid=(S // tq,),
        in_specs=[pl.BlockSpec(memory_space=pltpu.SMEM), pl.BlockSpec((tq, 512), lambda i: (i, SWA_Q_COL)), full, full,
                  pl.BlockSpec((tq, 512), lambda i: (i, 1))],
        out_specs=[pl.BlockSpec((tq, 512), lambda i: (i, 0)), full, full, pl.BlockSpec((8, 128), lambda i: (0, 0))],
        out_shape=[jax.ShapeDtypeStruct((S, 512), BF16), jax.ShapeDtypeStruct((S + W, 2 * HEAD_DIM), F32),
                   jax.ShapeDtypeStruct((S + W, 2 * HEAD_DIM), F32), jax.ShapeDtypeStruct((8, 128), F32)],
        compiler_params=_cparams(("arbitrary",)),
    )(sinks, z, kpad, vpad, dcat)


XA_SCALE = XA_HEAD_DIM ** -0.5


def _xa_probs(qh, kh):
    s = lax.dot_general(qh, kh, (((1,), (1,)), ((), ())), preferred_element_type=F32) * XA_SCALE
    m = jnp.max(s, axis=-1, keepdims=True)
    p = jnp.exp(s - m)
    return p, jnp.sum(p, axis=-1, keepdims=True)


def _xa_fwd(q, kv, *, name):
    S, D = q.shape
    M = kv.shape[0]
    ts = _div_tile(S, 512, 16)
    HD = XA_HEAD_DIM

    def body(q_ref, k_ref, v_ref, o_ref):
        for h in range(XA_HEADS):
            qh = q_ref[:, pl.ds(HD * h, HD)]
            p, l = _xa_probs(qh, k_ref[:, pl.ds(HD * h, HD)])
            o = jnp.dot(p.astype(BF16), v_ref[:, pl.ds(HD * h, HD)], preferred_element_type=F32) / l
            o_ref[:, pl.ds(HD * h, HD)] = o.astype(BF16)

    return pl.pallas_call(
        body, name=name, grid=(S // ts,),
        in_specs=[pl.BlockSpec((ts, D), lambda i: (i, 0)), pl.BlockSpec((M, D), lambda i: (0, 0)),
                  pl.BlockSpec((M, D), lambda i: (0, 1))],
        out_specs=pl.BlockSpec((ts, D), lambda i: (i, 0)),
        out_shape=jax.ShapeDtypeStruct((S, D), BF16),
        compiler_params=_cparams(("parallel",)),
    )(q, kv, kv)


def _xa_bwd(q, kv, do, *, name):
    S, D = q.shape
    M = kv.shape[0]
    ts = _div_tile(S, 512, 16)
    HD = XA_HEAD_DIM

    def body(q_ref, k_ref, v_ref, do_ref, dq_ref, dkv_ref):
        i = pl.program_id(0)

        @pl.when(i == 0)
        def _():
            dkv_ref[...] = jnp.zeros(dkv_ref.shape, F32)

        for h in range(XA_HEADS):
            qh = q_ref[:, pl.ds(HD * h, HD)]
            kh = k_ref[:, pl.ds(HD * h, HD)]
            vh = v_ref[:, pl.ds(HD * h, HD)]
            doh = do_ref[:, pl.ds(HD * h, HD)]
            p, l = _xa_probs(qh, kh)
            pn = p * (1.0 / l)
            dp = lax.dot_general(doh, vh, (((1,), (1,)), ((), ())), preferred_element_type=F32)
            delta = jnp.sum(pn * dp, axis=-1, keepdims=True)
            dsc = (pn * (dp - delta)).astype(BF16)
            dq_ref[:, pl.ds(HD * h, HD)] = (jnp.dot(dsc, kh, preferred_element_type=F32) * XA_SCALE).astype(BF16)
            dkv_ref[:, pl.ds(HD * h, HD)] += lax.dot_general(
                dsc, qh, (((0,), (0,)), ((), ())), preferred_element_type=F32) * XA_SCALE
            dkv_ref[:, pl.ds(D + HD * h, HD)] += lax.dot_general(
                pn.astype(BF16), doh, (((0,), (0,)), ((), ())), preferred_element_type=F32)

    row = pl.BlockSpec((ts, D), lambda i: (i, 0))
    return pl.pallas_call(
        body, name=name, grid=(S // ts,),
        in_specs=[row, pl.BlockSpec((M, D), lambda i: (0, 0)), pl.BlockSpec((M, D), lambda i: (0, 1)), row],
        out_specs=[row, pl.BlockSpec((M, 2 * D), lambda i: (0, 0))],
        out_shape=[jax.ShapeDtypeStruct((S, D), BF16), jax.ShapeDtypeStruct((M, 2 * D), F32)],
        compiler_params=_cparams(("arbitrary",)),
    )(q, kv, kv, do)


def _adam_math(w, g, m, v):
    m = ADAM_B1 * m + (1.0 - ADAM_B1) * g
    v = ADAM_B2 * v + (1.0 - ADAM_B2) * (g * g)
    m_hat = m / (1.0 - ADAM_B1 ** ADAM_STEP)
    v_hat = v / (1.0 - ADAM_B2 ** ADAM_STEP)
    delta = -ADAM_LR * (m_hat / (jnp.sqrt(v_hat) + ADAM_EPS) + ADAM_WD * w)
    return delta, m, v


def _adamw_layers(w, m, v, gsrc, *, name):
    L, A, B = w.shape
    tr = _div_tile(A, max(8, ADAM_TILE_ELEMS // B // 8 * 8))
    nt = A // tr
    flat = [a for srcs in gsrc for a in srcs]
    owner = [l for l, srcs in enumerate(gsrc) for _ in srcs]
    ng = len(flat)

    def body(*refs):
        w_ref, m_ref, v_ref = refs[:3]
        g_refs = refs[3:3 + ng]
        g_ref, d_ref, nm_ref, nv_ref = refs[3 + ng:]
        layer = pl.program_id(0)
        g = None
        for l in range(L):
            gl = None
            for a_ref, o in zip(g_refs, owner):
                if o == l:
                    gl = a_ref[...] if gl is None else gl + a_ref[...]
            g = gl if g is None else jnp.where(layer == l, gl, g)
        d, nm, nv = _adam_math(w_ref[...], g, m_ref[...], v_ref[...])
        g_ref[...] = g
        d_ref[...] = d
        nm_ref[...] = nm
        nv_ref[...] = nv

    def src_spec(o):
        return pl.BlockSpec((None, tr, B),
                            lambda l, i: (0, jnp.where(l == o, i, jnp.where(l > o, nt - 1, 0)), 0))

    spec = pl.BlockSpec((None, tr, B), lambda l, i: (l, i, 0))
    sds = jax.ShapeDtypeStruct((L, A, B), F32)
    return pl.pallas_call(
        body, name=name, grid=(L, nt), in_specs=[spec] * 3 + [src_spec(o) for o in owner], out_specs=[spec] * 4,
        out_shape=[sds] * 4, compiler_params=_cparams(("arbitrary", "arbitrary")),
    )(w, m, v, *flat)


def _adamw_small(ws, ms, vs, gparts, *, name):
    n = len(ws)
    R = gparts.shape[1]

    def body(*refs):
        w_refs, m_refs, v_refs = refs[:n], refs[n:2 * n], refs[2 * n:3 * n]
        gp_ref = refs[3 * n]
        outs = refs[3 * n + 1:7 * n + 1]
        packed = refs[7 * n + 1]
        g = gp_ref[0]
        for k in range(1, N_DEV):
            g = g + gp_ref[k]
        packed[...] = g
        row = 0
        for p in range(n):
            r, c = ws[p].shape
            per, lanes = max(c // 128, 1), min(c, 128)
            g_ref = outs[p]
            for i in range(r):
                for k in range(per):
                    g_ref[pl.ds(i, 1), pl.ds(128 * k, lanes)] = packed[pl.ds(row, 1), pl.ds(0, lanes)]
                    row += 1
            d, nm, nv = _adam_math(w_refs[p][...], g_ref[...], m_refs[p][...], v_refs[p][...])
            outs[n + p][...] = d
            outs[2 * n + p][...] = nm
            outs[3 * n + p][...] = nv

    shapes = [jax.ShapeDtypeStruct(a.shape, F32) for a in ws]
    res = pl.pallas_call(body, name=name, out_shape=shapes * 4, scratch_shapes=[pltpu.VMEM((R, 128), F32)],
                         compiler_params=_cparams())(*ws, *ms, *vs, gparts)
    return res[:n], res[n:2 * n], res[2 * n:3 * n], res[3 * n:]


ANY = pl.BlockSpec(memory_space=pl.ANY)


def _mesh_pos():
    return lax.axis_index("x"), lax.axis_index("y"), lax.axis_index("c")


def _other_chips(x, y):
    return [(1 - x, y), (x, 1 - y), (1 - x, 1 - y)]


LAYOUT = {'ffn1_w_gu': 'col', 'ffn1_w_down': 'stk', 'even_w_in': 'stk', 'even_w_out': 'stk', 'odd_w_in': 'col',
          'odd_w_out': 'stk', 'xa_wq': 'stk', 'xa_wkv': 'col', 'xa_wo': 'stk', 'ffn2_w_gu': 'col',
          'ffn2_w_down': 'stk', 'tiny': 'stk'}
COMM_NAMES = list(LAYOUT)
TINY_ROWS = 48


def _gathered_piece(ref, kind, L, A, h):
    if L == 2:
        return ref.at[h]
    rows = pl.ds(pl.multiple_of(h * (A // 2), 8), A // 2)
    return ref.at[0, rows] if kind == 'col' else ref.at[0, :, rows]


def _chip_part(piece, kind, B, s):
    if kind == 'col':
        return piece.at[:, pl.ds(pl.multiple_of(s * B, 128), B)]
    return piece.at[s]


def _place(shard, layer, kind, chip_idx, out_dtype, *, name, after=None):
    L, A, B = shard.shape
    ta = _div_tile(A, 256, 16)
    extra = [] if after is None else [after]

    def body(s_ref, x_ref, *rest):
        rest[-1][...] = x_ref[...].astype(out_dtype)

    if kind == 'col':
        shape = (1, A, N_CHIPS * B)
        out_spec = pl.BlockSpec((None, ta, B), lambda i, s: (0, i, s[0]))
    else:
        shape = (1, N_CHIPS, A, B)
        out_spec = pl.BlockSpec((None, None, ta, B), lambda i, s: (0, s[0], i, 0))
    grid_spec = pltpu.PrefetchScalarGridSpec(
        num_scalar_prefetch=1, grid=(A // ta,),
        in_specs=[pl.BlockSpec((None, ta, B), lambda i, s: (layer, i, 0))]
        + [pl.BlockSpec(memory_space=pl.ANY)] * len(extra), out_specs=out_spec)
    return pl.pallas_call(
        body, name=name, grid_spec=grid_spec, out_shape=jax.ShapeDtypeStruct(shape, out_dtype),
        compiler_params=_cparams(("parallel",)),
    )(chip_idx, shard, *extra)


HBM = pl.BlockSpec(memory_space=pltpu.HBM)
SEM = pl.BlockSpec(memory_space=pltpu.SEMAPHORE)
DATAFLOW = pltpu.SideEffectType.DATAFLOW_SIDE_EFFECTING


def _own_part_copies(refs, meta, send_sems, recv_sems):
    x, y, c = _mesh_pos()
    cps = []
    for k, (kind, L, A, B) in enumerate(meta):
        for j, (cx, cy) in enumerate(_other_chips(x, y)):
            part = _chip_part(refs[k].at[0], kind, B, 2 * x + y)
            cps.append(pltpu.make_async_remote_copy(
                src_ref=part, dst_ref=part, send_sem=send_sems.at[3 * k + j], recv_sem=recv_sems.at[3 * k + j],
                device_id=(cx, cy, c), device_id_type=MESH))
    return cps


def _half_part_copies(refs, meta, send_sems, recv_sems):
    x, y, c = _mesh_pos()
    cps = []
    for k, (kind, L, A, B) in enumerate(meta):
        for j, (cx, cy) in enumerate(_other_chips(x, y)):
            part = _chip_part(_gathered_piece(refs[k], kind, 1, A, c), kind, B, 2 * x + y)
            cps.append(pltpu.make_async_remote_copy(
                src_ref=part, dst_ref=part, send_sem=send_sems.at[3 * k + j], recv_sem=recv_sems.at[3 * k + j],
                device_id=(cx, cy, c), device_id_type=MESH))
    return cps


def _forward_copies(refs, meta, send_sems, recv_sems):
    x, y, c = _mesh_pos()
    cps = []
    for k, (kind, L, A, B) in enumerate(meta):
        for j, (cx, cy) in enumerate(_other_chips(x, y)):
            part = _chip_part(_gathered_piece(refs[k], kind, 1, A, c), kind, B, 2 * cx + cy)
            cps.append(pltpu.make_async_remote_copy(
                src_ref=part, dst_ref=part, send_sem=send_sems.at[3 * k + j], recv_sem=recv_sems.at[3 * k + j],
                device_id=(x, y, 1 - c), device_id_type=MESH))
    return cps


def _gather_start(fulls, meta, after, tag, copies=_own_part_copies):
    n = len(fulls)

    def body(*refs):
        send_sems, recv_sems = refs[n + 1], refs[n + 2]
        outs = refs[n + 3:2 * n + 3]
        token = refs[2 * n + 3]
        for cp in copies(outs, meta, send_sems, recv_sems):
            cp.start()
        token[...] = jnp.zeros_like(token)

    res = pl.pallas_call(
        body, name=f"ag_start_{tag}", in_specs=[HBM] * n + [pl.BlockSpec(memory_space=pl.ANY)],
        out_specs=(SEM, SEM) + (HBM,) * n + (pl.BlockSpec(memory_space=pltpu.VMEM),),
        out_shape=(pltpu.SemaphoreType.DMA((3 * n,)), pltpu.SemaphoreType.DMA((3 * n,)))
        + tuple(pltpu.HBM(f.shape, f.dtype) for f in fulls) + (jax.ShapeDtypeStruct((8, 128), F32),),
        input_output_aliases={k: 2 + k for k in range(n)},
        compiler_params=pltpu.CompilerParams(has_side_effects=DATAFLOW),
    )(*[pltpu.with_memory_space_constraint(f, pltpu.HBM) for f in fulls], after)
    return res[0], res[1], list(res[2:2 + n]), res[2 + n]


def _gather_wait(send_sems, recv_sems, fulls, meta, after, tag, copies=_own_part_copies):
    n = len(fulls)

    def body(*refs):
        f_refs = refs[:n]
        send_sems, recv_sems = refs[n], refs[n + 1]
        for cp in copies(f_refs, meta, send_sems, recv_sems):
            cp.wait_send()
            cp.wait_recv()

    return pl.pallas_call(
        body, name=f"ag_wait_{tag}", in_specs=[HBM] * n + [SEM, SEM, pl.BlockSpec(memory_space=pl.ANY)],
        out_specs=[HBM] * n, out_shape=[pltpu.HBM(f.shape, f.dtype) for f in fulls],
        input_output_aliases={k: k for k in range(n)},
        compiler_params=pltpu.CompilerParams(has_side_effects=DATAFLOW),
    )(*fulls, send_sems, recv_sems, after)


def _scatter_copies(g_refs, land_refs, meta, send_sems, recv_sems):
    x, y, c = _mesh_pos()
    cps = []
    for k, (kind, L, A, B) in enumerate(meta):
        for j, (cx, cy) in enumerate(_other_chips(x, y)):
            cps.append(pltpu.make_async_remote_copy(
                src_ref=_chip_part(g_refs[k].at[0], kind, B, 2 * cx + cy), dst_ref=land_refs[k].at[j],
                send_sem=send_sems.at[3 * k + j], recv_sem=recv_sems.at[3 * k + j], device_id=(cx, cy, c),
                device_id_type=MESH))
    return cps


def _scatter_start(gs, meta, after, tag):
    n = len(gs)

    def body(*refs):
        send_sems, recv_sems = refs[2 * n + 1], refs[2 * n + 2]
        g_out = refs[2 * n + 3:3 * n + 3]
        lands = refs[3 * n + 3:4 * n + 3]
        token = refs[4 * n + 3]
        for cp in _scatter_copies(g_out, lands, meta, send_sems, recv_sems):
            cp.start()
        token[...] = jnp.zeros_like(token)

    land_shapes = [(3, A, B) for kind, L, A, B in meta]
    lands = [pltpu.with_memory_space_constraint(lax.empty(s, g.dtype), pltpu.HBM) for s, g in zip(land_shapes, gs)]
    res = pl.pallas_call(
        body, name=f"rs_start_{tag}", in_specs=[HBM] * (2 * n) + [pl.BlockSpec(memory_space=pl.ANY)],
        out_specs=(SEM, SEM) + (HBM,) * (2 * n) + (pl.BlockSpec(memory_space=pltpu.VMEM),),
        out_shape=(pltpu.SemaphoreType.DMA((3 * n,)), pltpu.SemaphoreType.DMA((3 * n,)))
        + tuple(pltpu.HBM(g.shape, g.dtype) for g in gs)
        + tuple(pltpu.HBM(s, g.dtype) for s, g in zip(land_shapes, gs)) + (jax.ShapeDtypeStruct((8, 128), F32),),
        input_output_aliases={k: 2 + k for k in range(2 * n)},
        compiler_params=pltpu.CompilerParams(has_side_effects=DATAFLOW),
    )(*[pltpu.with_memory_space_constraint(g, pltpu.HBM) for g in gs], *lands, after)
    return res[0], res[1], list(res[2:2 + n]), list(res[2 + n:2 + 2 * n]), res[2 + 2 * n]


def _scatter_wait(send_sems, recv_sems, gs, lands, meta, after, tag):
    n = len(gs)

    def body(*refs):
        g_refs, land_refs = refs[:n], refs[n:2 * n]
        send_sems, recv_sems = refs[2 * n], refs[2 * n + 1]
        for cp in _scatter_copies(g_refs, land_refs, meta, send_sems, recv_sems):
            cp.wait_send()
            cp.wait_recv()

    both = list(gs) + list(lands)
    res = pl.pallas_call(
        body, name=f"rs_wait_{tag}", in_specs=[HBM] * (2 * n) + [SEM, SEM, pl.BlockSpec(memory_space=pl.ANY)],
        out_specs=[HBM] * (2 * n), out_shape=[pltpu.HBM(a.shape, a.dtype) for a in both],
        input_output_aliases={k: k for k in range(2 * n)},
        compiler_params=pltpu.CompilerParams(has_side_effects=DATAFLOW),
    )(*both, send_sems, recv_sems, after)
    return list(res[:n]), list(res[n:])


def _chip_sum_full(g, got, m, chip_idx, *, name):
    kind, L, A, B = m
    ta = _div_tile(A, 256, 16)

    def body(r_ref, a_ref, b_ref, o_ref):
        acc = a_ref[...].astype(F32)
        for j in range(3):
            acc = acc + b_ref[j].astype(F32)
        o_ref[...] = acc

    if kind == 'col':
        g_spec = pl.BlockSpec((None, ta, B), lambda i, r: (0, i, r[0]))
    else:
        g_spec = pl.BlockSpec((None, None, ta, B), lambda i, r: (0, r[0], i, 0))
    grid_spec = pltpu.PrefetchScalarGridSpec(
        num_scalar_prefetch=1, grid=(A // ta,),
        in_specs=[g_spec, pl.BlockSpec((3, ta, B), lambda i, r: (0, i, 0))],
        out_specs=pl.BlockSpec((None, ta, B), lambda i, r: (0, i, 0)))
    return pl.pallas_call(
        body, name=name, grid_spec=grid_spec, out_shape=jax.ShapeDtypeStruct((1, A, B), F32),
        compiler_params=_cparams(("parallel",)),
    )(chip_idx, g, got)


def _swap_copies(src_refs, land_refs, send_sems, recv_sems):
    x, y, c = _mesh_pos()
    return [pltpu.make_async_remote_copy(src_ref=s, dst_ref=d, send_sem=send_sems.at[k], recv_sem=recv_sems.at[k],
                                         device_id=(x, y, 1 - c), device_id_type=MESH)
            for k, (s, d) in enumerate(zip(src_refs, land_refs))]


def _swap_start(sums, after, tag):
    n = len(sums)

    def body(*refs):
        send_sems, recv_sems = refs[2 * n + 1], refs[2 * n + 2]
        s_out = refs[2 * n + 3:3 * n + 3]
        lands = refs[3 * n + 3:4 * n + 3]
        token = refs[4 * n + 3]
        for cp in _swap_copies(s_out, lands, send_sems, recv_sems):
            cp.start()
        token[...] = jnp.zeros_like(token)

    lands = [pltpu.with_memory_space_constraint(lax.empty(s.shape, s.dtype), pltpu.HBM) for s in sums]
    res = pl.pallas_call(
        body, name=f"rs_swap_start_{tag}", in_specs=[HBM] * (2 * n) + [pl.BlockSpec(memory_space=pl.ANY)],
        out_specs=(SEM, SEM) + (HBM,) * (2 * n) + (pl.BlockSpec(memory_space=pltpu.VMEM),),
        out_shape=(pltpu.SemaphoreType.DMA((n,)), pltpu.SemaphoreType.DMA((n,)))
        + tuple(pltpu.HBM(s.shape, s.dtype) for s in sums) * 2 + (jax.ShapeDtypeStruct((8, 128), F32),),
        input_output_aliases={k: 2 + k for k in range(2 * n)},
        compiler_params=pltpu.CompilerParams(has_side_effects=DATAFLOW),
    )(*[pltpu.with_memory_space_constraint(s, pltpu.HBM) for s in sums], *lands, after)
    return res[0], res[1], list(res[2:2 + n]), list(res[2 + n:2 + 2 * n]), res[2 + 2 * n]


def _swap_wait(send_sems, recv_sems, sums, lands, after, tag):
    n = len(sums)

    def body(*refs):
        send_sems, recv_sems = refs[2 * n], refs[2 * n + 1]
        for cp in _swap_copies(refs[:n], refs[n:2 * n], send_sems, recv_sems):
            cp.wait_send()
            cp.wait_recv()

    both = list(sums) + list(lands)
    res = pl.pallas_call(
        body, name=f"rs_swap_wait_{tag}", in_specs=[HBM] * (2 * n) + [SEM, SEM, pl.BlockSpec(memory_space=pl.ANY)],
        out_specs=[HBM] * (2 * n), out_shape=[pltpu.HBM(a.shape, a.dtype) for a in both],
        input_output_aliases={k: k for k in range(2 * n)},
        compiler_params=pltpu.CompilerParams(has_side_effects=DATAFLOW),
    )(*both, send_sems, recv_sems, after)
    return list(res[:n]), list(res[n:])


def _pair_swap(sums, small, *, tag):
    ns = len(sums)
    with_small = small is not None
    n_in = ns + with_small

    def body(*refs):
        sum_refs = refs[:ns]
        got_refs = refs[n_in:n_in + ns]
        send_sems, recv_sems = refs[2 * n_in], refs[2 * n_in + 1]
        x, y, c = _mesh_pos()
        cps = []
        for k in range(ns):
            cp = pltpu.make_async_remote_copy(
                src_ref=sum_refs[k], dst_ref=got_refs[k], send_sem=send_sems.at[k], recv_sem=recv_sems.at[k],
                device_id=(x, y, 1 - c), device_id_type=MESH)
            cp.start()
            cps.append(cp)
        if with_small:
            small_ref, sm_ref, local_sem = refs[ns], refs[n_in + ns], refs[2 * n_in + 2]
            me = 4 * x + 2 * y + c
            own = pltpu.make_async_copy(small_ref, sm_ref.at[me], local_sem)
            own.start()
            for r in range(1, N_DEV):
                fx, fy, fc = (r >> 2) & 1, (r >> 1) & 1, r & 1
                peer = (1 - x if fx else x, 1 - y if fy else y, 1 - c if fc else c)
                cp = pltpu.make_async_remote_copy(
                    src_ref=small_ref, dst_ref=sm_ref.at[me], send_sem=send_sems.at[ns + r],
                    recv_sem=recv_sems.at[ns + r], device_id=peer, device_id_type=MESH)
                cp.start()
                cps.append(cp)
        for cp in cps:
            cp.wait()
        if with_small:
            own.wait()

    out_shape = [jax.ShapeDtypeStruct(s.shape, s.dtype) for s in sums]
    scratch = [pltpu.SemaphoreType.DMA((ns + N_DEV,)), pltpu.SemaphoreType.DMA((ns + N_DEV,))]
    args = list(sums)
    if with_small:
        out_shape.append(jax.ShapeDtypeStruct((N_DEV,) + small.shape, F32))
        scratch.append(pltpu.SemaphoreType.DMA)
        args.append(small)
    res = pl.pallas_call(
        body, name=f"rs_pair_swap_{tag}", in_specs=[ANY] * n_in, out_specs=[ANY] * n_in, out_shape=out_shape,
        scratch_shapes=scratch,
    )(*args)
    return list(res[:ns]), (res[ns] if with_small else None)


def _tiny_pack(conv_a_w, sc_conv_w):
    lead = conv_a_w.shape[:-2]
    sc = sc_conv_w.reshape(lead + (2 * SC_WIDTH, 128))
    z = lambda r: jnp.zeros(lead + (r, 128), F32)
    return jnp.concatenate([conv_a_w, z(32 - CONV_A_WIDTH), sc, z(TINY_ROWS - 32 - 2 * SC_WIDTH)], axis=-2)


def _tiny_unpack(t):
    lead = t.shape[:-2]
    return t[..., :CONV_A_WIDTH, :], t[..., 32:32 + 2 * SC_WIDTH, :].reshape(lead + (SC_WIDTH, 256))


def _pack_small(d):
    parts = []
    for n in SMALL_NAMES:
        flat = d[n].astype(F32).reshape(-1)
        parts.append(jnp.pad(flat, (0, -flat.shape[0] % 128)))
    flat = jnp.concatenate(parts)
    return jnp.pad(flat, (0, -flat.shape[0] % 1024)).reshape(-1, 128)


def _ffn_fwd(h, g, W, n_gu, n_down, i, tag):
    h2, u, gate, up, a = _ffn_fwd_fused(h, g, W[n_gu][i], W[n_down][i], 0, name=f"{tag}_fwd")
    return h2, (h, u, gate, up, a)


def _ffn_bwd(dh, saved, g, W, n_gu, n_down, i, G, tag):
    h, u, gate, up, a = saved
    dh_in, dg, dgate, dup = _ffn_bwd_fused(dh, h, g, gate, up, W[n_gu][i], W[n_down][i], 0, name=f"{tag}_bwd")
    G[(n_down, i)] = _mm(a, dh, name=f"{tag}_b_wdown", ta=True, tm=1408, tn=1024, tk=1024, scale=0.5)
    tn = FFN_CHUNK
    half = _mm(u, dgate, name=f"{tag}_b_wg", ta=True, tm=1024, tn=tn, tk=2048, stack=(1, 0, None, 2 * D_FF))
    G[(n_gu, i)] = _mm(u, dup, name=f"{tag}_b_wu", ta=True, tm=1024, tn=tn, tk=2048, stack=(1, 0, half, 2 * D_FF),
                       n_map=lambda j: j + D_FF // tn)
    return dh_in, dg


def _xa_block_fwd(h, mem, g, gm, W, i, tag):
    mn = _rms_fwd(mem, gm, name=f"{tag}_mem_norm")
    u, q = _norm_mm(h, g, W['xa_wq'][i], 0, name=f"{tag}_q")
    kv = _mm(mn, W['xa_wkv'][i], b_layer=0, name=f"{tag}_kv", tm=256, tn=1024, tk=1024)
    o = _xa_fwd(q, kv, name=f"{tag}_attn")
    h2 = _mm(o, W['xa_wo'][i], b_layer=0, name=f"{tag}_o", out_dtype=F32, tm=1024, tn=1024, tk=1024, res=h)
    return h2, (h, u, mn, q, kv, o)


def _xa_block_bwd(dh, saved, mem, g, gm, W, i, G, tag):
    h, u, mn, q, kv, o = saved
    do = _mm(dh, W['xa_wo'][i], b_layer=0, name=f"{tag}_b_do", tb=True, tm=1024, tn=1024, tk=1024)
    G[('xa_wo', i)] = _mm(o, dh, name=f"{tag}_b_wo", ta=True, tm=1024, tn=1024, tk=1024)
    dq, dkv = _xa_bwd(q, kv, do, name=f"{tag}_b_attn")
    G[('xa_wq', i)] = _mm(u, dq, name=f"{tag}_b_wq", ta=True, tm=1024, tn=1024, tk=1024)
    dh_in, dg = _mm_norm_bwd(dq, W['xa_wq'][i], 0, h, g, dh, name=f"{tag}_b_du", tk=1024)
    G[('xa_wkv', i)] = _mm(mn, dkv, name=f"{tag}_b_wkv", ta=True, tm=1024, tn=1024, tk=256)
    dmn = _mm(dkv, W['xa_wkv'][i], b_layer=0, name=f"{tag}_b_dmn", tb=True, out_dtype=F32, tm=256, tn=1024, tk=1024)
    _, dgm = _rms_bwd(mem, gm, dmn, None, name=f"{tag}_b_mem_norm")
    return dh_in, dg, dgm


def _pad_conv_w(w, rows):
    return jnp.pad(w.astype(F32), ((0, rows - w.shape[0]), (0, 0)))


def _even_fwd(h, g, W, conv_w, conv_b, ln_g, ln_b, sinks, tag):
    u, z = _norm_mm(h, g, W['even_w_in'][0], 0, name=f"{tag}_in")
    c, act = _conv_a_fwd(z, conv_w, conv_b, ln_g, ln_b, name=f"{tag}_conv")
    kpad = jnp.pad(z[:, 1536:1664], ((WINDOW, 0), (0, 0)))
    vpad = jnp.pad(z[:, 1664:1792], ((WINDOW, 0), (0, 0)))
    o = _swa_fwd(z, kpad, vpad, sinks, name=f"{tag}_swa")
    cat = jnp.concatenate([act, o], axis=-1)
    h2 = _mm(cat, W['even_w_out'][0], b_layer=0, name=f"{tag}_out", out_dtype=F32, tm=1024, tn=1024, tk=1024, res=h)
    return h2, (h, u, z, c, kpad, vpad, cat)


def _even_bwd(dh, saved, g, W, conv_w, ln_g, ln_b, sinks, G, tag):
    h, u, z, c, kpad, vpad, cat = saved
    dcat = _mm(dh, W['even_w_out'][0], b_layer=0, name=f"{tag}_b_dcat", tb=True, tm=1024, tn=1024, tk=1024)
    G[('even_w_out', 0)] = _mm(cat, dh, name=f"{tag}_b_wout", ta=True, tm=1024, tn=1024, tk=1024)
    dz_a, small = _conv_a_bwd(z, c, dcat, conv_w, ln_g, ln_b, name=f"{tag}_b_conv")
    dq, dkp, dvp, dsinks = _swa_bwd(z, kpad, vpad, sinks, dcat, name=f"{tag}_b_swa")
    dz = jnp.concatenate([dz_a, dq, dkp[WINDOW:].astype(BF16), dvp[WINDOW:].astype(BF16)], axis=-1)
    G[('even_w_in', 0)] = _mm(u, dz, name=f"{tag}_b_win", ta=True, tm=1024, tn=1792, tk=1024)
    dh_in, dg = _mm_norm_bwd(dz, W['even_w_in'][0], 0, h, g, dh, name=f"{tag}_b_du", tk=1792)
    grads = dict(mix=dg, conv_a_w=small[:CONV_A_WIDTH], conv_a_b=small[32:33], conv_a_ln_g=small[33:34],
                 conv_a_ln_b=small[34:35], swa_sinks=dsinks[:, 0])
    return dh_in, grads


def _odd_fwd(h, g, W, conv_w, tag):
    u, z = _norm_mm(h, g, W['odd_w_in'][0], 0, name=f"{tag}_in")
    y, cc = _sconv_fwd(z, conv_w, name=f"{tag}_conv")
    h2 = _mm(y, W['odd_w_out'][0], b_layer=0, name=f"{tag}_out", out_dtype=F32, tm=1024, tn=1024, tk=1024, res=h)
    return h2, (h, u, z, y, cc)


def _odd_bwd(dh, saved, g, W, conv_w, G, tag):
    h, u, z, y, cc = saved
    dy = _mm(dh, W['odd_w_out'][0], b_layer=0, name=f"{tag}_b_dy", tb=True, tm=1024, tn=1024, tk=1024)
    G[('odd_w_out', 0)] = _mm(y, dh, name=f"{tag}_b_wout", ta=True, tm=1024, tn=1024, tk=1024)
    dz, dw = _sconv_bwd(z, cc, dy, conv_w, name=f"{tag}_b_conv")
    G[('odd_w_in', 0)] = _mm(u, dz, name=f"{tag}_b_win", ta=True, tm=1024, tn=1024, tk=1024)
    dh_in, dg = _mm_norm_bwd(dz, W['odd_w_in'][0], 0, h, g, dh, name=f"{tag}_b_du", tk=1024)
    return dh_in, dict(mix=dg, sc_conv_w=dw[:SC_WIDTH])


def _local_step(x, mem, tgt, W, need, token, ready, conv_a_w, sc_conv_w, P):
    row = lambda v: v.reshape(1, -1)
    conv_a_w = _pad_conv_w(conv_a_w, 32)
    sc_w = _pad_conv_w(sc_conv_w, 8)
    sinks = P['swa_sinks'][0]

    def arrive(stage, h):
        for n, ws in need(stage, h).items():
            W[n] = W.get(n, []) + ws

    h = x
    saved = []
    for i in range(2):
        t = f"l{i}"
        if i == 1:
            arrive('l1_ffn1', h)
        g1 = row(P['ffn1_norm'][i]) + (token if i == 0 else 0.0)
        h, s1 = _ffn_fwd(h, g1, W, 'ffn1_w_gu', 'ffn1_w_down', i, f"{t}_ffn1")
        arrive(f"{t}_mix", h)
        if i == 0:
            h, s2 = _even_fwd(h, row(P['mix_norm'][i]), W, conv_a_w, P['conv_a_b'], P['conv_a_ln_g'],
                              P['conv_a_ln_b'], sinks, f"{t}_even")
        else:
            h, s2 = _odd_fwd(h, row(P['mix_norm'][i]), W, sc_w, f"{t}_odd")
        h, s3 = _xa_block_fwd(h, mem, row(P['xa_norm'][i]), row(P['xa_mem_norm'][i]), W, i, f"{t}_xa")
        arrive(f"{t}_ffn2", h)
        h, s4 = _ffn_fwd(h, row(P['ffn2_norm'][i]), W, 'ffn2_w_gu', 'ffn2_w_down', i, f"{t}_ffn2")
        saved.append((s1, s2, s3, s4))

    loss, dh, d_final = _final_loss(h, row(P['final_norm']), tgt, name="final_loss")

    G = {}
    gp = {n: [None, None] for n in ('ffn1_norm', 'mix_norm', 'xa_norm', 'xa_mem_norm', 'ffn2_norm')}
    single = {}
    for i in (1, 0):
        t = f"l{i}"
        s1, s2, s3, s4 = saved[i]
        g4 = row(P['ffn2_norm'][i]) + (ready('l1', G) if i == 0 else 0.0)
        dh, gp['ffn2_norm'][i] = _ffn_bwd(dh, s4, g4, W, 'ffn2_w_gu', 'ffn2_w_down', i, G, f"{t}_ffn2")
        dh, gp['xa_norm'][i], gp['xa_mem_norm'][i] = _xa_block_bwd(
            dh, s3, mem, row(P['xa_norm'][i]), row(P['xa_mem_norm'][i]), W, i, G, f"{t}_xa")
        if i == 0:
            dh, g2 = _even_bwd(dh, s2, row(P['mix_norm'][i]), W, conv_a_w, P['conv_a_ln_g'], P['conv_a_ln_b'], sinks,
                               G, f"{t}_even")
        else:
            dh, g2 = _odd_bwd(dh, s2, row(P['mix_norm'][i]), W, sc_w, G, f"{t}_odd")
        gp['mix_norm'][i] = g2.pop('mix')
        single.update(g2)
        g1 = row(P['ffn1_norm'][i]) + (ready('l0_rest', G) if i == 0 else 0.0)
        dh, gp['ffn1_norm'][i] = _ffn_bwd(dh, s1, g1, W, 'ffn1_w_gu', 'ffn1_w_down', i, G, f"{t}_ffn1")

    small = {n: jnp.concatenate(v, axis=0) for n, v in gp.items()}
    small['conv_a_b'] = single['conv_a_b']
    small['conv_a_ln_g'] = single['conv_a_ln_g']
    small['conv_a_ln_b'] = single['conv_a_ln_b']
    small['swa_sinks'] = single['swa_sinks'][None]
    small['final_norm'] = d_final[0]
    small['conv_a_w'] = single['conv_a_w']
    small['sc_conv_w'] = single['sc_conv_w']
    return loss[0, 0], dh, G, small


def kernel(x, mem, ffn1_norm, ffn1_w_gu, ffn1_w_down, mix_norm, even_w_in, conv_a_w, conv_a_b, conv_a_ln_g, conv_a_ln_b, swa_sinks, even_w_out, odd_w_in, sc_conv_w, odd_w_out, xa_norm, xa_mem_norm, xa_wq, xa_wkv, xa_wo, ffn2_norm, ffn2_w_gu, ffn2_w_down, final_norm, loss_target, m_ffn1_norm, m_ffn1_w_gu, m_ffn1_w_down, m_mix_norm, m_even_w_in, m_conv_a_w, m_conv_a_b, m_conv_a_ln_g, m_conv_a_ln_b, m_swa_sinks, m_even_w_out, m_odd_w_in, m_sc_conv_w, m_odd_w_out, m_xa_norm, m_xa_mem_norm, m_xa_wq, m_xa_wkv, m_xa_wo, m_ffn2_norm, m_ffn2_w_gu, m_ffn2_w_down, m_final_norm, v_ffn1_norm, v_ffn1_w_gu, v_ffn1_w_down, v_mix_norm, v_even_w_in, v_conv_a_w, v_conv_a_b, v_conv_a_ln_g, v_conv_a_ln_b, v_swa_sinks, v_even_w_out, v_odd_w_in, v_sc_conv_w, v_odd_w_out, v_xa_norm, v_xa_mem_norm, v_xa_wq, v_xa_wkv, v_xa_wo, v_ffn2_norm, v_ffn2_w_gu, v_ffn2_w_down, v_final_norm):
    w = dict(zip(WEIGHT_NAMES, (ffn1_norm, ffn1_w_gu, ffn1_w_down, mix_norm, even_w_in, conv_a_w, conv_a_b, conv_a_ln_g, conv_a_ln_b, swa_sinks, even_w_out, odd_w_in, sc_conv_w, odd_w_out, xa_norm, xa_mem_norm, xa_wq, xa_wkv, xa_wo, ffn2_norm, ffn2_w_gu, ffn2_w_down, final_norm)))
    m = dict(zip(WEIGHT_NAMES, (m_ffn1_norm, m_ffn1_w_gu, m_ffn1_w_down, m_mix_norm, m_even_w_in, m_conv_a_w, m_conv_a_b, m_conv_a_ln_g, m_conv_a_ln_b, m_swa_sinks, m_even_w_out, m_odd_w_in, m_sc_conv_w, m_odd_w_out, m_xa_norm, m_xa_mem_norm, m_xa_wq, m_xa_wkv, m_xa_wo, m_ffn2_norm, m_ffn2_w_gu, m_ffn2_w_down, m_final_norm)))
    v = dict(zip(WEIGHT_NAMES, (v_ffn1_norm, v_ffn1_w_gu, v_ffn1_w_down, v_mix_norm, v_even_w_in, v_conv_a_w, v_conv_a_b, v_conv_a_ln_g, v_conv_a_ln_b, v_swa_sinks, v_even_w_out, v_odd_w_in, v_sc_conv_w, v_odd_w_out, v_xa_norm, v_xa_mem_norm, v_xa_wq, v_xa_wkv, v_xa_wo, v_ffn2_norm, v_ffn2_w_gu, v_ffn2_w_down, v_final_norm)))
    cx, cy, cc = lax.axis_index("x"), lax.axis_index("y"), lax.axis_index("c")
    chip_idx = (2 * cx + cy).astype(jnp.int32).reshape(1)

    shards = {n: w[n] for n in COMM_NAMES if n != 'tiny'}
    shards['tiny'] = _tiny_pack(conv_a_w, sc_conv_w)
    first =[('ffn1_w_gu', 0), ('ffn1_w_down', 0), ('tiny', 0)]
    stages = {
        'l0_mix': [('even_w_in', 0), ('even_w_out', 0), ('xa_wq', 0), ('xa_wkv', 0), ('xa_wo', 0)],
        'l0_ffn2': [('ffn2_w_gu', 0), ('ffn2_w_down', 0)],
        'l1_ffn1': [('ffn1_w_gu', 1), ('ffn1_w_down', 1)],
        'l1_mix': [('odd_w_in', 0), ('odd_w_out', 0), ('xa_wq', 1), ('xa_wkv', 1), ('xa_wo', 1)],
        'l1_ffn2': [('ffn2_w_gu', 1), ('ffn2_w_down', 1)],
    }
    grad_stages = {'l1': stages['l1_ffn1'] + stages['l1_mix'] + stages['l1_ffn2'],
                   'l0_rest': stages['l0_mix'] + stages['l0_ffn2'], 'l0_ffn1': first}

    def place(items, after=None):
        out = []
        for n, l in items:
            out.append(_place(shards[n], l, LAYOUT[n], chip_idx, F32 if n == 'tiny' else BF16,
                              name=f"place_{n}_{l}", after=after))
            after = out[-1] if after is not None else None
        return out

    def item_meta(items):
        return [(LAYOUT[n], 1) + shards[n].shape[1:] for n, l in items]

    def natural(items, arrays):
        out = {}
        for (n, l), a in zip(items, arrays):
            if n == 'tiny':
                continue
            if n == 'even_w_in':
                out[n] = [a.transpose(0, 2, 1, 3).reshape(1, D_MODEL, -1)]
            else:
                out[n] = [a if LAYOUT[n] == 'col' else a.reshape(1, N_CHIPS * a.shape[2], a.shape[3])]
        return out

    meta_first = item_meta(first)
    send, recv, in_flight, token = _gather_start(place(first), meta_first, chip_idx, "first_ici", _half_part_copies)
    placed, last = {}, token
    for stage, items in stages.items():
        placed[stage] = place(items, last)
        last = placed[stage][-1]
    landed = _gather_wait(send, recv, in_flight, meta_first, last, "first_ici", _half_part_copies)
    send, recv, in_flight, token = _gather_start(landed, meta_first, token, "first_d2d", _forward_copies)
    first_d2d = (send, recv, in_flight)
    gathers = {}
    for stage, items in stages.items():
        send, recv, in_flight, token = _gather_start(placed[stage], item_meta(items), token, stage)
        gathers[stage] = (send, recv, in_flight)
    first_full = _gather_wait(*first_d2d, meta_first, token, "first_d2d", _forward_copies)
    W = natural(first, first_full)
    ca, sc = _tiny_unpack(first_full[-1][0])
    conv_a_full = ca.transpose(1, 0, 2).reshape(CONV_A_WIDTH, CONV_A_CH)
    sc_full = sc.transpose(1, 0, 2).reshape(SC_WIDTH, SC_CH)

    def need(stage, h):
        send, recv, in_flight = gathers[stage]
        items = stages[stage]
        return natural(items, _gather_wait(send, recv, in_flight, item_meta(items), h, stage))

    def gathered_layout(G, item):
        n, l = item
        A, B = shards[n].shape[1:]
        g = G[item]
        if n == 'tiny':
            return g
        if n == 'even_w_in':
            return g.reshape(A, N_CHIPS, B).transpose(1, 0, 2)[None]
        return g.reshape(1, A, N_CHIPS * B) if LAYOUT[n] == 'col' else g.reshape(1, N_CHIPS, A, B)

    scatters, tokens = {}, {}

    def ready(stage, G):
        items = grad_stages[stage]
        send, recv, gs1, lands, tok = _scatter_start([gathered_layout(G, it) for it in items], item_meta(items),
                                                     chip_idx, stage)
        scatters[stage] = (send, recv, gs1, lands)
        tokens[stage] = tok
        return tok[:1, :1]

    loss_part, grad_x, G, g_small = _local_step(x[0], mem[0], loss_target[0], W, need, token[:1, :1], ready,
                                                conv_a_full, sc_full, {n: w[n] for n in SMALL_NAMES})
    G[('tiny', 0)] = _tiny_pack(g_small['conv_a_w'].reshape(CONV_A_WIDTH, N_CHIPS, 128).transpose(1, 0, 2),
                                g_small['sc_conv_w'].reshape(SC_WIDTH, N_CHIPS, 256).transpose(1, 0, 2))[None]
    loss = lax.psum(loss_part, ("x", "y", "c"))

    ready('l0_ffn1', G)
    started = tokens['l0_ffn1']

    def summed(stage, after):
        send, recv, gs1, lands = scatters[stage]
        items = grad_stages[stage]
        sent, landed = _scatter_wait(send, recv, gs1, lands, item_meta(items), after, stage)
        return items, [_chip_sum_full(g, r, m_, chip_idx, name=f"rs_chip_sum_{n}_{l}")
                       for (n, l), g, r, m_ in zip(items, sent, landed, item_meta(items))]

    def adamw(n, g1):
        if n == 'tiny':
            pk = lambda d: _tiny_pack(d['conv_a_w'], d['sc_conv_w'])
            res = [_tiny_unpack(a) for a in _adamw_layers(pk(w), pk(m), pk(v), [g1[('tiny', 0)]], name="adamw_tiny")]
            for k, nn in enumerate(('conv_a_w', 'sc_conv_w')):
                grads[nn], deltas[nn], new_m[nn], new_v[nn] = (r[k] for r in res)
        else:
            gsrc = [g1[(n, l)] for l in range(w[n].shape[0])]
            grads[n], deltas[n], new_m[n], new_v[n] = _adamw_layers(w[n], m[n], v[n], gsrc, name=f"adamw_{n}")

    grads, deltas, new_m, new_v = {}, {}, {}, {}
    sum_of = {}
    for stage in ('l1', 'l0_rest'):
        its, ss = summed(stage, started)
        sum_of.update(zip(its, ss))
    swap_groups = [['even_w_in', 'even_w_out', 'odd_w_in', 'odd_w_out', 'xa_wq', 'xa_wkv', 'xa_wo'],
                   ['ffn2_w_gu', 'ffn2_w_down'], ['ffn1_w_gu', 'ffn1_w_down']]
    swaps, after = [], started
    for gi, names in enumerate(swap_groups):
        its = [it for it in sum_of if it[0] in names]
        send, recv, own, lands, after = _swap_start([sum_of[it] for it in its], after, f"g{gi}")
        swaps.append((its, send, recv, own, lands))
    _, small_parts = _pair_swap([], _pack_small(g_small), tag="small")
    g1 = {}

    def swapped(gi, after):
        its, send, recv, own, lands = swaps[gi]
        mine, theirs = _swap_wait(send, recv, own, lands, after, f"g{gi}")
        g1.update({it: [a, b] for it, a, b in zip(its, mine, theirs)})

    for gi in (0, 1):
        swapped(gi, after)
        for n in swap_groups[gi]:
            adamw(n, g1)
        after = deltas[swap_groups[gi][-1]]

    swapped(2, after)
    its, ss = summed('l0_ffn1', after)
    sib, _ = _pair_swap(ss, None, tag="last")
    g1.update({it: [a, b] for it, a, b in zip(its, ss, sib)})
    for n in ('ffn1_w_gu', 'ffn1_w_down', 'tiny'):
        adamw(n, g1)
    rows2d = lambda d: [d[n].reshape(-1, d[n].shape[-1]) for n in SMALL_NAMES]
    for dst, arrs in zip((grads, deltas, new_m, new_v),
                         _adamw_small(rows2d(w), rows2d(m), rows2d(v), small_parts, name="adamw_small")):
        dst.update({n: a.reshape(w[n].shape) for n, a in zip(SMALL_NAMES, arrs)})

    return (loss, grad_x[None], *[grads[n] for n in WEIGHT_NAMES], *[deltas[n] for n in WEIGHT_NAMES],
            *[new_m[n] for n in WEIGHT_NAMES], *[new_v[n] for n in WEIGHT_NAMES])
```

```python
import jax
import jax.numpy as jnp
from jax import lax
from jax.experimental import pallas as pl
from jax.experimental.pallas import tpu as pltpu

F32 = jnp.float32
BF16 = jnp.bfloat16

D_MODEL = 1024
D_FF = 2816
CONV_A_CH = 512
CONV_A_WIDTH = 31
SWA_HEADS = 8
SWA_KV_HEADS = 2
SWA_GROUP = 4
HEAD_DIM = 64
WINDOW = 128
SC_CH = 1024
SC_WIDTH = 3
XA_HEADS = 4
XA_HEAD_DIM = 256
RMS_EPS = 1e-6
LN_EPS = 1e-5

ADAM_LR = 0.001
ADAM_B1 = 0.9
ADAM_B2 = 0.999
ADAM_EPS = 1e-08
ADAM_WD = 0.01
ADAM_STEP = 10
ADAM_TILE_ELEMS = 384 * 1024

N_CHIPS = 4
N_DEV = 8
NEG_BIG = -1e30
VMEM_LIMIT = 56 * 1024 * 1024
MESH = pl.DeviceIdType.MESH

INPUT_NAMES = ['x', 'mem', 'ffn1_norm', 'ffn1_w_gu', 'ffn1_w_down', 'mix_norm', 'even_w_in', 'conv_a_w', 'conv_a_b',
               'conv_a_ln_g', 'conv_a_ln_b', 'swa_sinks', 'even_w_out', 'odd_w_in', 'sc_conv_w', 'odd_w_out', 'xa_norm',
               'xa_mem_norm', 'xa_wq', 'xa_wkv', 'xa_wo', 'ffn2_norm', 'ffn2_w_gu', 'ffn2_w_down', 'final_norm']
WEIGHT_NAMES = INPUT_NAMES[2:]
BIG = [('ffn1_w_gu', 'col'), ('ffn1_w_down', 'row'), ('even_w_in', 'col'), ('conv_a_w', 'col'), ('even_w_out', 'row'),
       ('odd_w_in', 'col'), ('sc_conv_w', 'col'), ('odd_w_out', 'row'), ('xa_wq', 'row'), ('xa_wkv', 'col'),
       ('xa_wo', 'row'), ('ffn2_w_gu', 'col'), ('ffn2_w_down', 'row')]
BIG_NAMES = [n for n, _ in BIG]
SMALL_NAMES = [n for n in WEIGHT_NAMES if n not in BIG_NAMES]


def _cparams(sem=None, vmem=VMEM_LIMIT):
    kw = dict(vmem_limit_bytes=vmem)
    if sem is not None:
        kw['dimension_semantics'] = sem
    return pltpu.CompilerParams(**kw)


def _div_tile(n, want, align=8):
    if n <= want:
        return n
    t = (want // align) * align
    while t >= align:
        if n % t == 0:
            return t
        t -= align
    return n


def _mm(a, b, *, name, ta=False, tb=False, out_dtype=BF16, tm=512, tn=512, tk=512, res=None, scale=1.0,
        b_layer=None, stack=None, n_map=None):
    n_map = n_map or (lambda j: j)
    if ta:
        K, M = a.shape
    else:
        M, K = a.shape
    if tb:
        N, K2 = b.shape[-2:]
    else:
        K2, N = b.shape[-2:]
    assert K == K2, (a.shape, b.shape, ta, tb)
    tm = _div_tile(M, tm, 128 if ta else 16)
    tn = _div_tile(N, tn, 128)
    tk = _div_tile(K, tk, 16 if ta else 128)
    nk = K // tk
    a_spec = pl.BlockSpec((tk, tm), lambda i, j, k: (k, i)) if ta else pl.BlockSpec((tm, tk), lambda i, j, k: (i, k))
    if b_layer is None:
        b_spec = pl.BlockSpec((tn, tk), lambda i, j, k: (j, k)) if tb else pl.BlockSpec((tk, tn), lambda i, j, k: (k, j))
    elif tb:
        b_spec = pl.BlockSpec((None, tn, tk), lambda i, j, k: (b_layer, j, k))
    else:
        b_spec = pl.BlockSpec((None, tk, tn), lambda i, j, k: (b_layer, k, j))
    o_spec = pl.BlockSpec((tm, tn), lambda i, j, k: (i, j))
    out_shape = jax.ShapeDtypeStruct((M, N), out_dtype)
    out_spec = o_spec
    aliases = {}
    extra_specs, extra_args = [], ()
    if stack is not None:
        n_layers, layer, buf = stack[:3]
        n_total = stack[3] if len(stack) > 3 else N
        out_shape = jax.ShapeDtypeStruct((n_layers, M, n_total), out_dtype)
        out_spec = pl.BlockSpec((None, tm, tn), lambda i, j, k: (layer, i, n_map(j)))
        if buf is not None:
            extra_specs, extra_args = [pl.BlockSpec(memory_space=pl.ANY)], (buf,)
            aliases = {2 + (res is not None): 0}
    dims = (((0 if ta else 1,), (1 if tb else 0,)), ((), ()))
    has_res = res is not None
    n_extra = len(extra_args)

    def body(*refs):
        if n_extra:
            refs = refs[:2 + has_res] + refs[2 + has_res + n_extra:]
        if has_res:
            a_ref, b_ref, r_ref, o_ref, acc_ref = refs
        else:
            a_ref, b_ref, o_ref, acc_ref = refs
        k = pl.program_id(2)
        p = lax.dot_general(a_ref[...].astype(BF16), b_ref[...].astype(BF16), dims, preferred_element_type=F32)

        @pl.when(k == 0)
        def _():
            acc_ref[...] = p

        @pl.when(k > 0)
        def _():
            acc_ref[...] += p

        @pl.when(k == nk - 1)
        def _():
            r = acc_ref[...] * scale
            if has_res:
                r = r_ref[...] + r
            o_ref[...] = r.astype(out_dtype)

    in_specs = [a_spec, b_spec] + ([o_spec] if has_res else []) + extra_specs
    args = (a, b) + ((res,) if has_res else ()) + extra_args
    return pl.pallas_call(
        body, name=name, grid=(M // tm, N // tn, nk), in_specs=in_specs, out_specs=out_spec,
        out_shape=out_shape, input_output_aliases=aliases,
        scratch_shapes=[pltpu.VMEM((tm, tn), F32)],
        compiler_params=_cparams(("parallel", "parallel", "arbitrary")),
    )(*args)


def _rms_fwd(x, g, *, name):
    S, D = x.shape
    ts = _div_tile(S, 512)

    def body(x_ref, g_ref, o_ref):
        xv = x_ref[...]
        r = lax.rsqrt(jnp.mean(xv * xv, axis=-1, keepdims=True) + RMS_EPS)
        o_ref[...] = (xv * r * g_ref[...]).astype(BF16)

    return pl.pallas_call(
        body, name=name, grid=(S // ts,),
        in_specs=[pl.BlockSpec((ts, D), lambda i: (i, 0)), pl.BlockSpec((1, D), lambda i: (0, 0))],
        out_specs=pl.BlockSpec((ts, D), lambda i: (i, 0)),
        out_shape=jax.ShapeDtypeStruct((S, D), BF16),
        compiler_params=_cparams(("parallel",)),
    )(x, g)


NORM_SLAB = 256


def _norm_mm(h, g, w, layer, *, name):
    S, D = h.shape
    N = w.shape[-1]
    tm = _div_tile(S, 1024, NORM_SLAB)
    slab = min(NORM_SLAB, tm)

    def body(h_ref, g_ref, w_ref, u_ref, z_ref):
        for r0 in range(0, tm, slab):
            rows = pl.ds(r0, slab)
            xv = h_ref[rows, :]
            r = lax.rsqrt(jnp.mean(xv * xv, axis=-1, keepdims=True) + RMS_EPS)
            u = (xv * r * g_ref[...]).astype(BF16)
            u_ref[rows, :] = u
            z_ref[rows, :] = jnp.dot(u, w_ref[...], preferred_element_type=F32).astype(BF16)

    row = pl.BlockSpec((tm, D), lambda i: (i, 0))
    return pl.pallas_call(
        body, name=name, grid=(S // tm,),
        in_specs=[row, pl.BlockSpec((1, D), lambda i: (0, 0)), pl.BlockSpec((None, D, N), lambda i: (layer, 0, 0))],
        out_specs=[row, pl.BlockSpec((tm, N), lambda i: (i, 0))],
        out_shape=[jax.ShapeDtypeStruct((S, D), BF16), jax.ShapeDtypeStruct((S, N), BF16)],
        compiler_params=_cparams(("parallel",)),
    )(h, g, w)


def _mm_norm_bwd(dz, w, layer, h, g, dres, *, name, tk):
    S, K = dz.shape
    D = h.shape[1]
    tm = _div_tile(S, 1024, NORM_SLAB)
    slab = min(NORM_SLAB, tm)
    tk = _div_tile(K, tk, 128)
    nk = K // tk
    nt = (((1,), (1,)), ((), ()))

    def body(dz_ref, w_ref, h_ref, g_ref, dr_ref, dx_ref, dg_ref, acc):
        i = pl.program_id(0)
        k = pl.program_id(1)

        def norm_bwd(du_of):
            part = jnp.zeros((1, D), F32)
            for r0 in range(0, tm, slab):
                rows = pl.ds(r0, slab)
                du = du_of(rows)
                xv = h_ref[rows, :]
                r = lax.rsqrt(jnp.mean(xv * xv, axis=-1, keepdims=True) + RMS_EPS)
                xhat = xv * r
                part = part + jnp.sum(du * xhat, axis=0, keepdims=True)
                dxhat = du * g_ref[...]
                dx_ref[rows, :] = dr_ref[rows, :] + r * (
                    dxhat - xhat * jnp.mean(dxhat * xhat, axis=-1, keepdims=True))

            @pl.when(i == 0)
            def _():
                dg_ref[...] = part

            @pl.when(i > 0)
            def _():
                dg_ref[...] += part

        if nk == 1:
            norm_bwd(lambda rows: lax.dot_general(dz_ref[rows, :], w_ref[...], nt, preferred_element_type=F32))
        else:
            p = lax.dot_general(dz_ref[...], w_ref[...], nt, preferred_element_type=F32)

            @pl.when(k == 0)
            def _():
                acc[...] = p

            @pl.when(k > 0)
            def _():
                acc[...] += p

            @pl.when(k == nk - 1)
            def _():
                norm_bwd(lambda rows: acc[rows, :])

    row = pl.BlockSpec((tm, D), lambda i, k: (i, 0))
    vec = pl.BlockSpec((1, D), lambda i, k: (0, 0))
    return pl.pallas_call(
        body, name=name, grid=(S // tm, nk),
        in_specs=[pl.BlockSpec((tm, tk), lambda i, k: (i, k)), pl.BlockSpec((None, D, tk), lambda i, k: (layer, 0, k)),
                  row, vec, row],
        out_specs=[row, vec],
        out_shape=[jax.ShapeDtypeStruct((S, D), F32), jax.ShapeDtypeStruct((1, D), F32)],
        scratch_shapes=[pltpu.VMEM((tm, D), F32)],
        compiler_params=_cparams(("arbitrary", "arbitrary")),
    )(dz, w, h, g, dres)


def _rms_bwd(x, g, du, dres, *, name):
    S, D = x.shape
    ts = _div_tile(S, 512)
    has_res = dres is not None

    def body(*refs):
        if has_res:
            x_ref, g_ref, du_ref, dr_ref, dx_ref, dg_ref = refs
        else:
            x_ref, g_ref, du_ref, dg_ref = refs
        i = pl.program_id(0)
        xv = x_ref[...]
        duv = du_ref[...].astype(F32)
        r = lax.rsqrt(jnp.mean(xv * xv, axis=-1, keepdims=True) + RMS_EPS)
        xhat = xv * r
        part = jnp.sum(duv * xhat, axis=0, keepdims=True)

        @pl.when(i == 0)
        def _():
            dg_ref[...] = part

        @pl.when(i > 0)
        def _():
            dg_ref[...] += part

        if has_res:
            dxhat = duv * g_ref[...]
            dx = r * (dxhat - xhat * jnp.mean(dxhat * xhat, axis=-1, keepdims=True))
            dx_ref[...] = dr_ref[...] + dx

    row = pl.BlockSpec((ts, D), lambda i: (i, 0))
    vec = pl.BlockSpec((1, D), lambda i: (0, 0))
    if has_res:
        dx, dg = pl.pallas_call(
            body, name=name, grid=(S // ts,), in_specs=[row, vec, row, row], out_specs=[row, vec],
            out_shape=[jax.ShapeDtypeStruct((S, D), F32), jax.ShapeDtypeStruct((1, D), F32)],
            compiler_params=_cparams(("arbitrary",)),
        )(x, g, du, dres)
        return dx, dg
    dg = pl.pallas_call(
        body, name=name, grid=(S // ts,), in_specs=[row, vec, row], out_specs=vec,
        out_shape=jax.ShapeDtypeStruct((1, D), F32),
        compiler_params=_cparams(("arbitrary",)),
    )(x, g, du)
    return None, dg


def _final_loss(h, g, tgt, *, name):
    S, D = h.shape
    ts = _div_tile(S, 512)

    def body(h_ref, g_ref, t_ref, loss_ref, dh_ref, dg_ref):
        i = pl.program_id(0)
        xv = h_ref[...]
        gv = g_ref[...]
        r = lax.rsqrt(jnp.mean(xv * xv, axis=-1, keepdims=True) + RMS_EPS)
        xhat = xv * r
        err = xhat * gv - t_ref[...]
        lpart = 0.5 * jnp.sum(jnp.mean(err * err, axis=-1, keepdims=True), axis=0, keepdims=True)
        dy = err * (1.0 / D)
        gpart = jnp.sum(dy * xhat, axis=0, keepdims=True)

        @pl.when(i == 0)
        def _():
            loss_ref[...] = jnp.broadcast_to(lpart, loss_ref.shape)
            dg_ref[...] = gpart

        @pl.when(i > 0)
        def _():
            loss_ref[...] += jnp.broadcast_to(lpart, loss_ref.shape)
            dg_ref[...] += gpart

        dxhat = dy * gv
        dh_ref[...] = r * (dxhat - xhat * jnp.mean(dxhat * xhat, axis=-1, keepdims=True))

    row = pl.BlockSpec((ts, D), lambda i: (i, 0))
    vec = pl.BlockSpec((1, D), lambda i: (0, 0))
    return pl.pallas_call(
        body, name=name, grid=(S // ts,), in_specs=[row, vec, row],
        out_specs=[pl.BlockSpec((8, 128), lambda i: (0, 0)), row, vec],
        out_shape=[jax.ShapeDtypeStruct((8, 128), F32), jax.ShapeDtypeStruct((S, D), F32),
                   jax.ShapeDtypeStruct((1, D), F32)],
        compiler_params=_cparams(("arbitrary",)),
    )(h, g, tgt)


def _sigmoid(x):
    return 1.0 / (1.0 + jnp.exp(-x))


FFN_CHUNK = 1408
FFN_CHUNKS = D_FF // FFN_CHUNK
FFN_BWD_PIECE = 384
FFN_BWD_SLAB = 256


def _ffn_fwd_fused(h, g, w_gu, w_down, layer, *, name):
    S, D = h.shape
    tm = _div_tile(S, 512, 16)
    tf, nj = FFN_CHUNK, FFN_CHUNKS

    def body(h_ref, g_ref, wg_ref, wu_ref, wd_ref, h2_ref, u_ref, gate_ref, up_ref, a_ref, u_s, acc):
        j = pl.program_id(1)

        @pl.when(j == 0)
        def _():
            xv = h_ref[...]
            r = lax.rsqrt(jnp.mean(xv * xv, axis=-1, keepdims=True) + RMS_EPS)
            u = (xv * r * g_ref[...]).astype(BF16)
            u_s[...] = u
            u_ref[...] = u

        u = u_s[...]
        gate = jnp.dot(u, wg_ref[...], preferred_element_type=F32)
        up = jnp.dot(u, wu_ref[...], preferred_element_type=F32)
        gate_ref[...] = gate.astype(BF16)
        up_ref[...] = up.astype(BF16)
        a = (gate * _sigmoid(gate) * up).astype(BF16)
        a_ref[...] = a
        p = jnp.dot(a, wd_ref[...], preferred_element_type=F32)

        @pl.when(j == 0)
        def _():
            acc[...] = p

        @pl.when(j > 0)
        def _():
            acc[...] += p

        @pl.when(j == nj - 1)
        def _():
            h2_ref[...] = h_ref[...] + 0.5 * acc[...]

    row = pl.BlockSpec((tm, D), lambda i, j: (i, 0))
    chunk = pl.BlockSpec((tm, tf), lambda i, j: (i, j))
    hidden = jax.ShapeDtypeStruct((S, D_FF), BF16)
    return pl.pallas_call(
        body, name=name, grid=(S // tm, nj),
        in_specs=[row, pl.BlockSpec((1, D), lambda i, j: (0, 0)),
                  pl.BlockSpec((None, D, tf), lambda i, j: (layer, 0, j)),
                  pl.BlockSpec((None, D, tf), lambda i, j: (layer, 0, nj + j)),
                  pl.BlockSpec((None, tf, D), lambda i, j: (layer, j, 0))],
        out_specs=[row, row, chunk, chunk, chunk],
        out_shape=[jax.ShapeDtypeStruct((S, D), F32), jax.ShapeDtypeStruct((S, D), BF16), hidden, hidden, hidden],
        scratch_shapes=[pltpu.VMEM((tm, D), BF16), pltpu.VMEM((tm, D), F32)],
        compiler_params=_cparams(("parallel", "arbitrary")),
    )(h, g, w_gu, w_gu, w_down)


def _ffn_bwd_fused(dh, h, g, gate, up, w_gu, w_down, layer, *, name):
    S, D = h.shape
    tm = _div_tile(S, 512, FFN_BWD_SLAB)
    tf = FFN_CHUNK
    nj = D_FF // tf
    slab = min(FFN_BWD_SLAB, tm)
    nt = (((1,), (1,)), ((), ()))
    pieces = [(c0, min(FFN_BWD_PIECE, tf - c0)) for c0 in range(0, tf, FFN_BWD_PIECE)]

    def body(dh_ref, h_ref, g_ref, gate_ref, up_ref, wg_ref, wu_ref, wd_ref, dx_ref, dg_ref, dgate_ref, dup_ref,
             dy_s, acc):
        i = pl.program_id(0)
        j = pl.program_id(1)

        @pl.when(j == 0)
        def _():
            for r0 in range(0, tm, slab):
                rows = pl.ds(r0, slab)
                dy_s[rows, :] = (0.5 * dh_ref[rows, :]).astype(BF16)

        p = None
        for c0, cw in pieces:
            cols = pl.ds(c0, cw)
            da = lax.dot_general(dy_s[...], wd_ref[cols, :], nt, preferred_element_type=F32)
            gt = gate_ref[:, cols].astype(F32)
            sg = _sigmoid(gt)
            dgate = (da * up_ref[:, cols].astype(F32) * sg * (1.0 + gt * (1.0 - sg))).astype(BF16)
            dup = (da * gt * sg).astype(BF16)
            dgate_ref[:, cols] = dgate
            dup_ref[:, cols] = dup
            q = (lax.dot_general(dgate, wg_ref[:, cols], nt, preferred_element_type=F32)
                 + lax.dot_general(dup, wu_ref[:, cols], nt, preferred_element_type=F32))
            p = q if p is None else p + q

        @pl.when(j == 0)
        def _():
            acc[...] = p

        @pl.when(j > 0)
        def _():
            acc[...] += p

        @pl.when(j == nj - 1)
        def _():
            part = jnp.zeros((1, D), F32)
            for r0 in range(0, tm, slab):
                rows = pl.ds(r0, slab)
                xv = h_ref[rows, :]
                du = acc[rows, :]
                r = lax.rsqrt(jnp.mean(xv * xv, axis=-1, keepdims=True) + RMS_EPS)
                xhat = xv * r
                part = part + jnp.sum(du * xhat, axis=0, keepdims=True)
                dxhat = du * g_ref[...]
                dx_ref[rows, :] = dh_ref[rows, :] + r * (
                    dxhat - xhat * jnp.mean(dxhat * xhat, axis=-1, keepdims=True))

            @pl.when(i == 0)
            def _():
                dg_ref[...] = part

            @pl.when(i > 0)
            def _():
                dg_ref[...] += part

    row = pl.BlockSpec((tm, D), lambda i, j: (i, 0))
    vec = pl.BlockSpec((1, D), lambda i, j: (0, 0))
    chunk = pl.BlockSpec((tm, tf), lambda i, j: (i, j))
    hidden = jax.ShapeDtypeStruct((S, D_FF), BF16)
    return pl.pallas_call(
        body, name=name, grid=(S // tm, nj),
        in_specs=[row, row, vec, chunk, chunk,
                  pl.BlockSpec((None, D, tf), lambda i, j: (layer, 0, j)),
                  pl.BlockSpec((None, D, tf), lambda i, j: (layer, 0, nj + j)),
                  pl.BlockSpec((None, tf, D), lambda i, j: (layer, j, 0))],
        out_specs=[row, vec, chunk, chunk],
        out_shape=[jax.ShapeDtypeStruct((S, D), F32), jax.ShapeDtypeStruct((1, D), F32), hidden, hidden],
        scratch_shapes=[pltpu.VMEM((tm, D), BF16), pltpu.VMEM((tm, D), F32)],
        compiler_params=_cparams(("arbitrary", "arbitrary")),
    )(dh, h, g, gate, up, w_gu, w_gu, w_down)


CONV_HALO = 32
CONV_SUB_ROWS = 128


def _shifted_taps(win, shifted, ts):
    n = ts + CONV_HALO - 8
    for r in range(1, 8):
        shifted[r - 1] = win[pl.ds(r, n), :]

    def tap(start, rows, lanes):
        q, r = divmod(start, 8)
        if r == 0:
            return win[pl.ds(start, rows), lanes]
        return shifted[r - 1, pl.ds(8 * q, rows), lanes]

    return tap


def _conv_a_fwd(z, w, bias, ln_g, ln_b, *, name):
    S = z.shape[0]
    C = CONV_A_CH
    ts = _div_tile(S, 256, 32)

    def body(val_ref, gate_ref, w_ref, b_ref, g_ref, lb_ref, c_ref, act_ref, win, shifted):
        i = pl.program_id(0)

        @pl.when(i == 0)
        def _():
            win[pl.ds(0, CONV_HALO), :] = jnp.zeros((CONV_HALO, C), F32)

        @pl.when(i > 0)
        def _():
            win[pl.ds(0, CONV_HALO), :] = win[pl.ds(ts, CONV_HALO), :]

        a = val_ref[...].astype(F32) * _sigmoid(gate_ref[...].astype(F32))
        win[pl.ds(CONV_HALO, ts), :] = a
        tap = _shifted_taps(win, shifted, ts)
        rs = min(CONV_SUB_ROWS, ts)
        for cb in range(C // 128):
            lanes = pl.ds(128 * cb, 128)
            for rt in range(ts // rs):
                sub = jnp.broadcast_to(b_ref[:, lanes], (rs, 128))
                for k in range(CONV_A_WIDTH):
                    sub = sub + w_ref[pl.ds(k, 1), lanes] * tap(
                        CONV_HALO - (CONV_A_WIDTH - 1) + k + rs * rt, rs, lanes)
                c_ref[pl.ds(rs * rt, rs), lanes] = sub
        acc = c_ref[...]
        mu = jnp.mean(acc, axis=-1, keepdims=True)
        xc = acc - mu
        var = jnp.mean(xc * xc, axis=-1, keepdims=True)
        ln = xc * lax.rsqrt(var + LN_EPS) * g_ref[...] + lb_ref[...]
        act_ref[...] = (ln * _sigmoid(ln)).astype(BF16)

    row = lambda col: pl.BlockSpec((ts, C), lambda i, col=col: (i, col))
    vec = pl.BlockSpec((1, C), lambda i: (0, 0))
    return pl.pallas_call(
        body, name=name, grid=(S // ts,),
        in_specs=[row(0), row(1), pl.BlockSpec((32, C), lambda i: (0, 0)), vec, vec, vec],
        out_specs=[row(0), row(0)],
        out_shape=[jax.ShapeDtypeStruct((S, C), F32), jax.ShapeDtypeStruct((S, 2 * C), BF16)],
        scratch_shapes=[pltpu.VMEM((ts + CONV_HALO, C), F32), pltpu.VMEM((7, ts + CONV_HALO - 8, C), F32)],
        compiler_params=_cparams(("arbitrary",)),
    )(z, z, w, bias, ln_g, ln_b)


def _conv_a_bwd(z, c, dcat, w, ln_g, ln_b, *, name):
    S = z.shape[0]
    C = CONV_A_CH
    ts = _div_tile(S, 256, 32)
    n = S // ts

    def body(val_ref, gate_ref, c_ref, da_ref, w_ref, g_ref, lb_ref, dz_ref, small_ref, win, a_s, da_s, dw8,
             shifted):
        i = pl.program_id(0)

        @pl.when(i == 0)
        def _():
            win[pl.ds(ts, CONV_HALO), :] = jnp.zeros((CONV_HALO, C), F32)
            small_ref[...] = jnp.zeros(small_ref.shape, F32)
            dw8[...] = jnp.zeros(dw8.shape, F32)

        @pl.when(i > 0)
        def _():
            win[pl.ds(ts, CONV_HALO), :] = win[pl.ds(0, CONV_HALO), :]

        cv = c_ref[...]
        gv = g_ref[...]
        mu = jnp.mean(cv, axis=-1, keepdims=True)
        xc = cv - mu
        var = jnp.mean(xc * xc, axis=-1, keepdims=True)
        rstd = lax.rsqrt(var + LN_EPS)
        xhat = xc * rstd
        ln = xhat * gv + lb_ref[...]
        sg = _sigmoid(ln)
        dln = da_ref[...].astype(F32) * (sg * (1.0 + ln * (1.0 - sg)))
        small_ref[pl.ds(33, 1), :] += jnp.sum(dln * xhat, axis=0, keepdims=True)
        small_ref[pl.ds(34, 1), :] += jnp.sum(dln, axis=0, keepdims=True)
        dxhat = dln * gv
        dc = rstd * (dxhat - jnp.mean(dxhat, axis=-1, keepdims=True)
                     - xhat * jnp.mean(dxhat * xhat, axis=-1, keepdims=True))
        small_ref[pl.ds(32, 1), :] += jnp.sum(dc, axis=0, keepdims=True)
        win[pl.ds(0, ts), :] = dc

        val = val_ref[...].astype(F32)
        sgg = _sigmoid(gate_ref[...].astype(F32))
        a_s[...] = val * sgg
        tap = _shifted_taps(win, shifted, ts)
        rs = min(CONV_SUB_ROWS, ts)
        for cb in range(C // 128):
            lanes = pl.ds(128 * cb, 128)
            for rt in range(ts // rs):
                a_sub = a_s[pl.ds(rs * rt, rs), lanes]
                da = jnp.zeros((rs, 128), F32)
                for k in range(CONV_A_WIDTH):
                    sh = tap(CONV_A_WIDTH - 1 - k + rs * rt, rs, lanes)
                    da = da + w_ref[pl.ds(k, 1), lanes] * sh
                    prod = a_sub * sh
                    part = prod[0:8]
                    for r in range(1, rs // 8):
                        part = part + prod[8 * r:8 * r + 8]
                    dw8[pl.ds(8 * k, 8), lanes] += part
                da_s[pl.ds(rs * rt, rs), lanes] = da
        da = da_s[...]
        dz_ref[:, pl.ds(0, C)] = (da * sgg).astype(BF16)
        dz_ref[:, pl.ds(C, C)] = (da * val * sgg * (1.0 - sgg)).astype(BF16)

        @pl.when(i == n - 1)
        def _():
            for k in range(CONV_A_WIDTH):
                small_ref[pl.ds(k, 1), :] = jnp.sum(dw8[pl.ds(8 * k, 8), :], axis=0, keepdims=True)

    row = lambda col: pl.BlockSpec((ts, C), lambda i, col=col: (n - 1 - i, col))
    vec = pl.BlockSpec((1, C), lambda i: (0, 0))
    return pl.pallas_call(
        body, name=name, grid=(n,),
        in_specs=[row(0), row(1), row(0), row(0), pl.BlockSpec((32, C), lambda i: (0, 0)), vec, vec],
        out_specs=[pl.BlockSpec((ts, 2 * C), lambda i: (n - 1 - i, 0)), pl.BlockSpec((40, C), lambda i: (0, 0))],
        out_shape=[jax.ShapeDtypeStruct((S, 2 * C), BF16), jax.ShapeDtypeStruct((40, C), F32)],
        scratch_shapes=[pltpu.VMEM((ts + CONV_HALO, C), F32), pltpu.VMEM((ts, C), F32), pltpu.VMEM((ts, C), F32),
                        pltpu.VMEM((8 * 32, C), F32), pltpu.VMEM((7, ts + CONV_HALO - 8, C), F32)],
        compiler_params=_cparams(("arbitrary",)),
    )(z, z, c, dcat, w, ln_g, ln_b)


SC_HALO = 8


def _sconv_fwd(z, w, *, name):
    S = z.shape[0]
    C = SC_CH
    ts = _div_tile(S, 256, 16)

    def body(gb_ref, gc_ref, v_ref, w_ref, y_ref, cc_ref, win):
        i = pl.program_id(0)

        @pl.when(i == 0)
        def _():
            win[pl.ds(0, SC_HALO), :] = jnp.zeros((SC_HALO, C), F32)

        @pl.when(i > 0)
        def _():
            win[pl.ds(0, SC_HALO), :] = win[pl.ds(ts, SC_HALO), :]

        win[pl.ds(SC_HALO, ts), :] = gc_ref[...].astype(F32) * v_ref[...].astype(F32)
        acc = jnp.zeros((ts, C), F32)
        for k in range(SC_WIDTH):
            acc = acc + w_ref[pl.ds(k, 1), :] * win[pl.ds(SC_HALO - (SC_WIDTH - 1) + k, ts), :]
        cc_ref[...] = acc.astype(BF16)
        y_ref[...] = (gb_ref[...].astype(F32) * acc).astype(BF16)

    row = lambda col: pl.BlockSpec((ts, C), lambda i, col=col: (i, col))
    return pl.pallas_call(
        body, name=name, grid=(S // ts,),
        in_specs=[row(0), row(1), row(2), pl.BlockSpec((8, C), lambda i: (0, 0))],
        out_specs=[row(0), row(0)],
        out_shape=[jax.ShapeDtypeStruct((S, C), BF16), jax.ShapeDtypeStruct((S, C), BF16)],
        scratch_shapes=[pltpu.VMEM((ts + SC_HALO, C), F32)],
        compiler_params=_cparams(("arbitrary",)),
    )(z, z, z, w)


def _sconv_bwd(z, cc, dy, w, *, name):
    S = z.shape[0]
    C = SC_CH
    ts = _div_tile(S, 256, 16)
    n = S // ts

    def body(gb_ref, gc_ref, v_ref, cc_ref, dy_ref, w_ref, dz_ref, dw_ref, win):
        i = pl.program_id(0)

        @pl.when(i == 0)
        def _():
            win[pl.ds(ts, SC_HALO), :] = jnp.zeros((SC_HALO, C), F32)
            dw_ref[...] = jnp.zeros(dw_ref.shape, F32)

        @pl.when(i > 0)
        def _():
            win[pl.ds(ts, SC_HALO), :] = win[pl.ds(0, SC_HALO), :]

        dyv = dy_ref[...].astype(F32)
        gb = gb_ref[...].astype(F32)
        gc = gc_ref[...].astype(F32)
        val = v_ref[...].astype(F32)
        dz_ref[:, pl.ds(0, C)] = (dyv * cc_ref[...].astype(F32)).astype(BF16)
        win[pl.ds(0, ts), :] = dyv * gb
        cv = gc * val
        dcv = jnp.zeros((ts, C), F32)
        for k in range(SC_WIDTH):
            sh = win[pl.ds(SC_WIDTH - 1 - k, ts), :]
            dcv = dcv + w_ref[pl.ds(k, 1), :] * sh
            dw_ref[pl.ds(k, 1), :] += jnp.sum(cv * sh, axis=0, keepdims=True)
        dz_ref[:, pl.ds(C, C)] = (dcv * val).astype(BF16)
        dz_ref[:, pl.ds(2 * C, C)] = (dcv * gc).astype(BF16)

    row = lambda col: pl.BlockSpec((ts, C), lambda i, col=col: (n - 1 - i, col))
    return pl.pallas_call(
        body, name=name, grid=(n,),
        in_specs=[row(0), row(1), row(2), row(0), row(0), pl.BlockSpec((8, C), lambda i: (0, 0))],
        out_specs=[pl.BlockSpec((ts, 3 * C), lambda i: (n - 1 - i, 0)), pl.BlockSpec((8, C), lambda i: (0, 0))],
        out_shape=[jax.ShapeDtypeStruct((S, 3 * C), BF16), jax.ShapeDtypeStruct((8, C), F32)],
        scratch_shapes=[pltpu.VMEM((ts + SC_HALO, C), F32)],
        compiler_params=_cparams(("arbitrary",)),
    )(z, z, z, cc, dy, w)


SWA_Q_COL = 2
SWA_SLOPES = [2.0 ** (-8.0 * (h + 1) / SWA_HEADS) for h in range(SWA_HEADS)]
SWA_SCALE = HEAD_DIM ** -0.5


SWA_GROUP_ROWS = SWA_GROUP * WINDOW


def _swa_masks():
    shape = (SWA_GROUP_ROWS, 2 * WINDOW)
    ii = lax.broadcasted_iota(jnp.int32, shape, 0)
    jj = lax.broadcasted_iota(jnp.int32, shape, 1)
    dist = (ii & (WINDOW - 1)) + WINDOW - jj
    valid = (dist >= 0) & (dist < WINDOW)
    grp = lax.broadcasted_iota(jnp.int32, (SWA_GROUP_ROWS, 1), 0) // WINDOW
    return dist.astype(F32), valid, jj, grp


def _by_group(grp, vals):
    out = jnp.full(grp.shape, vals[SWA_GROUP - 1], F32)
    for g in range(SWA_GROUP - 2, -1, -1):
        out = jnp.where(grp == g, vals[g], out)
    return out


def _stack_heads(ref, rows, kv):
    return jnp.concatenate([ref[rows, pl.ds(HEAD_DIM * (kv * SWA_GROUP + g), HEAD_DIM)] for g in range(SWA_GROUP)],
                           axis=0)


def _swa_probs(qg, kk, sink, slope, distf, valid):
    s = lax.dot_general(qg, kk, (((1,), (1,)), ((), ())), preferred_element_type=F32) * SWA_SCALE
    s = s - slope * distf
    s = jnp.where(valid, s, NEG_BIG)
    m = jnp.maximum(jnp.max(s, axis=-1, keepdims=True), sink)
    p = jnp.exp(s - m)
    l = jnp.sum(p, axis=-1, keepdims=True) + jnp.exp(sink - m)
    return p, m, l


def _swa_fwd(z, kpad, vpad, sinks, cat, *, name):
    S = z.shape[0]
    tq = _div_tile(S, 256, 128)
    nblk = tq // WINDOW
    W = WINDOW

    def body(sink_ref, q_ref, k_ref, v_ref, cat_ref, o_ref):
        i = pl.program_id(0)
        distf, valid0, jj, grp = _swa_masks()
        for kv in range(SWA_KV_HEADS):
            heads = range(kv * SWA_GROUP, (kv + 1) * SWA_GROUP)
            sink = _by_group(grp, [sink_ref[h] for h in heads])
            slope = _by_group(grp, [SWA_SLOPES[h] for h in heads])
            for b in range(nblk):
                nb = i * nblk + b
                start = pl.multiple_of(nb * W, W)
                rows = pl.ds(W * b, W)
                valid = valid0 & ((jj >= W) | (nb > 0))
                kk = k_ref[pl.ds(start, 2 * W), pl.ds(HEAD_DIM * kv, HEAD_DIM)]
                vv = v_ref[pl.ds(start, 2 * W), pl.ds(HEAD_DIM * kv, HEAD_DIM)]
                p, m, l = _swa_probs(_stack_heads(q_ref, rows, kv), kk, sink, slope, distf, valid)
                o = (jnp.dot(p.astype(BF16), vv, preferred_element_type=F32) / l).astype(BF16)
                for g, h in enumerate(heads):
                    o_ref[rows, pl.ds(HEAD_DIM * h, HEAD_DIM)] = o[W * g:W * (g + 1)]

    full = pl.BlockSpec((S + W, 2 * HEAD_DIM), lambda i: (0, 0))
    return pl.pallas_call(
        body, name=name, grid=(S // tq,),
        in_specs=[pl.BlockSpec(memory_space=pltpu.SMEM), pl.BlockSpec((tq, 512), lambda i: (i, SWA_Q_COL)), full, full,
                  pl.BlockSpec(memory_space=pl.ANY)],
        out_specs=pl.BlockSpec((tq, 512), lambda i: (i, 1)),
        out_shape=jax.ShapeDtypeStruct((S, 1024), BF16), input_output_aliases={4: 0},
        compiler_params=_cparams(("parallel",)),
    )(sinks, z, kpad, vpad, cat)


def _swa_bwd(z, kpad, vpad, sinks, dcat, *, name):
    S = z.shape[0]
    tq = _div_tile(S, 256, 128)
    nblk = tq // WINDOW
    W = WINDOW

    def body(sink_ref, q_ref, k_ref, v_ref, do_ref, dq_ref, dk_ref, dv_ref, ds_ref):
        i = pl.program_id(0)

        @pl.when(i == 0)
        def _():
            dk_ref[...] = jnp.zeros(dk_ref.shape, F32)
            dv_ref[...] = jnp.zeros(dv_ref.shape, F32)
            ds_ref[...] = jnp.zeros(ds_ref.shape, F32)

        distf, valid0, jj, grp = _swa_masks()
        tn = (((0,), (0,)), ((), ()))
        for kv in range(SWA_KV_HEADS):
            heads = range(kv * SWA_GROUP, (kv + 1) * SWA_GROUP)
            sink = _by_group(grp, [sink_ref[h] for h in heads])
            slope = _by_group(grp, [SWA_SLOPES[h] for h in heads])
            for b in range(nblk):
                nb = i * nblk + b
                start = pl.multiple_of(nb * W, W)
                rows = pl.ds(W * b, W)
                valid = valid0 & ((jj >= W) | (nb > 0))
                kk = k_ref[pl.ds(start, 2 * W), pl.ds(HEAD_DIM * kv, HEAD_DIM)]
                vv = v_ref[pl.ds(start, 2 * W), pl.ds(HEAD_DIM * kv, HEAD_DIM)]
                qg = _stack_heads(q_ref, rows, kv)
                dog = _stack_heads(do_ref, rows, kv)
                p, m, l = _swa_probs(qg, kk, sink, slope, distf, valid)
                inv_l = 1.0 / l
                pn = p * inv_l
                dp = lax.dot_general(dog, vv, (((1,), (1,)), ((), ())), preferred_element_type=F32)
                delta = jnp.sum(pn * dp, axis=-1, keepdims=True)
                dsc = (pn * (dp - delta)).astype(BF16)
                dsink = jnp.exp(sink - m) * inv_l * delta
                dq = (jnp.dot(dsc, kk, preferred_element_type=F32) * SWA_SCALE).astype(BF16)
                for g, h in enumerate(heads):
                    ds_ref[pl.ds(h, 1), :] += jnp.broadcast_to(
                        -jnp.sum(dsink[W * g:W * (g + 1)], axis=0, keepdims=True), (1, 128))
                    dq_ref[rows, pl.ds(HEAD_DIM * h, HEAD_DIM)] = dq[W * g:W * (g + 1)]
                dk_ref[pl.ds(start, 2 * W), pl.ds(HEAD_DIM * kv, HEAD_DIM)] += lax.dot_general(
                    dsc, qg, tn, preferred_element_type=F32) * SWA_SCALE
                dv_ref[pl.ds(start, 2 * W), pl.ds(HEAD_DIM * kv, HEAD_DIM)] += lax.dot_general(
                    pn.astype(BF16), dog, tn, preferred_element_type=F32)

    full = pl.BlockSpec((S + W, 2 * HEAD_DIM), lambda i: (0, 0))
    return pl.pallas_call(
        body, name=name, grid=(S // tq,),
        in_specs=[pl.BlockSpec(memory_space=pltpu.SMEM), pl.BlockSpec((tq, 512), lambda i: (i, SWA_Q_COL)), full, full,
                  pl.BlockSpec((tq, 512), lambda i: (i, 1))],
        out_specs=[pl.BlockSpec((tq, 512), lambda i: (i, 0)), full, full, pl.BlockSpec((8, 128), lambda i: (0, 0))],
        out_shape=[jax.ShapeDtypeStruct((S, 512), BF16), jax.ShapeDtypeStruct((S + W, 2 * HEAD_DIM), F32),
                   jax.ShapeDtypeStruct((S + W, 2 * HEAD_DIM), F32), jax.ShapeDtypeStruct((8, 128), F32)],
        compiler_params=_cparams(("arbitrary",)),
    )(sinks, z, kpad, vpad, dcat)


XA_SCALE = XA_HEAD_DIM ** -0.5


def _xa_probs(qh, kh):
    s = lax.dot_general(qh, kh, (((1,), (1,)), ((), ())), preferred_element_type=F32) * XA_SCALE
    m = jnp.max(s, axis=-1, keepdims=True)
    p = jnp.exp(s - m)
    return p, jnp.sum(p, axis=-1, keepdims=True)


def _xa_fwd(q, kv, *, name):
    S, D = q.shape
    M = kv.shape[0]
    ts = _div_tile(S, 512, 16)
    HD = XA_HEAD_DIM

    def body(q_ref, k_ref, v_ref, o_ref):
        for h in range(XA_HEADS):
            qh = q_ref[:, pl.ds(HD * h, HD)]
            p, l = _xa_probs(qh, k_ref[:, pl.ds(HD * h, HD)])
            o = jnp.dot(p.astype(BF16), v_ref[:, pl.ds(HD * h, HD)], preferred_element_type=F32) / l
            o_ref[:, pl.ds(HD * h, HD)] = o.astype(BF16)

    return pl.pallas_call(
        body, name=name, grid=(S // ts,),
        in_specs=[pl.BlockSpec((ts, D), lambda i: (i, 0)), pl.BlockSpec((M, D), lambda i: (0, 0)),
                  pl.BlockSpec((M, D), lambda i: (0, 1))],
        out_specs=pl.BlockSpec((ts, D), lambda i: (i, 0)),
        out_shape=jax.ShapeDtypeStruct((S, D), BF16),
        compiler_params=_cparams(("parallel",)),
    )(q, kv, kv)


def _xa_bwd(q, kv, do, *, name):
    S, D = q.shape
    M = kv.shape[0]
    ts = _div_tile(S, 512, 16)
    HD = XA_HEAD_DIM

    def body(q_ref, k_ref, v_ref, do_ref, dq_ref, dkv_ref):
        i = pl.program_id(0)

        @pl.when(i == 0)
        def _():
            dkv_ref[...] = jnp.zeros(dkv_ref.shape, F32)

        for h in range(XA_HEADS):
            qh = q_ref[:, pl.ds(HD * h, HD)]
            kh = k_ref[:, pl.ds(HD * h, HD)]
            vh = v_ref[:, pl.ds(HD * h, HD)]
            doh = do_ref[:, pl.ds(HD * h, HD)]
            p, l = _xa_probs(qh, kh)
            pn = p * (1.0 / l)
            dp = lax.dot_general(doh, vh, (((1,), (1,)), ((), ())), preferred_element_type=F32)
            delta = jnp.sum(pn * dp, axis=-1, keepdims=True)
            dsc = (pn * (dp - delta)).astype(BF16)
            dq_ref[:, pl.ds(HD * h, HD)] = (jnp.dot(dsc, kh, preferred_element_type=F32) * XA_SCALE).astype(BF16)
            dkv_ref[:, pl.ds(HD * h, HD)] += lax.dot_general(
                dsc, qh, (((0,), (0,)), ((), ())), preferred_element_type=F32) * XA_SCALE
            dkv_ref[:, pl.ds(D + HD * h, HD)] += lax.dot_general(
                pn.astype(BF16), doh, (((0,), (0,)), ((), ())), preferred_element_type=F32)

    row = pl.BlockSpec((ts, D), lambda i: (i, 0))
    return pl.pallas_call(
        body, name=name, grid=(S // ts,),
        in_specs=[row, pl.BlockSpec((M, D), lambda i: (0, 0)), pl.BlockSpec((M, D), lambda i: (0, 1)), row],
        out_specs=[row, pl.BlockSpec((M, 2 * D), lambda i: (0, 0))],
        out_shape=[jax.ShapeDtypeStruct((S, D), BF16), jax.ShapeDtypeStruct((M, 2 * D), F32)],
        compiler_params=_cparams(("arbitrary",)),
    )(q, kv, kv, do)


def _adam_math(w, g, m, v):
    m = ADAM_B1 * m + (1.0 - ADAM_B1) * g
    v = ADAM_B2 * v + (1.0 - ADAM_B2) * (g * g)
    m_hat = m / (1.0 - ADAM_B1 ** ADAM_STEP)
    v_hat = v / (1.0 - ADAM_B2 ** ADAM_STEP)
    delta = -ADAM_LR * (m_hat / (jnp.sqrt(v_hat) + ADAM_EPS) + ADAM_WD * w)
    return delta, m, v


def _adamw_layers(w, m, v, gsrc, *, name):
    L, A, B = w.shape
    tr = _div_tile(A, max(8, ADAM_TILE_ELEMS // B // 8 * 8))
    nt = A // tr
    flat = [a for srcs in gsrc for a in srcs]
    owner = [l for l, srcs in enumerate(gsrc) for _ in srcs]
    ng = len(flat)

    def body(*refs):
        w_ref, m_ref, v_ref = refs[:3]
        g_refs = refs[3:3 + ng]
        g_ref, d_ref, nm_ref, nv_ref = refs[3 + ng:]
        layer = pl.program_id(0)
        g = None
        for l in range(L):
            gl = None
            for a_ref, o in zip(g_refs, owner):
                if o == l:
                    gl = a_ref[...] if gl is None else gl + a_ref[...]
            g = gl if g is None else jnp.where(layer == l, gl, g)
        d, nm, nv = _adam_math(w_ref[...], g, m_ref[...], v_ref[...])
        g_ref[...] = g
        d_ref[...] = d
        nm_ref[...] = nm
        nv_ref[...] = nv

    def src_spec(o):
        return pl.BlockSpec((None, tr, B),
                            lambda l, i: (0, jnp.where(l == o, i, jnp.where(l > o, nt - 1, 0)), 0))

    spec = pl.BlockSpec((None, tr, B), lambda l, i: (l, i, 0))
    sds = jax.ShapeDtypeStruct((L, A, B), F32)
    return pl.pallas_call(
        body, name=name, grid=(L, nt), in_specs=[spec] * 3 + [src_spec(o) for o in owner], out_specs=[spec] * 4,
        out_shape=[sds] * 4, compiler_params=_cparams(("arbitrary", "arbitrary")),
    )(w, m, v, *flat)


def _adamw_small(ws, ms, vs, gparts, *, name):
    n = len(ws)
    R = gparts.shape[1]

    def body(*refs):
        w_refs, m_refs, v_refs = refs[:n], refs[n:2 * n], refs[2 * n:3 * n]
        gp_ref = refs[3 * n]
        outs = refs[3 * n + 1:7 * n + 1]
        packed = refs[7 * n + 1]
        g = gp_ref[0]
        for k in range(1, N_DEV):
            g = g + gp_ref[k]
        packed[...] = g
        row = 0
        for p in range(n):
            r, c = ws[p].shape
            per, lanes = max(c // 128, 1), min(c, 128)
            g_ref = outs[p]
            for i in range(r):
                for k in range(per):
                    g_ref[pl.ds(i, 1), pl.ds(128 * k, lanes)] = packed[pl.ds(row, 1), pl.ds(0, lanes)]
                    row += 1
            d, nm, nv = _adam_math(w_refs[p][...], g_ref[...], m_refs[p][...], v_refs[p][...])
            outs[n + p][...] = d
            outs[2 * n + p][...] = nm
            outs[3 * n + p][...] = nv

    shapes = [jax.ShapeDtypeStruct(a.shape, F32) for a in ws]
    res = pl.pallas_call(body, name=name, out_shape=shapes * 4, scratch_shapes=[pltpu.VMEM((R, 128), F32)],
                         compiler_params=_cparams())(*ws, *ms, *vs, gparts)
    return res[:n], res[n:2 * n], res[2 * n:3 * n], res[3 * n:]


ANY = pl.BlockSpec(memory_space=pl.ANY)


def _mesh_pos():
    return lax.axis_index("x"), lax.axis_index("y"), lax.axis_index("c")


def _other_chips(x, y):
    return [(1 - x, y), (x, 1 - y), (1 - x, 1 - y)]


LAYOUT = {'ffn1_w_gu': 'col', 'ffn1_w_down': 'stk', 'even_w_in': 'stk', 'even_w_out': 'stk', 'odd_w_in': 'col',
          'odd_w_out': 'stk', 'xa_wq': 'stk', 'xa_wkv': 'col', 'xa_wo': 'stk', 'ffn2_w_gu': 'col',
          'ffn2_w_down': 'stk', 'tiny': 'stk'}
COMM_NAMES = list(LAYOUT)
TINY_ROWS = 48


def _gathered_piece(ref, kind, L, A, h):
    if L == 2:
        return ref.at[h]
    rows = pl.ds(pl.multiple_of(h * (A // 2), 8), A // 2)
    return ref.at[0, rows] if kind == 'col' else ref.at[0, :, rows]


def _chip_part(piece, kind, B, s):
    if kind == 'col':
        return piece.at[:, pl.ds(pl.multiple_of(s * B, 128), B)]
    return piece.at[s]


def _place(shard, layer, kind, chip_idx, out_dtype, *, name, after=None):
    L, A, B = shard.shape
    ta = _div_tile(A, 256, 16)
    extra = [] if after is None else [after]

    def body(s_ref, x_ref, *rest):
        rest[-1][...] = x_ref[...].astype(out_dtype)

    if kind == 'col':
        shape = (1, A, N_CHIPS * B)
        out_spec = pl.BlockSpec((None, ta, B), lambda i, s: (0, i, s[0]))
    else:
        shape = (1, N_CHIPS, A, B)
        out_spec = pl.BlockSpec((None, None, ta, B), lambda i, s: (0, s[0], i, 0))
    grid_spec = pltpu.PrefetchScalarGridSpec(
        num_scalar_prefetch=1, grid=(A // ta,),
        in_specs=[pl.BlockSpec((None, ta, B), lambda i, s: (layer, i, 0))]
        + [pl.BlockSpec(memory_space=pl.ANY)] * len(extra), out_specs=out_spec)
    return pl.pallas_call(
        body, name=name, grid_spec=grid_spec, out_shape=jax.ShapeDtypeStruct(shape, out_dtype),
        compiler_params=_cparams(("parallel",)),
    )(chip_idx, shard, *extra)


HBM = pl.BlockSpec(memory_space=pltpu.HBM)
SEM = pl.BlockSpec(memory_space=pltpu.SEMAPHORE)
DATAFLOW = pltpu.SideEffectType.DATAFLOW_SIDE_EFFECTING


def _own_part_copies(refs, meta, send_sems, recv_sems):
    x, y, c = _mesh_pos()
    cps = []
    for k, (kind, L, A, B) in enumerate(meta):
        for j, (cx, cy) in enumerate(_other_chips(x, y)):
            part = _chip_part(refs[k].at[0], kind, B, 2 * x + y)
            cps.append(pltpu.make_async_remote_copy(
                src_ref=part, dst_ref=part, send_sem=send_sems.at[3 * k + j], recv_sem=recv_sems.at[3 * k + j],
                device_id=(cx, cy, c), device_id_type=MESH))
    return cps


def _half_part_copies(refs, meta, send_sems, recv_sems):
    x, y, c = _mesh_pos()
    cps = []
    for k, (kind, L, A, B) in enumerate(meta):
        for j, (cx, cy) in enumerate(_other_chips(x, y)):
            part = _chip_part(_gathered_piece(refs[k], kind, 1, A, c), kind, B, 2 * x + y)
            cps.append(pltpu.make_async_remote_copy(
                src_ref=part, dst_ref=part, send_sem=send_sems.at[3 * k + j], recv_sem=recv_sems.at[3 * k + j],
                device_id=(cx, cy, c), device_id_type=MESH))
    return cps


def _forward_copies(refs, meta, send_sems, recv_sems):
    x, y, c = _mesh_pos()
    cps = []
    for k, (kind, L, A, B) in enumerate(meta):
        for j, (cx, cy) in enumerate(_other_chips(x, y)):
            part = _chip_part(_gathered_piece(refs[k], kind, 1, A, c), kind, B, 2 * cx + cy)
            cps.append(pltpu.make_async_remote_copy(
                src_ref=part, dst_ref=part, send_sem=send_sems.at[3 * k + j], recv_sem=recv_sems.at[3 * k + j],
                device_id=(x, y, 1 - c), device_id_type=MESH))
    return cps


def _gather_start(fulls, meta, after, tag, copies=_own_part_copies):
    n = len(fulls)

    def body(*refs):
        send_sems, recv_sems = refs[n + 1], refs[n + 2]
        outs = refs[n + 3:2 * n + 3]
        token = refs[2 * n + 3]
        for cp in copies(outs, meta, send_sems, recv_sems):
            cp.start()
        token[...] = jnp.zeros_like(token)

    res = pl.pallas_call(
        body, name=f"ag_start_{tag}", in_specs=[HBM] * n + [pl.BlockSpec(memory_space=pl.ANY)],
        out_specs=(SEM, SEM) + (HBM,) * n + (pl.BlockSpec(memory_space=pltpu.VMEM),),
        out_shape=(pltpu.SemaphoreType.DMA((3 * n,)), pltpu.SemaphoreType.DMA((3 * n,)))
        + tuple(pltpu.HBM(f.shape, f.dtype) for f in fulls) + (jax.ShapeDtypeStruct((8, 128), F32),),
        input_output_aliases={k: 2 + k for k in range(n)},
        compiler_params=pltpu.CompilerParams(has_side_effects=DATAFLOW),
    )(*[pltpu.with_memory_space_constraint(f, pltpu.HBM) for f in fulls], after)
    return res[0], res[1], list(res[2:2 + n]), res[2 + n]


def _gather_wait(send_sems, recv_sems, fulls, meta, after, tag, copies=_own_part_copies):
    n = len(fulls)

    def body(*refs):
        f_refs = refs[:n]
        send_sems, recv_sems = refs[n], refs[n + 1]
        for cp in copies(f_refs, meta, send_sems, recv_sems):
            cp.wait_send()
            cp.wait_recv()

    return pl.pallas_call(
        body, name=f"ag_wait_{tag}", in_specs=[HBM] * n + [SEM, SEM, pl.BlockSpec(memory_space=pl.ANY)],
        out_specs=[HBM] * n, out_shape=[pltpu.HBM(f.shape, f.dtype) for f in fulls],
        input_output_aliases={k: k for k in range(n)},
        compiler_params=pltpu.CompilerParams(has_side_effects=DATAFLOW),
    )(*fulls, send_sems, recv_sems, after)


def _scatter_copies(g_refs, land_refs, meta, send_sems, recv_sems):
    x, y, c = _mesh_pos()
    cps = []
    for k, (kind, L, A, B) in enumerate(meta):
        for j, (cx, cy) in enumerate(_other_chips(x, y)):
            cps.append(pltpu.make_async_remote_copy(
                src_ref=_chip_part(g_refs[k].at[0], kind, B, 2 * cx + cy), dst_ref=land_refs[k].at[j],
                send_sem=send_sems.at[3 * k + j], recv_sem=recv_sems.at[3 * k + j], device_id=(cx, cy, c),
                device_id_type=MESH))
    return cps


def _scatter_start(gs, meta, after, tag):
    n = len(gs)

    def body(*refs):
        send_sems, recv_sems = refs[2 * n + 1], refs[2 * n + 2]
        g_out = refs[2 * n + 3:3 * n + 3]
        lands = refs[3 * n + 3:4 * n + 3]
        token = refs[4 * n + 3]
        for cp in _scatter_copies(g_out, lands, meta, send_sems, recv_sems):
            cp.start()
        token[...] = jnp.zeros_like(token)

    land_shapes = [(3, A, B) for kind, L, A, B in meta]
    lands = [pltpu.with_memory_space_constraint(lax.empty(s, g.dtype), pltpu.HBM) for s, g in zip(land_shapes, gs)]
    res = pl.pallas_call(
        body, name=f"rs_start_{tag}", in_specs=[HBM] * (2 * n) + [pl.BlockSpec(memory_space=pl.ANY)],
        out_specs=(SEM, SEM) + (HBM,) * (2 * n) + (pl.BlockSpec(memory_space=pltpu.VMEM),),
        out_shape=(pltpu.SemaphoreType.DMA((3 * n,)), pltpu.SemaphoreType.DMA((3 * n,)))
        + tuple(pltpu.HBM(g.shape, g.dtype) for g in gs)
        + tuple(pltpu.HBM(s, g.dtype) for s, g in zip(land_shapes, gs)) + (jax.ShapeDtypeStruct((8, 128), F32),),
        input_output_aliases={k: 2 + k for k in range(2 * n)},
        compiler_params=pltpu.CompilerParams(has_side_effects=DATAFLOW),
    )(*[pltpu.with_memory_space_constraint(g, pltpu.HBM) for g in gs], *lands, after)
    return res[0], res[1], list(res[2:2 + n]), list(res[2 + n:2 + 2 * n]), res[2 + 2 * n]


def _scatter_wait(send_sems, recv_sems, gs, lands, meta, after, tag):
    n = len(gs)

    def body(*refs):
        g_refs, land_refs = refs[:n], refs[n:2 * n]
        send_sems, recv_sems = refs[2 * n], refs[2 * n + 1]
        for cp in _scatter_copies(g_refs, land_refs, meta, send_sems, recv_sems):
            cp.wait_send()
            cp.wait_recv()

    both = list(gs) + list(lands)
    res = pl.pallas_call(
        body, name=f"rs_wait_{tag}", in_specs=[HBM] * (2 * n) + [SEM, SEM, pl.BlockSpec(memory_space=pl.ANY)],
        out_specs=[HBM] * (2 * n), out_shape=[pltpu.HBM(a.shape, a.dtype) for a in both],
        input_output_aliases={k: k for k in range(2 * n)},
        compiler_params=pltpu.CompilerParams(has_side_effects=DATAFLOW),
    )(*both, send_sems, recv_sems, after)
    return list(res[:n]), list(res[n:])


def _chip_sum_full(g, got, m, chip_idx, *, name):
    kind, L, A, B = m
    ta = _div_tile(A, 256, 16)

    def body(r_ref, a_ref, b_ref, o_ref):
        acc = a_ref[...].astype(F32)
        for j in range(3):
            acc = acc + b_ref[j].astype(F32)
        o_ref[...] = acc

    if kind == 'col':
        g_spec = pl.BlockSpec((None, ta, B), lambda i, r: (0, i, r[0]))
    else:
        g_spec = pl.BlockSpec((None, None, ta, B), lambda i, r: (0, r[0], i, 0))
    grid_spec = pltpu.PrefetchScalarGridSpec(
        num_scalar_prefetch=1, grid=(A // ta,),
        in_specs=[g_spec, pl.BlockSpec((3, ta, B), lambda i, r: (0, i, 0))],
        out_specs=pl.BlockSpec((None, ta, B), lambda i, r: (0, i, 0)))
    return pl.pallas_call(
        body, name=name, grid_spec=grid_spec, out_shape=jax.ShapeDtypeStruct((1, A, B), F32),
        compiler_params=_cparams(("parallel",)),
    )(chip_idx, g, got)


def _swap_copies(src_refs, land_refs, send_sems, recv_sems):
    x, y, c = _mesh_pos()
    return [pltpu.make_async_remote_copy(src_ref=s, dst_ref=d, send_sem=send_sems.at[k], recv_sem=recv_sems.at[k],
                                         device_id=(x, y, 1 - c), device_id_type=MESH)
            for k, (s, d) in enumerate(zip(src_refs, land_refs))]


def _swap_start(sums, after, tag):
    n = len(sums)

    def body(*refs):
        send_sems, recv_sems = refs[2 * n + 1], refs[2 * n + 2]
        s_out = refs[2 * n + 3:3 * n + 3]
        lands = refs[3 * n + 3:4 * n + 3]
        token = refs[4 * n + 3]
        for cp in _swap_copies(s_out, lands, send_sems, recv_sems):
            cp.start()
        token[...] = jnp.zeros_like(token)

    lands = [pltpu.with_memory_space_constraint(lax.empty(s.shape, s.dtype), pltpu.HBM) for s in sums]
    res = pl.pallas_call(
        body, name=f"rs_swap_start_{tag}", in_specs=[HBM] * (2 * n) + [pl.BlockSpec(memory_space=pl.ANY)],
        out_specs=(SEM, SEM) + (HBM,) * (2 * n) + (pl.BlockSpec(memory_space=pltpu.VMEM),),
        out_shape=(pltpu.SemaphoreType.DMA((n,)), pltpu.SemaphoreType.DMA((n,)))
        + tuple(pltpu.HBM(s.shape, s.dtype) for s in sums) * 2 + (jax.ShapeDtypeStruct((8, 128), F32),),
        input_output_aliases={k: 2 + k for k in range(2 * n)},
        compiler_params=pltpu.CompilerParams(has_side_effects=DATAFLOW),
    )(*[pltpu.with_memory_space_constraint(s, pltpu.HBM) for s in sums], *lands, after)
    return res[0], res[1], list(res[2:2 + n]), list(res[2 + n:2 + 2 * n]), res[2 + 2 * n]


def _swap_wait(send_sems, recv_sems, sums, lands, after, tag):
    n = len(sums)

    def body(*refs):
        send_sems, recv_sems = refs[2 * n], refs[2 * n + 1]
        for cp in _swap_copies(refs[:n], refs[n:2 * n], send_sems, recv_sems):
            cp.wait_send()
            cp.wait_recv()

    both = list(sums) + list(lands)
    res = pl.pallas_call(
        body, name=f"rs_swap_wait_{tag}", in_specs=[HBM] * (2 * n) + [SEM, SEM, pl.BlockSpec(memory_space=pl.ANY)],
        out_specs=[HBM] * (2 * n), out_shape=[pltpu.HBM(a.shape, a.dtype) for a in both],
        input_output_aliases={k: k for k in range(2 * n)},
        compiler_params=pltpu.CompilerParams(has_side_effects=DATAFLOW),
    )(*both, send_sems, recv_sems, after)
    return list(res[:n]), list(res[n:])


def _pair_swap(sums, small, *, tag):
    ns = len(sums)
    with_small = small is not None
    n_in = ns + with_small

    def body(*refs):
        sum_refs = refs[:ns]
        got_refs = refs[n_in:n_in + ns]
        send_sems, recv_sems = refs[2 * n_in], refs[2 * n_in + 1]
        x, y, c = _mesh_pos()
        cps = []
        for k in range(ns):
            cp = pltpu.make_async_remote_copy(
                src_ref=sum_refs[k], dst_ref=got_refs[k], send_sem=send_sems.at[k], recv_sem=recv_sems.at[k],
                device_id=(x, y, 1 - c), device_id_type=MESH)
            cp.start()
            cps.append(cp)
        if with_small:
            small_ref, sm_ref, local_sem = refs[ns], refs[n_in + ns], refs[2 * n_in + 2]
            me = 4 * x + 2 * y + c
            own = pltpu.make_async_copy(small_ref, sm_ref.at[me], local_sem)
            own.start()
            for r in range(1, N_DEV):
                fx, fy, fc = (r >> 2) & 1, (r >> 1) & 1, r & 1
                peer = (1 - x if fx else x, 1 - y if fy else y, 1 - c if fc else c)
                cp = pltpu.make_async_remote_copy(
                    src_ref=small_ref, dst_ref=sm_ref.at[me], send_sem=send_sems.at[ns + r],
                    recv_sem=recv_sems.at[ns + r], device_id=peer, device_id_type=MESH)
                cp.start()
                cps.append(cp)
        for cp in cps:
            cp.wait()
        if with_small:
            own.wait()

    out_shape = [jax.ShapeDtypeStruct(s.shape, s.dtype) for s in sums]
    scratch = [pltpu.SemaphoreType.DMA((ns + N_DEV,)), pltpu.SemaphoreType.DMA((ns + N_DEV,))]
    args = list(sums)
    if with_small:
        out_shape.append(jax.ShapeDtypeStruct((N_DEV,) + small.shape, F32))
        scratch.append(pltpu.SemaphoreType.DMA)
        args.append(small)
    res = pl.pallas_call(
        body, name=f"rs_pair_swap_{tag}", in_specs=[ANY] * n_in, out_specs=[ANY] * n_in, out_shape=out_shape,
        scratch_shapes=scratch,
    )(*args)
    return list(res[:ns]), (res[ns] if with_small else None)


def _tiny_pack(conv_a_w, sc_conv_w):
    lead = conv_a_w.shape[:-2]
    sc = sc_conv_w.reshape(lead + (2 * SC_WIDTH, 128))
    z = lambda r: jnp.zeros(lead + (r, 128), F32)
    return jnp.concatenate([conv_a_w, z(32 - CONV_A_WIDTH), sc, z(TINY_ROWS - 32 - 2 * SC_WIDTH)], axis=-2)


def _tiny_unpack(t):
    lead = t.shape[:-2]
    return t[..., :CONV_A_WIDTH, :], t[..., 32:32 + 2 * SC_WIDTH, :].reshape(lead + (SC_WIDTH, 256))


def _pack_small(d):
    parts = []
    for n in SMALL_NAMES:
        flat = d[n].astype(F32).reshape(-1)
        parts.append(jnp.pad(flat, (0, -flat.shape[0] % 128)))
    flat = jnp.concatenate(parts)
    return jnp.pad(flat, (0, -flat.shape[0] % 1024)).reshape(-1, 128)


def _ffn_fwd(h, g, W, n_gu, n_down, i, tag):
    h2, u, gate, up, a = _ffn_fwd_fused(h, g, W[n_gu][i], W[n_down][i], 0, name=f"{tag}_fwd")
    return h2, (h, u, gate, up, a)


def _ffn_bwd(dh, saved, g, W, n_gu, n_down, i, G, tag):
    h, u, gate, up, a = saved
    dh_in, dg, dgate, dup = _ffn_bwd_fused(dh, h, g, gate, up, W[n_gu][i], W[n_down][i], 0, name=f"{tag}_bwd")
    G[(n_down, i)] = _mm(a, dh, name=f"{tag}_b_wdown", ta=True, tm=1408, tn=1024, tk=1024, scale=0.5)
    tn = FFN_CHUNK
    half = _mm(u, dgate, name=f"{tag}_b_wg", ta=True, tm=1024, tn=tn, tk=2048, stack=(1, 0, None, 2 * D_FF))
    G[(n_gu, i)] = _mm(u, dup, name=f"{tag}_b_wu", ta=True, tm=1024, tn=tn, tk=2048, stack=(1, 0, half, 2 * D_FF),
                       n_map=lambda j: j + D_FF // tn)
    return dh_in, dg


def _xa_block_fwd(h, mem, g, gm, W, i, tag):
    mn = _rms_fwd(mem, gm, name=f"{tag}_mem_norm")
    u, q = _norm_mm(h, g, W['xa_wq'][i], 0, name=f"{tag}_q")
    kv = _mm(mn, W['xa_wkv'][i], b_layer=0, name=f"{tag}_kv", tm=256, tn=1024, tk=1024)
    o = _xa_fwd(q, kv, name=f"{tag}_attn")
    h2 = _mm(o, W['xa_wo'][i], b_layer=0, name=f"{tag}_o", out_dtype=F32, tm=1024, tn=1024, tk=1024, res=h)
    return h2, (h, u, mn, q, kv, o)


def _xa_block_bwd(dh, saved, mem, g, gm, W, i, G, tag):
    h, u, mn, q, kv, o = saved
    do = _mm(dh, W['xa_wo'][i], b_layer=0, name=f"{tag}_b_do", tb=True, tm=1024, tn=1024, tk=1024)
    G[('xa_wo', i)] = _mm(o, dh, name=f"{tag}_b_wo", ta=True, tm=1024, tn=1024, tk=1024)
    dq, dkv = _xa_bwd(q, kv, do, name=f"{tag}_b_attn")
    G[('xa_wq', i)] = _mm(u, dq, name=f"{tag}_b_wq", ta=True, tm=1024, tn=1024, tk=1024)
    dh_in, dg = _mm_norm_bwd(dq, W['xa_wq'][i], 0, h, g, dh, name=f"{tag}_b_du", tk=1024)
    G[('xa_wkv', i)] = _mm(mn, dkv, name=f"{tag}_b_wkv", ta=True, tm=1024, tn=1024, tk=256)
    dmn = _mm(dkv, W['xa_wkv'][i], b_layer=0, name=f"{tag}_b_dmn", tb=True, out_dtype=F32, tm=256, tn=1024, tk=1024)
    _, dgm = _rms_bwd(mem, gm, dmn, None, name=f"{tag}_b_mem_norm")
    return dh_in, dg, dgm


def _pad_conv_w(w, rows):
    return jnp.pad(w.astype(F32), ((0, rows - w.shape[0]), (0, 0)))


def _even_fwd(h, g, W, conv_w, conv_b, ln_g, ln_b, sinks, tag):
    u, z = _norm_mm(h, g, W['even_w_in'][0], 0, name=f"{tag}_in")
    c, cat = _conv_a_fwd(z, conv_w, conv_b, ln_g, ln_b, name=f"{tag}_conv")
    kpad = jnp.pad(z[:, 1536:1664], ((WINDOW, 0), (0, 0)))
    vpad = jnp.pad(z[:, 1664:1792], ((WINDOW, 0), (0, 0)))
    cat = _swa_fwd(z, kpad, vpad, sinks, cat, name=f"{tag}_swa")
    h2 = _mm(cat, W['even_w_out'][0], b_layer=0, name=f"{tag}_out", out_dtype=F32, tm=1024, tn=1024, tk=1024, res=h)
    return h2, (h, u, z, c, kpad, vpad, cat)


def _even_bwd(dh, saved, g, W, conv_w, ln_g, ln_b, sinks, G, tag):
    h, u, z, c, kpad, vpad, cat = saved
    dcat = _mm(dh, W['even_w_out'][0], b_layer=0, name=f"{tag}_b_dcat", tb=True, tm=1024, tn=1024, tk=1024)
    G[('even_w_out', 0)] = _mm(cat, dh, name=f"{tag}_b_wout", ta=True, tm=1024, tn=1024, tk=1024)
    dz_a, small = _conv_a_bwd(z, c, dcat, conv_w, ln_g, ln_b, name=f"{tag}_b_conv")
    dq, dkp, dvp, dsinks = _swa_bwd(z, kpad, vpad, sinks, dcat, name=f"{tag}_b_swa")
    dz = jnp.concatenate([dz_a, dq, dkp[WINDOW:].astype(BF16), dvp[WINDOW:].astype(BF16)], axis=-1)
    G[('even_w_in', 0)] = _mm(u, dz, name=f"{tag}_b_win", ta=True, tm=1024, tn=1792, tk=1024)
    dh_in, dg = _mm_norm_bwd(dz, W['even_w_in'][0], 0, h, g, dh, name=f"{tag}_b_du", tk=1792)
    grads = dict(mix=dg, conv_a_w=small[:CONV_A_WIDTH], conv_a_b=small[32:33], conv_a_ln_g=small[33:34],
                 conv_a_ln_b=small[34:35], swa_sinks=dsinks[:, 0])
    return dh_in, grads


def _odd_fwd(h, g, W, conv_w, tag):
    u, z = _norm_mm(h, g, W['odd_w_in'][0], 0, name=f"{tag}_in")
    y, cc = _sconv_fwd(z, conv_w, name=f"{tag}_conv")
    h2 = _mm(y, W['odd_w_out'][0], b_layer=0, name=f"{tag}_out", out_dtype=F32, tm=1024, tn=1024, tk=1024, res=h)
    return h2, (h, u, z, y, cc)


def _odd_bwd(dh, saved, g, W, conv_w, G, tag):
    h, u, z, y, cc = saved
    dy = _mm(dh, W['odd_w_out'][0], b_layer=0, name=f"{tag}_b_dy", tb=True, tm=1024, tn=1024, tk=1024)
    G[('odd_w_out', 0)] = _mm(y, dh, name=f"{tag}_b_wout", ta=True, tm=1024, tn=1024, tk=1024)
    dz, dw = _sconv_bwd(z, cc, dy, conv_w, name=f"{tag}_b_conv")
    G[('odd_w_in', 0)] = _mm(u, dz, name=f"{tag}_b_win", ta=True, tm=1024, tn=1024, tk=1024)
    dh_in, dg = _mm_norm_bwd(dz, W['odd_w_in'][0], 0, h, g, dh, name=f"{tag}_b_du", tk=1024)
    return dh_in, dict(mix=dg, sc_conv_w=dw[:SC_WIDTH])


def _local_step(x, mem, tgt, W, need, token, ready, conv_a_w, sc_conv_w, P):
    row = lambda v: v.reshape(1, -1)
    conv_a_w = _pad_conv_w(conv_a_w, 32)
    sc_w = _pad_conv_w(sc_conv_w, 8)
    sinks = P['swa_sinks'][0]

    def arrive(stage, h):
        for n, ws in need(stage, h).items():
            W[n] = W.get(n, []) + ws

    h = x
    saved = []
    for i in range(2):
        t = f"l{i}"
        if i == 1:
            arrive('l1_ffn1', h)
        g1 = row(P['ffn1_norm'][i]) + (token if i == 0 else 0.0)
        h, s1 = _ffn_fwd(h, g1, W, 'ffn1_w_gu', 'ffn1_w_down', i, f"{t}_ffn1")
        arrive(f"{t}_mix", h)
        if i == 0:
            h, s2 = _even_fwd(h, row(P['mix_norm'][i]), W, conv_a_w, P['conv_a_b'], P['conv_a_ln_g'],
                              P['conv_a_ln_b'], sinks, f"{t}_even")
        else:
            h, s2 = _odd_fwd(h, row(P['mix_norm'][i]), W, sc_w, f"{t}_odd")
        h, s3 = _xa_block_fwd(h, mem, row(P['xa_norm'][i]), row(P['xa_mem_norm'][i]), W, i, f"{t}_xa")
        arrive(f"{t}_ffn2", h)
        h, s4 = _ffn_fwd(h, row(P['ffn2_norm'][i]), W, 'ffn2_w_gu', 'ffn2_w_down', i, f"{t}_ffn2")
        saved.append((s1, s2, s3, s4))

    loss, dh, d_final = _final_loss(h, row(P['final_norm']), tgt, name="final_loss")

    G = {}
    gp = {n: [None, None] for n in ('ffn1_norm', 'mix_norm', 'xa_norm', 'xa_mem_norm', 'ffn2_norm')}
    single = {}
    for i in (1, 0):
        t = f"l{i}"
        s1, s2, s3, s4 = saved[i]
        g4 = row(P['ffn2_norm'][i]) + (ready('l1', G) if i == 0 else 0.0)
        dh, gp['ffn2_norm'][i] = _ffn_bwd(dh, s4, g4, W, 'ffn2_w_gu', 'ffn2_w_down', i, G, f"{t}_ffn2")
        dh, gp['xa_norm'][i], gp['xa_mem_norm'][i] = _xa_block_bwd(
            dh, s3, mem, row(P['xa_norm'][i]), row(P['xa_mem_norm'][i]), W, i, G, f"{t}_xa")
        if i == 0:
            dh, g2 = _even_bwd(dh, s2, row(P['mix_norm'][i]), W, conv_a_w, P['conv_a_ln_g'], P['conv_a_ln_b'], sinks,
                               G, f"{t}_even")
        else:
            dh, g2 = _odd_bwd(dh, s2, row(P['mix_norm'][i]), W, sc_w, G, f"{t}_odd")
        gp['mix_norm'][i] = g2.pop('mix')
        single.update(g2)
        g1 = row(P['ffn1_norm'][i]) + (ready('l0_rest', G) if i == 0 else 0.0)
        dh, gp['ffn1_norm'][i] = _ffn_bwd(dh, s1, g1, W, 'ffn1_w_gu', 'ffn1_w_down', i, G, f"{t}_ffn1")

    small = {n: jnp.concatenate(v, axis=0) for n, v in gp.items()}
    small['conv_a_b'] = single['conv_a_b']
    small['conv_a_ln_g'] = single['conv_a_ln_g']
    small['conv_a_ln_b'] = single['conv_a_ln_b']
    small['swa_sinks'] = single['swa_sinks'][None]
    small['final_norm'] = d_final[0]
    small['conv_a_w'] = single['conv_a_w']
    small['sc_conv_w'] = single['sc_conv_w']
    return loss[0, 0], dh, G, small


def kernel(x, mem, ffn1_norm, ffn1_w_gu, ffn1_w_down, mix_norm, even_w_in, conv_a_w, conv_a_b, conv_a_ln_g, conv_a_ln_b, swa_sinks, even_w_out, odd_w_in, sc_conv_w, odd_w_out, xa_norm, xa_mem_norm, xa_wq, xa_wkv, xa_wo, ffn2_norm, ffn2_w_gu, ffn2_w_down, final_norm, loss_target, m_ffn1_norm, m_ffn1_w_gu, m_ffn1_w_down, m_mix_norm, m_even_w_in, m_conv_a_w, m_conv_a_b, m_conv_a_ln_g, m_conv_a_ln_b, m_swa_sinks, m_even_w_out, m_odd_w_in, m_sc_conv_w, m_odd_w_out, m_xa_norm, m_xa_mem_norm, m_xa_wq, m_xa_wkv, m_xa_wo, m_ffn2_norm, m_ffn2_w_gu, m_ffn2_w_down, m_final_norm, v_ffn1_norm, v_ffn1_w_gu, v_ffn1_w_down, v_mix_norm, v_even_w_in, v_conv_a_w, v_conv_a_b, v_conv_a_ln_g, v_conv_a_ln_b, v_swa_sinks, v_even_w_out, v_odd_w_in, v_sc_conv_w, v_odd_w_out, v_xa_norm, v_xa_mem_norm, v_xa_wq, v_xa_wkv, v_xa_wo, v_ffn2_norm, v_ffn2_w_gu, v_ffn2_w_down, v_final_norm):
    w = dict(zip(WEIGHT_NAMES, (ffn1_norm, ffn1_w_gu, ffn1_w_down, mix_norm, even_w_in, conv_a_w, conv_a_b, conv_a_ln_g, conv_a_ln_b, swa_sinks, even_w_out, odd_w_in, sc_conv_w, odd_w_out, xa_norm, xa_mem_norm, xa_wq, xa_wkv, xa_wo, ffn2_norm, ffn2_w_gu, ffn2_w_down, final_norm)))
    m = dict(zip(WEIGHT_NAMES, (m_ffn1_norm, m_ffn1_w_gu, m_ffn1_w_down, m_mix_norm, m_even_w_in, m_conv_a_w, m_conv_a_b, m_conv_a_ln_g, m_conv_a_ln_b, m_swa_sinks, m_even_w_out, m_odd_w_in, m_sc_conv_w, m_odd_w_out, m_xa_norm, m_xa_mem_norm, m_xa_wq, m_xa_wkv, m_xa_wo, m_ffn2_norm, m_ffn2_w_gu, m_ffn2_w_down, m_final_norm)))
    v = dict(zip(WEIGHT_NAMES, (v_ffn1_norm, v_ffn1_w_gu, v_ffn1_w_down, v_mix_norm, v_even_w_in, v_conv_a_w, v_conv_a_b, v_conv_a_ln_g, v_conv_a_ln_b, v_swa_sinks, v_even_w_out, v_odd_w_in, v_sc_conv_w, v_odd_w_out, v_xa_norm, v_xa_mem_norm, v_xa_wq, v_xa_wkv, v_xa_wo, v_ffn2_norm, v_ffn2_w_gu, v_ffn2_w_down, v_final_norm)))
    cx, cy, cc = lax.axis_index("x"), lax.axis_index("y"), lax.axis_index("c")
    chip_idx = (2 * cx + cy).astype(jnp.int32).reshape(1)

    shards = {n: w[n] for n in COMM_NAMES if n != 'tiny'}
    shards['tiny'] = _tiny_pack(conv_a_w, sc_conv_w)
    first =[('ffn1_w_gu', 0), ('ffn1_w_down', 0), ('tiny', 0)]
    stages = {
        'l0_mix': [('even_w_in', 0), ('even_w_out', 0), ('xa_wq', 0), ('xa_wkv', 0), ('xa_wo', 0)],
        'l0_ffn2': [('ffn2_w_gu', 0), ('ffn2_w_down', 0)],
        'l1_ffn1': [('ffn1_w_gu', 1), ('ffn1_w_down', 1)],
        'l1_mix': [('odd_w_in', 0), ('odd_w_out', 0), ('xa_wq', 1), ('xa_wkv', 1), ('xa_wo', 1)],
        'l1_ffn2': [('ffn2_w_gu', 1), ('ffn2_w_down', 1)],
    }
    grad_stages = {'l1': stages['l1_ffn1'] + stages['l1_mix'] + stages['l1_ffn2'],
                   'l0_rest': stages['l0_mix'] + stages['l0_ffn2'], 'l0_ffn1': first}

    def place(items, after=None):
        out = []
        for n, l in items:
            out.append(_place(shards[n], l, LAYOUT[n], chip_idx, F32 if n == 'tiny' else BF16,
                              name=f"place_{n}_{l}", after=after))
            after = out[-1] if after is not None else None
        return out

    def item_meta(items):
        return [(LAYOUT[n], 1) + shards[n].shape[1:] for n, l in items]

    def natural(items, arrays):
        out = {}
        for (n, l), a in zip(items, arrays):
            if n == 'tiny':
                continue
            if n == 'even_w_in':
                out[n] = [a.transpose(0, 2, 1, 3).reshape(1, D_MODEL, -1)]
            else:
                out[n] = [a if LAYOUT[n] == 'col' else a.reshape(1, N_CHIPS * a.shape[2], a.shape[3])]
        return out

    meta_first = item_meta(first)
    send, recv, in_flight, token = _gather_start(place(first), meta_first, chip_idx, "first_ici", _half_part_copies)
    placed, last = {}, token
    for stage, items in stages.items():
        placed[stage] = place(items, last)
        last = placed[stage][-1]
    landed = _gather_wait(send, recv, in_flight, meta_first, last, "first_ici", _half_part_copies)
    send, recv, in_flight, token = _gather_start(landed, meta_first, token, "first_d2d", _forward_copies)
    first_d2d = (send, recv, in_flight)
    gathers = {}
    for stage, items in stages.items():
        send, recv, in_flight, token = _gather_start(placed[stage], item_meta(items), token, stage)
        gathers[stage] = (send, recv, in_flight)
    first_full = _gather_wait(*first_d2d, meta_first, token, "first_d2d", _forward_copies)
    W = natural(first, first_full)
    ca, sc = _tiny_unpack(first_full[-1][0])
    conv_a_full = ca.transpose(1, 0, 2).reshape(CONV_A_WIDTH, CONV_A_CH)
    sc_full = sc.transpose(1, 0, 2).reshape(SC_WIDTH, SC_CH)

    def need(stage, h):
        send, recv, in_flight = gathers[stage]
        items = stages[stage]
        return natural(items, _gather_wait(send, recv, in_flight, item_meta(items), h, stage))

    def gathered_layout(G, item):
        n, l = item
        A, B = shards[n].shape[1:]
        g = G[item]
        if n == 'tiny':
            return g
        if n == 'even_w_in':
            return g.reshape(A, N_CHIPS, B).transpose(1, 0, 2)[None]
        return g.reshape(1, A, N_CHIPS * B) if LAYOUT[n] == 'col' else g.reshape(1, N_CHIPS, A, B)

    scatters, tokens = {}, {}

    def ready(stage, G):
        items = grad_stages[stage]
        send, recv, gs1, lands, tok = _scatter_start([gathered_layout(G, it) for it in items], item_meta(items),
                                                     chip_idx, stage)
        scatters[stage] = (send, recv, gs1, lands)
        tokens[stage] = tok
        return tok[:1, :1]

    loss_part, grad_x, G, g_small = _local_step(x[0], mem[0], loss_target[0], W, need, token[:1, :1], ready,
                                                conv_a_full, sc_full, {n: w[n] for n in SMALL_NAMES})
    G[('tiny', 0)] = _tiny_pack(g_small['conv_a_w'].reshape(CONV_A_WIDTH, N_CHIPS, 128).transpose(1, 0, 2),
                                g_small['sc_conv_w'].reshape(SC_WIDTH, N_CHIPS, 256).transpose(1, 0, 2))[None]
    loss = lax.psum(loss_part, ("x", "y", "c"))

    ready('l0_ffn1', G)
    started = tokens['l0_ffn1']

    def summed(stage, after):
        send, recv, gs1, lands = scatters[stage]
        items = grad_stages[stage]
        sent, landed = _scatter_wait(send, recv, gs1, lands, item_meta(items), after, stage)
        return items, [_chip_sum_full(g, r, m_, chip_idx, name=f"rs_chip_sum_{n}_{l}")
                       for (n, l), g, r, m_ in zip(items, sent, landed, item_meta(items))]

    def adamw(n, g1):
        if n == 'tiny':
            pk = lambda d: _tiny_pack(d['conv_a_w'], d['sc_conv_w'])
            res = [_tiny_unpack(a) for a in _adamw_layers(pk(w), pk(m), pk(v), [g1[('tiny', 0)]], name="adamw_tiny")]
            for k, nn in enumerate(('conv_a_w', 'sc_conv_w')):
                grads[nn], deltas[nn], new_m[nn], new_v[nn] = (r[k] for r in res)
        else:
            gsrc = [g1[(n, l)] for l in range(w[n].shape[0])]
            grads[n], deltas[n], new_m[n], new_v[n] = _adamw_layers(w[n], m[n], v[n], gsrc, name=f"adamw_{n}")

    grads, deltas, new_m, new_v = {}, {}, {}, {}
    sum_of = {}
    for stage in ('l1', 'l0_rest'):
        its, ss = summed(stage, started)
        sum_of.update(zip(its, ss))
    swap_groups = [['even_w_in', 'even_w_out', 'odd_w_in', 'odd_w_out', 'xa_wq', 'xa_wkv', 'xa_wo'],
                   ['ffn2_w_gu', 'ffn2_w_down'], ['ffn1_w_gu', 'ffn1_w_down']]
    swaps, after = [], started
    for gi, names in enumerate(swap_groups):
        its = [it for it in sum_of if it[0] in names]
        send, recv, own, lands, after = _swap_start([sum_of[it] for it in its], after, f"g{gi}")
        swaps.append((its, send, recv, own, lands))
    _, small_parts = _pair_swap([], _pack_small(g_small), tag="small")
    g1 = {}

    def swapped(gi, after):
        its, send, recv, own, lands = swaps[gi]
        mine, theirs = _swap_wait(send, recv, own, lands, after, f"g{gi}")
        g1.update({it: [a, b] for it, a, b in zip(its, mine, theirs)})

    for gi in (0, 1):
        swapped(gi, after)
        for n in swap_groups[gi]:
            adamw(n, g1)
        after = deltas[swap_groups[gi][-1]]

    swapped(2, after)
    its, ss = summed('l0_ffn1', after)
    sib, _ = _pair_swap(ss, None, tag="last")
    g1.update({it: [a, b] for it, a, b in zip(its, ss, sib)})
    for n in ('ffn1_w_gu', 'ffn1_w_down', 'tiny'):
        adamw(n, g1)
    rows2d = lambda d: [d[n].reshape(-1, d[n].shape[-1]) for n in SMALL_NAMES]
    for dst, arrs in zip((grads, deltas, new_m, new_v),
                         _adamw_small(rows2d(w), rows2d(m), rows2d(v), small_parts, name="adamw_small")):
        dst.update({n: a.reshape(w[n].shape) for n, a in zip(SMALL_NAMES, arrs)})

    return (loss, grad_x[None], *[grads[n] for n in WEIGHT_NAMES], *[deltas[n] for n in WEIGHT_NAMES],
            *[new_m[n] for n in WEIGHT_NAMES], *[new_v[n] for n in WEIGHT_NAMES])
```

```python
import jax
import jax.numpy as jnp
from jax import lax
from jax.experimental import pallas as pl
from jax.experimental.pallas import tpu as pltpu

F32 = jnp.float32
BF16 = jnp.bfloat16

D_MODEL = 1024
D_FF = 2816
CONV_A_CH = 512
CONV_A_WIDTH = 31
SWA_HEADS = 8
SWA_KV_HEADS = 2
SWA_GROUP = 4
HEAD_DIM = 64
WINDOW = 128
SC_CH = 1024
SC_WIDTH = 3
XA_HEADS = 4
XA_HEAD_DIM = 256
RMS_EPS = 1e-6
LN_EPS = 1e-5

ADAM_LR = 0.001
ADAM_B1 = 0.9
ADAM_B2 = 0.999
ADAM_EPS = 1e-08
ADAM_WD = 0.01
ADAM_STEP = 10
ADAM_TILE_ELEMS = 384 * 1024

N_CHIPS = 4
N_DEV = 8
NEG_BIG = -1e30
VMEM_LIMIT = 56 * 1024 * 1024
MESH = pl.DeviceIdType.MESH

INPUT_NAMES = ['x', 'mem', 'ffn1_norm', 'ffn1_w_gu', 'ffn1_w_down', 'mix_norm', 'even_w_in', 'conv_a_w', 'conv_a_b',
               'conv_a_ln_g', 'conv_a_ln_b', 'swa_sinks', 'even_w_out', 'odd_w_in', 'sc_conv_w', 'odd_w_out', 'xa_norm',
               'xa_mem_norm', 'xa_wq', 'xa_wkv', 'xa_wo', 'ffn2_norm', 'ffn2_w_gu', 'ffn2_w_down', 'final_norm']
WEIGHT_NAMES = INPUT_NAMES[2:]
BIG = [('ffn1_w_gu', 'col'), ('ffn1_w_down', 'row'), ('even_w_in', 'col'), ('conv_a_w', 'col'), ('even_w_out', 'row'),
       ('odd_w_in', 'col'), ('sc_conv_w', 'col'), ('odd_w_out', 'row'), ('xa_wq', 'row'), ('xa_wkv', 'col'),
       ('xa_wo', 'row'), ('ffn2_w_gu', 'col'), ('ffn2_w_down', 'row')]
BIG_NAMES = [n for n, _ in BIG]
SMALL_NAMES = [n for n in WEIGHT_NAMES if n not in BIG_NAMES]


def _cparams(sem=None, vmem=VMEM_LIMIT):
    kw = dict(vmem_limit_bytes=vmem)
    if sem is not None:
        kw['dimension_semantics'] = sem
    return pltpu.CompilerParams(**kw)


def _div_tile(n, want, align=8):
    if n <= want:
        return n
    t = (want // align) * align
    while t >= align:
        if n % t == 0:
            return t
        t -= align
    return n


def _mm(a, b, *, name, ta=False, tb=False, out_dtype=BF16, tm=512, tn=512, tk=512, res=None, scale=1.0,
        b_layer=None, stack=None, n_map=None):
    n_map = n_map or (lambda j: j)
    if ta:
        K, M = a.shape
    else:
        M, K = a.shape
    if tb:
        N, K2 = b.shape[-2:]
    else:
        K2, N = b.shape[-2:]
    assert K == K2, (a.shape, b.shape, ta, tb)
    tm = _div_tile(M, tm, 128 if ta else 16)
    tn = _div_tile(N, tn, 128)
    tk = _div_tile(K, tk, 16 if ta else 128)
    nk = K // tk
    a_spec = pl.BlockSpec((tk, tm), lambda i, j, k: (k, i)) if ta else pl.BlockSpec((tm, tk), lambda i, j, k: (i, k))
    if b_layer is None:
        b_spec = pl.BlockSpec((tn, tk), lambda i, j, k: (j, k)) if tb else pl.BlockSpec((tk, tn), lambda i, j, k: (k, j))
    elif tb:
        b_spec = pl.BlockSpec((None, tn, tk), lambda i, j, k: (b_layer, j, k))
    else:
        b_spec = pl.BlockSpec((None, tk, tn), lambda i, j, k: (b_layer, k, j))
    o_spec = pl.BlockSpec((tm, tn), lambda i, j, k: (i, j))
    out_shape = jax.ShapeDtypeStruct((M, N), out_dtype)
    out_spec = o_spec
    aliases = {}
    extra_specs, extra_args = [], ()
    if stack is not None:
        n_layers, layer, buf = stack[:3]
        n_total = stack[3] if len(stack) > 3 else N
        out_shape = jax.ShapeDtypeStruct((n_layers, M, n_total), out_dtype)
        out_spec = pl.BlockSpec((None, tm, tn), lambda i, j, k: (layer, i, n_map(j)))
        if buf is not None:
            extra_specs, extra_args = [pl.BlockSpec(memory_space=pl.ANY)], (buf,)
            aliases = {2 + (res is not None): 0}
    dims = (((0 if ta else 1,), (1 if tb else 0,)), ((), ()))
    has_res = res is not None
    n_extra = len(extra_args)

    def body(*refs):
        if n_extra:
            refs = refs[:2 + has_res] + refs[2 + has_res + n_extra:]
        if has_res:
            a_ref, b_ref, r_ref, o_ref, acc_ref = refs
        else:
            a_ref, b_ref, o_ref, acc_ref = refs
        k = pl.program_id(2)
        p = lax.dot_general(a_ref[...].astype(BF16), b_ref[...].astype(BF16), dims, preferred_element_type=F32)

        @pl.when(k == 0)
        def _():
            acc_ref[...] = p

        @pl.when(k > 0)
        def _():
            acc_ref[...] += p

        @pl.when(k == nk - 1)
        def _():
            r = acc_ref[...] * scale
            if has_res:
                r = r_ref[...] + r
            o_ref[...] = r.astype(out_dtype)

    in_specs = [a_spec, b_spec] + ([o_spec] if has_res else []) + extra_specs
    args = (a, b) + ((res,) if has_res else ()) + extra_args
    return pl.pallas_call(
        body, name=name, grid=(M // tm, N // tn, nk), in_specs=in_specs, out_specs=out_spec,
        out_shape=out_shape, input_output_aliases=aliases,
        scratch_shapes=[pltpu.VMEM((tm, tn), F32)],
        compiler_params=_cparams(("parallel", "parallel", "arbitrary")),
    )(*args)


def _rms_fwd(x, g, *, name):
    S, D = x.shape
    ts = _div_tile(S, 512)

    def body(x_ref, g_ref, o_ref):
        xv = x_ref[...]
        r = lax.rsqrt(jnp.mean(xv * xv, axis=-1, keepdims=True) + RMS_EPS)
        o_ref[...] = (xv * r * g_ref[...]).astype(BF16)

    return pl.pallas_call(
        body, name=name, grid=(S // ts,),
        in_specs=[pl.BlockSpec((ts, D), lambda i: (i, 0)), pl.BlockSpec((1, D), lambda i: (0, 0))],
        out_specs=pl.BlockSpec((ts, D), lambda i: (i, 0)),
        out_shape=jax.ShapeDtypeStruct((S, D), BF16),
        compiler_params=_cparams(("parallel",)),
    )(x, g)


NORM_SLAB = 256


def _norm_mm(h, g, w, layer, *, name):
    S, D = h.shape
    N = w.shape[-1]
    tm = _div_tile(S, 1024, NORM_SLAB)
    slab = min(NORM_SLAB, tm)

    def body(h_ref, g_ref, w_ref, u_ref, z_ref):
        for r0 in range(0, tm, slab):
            rows = pl.ds(r0, slab)
            xv = h_ref[rows, :]
            r = lax.rsqrt(jnp.mean(xv * xv, axis=-1, keepdims=True) + RMS_EPS)
            u = (xv * r * g_ref[...]).astype(BF16)
            u_ref[rows, :] = u
            z_ref[rows, :] = jnp.dot(u, w_ref[...], preferred_element_type=F32).astype(BF16)

    row = pl.BlockSpec((tm, D), lambda i: (i, 0))
    return pl.pallas_call(
        body, name=name, grid=(S // tm,),
        in_specs=[row, pl.BlockSpec((1, D), lambda i: (0, 0)), pl.BlockSpec((None, D, N), lambda i: (layer, 0, 0))],
        out_specs=[row, pl.BlockSpec((tm, N), lambda i: (i, 0))],
        out_shape=[jax.ShapeDtypeStruct((S, D), BF16), jax.ShapeDtypeStruct((S, N), BF16)],
        compiler_params=_cparams(("parallel",)),
    )(h, g, w)


def _mm_norm_bwd(dz, w, layer, h, g, dres, *, name, tk):
    S, K = dz.shape
    D = h.shape[1]
    tm = _div_tile(S, 1024, NORM_SLAB)
    slab = min(NORM_SLAB, tm)
    tk = _div_tile(K, tk, 128)
    nk = K // tk
    nt = (((1,), (1,)), ((), ()))

    def body(dz_ref, w_ref, h_ref, g_ref, dr_ref, dx_ref, dg_ref, acc):
        i = pl.program_id(0)
        k = pl.program_id(1)

        def norm_bwd(du_of):
            part = jnp.zeros((1, D), F32)
            for r0 in range(0, tm, slab):
                rows = pl.ds(r0, slab)
                du = du_of(rows)
                xv = h_ref[rows, :]
                r = lax.rsqrt(jnp.mean(xv * xv, axis=-1, keepdims=True) + RMS_EPS)
                xhat = xv * r
                part = part + jnp.sum(du * xhat, axis=0, keepdims=True)
                dxhat = du * g_ref[...]
                dx_ref[rows, :] = dr_ref[rows, :] + r * (
                    dxhat - xhat * jnp.mean(dxhat * xhat, axis=-1, keepdims=True))

            @pl.when(i == 0)
            def _():
                dg_ref[...] = part

            @pl.when(i > 0)
            def _():
                dg_ref[...] += part

        if nk == 1:
            norm_bwd(lambda rows: lax.dot_general(dz_ref[rows, :], w_ref[...], nt, preferred_element_type=F32))
        else:
            p = lax.dot_general(dz_ref[...], w_ref[...], nt, preferred_element_type=F32)

            @pl.when(k == 0)
            def _():
                acc[...] = p

            @pl.when(k > 0)
            def _():
                acc[...] += p

            @pl.when(k == nk - 1)
            def _():
                norm_bwd(lambda rows: acc[rows, :])

    row = pl.BlockSpec((tm, D), lambda i, k: (i, 0))
    vec = pl.BlockSpec((1, D), lambda i, k: (0, 0))
    return pl.pallas_call(
        body, name=name, grid=(S // tm, nk),
        in_specs=[pl.BlockSpec((tm, tk), lambda i, k: (i, k)), pl.BlockSpec((None, D, tk), lambda i, k: (layer, 0, k)),
                  row, vec, row],
        out_specs=[row, vec],
        out_shape=[jax.ShapeDtypeStruct((S, D), F32), jax.ShapeDtypeStruct((1, D), F32)],
        scratch_shapes=[pltpu.VMEM((tm, D), F32)],
        compiler_params=_cparams(("arbitrary", "arbitrary")),
    )(dz, w, h, g, dres)


def _rms_bwd(x, g, du, dres, *, name):
    S, D = x.shape
    ts = _div_tile(S, 512)
    has_res = dres is not None

    def body(*refs):
        if has_res:
            x_ref, g_ref, du_ref, dr_ref, dx_ref, dg_ref = refs
        else:
            x_ref, g_ref, du_ref, dg_ref = refs
        i = pl.program_id(0)
        xv = x_ref[...]
        duv = du_ref[...].astype(F32)
        r = lax.rsqrt(jnp.mean(xv * xv, axis=-1, keepdims=True) + RMS_EPS)
        xhat = xv * r
        part = jnp.sum(duv * xhat, axis=0, keepdims=True)

        @pl.when(i == 0)
        def _():
            dg_ref[...] = part

        @pl.when(i > 0)
        def _():
            dg_ref[...] += part

        if has_res:
            dxhat = duv * g_ref[...]
            dx = r * (dxhat - xhat * jnp.mean(dxhat * xhat, axis=-1, keepdims=True))
            dx_ref[...] = dr_ref[...] + dx

    row = pl.BlockSpec((ts, D), lambda i: (i, 0))
    vec = pl.BlockSpec((1, D), lambda i: (0, 0))
    if has_res:
        dx, dg = pl.pallas_call(
            body, name=name, grid=(S // ts,), in_specs=[row, vec, row, row], out_specs=[row, vec],
            out_shape=[jax.ShapeDtypeStruct((S, D), F32), jax.ShapeDtypeStruct((1, D), F32)],
            compiler_params=_cparams(("arbitrary",)),
        )(x, g, du, dres)
        return dx, dg
    dg = pl.pallas_call(
        body, name=name, grid=(S // ts,), in_specs=[row, vec, row], out_specs=vec,
        out_shape=jax.ShapeDtypeStruct((1, D), F32),
        compiler_params=_cparams(("arbitrary",)),
    )(x, g, du)
    return None, dg


def _final_loss(h, g, tgt, *, name):
    S, D = h.shape
    ts = _div_tile(S, 512)

    def body(h_ref, g_ref, t_ref, loss_ref, dh_ref, dg_ref):
        i = pl.program_id(0)
        xv = h_ref[...]
        gv = g_ref[...]
        r = lax.rsqrt(jnp.mean(xv * xv, axis=-1, keepdims=True) + RMS_EPS)
        xhat = xv * r
        err = xhat * gv - t_ref[...]
        lpart = 0.5 * jnp.sum(jnp.mean(err * err, axis=-1, keepdims=True), axis=0, keepdims=True)
        dy = err * (1.0 / D)
        gpart = jnp.sum(dy * xhat, axis=0, keepdims=True)

        @pl.when(i == 0)
        def _():
            loss_ref[...] = jnp.broadcast_to(lpart, loss_ref.shape)
            dg_ref[...] = gpart

        @pl.when(i > 0)
        def _():
            loss_ref[...] += jnp.broadcast_to(lpart, loss_ref.shape)
            dg_ref[...] += gpart

        dxhat = dy * gv
        dh_ref[...] = r * (dxhat - xhat * jnp.mean(dxhat * xhat, axis=-1, keepdims=True))

    row = pl.BlockSpec((ts, D), lambda i: (i, 0))
    vec = pl.BlockSpec((1, D), lambda i: (0, 0))
    return pl.pallas_call(
        body, name=name, grid=(S // ts,), in_specs=[row, vec, row],
        out_specs=[pl.BlockSpec((8, 128), lambda i: (0, 0)), row, vec],
        out_shape=[jax.ShapeDtypeStruct((8, 128), F32), jax.ShapeDtypeStruct((S, D), F32),
                   jax.ShapeDtypeStruct((1, D), F32)],
        compiler_params=_cparams(("arbitrary",)),
    )(h, g, tgt)


def _sigmoid(x):
    return 1.0 / (1.0 + jnp.exp(-x))


FFN_CHUNK = 1408
FFN_CHUNKS = D_FF // FFN_CHUNK
FFN_BWD_PIECE = 384
FFN_BWD_SLAB = 256


def _ffn_fwd_fused(h, g, w_gu, w_down, layer, *, name):
    S, D = h.shape
    tm = _div_tile(S, 512, 16)
    tf, nj = FFN_CHUNK, FFN_CHUNKS

    def body(h_ref, g_ref, wg_ref, wu_ref, wd_ref, h2_ref, u_ref, gate_ref, up_ref, a_ref, u_s, acc):
        j = pl.program_id(1)

        @pl.when(j == 0)
        def _():
            xv = h_ref[...]
            r = lax.rsqrt(jnp.mean(xv * xv, axis=-1, keepdims=True) + RMS_EPS)
            u = (xv * r * g_ref[...]).astype(BF16)
            u_s[...] = u
            u_ref[...] = u

        u = u_s[...]
        gate = jnp.dot(u, wg_ref[...], preferred_element_type=F32)
        up = jnp.dot(u, wu_ref[...], preferred_element_type=F32)
        gate_ref[...] = gate.astype(BF16)
        up_ref[...] = up.astype(BF16)
        a = (gate * _sigmoid(gate) * up).astype(BF16)
        a_ref[...] = a
        p = jnp.dot(a, wd_ref[...], preferred_element_type=F32)

        @pl.when(j == 0)
        def _():
            acc[...] = p

        @pl.when(j > 0)
        def _():
            acc[...] += p

        @pl.when(j == nj - 1)
        def _():
            h2_ref[...] = h_ref[...] + 0.5 * acc[...]

    row = pl.BlockSpec((tm, D), lambda i, j: (i, 0))
    chunk = pl.BlockSpec((tm, tf), lambda i, j: (i, j))
    hidden = jax.ShapeDtypeStruct((S, D_FF), BF16)
    return pl.pallas_call(
        body, name=name, grid=(S // tm, nj),
        in_specs=[row, pl.BlockSpec((1, D), lambda i, j: (0, 0)),
                  pl.BlockSpec((None, D, tf), lambda i, j: (layer, 0, j)),
                  pl.BlockSpec((None, D, tf), lambda i, j: (layer, 0, nj + j)),
                  pl.BlockSpec((None, tf, D), lambda i, j: (layer, j, 0))],
        out_specs=[row, row, chunk, chunk, chunk],
        out_shape=[jax.ShapeDtypeStruct((S, D), F32), jax.ShapeDtypeStruct((S, D), BF16), hidden, hidden, hidden],
        scratch_shapes=[pltpu.VMEM((tm, D), BF16), pltpu.VMEM((tm, D), F32)],
        compiler_params=_cparams(("parallel", "arbitrary")),
    )(h, g, w_gu, w_gu, w_down)


def _ffn_bwd_fused(dh, h, g, gate, up, w_gu, w_down, layer, *, name):
    S, D = h.shape
    tm = _div_tile(S, 512, FFN_BWD_SLAB)
    tf = FFN_CHUNK
    nj = D_FF // tf
    slab = min(FFN_BWD_SLAB, tm)
    nt = (((1,), (1,)), ((), ()))
    pieces = [(c0, min(FFN_BWD_PIECE, tf - c0)) for c0 in range(0, tf, FFN_BWD_PIECE)]

    def body(dh_ref, h_ref, g_ref, gate_ref, up_ref, wg_ref, wu_ref, wd_ref, dx_ref, dg_ref, dgate_ref, dup_ref,
             dy_s, acc):
        i = pl.program_id(0)
        j = pl.program_id(1)

        @pl.when(j == 0)
        def _():
            for r0 in range(0, tm, slab):
                rows = pl.ds(r0, slab)
                dy_s[rows, :] = (0.5 * dh_ref[rows, :]).astype(BF16)

        p = None
        for c0, cw in pieces:
            cols = pl.ds(c0, cw)
            da = lax.dot_general(dy_s[...], wd_ref[cols, :], nt, preferred_element_type=F32)
            gt = gate_ref[:, cols].astype(F32)
            sg = _sigmoid(gt)
            dgate = (da * up_ref[:, cols].astype(F32) * sg * (1.0 + gt * (1.0 - sg))).astype(BF16)
            dup = (da * gt * sg).astype(BF16)
            dgate_ref[:, cols] = dgate
            dup_ref[:, cols] = dup
            q = (lax.dot_general(dgate, wg_ref[:, cols], nt, preferred_element_type=F32)
                 + lax.dot_general(dup, wu_ref[:, cols], nt, preferred_element_type=F32))
            p = q if p is None else p + q

        @pl.when(j == 0)
        def _():
            acc[...] = p

        @pl.when(j > 0)
        def _():
            acc[...] += p

        @pl.when(j == nj - 1)
        def _():
            part = jnp.zeros((1, D), F32)
            for r0 in range(0, tm, slab):
                rows = pl.ds(r0, slab)
                xv = h_ref[rows, :]
                du = acc[rows, :]
                r = lax.rsqrt(jnp.mean(xv * xv, axis=-1, keepdims=True) + RMS_EPS)
                xhat = xv * r
                part = part + jnp.sum(du * xhat, axis=0, keepdims=True)
                dxhat = du * g_ref[...]
                dx_ref[rows, :] = dh_ref[rows, :] + r * (
                    dxhat - xhat * jnp.mean(dxhat * xhat, axis=-1, keepdims=True))

            @pl.when(i == 0)
            def _():
                dg_ref[...] = part

            @pl.when(i > 0)
            def _():
                dg_ref[...] += part

    row = pl.BlockSpec((tm, D), lambda i, j: (i, 0))
    vec = pl.BlockSpec((1, D), lambda i, j: (0, 0))
    chunk = pl.BlockSpec((tm, tf), lambda i, j: (i, j))
    hidden = jax.ShapeDtypeStruct((S, D_FF), BF16)
    return pl.pallas_call(
        body, name=name, grid=(S // tm, nj),
        in_specs=[row, row, vec, chunk, chunk,
                  pl.BlockSpec((None, D, tf), lambda i, j: (layer, 0, j)),
                  pl.BlockSpec((None, D, tf), lambda i, j: (layer, 0, nj + j)),
                  pl.BlockSpec((None, tf, D), lambda i, j: (layer, j, 0))],
        out_specs=[row, vec, chunk, chunk],
        out_shape=[jax.ShapeDtypeStruct((S, D), F32), jax.ShapeDtypeStruct((1, D), F32), hidden, hidden],
        scratch_shapes=[pltpu.VMEM((tm, D), BF16), pltpu.VMEM((tm, D), F32)],
        compiler_params=_cparams(("arbitrary", "arbitrary")),
    )(dh, h, g, gate, up, w_gu, w_gu, w_down)


CONV_HALO = 32
CONV_SUB_ROWS = 128


def _shifted_taps(win, shifted, n, residues=tuple(range(1, 8))):
    for idx, r in enumerate(residues):
        shifted[idx] = win[pl.ds(r, n), :]

    def tap(start, rows, lanes):
        q, r = divmod(start, 8)
        if r == 0:
            return win[pl.ds(start, rows), lanes]
        return shifted[residues.index(r), pl.ds(8 * q, rows), lanes]

    return tap


def _conv_a_fwd(z, w, bias, ln_g, ln_b, *, name):
    S = z.shape[0]
    C = CONV_A_CH
    ts = _div_tile(S, 256, 32)

    def body(val_ref, gate_ref, w_ref, b_ref, g_ref, lb_ref, c_ref, act_ref, win, shifted):
        i = pl.program_id(0)

        @pl.when(i == 0)
        def _():
            win[pl.ds(0, CONV_HALO), :] = jnp.zeros((CONV_HALO, C), F32)

        @pl.when(i > 0)
        def _():
            win[pl.ds(0, CONV_HALO), :] = win[pl.ds(ts, CONV_HALO), :]

        a = val_ref[...].astype(F32) * _sigmoid(gate_ref[...].astype(F32))
        win[pl.ds(CONV_HALO, ts), :] = a
        tap = _shifted_taps(win, shifted, ts + CONV_HALO - 8)
        rs = min(CONV_SUB_ROWS, ts)
        for cb in range(C // 128):
            lanes = pl.ds(128 * cb, 128)
            for rt in range(ts // rs):
                sub = jnp.broadcast_to(b_ref[:, lanes], (rs, 128))
                for k in range(CONV_A_WIDTH):
                    sub = sub + w_ref[pl.ds(k, 1), lanes] * tap(
                        CONV_HALO - (CONV_A_WIDTH - 1) + k + rs * rt, rs, lanes)
                c_ref[pl.ds(rs * rt, rs), lanes] = sub
        acc = c_ref[...]
        mu = jnp.mean(acc, axis=-1, keepdims=True)
        xc = acc - mu
        var = jnp.mean(xc * xc, axis=-1, keepdims=True)
        ln = xc * lax.rsqrt(var + LN_EPS) * g_ref[...] + lb_ref[...]
        act_ref[...] = (ln * _sigmoid(ln)).astype(BF16)

    row = lambda col: pl.BlockSpec((ts, C), lambda i, col=col: (i, col))
    vec = pl.BlockSpec((1, C), lambda i: (0, 0))
    return pl.pallas_call(
        body, name=name, grid=(S // ts,),
        in_specs=[row(0), row(1), pl.BlockSpec((32, C), lambda i: (0, 0)), vec, vec, vec],
        out_specs=[row(0), row(0)],
        out_shape=[jax.ShapeDtypeStruct((S, C), F32), jax.ShapeDtypeStruct((S, 2 * C), BF16)],
        scratch_shapes=[pltpu.VMEM((ts + CONV_HALO, C), F32), pltpu.VMEM((7, ts + CONV_HALO - 8, C), F32)],
        compiler_params=_cparams(("arbitrary",)),
    )(z, z, w, bias, ln_g, ln_b)


def _conv_a_bwd(z, c, dcat, w, ln_g, ln_b, *, name):
    S = z.shape[0]
    C = CONV_A_CH
    ts = _div_tile(S, 256, 32)
    n = S // ts

    def body(val_ref, gate_ref, c_ref, da_ref, w_ref, g_ref, lb_ref, dz_ref, small_ref, win, a_s, da_s, dw8,
             shifted):
        i = pl.program_id(0)

        @pl.when(i == 0)
        def _():
            win[pl.ds(ts, CONV_HALO), :] = jnp.zeros((CONV_HALO, C), F32)
            small_ref[...] = jnp.zeros(small_ref.shape, F32)
            dw8[...] = jnp.zeros(dw8.shape, F32)

        @pl.when(i > 0)
        def _():
            win[pl.ds(ts, CONV_HALO), :] = win[pl.ds(0, CONV_HALO), :]

        cv = c_ref[...]
        gv = g_ref[...]
        mu = jnp.mean(cv, axis=-1, keepdims=True)
        xc = cv - mu
        var = jnp.mean(xc * xc, axis=-1, keepdims=True)
        rstd = lax.rsqrt(var + LN_EPS)
        xhat = xc * rstd
        ln = xhat * gv + lb_ref[...]
        sg = _sigmoid(ln)
        dln = da_ref[...].astype(F32) * (sg * (1.0 + ln * (1.0 - sg)))
        small_ref[pl.ds(33, 1), :] += jnp.sum(dln * xhat, axis=0, keepdims=True)
        small_ref[pl.ds(34, 1), :] += jnp.sum(dln, axis=0, keepdims=True)
        dxhat = dln * gv
        dc = rstd * (dxhat - jnp.mean(dxhat, axis=-1, keepdims=True)
                     - xhat * jnp.mean(dxhat * xhat, axis=-1, keepdims=True))
        small_ref[pl.ds(32, 1), :] += jnp.sum(dc, axis=0, keepdims=True)
        win[pl.ds(0, ts), :] = dc

        val = val_ref[...].astype(F32)
        sgg = _sigmoid(gate_ref[...].astype(F32))
        a_s[...] = val * sgg
        tap = _shifted_taps(win, shifted, ts + CONV_HALO - 8)
        rs = min(CONV_SUB_ROWS, ts)
        for cb in range(C // 128):
            lanes = pl.ds(128 * cb, 128)
            for rt in range(ts // rs):
                a_sub = a_s[pl.ds(rs * rt, rs), lanes]
                da = jnp.zeros((rs, 128), F32)
                for k in range(CONV_A_WIDTH):
                    sh = tap(CONV_A_WIDTH - 1 - k + rs * rt, rs, lanes)
                    da = da + w_ref[pl.ds(k, 1), lanes] * sh
                    prod = a_sub * sh
                    part = prod[0:8]
                    for r in range(1, rs // 8):
                        part = part + prod[8 * r:8 * r + 8]
                    dw8[pl.ds(8 * k, 8), lanes] += part
                da_s[pl.ds(rs * rt, rs), lanes] = da
        da = da_s[...]
        dz_ref[:, pl.ds(0, C)] = (da * sgg).astype(BF16)
        dz_ref[:, pl.ds(C, C)] = (da * val * sgg * (1.0 - sgg)).astype(BF16)

        @pl.when(i == n - 1)
        def _():
            for k in range(CONV_A_WIDTH):
                small_ref[pl.ds(k, 1), :] = jnp.sum(dw8[pl.ds(8 * k, 8), :], axis=0, keepdims=True)

    row = lambda col: pl.BlockSpec((ts, C), lambda i, col=col: (n - 1 - i, col))
    vec = pl.BlockSpec((1, C), lambda i: (0, 0))
    return pl.pallas_call(
        body, name=name, grid=(n,),
        in_specs=[row(0), row(1), row(0), row(0), pl.BlockSpec((32, C), lambda i: (0, 0)), vec, vec],
        out_specs=[pl.BlockSpec((ts, 2 * C), lambda i: (n - 1 - i, 0)), pl.BlockSpec((40, C), lambda i: (0, 0))],
        out_shape=[jax.ShapeDtypeStruct((S, 2 * C), BF16), jax.ShapeDtypeStruct((40, C), F32)],
        scratch_shapes=[pltpu.VMEM((ts + CONV_HALO, C), F32), pltpu.VMEM((ts, C), F32), pltpu.VMEM((ts, C), F32),
                        pltpu.VMEM((8 * 32, C), F32), pltpu.VMEM((7, ts + CONV_HALO - 8, C), F32)],
        compiler_params=_cparams(("arbitrary",)),
    )(z, z, c, dcat, w, ln_g, ln_b)


SC_HALO = 8


def _sconv_fwd(z, w, *, name):
    S = z.shape[0]
    C = SC_CH
    ts = _div_tile(S, 256, 16)

    first = SC_HALO - (SC_WIDTH - 1)
    residues = tuple(sorted({(first + k) % 8 for k in range(SC_WIDTH)} - {0}))

    def body(gb_ref, gc_ref, v_ref, w_ref, y_ref, cc_ref, win, shifted):
        i = pl.program_id(0)

        @pl.when(i == 0)
        def _():
            win[pl.ds(0, SC_HALO), :] = jnp.zeros((SC_HALO, C), F32)

        @pl.when(i > 0)
        def _():
            win[pl.ds(0, SC_HALO), :] = win[pl.ds(ts, SC_HALO), :]

        win[pl.ds(SC_HALO, ts), :] = gc_ref[...].astype(F32) * v_ref[...].astype(F32)
        tap = _shifted_taps(win, shifted, ts, residues)
        rs = min(CONV_SUB_ROWS, ts)
        for cb in range(C // 128):
            lanes = pl.ds(128 * cb, 128)
            for rt in range(ts // rs):
                rows = pl.ds(rs * rt, rs)
                sub = jnp.zeros((rs, 128), F32)
                for k in range(SC_WIDTH):
                    sub = sub + w_ref[pl.ds(k, 1), lanes] * tap(first + k + rs * rt, rs, lanes)
                cc_ref[rows, lanes] = sub.astype(BF16)
                y_ref[rows, lanes] = (gb_ref[rows, lanes].astype(F32) * sub).astype(BF16)

    row = lambda col: pl.BlockSpec((ts, C), lambda i, col=col: (i, col))
    return pl.pallas_call(
        body, name=name, grid=(S // ts,),
        in_specs=[row(0), row(1), row(2), pl.BlockSpec((8, C), lambda i: (0, 0))],
        out_specs=[row(0), row(0)],
        out_shape=[jax.ShapeDtypeStruct((S, C), BF16), jax.ShapeDtypeStruct((S, C), BF16)],
        scratch_shapes=[pltpu.VMEM((ts + SC_HALO, C), F32), pltpu.VMEM((len(residues), ts, C), F32)],
        compiler_params=_cparams(("arbitrary",)),
    )(z, z, z, w)


def _sconv_bwd(z, cc, dy, w, *, name):
    S = z.shape[0]
    C = SC_CH
    ts = _div_tile(S, 256, 16)
    n = S // ts

    residues = tuple(range(1, SC_WIDTH))

    def body(gb_ref, gc_ref, v_ref, cc_ref, dy_ref, w_ref, dz_ref, dw_ref, win, shifted, dw8):
        i = pl.program_id(0)

        @pl.when(i == 0)
        def _():
            win[pl.ds(ts, SC_HALO), :] = jnp.zeros((SC_HALO, C), F32)
            dw_ref[...] = jnp.zeros(dw_ref.shape, F32)
            dw8[...] = jnp.zeros(dw8.shape, F32)

        @pl.when(i > 0)
        def _():
            win[pl.ds(ts, SC_HALO), :] = win[pl.ds(0, SC_HALO), :]

        dyv = dy_ref[...].astype(F32)
        dz_ref[:, pl.ds(0, C)] = (dyv * cc_ref[...].astype(F32)).astype(BF16)
        win[pl.ds(0, ts), :] = dyv * gb_ref[...].astype(F32)
        tap = _shifted_taps(win, shifted, ts, residues)
        rs = min(CONV_SUB_ROWS, ts)
        for cb in range(C // 128):
            lanes = pl.ds(128 * cb, 128)
            for rt in range(ts // rs):
                rows = pl.ds(rs * rt, rs)
                gc = gc_ref[rows, lanes].astype(F32)
                val = v_ref[rows, lanes].astype(F32)
                cv = gc * val
                dcv = jnp.zeros((rs, 128), F32)
                for k in range(SC_WIDTH):
                    sh = tap(SC_WIDTH - 1 - k + rs * rt, rs, lanes)
                    dcv = dcv + w_ref[pl.ds(k, 1), lanes] * sh
                    prod = cv * sh
                    part = prod[0:8]
                    for r in range(1, rs // 8):
                        part = part + prod[8 * r:8 * r + 8]
                    dw8[pl.ds(8 * k, 8), lanes] += part
                dz_ref[rows, pl.ds(C + 128 * cb, 128)] = (dcv * val).astype(BF16)
                dz_ref[rows, pl.ds(2 * C + 128 * cb, 128)] = (dcv * gc).astype(BF16)

        @pl.when(i == n - 1)
        def _():
            for k in range(SC_WIDTH):
                dw_ref[pl.ds(k, 1), :] = jnp.sum(dw8[pl.ds(8 * k, 8), :], axis=0, keepdims=True)

    row = lambda col: pl.BlockSpec((ts, C), lambda i, col=col: (n - 1 - i, col))
    return pl.pallas_call(
        body, name=name, grid=(n,),
        in_specs=[row(0), row(1), row(2), row(0), row(0), pl.BlockSpec((8, C), lambda i: (0, 0))],
        out_specs=[pl.BlockSpec((ts, 3 * C), lambda i: (n - 1 - i, 0)), pl.BlockSpec((8, C), lambda i: (0, 0))],
        out_shape=[jax.ShapeDtypeStruct((S, 3 * C), BF16), jax.ShapeDtypeStruct((8, C), F32)],
        scratch_shapes=[pltpu.VMEM((ts + SC_HALO, C), F32), pltpu.VMEM((len(residues), ts, C), F32),
                        pltpu.VMEM((8 * SC_WIDTH, C), F32)],
        compiler_params=_cparams(("arbitrary",)),
    )(z, z, z, cc, dy, w)


SWA_Q_COL = 2
SWA_SLOPES = [2.0 ** (-8.0 * (h + 1) / SWA_HEADS) for h in range(SWA_HEADS)]
SWA_SCALE = HEAD_DIM ** -0.5


SWA_GROUP_ROWS = SWA_GROUP * WINDOW


def _swa_masks():
    shape = (SWA_GROUP_ROWS, 2 * WINDOW)
    ii = lax.broadcasted_iota(jnp.int32, shape, 0)
    jj = lax.broadcasted_iota(jnp.int32, shape, 1)
    dist = (ii & (WINDOW - 1)) + WINDOW - jj
    valid = (dist >= 0) & (dist < WINDOW)
    grp = lax.broadcasted_iota(jnp.int32, (SWA_GROUP_ROWS, 1), 0) // WINDOW
    return dist.astype(F32), valid, jj, grp


def _by_group(grp, vals):
    out = jnp.full(grp.shape, vals[SWA_GROUP - 1], F32)
    for g in range(SWA_GROUP - 2, -1, -1):
        out = jnp.where(grp == g, vals[g], out)
    return out


def _stack_heads(ref, rows, kv):
    return jnp.concatenate([ref[rows, pl.ds(HEAD_DIM * (kv * SWA_GROUP + g), HEAD_DIM)] for g in range(SWA_GROUP)],
                           axis=0)


def _swa_probs(qg, kk, sink, slope, distf, valid):
    s = lax.dot_general(qg, kk, (((1,), (1,)), ((), ())), preferred_element_type=F32) * SWA_SCALE
    s = s - slope * distf
    s = jnp.where(valid, s, NEG_BIG)
    m = jnp.maximum(jnp.max(s, axis=-1, keepdims=True), sink)
    p = jnp.exp(s - m)
    l = jnp.sum(p, axis=-1, keepdims=True) + jnp.exp(sink - m)
    return p, m, l


def _swa_fwd(z, kpad, vpad, sinks, cat, *, name):
    S = z.shape[0]
    tq = _div_tile(S, 256, 128)
    nblk = tq // WINDOW
    W = WINDOW

    def body(sink_ref, q_ref, k_ref, v_ref, cat_ref, o_ref):
        i = pl.program_id(0)
        distf, valid0, jj, grp = _swa_masks()
        for kv in range(SWA_KV_HEADS):
            heads = range(kv * SWA_GROUP, (kv + 1) * SWA_GROUP)
            sink = _by_group(grp, [sink_ref[h] for h in heads])
            slope = _by_group(grp, [SWA_SLOPES[h] for h in heads])
            for b in range(nblk):
                nb = i * nblk + b
                start = pl.multiple_of(nb * W, W)
                rows = pl.ds(W * b, W)
                valid = valid0 & ((jj >= W) | (nb > 0))
                kk = k_ref[pl.ds(start, 2 * W), pl.ds(HEAD_DIM * kv, HEAD_DIM)]
                vv = v_ref[pl.ds(start, 2 * W), pl.ds(HEAD_DIM * kv, HEAD_DIM)]
                p, m, l = _swa_probs(_stack_heads(q_ref, rows, kv), kk, sink, slope, distf, valid)
                o = (jnp.dot(p.astype(BF16), vv, preferred_element_type=F32) / l).astype(BF16)
                for g, h in enumerate(heads):
                    o_ref[rows, pl.ds(HEAD_DIM * h, HEAD_DIM)] = o[W * g:W * (g + 1)]

    full = pl.BlockSpec((S + W, 2 * HEAD_DIM), lambda i: (0, 0))
    return pl.pallas_call(
        body, name=name, grid=(S // tq,),
        in_specs=[pl.BlockSpec(memory_space=pltpu.SMEM), pl.BlockSpec((tq, 512), lambda i: (i, SWA_Q_COL)), full, full,
                  pl.BlockSpec(memory_space=pl.ANY)],
        out_specs=pl.BlockSpec((tq, 512), lambda i: (i, 1)),
        out_shape=jax.ShapeDtypeStruct((S, 1024), BF16), input_output_aliases={4: 0},
        compiler_params=_cparams(("parallel",)),
    )(sinks, z, kpad, vpad, cat)


def _swa_bwd(z, kpad, vpad, sinks, dcat, *, name):
    S = z.shape[0]
    tq = _div_tile(S, 256, 128)
    nblk = tq // WINDOW
    W = WINDOW

    def body(sink_ref, q_ref, k_ref, v_ref, do_ref, dq_ref, dk_ref, dv_ref, ds_ref):
        i = pl.program_id(0)

        @pl.when(i == 0)
        def _():
            dk_ref[...] = jnp.zeros(dk_ref.shape, F32)
            dv_ref[...] = jnp.zeros(dv_ref.shape, F32)
            ds_ref[...] = jnp.zeros(ds_ref.shape, F32)

        distf, valid0, jj, grp = _swa_masks()
        tn = (((0,), (0,)), ((), ()))
        for kv in range(SWA_KV_HEADS):
            heads = range(kv * SWA_GROUP, (kv + 1) * SWA_GROUP)
            sink = _by_group(grp, [sink_ref[h] for h in heads])
            slope = _by_group(grp, [SWA_SLOPES[h] for h in heads])
            for b in range(nblk):
                nb = i * nblk + b
                start = pl.multiple_of(nb * W, W)
                rows = pl.ds(W * b, W)
                valid = valid0 & ((jj >= W) | (nb > 0))
                kk = k_ref[pl.ds(start, 2 * W), pl.ds(HEAD_DIM * kv, HEAD_DIM)]
                vv = v_ref[pl.ds(start, 2 * W), pl.ds(HEAD_DIM * kv, HEAD_DIM)]
                qg = _stack_heads(q_ref, rows, kv)
                dog = _stack_heads(do_ref, rows, kv)
                p, m, l = _swa_probs(qg, kk, sink, slope, distf, valid)
                inv_l = 1.0 / l
                pn = p * inv_l
                dp = lax.dot_general(dog, vv, (((1,), (1,)), ((), ())), preferred_element_type=F32)
                delta = jnp.sum(pn * dp, axis=-1, keepdims=True)
                dsc = (pn * (dp - delta)).astype(BF16)
                dsink = jnp.exp(sink - m) * inv_l * delta
                dq = (jnp.dot(dsc, kk, preferred_element_type=F32) * SWA_SCALE).astype(BF16)
                for g, h in enumerate(heads):
                    ds_ref[pl.ds(h, 1), :] += jnp.broadcast_to(
                        -jnp.sum(dsink[W * g:W * (g + 1)], axis=0, keepdims=True), (1, 128))
                    dq_ref[rows, pl.ds(HEAD_DIM * h, HEAD_DIM)] = dq[W * g:W * (g + 1)]
                dk_ref[pl.ds(start, 2 * W), pl.ds(HEAD_DIM * kv, HEAD_DIM)] += lax.dot_general(
                    dsc, qg, tn, preferred_element_type=F32) * SWA_SCALE
                dv_ref[pl.ds(start, 2 * W), pl.ds(HEAD_DIM * kv, HEAD_DIM)] += lax.dot_general(
                    pn.astype(BF16), dog, tn, preferred_element_type=F32)

    full = pl.BlockSpec((S + W, 2 * HEAD_DIM), lambda i: (0, 0))
    return pl.pallas_call(
        body, name=name, grid=(S // tq,),
        in_specs=[pl.BlockSpec(memory_space=pltpu.SMEM), pl.BlockSpec((tq, 512), lambda i: (i, SWA_Q_COL)), full, full,
                  pl.BlockSpec((tq, 512), lambda i: (i, 1))],
        out_specs=[pl.BlockSpec((tq, 512), lambda i: (i, 0)), full, full, pl.BlockSpec((8, 128), lambda i: (0, 0))],
        out_shape=[jax.ShapeDtypeStruct((S, 512), BF16), jax.ShapeDtypeStruct((S + W, 2 * HEAD_DIM), F32),
                   jax.ShapeDtypeStruct((S + W, 2 * HEAD_DIM), F32), jax.ShapeDtypeStruct((8, 128), F32)],
        compiler_params=_cparams(("arbitrary",)),
    )(sinks, z, kpad, vpad, dcat)


XA_SCALE = XA_HEAD_DIM ** -0.5


def _xa_probs(qh, kh):
    s = lax.dot_general(qh, kh, (((1,), (1,)), ((), ())), preferred_element_type=F32) * XA_SCALE
    m = jnp.max(s, axis=-1, keepdims=True)
    p = jnp.exp(s - m)
    return p, jnp.sum(p, axis=-1, keepdims=True)


def _xa_fwd(q, kv, *, name):
    S, D = q.shape
    M = kv.shape[0]
    ts = _div_tile(S, 512, 16)
    HD = XA_HEAD_DIM

    def body(q_ref, k_ref, v_ref, o_ref):
        for h in range(XA_HEADS):
            qh = q_ref[:, pl.ds(HD * h, HD)]
            p, l = _xa_probs(qh, k_ref[:, pl.ds(HD * h, HD)])
            o = jnp.dot(p.astype(BF16), v_ref[:, pl.ds(HD * h, HD)], preferred_element_type=F32) / l
            o_ref[:, pl.ds(HD * h, HD)] = o.astype(BF16)

    return pl.pallas_call(
        body, name=name, grid=(S // ts,),
        in_specs=[pl.BlockSpec((ts, D), lambda i: (i, 0)), pl.BlockSpec((M, D), lambda i: (0, 0)),
                  pl.BlockSpec((M, D), lambda i: (0, 1))],
        out_specs=pl.BlockSpec((ts, D), lambda i: (i, 0)),
        out_shape=jax.ShapeDtypeStruct((S, D), BF16),
        compiler_params=_cparams(("parallel",)),
    )(q, kv, kv)


def _xa_bwd(q, kv, do, *, name):
    S, D = q.shape
    M = kv.shape[0]
    ts = _div_tile(S, 512, 16)
    HD = XA_HEAD_DIM

    def body(q_ref, k_ref, v_ref, do_ref, dq_ref, dkv_ref):
        i = pl.program_id(0)

        @pl.when(i == 0)
        def _():
            dkv_ref[...] = jnp.zeros(dkv_ref.shape, F32)

        for h in range(XA_HEADS):
            qh = q_ref[:, pl.ds(HD * h, HD)]
            kh = k_ref[:, pl.ds(HD * h, HD)]
            vh = v_ref[:, pl.ds(HD * h, HD)]
            doh = do_ref[:, pl.ds(HD * h, HD)]
            p, l = _xa_probs(qh, kh)
            pn = p * (1.0 / l)
            dp = lax.dot_general(doh, vh, (((1,), (1,)), ((), ())), preferred_element_type=F32)
            delta = jnp.sum(pn * dp, axis=-1, keepdims=True)
            dsc = (pn * (dp - delta)).astype(BF16)
            dq_ref[:, pl.ds(HD * h, HD)] = (jnp.dot(dsc, kh, preferred_element_type=F32) * XA_SCALE).astype(BF16)
            dkv_ref[:, pl.ds(HD * h, HD)] += lax.dot_general(
                dsc, qh, (((0,), (0,)), ((), ())), preferred_element_type=F32) * XA_SCALE
            dkv_ref[:, pl.ds(D + HD * h, HD)] += lax.dot_general(
                pn.astype(BF16), doh, (((0,), (0,)), ((), ())), preferred_element_type=F32)

    row = pl.BlockSpec((ts, D), lambda i: (i, 0))
    return pl.pallas_call(
        body, name=name, grid=(S // ts,),
        in_specs=[row, pl.BlockSpec((M, D), lambda i: (0, 0)), pl.BlockSpec((M, D), lambda i: (0, 1)), row],
        out_specs=[row, pl.BlockSpec((M, 2 * D), lambda i: (0, 0))],
        out_shape=[jax.ShapeDtypeStruct((S, D), BF16), jax.ShapeDtypeStruct((M, 2 * D), F32)],
        compiler_params=_cparams(("arbitrary",)),
    )(q, kv, kv, do)


def _adam_math(w, g, m, v):
    m = ADAM_B1 * m + (1.0 - ADAM_B1) * g
    v = ADAM_B2 * v + (1.0 - ADAM_B2) * (g * g)
    m_hat = m / (1.0 - ADAM_B1 ** ADAM_STEP)
    v_hat = v / (1.0 - ADAM_B2 ** ADAM_STEP)
    delta = -ADAM_LR * (m_hat / (jnp.sqrt(v_hat) + ADAM_EPS) + ADAM_WD * w)
    return delta, m, v


def _adamw_layers(w, m, v, gsrc, *, name):
    L, A, B = w.shape
    tr = _div_tile(A, max(8, ADAM_TILE_ELEMS // B // 8 * 8))
    nt = A // tr
    flat = [a for srcs in gsrc for a in srcs]
    owner = [l for l, srcs in enumerate(gsrc) for _ in srcs]
    ng = len(flat)

    def body(*refs):
        w_ref, m_ref, v_ref = refs[:3]
        g_refs = refs[3:3 + ng]
        g_ref, d_ref, nm_ref, nv_ref = refs[3 + ng:]
        layer = pl.program_id(0)
        g = None
        for l in range(L):
            gl = None
            for a_ref, o in zip(g_refs, owner):
                if o == l:
                    gl = a_ref[...] if gl is None else gl + a_ref[...]
            g = gl if g is None else jnp.where(layer == l, gl, g)
        d, nm, nv = _adam_math(w_ref[...], g, m_ref[...], v_ref[...])
        g_ref[...] = g
        d_ref[...] = d
        nm_ref[...] = nm
        nv_ref[...] = nv

    def src_spec(o):
        return pl.BlockSpec((None, tr, B),
                            lambda l, i: (0, jnp.where(l == o, i, jnp.where(l > o, nt - 1, 0)), 0))

    spec = pl.BlockSpec((None, tr, B), lambda l, i: (l, i, 0))
    sds = jax.ShapeDtypeStruct((L, A, B), F32)
    return pl.pallas_call(
        body, name=name, grid=(L, nt), in_specs=[spec] * 3 + [src_spec(o) for o in owner], out_specs=[spec] * 4,
        out_shape=[sds] * 4, compiler_params=_cparams(("arbitrary", "arbitrary")),
    )(w, m, v, *flat)


def _adamw_small(ws, ms, vs, gparts, *, name):
    n = len(ws)
    R = gparts.shape[1]

    def body(*refs):
        w_refs, m_refs, v_refs = refs[:n], refs[n:2 * n], refs[2 * n:3 * n]
        gp_ref = refs[3 * n]
        outs = refs[3 * n + 1:7 * n + 1]
        packed = refs[7 * n + 1]
        g = gp_ref[0]
        for k in range(1, N_DEV):
            g = g + gp_ref[k]
        packed[...] = g
        row = 0
        for p in range(n):
            r, c = ws[p].shape
            per, lanes = max(c // 128, 1), min(c, 128)
            g_ref = outs[p]
            for i in range(r):
                for k in range(per):
                    g_ref[pl.ds(i, 1), pl.ds(128 * k, lanes)] = packed[pl.ds(row, 1), pl.ds(0, lanes)]
                    row += 1
            d, nm, nv = _adam_math(w_refs[p][...], g_ref[...], m_refs[p][...], v_refs[p][...])
            outs[n + p][...] = d
            outs[2 * n + p][...] = nm
            outs[3 * n + p][...] = nv

    shapes = [jax.ShapeDtypeStruct(a.shape, F32) for a in ws]
    res = pl.pallas_call(body, name=name, out_shape=shapes * 4, scratch_shapes=[pltpu.VMEM((R, 128), F32)],
                         compiler_params=_cparams())(*ws, *ms, *vs, gparts)
    return res[:n], res[n:2 * n], res[2 * n:3 * n], res[3 * n:]


ANY = pl.BlockSpec(memory_space=pl.ANY)


def _mesh_pos():
    return lax.axis_index("x"), lax.axis_index("y"), lax.axis_index("c")


def _other_chips(x, y):
    return [(1 - x, y), (x, 1 - y), (1 - x, 1 - y)]


LAYOUT = {'ffn1_w_gu': 'col', 'ffn1_w_down': 'stk', 'even_w_in': 'stk', 'even_w_out': 'stk', 'odd_w_in': 'col',
          'odd_w_out': 'stk', 'xa_wq': 'stk', 'xa_wkv': 'col', 'xa_wo': 'stk', 'ffn2_w_gu': 'col',
          'ffn2_w_down': 'stk', 'tiny': 'stk'}
COMM_NAMES = list(LAYOUT)
TINY_ROWS = 48


def _gathered_piece(ref, kind, L, A, h):
    if L == 2:
        return ref.at[h]
    rows = pl.ds(pl.multiple_of(h * (A // 2), 8), A // 2)
    return ref.at[0, rows] if kind == 'col' else ref.at[0, :, rows]


def _chip_part(piece, kind, B, s):
    if kind == 'col':
        return piece.at[:, pl.ds(pl.multiple_of(s * B, 128), B)]
    return piece.at[s]


def _place(shard, layer, kind, chip_idx, out_dtype, *, name, after=None):
    L, A, B = shard.shape
    ta = _div_tile(A, 256, 16)
    extra = [] if after is None else [after]

    def body(s_ref, x_ref, *rest):
        rest[-1][...] = x_ref[...].astype(out_dtype)

    if kind == 'col':
        shape = (1, A, N_CHIPS * B)
        out_spec = pl.BlockSpec((None, ta, B), lambda i, s: (0, i, s[0]))
    else:
        shape = (1, N_CHIPS, A, B)
        out_spec = pl.BlockSpec((None, None, ta, B), lambda i, s: (0, s[0], i, 0))
    grid_spec = pltpu.PrefetchScalarGridSpec(
        num_scalar_prefetch=1, grid=(A // ta,),
        in_specs=[pl.BlockSpec((None, ta, B), lambda i, s: (layer, i, 0))]
        + [pl.BlockSpec(memory_space=pl.ANY)] * len(extra), out_specs=out_spec)
    return pl.pallas_call(
        body, name=name, grid_spec=grid_spec, out_shape=jax.ShapeDtypeStruct(shape, out_dtype),
        compiler_params=_cparams(("parallel",)),
    )(chip_idx, shard, *extra)


HBM = pl.BlockSpec(memory_space=pltpu.HBM)
SEM = pl.BlockSpec(memory_space=pltpu.SEMAPHORE)
DATAFLOW = pltpu.SideEffectType.DATAFLOW_SIDE_EFFECTING


def _own_part_copies(refs, meta, send_sems, recv_sems):
    x, y, c = _mesh_pos()
    cps = []
    for k, (kind, L, A, B) in enumerate(meta):
        for j, (cx, cy) in enumerate(_other_chips(x, y)):
            part = _chip_part(refs[k].at[0], kind, B, 2 * x + y)
            cps.append(pltpu.make_async_remote_copy(
                src_ref=part, dst_ref=part, send_sem=send_sems.at[3 * k + j], recv_sem=recv_sems.at[3 * k + j],
                device_id=(cx, cy, c), device_id_type=MESH))
    return cps


def _half_part_copies(refs, meta, send_sems, recv_sems):
    x, y, c = _mesh_pos()
    cps = []
    for k, (kind, L, A, B) in enumerate(meta):
        for j, (cx, cy) in enumerate(_other_chips(x, y)):
            part = _chip_part(_gathered_piece(refs[k], kind, 1, A, c), kind, B, 2 * x + y)
            cps.append(pltpu.make_async_remote_copy(
                src_ref=part, dst_ref=part, send_sem=send_sems.at[3 * k + j], recv_sem=recv_sems.at[3 * k + j],
                device_id=(cx, cy, c), device_id_type=MESH))
    return cps


def _forward_copies(refs, meta, send_sems, recv_sems):
    x, y, c = _mesh_pos()
    cps = []
    for k, (kind, L, A, B) in enumerate(meta):
        for j, (cx, cy) in enumerate(_other_chips(x, y)):
            part = _chip_part(_gathered_piece(refs[k], kind, 1, A, c), kind, B, 2 * cx + cy)
            cps.append(pltpu.make_async_remote_copy(
                src_ref=part, dst_ref=part, send_sem=send_sems.at[3 * k + j], recv_sem=recv_sems.at[3 * k + j],
                device_id=(x, y, 1 - c), device_id_type=MESH))
    return cps


def _gather_start(fulls, meta, after, tag, copies=_own_part_copies):
    n = len(fulls)

    def body(*refs):
        send_sems, recv_sems = refs[n + 1], refs[n + 2]
        outs = refs[n + 3:2 * n + 3]
        token = refs[2 * n + 3]
        for cp in copies(outs, meta, send_sems, recv_sems):
            cp.start()
        token[...] = jnp.zeros_like(token)

    res = pl.pallas_call(
        body, name=f"ag_start_{tag}", in_specs=[HBM] * n + [pl.BlockSpec(memory_space=pl.ANY)],
        out_specs=(SEM, SEM) + (HBM,) * n + (pl.BlockSpec(memory_space=pltpu.VMEM),),
        out_shape=(pltpu.SemaphoreType.DMA((3 * n,)), pltpu.SemaphoreType.DMA((3 * n,)))
        + tuple(pltpu.HBM(f.shape, f.dtype) for f in fulls) + (jax.ShapeDtypeStruct((8, 128), F32),),
        input_output_aliases={k: 2 + k for k in range(n)},
        compiler_params=pltpu.CompilerParams(has_side_effects=DATAFLOW),
    )(*[pltpu.with_memory_space_constraint(f, pltpu.HBM) for f in fulls], after)
    return res[0], res[1], list(res[2:2 + n]), res[2 + n]


def _gather_wait(send_sems, recv_sems, fulls, meta, after, tag, copies=_own_part_copies):
    n = len(fulls)

    def body(*refs):
        f_refs = refs[:n]
        send_sems, recv_sems = refs[n], refs[n + 1]
        for cp in copies(f_refs, meta, send_sems, recv_sems):
            cp.wait_send()
            cp.wait_recv()

    return pl.pallas_call(
        body, name=f"ag_wait_{tag}", in_specs=[HBM] * n + [SEM, SEM, pl.BlockSpec(memory_space=pl.ANY)],
        out_specs=[HBM] * n, out_shape=[pltpu.HBM(f.shape, f.dtype) for f in fulls],
        input_output_aliases={k: k for k in range(n)},
        compiler_params=pltpu.CompilerParams(has_side_effects=DATAFLOW),
    )(*fulls, send_sems, recv_sems, after)


def _scatter_copies(g_refs, land_refs, meta, send_sems, recv_sems):
    x, y, c = _mesh_pos()
    cps = []
    for k, (kind, L, A, B) in enumerate(meta):
        for j, (cx, cy) in enumerate(_other_chips(x, y)):
            cps.append(pltpu.make_async_remote_copy(
                src_ref=_chip_part(g_refs[k].at[0], kind, B, 2 * cx + cy), dst_ref=land_refs[k].at[j],
                send_sem=send_sems.at[3 * k + j], recv_sem=recv_sems.at[3 * k + j], device_id=(cx, cy, c),
                device_id_type=MESH))
    return cps


def _scatter_start(gs, meta, after, tag):
    n = len(gs)

    def body(*refs):
        send_sems, recv_sems = refs[2 * n + 1], refs[2 * n + 2]
        g_out = refs[2 * n + 3:3 * n + 3]
        lands = refs[3 * n + 3:4 * n + 3]
        token = refs[4 * n + 3]
        for cp in _scatter_copies(g_out, lands, meta, send_sems, recv_sems):
            cp.start()
        token[...] = jnp.zeros_like(token)

    land_shapes = [(3, A, B) for kind, L, A, B in meta]
    lands = [pltpu.with_memory_space_constraint(lax.empty(s, g.dtype), pltpu.HBM) for s, g in zip(land_shapes, gs)]
    res = pl.pallas_call(
        body, name=f"rs_start_{tag}", in_specs=[HBM] * (2 * n) + [pl.BlockSpec(memory_space=pl.ANY)],
        out_specs=(SEM, SEM) + (HBM,) * (2 * n) + (pl.BlockSpec(memory_space=pltpu.VMEM),),
        out_shape=(pltpu.SemaphoreType.DMA((3 * n,)), pltpu.SemaphoreType.DMA((3 * n,)))
        + tuple(pltpu.HBM(g.shape, g.dtype) for g in gs)
        + tuple(pltpu.HBM(s, g.dtype) for s, g in zip(land_shapes, gs)) + (jax.ShapeDtypeStruct((8, 128), F32),),
        input_output_aliases={k: 2 + k for k in range(2 * n)},
        compiler_params=pltpu.CompilerParams(has_side_effects=DATAFLOW),
    )(*[pltpu.with_memory_space_constraint(g, pltpu.HBM) for g in gs], *lands, after)
    return res[0], res[1], list(res[2:2 + n]), list(res[2 + n:2 + 2 * n]), res[2 + 2 * n]


def _scatter_wait(send_sems, recv_sems, gs, lands, meta, after, tag):
    n = len(gs)

    def body(*refs):
        g_refs, land_refs = refs[:n], refs[n:2 * n]
        send_sems, recv_sems = refs[2 * n], refs[2 * n + 1]
        for cp in _scatter_copies(g_refs, land_refs, meta, send_sems, recv_sems):
            cp.wait_send()
            cp.wait_recv()

    both = list(gs) + list(lands)
    res = pl.pallas_call(
        body, name=f"rs_wait_{tag}", in_specs=[HBM] * (2 * n) + [SEM, SEM, pl.BlockSpec(memory_space=pl.ANY)],
        out_specs=[HBM] * (2 * n), out_shape=[pltpu.HBM(a.shape, a.dtype) for a in both],
        input_output_aliases={k: k for k in range(2 * n)},
        compiler_params=pltpu.CompilerParams(has_side_effects=DATAFLOW),
    )(*both, send_sems, recv_sems, after)
    return list(res[:n]), list(res[n:])


def _chip_sum_full(g, got, m, chip_idx, *, name):
    kind, L, A, B = m
    ta = _div_tile(A, 256, 16)

    def body(r_ref, a_ref, b_ref, o_ref):
        acc = a_ref[...].astype(F32)
        for j in range(3):
            acc = acc + b_ref[j].astype(F32)
        o_ref[...] = acc

    if kind == 'col':
        g_spec = pl.BlockSpec((None, ta, B), lambda i, r: (0, i, r[0]))
    else:
        g_spec = pl.BlockSpec((None, None, ta, B), lambda i, r: (0, r[0], i, 0))
    grid_spec = pltpu.PrefetchScalarGridSpec(
        num_scalar_prefetch=1, grid=(A // ta,),
        in_specs=[g_spec, pl.BlockSpec((3, ta, B), lambda i, r: (0, i, 0))],
        out_specs=pl.BlockSpec((None, ta, B), lambda i, r: (0, i, 0)))
    return pl.pallas_call(
        body, name=name, grid_spec=grid_spec, out_shape=jax.ShapeDtypeStruct((1, A, B), F32),
        compiler_params=_cparams(("parallel",)),
    )(chip_idx, g, got)


def _swap_copies(src_refs, land_refs, send_sems, recv_sems):
    x, y, c = _mesh_pos()
    return [pltpu.make_async_remote_copy(src_ref=s, dst_ref=d, send_sem=send_sems.at[k], recv_sem=recv_sems.at[k],
                                         device_id=(x, y, 1 - c), device_id_type=MESH)
            for k, (s, d) in enumerate(zip(src_refs, land_refs))]


def _swap_start(sums, after, tag):
    n = len(sums)

    def body(*refs):
        send_sems, recv_sems = refs[2 * n + 1], refs[2 * n + 2]
        s_out = refs[2 * n + 3:3 * n + 3]
        lands = refs[3 * n + 3:4 * n + 3]
        token = refs[4 * n + 3]
        for cp in _swap_copies(s_out, lands, send_sems, recv_sems):
            cp.start()
        token[...] = jnp.zeros_like(token)

    lands = [pltpu.with_memory_space_constraint(lax.empty(s.shape, s.dtype), pltpu.HBM) for s in sums]
    res = pl.pallas_call(
        body, name=f"rs_swap_start_{tag}", in_specs=[HBM] * (2 * n) + [pl.BlockSpec(memory_space=pl.ANY)],
        out_specs=(SEM, SEM) + (HBM,) * (2 * n) + (pl.BlockSpec(memory_space=pltpu.VMEM),),
        out_shape=(pltpu.SemaphoreType.DMA((n,)), pltpu.SemaphoreType.DMA((n,)))
        + tuple(pltpu.HBM(s.shape, s.dtype) for s in sums) * 2 + (jax.ShapeDtypeStruct((8, 128), F32),),
        input_output_aliases={k: 2 + k for k in range(2 * n)},
        compiler_params=pltpu.CompilerParams(has_side_effects=DATAFLOW),
    )(*[pltpu.with_memory_space_constraint(s, pltpu.HBM) for s in sums], *lands, after)
    return res[0], res[1], list(res[2:2 + n]), list(res[2 + n:2 + 2 * n]), res[2 + 2 * n]


def _swap_wait(send_sems, recv_sems, sums, lands, after, tag):
    n = len(sums)

    def body(*refs):
        send_sems, recv_sems = refs[2 * n], refs[2 * n + 1]
        for cp in _swap_copies(refs[:n], refs[n:2 * n], send_sems, recv_sems):
            cp.wait_send()
            cp.wait_recv()

    both = list(sums) + list(lands)
    res = pl.pallas_call(
        body, name=f"rs_swap_wait_{tag}", in_specs=[HBM] * (2 * n) + [SEM, SEM, pl.BlockSpec(memory_space=pl.ANY)],
        out_specs=[HBM] * (2 * n), out_shape=[pltpu.HBM(a.shape, a.dtype) for a in both],
        input_output_aliases={k: k for k in range(2 * n)},
        compiler_params=pltpu.CompilerParams(has_side_effects=DATAFLOW),
    )(*both, send_sems, recv_sems, after)
    return list(res[:n]), list(res[n:])


def _pair_swap(sums, small, *, tag):
    ns = len(sums)
    with_small = small is not None
    n_in = ns + with_small

    def body(*refs):
        sum_refs = refs[:ns]
        got_refs = refs[n_in:n_in + ns]
        send_sems, recv_sems = refs[2 * n_in], refs[2 * n_in + 1]
        x, y, c = _mesh_pos()
        cps = []
        for k in range(ns):
            cp = pltpu.make_async_remote_copy(
                src_ref=sum_refs[k], dst_ref=got_refs[k], send_sem=send_sems.at[k], recv_sem=recv_sems.at[k],
                device_id=(x, y, 1 - c), device_id_type=MESH)
            cp.start()
            cps.append(cp)
        if with_small:
            small_ref, sm_ref, local_sem = refs[ns], refs[n_in + ns], refs[2 * n_in + 2]
            me = 4 * x + 2 * y + c
            own = pltpu.make_async_copy(small_ref, sm_ref.at[me], local_sem)
            own.start()
            for r in range(1, N_DEV):
                fx, fy, fc = (r >> 2) & 1, (r >> 1) & 1, r & 1
                peer = (1 - x if fx else x, 1 - y if fy else y, 1 - c if fc else c)
                cp = pltpu.make_async_remote_copy(
                    src_ref=small_ref, dst_ref=sm_ref.at[me], send_sem=send_sems.at[ns + r],
                    recv_sem=recv_sems.at[ns + r], device_id=peer, device_id_type=MESH)
                cp.start()
                cps.append(cp)
        for cp in cps:
            cp.wait()
        if with_small:
            own.wait()

    out_shape = [jax.ShapeDtypeStruct(s.shape, s.dtype) for s in sums]
    scratch = [pltpu.SemaphoreType.DMA((ns + N_DEV,)), pltpu.SemaphoreType.DMA((ns + N_DEV,))]
    args = list(sums)
    if with_small:
        out_shape.append(jax.ShapeDtypeStruct((N_DEV,) + small.shape, F32))
        scratch.append(pltpu.SemaphoreType.DMA)
        args.append(small)
    res = pl.pallas_call(
        body, name=f"rs_pair_swap_{tag}", in_specs=[ANY] * n_in, out_specs=[ANY] * n_in, out_shape=out_shape,
        scratch_shapes=scratch,
    )(*args)
    return list(res[:ns]), (res[ns] if with_small else None)


def _tiny_pack(conv_a_w, sc_conv_w):
    lead = conv_a_w.shape[:-2]
    sc = sc_conv_w.reshape(lead + (2 * SC_WIDTH, 128))
    z = lambda r: jnp.zeros(lead + (r, 128), F32)
    return jnp.concatenate([conv_a_w, z(32 - CONV_A_WIDTH), sc, z(TINY_ROWS - 32 - 2 * SC_WIDTH)], axis=-2)


def _tiny_unpack(t):
    lead = t.shape[:-2]
    return t[..., :CONV_A_WIDTH, :], t[..., 32:32 + 2 * SC_WIDTH, :].reshape(lead + (SC_WIDTH, 256))


def _pack_small(d):
    parts = []
    for n in SMALL_NAMES:
        flat = d[n].astype(F32).reshape(-1)
        parts.append(jnp.pad(flat, (0, -flat.shape[0] % 128)))
    flat = jnp.concatenate(parts)
    return jnp.pad(flat, (0, -flat.shape[0] % 1024)).reshape(-1, 128)


def _ffn_fwd(h, g, W, n_gu, n_down, i, tag):
    h2, u, gate, up, a = _ffn_fwd_fused(h, g, W[n_gu][i], W[n_down][i], 0, name=f"{tag}_fwd")
    return h2, (h, u, gate, up, a)


def _ffn_bwd(dh, saved, g, W, n_gu, n_down, i, G, tag):
    h, u, gate, up, a = saved
    dh_in, dg, dgate, dup = _ffn_bwd_fused(dh, h, g, gate, up, W[n_gu][i], W[n_down][i], 0, name=f"{tag}_bwd")
    G[(n_down, i)] = _mm(a, dh, name=f"{tag}_b_wdown", ta=True, tm=1408, tn=1024, tk=1024, scale=0.5)
    tn = FFN_CHUNK
    half = _mm(u, dgate, name=f"{tag}_b_wg", ta=True, tm=1024, tn=tn, tk=2048, stack=(1, 0, None, 2 * D_FF))
    G[(n_gu, i)] = _mm(u, dup, name=f"{tag}_b_wu", ta=True, tm=1024, tn=tn, tk=2048, stack=(1, 0, half, 2 * D_FF),
                       n_map=lambda j: j + D_FF // tn)
    return dh_in, dg


def _xa_block_fwd(h, mem, g, gm, W, i, tag):
    mn = _rms_fwd(mem, gm, name=f"{tag}_mem_norm")
    u, q = _norm_mm(h, g, W['xa_wq'][i], 0, name=f"{tag}_q")
    kv = _mm(mn, W['xa_wkv'][i], b_layer=0, name=f"{tag}_kv", tm=256, tn=1024, tk=1024)
    o = _xa_fwd(q, kv, name=f"{tag}_attn")
    h2 = _mm(o, W['xa_wo'][i], b_layer=0, name=f"{tag}_o", out_dtype=F32, tm=1024, tn=1024, tk=1024, res=h)
    return h2, (h, u, mn, q, kv, o)


def _xa_block_bwd(dh, saved, mem, g, gm, W, i, G, tag):
    h, u, mn, q, kv, o = saved
    do = _mm(dh, W['xa_wo'][i], b_layer=0, name=f"{tag}_b_do", tb=True, tm=1024, tn=1024, tk=1024)
    G[('xa_wo', i)] = _mm(o, dh, name=f"{tag}_b_wo", ta=True, tm=1024, tn=1024, tk=1024)
    dq, dkv = _xa_bwd(q, kv, do, name=f"{tag}_b_attn")
    G[('xa_wq', i)] = _mm(u, dq, name=f"{tag}_b_wq", ta=True, tm=1024, tn=1024, tk=1024)
    dh_in, dg = _mm_norm_bwd(dq, W['xa_wq'][i], 0, h, g, dh, name=f"{tag}_b_du", tk=1024)
    G[('xa_wkv', i)] = _mm(mn, dkv, name=f"{tag}_b_wkv", ta=True, tm=1024, tn=1024, tk=256)
    dmn = _mm(dkv, W['xa_wkv'][i], b_layer=0, name=f"{tag}_b_dmn", tb=True, out_dtype=F32, tm=256, tn=1024, tk=1024)
    _, dgm = _rms_bwd(mem, gm, dmn, None, name=f"{tag}_b_mem_norm")
    return dh_in, dg, dgm


def _pad_conv_w(w, rows):
    return jnp.pad(w.astype(F32), ((0, rows - w.shape[0]), (0, 0)))


def _even_fwd(h, g, W, conv_w, conv_b, ln_g, ln_b, sinks, tag):
    u, z = _norm_mm(h, g, W['even_w_in'][0], 0, name=f"{tag}_in")
    c, cat = _conv_a_fwd(z, conv_w, conv_b, ln_g, ln_b, name=f"{tag}_conv")
    kpad = jnp.pad(z[:, 1536:1664], ((WINDOW, 0), (0, 0)))
    vpad = jnp.pad(z[:, 1664:1792], ((WINDOW, 0), (0, 0)))
    cat = _swa_fwd(z, kpad, vpad, sinks, cat, name=f"{tag}_swa")
    h2 = _mm(cat, W['even_w_out'][0], b_layer=0, name=f"{tag}_out", out_dtype=F32, tm=1024, tn=1024, tk=1024, res=h)
    return h2, (h, u, z, c, kpad, vpad, cat)


def _even_bwd(dh, saved, g, W, conv_w, ln_g, ln_b, sinks, G, tag):
    h, u, z, c, kpad, vpad, cat = saved
    dcat = _mm(dh, W['even_w_out'][0], b_layer=0, name=f"{tag}_b_dcat", tb=True, tm=1024, tn=1024, tk=1024)
    G[('even_w_out', 0)] = _mm(cat, dh, name=f"{tag}_b_wout", ta=True, tm=1024, tn=1024, tk=1024)
    dz_a, small = _conv_a_bwd(z, c, dcat, conv_w, ln_g, ln_b, name=f"{tag}_b_conv")
    dq, dkp, dvp, dsinks = _swa_bwd(z, kpad, vpad, sinks, dcat, name=f"{tag}_b_swa")
    dz = jnp.concatenate([dz_a, dq, dkp[WINDOW:].astype(BF16), dvp[WINDOW:].astype(BF16)], axis=-1)
    G[('even_w_in', 0)] = _mm(u, dz, name=f"{tag}_b_win", ta=True, tm=1024, tn=1792, tk=1024)
    dh_in, dg = _mm_norm_bwd(dz, W['even_w_in'][0], 0, h, g, dh, name=f"{tag}_b_du", tk=1792)
    grads = dict(mix=dg, conv_a_w=small[:CONV_A_WIDTH], conv_a_b=small[32:33], conv_a_ln_g=small[33:34],
                 conv_a_ln_b=small[34:35], swa_sinks=dsinks[:, 0])
    return dh_in, grads


def _odd_fwd(h, g, W, conv_w, tag):
    u, z = _norm_mm(h, g, W['odd_w_in'][0], 0, name=f"{tag}_in")
    y, cc = _sconv_fwd(z, conv_w, name=f"{tag}_conv")
    h2 = _mm(y, W['odd_w_out'][0], b_layer=0, name=f"{tag}_out", out_dtype=F32, tm=1024, tn=1024, tk=1024, res=h)
    return h2, (h, u, z, y, cc)


def _odd_bwd(dh, saved, g, W, conv_w, G, tag):
    h, u, z, y, cc = saved
    dy = _mm(dh, W['odd_w_out'][0], b_layer=0, name=f"{tag}_b_dy", tb=True, tm=1024, tn=1024, tk=1024)
    G[('odd_w_out', 0)] = _mm(y, dh, name=f"{tag}_b_wout", ta=True, tm=1024, tn=1024, tk=1024)
    dz, dw = _sconv_bwd(z, cc, dy, conv_w, name=f"{tag}_b_conv")
    G[('odd_w_in', 0)] = _mm(u, dz, name=f"{tag}_b_win", ta=True, tm=1024, tn=1024, tk=1024)
    dh_in, dg = _mm_norm_bwd(dz, W['odd_w_in'][0], 0, h, g, dh, name=f"{tag}_b_du", tk=1024)
    return dh_in, dict(mix=dg, sc_conv_w=dw[:SC_WIDTH])


def _local_step(x, mem, tgt, W, need, token, ready, conv_a_w, sc_conv_w, P):
    row = lambda v: v.reshape(1, -1)
    conv_a_w = _pad_conv_w(conv_a_w, 32)
    sc_w = _pad_conv_w(sc_conv_w, 8)
    sinks = P['swa_sinks'][0]

    def arrive(stage, h):
        for n, ws in need(stage, h).items():
            W[n] = W.get(n, []) + ws

    h = x
    saved = []
    for i in range(2):
        t = f"l{i}"
        if i == 1:
            arrive('l1_ffn1', h)
        g1 = row(P['ffn1_norm'][i]) + (token if i == 0 else 0.0)
        h, s1 = _ffn_fwd(h, g1, W, 'ffn1_w_gu', 'ffn1_w_down', i, f"{t}_ffn1")
        arrive(f"{t}_mix", h)
        if i == 0:
            h, s2 = _even_fwd(h, row(P['mix_norm'][i]), W, conv_a_w, P['conv_a_b'], P['conv_a_ln_g'],
                              P['conv_a_ln_b'], sinks, f"{t}_even")
        else:
            h, s2 = _odd_fwd(h, row(P['mix_norm'][i]), W, sc_w, f"{t}_odd")
        h, s3 = _xa_block_fwd(h, mem, row(P['xa_norm'][i]), row(P['xa_mem_norm'][i]), W, i, f"{t}_xa")
        arrive(f"{t}_ffn2", h)
        h, s4 = _ffn_fwd(h, row(P['ffn2_norm'][i]), W, 'ffn2_w_gu', 'ffn2_w_down', i, f"{t}_ffn2")
        saved.append((s1, s2, s3, s4))

    loss, dh, d_final = _final_loss(h, row(P['final_norm']), tgt, name="final_loss")

    G = {}
    gp = {n: [None, None] for n in ('ffn1_norm', 'mix_norm', 'xa_norm', 'xa_mem_norm', 'ffn2_norm')}
    single = {}
    for i in (1, 0):
        t = f"l{i}"
        s1, s2, s3, s4 = saved[i]
        g4 = row(P['ffn2_norm'][i]) + (ready('l1', G) if i == 0 else 0.0)
        dh, gp['ffn2_norm'][i] = _ffn_bwd(dh, s4, g4, W, 'ffn2_w_gu', 'ffn2_w_down', i, G, f"{t}_ffn2")
        dh, gp['xa_norm'][i], gp['xa_mem_norm'][i] = _xa_block_bwd(
            dh, s3, mem, row(P['xa_norm'][i]), row(P['xa_mem_norm'][i]), W, i, G, f"{t}_xa")
        if i == 0:
            dh, g2 = _even_bwd(dh, s2, row(P['mix_norm'][i]), W, conv_a_w, P['conv_a_ln_g'], P['conv_a_ln_b'], sinks,
                               G, f"{t}_even")
        else:
            dh, g2 = _odd_bwd(dh, s2, row(P['mix_norm'][i]), W, sc_w, G, f"{t}_odd")
        gp['mix_norm'][i] = g2.pop('mix')
        single.update(g2)
        g1 = row(P['ffn1_norm'][i]) + (ready('l0_rest', G) if i == 0 else 0.0)
        dh, gp['ffn1_norm'][i] = _ffn_bwd(dh, s1, g1, W, 'ffn1_w_gu', 'ffn1_w_down', i, G, f"{t}_ffn1")

    small = {n: jnp.concatenate(v, axis=0) for n, v in gp.items()}
    small['conv_a_b'] = single['conv_a_b']
    small['conv_a_ln_g'] = single['conv_a_ln_g']
    small['conv_a_ln_b'] = single['conv_a_ln_b']
    small['swa_sinks'] = single['swa_sinks'][None]
    small['final_norm'] = d_final[0]
    small['conv_a_w'] = single['conv_a_w']
    small['sc_conv_w'] = single['sc_conv_w']
    return loss[0, 0], dh, G, small


def kernel(x, mem, ffn1_norm, ffn1_w_gu, ffn1_w_down, mix_norm, even_w_in, conv_a_w, conv_a_b, conv_a_ln_g, conv_a_ln_b, swa_sinks, even_w_out, odd_w_in, sc_conv_w, odd_w_out, xa_norm, xa_mem_norm, xa_wq, xa_wkv, xa_wo, ffn2_norm, ffn2_w_gu, ffn2_w_down, final_norm, loss_target, m_ffn1_norm, m_ffn1_w_gu, m_ffn1_w_down, m_mix_norm, m_even_w_in, m_conv_a_w, m_conv_a_b, m_conv_a_ln_g, m_conv_a_ln_b, m_swa_sinks, m_even_w_out, m_odd_w_in, m_sc_conv_w, m_odd_w_out, m_xa_norm, m_xa_mem_norm, m_xa_wq, m_xa_wkv, m_xa_wo, m_ffn2_norm, m_ffn2_w_gu, m_ffn2_w_down, m_final_norm, v_ffn1_norm, v_ffn1_w_gu, v_ffn1_w_down, v_mix_norm, v_even_w_in, v_conv_a_w, v_conv_a_b, v_conv_a_ln_g, v_conv_a_ln_b, v_swa_sinks, v_even_w_out, v_odd_w_in, v_sc_conv_w, v_odd_w_out, v_xa_norm, v_xa_mem_norm, v_xa_wq, v_xa_wkv, v_xa_wo, v_ffn2_norm, v_ffn2_w_gu, v_ffn2_w_down, v_final_norm):
    w = dict(zip(WEIGHT_NAMES, (ffn1_norm, ffn1_w_gu, ffn1_w_down, mix_norm, even_w_in, conv_a_w, conv_a_b, conv_a_ln_g, conv_a_ln_b, swa_sinks, even_w_out, odd_w_in, sc_conv_w, odd_w_out, xa_norm, xa_mem_norm, xa_wq, xa_wkv, xa_wo, ffn2_norm, ffn2_w_gu, ffn2_w_down, final_norm)))
    m = dict(zip(WEIGHT_NAMES, (m_ffn1_norm, m_ffn1_w_gu, m_ffn1_w_down, m_mix_norm, m_even_w_in, m_conv_a_w, m_conv_a_b, m_conv_a_ln_g, m_conv_a_ln_b, m_swa_sinks, m_even_w_out, m_odd_w_in, m_sc_conv_w, m_odd_w_out, m_xa_norm, m_xa_mem_norm, m_xa_wq, m_xa_wkv, m_xa_wo, m_ffn2_norm, m_ffn2_w_gu, m_ffn2_w_down, m_final_norm)))
    v = dict(zip(WEIGHT_NAMES, (v_ffn1_norm, v_ffn1_w_gu, v_ffn1_w_down, v_mix_norm, v_even_w_in, v_conv_a_w, v_conv_a_b, v_conv_a_ln_g, v_conv_a_ln_b, v_swa_sinks, v_even_w_out, v_odd_w_in, v_sc_conv_w, v_odd_w_out, v_xa_norm, v_xa_mem_norm, v_xa_wq, v_xa_wkv, v_xa_wo, v_ffn2_norm, v_ffn2_w_gu, v_ffn2_w_down, v_final_norm)))
    cx, cy, cc = lax.axis_index("x"), lax.axis_index("y"), lax.axis_index("c")
    chip_idx = (2 * cx + cy).astype(jnp.int32).reshape(1)

    shards = {n: w[n] for n in COMM_NAMES if n != 'tiny'}
    shards['tiny'] = _tiny_pack(conv_a_w, sc_conv_w)
    first =[('ffn1_w_gu', 0), ('ffn1_w_down', 0), ('tiny', 0)]
    stages = {
        'l0_mix': [('even_w_in', 0), ('even_w_out', 0), ('xa_wq', 0), ('xa_wkv', 0), ('xa_wo', 0)],
        'l0_ffn2': [('ffn2_w_gu', 0), ('ffn2_w_down', 0)],
        'l1_ffn1': [('ffn1_w_gu', 1), ('ffn1_w_down', 1)],
        'l1_mix': [('odd_w_in', 0), ('odd_w_out', 0), ('xa_wq', 1), ('xa_wkv', 1), ('xa_wo', 1)],
        'l1_ffn2': [('ffn2_w_gu', 1), ('ffn2_w_down', 1)],
    }
    grad_stages = {'l1': stages['l1_ffn1'] + stages['l1_mix'] + stages['l1_ffn2'],
                   'l0_rest': stages['l0_mix'] + stages['l0_ffn2'], 'l0_ffn1': first}

    def place(items, after=None):
        out = []
        for n, l in items:
            out.append(_place(shards[n], l, LAYOUT[n], chip_idx, F32 if n == 'tiny' else BF16,
                              name=f"place_{n}_{l}", after=after))
            after = out[-1] if after is not None else None
        return out

    def item_meta(items):
        return [(LAYOUT[n], 1) + shards[n].shape[1:] for n, l in items]

    def natural(items, arrays):
        out = {}
        for (n, l), a in zip(items, arrays):
            if n == 'tiny':
                continue
            if n == 'even_w_in':
                out[n] = [a.transpose(0, 2, 1, 3).reshape(1, D_MODEL, -1)]
            else:
                out[n] = [a if LAYOUT[n] == 'col' else a.reshape(1, N_CHIPS * a.shape[2], a.shape[3])]
        return out

    meta_first = item_meta(first)
    send, recv, in_flight, token = _gather_start(place(first), meta_first, chip_idx, "first_ici", _half_part_copies)
    placed, last = {}, token
    for stage, items in stages.items():
        placed[stage] = place(items, last)
        last = placed[stage][-1]
    landed = _gather_wait(send, recv, in_flight, meta_first, last, "first_ici", _half_part_copies)
    send, recv, in_flight, token = _gather_start(landed, meta_first, token, "first_d2d", _forward_copies)
    first_d2d = (send, recv, in_flight)
    gathers = {}
    for stage, items in stages.items():
        send, recv, in_flight, token = _gather_start(placed[stage], item_meta(items), token, stage)
        gathers[stage] = (send, recv, in_flight)
    first_full = _gather_wait(*first_d2d, meta_first, token, "first_d2d", _forward_copies)
    W = natural(first, first_full)
    ca, sc = _tiny_unpack(first_full[-1][0])
    conv_a_full = ca.transpose(1, 0, 2).reshape(CONV_A_WIDTH, CONV_A_CH)
    sc_full = sc.transpose(1, 0, 2).reshape(SC_WIDTH, SC_CH)

    def need(stage, h):
        send, recv, in_flight = gathers[stage]
        items = stages[stage]
        return natural(items, _gather_wait(send, recv, in_flight, item_meta(items), h, stage))

    def gathered_layout(G, item):
        n, l = item
        A, B = shards[n].shape[1:]
        g = G[item]
        if n == 'tiny':
            return g
        if n == 'even_w_in':
            return g.reshape(A, N_CHIPS, B).transpose(1, 0, 2)[None]
        return g.reshape(1, A, N_CHIPS * B) if LAYOUT[n] == 'col' else g.reshape(1, N_CHIPS, A, B)

    scatters, tokens = {}, {}

    def ready(stage, G):
        items = grad_stages[stage]
        send, recv, gs1, lands, tok = _scatter_start([gathered_layout(G, it) for it in items], item_meta(items),
                                                     chip_idx, stage)
        scatters[stage] = (send, recv, gs1, lands)
        tokens[stage] = tok
        return tok[:1, :1]

    loss_part, grad_x, G, g_small = _local_step(x[0], mem[0], loss_target[0], W, need, token[:1, :1], ready,
                                                conv_a_full, sc_full, {n: w[n] for n in SMALL_NAMES})
    G[('tiny', 0)] = _tiny_pack(g_small['conv_a_w'].reshape(CONV_A_WIDTH, N_CHIPS, 128).transpose(1, 0, 2),
                                g_small['sc_conv_w'].reshape(SC_WIDTH, N_CHIPS, 256).transpose(1, 0, 2))[None]
    loss = lax.psum(loss_part, ("x", "y", "c"))

    ready('l0_ffn1', G)
    started = tokens['l0_ffn1']

    def summed(stage, after):
        send, recv, gs1, lands = scatters[stage]
        items = grad_stages[stage]
        sent, landed = _scatter_wait(send, recv, gs1, lands, item_meta(items), after, stage)
        return items, [_chip_sum_full(g, r, m_, chip_idx, name=f"rs_chip_sum_{n}_{l}")
                       for (n, l), g, r, m_ in zip(items, sent, landed, item_meta(items))]

    def adamw(n, g1):
        if n == 'tiny':
            pk = lambda d: _tiny_pack(d['conv_a_w'], d['sc_conv_w'])
            res = [_tiny_unpack(a) for a in _adamw_layers(pk(w), pk(m), pk(v), [g1[('tiny', 0)]], name="adamw_tiny")]
            for k, nn in enumerate(('conv_a_w', 'sc_conv_w')):
                grads[nn], deltas[nn], new_m[nn], new_v[nn] = (r[k] for r in res)
        else:
            gsrc = [g1[(n, l)] for l in range(w[n].shape[0])]
            grads[n], deltas[n], new_m[n], new_v[n] = _adamw_layers(w[n], m[n], v[n], gsrc, name=f"adamw_{n}")

    grads, deltas, new_m, new_v = {}, {}, {}, {}
    sum_of = {}
    for stage in ('l1', 'l0_rest'):
        its, ss = summed(stage, started)
        sum_of.update(zip(its, ss))
    swap_groups = [['even_w_in', 'even_w_out', 'odd_w_in', 'odd_w_out', 'xa_wq', 'xa_wkv', 'xa_wo'],
                   ['ffn2_w_gu', 'ffn2_w_down'], ['ffn1_w_gu', 'ffn1_w_down']]
    swaps, after = [], started
    for gi, names in enumerate(swap_groups):
        its = [it for it in sum_of if it[0] in names]
        send, recv, own, lands, after = _swap_start([sum_of[it] for it in its], after, f"g{gi}")
        swaps.append((its, send, recv, own, lands))
    _, small_parts = _pair_swap([], _pack_small(g_small), tag="small")
    g1 = {}

    def swapped(gi, after):
        its, send, recv, own, lands = swaps[gi]
        mine, theirs = _swap_wait(send, recv, own, lands, after, f"g{gi}")
        g1.update({it: [a, b] for it, a, b in zip(its, mine, theirs)})

    for gi in (0, 1):
        swapped(gi, after)
        for n in swap_groups[gi]:
            adamw(n, g1)
        after = deltas[swap_groups[gi][-1]]

    swapped(2, after)
    its, ss = summed('l0_ffn1', after)
    sib, _ = _pair_swap(ss, None, tag="last")
    g1.update({it: [a, b] for it, a, b in zip(its, ss, sib)})
    for n in ('ffn1_w_gu', 'ffn1_w_down', 'tiny'):
        adamw(n, g1)
    rows2d = lambda d: [d[n].reshape(-1, d[n].shape[-1]) for n in SMALL_NAMES]
    for dst, arrs in zip((grads, deltas, new_m, new_v),
                         _adamw_small(rows2d(w), rows2d(m), rows2d(v), small_parts, name="adamw_small")):
        dst.update({n: a.reshape(w[n].shape) for n, a in zip(SMALL_NAMES, arrs)})

    return (loss, grad_x[None], *[grads[n] for n in WEIGHT_NAMES], *[deltas[n] for n in WEIGHT_NAMES],
            *[new_m[n] for n in WEIGHT_NAMES], *[new_v[n] for n in WEIGHT_NAMES])
```

```python
import jax
import jax.numpy as jnp
from jax import lax
from jax.experimental import pallas as pl
from jax.experimental.pallas import tpu as pltpu

F32 = jnp.float32
BF16 = jnp.bfloat16

D_MODEL = 1024
D_FF = 2816
CONV_A_CH = 512
CONV_A_WIDTH = 31
SWA_HEADS = 8
SWA_KV_HEADS = 2
SWA_GROUP = 4
HEAD_DIM = 64
WINDOW = 128
SC_CH = 1024
SC_WIDTH = 3
XA_HEADS = 4
XA_HEAD_DIM = 256
RMS_EPS = 1e-6
LN_EPS = 1e-5

ADAM_LR = 0.001
ADAM_B1 = 0.9
ADAM_B2 = 0.999
ADAM_EPS = 1e-08
ADAM_WD = 0.01
ADAM_STEP = 10
ADAM_TILE_ELEMS = 384 * 1024

N_CHIPS = 4
N_DEV = 8
NEG_BIG = -1e30
VMEM_LIMIT = 56 * 1024 * 1024
MESH = pl.DeviceIdType.MESH

INPUT_NAMES = ['x', 'mem', 'ffn1_norm', 'ffn1_w_gu', 'ffn1_w_down', 'mix_norm', 'even_w_in', 'conv_a_w', 'conv_a_b',
               'conv_a_ln_g', 'conv_a_ln_b', 'swa_sinks', 'even_w_out', 'odd_w_in', 'sc_conv_w', 'odd_w_out', 'xa_norm',
               'xa_mem_norm', 'xa_wq', 'xa_wkv', 'xa_wo', 'ffn2_norm', 'ffn2_w_gu', 'ffn2_w_down', 'final_norm']
WEIGHT_NAMES = INPUT_NAMES[2:]
BIG = [('ffn1_w_gu', 'col'), ('ffn1_w_down', 'row'), ('even_w_in', 'col'), ('conv_a_w', 'col'), ('even_w_out', 'row'),
       ('odd_w_in', 'col'), ('sc_conv_w', 'col'), ('odd_w_out', 'row'), ('xa_wq', 'row'), ('xa_wkv', 'col'),
       ('xa_wo', 'row'), ('ffn2_w_gu', 'col'), ('ffn2_w_down', 'row')]
BIG_NAMES = [n for n, _ in BIG]
SMALL_NAMES = [n for n in WEIGHT_NAMES if n not in BIG_NAMES]


def _cparams(sem=None, vmem=VMEM_LIMIT):
    kw = dict(vmem_limit_bytes=vmem)
    if sem is not None:
        kw['dimension_semantics'] = sem
    return pltpu.CompilerParams(**kw)


def _div_tile(n, want, align=8):
    if n <= want:
        return n
    t = (want // align) * align
    while t >= align:
        if n % t == 0:
            return t
        t -= align
    return n


def _mm(a, b, *, name, ta=False, tb=False, out_dtype=BF16, tm=512, tn=512, tk=512, res=None, scale=1.0,
        b_layer=None, stack=None, n_map=None):
    n_map = n_map or (lambda j: j)
    if ta:
        K, M = a.shape
    else:
        M, K = a.shape
    if tb:
        N, K2 = b.shape[-2:]
    else:
        K2, N = b.shape[-2:]
    assert K == K2, (a.shape, b.shape, ta, tb)
    tm = _div_tile(M, tm, 128 if ta else 16)
    tn = _div_tile(N, tn, 128)
    tk = _div_tile(K, tk, 16 if ta else 128)
    nk = K // tk
    a_spec = pl.BlockSpec((tk, tm), lambda i, j, k: (k, i)) if ta else pl.BlockSpec((tm, tk), lambda i, j, k: (i, k))
    if b_layer is None:
        b_spec = pl.BlockSpec((tn, tk), lambda i, j, k: (j, k)) if tb else pl.BlockSpec((tk, tn), lambda i, j, k: (k, j))
    elif tb:
        b_spec = pl.BlockSpec((None, tn, tk), lambda i, j, k: (b_layer, j, k))
    else:
        b_spec = pl.BlockSpec((None, tk, tn), lambda i, j, k: (b_layer, k, j))
    o_spec = pl.BlockSpec((tm, tn), lambda i, j, k: (i, j))
    out_shape = jax.ShapeDtypeStruct((M, N), out_dtype)
    out_spec = o_spec
    aliases = {}
    extra_specs, extra_args = [], ()
    if stack is not None:
        n_layers, layer, buf = stack[:3]
        n_total = stack[3] if len(stack) > 3 else N
        out_shape = jax.ShapeDtypeStruct((n_layers, M, n_total), out_dtype)
        out_spec = pl.BlockSpec((None, tm, tn), lambda i, j, k: (layer, i, n_map(j)))
        if buf is not None:
            extra_specs, extra_args = [pl.BlockSpec(memory_space=pl.ANY)], (buf,)
            aliases = {2 + (res is not None): 0}
    dims = (((0 if ta else 1,), (1 if tb else 0,)), ((), ()))
    has_res = res is not None
    n_extra = len(extra_args)

    def body(*refs):
        if n_extra:
            refs = refs[:2 + has_res] + refs[2 + has_res + n_extra:]
        if has_res:
            a_ref, b_ref, r_ref, o_ref, acc_ref = refs
        else:
            a_ref, b_ref, o_ref, acc_ref = refs
        k = pl.program_id(2)
        p = lax.dot_general(a_ref[...].astype(BF16), b_ref[...].astype(BF16), dims, preferred_element_type=F32)

        @pl.when(k == 0)
        def _():
            acc_ref[...] = p

        @pl.when(k > 0)
        def _():
            acc_ref[...] += p

        @pl.when(k == nk - 1)
        def _():
            r = acc_ref[...] * scale
            if has_res:
                r = r_ref[...] + r
            o_ref[...] = r.astype(out_dtype)

    in_specs = [a_spec, b_spec] + ([o_spec] if has_res else []) + extra_specs
    args = (a, b) + ((res,) if has_res else ()) + extra_args
    return pl.pallas_call(
        body, name=name, grid=(M // tm, N // tn, nk), in_specs=in_specs, out_specs=out_spec,
        out_shape=out_shape, input_output_aliases=aliases,
        scratch_shapes=[pltpu.VMEM((tm, tn), F32)],
        compiler_params=_cparams(("parallel", "parallel", "arbitrary")),
    )(*args)


def _rms_fwd(x, g, *, name):
    S, D = x.shape
    ts = _div_tile(S, 512)

    def body(x_ref, g_ref, o_ref):
        xv = x_ref[...]
        r = lax.rsqrt(jnp.mean(xv * xv, axis=-1, keepdims=True) + RMS_EPS)
        o_ref[...] = (xv * r * g_ref[...]).astype(BF16)

    return pl.pallas_call(
        body, name=name, grid=(S // ts,),
        in_specs=[pl.BlockSpec((ts, D), lambda i: (i, 0)), pl.BlockSpec((1, D), lambda i: (0, 0))],
        out_specs=pl.BlockSpec((ts, D), lambda i: (i, 0)),
        out_shape=jax.ShapeDtypeStruct((S, D), BF16),
        compiler_params=_cparams(("parallel",)),
    )(x, g)


NORM_SLAB = 256


def _norm_mm(h, g, w, layer, *, name):
    S, D = h.shape
    N = w.shape[-1]
    tm = _div_tile(S, 1024, NORM_SLAB)
    slab = min(NORM_SLAB, tm)

    def body(h_ref, g_ref, w_ref, u_ref, z_ref):
        for r0 in range(0, tm, slab):
            rows = pl.ds(r0, slab)
            xv = h_ref[rows, :]
            r = lax.rsqrt(jnp.mean(xv * xv, axis=-1, keepdims=True) + RMS_EPS)
            u = (xv * r * g_ref[...]).astype(BF16)
            u_ref[rows, :] = u
            z_ref[rows, :] = jnp.dot(u, w_ref[...], preferred_element_type=F32).astype(BF16)

    row = pl.BlockSpec((tm, D), lambda i: (i, 0))
    return pl.pallas_call(
        body, name=name, grid=(S // tm,),
        in_specs=[row, pl.BlockSpec((1, D), lambda i: (0, 0)), pl.BlockSpec((None, D, N), lambda i: (layer, 0, 0))],
        out_specs=[row, pl.BlockSpec((tm, N), lambda i: (i, 0))],
        out_shape=[jax.ShapeDtypeStruct((S, D), BF16), jax.ShapeDtypeStruct((S, N), BF16)],
        compiler_params=_cparams(("parallel",)),
    )(h, g, w)


def _mm_norm_bwd(dz, w, layer, h, g, dres, *, name, tk):
    S, K = dz.shape
    D = h.shape[1]
    tm = _div_tile(S, 1024, NORM_SLAB)
    slab = min(NORM_SLAB, tm)
    tk = _div_tile(K, tk, 128)
    nk = K // tk
    nt = (((1,), (1,)), ((), ()))

    def body(dz_ref, w_ref, h_ref, g_ref, dr_ref, dx_ref, dg_ref, acc):
        i = pl.program_id(0)
        k = pl.program_id(1)

        def norm_bwd(du_of):
            part = jnp.zeros((1, D), F32)
            for r0 in range(0, tm, slab):
                rows = pl.ds(r0, slab)
                du = du_of(rows)
                xv = h_ref[rows, :]
                r = lax.rsqrt(jnp.mean(xv * xv, axis=-1, keepdims=True) + RMS_EPS)
                xhat = xv * r
                part = part + jnp.sum(du * xhat, axis=0, keepdims=True)
                dxhat = du * g_ref[...]
                dx_ref[rows, :] = dr_ref[rows, :] + r * (
                    dxhat - xhat * jnp.mean(dxhat * xhat, axis=-1, keepdims=True))

            @pl.when(i == 0)
            def _():
                dg_ref[...] = part

            @pl.when(i > 0)
            def _():
                dg_ref[...] += part

        if nk == 1:
            norm_bwd(lambda rows: lax.dot_general(dz_ref[rows, :], w_ref[...], nt, preferred_element_type=F32))
        else:
            p = lax.dot_general(dz_ref[...], w_ref[...], nt, preferred_element_type=F32)

            @pl.when(k == 0)
            def _():
                acc[...] = p

            @pl.when(k > 0)
            def _():
                acc[...] += p

            @pl.when(k == nk - 1)
            def _():
                norm_bwd(lambda rows: acc[rows, :])

    row = pl.BlockSpec((tm, D), lambda i, k: (i, 0))
    vec = pl.BlockSpec((1, D), lambda i, k: (0, 0))
    return pl.pallas_call(
        body, name=name, grid=(S // tm, nk),
        in_specs=[pl.BlockSpec((tm, tk), lambda i, k: (i, k)), pl.BlockSpec((None, D, tk), lambda i, k: (layer, 0, k)),
                  row, vec, row],
        out_specs=[row, vec],
        out_shape=[jax.ShapeDtypeStruct((S, D), F32), jax.ShapeDtypeStruct((1, D), F32)],
        scratch_shapes=[pltpu.VMEM((tm, D), F32)],
        compiler_params=_cparams(("arbitrary", "arbitrary")),
    )(dz, w, h, g, dres)


def _rms_bwd(x, g, du, dres, *, name):
    S, D = x.shape
    ts = _div_tile(S, 512)
    has_res = dres is not None

    def body(*refs):
        if has_res:
            x_ref, g_ref, du_ref, dr_ref, dx_ref, dg_ref = refs
        else:
            x_ref, g_ref, du_ref, dg_ref = refs
        i = pl.program_id(0)
        xv = x_ref[...]
        duv = du_ref[...].astype(F32)
        r = lax.rsqrt(jnp.mean(xv * xv, axis=-1, keepdims=True) + RMS_EPS)
        xhat = xv * r
        part = jnp.sum(duv * xhat, axis=0, keepdims=True)

        @pl.when(i == 0)
        def _():
            dg_ref[...] = part

        @pl.when(i > 0)
        def _():
            dg_ref[...] += part

        if has_res:
            dxhat = duv * g_ref[...]
            dx = r * (dxhat - xhat * jnp.mean(dxhat * xhat, axis=-1, keepdims=True))
            dx_ref[...] = dr_ref[...] + dx

    row = pl.BlockSpec((ts, D), lambda i: (i, 0))
    vec = pl.BlockSpec((1, D), lambda i: (0, 0))
    if has_res:
        dx, dg = pl.pallas_call(
            body, name=name, grid=(S // ts,), in_specs=[row, vec, row, row], out_specs=[row, vec],
            out_shape=[jax.ShapeDtypeStruct((S, D), F32), jax.ShapeDtypeStruct((1, D), F32)],
            compiler_params=_cparams(("arbitrary",)),
        )(x, g, du, dres)
        return dx, dg
    dg = pl.pallas_call(
        body, name=name, grid=(S // ts,), in_specs=[row, vec, row], out_specs=vec,
        out_shape=jax.ShapeDtypeStruct((1, D), F32),
        compiler_params=_cparams(("arbitrary",)),
    )(x, g, du)
    return None, dg


def _final_loss(h, g, tgt, *, name):
    S, D = h.shape
    ts = _div_tile(S, 512)

    def body(h_ref, g_ref, t_ref, loss_ref, dh_ref, dg_ref):
        i = pl.program_id(0)
        xv = h_ref[...]
        gv = g_ref[...]
        r = lax.rsqrt(jnp.mean(xv * xv, axis=-1, keepdims=True) + RMS_EPS)
        xhat = xv * r
        err = xhat * gv - t_ref[...]
        lpart = 0.5 * jnp.sum(jnp.mean(err * err, axis=-1, keepdims=True), axis=0, keepdims=True)
        dy = err * (1.0 / D)
        gpart = jnp.sum(dy * xhat, axis=0, keepdims=True)

        @pl.when(i == 0)
        def _():
            loss_ref[...] = jnp.broadcast_to(lpart, loss_ref.shape)
            dg_ref[...] = gpart

        @pl.when(i > 0)
        def _():
            loss_ref[...] += jnp.broadcast_to(lpart, loss_ref.shape)
            dg_ref[...] += gpart

        dxhat = dy * gv
        dh_ref[...] = r * (dxhat - xhat * jnp.mean(dxhat * xhat, axis=-1, keepdims=True))

    row = pl.BlockSpec((ts, D), lambda i: (i, 0))
    vec = pl.BlockSpec((1, D), lambda i: (0, 0))
    return pl.pallas_call(
        body, name=name, grid=(S // ts,), in_specs=[row, vec, row],
        out_specs=[pl.BlockSpec((8, 128), lambda i: (0, 0)), row, vec],
        out_shape=[jax.ShapeDtypeStruct((8, 128), F32), jax.ShapeDtypeStruct((S, D), F32),
                   jax.ShapeDtypeStruct((1, D), F32)],
        compiler_params=_cparams(("arbitrary",)),
    )(h, g, tgt)


def _sigmoid(x):
    return 1.0 / (1.0 + jnp.exp(-x))


FFN_CHUNK = 1408
FFN_CHUNKS = D_FF // FFN_CHUNK
FFN_BWD_PIECE = 384
FFN_BWD_SLAB = 256


def _ffn_fwd_fused(h, g, w_gu, w_down, layer, *, name):
    S, D = h.shape
    tm = _div_tile(S, 512, 16)
    tf, nj = FFN_CHUNK, FFN_CHUNKS

    def body(h_ref, g_ref, wg_ref, wu_ref, wd_ref, h2_ref, u_ref, gate_ref, up_ref, a_ref, u_s, acc):
        j = pl.program_id(1)

        @pl.when(j == 0)
        def _():
            xv = h_ref[...]
            r = lax.rsqrt(jnp.mean(xv * xv, axis=-1, keepdims=True) + RMS_EPS)
            u = (xv * r * g_ref[...]).astype(BF16)
            u_s[...] = u
            u_ref[...] = u

        u = u_s[...]
        gate = jnp.dot(u, wg_ref[...], preferred_element_type=F32)
        up = jnp.dot(u, wu_ref[...], preferred_element_type=F32)
        gate_ref[...] = gate.astype(BF16)
        up_ref[...] = up.astype(BF16)
        a = (gate * _sigmoid(gate) * up).astype(BF16)
        a_ref[...] = a
        p = jnp.dot(a, wd_ref[...], preferred_element_type=F32)

        @pl.when(j == 0)
        def _():
            acc[...] = p

        @pl.when(j > 0)
        def _():
            acc[...] += p

        @pl.when(j == nj - 1)
        def _():
            h2_ref[...] = h_ref[...] + 0.5 * acc[...]

    row = pl.BlockSpec((tm, D), lambda i, j: (i, 0))
    chunk = pl.BlockSpec((tm, tf), lambda i, j: (i, j))
    hidden = jax.ShapeDtypeStruct((S, D_FF), BF16)
    return pl.pallas_call(
        body, name=name, grid=(S // tm, nj),
        in_specs=[row, pl.BlockSpec((1, D), lambda i, j: (0, 0)),
                  pl.BlockSpec((None, D, tf), lambda i, j: (layer, 0, j)),
                  pl.BlockSpec((None, D, tf), lambda i, j: (layer, 0, nj + j)),
                  pl.BlockSpec((None, tf, D), lambda i, j: (layer, j, 0))],
        out_specs=[row, row, chunk, chunk, chunk],
        out_shape=[jax.ShapeDtypeStruct((S, D), F32), jax.ShapeDtypeStruct((S, D), BF16), hidden, hidden, hidden],
        scratch_shapes=[pltpu.VMEM((tm, D), BF16), pltpu.VMEM((tm, D), F32)],
        compiler_params=_cparams(("parallel", "arbitrary")),
    )(h, g, w_gu, w_gu, w_down)


def _ffn_bwd_fused(dh, h, g, gate, up, w_gu, w_down, layer, *, name):
    S, D = h.shape
    tm = _div_tile(S, 512, FFN_BWD_SLAB)
    tf = FFN_CHUNK
    nj = D_FF // tf
    slab = min(FFN_BWD_SLAB, tm)
    nt = (((1,), (1,)), ((), ()))
    pieces = [(c0, min(FFN_BWD_PIECE, tf - c0)) for c0 in range(0, tf, FFN_BWD_PIECE)]

    def body(dh_ref, h_ref, g_ref, gate_ref, up_ref, wg_ref, wu_ref, wd_ref, dx_ref, dg_ref, dgate_ref, dup_ref,
             dy_s, acc):
        i = pl.program_id(0)
        j = pl.program_id(1)

        @pl.when(j == 0)
        def _():
            for r0 in range(0, tm, slab):
                rows = pl.ds(r0, slab)
                dy_s[rows, :] = (0.5 * dh_ref[rows, :]).astype(BF16)

        p = None
        for c0, cw in pieces:
            cols = pl.ds(c0, cw)
            da = lax.dot_general(dy_s[...], wd_ref[cols, :], nt, preferred_element_type=F32)
            gt = gate_ref[:, cols].astype(F32)
            sg = _sigmoid(gt)
            dgate = (da * up_ref[:, cols].astype(F32) * sg * (1.0 + gt * (1.0 - sg))).astype(BF16)
            dup = (da * gt * sg).astype(BF16)
            dgate_ref[:, cols] = dgate
            dup_ref[:, cols] = dup
            q = (lax.dot_general(dgate, wg_ref[:, cols], nt, preferred_element_type=F32)
                 + lax.dot_general(dup, wu_ref[:, cols], nt, preferred_element_type=F32))
            p = q if p is None else p + q

        @pl.when(j == 0)
        def _():
            acc[...] = p

        @pl.when(j > 0)
        def _():
            acc[...] += p

        @pl.when(j == nj - 1)
        def _():
            part = jnp.zeros((1, D), F32)
            for r0 in range(0, tm, slab):
                rows = pl.ds(r0, slab)
                xv = h_ref[rows, :]
                du = acc[rows, :]
                r = lax.rsqrt(jnp.mean(xv * xv, axis=-1, keepdims=True) + RMS_EPS)
                xhat = xv * r
                part = part + jnp.sum(du * xhat, axis=0, keepdims=True)
                dxhat = du * g_ref[...]
                dx_ref[rows, :] = dh_ref[rows, :] + r * (
                    dxhat - xhat * jnp.mean(dxhat * xhat, axis=-1, keepdims=True))

            @pl.when(i == 0)
            def _():
                dg_ref[...] = part

            @pl.when(i > 0)
            def _():
                dg_ref[...] += part

    row = pl.BlockSpec((tm, D), lambda i, j: (i, 0))
    vec = pl.BlockSpec((1, D), lambda i, j: (0, 0))
    chunk = pl.BlockSpec((tm, tf), lambda i, j: (i, j))
    hidden = jax.ShapeDtypeStruct((S, D_FF), BF16)
    return pl.pallas_call(
        body, name=name, grid=(S // tm, nj),
        in_specs=[row, row, vec, chunk, chunk,
                  pl.BlockSpec((None, D, tf), lambda i, j: (layer, 0, j)),
                  pl.BlockSpec((None, D, tf), lambda i, j: (layer, 0, nj + j)),
                  pl.BlockSpec((None, tf, D), lambda i, j: (layer, j, 0))],
        out_specs=[row, vec, chunk, chunk],
        out_shape=[jax.ShapeDtypeStruct((S, D), F32), jax.ShapeDtypeStruct((1, D), F32), hidden, hidden],
        scratch_shapes=[pltpu.VMEM((tm, D), BF16), pltpu.VMEM((tm, D), F32)],
        compiler_params=_cparams(("arbitrary", "arbitrary")),
    )(dh, h, g, gate, up, w_gu, w_gu, w_down)


CONV_HALO = 32
CONV_SUB_ROWS = 128


def _shifted_taps(win, shifted, ts):
    n = ts + CONV_HALO - 8
    for r in range(1, 8):
        shifted[r - 1] = win[pl.ds(r, n), :]

    def tap(start, rows, lanes):
        q, r = divmod(start, 8)
        if r == 0:
            return win[pl.ds(start, rows), lanes]
        return shifted[r - 1, pl.ds(8 * q, rows), lanes]

    return tap


def _conv_a_fwd(z, w, bias, ln_g, ln_b, *, name):
    S = z.shape[0]
    C = CONV_A_CH
    ts = _div_tile(S, 256, 32)

    def body(val_ref, gate_ref, w_ref, b_ref, g_ref, lb_ref, c_ref, act_ref, win, shifted):
        i = pl.program_id(0)

        @pl.when(i == 0)
        def _():
            win[pl.ds(0, CONV_HALO), :] = jnp.zeros((CONV_HALO, C), F32)

        @pl.when(i > 0)
        def _():
            win[pl.ds(0, CONV_HALO), :] = win[pl.ds(ts, CONV_HALO), :]

        a = val_ref[...].astype(F32) * _sigmoid(gate_ref[...].astype(F32))
        win[pl.ds(CONV_HALO, ts), :] = a
        tap = _shifted_taps(win, shifted, ts)
        rs = min(CONV_SUB_ROWS, ts)
        for cb in range(C // 128):
            lanes = pl.ds(128 * cb, 128)
            for rt in range(ts // rs):
                sub = jnp.broadcast_to(b_ref[:, lanes], (rs, 128))
                for k in range(CONV_A_WIDTH):
                    sub = sub + w_ref[pl.ds(k, 1), lanes] * tap(
                        CONV_HALO - (CONV_A_WIDTH - 1) + k + rs * rt, rs, lanes)
                c_ref[pl.ds(rs * rt, rs), lanes] = sub
        acc = c_ref[...]
        mu = jnp.mean(acc, axis=-1, keepdims=True)
        xc = acc - mu
        var = jnp.mean(xc * xc, axis=-1, keepdims=True)
        ln = xc * lax.rsqrt(var + LN_EPS) * g_ref[...] + lb_ref[...]
        act_ref[...] = (ln * _sigmoid(ln)).astype(BF16)

    row = lambda col: pl.BlockSpec((ts, C), lambda i, col=col: (i, col))
    vec = pl.BlockSpec((1, C), lambda i: (0, 0))
    return pl.pallas_call(
        body, name=name, grid=(S // ts,),
        in_specs=[row(0), row(1), pl.BlockSpec((32, C), lambda i: (0, 0)), vec, vec, vec],
        out_specs=[row(0), row(0)],
        out_shape=[jax.ShapeDtypeStruct((S, C), F32), jax.ShapeDtypeStruct((S, 2 * C), BF16)],
        scratch_shapes=[pltpu.VMEM((ts + CONV_HALO, C), F32), pltpu.VMEM((7, ts + CONV_HALO - 8, C), F32)],
        compiler_params=_cparams(("arbitrary",)),
    )(z, z, w, bias, ln_g, ln_b)


def _conv_a_bwd(z, c, dcat, w, ln_g, ln_b, *, name):
    S = z.shape[0]
    C = CONV_A_CH
    ts = _div_tile(S, 256, 32)
    n = S // ts

    def body(val_ref, gate_ref, c_ref, da_ref, w_ref, g_ref, lb_ref, dz_ref, small_ref, win, a_s, da_s, dw8,
             shifted):
        i = pl.program_id(0)

        @pl.when(i == 0)
        def _():
            win[pl.ds(ts, CONV_HALO), :] = jnp.zeros((CONV_HALO, C), F32)
            small_ref[...] = jnp.zeros(small_ref.shape, F32)
            dw8[...] = jnp.zeros(dw8.shape, F32)

        @pl.when(i > 0)
        def _():
            win[pl.ds(ts, CONV_HALO), :] = win[pl.ds(0, CONV_HALO), :]

        cv = c_ref[...]
        gv = g_ref[...]
        mu = jnp.mean(cv, axis=-1, keepdims=True)
        xc = cv - mu
        var = jnp.mean(xc * xc, axis=-1, keepdims=True)
        rstd = lax.rsqrt(var + LN_EPS)
        xhat = xc * rstd
        ln = xhat * gv + lb_ref[...]
        sg = _sigmoid(ln)
        dln = da_ref[...].astype(F32) * (sg * (1.0 + ln * (1.0 - sg)))
        small_ref[pl.ds(33, 1), :] += jnp.sum(dln * xhat, axis=0, keepdims=True)
        small_ref[pl.ds(34, 1), :] += jnp.sum(dln, axis=0, keepdims=True)
        dxhat = dln * gv
        dc = rstd * (dxhat - jnp.mean(dxhat, axis=-1, keepdims=True)
                     - xhat * jnp.mean(dxhat * xhat, axis=-1, keepdims=True))
        small_ref[pl.ds(32, 1), :] += jnp.sum(dc, axis=0, keepdims=True)
        win[pl.ds(0, ts), :] = dc

        val = val_ref[...].astype(F32)
        sgg = _sigmoid(gate_ref[...].astype(F32))
        a_s[...] = val * sgg
        tap = _shifted_taps(win, shifted, ts)
        rs = min(CONV_SUB_ROWS, ts)
        for cb in range(C // 128):
            lanes = pl.ds(128 * cb, 128)
            for rt in range(ts // rs):
                a_sub = a_s[pl.ds(rs * rt, rs), lanes]
                da = jnp.zeros((rs, 128), F32)
                for k in range(CONV_A_WIDTH):
                    sh = tap(CONV_A_WIDTH - 1 - k + rs * rt, rs, lanes)
                    da = da + w_ref[pl.ds(k, 1), lanes] * sh
                    prod = a_sub * sh
                    part = prod[0:8]
                    for r in range(1, rs // 8):
                        part = part + prod[8 * r:8 * r + 8]
                    dw8[pl.ds(8 * k, 8), lanes] += part
                da_s[pl.ds(rs * rt, rs), lanes] = da
        da = da_s[...]
        dz_ref[:, pl.ds(0, C)] = (da * sgg).astype(BF16)
        dz_ref[:, pl.ds(C, C)] = (da * val * sgg * (1.0 - sgg)).astype(BF16)

        @pl.when(i == n - 1)
        def _():
            for k in range(CONV_A_WIDTH):
                small_ref[pl.ds(k, 1), :] = jnp.sum(dw8[pl.ds(8 * k, 8), :], axis=0, keepdims=True)

    row = lambda col: pl.BlockSpec((ts, C), lambda i, col=col: (n - 1 - i, col))
    vec = pl.BlockSpec((1, C), lambda i: (0, 0))
    return pl.pallas_call(
        body, name=name, grid=(n,),
        in_specs=[row(0), row(1), row(0), row(0), pl.BlockSpec((32, C), lambda i: (0, 0)), vec, vec],
        out_specs=[pl.BlockSpec((ts, 2 * C), lambda i: (n - 1 - i, 0)), pl.BlockSpec((40, C), lambda i: (0, 0))],
        out_shape=[jax.ShapeDtypeStruct((S, 2 * C), BF16), jax.ShapeDtypeStruct((40, C), F32)],
        scratch_shapes=[pltpu.VMEM((ts + CONV_HALO, C), F32), pltpu.VMEM((ts, C), F32), pltpu.VMEM((ts, C), F32),
                        pltpu.VMEM((8 * 32, C), F32), pltpu.VMEM((7, ts + CONV_HALO - 8, C), F32)],
        compiler_params=_cparams(("arbitrary",)),
    )(z, z, c, dcat, w, ln_g, ln_b)


SC_HALO = 8


def _sconv_fwd(z, w, *, name):
    S = z.shape[0]
    C = SC_CH
    ts = _div_tile(S, 256, 16)

    def body(gb_ref, gc_ref, v_ref, w_ref, y_ref, cc_ref, win):
        i = pl.program_id(0)

        @pl.when(i == 0)
        def _():
            win[pl.ds(0, SC_HALO), :] = jnp.zeros((SC_HALO, C), F32)

        @pl.when(i > 0)
        def _():
            win[pl.ds(0, SC_HALO), :] = win[pl.ds(ts, SC_HALO), :]

        win[pl.ds(SC_HALO, ts), :] = gc_ref[...].astype(F32) * v_ref[...].astype(F32)
        acc = jnp.zeros((ts, C), F32)
        for k in range(SC_WIDTH):
            acc = acc + w_ref[pl.ds(k, 1), :] * win[pl.ds(SC_HALO - (SC_WIDTH - 1) + k, ts), :]
        cc_ref[...] = acc.astype(BF16)
        y_ref[...] = (gb_ref[...].astype(F32) * acc).astype(BF16)

    row = lambda col: pl.BlockSpec((ts, C), lambda i, col=col: (i, col))
    return pl.pallas_call(
        body, name=name, grid=(S // ts,),
        in_specs=[row(0), row(1), row(2), pl.BlockSpec((8, C), lambda i: (0, 0))],
        out_specs=[row(0), row(0)],
        out_shape=[jax.ShapeDtypeStruct((S, C), BF16), jax.ShapeDtypeStruct((S, C), BF16)],
        scratch_shapes=[pltpu.VMEM((ts + SC_HALO, C), F32)],
        compiler_params=_cparams(("arbitrary",)),
    )(z, z, z, w)


def _sconv_bwd(z, cc, dy, w, *, name):
    S = z.shape[0]
    C = SC_CH
    ts = _div_tile(S, 256, 16)
    n = S // ts

    def body(gb_ref, gc_ref, v_ref, cc_ref, dy_ref, w_ref, dz_ref, dw_ref, win):
        i = pl.program_id(0)

        @pl.when(i == 0)
        def _():
            win[pl.ds(ts, SC_HALO), :] = jnp.zeros((SC_HALO, C), F32)
            dw_ref[...] = jnp.zeros(dw_ref.shape, F32)

        @pl.when(i > 0)
        def _():
            win[pl.ds(ts, SC_HALO), :] = win[pl.ds(0, SC_HALO), :]

        dyv = dy_ref[...].astype(F32)
        gb = gb_ref[...].astype(F32)
        gc = gc_ref[...].astype(F32)
        val = v_ref[...].astype(F32)
        dz_ref[:, pl.ds(0, C)] = (dyv * cc_ref[...].astype(F32)).astype(BF16)
        win[pl.ds(0, ts), :] = dyv * gb
        cv = gc * val
        dcv = jnp.zeros((ts, C), F32)
        for k in range(SC_WIDTH):
            sh = win[pl.ds(SC_WIDTH - 1 - k, ts), :]
            dcv = dcv + w_ref[pl.ds(k, 1), :] * sh
            dw_ref[pl.ds(k, 1), :] += jnp.sum(cv * sh, axis=0, keepdims=True)
        dz_ref[:, pl.ds(C, C)] = (dcv * val).astype(BF16)
        dz_ref[:, pl.ds(2 * C, C)] = (dcv * gc).astype(BF16)

    row = lambda col: pl.BlockSpec((ts, C), lambda i, col=col: (n - 1 - i, col))
    return pl.pallas_call(
        body, name=name, grid=(n,),
        in_specs=[row(0), row(1), row(2), row(0), row(0), pl.BlockSpec((8, C), lambda i: (0, 0))],
        out_specs=[pl.BlockSpec((ts, 3 * C), lambda i: (n - 1 - i, 0)), pl.BlockSpec((8, C), lambda i: (0, 0))],
        out_shape=[jax.ShapeDtypeStruct((S, 3 * C), BF16), jax.ShapeDtypeStruct((8, C), F32)],
        scratch_shapes=[pltpu.VMEM((ts + SC_HALO, C), F32)],
        compiler_params=_cparams(("arbitrary",)),
    )(z, z, z, cc, dy, w)


SWA_Q_COL = 2
SWA_SLOPES = [2.0 ** (-8.0 * (h + 1) / SWA_HEADS) for h in range(SWA_HEADS)]
SWA_SCALE = HEAD_DIM ** -0.5


SWA_GROUP_ROWS = SWA_GROUP * WINDOW


def _swa_masks():
    shape = (SWA_GROUP_ROWS, 2 * WINDOW)
    ii = lax.broadcasted_iota(jnp.int32, shape, 0)
    jj = lax.broadcasted_iota(jnp.int32, shape, 1)
    dist = (ii & (WINDOW - 1)) + WINDOW - jj
    valid = (dist >= 0) & (dist < WINDOW)
    grp = lax.broadcasted_iota(jnp.int32, (SWA_GROUP_ROWS, 1), 0) // WINDOW
    return dist.astype(F32), valid, jj, grp


def _by_group(grp, vals):
    out = jnp.full(grp.shape, vals[SWA_GROUP - 1], F32)
    for g in range(SWA_GROUP - 2, -1, -1):
        out = jnp.where(grp == g, vals[g], out)
    return out


def _stack_heads(ref, rows, kv):
    return jnp.concatenate([ref[rows, pl.ds(HEAD_DIM * (kv * SWA_GROUP + g), HEAD_DIM)] for g in range(SWA_GROUP)],
                           axis=0)


def _swa_probs(qg, kk, sink, slope, distf, valid):
    s = lax.dot_general(qg, kk, (((1,), (1,)), ((), ())), preferred_element_type=F32) * SWA_SCALE
    s = s - slope * distf
    s = jnp.where(valid, s, NEG_BIG)
    m = jnp.maximum(jnp.max(s, axis=-1, keepdims=True), sink)
    p = jnp.exp(s - m)
    l = jnp.sum(p, axis=-1, keepdims=True) + jnp.exp(sink - m)
    return p, m, l


def _swa_fwd(z, kpad, vpad, sinks, cat, *, name):
    S = z.shape[0]
    tq = _div_tile(S, 256, 128)
    nblk = tq // WINDOW
    W = WINDOW

    def body(sink_ref, q_ref, k_ref, v_ref, cat_ref, o_ref):
        i = pl.program_id(0)
        distf, valid0, jj, grp = _swa_masks()
        for kv in range(SWA_KV_HEADS):
            heads = range(kv * SWA_GROUP, (kv + 1) * SWA_GROUP)
            sink = _by_group(grp, [sink_ref[h] for h in heads])
            slope = _by_group(grp, [SWA_SLOPES[h] for h in heads])
            for b in range(nblk):
                nb = i * nblk + b
                start = pl.multiple_of(nb * W, W)
                rows = pl.ds(W * b, W)
                valid = valid0 & ((jj >= W) | (nb > 0))
                kk = k_ref[pl.ds(start, 2 * W), pl.ds(HEAD_DIM * kv, HEAD_DIM)]
                vv = v_ref[pl.ds(start, 2 * W), pl.ds(HEAD_DIM * kv, HEAD_DIM)]
                p, m, l = _swa_probs(_stack_heads(q_ref, rows, kv), kk, sink, slope, distf, valid)
                o = (jnp.dot(p.astype(BF16), vv, preferred_element_type=F32) / l).astype(BF16)
                for g, h in enumerate(heads):
                    o_ref[rows, pl.ds(HEAD_DIM * h, HEAD_DIM)] = o[W * g:W * (g + 1)]

    full = pl.BlockSpec((S + W, 2 * HEAD_DIM), lambda i: (0, 0))
    return pl.pallas_call(
        body, name=name, grid=(S // tq,),
        in_specs=[pl.BlockSpec(memory_space=pltpu.SMEM), pl.BlockSpec((tq, 512), lambda i: (i, SWA_Q_COL)), full, full,
                  pl.BlockSpec(memory_space=pl.ANY)],
        out_specs=pl.BlockSpec((tq, 512), lambda i: (i, 1)),
        out_shape=jax.ShapeDtypeStruct((S, 1024), BF16), input_output_aliases={4: 0},
        compiler_params=_cparams(("parallel",)),
    )(sinks, z, kpad, vpad, cat)


def _swa_bwd(z, kpad, vpad, sinks, dcat, *, name):
    S = z.shape[0]
    tq = _div_tile(S, 256, 128)
    nblk = tq // WINDOW
    W = WINDOW

    def body(sink_ref, q_ref, k_ref, v_ref, do_ref, dq_ref, dk_ref, dv_ref, ds_ref):
        i = pl.program_id(0)

        @pl.when(i == 0)
        def _():
            dk_ref[...] = jnp.zeros(dk_ref.shape, F32)
            dv_ref[...] = jnp.zeros(dv_ref.shape, F32)
            ds_ref[...] = jnp.zeros(ds_ref.shape, F32)

        distf, valid0, jj, grp = _swa_masks()
        tn = (((0,), (0,)), ((), ()))
        for kv in range(SWA_KV_HEADS):
            heads = range(kv * SWA_GROUP, (kv + 1) * SWA_GROUP)
            sink = _by_group(grp, [sink_ref[h] for h in heads])
            slope = _by_group(grp, [SWA_SLOPES[h] for h in heads])
            for b in range(nblk):
                nb = i * nblk + b
                start = pl.multiple_of(nb * W, W)
                rows = pl.ds(W * b, W)
                valid = valid0 & ((jj >= W) | (nb > 0))
                kk = k_ref[pl.ds(start, 2 * W), pl.ds(HEAD_DIM * kv, HEAD_DIM)]
                vv = v_ref[pl.ds(start, 2 * W), pl.ds(HEAD_DIM * kv, HEAD_DIM)]
                qg = _stack_heads(q_ref, rows, kv)
                dog = _stack_heads(do_ref, rows, kv)
                p, m, l = _swa_probs(qg, kk, sink, slope, distf, valid)
                inv_l = 1.0 / l
                pn = p * inv_l
                dp = lax.dot_general(dog, vv, (((1,), (1,)), ((), ())), preferred_element_type=F32)
                delta = jnp.sum(pn * dp, axis=-1, keepdims=True)
                dsc = (pn * (dp - delta)).astype(BF16)
                dsink = jnp.exp(sink - m) * inv_l * delta
                dq = (jnp.dot(dsc, kk, preferred_element_type=F32) * SWA_SCALE).astype(BF16)
                for g, h in enumerate(heads):
                    ds_ref[pl.ds(h, 1), :] += jnp.broadcast_to(
                        -jnp.sum(dsink[W * g:W * (g + 1)], axis=0, keepdims=True), (1, 128))
                    dq_ref[rows, pl.ds(HEAD_DIM * h, HEAD_DIM)] = dq[W * g:W * (g + 1)]
                dk_ref[pl.ds(start, 2 * W), pl.ds(HEAD_DIM * kv, HEAD_DIM)] += lax.dot_general(
                    dsc, qg, tn, preferred_element_type=F32) * SWA_SCALE
                dv_ref[pl.ds(start, 2 * W), pl.ds(HEAD_DIM * kv, HEAD_DIM)] += lax.dot_general(
                    pn.astype(BF16), dog, tn, preferred_element_type=F32)

    full = pl.BlockSpec((S + W, 2 * HEAD_DIM), lambda i: (0, 0))
    return pl.pallas_call(
        body, name=name, grid=(S // tq,),
        in_specs=[pl.BlockSpec(memory_space=pltpu.SMEM), pl.BlockSpec((tq, 512), lambda i: (i, SWA_Q_COL)), full, full,
                  pl.BlockSpec((tq, 512), lambda i: (i, 1))],
        out_specs=[pl.BlockSpec((tq, 512), lambda i: (i, 0)), full, full, pl.BlockSpec((8, 128), lambda i: (0, 0))],
        out_shape=[jax.ShapeDtypeStruct((S, 512), BF16), jax.ShapeDtypeStruct((S + W, 2 * HEAD_DIM), F32),
                   jax.ShapeDtypeStruct((S + W, 2 * HEAD_DIM), F32), jax.ShapeDtypeStruct((8, 128), F32)],
        compiler_params=_cparams(("arbitrary",)),
    )(sinks, z, kpad, vpad, dcat)


XA_SCALE = XA_HEAD_DIM ** -0.5


def _xa_probs(qh, kh):
    s = lax.dot_general(qh, kh, (((1,), (1,)), ((), ())), preferred_element_type=F32) * XA_SCALE
    m = jnp.max(s, axis=-1, keepdims=True)
    p = jnp.exp(s - m)
    return p, jnp.sum(p, axis=-1, keepdims=True)


def _xa_fwd(h, g, wq, wo, layer, kv, *, name):
    S, D = h.shape
    M = kv.shape[0]
    tm = _div_tile(S, 512, NORM_SLAB)
    slab = min(NORM_SLAB, tm)
    HD = XA_HEAD_DIM

    def body(h_ref, g_ref, wq_ref, wo_ref, k_ref, v_ref, h2_ref, u_ref, q_ref, o_ref):
        for r0 in range(0, tm, slab):
            rows = pl.ds(r0, slab)
            xv = h_ref[rows, :]
            r = lax.rsqrt(jnp.mean(xv * xv, axis=-1, keepdims=True) + RMS_EPS)
            u = (xv * r * g_ref[...]).astype(BF16)
            u_ref[rows, :] = u
            q_ref[rows, :] = jnp.dot(u, wq_ref[...], preferred_element_type=F32).astype(BF16)
            for hd in range(XA_HEADS):
                cols = pl.ds(HD * hd, HD)
                p, l = _xa_probs(q_ref[rows, cols], k_ref[:, cols])
                o = jnp.dot(p.astype(BF16), v_ref[:, cols], preferred_element_type=F32) / l
                o_ref[rows, cols] = o.astype(BF16)
            h2_ref[rows, :] = xv + jnp.dot(o_ref[rows, :], wo_ref[...], preferred_element_type=F32)

    row = pl.BlockSpec((tm, D), lambda i: (i, 0))
    weight = pl.BlockSpec((None, D, D), lambda i: (layer, 0, 0))
    act = jax.ShapeDtypeStruct((S, D), BF16)
    return pl.pallas_call(
        body, name=name, grid=(S // tm,),
        in_specs=[row, pl.BlockSpec((1, D), lambda i: (0, 0)), weight, weight,
                  pl.BlockSpec((M, D), lambda i: (0, 0)), pl.BlockSpec((M, D), lambda i: (0, 1))],
        out_specs=[row, row, row, row],
        out_shape=[jax.ShapeDtypeStruct((S, D), F32), act, act, act],
        compiler_params=_cparams(("parallel",)),
    )(h, g, wq, wo, kv, kv)


def _xa_bwd(dh, h, g, q, kv, wq, wo, layer, *, name):
    S, D = h.shape
    M = kv.shape[0]
    tm = _div_tile(S, 512, NORM_SLAB)
    slab = min(NORM_SLAB, tm)
    HD = XA_HEAD_DIM
    nt = (((1,), (1,)), ((), ()))
    tn = (((0,), (0,)), ((), ()))

    def body(dh_ref, h_ref, g_ref, q_ref, k_ref, v_ref, wq_ref, wo_ref, dx_ref, dg_ref, dq_ref, dkv_ref, do_s):
        i = pl.program_id(0)

        @pl.when(i == 0)
        def _():
            dkv_ref[...] = jnp.zeros(dkv_ref.shape, F32)

        part = jnp.zeros((1, D), F32)
        for r0 in range(0, tm, slab):
            rows = pl.ds(r0, slab)
            dhs = dh_ref[rows, :]
            do_s[rows, :] = lax.dot_general(dhs.astype(BF16), wo_ref[...], nt,
                                            preferred_element_type=F32).astype(BF16)
            for hd in range(XA_HEADS):
                cols = pl.ds(HD * hd, HD)
                qh, kh, vh, doh = q_ref[rows, cols], k_ref[:, cols], v_ref[:, cols], do_s[rows, cols]
                p, l = _xa_probs(qh, kh)
                pn = p * (1.0 / l)
                dp = lax.dot_general(doh, vh, nt, preferred_element_type=F32)
                delta = jnp.sum(pn * dp, axis=-1, keepdims=True)
                dsc = (pn * (dp - delta)).astype(BF16)
                dq_ref[rows, cols] = (jnp.dot(dsc, kh, preferred_element_type=F32) * XA_SCALE).astype(BF16)
                dkv_ref[:, cols] += lax.dot_general(dsc, qh, tn, preferred_element_type=F32) * XA_SCALE
                dkv_ref[:, pl.ds(D + HD * hd, HD)] += lax.dot_general(pn.astype(BF16), doh, tn,
                                                                      preferred_element_type=F32)
            du = lax.dot_general(dq_ref[rows, :], wq_ref[...], nt, preferred_element_type=F32)
            xv = h_ref[rows, :]
            r = lax.rsqrt(jnp.mean(xv * xv, axis=-1, keepdims=True) + RMS_EPS)
            xhat = xv * r
            part = part + jnp.sum(du * xhat, axis=0, keepdims=True)
            dxhat = du * g_ref[...]
            dx_ref[rows, :] = dhs + r * (dxhat - xhat * jnp.mean(dxhat * xhat, axis=-1, keepdims=True))

        @pl.when(i == 0)
        def _():
            dg_ref[...] = part

        @pl.when(i > 0)
        def _():
            dg_ref[...] += part

    row = pl.BlockSpec((tm, D), lambda i: (i, 0))
    vec = pl.BlockSpec((1, D), lambda i: (0, 0))
    weight = pl.BlockSpec((None, D, D), lambda i: (layer, 0, 0))
    return pl.pallas_call(
        body, name=name, grid=(S // tm,),
        in_specs=[row, row, vec, row, pl.BlockSpec((M, D), lambda i: (0, 0)), pl.BlockSpec((M, D), lambda i: (0, 1)),
                  weight, weight],
        out_specs=[row, vec, row, pl.BlockSpec((M, 2 * D), lambda i: (0, 0))],
        out_shape=[jax.ShapeDtypeStruct((S, D), F32), jax.ShapeDtypeStruct((1, D), F32),
                   jax.ShapeDtypeStruct((S, D), BF16), jax.ShapeDtypeStruct((M, 2 * D), F32)],
        scratch_shapes=[pltpu.VMEM((tm, D), BF16)],
        compiler_params=_cparams(("arbitrary",)),
    )(dh, h, g, q, kv, kv, wq, wo)


def _adam_math(w, g, m, v):
    m = ADAM_B1 * m + (1.0 - ADAM_B1) * g
    v = ADAM_B2 * v + (1.0 - ADAM_B2) * (g * g)
    m_hat = m / (1.0 - ADAM_B1 ** ADAM_STEP)
    v_hat = v / (1.0 - ADAM_B2 ** ADAM_STEP)
    delta = -ADAM_LR * (m_hat / (jnp.sqrt(v_hat) + ADAM_EPS) + ADAM_WD * w)
    return delta, m, v


def _adamw_layers(w, m, v, gsrc, *, name):
    L, A, B = w.shape
    tr = _div_tile(A, max(8, ADAM_TILE_ELEMS // B // 8 * 8))
    nt = A // tr
    flat = [a for srcs in gsrc for a in srcs]
    owner = [l for l, srcs in enumerate(gsrc) for _ in srcs]
    ng = len(flat)

    def body(*refs):
        w_ref, m_ref, v_ref = refs[:3]
        g_refs = refs[3:3 + ng]
        g_ref, d_ref, nm_ref, nv_ref = refs[3 + ng:]
        layer = pl.program_id(0)
        g = None
        for l in range(L):
            gl = None
            for a_ref, o in zip(g_refs, owner):
                if o == l:
                    gl = a_ref[...] if gl is None else gl + a_ref[...]
            g = gl if g is None else jnp.where(layer == l, gl, g)
        d, nm, nv = _adam_math(w_ref[...], g, m_ref[...], v_ref[...])
        g_ref[...] = g
        d_ref[...] = d
        nm_ref[...] = nm
        nv_ref[...] = nv

    def src_spec(o):
        return pl.BlockSpec((None, tr, B),
                            lambda l, i: (0, jnp.where(l == o, i, jnp.where(l > o, nt - 1, 0)), 0))

    spec = pl.BlockSpec((None, tr, B), lambda l, i: (l, i, 0))
    sds = jax.ShapeDtypeStruct((L, A, B), F32)
    return pl.pallas_call(
        body, name=name, grid=(L, nt), in_specs=[spec] * 3 + [src_spec(o) for o in owner], out_specs=[spec] * 4,
        out_shape=[sds] * 4, compiler_params=_cparams(("arbitrary", "arbitrary")),
    )(w, m, v, *flat)


def _adamw_small(ws, ms, vs, gparts, *, name):
    n = len(ws)
    R = gparts.shape[1]

    def body(*refs):
        w_refs, m_refs, v_refs = refs[:n], refs[n:2 * n], refs[2 * n:3 * n]
        gp_ref = refs[3 * n]
        outs = refs[3 * n + 1:7 * n + 1]
        packed = refs[7 * n + 1]
        g = gp_ref[0]
        for k in range(1, N_DEV):
            g = g + gp_ref[k]
        packed[...] = g
        row = 0
        for p in range(n):
            r, c = ws[p].shape
            per, lanes = max(c // 128, 1), min(c, 128)
            g_ref = outs[p]
            for i in range(r):
                for k in range(per):
                    g_ref[pl.ds(i, 1), pl.ds(128 * k, lanes)] = packed[pl.ds(row, 1), pl.ds(0, lanes)]
                    row += 1
            d, nm, nv = _adam_math(w_refs[p][...], g_ref[...], m_refs[p][...], v_refs[p][...])
            outs[n + p][...] = d
            outs[2 * n + p][...] = nm
            outs[3 * n + p][...] = nv

    shapes = [jax.ShapeDtypeStruct(a.shape, F32) for a in ws]
    res = pl.pallas_call(body, name=name, out_shape=shapes * 4, scratch_shapes=[pltpu.VMEM((R, 128), F32)],
                         compiler_params=_cparams())(*ws, *ms, *vs, gparts)
    return res[:n], res[n:2 * n], res[2 * n:3 * n], res[3 * n:]


ANY = pl.BlockSpec(memory_space=pl.ANY)


def _mesh_pos():
    return lax.axis_index("x"), lax.axis_index("y"), lax.axis_index("c")


def _other_chips(x, y):
    return [(1 - x, y), (x, 1 - y), (1 - x, 1 - y)]


LAYOUT = {'ffn1_w_gu': 'col', 'ffn1_w_down': 'stk', 'even_w_in': 'stk', 'even_w_out': 'stk', 'odd_w_in': 'col',
          'odd_w_out': 'stk', 'xa_wq': 'stk', 'xa_wkv': 'col', 'xa_wo': 'stk', 'ffn2_w_gu': 'col',
          'ffn2_w_down': 'stk', 'tiny': 'stk'}
COMM_NAMES = list(LAYOUT)
TINY_ROWS = 48


def _gathered_piece(ref, kind, L, A, h):
    if L == 2:
        return ref.at[h]
    rows = pl.ds(pl.multiple_of(h * (A // 2), 8), A // 2)
    return ref.at[0, rows] if kind == 'col' else ref.at[0, :, rows]


def _chip_part(piece, kind, B, s):
    if kind == 'col':
        return piece.at[:, pl.ds(pl.multiple_of(s * B, 128), B)]
    return piece.at[s]


def _place(shard, layer, kind, chip_idx, out_dtype, *, name, after=None):
    L, A, B = shard.shape
    ta = _div_tile(A, 256, 16)
    extra = [] if after is None else [after]

    def body(s_ref, x_ref, *rest):
        rest[-1][...] = x_ref[...].astype(out_dtype)

    if kind == 'col':
        shape = (1, A, N_CHIPS * B)
        out_spec = pl.BlockSpec((None, ta, B), lambda i, s: (0, i, s[0]))
    else:
        shape = (1, N_CHIPS, A, B)
        out_spec = pl.BlockSpec((None, None, ta, B), lambda i, s: (0, s[0], i, 0))
    grid_spec = pltpu.PrefetchScalarGridSpec(
        num_scalar_prefetch=1, grid=(A // ta,),
        in_specs=[pl.BlockSpec((None, ta, B), lambda i, s: (layer, i, 0))]
        + [pl.BlockSpec(memory_space=pl.ANY)] * len(extra), out_specs=out_spec)
    return pl.pallas_call(
        body, name=name, grid_spec=grid_spec, out_shape=jax.ShapeDtypeStruct(shape, out_dtype),
        compiler_params=_cparams(("parallel",)),
    )(chip_idx, shard, *extra)


HBM = pl.BlockSpec(memory_space=pltpu.HBM)
SEM = pl.BlockSpec(memory_space=pltpu.SEMAPHORE)
DATAFLOW = pltpu.SideEffectType.DATAFLOW_SIDE_EFFECTING


def _own_part_copies(refs, meta, send_sems, recv_sems):
    x, y, c = _mesh_pos()
    cps = []
    for k, (kind, L, A, B) in enumerate(meta):
        for j, (cx, cy) in enumerate(_other_chips(x, y)):
            part = _chip_part(refs[k].at[0], kind, B, 2 * x + y)
            cps.append(pltpu.make_async_remote_copy(
                src_ref=part, dst_ref=part, send_sem=send_sems.at[3 * k + j], recv_sem=recv_sems.at[3 * k + j],
                device_id=(cx, cy, c), device_id_type=MESH))
    return cps


def _half_part_copies(refs, meta, send_sems, recv_sems):
    x, y, c = _mesh_pos()
    cps = []
    for k, (kind, L, A, B) in enumerate(meta):
        for j, (cx, cy) in enumerate(_other_chips(x, y)):
            part = _chip_part(_gathered_piece(refs[k], kind, 1, A, c), kind, B, 2 * x + y)
            cps.append(pltpu.make_async_remote_copy(
                src_ref=part, dst_ref=part, send_sem=send_sems.at[3 * k + j], recv_sem=recv_sems.at[3 * k + j],
                device_id=(cx, cy, c), device_id_type=MESH))
    return cps


def _forward_copies(refs, meta, send_sems, recv_sems):
    x, y, c = _mesh_pos()
    cps = []
    for k, (kind, L, A, B) in enumerate(meta):
        for j, (cx, cy) in enumerate(_other_chips(x, y)):
            part = _chip_part(_gathered_piece(refs[k], kind, 1, A, c), kind, B, 2 * cx + cy)
            cps.append(pltpu.make_async_remote_copy(
                src_ref=part, dst_ref=part, send_sem=send_sems.at[3 * k + j], recv_sem=recv_sems.at[3 * k + j],
                device_id=(x, y, 1 - c), device_id_type=MESH))
    return cps


def _gather_start(fulls, meta, after, tag, copies=_own_part_copies):
    n = len(fulls)

    def body(*refs):
        send_sems, recv_sems = refs[n + 1], refs[n + 2]
        outs = refs[n + 3:2 * n + 3]
        token = refs[2 * n + 3]
        for cp in copies(outs, meta, send_sems, recv_sems):
            cp.start()
        token[...] = jnp.zeros_like(token)

    res = pl.pallas_call(
        body, name=f"ag_start_{tag}", in_specs=[HBM] * n + [pl.BlockSpec(memory_space=pl.ANY)],
        out_specs=(SEM, SEM) + (HBM,) * n + (pl.BlockSpec(memory_space=pltpu.VMEM),),
        out_shape=(pltpu.SemaphoreType.DMA((3 * n,)), pltpu.SemaphoreType.DMA((3 * n,)))
        + tuple(pltpu.HBM(f.shape, f.dtype) for f in fulls) + (jax.ShapeDtypeStruct((8, 128), F32),),
        input_output_aliases={k: 2 + k for k in range(n)},
        compiler_params=pltpu.CompilerParams(has_side_effects=DATAFLOW),
    )(*[pltpu.with_memory_space_constraint(f, pltpu.HBM) for f in fulls], after)
    return res[0], res[1], list(res[2:2 + n]), res[2 + n]


def _gather_wait(send_sems, recv_sems, fulls, meta, after, tag, copies=_own_part_copies):
    n = len(fulls)

    def body(*refs):
        f_refs = refs[:n]
        send_sems, recv_sems = refs[n], refs[n + 1]
        for cp in copies(f_refs, meta, send_sems, recv_sems):
            cp.wait_send()
            cp.wait_recv()

    return pl.pallas_call(
        body, name=f"ag_wait_{tag}", in_specs=[HBM] * n + [SEM, SEM, pl.BlockSpec(memory_space=pl.ANY)],
        out_specs=[HBM] * n, out_shape=[pltpu.HBM(f.shape, f.dtype) for f in fulls],
        input_output_aliases={k: k for k in range(n)},
        compiler_params=pltpu.CompilerParams(has_side_effects=DATAFLOW),
    )(*fulls, send_sems, recv_sems, after)


def _scatter_copies(g_refs, land_refs, meta, send_sems, recv_sems):
    x, y, c = _mesh_pos()
    cps = []
    for k, (kind, L, A, B) in enumerate(meta):
        for j, (cx, cy) in enumerate(_other_chips(x, y)):
            cps.append(pltpu.make_async_remote_copy(
                src_ref=_chip_part(g_refs[k].at[0], kind, B, 2 * cx + cy), dst_ref=land_refs[k].at[j],
                send_sem=send_sems.at[3 * k + j], recv_sem=recv_sems.at[3 * k + j], device_id=(cx, cy, c),
                device_id_type=MESH))
    return cps


def _scatter_start(gs, meta, after, tag):
    n = len(gs)

    def body(*refs):
        send_sems, recv_sems = refs[2 * n + 1], refs[2 * n + 2]
        g_out = refs[2 * n + 3:3 * n + 3]
        lands = refs[3 * n + 3:4 * n + 3]
        token = refs[4 * n + 3]
        for cp in _scatter_copies(g_out, lands, meta, send_sems, recv_sems):
            cp.start()
        token[...] = jnp.zeros_like(token)

    land_shapes = [(3, A, B) for kind, L, A, B in meta]
    lands = [pltpu.with_memory_space_constraint(lax.empty(s, g.dtype), pltpu.HBM) for s, g in zip(land_shapes, gs)]
    res = pl.pallas_call(
        body, name=f"rs_start_{tag}", in_specs=[HBM] * (2 * n) + [pl.BlockSpec(memory_space=pl.ANY)],
        out_specs=(SEM, SEM) + (HBM,) * (2 * n) + (pl.BlockSpec(memory_space=pltpu.VMEM),),
        out_shape=(pltpu.SemaphoreType.DMA((3 * n,)), pltpu.SemaphoreType.DMA((3 * n,)))
        + tuple(pltpu.HBM(g.shape, g.dtype) for g in gs)
        + tuple(pltpu.HBM(s, g.dtype) for s, g in zip(land_shapes, gs)) + (jax.ShapeDtypeStruct((8, 128), F32),),
        input_output_aliases={k: 2 + k for k in range(2 * n)},
        compiler_params=pltpu.CompilerParams(has_side_effects=DATAFLOW),
    )(*[pltpu.with_memory_space_constraint(g, pltpu.HBM) for g in gs], *lands, after)
    return res[0], res[1], list(res[2:2 + n]), list(res[2 + n:2 + 2 * n]), res[2 + 2 * n]


def _scatter_wait(send_sems, recv_sems, gs, lands, meta, after, tag):
    n = len(gs)

    def body(*refs):
        g_refs, land_refs = refs[:n], refs[n:2 * n]
        send_sems, recv_sems = refs[2 * n], refs[2 * n + 1]
        for cp in _scatter_copies(g_refs, land_refs, meta, send_sems, recv_sems):
            cp.wait_send()
            cp.wait_recv()

    both = list(gs) + list(lands)
    res = pl.pallas_call(
        body, name=f"rs_wait_{tag}", in_specs=[HBM] * (2 * n) + [SEM, SEM, pl.BlockSpec(memory_space=pl.ANY)],
        out_specs=[HBM] * (2 * n), out_shape=[pltpu.HBM(a.shape, a.dtype) for a in both],
        input_output_aliases={k: k for k in range(2 * n)},
        compiler_params=pltpu.CompilerParams(has_side_effects=DATAFLOW),
    )(*both, send_sems, recv_sems, after)
    return list(res[:n]), list(res[n:])


def _chip_sum_full(g, got, m, chip_idx, *, name):
    kind, L, A, B = m
    ta = _div_tile(A, 256, 16)

    def body(r_ref, a_ref, b_ref, o_ref):
        acc = a_ref[...].astype(F32)
        for j in range(3):
            acc = acc + b_ref[j].astype(F32)
        o_ref[...] = acc

    if kind == 'col':
        g_spec = pl.BlockSpec((None, ta, B), lambda i, r: (0, i, r[0]))
    else:
        g_spec = pl.BlockSpec((None, None, ta, B), lambda i, r: (0, r[0], i, 0))
    grid_spec = pltpu.PrefetchScalarGridSpec(
        num_scalar_prefetch=1, grid=(A // ta,),
        in_specs=[g_spec, pl.BlockSpec((3, ta, B), lambda i, r: (0, i, 0))],
        out_specs=pl.BlockSpec((None, ta, B), lambda i, r: (0, i, 0)))
    return pl.pallas_call(
        body, name=name, grid_spec=grid_spec, out_shape=jax.ShapeDtypeStruct((1, A, B), F32),
        compiler_params=_cparams(("parallel",)),
    )(chip_idx, g, got)


def _swap_copies(src_refs, land_refs, send_sems, recv_sems):
    x, y, c = _mesh_pos()
    return [pltpu.make_async_remote_copy(src_ref=s, dst_ref=d, send_sem=send_sems.at[k], recv_sem=recv_sems.at[k],
                                         device_id=(x, y, 1 - c), device_id_type=MESH)
            for k, (s, d) in enumerate(zip(src_refs, land_refs))]


def _swap_start(sums, after, tag):
    n = len(sums)

    def body(*refs):
        send_sems, recv_sems = refs[2 * n + 1], refs[2 * n + 2]
        s_out = refs[2 * n + 3:3 * n + 3]
        lands = refs[3 * n + 3:4 * n + 3]
        token = refs[4 * n + 3]
        for cp in _swap_copies(s_out, lands, send_sems, recv_sems):
            cp.start()
        token[...] = jnp.zeros_like(token)

    lands = [pltpu.with_memory_space_constraint(lax.empty(s.shape, s.dtype), pltpu.HBM) for s in sums]
    res = pl.pallas_call(
        body, name=f"rs_swap_start_{tag}", in_specs=[HBM] * (2 * n) + [pl.BlockSpec(memory_space=pl.ANY)],
        out_specs=(SEM, SEM) + (HBM,) * (2 * n) + (pl.BlockSpec(memory_space=pltpu.VMEM),),
        out_shape=(pltpu.SemaphoreType.DMA((n,)), pltpu.SemaphoreType.DMA((n,)))
        + tuple(pltpu.HBM(s.shape, s.dtype) for s in sums) * 2 + (jax.ShapeDtypeStruct((8, 128), F32),),
        input_output_aliases={k: 2 + k for k in range(2 * n)},
        compiler_params=pltpu.CompilerParams(has_side_effects=DATAFLOW),
    )(*[pltpu.with_memory_space_constraint(s, pltpu.HBM) for s in sums], *lands, after)
    return res[0], res[1], list(res[2:2 + n]), list(res[2 + n:2 + 2 * n]), res[2 + 2 * n]


def _swap_wait(send_sems, recv_sems, sums, lands, after, tag):
    n = len(sums)

    def body(*refs):
        send_sems, recv_sems = refs[2 * n], refs[2 * n + 1]
        for cp in _swap_copies(refs[:n], refs[n:2 * n], send_sems, recv_sems):
            cp.wait_send()
            cp.wait_recv()

    both = list(sums) + list(lands)
    res = pl.pallas_call(
        body, name=f"rs_swap_wait_{tag}", in_specs=[HBM] * (2 * n) + [SEM, SEM, pl.BlockSpec(memory_space=pl.ANY)],
        out_specs=[HBM] * (2 * n), out_shape=[pltpu.HBM(a.shape, a.dtype) for a in both],
        input_output_aliases={k: k for k in range(2 * n)},
        compiler_params=pltpu.CompilerParams(has_side_effects=DATAFLOW),
    )(*both, send_sems, recv_sems, after)
    return list(res[:n]), list(res[n:])


def _pair_swap(sums, small, *, tag):
    ns = len(sums)
    with_small = small is not None
    n_in = ns + with_small

    def body(*refs):
        sum_refs = refs[:ns]
        got_refs = refs[n_in:n_in + ns]
        send_sems, recv_sems = refs[2 * n_in], refs[2 * n_in + 1]
        x, y, c = _mesh_pos()
        cps = []
        for k in range(ns):
            cp = pltpu.make_async_remote_copy(
                src_ref=sum_refs[k], dst_ref=got_refs[k], send_sem=send_sems.at[k], recv_sem=recv_sems.at[k],
                device_id=(x, y, 1 - c), device_id_type=MESH)
            cp.start()
            cps.append(cp)
        if with_small:
            small_ref, sm_ref, local_sem = refs[ns], refs[n_in + ns], refs[2 * n_in + 2]
            me = 4 * x + 2 * y + c
            own = pltpu.make_async_copy(small_ref, sm_ref.at[me], local_sem)
            own.start()
            for r in range(1, N_DEV):
                fx, fy, fc = (r >> 2) & 1, (r >> 1) & 1, r & 1
                peer = (1 - x if fx else x, 1 - y if fy else y, 1 - c if fc else c)
                cp = pltpu.make_async_remote_copy(
                    src_ref=small_ref, dst_ref=sm_ref.at[me], send_sem=send_sems.at[ns + r],
                    recv_sem=recv_sems.at[ns + r], device_id=peer, device_id_type=MESH)
                cp.start()
                cps.append(cp)
        for cp in cps:
            cp.wait()
        if with_small:
            own.wait()

    out_shape = [jax.ShapeDtypeStruct(s.shape, s.dtype) for s in sums]
    scratch = [pltpu.SemaphoreType.DMA((ns + N_DEV,)), pltpu.SemaphoreType.DMA((ns + N_DEV,))]
    args = list(sums)
    if with_small:
        out_shape.append(jax.ShapeDtypeStruct((N_DEV,) + small.shape, F32))
        scratch.append(pltpu.SemaphoreType.DMA)
        args.append(small)
    res = pl.pallas_call(
        body, name=f"rs_pair_swap_{tag}", in_specs=[ANY] * n_in, out_specs=[ANY] * n_in, out_shape=out_shape,
        scratch_shapes=scratch,
    )(*args)
    return list(res[:ns]), (res[ns] if with_small else None)


def _tiny_pack(conv_a_w, sc_conv_w):
    lead = conv_a_w.shape[:-2]
    sc = sc_conv_w.reshape(lead + (2 * SC_WIDTH, 128))
    z = lambda r: jnp.zeros(lead + (r, 128), F32)
    return jnp.concatenate([conv_a_w, z(32 - CONV_A_WIDTH), sc, z(TINY_ROWS - 32 - 2 * SC_WIDTH)], axis=-2)


def _tiny_unpack(t):
    lead = t.shape[:-2]
    return t[..., :CONV_A_WIDTH, :], t[..., 32:32 + 2 * SC_WIDTH, :].reshape(lead + (SC_WIDTH, 256))


def _pack_small(d):
    parts = []
    for n in SMALL_NAMES:
        flat = d[n].astype(F32).reshape(-1)
        parts.append(jnp.pad(flat, (0, -flat.shape[0] % 128)))
    flat = jnp.concatenate(parts)
    return jnp.pad(flat, (0, -flat.shape[0] % 1024)).reshape(-1, 128)


def _ffn_fwd(h, g, W, n_gu, n_down, i, tag):
    h2, u, gate, up, a = _ffn_fwd_fused(h, g, W[n_gu][i], W[n_down][i], 0, name=f"{tag}_fwd")
    return h2, (h, u, gate, up, a)


def _ffn_bwd(dh, saved, g, W, n_gu, n_down, i, G, tag):
    h, u, gate, up, a = saved
    dh_in, dg, dgate, dup = _ffn_bwd_fused(dh, h, g, gate, up, W[n_gu][i], W[n_down][i], 0, name=f"{tag}_bwd")
    G[(n_down, i)] = _mm(a, dh, name=f"{tag}_b_wdown", ta=True, tm=1408, tn=1024, tk=1024, scale=0.5)
    tn = FFN_CHUNK
    half = _mm(u, dgate, name=f"{tag}_b_wg", ta=True, tm=1024, tn=tn, tk=2048, stack=(1, 0, None, 2 * D_FF))
    G[(n_gu, i)] = _mm(u, dup, name=f"{tag}_b_wu", ta=True, tm=1024, tn=tn, tk=2048, stack=(1, 0, half, 2 * D_FF),
                       n_map=lambda j: j + D_FF // tn)
    return dh_in, dg


def _xa_block_fwd(h, mem, g, gm, W, i, tag):
    mn = _rms_fwd(mem, gm, name=f"{tag}_mem_norm")
    kv = _mm(mn, W['xa_wkv'][i], b_layer=0, name=f"{tag}_kv", tm=256, tn=1024, tk=1024)
    h2, u, q, o = _xa_fwd(h, g, W['xa_wq'][i], W['xa_wo'][i], 0, kv, name=f"{tag}_fwd")
    return h2, (h, u, mn, q, kv, o)


def _xa_block_bwd(dh, saved, mem, g, gm, W, i, G, tag):
    h, u, mn, q, kv, o = saved
    dh_in, dg, dq, dkv = _xa_bwd(dh, h, g, q, kv, W['xa_wq'][i], W['xa_wo'][i], 0, name=f"{tag}_bwd")
    G[('xa_wo', i)] = _mm(o, dh, name=f"{tag}_b_wo", ta=True, tm=1024, tn=1024, tk=1024)
    G[('xa_wq', i)] = _mm(u, dq, name=f"{tag}_b_wq", ta=True, tm=1024, tn=1024, tk=1024)
    G[('xa_wkv', i)] = _mm(mn, dkv, name=f"{tag}_b_wkv", ta=True, tm=1024, tn=1024, tk=256)
    dmn = _mm(dkv, W['xa_wkv'][i], b_layer=0, name=f"{tag}_b_dmn", tb=True, out_dtype=F32, tm=256, tn=1024, tk=1024)
    _, dgm = _rms_bwd(mem, gm, dmn, None, name=f"{tag}_b_mem_norm")
    return dh_in, dg, dgm


def _pad_conv_w(w, rows):
    return jnp.pad(w.astype(F32), ((0, rows - w.shape[0]), (0, 0)))


def _even_fwd(h, g, W, conv_w, conv_b, ln_g, ln_b, sinks, tag):
    u, z = _norm_mm(h, g, W['even_w_in'][0], 0, name=f"{tag}_in")
    c, cat = _conv_a_fwd(z, conv_w, conv_b, ln_g, ln_b, name=f"{tag}_conv")
    kpad = jnp.pad(z[:, 1536:1664], ((WINDOW, 0), (0, 0)))
    vpad = jnp.pad(z[:, 1664:1792], ((WINDOW, 0), (0, 0)))
    cat = _swa_fwd(z, kpad, vpad, sinks, cat, name=f"{tag}_swa")
    h2 = _mm(cat, W['even_w_out'][0], b_layer=0, name=f"{tag}_out", out_dtype=F32, tm=1024, tn=1024, tk=1024, res=h)
    return h2, (h, u, z, c, kpad, vpad, cat)


def _even_bwd(dh, saved, g, W, conv_w, ln_g, ln_b, sinks, G, tag):
    h, u, z, c, kpad, vpad, cat = saved
    dcat = _mm(dh, W['even_w_out'][0], b_layer=0, name=f"{tag}_b_dcat", tb=True, tm=1024, tn=1024, tk=1024)
    G[('even_w_out', 0)] = _mm(cat, dh, name=f"{tag}_b_wout", ta=True, tm=1024, tn=1024, tk=1024)
    dz_a, small = _conv_a_bwd(z, c, dcat, conv_w, ln_g, ln_b, name=f"{tag}_b_conv")
    dq, dkp, dvp, dsinks = _swa_bwd(z, kpad, vpad, sinks, dcat, name=f"{tag}_b_swa")
    dz = jnp.concatenate([dz_a, dq, dkp[WINDOW:].astype(BF16), dvp[WINDOW:].astype(BF16)], axis=-1)
    G[('even_w_in', 0)] = _mm(u, dz, name=f"{tag}_b_win", ta=True, tm=1024, tn=1792, tk=1024)
    dh_in, dg = _mm_norm_bwd(dz, W['even_w_in'][0], 0, h, g, dh, name=f"{tag}_b_du", tk=1792)
    grads = dict(mix=dg, conv_a_w=small[:CONV_A_WIDTH], conv_a_b=small[32:33], conv_a_ln_g=small[33:34],
                 conv_a_ln_b=small[34:35], swa_sinks=dsinks[:, 0])
    return dh_in, grads


def _odd_fwd(h, g, W, conv_w, tag):
    u, z = _norm_mm(h, g, W['odd_w_in'][0], 0, name=f"{tag}_in")
    y, cc = _sconv_fwd(z, conv_w, name=f"{tag}_conv")
    h2 = _mm(y, W['odd_w_out'][0], b_layer=0, name=f"{tag}_out", out_dtype=F32, tm=1024, tn=1024, tk=1024, res=h)
    return h2, (h, u, z, y, cc)


def _odd_bwd(dh, saved, g, W, conv_w, G, tag):
    h, u, z, y, cc = saved
    dy = _mm(dh, W['odd_w_out'][0], b_layer=0, name=f"{tag}_b_dy", tb=True, tm=1024, tn=1024, tk=1024)
    G[('odd_w_out', 0)] = _mm(y, dh, name=f"{tag}_b_wout", ta=True, tm=1024, tn=1024, tk=1024)
    dz, dw = _sconv_bwd(z, cc, dy, conv_w, name=f"{tag}_b_conv")
    G[('odd_w_in', 0)] = _mm(u, dz, name=f"{tag}_b_win", ta=True, tm=1024, tn=1024, tk=1024)
    dh_in, dg = _mm_norm_bwd(dz, W['odd_w_in'][0], 0, h, g, dh, name=f"{tag}_b_du", tk=1024)
    return dh_in, dict(mix=dg, sc_conv_w=dw[:SC_WIDTH])


def _local_step(x, mem, tgt, W, need, token, ready, conv_a_w, sc_conv_w, P):
    row = lambda v: v.reshape(1, -1)
    conv_a_w = _pad_conv_w(conv_a_w, 32)
    sc_w = _pad_conv_w(sc_conv_w, 8)
    sinks = P['swa_sinks'][0]

    def arrive(stage, h):
        for n, ws in need(stage, h).items():
            W[n] = W.get(n, []) + ws

    h = x
    saved = []
    for i in range(2):
        t = f"l{i}"
        if i == 1:
            arrive('l1_ffn1', h)
        g1 = row(P['ffn1_norm'][i]) + (token if i == 0 else 0.0)
        h, s1 = _ffn_fwd(h, g1, W, 'ffn1_w_gu', 'ffn1_w_down', i, f"{t}_ffn1")
        arrive(f"{t}_mix", h)
        if i == 0:
            h, s2 = _even_fwd(h, row(P['mix_norm'][i]), W, conv_a_w, P['conv_a_b'], P['conv_a_ln_g'],
                              P['conv_a_ln_b'], sinks, f"{t}_even")
        else:
            h, s2 = _odd_fwd(h, row(P['mix_norm'][i]), W, sc_w, f"{t}_odd")
        h, s3 = _xa_block_fwd(h, mem, row(P['xa_norm'][i]), row(P['xa_mem_norm'][i]), W, i, f"{t}_xa")
        arrive(f"{t}_ffn2", h)
        h, s4 = _ffn_fwd(h, row(P['ffn2_norm'][i]), W, 'ffn2_w_gu', 'ffn2_w_down', i, f"{t}_ffn2")
        saved.append((s1, s2, s3, s4))

    loss, dh, d_final = _final_loss(h, row(P['final_norm']), tgt, name="final_loss")

    G = {}
    gp = {n: [None, None] for n in ('ffn1_norm', 'mix_norm', 'xa_norm', 'xa_mem_norm', 'ffn2_norm')}
    single = {}
    for i in (1, 0):
        t = f"l{i}"
        s1, s2, s3, s4 = saved[i]
        g4 = row(P['ffn2_norm'][i]) + (ready('l1', G) if i == 0 else 0.0)
        dh, gp['ffn2_norm'][i] = _ffn_bwd(dh, s4, g4, W, 'ffn2_w_gu', 'ffn2_w_down', i, G, f"{t}_ffn2")
        dh, gp['xa_norm'][i], gp['xa_mem_norm'][i] = _xa_block_bwd(
            dh, s3, mem, row(P['xa_norm'][i]), row(P['xa_mem_norm'][i]), W, i, G, f"{t}_xa")
        if i == 0:
            dh, g2 = _even_bwd(dh, s2, row(P['mix_norm'][i]), W, conv_a_w, P['conv_a_ln_g'], P['conv_a_ln_b'], sinks,
                               G, f"{t}_even")
        else:
            dh, g2 = _odd_bwd(dh, s2, row(P['mix_norm'][i]), W, sc_w, G, f"{t}_odd")
        gp['mix_norm'][i] = g2.pop('mix')
        single.update(g2)
        g1 = row(P['ffn1_norm'][i]) + (ready('l0_rest', G) if i == 0 else 0.0)
        dh, gp['ffn1_norm'][i] = _ffn_bwd(dh, s1, g1, W, 'ffn1_w_gu', 'ffn1_w_down', i, G, f"{t}_ffn1")

    small = {n: jnp.concatenate(v, axis=0) for n, v in gp.items()}
    small['conv_a_b'] = single['conv_a_b']
    small['conv_a_ln_g'] = single['conv_a_ln_g']
    small['conv_a_ln_b'] = single['conv_a_ln_b']
    small['swa_sinks'] = single['swa_sinks'][None]
    small['final_norm'] = d_final[0]
    small['conv_a_w'] = single['conv_a_w']
    small['sc_conv_w'] = single['sc_conv_w']
    return loss[0, 0], dh, G, small


def kernel(x, mem, ffn1_norm, ffn1_w_gu, ffn1_w_down, mix_norm, even_w_in, conv_a_w, conv_a_b, conv_a_ln_g, conv_a_ln_b, swa_sinks, even_w_out, odd_w_in, sc_conv_w, odd_w_out, xa_norm, xa_mem_norm, xa_wq, xa_wkv, xa_wo, ffn2_norm, ffn2_w_gu, ffn2_w_down, final_norm, loss_target, m_ffn1_norm, m_ffn1_w_gu, m_ffn1_w_down, m_mix_norm, m_even_w_in, m_conv_a_w, m_conv_a_b, m_conv_a_ln_g, m_conv_a_ln_b, m_swa_sinks, m_even_w_out, m_odd_w_in, m_sc_conv_w, m_odd_w_out, m_xa_norm, m_xa_mem_norm, m_xa_wq, m_xa_wkv, m_xa_wo, m_ffn2_norm, m_ffn2_w_gu, m_ffn2_w_down, m_final_norm, v_ffn1_norm, v_ffn1_w_gu, v_ffn1_w_down, v_mix_norm, v_even_w_in, v_conv_a_w, v_conv_a_b, v_conv_a_ln_g, v_conv_a_ln_b, v_swa_sinks, v_even_w_out, v_odd_w_in, v_sc_conv_w, v_odd_w_out, v_xa_norm, v_xa_mem_norm, v_xa_wq, v_xa_wkv, v_xa_wo, v_ffn2_norm, v_ffn2_w_gu, v_ffn2_w_down, v_final_norm):
    w = dict(zip(WEIGHT_NAMES, (ffn1_norm, ffn1_w_gu, ffn1_w_down, mix_norm, even_w_in, conv_a_w, conv_a_b, conv_a_ln_g, conv_a_ln_b, swa_sinks, even_w_out, odd_w_in, sc_conv_w, odd_w_out, xa_norm, xa_mem_norm, xa_wq, xa_wkv, xa_wo, ffn2_norm, ffn2_w_gu, ffn2_w_down, final_norm)))
    m = dict(zip(WEIGHT_NAMES, (m_ffn1_norm, m_ffn1_w_gu, m_ffn1_w_down, m_mix_norm, m_even_w_in, m_conv_a_w, m_conv_a_b, m_conv_a_ln_g, m_conv_a_ln_b, m_swa_sinks, m_even_w_out, m_odd_w_in, m_sc_conv_w, m_odd_w_out, m_xa_norm, m_xa_mem_norm, m_xa_wq, m_xa_wkv, m_xa_wo, m_ffn2_norm, m_ffn2_w_gu, m_ffn2_w_down, m_final_norm)))
    v = dict(zip(WEIGHT_NAMES, (v_ffn1_norm, v_ffn1_w_gu, v_ffn1_w_down, v_mix_norm, v_even_w_in, v_conv_a_w, v_conv_a_b, v_conv_a_ln_g, v_conv_a_ln_b, v_swa_sinks, v_even_w_out, v_odd_w_in, v_sc_conv_w, v_odd_w_out, v_xa_norm, v_xa_mem_norm, v_xa_wq, v_xa_wkv, v_xa_wo, v_ffn2_norm, v_ffn2_w_gu, v_ffn2_w_down, v_final_norm)))
    cx, cy, cc = lax.axis_index("x"), lax.axis_index("y"), lax.axis_index("c")
    chip_idx = (2 * cx + cy).astype(jnp.int32).reshape(1)

    shards = {n: w[n] for n in COMM_NAMES if n != 'tiny'}
    shards['tiny'] = _tiny_pack(conv_a_w, sc_conv_w)
    first =[('ffn1_w_gu', 0), ('ffn1_w_down', 0), ('tiny', 0)]
    stages = {
        'l0_mix': [('even_w_in', 0), ('even_w_out', 0), ('xa_wq', 0), ('xa_wkv', 0), ('xa_wo', 0)],
        'l0_ffn2': [('ffn2_w_gu', 0), ('ffn2_w_down', 0)],
        'l1_ffn1': [('ffn1_w_gu', 1), ('ffn1_w_down', 1)],
        'l1_mix': [('odd_w_in', 0), ('odd_w_out', 0), ('xa_wq', 1), ('xa_wkv', 1), ('xa_wo', 1)],
        'l1_ffn2': [('ffn2_w_gu', 1), ('ffn2_w_down', 1)],
    }
    grad_stages = {'l1': stages['l1_ffn1'] + stages['l1_mix'] + stages['l1_ffn2'],
                   'l0_rest': stages['l0_mix'] + stages['l0_ffn2'], 'l0_ffn1': first}

    def place(items, after=None):
        out = []
        for n, l in items:
            out.append(_place(shards[n], l, LAYOUT[n], chip_idx, F32 if n == 'tiny' else BF16,
                              name=f"place_{n}_{l}", after=after))
            after = out[-1] if after is not None else None
        return out

    def item_meta(items):
        return [(LAYOUT[n], 1) + shards[n].shape[1:] for n, l in items]

    def natural(items, arrays):
        out = {}
        for (n, l), a in zip(items, arrays):
            if n == 'tiny':
                continue
            if n == 'even_w_in':
                out[n] = [a.transpose(0, 2, 1, 3).reshape(1, D_MODEL, -1)]
            else:
                out[n] = [a if LAYOUT[n] == 'col' else a.reshape(1, N_CHIPS * a.shape[2], a.shape[3])]
        return out

    meta_first = item_meta(first)
    send, recv, in_flight, token = _gather_start(place(first), meta_first, chip_idx, "first_ici", _half_part_copies)
    placed, last = {}, token
    for stage, items in stages.items():
        placed[stage] = place(items, last)
        last = placed[stage][-1]
    landed = _gather_wait(send, recv, in_flight, meta_first, last, "first_ici", _half_part_copies)
    send, recv, in_flight, token = _gather_start(landed, meta_first, token, "first_d2d", _forward_copies)
    first_d2d = (send, recv, in_flight)
    gathers = {}
    for stage, items in stages.items():
        send, recv, in_flight, token = _gather_start(placed[stage], item_meta(items), token, stage)
        gathers[stage] = (send, recv, in_flight)
    first_full = _gather_wait(*first_d2d, meta_first, token, "first_d2d", _forward_copies)
    W = natural(first, first_full)
    ca, sc = _tiny_unpack(first_full[-1][0])
    conv_a_full = ca.transpose(1, 0, 2).reshape(CONV_A_WIDTH, CONV_A_CH)
    sc_full = sc.transpose(1, 0, 2).reshape(SC_WIDTH, SC_CH)

    def need(stage, h):
        send, recv, in_flight = gathers[stage]
        items = stages[stage]
        return natural(items, _gather_wait(send, recv, in_flight, item_meta(items), h, stage))

    def gathered_layout(G, item):
        n, l = item
        A, B = shards[n].shape[1:]
        g = G[item]
        if n == 'tiny':
            return g
        if n == 'even_w_in':
            return g.reshape(A, N_CHIPS, B).transpose(1, 0, 2)[None]
        return g.reshape(1, A, N_CHIPS * B) if LAYOUT[n] == 'col' else g.reshape(1, N_CHIPS, A, B)

    scatters, tokens = {}, {}

    def ready(stage, G):
        items = grad_stages[stage]
        send, recv, gs1, lands, tok = _scatter_start([gathered_layout(G, it) for it in items], item_meta(items),
                                                     chip_idx, stage)
        scatters[stage] = (send, recv, gs1, lands)
        tokens[stage] = tok
        return tok[:1, :1]

    loss_part, grad_x, G, g_small = _local_step(x[0], mem[0], loss_target[0], W, need, token[:1, :1], ready,
                                                conv_a_full, sc_full, {n: w[n] for n in SMALL_NAMES})
    G[('tiny', 0)] = _tiny_pack(g_small['conv_a_w'].reshape(CONV_A_WIDTH, N_CHIPS, 128).transpose(1, 0, 2),
                                g_small['sc_conv_w'].reshape(SC_WIDTH, N_CHIPS, 256).transpose(1, 0, 2))[None]
    loss = lax.psum(loss_part, ("x", "y", "c"))

    ready('l0_ffn1', G)
    started = tokens['l0_ffn1']

    def summed(stage, after):
        send, recv, gs1, lands = scatters[stage]
        items = grad_stages[stage]
        sent, landed = _scatter_wait(send, recv, gs1, lands, item_meta(items), after, stage)
        return items, [_chip_sum_full(g, r, m_, chip_idx, name=f"rs_chip_sum_{n}_{l}")
                       for (n, l), g, r, m_ in zip(items, sent, landed, item_meta(items))]

    def adamw(n, g1):
        if n == 'tiny':
            pk = lambda d: _tiny_pack(d['conv_a_w'], d['sc_conv_w'])
            res = [_tiny_unpack(a) for a in _adamw_layers(pk(w), pk(m), pk(v), [g1[('tiny', 0)]], name="adamw_tiny")]
            for k, nn in enumerate(('conv_a_w', 'sc_conv_w')):
                grads[nn], deltas[nn], new_m[nn], new_v[nn] = (r[k] for r in res)
        else:
            gsrc = [g1[(n, l)] for l in range(w[n].shape[0])]
            grads[n], deltas[n], new_m[n], new_v[n] = _adamw_layers(w[n], m[n], v[n], gsrc, name=f"adamw_{n}")

    grads, deltas, new_m, new_v = {}, {}, {}, {}
    sum_of = {}
    for stage in ('l1', 'l0_rest'):
        its, ss = summed(stage, started)
        sum_of.update(zip(its, ss))
    swap_groups = [['even_w_in', 'even_w_out', 'odd_w_in', 'odd_w_out', 'xa_wq', 'xa_wkv', 'xa_wo'],
                   ['ffn2_w_gu', 'ffn2_w_down'], ['ffn1_w_gu', 'ffn1_w_down']]
    swaps, after = [], started
    for gi, names in enumerate(swap_groups):
        its = [it for it in sum_of if it[0] in names]
        send, recv, own, lands, after = _swap_start([sum_of[it] for it in its], after, f"g{gi}")
        swaps.append((its, send, recv, own, lands))
    _, small_parts = _pair_swap([], _pack_small(g_small), tag="small")
    g1 = {}

    def swapped(gi, after):
        its, send, recv, own, lands = swaps[gi]
        mine, theirs = _swap_wait(send, recv, own, lands, after, f"g{gi}")
        g1.update({it: [a, b] for it, a, b in zip(its, mine, theirs)})

    for gi in (0, 1):
        swapped(gi, after)
        for n in swap_groups[gi]:
            adamw(n, g1)
        after = deltas[swap_groups[gi][-1]]

    swapped(2, after)
    its, ss = summed('l0_ffn1', after)
    sib, _ = _pair_swap(ss, None, tag="last")
    g1.update({it: [a, b] for it, a, b in zip(its, ss, sib)})
    for n in ('ffn1_w_gu', 'ffn1_w_down', 'tiny'):
        adamw(n, g1)
    rows2d = lambda d: [d[n].reshape(-1, d[n].shape[-1]) for n in SMALL_NAMES]
    for dst, arrs in zip((grads, deltas, new_m, new_v),
                         _adamw_small(rows2d(w), rows2d(m), rows2d(v), small_parts, name="adamw_small")):
        dst.update({n: a.reshape(w[n].shape) for n, a in zip(SMALL_NAMES, arrs)})

    return (loss, grad_x[None], *[grads[n] for n in WEIGHT_NAMES], *[deltas[n] for n in WEIGHT_NAMES],
            *[new_m[n] for n in WEIGHT_NAMES], *[new_v[n] for n in WEIGHT_NAMES])
```

```python
import jax
import jax.numpy as jnp
from jax import lax
from jax.experimental import pallas as pl
from jax.experimental.pallas import tpu as pltpu

F32 = jnp.float32
BF16 = jnp.bfloat16

D_MODEL = 1024
D_FF = 2816
CONV_A_CH = 512
CONV_A_WIDTH = 31
SWA_HEADS = 8
SWA_KV_HEADS = 2
SWA_GROUP = 4
HEAD_DIM = 64
WINDOW = 128
SC_CH = 1024
SC_WIDTH = 3
XA_HEADS = 4
XA_HEAD_DIM = 256
RMS_EPS = 1e-6
LN_EPS = 1e-5

ADAM_LR = 0.001
ADAM_B1 = 0.9
ADAM_B2 = 0.999
ADAM_EPS = 1e-08
ADAM_WD = 0.01
ADAM_STEP = 10
ADAM_TILE_ELEMS = 384 * 1024

N_CHIPS = 4
N_DEV = 8
NEG_BIG = -1e30
VMEM_LIMIT = 56 * 1024 * 1024
MESH = pl.DeviceIdType.MESH

INPUT_NAMES = ['x', 'mem', 'ffn1_norm', 'ffn1_w_gu', 'ffn1_w_down', 'mix_norm', 'even_w_in', 'conv_a_w', 'conv_a_b',
               'conv_a_ln_g', 'conv_a_ln_b', 'swa_sinks', 'even_w_out', 'odd_w_in', 'sc_conv_w', 'odd_w_out', 'xa_norm',
               'xa_mem_norm', 'xa_wq', 'xa_wkv', 'xa_wo', 'ffn2_norm', 'ffn2_w_gu', 'ffn2_w_down', 'final_norm']
WEIGHT_NAMES = INPUT_NAMES[2:]
BIG = [('ffn1_w_gu', 'col'), ('ffn1_w_down', 'row'), ('even_w_in', 'col'), ('conv_a_w', 'col'), ('even_w_out', 'row'),
       ('odd_w_in', 'col'), ('sc_conv_w', 'col'), ('odd_w_out', 'row'), ('xa_wq', 'row'), ('xa_wkv', 'col'),
       ('xa_wo', 'row'), ('ffn2_w_gu', 'col'), ('ffn2_w_down', 'row')]
BIG_NAMES = [n for n, _ in BIG]
SMALL_NAMES = [n for n in WEIGHT_NAMES if n not in BIG_NAMES]


def _cparams(sem=None, vmem=VMEM_LIMIT):
    kw = dict(vmem_limit_bytes=vmem)
    if sem is not None:
        kw['dimension_semantics'] = sem
    return pltpu.CompilerParams(**kw)


def _div_tile(n, want, align=8):
    if n <= want:
        return n
    t = (want // align) * align
    while t >= align:
        if n % t == 0:
            return t
        t -= align
    return n


def _mm(a, b, *, name, ta=False, tb=False, out_dtype=BF16, tm=512, tn=512, tk=512, res=None, scale=1.0,
        b_layer=None, stack=None, n_map=None):
    n_map = n_map or (lambda j: j)
    if ta:
        K, M = a.shape
    else:
        M, K = a.shape
    if tb:
        N, K2 = b.shape[-2:]
    else:
        K2, N = b.shape[-2:]
    assert K == K2, (a.shape, b.shape, ta, tb)
    tm = _div_tile(M, tm, 128 if ta else 16)
    tn = _div_tile(N, tn, 128)
    tk = _div_tile(K, tk, 16 if ta else 128)
    nk = K // tk
    a_spec = pl.BlockSpec((tk, tm), lambda i, j, k: (k, i)) if ta else pl.BlockSpec((tm, tk), lambda i, j, k: (i, k))
    if b_layer is None:
        b_spec = pl.BlockSpec((tn, tk), lambda i, j, k: (j, k)) if tb else pl.BlockSpec((tk, tn), lambda i, j, k: (k, j))
    elif tb:
        b_spec = pl.BlockSpec((None, tn, tk), lambda i, j, k: (b_layer, j, k))
    else:
        b_spec = pl.BlockSpec((None, tk, tn), lambda i, j, k: (b_layer, k, j))
    o_spec = pl.BlockSpec((tm, tn), lambda i, j, k: (i, j))
    out_shape = jax.ShapeDtypeStruct((M, N), out_dtype)
    out_spec = o_spec
    aliases = {}
    extra_specs, extra_args = [], ()
    if stack is not None:
        n_layers, layer, buf = stack[:3]
        n_total = stack[3] if len(stack) > 3 else N
        out_shape = jax.ShapeDtypeStruct((n_layers, M, n_total), out_dtype)
        out_spec = pl.BlockSpec((None, tm, tn), lambda i, j, k: (layer, i, n_map(j)))
        if buf is not None:
            extra_specs, extra_args = [pl.BlockSpec(memory_space=pl.ANY)], (buf,)
            aliases = {2 + (res is not None): 0}
    dims = (((0 if ta else 1,), (1 if tb else 0,)), ((), ()))
    has_res = res is not None
    n_extra = len(extra_args)

    def body(*refs):
        if n_extra:
            refs = refs[:2 + has_res] + refs[2 + has_res + n_extra:]
        if has_res:
            a_ref, b_ref, r_ref, o_ref, acc_ref = refs
        else:
            a_ref, b_ref, o_ref, acc_ref = refs
        k = pl.program_id(2)
        p = lax.dot_general(a_ref[...].astype(BF16), b_ref[...].astype(BF16), dims, preferred_element_type=F32)

        @pl.when(k == 0)
        def _():
            acc_ref[...] = p

        @pl.when(k > 0)
        def _():
            acc_ref[...] += p

        @pl.when(k == nk - 1)
        def _():
            r = acc_ref[...] * scale
            if has_res:
                r = r_ref[...] + r
            o_ref[...] = r.astype(out_dtype)

    in_specs = [a_spec, b_spec] + ([o_spec] if has_res else []) + extra_specs
    args = (a, b) + ((res,) if has_res else ()) + extra_args
    return pl.pallas_call(
        body, name=name, grid=(M // tm, N // tn, nk), in_specs=in_specs, out_specs=out_spec,
        out_shape=out_shape, input_output_aliases=aliases,
        scratch_shapes=[pltpu.VMEM((tm, tn), F32)],
        compiler_params=_cparams(("parallel", "parallel", "arbitrary")),
    )(*args)


def _rms_fwd(x, g, *, name):
    S, D = x.shape
    ts = _div_tile(S, 512)

    def body(x_ref, g_ref, o_ref):
        xv = x_ref[...]
        r = lax.rsqrt(jnp.mean(xv * xv, axis=-1, keepdims=True) + RMS_EPS)
        o_ref[...] = (xv * r * g_ref[...]).astype(BF16)

    return pl.pallas_call(
        body, name=name, grid=(S // ts,),
        in_specs=[pl.BlockSpec((ts, D), lambda i: (i, 0)), pl.BlockSpec((1, D), lambda i: (0, 0))],
        out_specs=pl.BlockSpec((ts, D), lambda i: (i, 0)),
        out_shape=jax.ShapeDtypeStruct((S, D), BF16),
        compiler_params=_cparams(("parallel",)),
    )(x, g)


NORM_SLAB = 256


def _norm_mm(h, g, w, layer, *, name):
    S, D = h.shape
    N = w.shape[-1]
    tm = _div_tile(S, 1024, NORM_SLAB)
    slab = min(NORM_SLAB, tm)

    def body(h_ref, g_ref, w_ref, u_ref, z_ref):
        for r0 in range(0, tm, slab):
            rows = pl.ds(r0, slab)
            xv = h_ref[rows, :]
            r = lax.rsqrt(jnp.mean(xv * xv, axis=-1, keepdims=True) + RMS_EPS)
            u = (xv * r * g_ref[...]).astype(BF16)
            u_ref[rows, :] = u
            z_ref[rows, :] = jnp.dot(u, w_ref[...], preferred_element_type=F32).astype(BF16)

    row = pl.BlockSpec((tm, D), lambda i: (i, 0))
    return pl.pallas_call(
        body, name=name, grid=(S // tm,),
        in_specs=[row, pl.BlockSpec((1, D), lambda i: (0, 0)), pl.BlockSpec((None, D, N), lambda i: (layer, 0, 0))],
        out_specs=[row, pl.BlockSpec((tm, N), lambda i: (i, 0))],
        out_shape=[jax.ShapeDtypeStruct((S, D), BF16), jax.ShapeDtypeStruct((S, N), BF16)],
        compiler_params=_cparams(("parallel",)),
    )(h, g, w)


def _mm_norm_bwd(dz, w, layer, h, g, dres, *, name, tk):
    S, K = dz.shape
    D = h.shape[1]
    tm = _div_tile(S, 1024, NORM_SLAB)
    slab = min(NORM_SLAB, tm)
    tk = _div_tile(K, tk, 128)
    nk = K // tk
    nt = (((1,), (1,)), ((), ()))

    def body(dz_ref, w_ref, h_ref, g_ref, dr_ref, dx_ref, dg_ref, acc):
        i = pl.program_id(0)
        k = pl.program_id(1)

        def norm_bwd(du_of):
            part = jnp.zeros((1, D), F32)
            for r0 in range(0, tm, slab):
                rows = pl.ds(r0, slab)
                du = du_of(rows)
                xv = h_ref[rows, :]
                r = lax.rsqrt(jnp.mean(xv * xv, axis=-1, keepdims=True) + RMS_EPS)
                xhat = xv * r
                part = part + jnp.sum(du * xhat, axis=0, keepdims=True)
                dxhat = du * g_ref[...]
                dx_ref[rows, :] = dr_ref[rows, :] + r * (
                    dxhat - xhat * jnp.mean(dxhat * xhat, axis=-1, keepdims=True))

            @pl.when(i == 0)
            def _():
                dg_ref[...] = part

            @pl.when(i > 0)
            def _():
                dg_ref[...] += part

        if nk == 1:
            norm_bwd(lambda rows: lax.dot_general(dz_ref[rows, :], w_ref[...], nt, preferred_element_type=F32))
        else:
            p = lax.dot_general(dz_ref[...], w_ref[...], nt, preferred_element_type=F32)

            @pl.when(k == 0)
            def _():
                acc[...] = p

            @pl.when(k > 0)
            def _():
                acc[...] += p

            @pl.when(k == nk - 1)
            def _():
                norm_bwd(lambda rows: acc[rows, :])

    row = pl.BlockSpec((tm, D), lambda i, k: (i, 0))
    vec = pl.BlockSpec((1, D), lambda i, k: (0, 0))
    return pl.pallas_call(
        body, name=name, grid=(S // tm, nk),
        in_specs=[pl.BlockSpec((tm, tk), lambda i, k: (i, k)), pl.BlockSpec((None, D, tk), lambda i, k: (layer, 0, k)),
                  row, vec, row],
        out_specs=[row, vec],
        out_shape=[jax.ShapeDtypeStruct((S, D), F32), jax.ShapeDtypeStruct((1, D), F32)],
        scratch_shapes=[pltpu.VMEM((tm, D), F32)],
        compiler_params=_cparams(("arbitrary", "arbitrary")),
    )(dz, w, h, g, dres)


def _rms_bwd(x, g, du, dres, *, name):
    S, D = x.shape
    ts = _div_tile(S, 512)
    has_res = dres is not None

    def body(*refs):
        if has_res:
            x_ref, g_ref, du_ref, dr_ref, dx_ref, dg_ref = refs
        else:
            x_ref, g_ref, du_ref, dg_ref = refs
        i = pl.program_id(0)
        xv = x_ref[...]
        duv = du_ref[...].astype(F32)
        r = lax.rsqrt(jnp.mean(xv * xv, axis=-1, keepdims=True) + RMS_EPS)
        xhat = xv * r
        part = jnp.sum(duv * xhat, axis=0, keepdims=True)

        @pl.when(i == 0)
        def _():
            dg_ref[...] = part

        @pl.when(i > 0)
        def _():
            dg_ref[...] += part

        if has_res:
            dxhat = duv * g_ref[...]
            dx = r * (dxhat - xhat * jnp.mean(dxhat * xhat, axis=-1, keepdims=True))
            dx_ref[...] = dr_ref[...] + dx

    row = pl.BlockSpec((ts, D), lambda i: (i, 0))
    vec = pl.BlockSpec((1, D), lambda i: (0, 0))
    if has_res:
        dx, dg = pl.pallas_call(
            body, name=name, grid=(S // ts,), in_specs=[row, vec, row, row], out_specs=[row, vec],
            out_shape=[jax.ShapeDtypeStruct((S, D), F32), jax.ShapeDtypeStruct((1, D), F32)],
            compiler_params=_cparams(("arbitrary",)),
        )(x, g, du, dres)
        return dx, dg
    dg = pl.pallas_call(
        body, name=name, grid=(S // ts,), in_specs=[row, vec, row], out_specs=vec,
        out_shape=jax.ShapeDtypeStruct((1, D), F32),
        compiler_params=_cparams(("arbitrary",)),
    )(x, g, du)
    return None, dg


def _final_loss(h, g, tgt, *, name):
    S, D = h.shape
    ts = _div_tile(S, 512)

    def body(h_ref, g_ref, t_ref, loss_ref, dh_ref, dg_ref):
        i = pl.program_id(0)
        xv = h_ref[...]
        gv = g_ref[...]
        r = lax.rsqrt(jnp.mean(xv * xv, axis=-1, keepdims=True) + RMS_EPS)
        xhat = xv * r
        err = xhat * gv - t_ref[...]
        lpart = 0.5 * jnp.sum(jnp.mean(err * err, axis=-1, keepdims=True), axis=0, keepdims=True)
        dy = err * (1.0 / D)
        gpart = jnp.sum(dy * xhat, axis=0, keepdims=True)

        @pl.when(i == 0)
        def _():
            loss_ref[...] = jnp.broadcast_to(lpart, loss_ref.shape)
            dg_ref[...] = gpart

        @pl.when(i > 0)
        def _():
            loss_ref[...] += jnp.broadcast_to(lpart, loss_ref.shape)
            dg_ref[...] += gpart

        dxhat = dy * gv
        dh_ref[...] = r * (dxhat - xhat * jnp.mean(dxhat * xhat, axis=-1, keepdims=True))

    row = pl.BlockSpec((ts, D), lambda i: (i, 0))
    vec = pl.BlockSpec((1, D), lambda i: (0, 0))
    return pl.pallas_call(
        body, name=name, grid=(S // ts,), in_specs=[row, vec, row],
        out_specs=[pl.BlockSpec((8, 128), lambda i: (0, 0)), row, vec],
        out_shape=[jax.ShapeDtypeStruct((8, 128), F32), jax.ShapeDtypeStruct((S, D), F32),
                   jax.ShapeDtypeStruct((1, D), F32)],
        compiler_params=_cparams(("arbitrary",)),
    )(h, g, tgt)


def _sigmoid(x):
    return 1.0 / (1.0 + jnp.exp(-x))


FFN_CHUNK = 1408
FFN_CHUNKS = D_FF // FFN_CHUNK
FFN_BWD_PIECE = 384
FFN_BWD_SLAB = 256


def _ffn_fwd_fused(h, g, w_gu, w_down, layer, *, name):
    S, D = h.shape
    tm = _div_tile(S, 512, 16)
    tf, nj = FFN_CHUNK, FFN_CHUNKS

    def body(h_ref, g_ref, wg_ref, wu_ref, wd_ref, h2_ref, u_ref, gate_ref, up_ref, a_ref, u_s, acc):
        j = pl.program_id(1)

        @pl.when(j == 0)
        def _():
            xv = h_ref[...]
            r = lax.rsqrt(jnp.mean(xv * xv, axis=-1, keepdims=True) + RMS_EPS)
            u = (xv * r * g_ref[...]).astype(BF16)
            u_s[...] = u
            u_ref[...] = u

        u = u_s[...]
        gate = jnp.dot(u, wg_ref[...], preferred_element_type=F32)
        up = jnp.dot(u, wu_ref[...], preferred_element_type=F32)
        gate_ref[...] = gate.astype(BF16)
        up_ref[...] = up.astype(BF16)
        a = (gate * _sigmoid(gate) * up).astype(BF16)
        a_ref[...] = a
        p = jnp.dot(a, wd_ref[...], preferred_element_type=F32)

        @pl.when(j == 0)
        def _():
            acc[...] = p

        @pl.when(j > 0)
        def _():
            acc[...] += p

        @pl.when(j == nj - 1)
        def _():
            h2_ref[...] = h_ref[...] + 0.5 * acc[...]

    row = pl.BlockSpec((tm, D), lambda i, j: (i, 0))
    chunk = pl.BlockSpec((tm, tf), lambda i, j: (i, j))
    hidden = jax.ShapeDtypeStruct((S, D_FF), BF16)
    return pl.pallas_call(
        body, name=name, grid=(S // tm, nj),
        in_specs=[row, pl.BlockSpec((1, D), lambda i, j: (0, 0)),
                  pl.BlockSpec((None, D, tf), lambda i, j: (layer, 0, j)),
                  pl.BlockSpec((None, D, tf), lambda i, j: (layer, 0, nj + j)),
                  pl.BlockSpec((None, tf, D), lambda i, j: (layer, j, 0))],
        out_specs=[row, row, chunk, chunk, chunk],
        out_shape=[jax.ShapeDtypeStruct((S, D), F32), jax.ShapeDtypeStruct((S, D), BF16), hidden, hidden, hidden],
        scratch_shapes=[pltpu.VMEM((tm, D), BF16), pltpu.VMEM((tm, D), F32)],
        compiler_params=_cparams(("parallel", "arbitrary")),
    )(h, g, w_gu, w_gu, w_down)


def _ffn_bwd_fused(dh, h, g, gate, up, w_gu, w_down, layer, *, name):
    S, D = h.shape
    tm = _div_tile(S, 512, FFN_BWD_SLAB)
    tf = FFN_CHUNK
    nj = D_FF // tf
    slab = min(FFN_BWD_SLAB, tm)
    nt = (((1,), (1,)), ((), ()))
    pieces = [(c0, min(FFN_BWD_PIECE, tf - c0)) for c0 in range(0, tf, FFN_BWD_PIECE)]

    def body(dh_ref, h_ref, g_ref, gate_ref, up_ref, wg_ref, wu_ref, wd_ref, dx_ref, dg_ref, dgate_ref, dup_ref,
             dy_s, acc):
        i = pl.program_id(0)
        j = pl.program_id(1)

        @pl.when(j == 0)
        def _():
            for r0 in range(0, tm, slab):
                rows = pl.ds(r0, slab)
                dy_s[rows, :] = (0.5 * dh_ref[rows, :]).astype(BF16)

        p = None
        for c0, cw in pieces:
            cols = pl.ds(c0, cw)
            da = lax.dot_general(dy_s[...], wd_ref[cols, :], nt, preferred_element_type=F32)
            gt = gate_ref[:, cols].astype(F32)
            sg = _sigmoid(gt)
            dgate = (da * up_ref[:, cols].astype(F32) * sg * (1.0 + gt * (1.0 - sg))).astype(BF16)
            dup = (da * gt * sg).astype(BF16)
            dgate_ref[:, cols] = dgate
            dup_ref[:, cols] = dup
            q = (lax.dot_general(dgate, wg_ref[:, cols], nt, preferred_element_type=F32)
                 + lax.dot_general(dup, wu_ref[:, cols], nt, preferred_element_type=F32))
            p = q if p is None else p + q

        @pl.when(j == 0)
        def _():
            acc[...] = p

        @pl.when(j > 0)
        def _():
            acc[...] += p

        @pl.when(j == nj - 1)
        def _():
            part = jnp.zeros((1, D), F32)
            for r0 in range(0, tm, slab):
                rows = pl.ds(r0, slab)
                xv = h_ref[rows, :]
                du = acc[rows, :]
                r = lax.rsqrt(jnp.mean(xv * xv, axis=-1, keepdims=True) + RMS_EPS)
                xhat = xv * r
                part = part + jnp.sum(du * xhat, axis=0, keepdims=True)
                dxhat = du * g_ref[...]
                dx_ref[rows, :] = dh_ref[rows, :] + r * (
                    dxhat - xhat * jnp.mean(dxhat * xhat, axis=-1, keepdims=True))

            @pl.when(i == 0)
            def _():
                dg_ref[...] = part

            @pl.when(i > 0)
            def _():
                dg_ref[...] += part

    row = pl.BlockSpec((tm, D), lambda i, j: (i, 0))
    vec = pl.BlockSpec((1, D), lambda i, j: (0, 0))
    chunk = pl.BlockSpec((tm, tf), lambda i, j: (i, j))
    hidden = jax.ShapeDtypeStruct((S, D_FF), BF16)
    return pl.pallas_call(
        body, name=name, grid=(S // tm, nj),
        in_specs=[row, row, vec, chunk, chunk,
                  pl.BlockSpec((None, D, tf), lambda i, j: (layer, 0, j)),
                  pl.BlockSpec((None, D, tf), lambda i, j: (layer, 0, nj + j)),
                  pl.BlockSpec((None, tf, D), lambda i, j: (layer, j, 0))],
        out_specs=[row, vec, chunk, chunk],
        out_shape=[jax.ShapeDtypeStruct((S, D), F32), jax.ShapeDtypeStruct((1, D), F32), hidden, hidden],
        scratch_shapes=[pltpu.VMEM((tm, D), BF16), pltpu.VMEM((tm, D), F32)],
        compiler_params=_cparams(("arbitrary", "arbitrary")),
    )(dh, h, g, gate, up, w_gu, w_gu, w_down)


CONV_HALO = 32
CONV_SUB_ROWS = 128


def _shifted_taps(win, shifted, ts):
    n = ts + CONV_HALO - 8
    for r in range(1, 8):
        shifted[r - 1] = win[pl.ds(r, n), :]

    def tap(start, rows, lanes):
        q, r = divmod(start, 8)
        if r == 0:
            return win[pl.ds(start, rows), lanes]
        return shifted[r - 1, pl.ds(8 * q, rows), lanes]

    return tap


def _conv_a_fwd(z, w, bias, ln_g, ln_b, *, name):
    S = z.shape[0]
    C = CONV_A_CH
    ts = _div_tile(S, 256, 32)

    def body(val_ref, gate_ref, w_ref, b_ref, g_ref, lb_ref, c_ref, act_ref, win, shifted):
        i = pl.program_id(0)

        @pl.when(i == 0)
        def _():
            win[pl.ds(0, CONV_HALO), :] = jnp.zeros((CONV_HALO, C), F32)

        @pl.when(i > 0)
        def _():
            win[pl.ds(0, CONV_HALO), :] = win[pl.ds(ts, CONV_HALO), :]

        a = val_ref[...].astype(F32) * _sigmoid(gate_ref[...].astype(F32))
        win[pl.ds(CONV_HALO, ts), :] = a
        tap = _shifted_taps(win, shifted, ts)
        rs = min(CONV_SUB_ROWS, ts)
        for cb in range(C // 128):
            lanes = pl.ds(128 * cb, 128)
            for rt in range(ts // rs):
                sub = jnp.broadcast_to(b_ref[:, lanes], (rs, 128))
                for k in range(CONV_A_WIDTH):
                    sub = sub + w_ref[pl.ds(k, 1), lanes] * tap(
                        CONV_HALO - (CONV_A_WIDTH - 1) + k + rs * rt, rs, lanes)
                c_ref[pl.ds(rs * rt, rs), lanes] = sub
        acc = c_ref[...]
        mu = jnp.mean(acc, axis=-1, keepdims=True)
        xc = acc - mu
        var = jnp.mean(xc * xc, axis=-1, keepdims=True)
        ln = xc * lax.rsqrt(var + LN_EPS) * g_ref[...] + lb_ref[...]
        act_ref[...] = (ln * _sigmoid(ln)).astype(BF16)

    row = lambda col: pl.BlockSpec((ts, C), lambda i, col=col: (i, col))
    vec = pl.BlockSpec((1, C), lambda i: (0, 0))
    return pl.pallas_call(
        body, name=name, grid=(S // ts,),
        in_specs=[row(0), row(1), pl.BlockSpec((32, C), lambda i: (0, 0)), vec, vec, vec],
        out_specs=[row(0), row(0)],
        out_shape=[jax.ShapeDtypeStruct((S, C), F32), jax.ShapeDtypeStruct((S, 2 * C), BF16)],
        scratch_shapes=[pltpu.VMEM((ts + CONV_HALO, C), F32), pltpu.VMEM((7, ts + CONV_HALO - 8, C), F32)],
        compiler_params=_cparams(("arbitrary",)),
    )(z, z, w, bias, ln_g, ln_b)


def _conv_a_bwd(z, c, dcat, w, ln_g, ln_b, *, name):
    S = z.shape[0]
    C = CONV_A_CH
    ts = _div_tile(S, 256, 32)
    n = S // ts

    def body(val_ref, gate_ref, c_ref, da_ref, w_ref, g_ref, lb_ref, dz_ref, small_ref, win, a_s, da_s, dw8,
             shifted):
        i = pl.program_id(0)

        @pl.when(i == 0)
        def _():
            win[pl.ds(ts, CONV_HALO), :] = jnp.zeros((CONV_HALO, C), F32)
            small_ref[...] = jnp.zeros(small_ref.shape, F32)
            dw8[...] = jnp.zeros(dw8.shape, F32)

        @pl.when(i > 0)
        def _():
            win[pl.ds(ts, CONV_HALO), :] = win[pl.ds(0, CONV_HALO), :]

        cv = c_ref[...]
        gv = g_ref[...]
        mu = jnp.mean(cv, axis=-1, keepdims=True)
        xc = cv - mu
        var = jnp.mean(xc * xc, axis=-1, keepdims=True)
        rstd = lax.rsqrt(var + LN_EPS)
        xhat = xc * rstd
        ln = xhat * gv + lb_ref[...]
        sg = _sigmoid(ln)
        dln = da_ref[...].astype(F32) * (sg * (1.0 + ln * (1.0 - sg)))
        small_ref[pl.ds(33, 1), :] += jnp.sum(dln * xhat, axis=0, keepdims=True)
        small_ref[pl.ds(34, 1), :] += jnp.sum(dln, axis=0, keepdims=True)
        dxhat = dln * gv
        dc = rstd * (dxhat - jnp.mean(dxhat, axis=-1, keepdims=True)
                     - xhat * jnp.mean(dxhat * xhat, axis=-1, keepdims=True))
        small_ref[pl.ds(32, 1), :] += jnp.sum(dc, axis=0, keepdims=True)
        win[pl.ds(0, ts), :] = dc

        val = val_ref[...].astype(F32)
        sgg = _sigmoid(gate_ref[...].astype(F32))
        a_s[...] = val * sgg
        tap = _shifted_taps(win, shifted, ts)
        rs = min(CONV_SUB_ROWS, ts)
        for cb in range(C // 128):
            lanes = pl.ds(128 * cb, 128)
            for rt in range(ts // rs):
                a_sub = a_s[pl.ds(rs * rt, rs), lanes]
                da = jnp.zeros((rs, 128), F32)
                for k in range(CONV_A_WIDTH):
                    sh = tap(CONV_A_WIDTH - 1 - k + rs * rt, rs, lanes)
                    da = da + w_ref[pl.ds(k, 1), lanes] * sh
                    prod = a_sub * sh
                    part = prod[0:8]
                    for r in range(1, rs // 8):
                        part = part + prod[8 * r:8 * r + 8]
                    dw8[pl.ds(8 * k, 8), lanes] += part
                da_s[pl.ds(rs * rt, rs), lanes] = da
        da = da_s[...]
        dz_ref[:, pl.ds(0, C)] = (da * sgg).astype(BF16)
        dz_ref[:, pl.ds(C, C)] = (da * val * sgg * (1.0 - sgg)).astype(BF16)

        @pl.when(i == n - 1)
        def _():
            for k in range(CONV_A_WIDTH):
                small_ref[pl.ds(k, 1), :] = jnp.sum(dw8[pl.ds(8 * k, 8), :], axis=0, keepdims=True)

    row = lambda col: pl.BlockSpec((ts, C), lambda i, col=col: (n - 1 - i, col))
    vec = pl.BlockSpec((1, C), lambda i: (0, 0))
    return pl.pallas_call(
        body, name=name, grid=(n,),
        in_specs=[row(0), row(1), row(0), row(0), pl.BlockSpec((32, C), lambda i: (0, 0)), vec, vec],
        out_specs=[pl.BlockSpec((ts, 2 * C), lambda i: (n - 1 - i, 0)), pl.BlockSpec((40, C), lambda i: (0, 0))],
        out_shape=[jax.ShapeDtypeStruct((S, 2 * C), BF16), jax.ShapeDtypeStruct((40, C), F32)],
        scratch_shapes=[pltpu.VMEM((ts + CONV_HALO, C), F32), pltpu.VMEM((ts, C), F32), pltpu.VMEM((ts, C), F32),
                        pltpu.VMEM((8 * 32, C), F32), pltpu.VMEM((7, ts + CONV_HALO - 8, C), F32)],
        compiler_params=_cparams(("arbitrary",)),
    )(z, z, c, dcat, w, ln_g, ln_b)


SC_HALO = 8


def _sconv_fwd(z, w, *, name):
    S = z.shape[0]
    C = SC_CH
    ts = _div_tile(S, 256, 16)

    def body(gb_ref, gc_ref, v_ref, w_ref, y_ref, cc_ref, win):
        i = pl.program_id(0)

        @pl.when(i == 0)
        def _():
            win[pl.ds(0, SC_HALO), :] = jnp.zeros((SC_HALO, C), F32)

        @pl.when(i > 0)
        def _():
            win[pl.ds(0, SC_HALO), :] = win[pl.ds(ts, SC_HALO), :]

        win[pl.ds(SC_HALO, ts), :] = gc_ref[...].astype(F32) * v_ref[...].astype(F32)
        acc = jnp.zeros((ts, C), F32)
        for k in range(SC_WIDTH):
            acc = acc + w_ref[pl.ds(k, 1), :] * win[pl.ds(SC_HALO - (SC_WIDTH - 1) + k, ts), :]
        cc_ref[...] = acc.astype(BF16)
        y_ref[...] = (gb_ref[...].astype(F32) * acc).astype(BF16)

    row = lambda col: pl.BlockSpec((ts, C), lambda i, col=col: (i, col))
    return pl.pallas_call(
        body, name=name, grid=(S // ts,),
        in_specs=[row(0), row(1), row(2), pl.BlockSpec((8, C), lambda i: (0, 0))],
        out_specs=[row(0), row(0)],
        out_shape=[jax.ShapeDtypeStruct((S, C), BF16), jax.ShapeDtypeStruct((S, C), BF16)],
        scratch_shapes=[pltpu.VMEM((ts + SC_HALO, C), F32)],
        compiler_params=_cparams(("arbitrary",)),
    )(z, z, z, w)


def _sconv_bwd(z, cc, dy, w, *, name):
    S = z.shape[0]
    C = SC_CH
    ts = _div_tile(S, 256, 16)
    n = S // ts

    def body(gb_ref, gc_ref, v_ref, cc_ref, dy_ref, w_ref, dz_ref, dw_ref, win):
        i = pl.program_id(0)

        @pl.when(i == 0)
        def _():
            win[pl.ds(ts, SC_HALO), :] = jnp.zeros((SC_HALO, C), F32)
            dw_ref[...] = jnp.zeros(dw_ref.shape, F32)

        @pl.when(i > 0)
        def _():
            win[pl.ds(ts, SC_HALO), :] = win[pl.ds(0, SC_HALO), :]

        dyv = dy_ref[...].astype(F32)
        gb = gb_ref[...].astype(F32)
        gc = gc_ref[...].astype(F32)
        val = v_ref[...].astype(F32)
        dz_ref[:, pl.ds(0, C)] = (dyv * cc_ref[...].astype(F32)).astype(BF16)
        win[pl.ds(0, ts), :] = dyv * gb
        cv = gc * val
        dcv = jnp.zeros((ts, C), F32)
        for k in range(SC_WIDTH):
            sh = win[pl.ds(SC_WIDTH - 1 - k, ts), :]
            dcv = dcv + w_ref[pl.ds(k, 1), :] * sh
            dw_ref[pl.ds(k, 1), :] += jnp.sum(cv * sh, axis=0, keepdims=True)
        dz_ref[:, pl.ds(C, C)] = (dcv * val).astype(BF16)
        dz_ref[:, pl.ds(2 * C, C)] = (dcv * gc).astype(BF16)

    row = lambda col: pl.BlockSpec((ts, C), lambda i, col=col: (n - 1 - i, col))
    return pl.pallas_call(
        body, name=name, grid=(n,),
        in_specs=[row(0), row(1), row(2), row(0), row(0), pl.BlockSpec((8, C), lambda i: (0, 0))],
        out_specs=[pl.BlockSpec((ts, 3 * C), lambda i: (n - 1 - i, 0)), pl.BlockSpec((8, C), lambda i: (0, 0))],
        out_shape=[jax.ShapeDtypeStruct((S, 3 * C), BF16), jax.ShapeDtypeStruct((8, C), F32)],
        scratch_shapes=[pltpu.VMEM((ts + SC_HALO, C), F32)],
        compiler_params=_cparams(("arbitrary",)),
    )(z, z, z, cc, dy, w)


SWA_Q_COL = 2
SWA_SLOPES = [2.0 ** (-8.0 * (h + 1) / SWA_HEADS) for h in range(SWA_HEADS)]
SWA_SCALE = HEAD_DIM ** -0.5


SWA_GROUP_ROWS = SWA_GROUP * WINDOW


def _swa_masks():
    shape = (SWA_GROUP_ROWS, 2 * WINDOW)
    ii = lax.broadcasted_iota(jnp.int32, shape, 0)
    jj = lax.broadcasted_iota(jnp.int32, shape, 1)
    dist = (ii & (WINDOW - 1)) + WINDOW - jj
    valid = (dist >= 0) & (dist < WINDOW)
    grp = lax.broadcasted_iota(jnp.int32, (SWA_GROUP_ROWS, 1), 0) // WINDOW
    return dist.astype(F32), valid, jj, grp


def _by_group(grp, vals):
    out = jnp.full(grp.shape, vals[SWA_GROUP - 1], F32)
    for g in range(SWA_GROUP - 2, -1, -1):
        out = jnp.where(grp == g, vals[g], out)
    return out


def _stack_heads(ref, rows, kv):
    return jnp.concatenate([ref[rows, pl.ds(HEAD_DIM * (kv * SWA_GROUP + g), HEAD_DIM)] for g in range(SWA_GROUP)],
                           axis=0)


def _swa_probs(qg, kk, sink, slope, distf, valid):
    s = lax.dot_general(qg, kk, (((1,), (1,)), ((), ())), preferred_element_type=F32) * SWA_SCALE
    s = s - slope * distf
    s = jnp.where(valid, s, NEG_BIG)
    m = jnp.maximum(jnp.max(s, axis=-1, keepdims=True), sink)
    p = jnp.exp(s - m)
    l = jnp.sum(p, axis=-1, keepdims=True) + jnp.exp(sink - m)
    return p, m, l


def _swa_fwd(z, kpad, vpad, sinks, cat, *, name):
    S = z.shape[0]
    tq = _div_tile(S, 256, 128)
    nblk = tq // WINDOW
    W = WINDOW

    def body(sink_ref, q_ref, k_ref, v_ref, cat_ref, o_ref):
        i = pl.program_id(0)
        distf, valid0, jj, grp = _swa_masks()
        for kv in range(SWA_KV_HEADS):
            heads = range(kv * SWA_GROUP, (kv + 1) * SWA_GROUP)
            sink = _by_group(grp, [sink_ref[h] for h in heads])
            slope = _by_group(grp, [SWA_SLOPES[h] for h in heads])
            for b in range(nblk):
                nb = i * nblk + b
                start = pl.multiple_of(nb * W, W)
                rows = pl.ds(W * b, W)
                valid = valid0 & ((jj >= W) | (nb > 0))
                kk = k_ref[pl.ds(start, 2 * W), pl.ds(HEAD_DIM * kv, HEAD_DIM)]
                vv = v_ref[pl.ds(start, 2 * W), pl.ds(HEAD_DIM * kv, HEAD_DIM)]
                p, m, l = _swa_probs(_stack_heads(q_ref, rows, kv), kk, sink, slope, distf, valid)
                o = (jnp.dot(p.astype(BF16), vv, preferred_element_type=F32) / l).astype(BF16)
                for g, h in enumerate(heads):
                    o_ref[rows, pl.ds(HEAD_DIM * h, HEAD_DIM)] = o[W * g:W * (g + 1)]

    full = pl.BlockSpec((S + W, 2 * HEAD_DIM), lambda i: (0, 0))
    return pl.pallas_call(
        body, name=name, grid=(S // tq,),
        in_specs=[pl.BlockSpec(memory_space=pltpu.SMEM), pl.BlockSpec((tq, 512), lambda i: (i, SWA_Q_COL)), full, full,
                  pl.BlockSpec(memory_space=pl.ANY)],
        out_specs=pl.BlockSpec((tq, 512), lambda i: (i, 1)),
        out_shape=jax.ShapeDtypeStruct((S, 1024), BF16), input_output_aliases={4: 0},
        compiler_params=_cparams(("parallel",)),
    )(sinks, z, kpad, vpad, cat)


def _swa_bwd(z, kpad, vpad, sinks, dcat, *, name):
    S = z.shape[0]
    tq = _div_tile(S, 256, 128)
    nblk = tq // WINDOW
    W = WINDOW

    def body(sink_ref, q_ref, k_ref, v_ref, do_ref, dq_ref, dk_ref, dv_ref, ds_ref):
        i = pl.program_id(0)

        @pl.when(i == 0)
        def _():
            dk_ref[...] = jnp.zeros(dk_ref.shape, F32)
            dv_ref[...] = jnp.zeros(dv_ref.shape, F32)
            ds_ref[...] = jnp.zeros(ds_ref.shape, F32)

        distf, valid0, jj, grp = _swa_masks()
        tn = (((0,), (0,)), ((), ()))
        for kv in range(SWA_KV_HEADS):
            heads = range(kv * SWA_GROUP, (kv + 1) * SWA_GROUP)
            sink = _by_group(grp, [sink_ref[h] for h in heads])
            slope = _by_group(grp, [SWA_SLOPES[h] for h in heads])
            for b in range(nblk):
                nb = i * nblk + b
                start = pl.multiple_of(nb * W, W)
                rows = pl.ds(W * b, W)
                valid = valid0 & ((jj >= W) | (nb > 0))
                kk = k_ref[pl.ds(start, 2 * W), pl.ds(HEAD_DIM * kv, HEAD_DIM)]
                vv = v_ref[pl.ds(start, 2 * W), pl.ds(HEAD_DIM * kv, HEAD_DIM)]
                qg = _stack_heads(q_ref, rows, kv)
                dog = _stack_heads(do_ref, rows, kv)
                p, m, l = _swa_probs(qg, kk, sink, slope, distf, valid)
                inv_l = 1.0 / l
                pn = p * inv_l
                dp = lax.dot_general(dog, vv, (((1,), (1,)), ((), ())), preferred_element_type=F32)
                delta = jnp.sum(pn * dp, axis=-1, keepdims=True)
                dsc = (pn * (dp - delta)).astype(BF16)
                dsink = jnp.exp(sink - m) * inv_l * delta
                dq = (jnp.dot(dsc, kk, preferred_element_type=F32) * SWA_SCALE).astype(BF16)
                for g, h in enumerate(heads):
                    ds_ref[pl.ds(h, 1), :] += jnp.broadcast_to(
                        -jnp.sum(dsink[W * g:W * (g + 1)], axis=0, keepdims=True), (1, 128))
                    dq_ref[rows, pl.ds(HEAD_DIM * h, HEAD_DIM)] = dq[W * g:W * (g + 1)]
                dk_ref[pl.ds(start, 2 * W), pl.ds(HEAD_DIM * kv, HEAD_DIM)] += lax.dot_general(
                    dsc, qg, tn, preferred_element_type=F32) * SWA_SCALE
                dv_ref[pl.ds(start, 2 * W), pl.ds(HEAD_DIM * kv, HEAD_DIM)] += lax.dot_general(
                    pn.astype(BF16), dog, tn, preferred_element_type=F32)

    full = pl.BlockSpec((S + W, 2 * HEAD_DIM), lambda i: (0, 0))
    return pl.pallas_call(
        body, name=name, grid=(S // tq,),
        in_specs=[pl.BlockSpec(memory_space=pltpu.SMEM), pl.BlockSpec((tq, 512), lambda i: (i, SWA_Q_COL)), full, full,
                  pl.BlockSpec((tq, 512), lambda i: (i, 1))],
        out_specs=[pl.BlockSpec((tq, 512), lambda i: (i, 0)), full, full, pl.BlockSpec((8, 128), lambda i: (0, 0))],
        out_shape=[jax.ShapeDtypeStruct((S, 512), BF16), jax.ShapeDtypeStruct((S + W, 2 * HEAD_DIM), F32),
                   jax.ShapeDtypeStruct((S + W, 2 * HEAD_DIM), F32), jax.ShapeDtypeStruct((8, 128), F32)],
        compiler_params=_cparams(("arbitrary",)),
    )(sinks, z, kpad, vpad, dcat)


XA_SCALE = XA_HEAD_DIM ** -0.5


def _xa_probs(qh, kh):
    s = lax.dot_general(qh, kh, (((1,), (1,)), ((), ())), preferred_element_type=F32) * XA_SCALE
    m = jnp.max(s, axis=-1, keepdims=True)
    p = jnp.exp(s - m)
    return p, jnp.sum(p, axis=-1, keepdims=True)


def _xa_fwd(h, g, wq, wo, layer, kv, *, name):
    S, D = h.shape
    M = kv.shape[0]
    tm = _div_tile(S, 512, NORM_SLAB)
    slab = min(NORM_SLAB, tm)
    HD = XA_HEAD_DIM

    def body(h_ref, g_ref, wq_ref, wo_ref, k_ref, v_ref, h2_ref, u_ref, q_ref, o_ref):
        for r0 in range(0, tm, slab):
            rows = pl.ds(r0, slab)
            xv = h_ref[rows, :]
            r = lax.rsqrt(jnp.mean(xv * xv, axis=-1, keepdims=True) + RMS_EPS)
            u = (xv * r * g_ref[...]).astype(BF16)
            u_ref[rows, :] = u
            q_ref[rows, :] = jnp.dot(u, wq_ref[...], preferred_element_type=F32).astype(BF16)
            for hd in range(XA_HEADS):
                cols = pl.ds(HD * hd, HD)
                p, l = _xa_probs(q_ref[rows, cols], k_ref[:, cols])
                o = jnp.dot(p.astype(BF16), v_ref[:, cols], preferred_element_type=F32) / l
                o_ref[rows, cols] = o.astype(BF16)
            h2_ref[rows, :] = xv + jnp.dot(o_ref[rows, :], wo_ref[...], preferred_element_type=F32)

    row = pl.BlockSpec((tm, D), lambda i: (i, 0))
    weight = pl.BlockSpec((None, D, D), lambda i: (layer, 0, 0))
    act = jax.ShapeDtypeStruct((S, D), BF16)
    return pl.pallas_call(
        body, name=name, grid=(S // tm,),
        in_specs=[row, pl.BlockSpec((1, D), lambda i: (0, 0)), weight, weight,
                  pl.BlockSpec((M, D), lambda i: (0, 0)), pl.BlockSpec((M, D), lambda i: (0, 1))],
        out_specs=[row, row, row, row],
        out_shape=[jax.ShapeDtypeStruct((S, D), F32), act, act, act],
        compiler_params=_cparams(("parallel",)),
    )(h, g, wq, wo, kv, kv)


def _xa_bwd(q, kv, do, *, name):
    S, D = q.shape
    M = kv.shape[0]
    ts = _div_tile(S, 512, 16)
    HD = XA_HEAD_DIM

    def body(q_ref, k_ref, v_ref, do_ref, dq_ref, dkv_ref):
        i = pl.program_id(0)

        @pl.when(i == 0)
        def _():
            dkv_ref[...] = jnp.zeros(dkv_ref.shape, F32)

        for h in range(XA_HEADS):
            qh = q_ref[:, pl.ds(HD * h, HD)]
            kh = k_ref[:, pl.ds(HD * h, HD)]
            vh = v_ref[:, pl.ds(HD * h, HD)]
            doh = do_ref[:, pl.ds(HD * h, HD)]
            p, l = _xa_probs(qh, kh)
            pn = p * (1.0 / l)
            dp = lax.dot_general(doh, vh, (((1,), (1,)), ((), ())), preferred_element_type=F32)
            delta = jnp.sum(pn * dp, axis=-1, keepdims=True)
            dsc = (pn * (dp - delta)).astype(BF16)
            dq_ref[:, pl.ds(HD * h, HD)] = (jnp.dot(dsc, kh, preferred_element_type=F32) * XA_SCALE).astype(BF16)
            dkv_ref[:, pl.ds(HD * h, HD)] += lax.dot_general(
                dsc, qh, (((0,), (0,)), ((), ())), preferred_element_type=F32) * XA_SCALE
            dkv_ref[:, pl.ds(D + HD * h, HD)] += lax.dot_general(
                pn.astype(BF16), doh, (((0,), (0,)), ((), ())), preferred_element_type=F32)

    row = pl.BlockSpec((ts, D), lambda i: (i, 0))
    return pl.pallas_call(
        body, name=name, grid=(S // ts,),
        in_specs=[row, pl.BlockSpec((M, D), lambda i: (0, 0)), pl.BlockSpec((M, D), lambda i: (0, 1)), row],
        out_specs=[row, pl.BlockSpec((M, 2 * D), lambda i: (0, 0))],
        out_shape=[jax.ShapeDtypeStruct((S, D), BF16), jax.ShapeDtypeStruct((M, 2 * D), F32)],
        compiler_params=_cparams(("arbitrary",)),
    )(q, kv, kv, do)


def _adam_math(w, g, m, v):
    m = ADAM_B1 * m + (1.0 - ADAM_B1) * g
    v = ADAM_B2 * v + (1.0 - ADAM_B2) * (g * g)
    m_hat = m / (1.0 - ADAM_B1 ** ADAM_STEP)
    v_hat = v / (1.0 - ADAM_B2 ** ADAM_STEP)
    delta = -ADAM_LR * (m_hat / (jnp.sqrt(v_hat) + ADAM_EPS) + ADAM_WD * w)
    return delta, m, v


def _adamw_layers(w, m, v, gsrc, *, name):
    L, A, B = w.shape
    tr = _div_tile(A, max(8, ADAM_TILE_ELEMS // B // 8 * 8))
    nt = A // tr
    flat = [a for srcs in gsrc for a in srcs]
    owner = [l for l, srcs in enumerate(gsrc) for _ in srcs]
    ng = len(flat)

    def body(*refs):
        w_ref, m_ref, v_ref = refs[:3]
        g_refs = refs[3:3 + ng]
        g_ref, d_ref, nm_ref, nv_ref = refs[3 + ng:]
        layer = pl.program_id(0)
        g = None
        for l in range(L):
            gl = None
            for a_ref, o in zip(g_refs, owner):
                if o == l:
                    gl = a_ref[...] if gl is None else gl + a_ref[...]
            g = gl if g is None else jnp.where(layer == l, gl, g)
        d, nm, nv = _adam_math(w_ref[...], g, m_ref[...], v_ref[...])
        g_ref[...] = g
        d_ref[...] = d
        nm_ref[...] = nm
        nv_ref[...] = nv

    def src_spec(o):
        return pl.BlockSpec((None, tr, B),
                            lambda l, i: (0, jnp.where(l == o, i, jnp.where(l > o, nt - 1, 0)), 0))

    spec = pl.BlockSpec((None, tr, B), lambda l, i: (l, i, 0))
    sds = jax.ShapeDtypeStruct((L, A, B), F32)
    return pl.pallas_call(
        body, name=name, grid=(L, nt), in_specs=[spec] * 3 + [src_spec(o) for o in owner], out_specs=[spec] * 4,
        out_shape=[sds] * 4, compiler_params=_cparams(("arbitrary", "arbitrary")),
    )(w, m, v, *flat)


def _adamw_small(ws, ms, vs, gparts, *, name):
    n = len(ws)
    R = gparts.shape[1]

    def body(*refs):
        w_refs, m_refs, v_refs = refs[:n], refs[n:2 * n], refs[2 * n:3 * n]
        gp_ref = refs[3 * n]
        outs = refs[3 * n + 1:7 * n + 1]
        packed = refs[7 * n + 1]
        g = gp_ref[0]
        for k in range(1, N_DEV):
            g = g + gp_ref[k]
        packed[...] = g
        row = 0
        for p in range(n):
            r, c = ws[p].shape
            per, lanes = max(c // 128, 1), min(c, 128)
            g_ref = outs[p]
            for i in range(r):
                for k in range(per):
                    g_ref[pl.ds(i, 1), pl.ds(128 * k, lanes)] = packed[pl.ds(row, 1), pl.ds(0, lanes)]
                    row += 1
            d, nm, nv = _adam_math(w_refs[p][...], g_ref[...], m_refs[p][...], v_refs[p][...])
            outs[n + p][...] = d
            outs[2 * n + p][...] = nm
            outs[3 * n + p][...] = nv

    shapes = [jax.ShapeDtypeStruct(a.shape, F32) for a in ws]
    res = pl.pallas_call(body, name=name, out_shape=shapes * 4, scratch_shapes=[pltpu.VMEM((R, 128), F32)],
                         compiler_params=_cparams())(*ws, *ms, *vs, gparts)
    return res[:n], res[n:2 * n], res[2 * n:3 * n], res[3 * n:]


ANY = pl.BlockSpec(memory_space=pl.ANY)


def _mesh_pos():
    return lax.axis_index("x"), lax.axis_index("y"), lax.axis_index("c")


def _other_chips(x, y):
    return [(1 - x, y), (x, 1 - y), (1 - x, 1 - y)]


LAYOUT = {'ffn1_w_gu': 'col', 'ffn1_w_down': 'stk', 'even_w_in': 'stk', 'even_w_out': 'stk', 'odd_w_in': 'col',
          'odd_w_out': 'stk', 'xa_wq': 'stk', 'xa_wkv': 'col', 'xa_wo': 'stk', 'ffn2_w_gu': 'col',
          'ffn2_w_down': 'stk', 'tiny': 'stk'}
COMM_NAMES = list(LAYOUT)
TINY_ROWS = 48


def _gathered_piece(ref, kind, L, A, h):
    if L == 2:
        return ref.at[h]
    rows = pl.ds(pl.multiple_of(h * (A // 2), 8), A // 2)
    return ref.at[0, rows] if kind == 'col' else ref.at[0, :, rows]


def _chip_part(piece, kind, B, s):
    if kind == 'col':
        return piece.at[:, pl.ds(pl.multiple_of(s * B, 128), B)]
    return piece.at[s]


def _place(shard, layer, kind, chip_idx, out_dtype, *, name, after=None):
    L, A, B = shard.shape
    ta = _div_tile(A, 256, 16)
    extra = [] if after is None else [after]

    def body(s_ref, x_ref, *rest):
        rest[-1][...] = x_ref[...].astype(out_dtype)

    if kind == 'col':
        shape = (1, A, N_CHIPS * B)
        out_spec = pl.BlockSpec((None, ta, B), lambda i, s: (0, i, s[0]))
    else:
        shape = (1, N_CHIPS, A, B)
        out_spec = pl.BlockSpec((None, None, ta, B), lambda i, s: (0, s[0], i, 0))
    grid_spec = pltpu.PrefetchScalarGridSpec(
        num_scalar_prefetch=1, grid=(A // ta,),
        in_specs=[pl.BlockSpec((None, ta, B), lambda i, s: (layer, i, 0))]
        + [pl.BlockSpec(memory_space=pl.ANY)] * len(extra), out_specs=out_spec)
    return pl.pallas_call(
        body, name=name, grid_spec=grid_spec, out_shape=jax.ShapeDtypeStruct(shape, out_dtype),
        compiler_params=_cparams(("parallel",)),
    )(chip_idx, shard, *extra)


HBM = pl.BlockSpec(memory_space=pltpu.HBM)
SEM = pl.BlockSpec(memory_space=pltpu.SEMAPHORE)
DATAFLOW = pltpu.SideEffectType.DATAFLOW_SIDE_EFFECTING


def _own_part_copies(refs, meta, send_sems, recv_sems):
    x, y, c = _mesh_pos()
    cps = []
    for k, (kind, L, A, B) in enumerate(meta):
        for j, (cx, cy) in enumerate(_other_chips(x, y)):
            part = _chip_part(refs[k].at[0], kind, B, 2 * x + y)
            cps.append(pltpu.make_async_remote_copy(
                src_ref=part, dst_ref=part, send_sem=send_sems.at[3 * k + j], recv_sem=recv_sems.at[3 * k + j],
                device_id=(cx, cy, c), device_id_type=MESH))
    return cps


def _half_part_copies(refs, meta, send_sems, recv_sems):
    x, y, c = _mesh_pos()
    cps = []
    for k, (kind, L, A, B) in enumerate(meta):
        for j, (cx, cy) in enumerate(_other_chips(x, y)):
            part = _chip_part(_gathered_piece(refs[k], kind, 1, A, c), kind, B, 2 * x + y)
            cps.append(pltpu.make_async_remote_copy(
                src_ref=part, dst_ref=part, send_sem=send_sems.at[3 * k + j], recv_sem=recv_sems.at[3 * k + j],
                device_id=(cx, cy, c), device_id_type=MESH))
    return cps


def _forward_copies(refs, meta, send_sems, recv_sems):
    x, y, c = _mesh_pos()
    cps = []
    for k, (kind, L, A, B) in enumerate(meta):
        for j, (cx, cy) in enumerate(_other_chips(x, y)):
            part = _chip_part(_gathered_piece(refs[k], kind, 1, A, c), kind, B, 2 * cx + cy)
            cps.append(pltpu.make_async_remote_copy(
                src_ref=part, dst_ref=part, send_sem=send_sems.at[3 * k + j], recv_sem=recv_sems.at[3 * k + j],
                device_id=(x, y, 1 - c), device_id_type=MESH))
    return cps


def _gather_start(fulls, meta, after, tag, copies=_own_part_copies):
    n = len(fulls)

    def body(*refs):
        send_sems, recv_sems = refs[n + 1], refs[n + 2]
        outs = refs[n + 3:2 * n + 3]
        token = refs[2 * n + 3]
        for cp in copies(outs, meta, send_sems, recv_sems):
            cp.start()
        token[...] = jnp.zeros_like(token)

    res = pl.pallas_call(
        body, name=f"ag_start_{tag}", in_specs=[HBM] * n + [pl.BlockSpec(memory_space=pl.ANY)],
        out_specs=(SEM, SEM) + (HBM,) * n + (pl.BlockSpec(memory_space=pltpu.VMEM),),
        out_shape=(pltpu.SemaphoreType.DMA((3 * n,)), pltpu.SemaphoreType.DMA((3 * n,)))
        + tuple(pltpu.HBM(f.shape, f.dtype) for f in fulls) + (jax.ShapeDtypeStruct((8, 128), F32),),
        input_output_aliases={k: 2 + k for k in range(n)},
        compiler_params=pltpu.CompilerParams(has_side_effects=DATAFLOW),
    )(*[pltpu.with_memory_space_constraint(f, pltpu.HBM) for f in fulls], after)
    return res[0], res[1], list(res[2:2 + n]), res[2 + n]


def _gather_wait(send_sems, recv_sems, fulls, meta, after, tag, copies=_own_part_copies):
    n = len(fulls)

    def body(*refs):
        f_refs = refs[:n]
        send_sems, recv_sems = refs[n], refs[n + 1]
        for cp in copies(f_refs, meta, send_sems, recv_sems):
            cp.wait_send()
            cp.wait_recv()

    return pl.pallas_call(
        body, name=f"ag_wait_{tag}", in_specs=[HBM] * n + [SEM, SEM, pl.BlockSpec(memory_space=pl.ANY)],
        out_specs=[HBM] * n, out_shape=[pltpu.HBM(f.shape, f.dtype) for f in fulls],
        input_output_aliases={k: k for k in range(n)},
        compiler_params=pltpu.CompilerParams(has_side_effects=DATAFLOW),
    )(*fulls, send_sems, recv_sems, after)


def _scatter_copies(g_refs, land_refs, meta, send_sems, recv_sems):
    x, y, c = _mesh_pos()
    cps = []
    for k, (kind, L, A, B) in enumerate(meta):
        for j, (cx, cy) in enumerate(_other_chips(x, y)):
            cps.append(pltpu.make_async_remote_copy(
                src_ref=_chip_part(g_refs[k].at[0], kind, B, 2 * cx + cy), dst_ref=land_refs[k].at[j],
                send_sem=send_sems.at[3 * k + j], recv_sem=recv_sems.at[3 * k + j], device_id=(cx, cy, c),
                device_id_type=MESH))
    return cps


def _scatter_start(gs, meta, after, tag):
    n = len(gs)

    def body(*refs):
        send_sems, recv_sems = refs[2 * n + 1], refs[2 * n + 2]
        g_out = refs[2 * n + 3:3 * n + 3]
        lands = refs[3 * n + 3:4 * n + 3]
        token = refs[4 * n + 3]
        for cp in _scatter_copies(g_out, lands, meta, send_sems, recv_sems):
            cp.start()
        token[...] = jnp.zeros_like(token)

    land_shapes = [(3, A, B) for kind, L, A, B in meta]
    lands = [pltpu.with_memory_space_constraint(lax.empty(s, g.dtype), pltpu.HBM) for s, g in zip(land_shapes, gs)]
    res = pl.pallas_call(
        body, name=f"rs_start_{tag}", in_specs=[HBM] * (2 * n) + [pl.BlockSpec(memory_space=pl.ANY)],
        out_specs=(SEM, SEM) + (HBM,) * (2 * n) + (pl.BlockSpec(memory_space=pltpu.VMEM),),
        out_shape=(pltpu.SemaphoreType.DMA((3 * n,)), pltpu.SemaphoreType.DMA((3 * n,)))
        + tuple(pltpu.HBM(g.shape, g.dtype) for g in gs)
        + tuple(pltpu.HBM(s, g.dtype) for s, g in zip(land_shapes, gs)) + (jax.ShapeDtypeStruct((8, 128), F32),),
        input_output_aliases={k: 2 + k for k in range(2 * n)},
        compiler_params=pltpu.CompilerParams(has_side_effects=DATAFLOW),
    )(*[pltpu.with_memory_space_constraint(g, pltpu.HBM) for g in gs], *lands, after)
    return res[0], res[1], list(res[2:2 + n]), list(res[2 + n:2 + 2 * n]), res[2 + 2 * n]


def _scatter_wait(send_sems, recv_sems, gs, lands, meta, after, tag):
    n = len(gs)

    def body(*refs):
        g_refs, land_refs = refs[:n], refs[n:2 * n]
        send_sems, recv_sems = refs[2 * n], refs[2 * n + 1]
        for cp in _scatter_copies(g_refs, land_refs, meta, send_sems, recv_sems):
            cp.wait_send()
            cp.wait_recv()

    both = list(gs) + list(lands)
    res = pl.pallas_call(
        body, name=f"rs_wait_{tag}", in_specs=[HBM] * (2 * n) + [SEM, SEM, pl.BlockSpec(memory_space=pl.ANY)],
        out_specs=[HBM] * (2 * n), out_shape=[pltpu.HBM(a.shape, a.dtype) for a in both],
        input_output_aliases={k: k for k in range(2 * n)},
        compiler_params=pltpu.CompilerParams(has_side_effects=DATAFLOW),
    )(*both, send_sems, recv_sems, after)
    return list(res[:n]), list(res[n:])


def _chip_sum_full(g, got, m, chip_idx, *, name):
    kind, L, A, B = m
    ta = _div_tile(A, 256, 16)

    def body(r_ref, a_ref, b_ref, o_ref):
        acc = a_ref[...].astype(F32)
        for j in range(3):
            acc = acc + b_ref[j].astype(F32)
        o_ref[...] = acc

    if kind == 'col':
        g_spec = pl.BlockSpec((None, ta, B), lambda i, r: (0, i, r[0]))
    else:
        g_spec = pl.BlockSpec((None, None, ta, B), lambda i, r: (0, r[0], i, 0))
    grid_spec = pltpu.PrefetchScalarGridSpec(
        num_scalar_prefetch=1, grid=(A // ta,),
        in_specs=[g_spec, pl.BlockSpec((3, ta, B), lambda i, r: (0, i, 0))],
        out_specs=pl.BlockSpec((None, ta, B), lambda i, r: (0, i, 0)))
    return pl.pallas_call(
        body, name=name, grid_spec=grid_spec, out_shape=jax.ShapeDtypeStruct((1, A, B), F32),
        compiler_params=_cparams(("parallel",)),
    )(chip_idx, g, got)


def _swap_copies(src_refs, land_refs, send_sems, recv_sems):
    x, y, c = _mesh_pos()
    return [pltpu.make_async_remote_copy(src_ref=s, dst_ref=d, send_sem=send_sems.at[k], recv_sem=recv_sems.at[k],
                                         device_id=(x, y, 1 - c), device_id_type=MESH)
            for k, (s, d) in enumerate(zip(src_refs, land_refs))]


def _swap_start(sums, after, tag):
    n = len(sums)

    def body(*refs):
        send_sems, recv_sems = refs[2 * n + 1], refs[2 * n + 2]
        s_out = refs[2 * n + 3:3 * n + 3]
        lands = refs[3 * n + 3:4 * n + 3]
        token = refs[4 * n + 3]
        for cp in _swap_copies(s_out, lands, send_sems, recv_sems):
            cp.start()
        token[...] = jnp.zeros_like(token)

    lands = [pltpu.with_memory_space_constraint(lax.empty(s.shape, s.dtype), pltpu.HBM) for s in sums]
    res = pl.pallas_call(
        body, name=f"rs_swap_start_{tag}", in_specs=[HBM] * (2 * n) + [pl.BlockSpec(memory_space=pl.ANY)],
        out_specs=(SEM, SEM) + (HBM,) * (2 * n) + (pl.BlockSpec(memory_space=pltpu.VMEM),),
        out_shape=(pltpu.SemaphoreType.DMA((n,)), pltpu.SemaphoreType.DMA((n,)))
        + tuple(pltpu.HBM(s.shape, s.dtype) for s in sums) * 2 + (jax.ShapeDtypeStruct((8, 128), F32),),
        input_output_aliases={k: 2 + k for k in range(2 * n)},
        compiler_params=pltpu.CompilerParams(has_side_effects=DATAFLOW),
    )(*[pltpu.with_memory_space_constraint(s, pltpu.HBM) for s in sums], *lands, after)
    return res[0], res[1], list(res[2:2 + n]), list(res[2 + n:2 + 2 * n]), res[2 + 2 * n]


def _swap_wait(send_sems, recv_sems, sums, lands, after, tag):
    n = len(sums)

    def body(*refs):
        send_sems, recv_sems = refs[2 * n], refs[2 * n + 1]
        for cp in _swap_copies(refs[:n], refs[n:2 * n], send_sems, recv_sems):
            cp.wait_send()
            cp.wait_recv()

    both = list(sums) + list(lands)
    res = pl.pallas_call(
        body, name=f"rs_swap_wait_{tag}", in_specs=[HBM] * (2 * n) + [SEM, SEM, pl.BlockSpec(memory_space=pl.ANY)],
        out_specs=[HBM] * (2 * n), out_shape=[pltpu.HBM(a.shape, a.dtype) for a in both],
        input_output_aliases={k: k for k in range(2 * n)},
        compiler_params=pltpu.CompilerParams(has_side_effects=DATAFLOW),
    )(*both, send_sems, recv_sems, after)
    return list(res[:n]), list(res[n:])


def _pair_swap(sums, small, *, tag):
    ns = len(sums)
    with_small = small is not None
    n_in = ns + with_small

    def body(*refs):
        sum_refs = refs[:ns]
        got_refs = refs[n_in:n_in + ns]
        send_sems, recv_sems = refs[2 * n_in], refs[2 * n_in + 1]
        x, y, c = _mesh_pos()
        cps = []
        for k in range(ns):
            cp = pltpu.make_async_remote_copy(
                src_ref=sum_refs[k], dst_ref=got_refs[k], send_sem=send_sems.at[k], recv_sem=recv_sems.at[k],
                device_id=(x, y, 1 - c), device_id_type=MESH)
            cp.start()
            cps.append(cp)
        if with_small:
            small_ref, sm_ref, local_sem = refs[ns], refs[n_in + ns], refs[2 * n_in + 2]
            me = 4 * x + 2 * y + c
            own = pltpu.make_async_copy(small_ref, sm_ref.at[me], local_sem)
            own.start()
            for r in range(1, N_DEV):
                fx, fy, fc = (r >> 2) & 1, (r >> 1) & 1, r & 1
                peer = (1 - x if fx else x, 1 - y if fy else y, 1 - c if fc else c)
                cp = pltpu.make_async_remote_copy(
                    src_ref=small_ref, dst_ref=sm_ref.at[me], send_sem=send_sems.at[ns + r],
                    recv_sem=recv_sems.at[ns + r], device_id=peer, device_id_type=MESH)
                cp.start()
                cps.append(cp)
        for cp in cps:
            cp.wait()
        if with_small:
            own.wait()

    out_shape = [jax.ShapeDtypeStruct(s.shape, s.dtype) for s in sums]
    scratch = [pltpu.SemaphoreType.DMA((ns + N_DEV,)), pltpu.SemaphoreType.DMA((ns + N_DEV,))]
    args = list(sums)
    if with_small:
        out_shape.append(jax.ShapeDtypeStruct((N_DEV,) + small.shape, F32))
        scratch.append(pltpu.SemaphoreType.DMA)
        args.append(small)
    res = pl.pallas_call(
        body, name=f"rs_pair_swap_{tag}", in_specs=[ANY] * n_in, out_specs=[ANY] * n_in, out_shape=out_shape,
        scratch_shapes=scratch,
    )(*args)
    return list(res[:ns]), (res[ns] if with_small else None)


def _tiny_pack(conv_a_w, sc_conv_w):
    lead = conv_a_w.shape[:-2]
    sc = sc_conv_w.reshape(lead + (2 * SC_WIDTH, 128))
    z = lambda r: jnp.zeros(lead + (r, 128), F32)
    return jnp.concatenate([conv_a_w, z(32 - CONV_A_WIDTH), sc, z(TINY_ROWS - 32 - 2 * SC_WIDTH)], axis=-2)


def _tiny_unpack(t):
    lead = t.shape[:-2]
    return t[..., :CONV_A_WIDTH, :], t[..., 32:32 + 2 * SC_WIDTH, :].reshape(lead + (SC_WIDTH, 256))


def _pack_small(d):
    parts = []
    for n in SMALL_NAMES:
        flat = d[n].astype(F32).reshape(-1)
        parts.append(jnp.pad(flat, (0, -flat.shape[0] % 128)))
    flat = jnp.concatenate(parts)
    return jnp.pad(flat, (0, -flat.shape[0] % 1024)).reshape(-1, 128)


def _ffn_fwd(h, g, W, n_gu, n_down, i, tag):
    h2, u, gate, up, a = _ffn_fwd_fused(h, g, W[n_gu][i], W[n_down][i], 0, name=f"{tag}_fwd")
    return h2, (h, u, gate, up, a)


def _ffn_bwd(dh, saved, g, W, n_gu, n_down, i, G, tag):
    h, u, gate, up, a = saved
    dh_in, dg, dgate, dup = _ffn_bwd_fused(dh, h, g, gate, up, W[n_gu][i], W[n_down][i], 0, name=f"{tag}_bwd")
    G[(n_down, i)] = _mm(a, dh, name=f"{tag}_b_wdown", ta=True, tm=1408, tn=1024, tk=1024, scale=0.5)
    tn = FFN_CHUNK
    half = _mm(u, dgate, name=f"{tag}_b_wg", ta=True, tm=1024, tn=tn, tk=2048, stack=(1, 0, None, 2 * D_FF))
    G[(n_gu, i)] = _mm(u, dup, name=f"{tag}_b_wu", ta=True, tm=1024, tn=tn, tk=2048, stack=(1, 0, half, 2 * D_FF),
                       n_map=lambda j: j + D_FF // tn)
    return dh_in, dg


def _xa_block_fwd(h, mem, g, gm, W, i, tag):
    mn = _rms_fwd(mem, gm, name=f"{tag}_mem_norm")
    kv = _mm(mn, W['xa_wkv'][i], b_layer=0, name=f"{tag}_kv", tm=256, tn=1024, tk=1024)
    h2, u, q, o = _xa_fwd(h, g, W['xa_wq'][i], W['xa_wo'][i], 0, kv, name=f"{tag}_fwd")
    return h2, (h, u, mn, q, kv, o)


def _xa_block_bwd(dh, saved, mem, g, gm, W, i, G, tag):
    h, u, mn, q, kv, o = saved
    do = _mm(dh, W['xa_wo'][i], b_layer=0, name=f"{tag}_b_do", tb=True, tm=1024, tn=1024, tk=1024)
    G[('xa_wo', i)] = _mm(o, dh, name=f"{tag}_b_wo", ta=True, tm=1024, tn=1024, tk=1024)
    dq, dkv = _xa_bwd(q, kv, do, name=f"{tag}_b_attn")
    G[('xa_wq', i)] = _mm(u, dq, name=f"{tag}_b_wq", ta=True, tm=1024, tn=1024, tk=1024)
    dh_in, dg = _mm_norm_bwd(dq, W['xa_wq'][i], 0, h, g, dh, name=f"{tag}_b_du", tk=1024)
    G[('xa_wkv', i)] = _mm(mn, dkv, name=f"{tag}_b_wkv", ta=True, tm=1024, tn=1024, tk=256)
    dmn = _mm(dkv, W['xa_wkv'][i], b_layer=0, name=f"{tag}_b_dmn", tb=True, out_dtype=F32, tm=256, tn=1024, tk=1024)
    _, dgm = _rms_bwd(mem, gm, dmn, None, name=f"{tag}_b_mem_norm")
    return dh_in, dg, dgm


def _pad_conv_w(w, rows):
    return jnp.pad(w.astype(F32), ((0, rows - w.shape[0]), (0, 0)))


def _even_fwd(h, g, W, conv_w, conv_b, ln_g, ln_b, sinks, tag):
    u, z = _norm_mm(h, g, W['even_w_in'][0], 0, name=f"{tag}_in")
    c, cat = _conv_a_fwd(z, conv_w, conv_b, ln_g, ln_b, name=f"{tag}_conv")
    kpad = jnp.pad(z[:, 1536:1664], ((WINDOW, 0), (0, 0)))
    vpad = jnp.pad(z[:, 1664:1792], ((WINDOW, 0), (0, 0)))
    cat = _swa_fwd(z, kpad, vpad, sinks, cat, name=f"{tag}_swa")
    h2 = _mm(cat, W['even_w_out'][0], b_layer=0, name=f"{tag}_out", out_dtype=F32, tm=1024, tn=1024, tk=1024, res=h)
    return h2, (h, u, z, c, kpad, vpad, cat)


def _even_bwd(dh, saved, g, W, conv_w, ln_g, ln_b, sinks, G, tag):
    h, u, z, c, kpad, vpad, cat = saved
    dcat = _mm(dh, W['even_w_out'][0], b_layer=0, name=f"{tag}_b_dcat", tb=True, tm=1024, tn=1024, tk=1024)
    G[('even_w_out', 0)] = _mm(cat, dh, name=f"{tag}_b_wout", ta=True, tm=1024, tn=1024, tk=1024)
    dz_a, small = _conv_a_bwd(z, c, dcat, conv_w, ln_g, ln_b, name=f"{tag}_b_conv")
    dq, dkp, dvp, dsinks = _swa_bwd(z, kpad, vpad, sinks, dcat, name=f"{tag}_b_swa")
    dz = jnp.concatenate([dz_a, dq, dkp[WINDOW:].astype(BF16), dvp[WINDOW:].astype(BF16)], axis=-1)
    G[('even_w_in', 0)] = _mm(u, dz, name=f"{tag}_b_win", ta=True, tm=1024, tn=1792, tk=1024)
    dh_in, dg = _mm_norm_bwd(dz, W['even_w_in'][0], 0, h, g, dh, name=f"{tag}_b_du", tk=1792)
    grads = dict(mix=dg, conv_a_w=small[:CONV_A_WIDTH], conv_a_b=small[32:33], conv_a_ln_g=small[33:34],
                 conv_a_ln_b=small[34:35], swa_sinks=dsinks[:, 0])
    return dh_in, grads


def _odd_fwd(h, g, W, conv_w, tag):
    u, z = _norm_mm(h, g, W['odd_w_in'][0], 0, name=f"{tag}_in")
    y, cc = _sconv_fwd(z, conv_w, name=f"{tag}_conv")
    h2 = _mm(y, W['odd_w_out'][0], b_layer=0, name=f"{tag}_out", out_dtype=F32, tm=1024, tn=1024, tk=1024, res=h)
    return h2, (h, u, z, y, cc)


def _odd_bwd(dh, saved, g, W, conv_w, G, tag):
    h, u, z, y, cc = saved
    dy = _mm(dh, W['odd_w_out'][0], b_layer=0, name=f"{tag}_b_dy", tb=True, tm=1024, tn=1024, tk=1024)
    G[('odd_w_out', 0)] = _mm(y, dh, name=f"{tag}_b_wout", ta=True, tm=1024, tn=1024, tk=1024)
    dz, dw = _sconv_bwd(z, cc, dy, conv_w, name=f"{tag}_b_conv")
    G[('odd_w_in', 0)] = _mm(u, dz, name=f"{tag}_b_win", ta=True, tm=1024, tn=1024, tk=1024)
    dh_in, dg = _mm_norm_bwd(dz, W['odd_w_in'][0], 0, h, g, dh, name=f"{tag}_b_du", tk=1024)
    return dh_in, dict(mix=dg, sc_conv_w=dw[:SC_WIDTH])


def _local_step(x, mem, tgt, W, need, token, ready, conv_a_w, sc_conv_w, P):
    row = lambda v: v.reshape(1, -1)
    conv_a_w = _pad_conv_w(conv_a_w, 32)
    sc_w = _pad_conv_w(sc_conv_w, 8)
    sinks = P['swa_sinks'][0]

    def arrive(stage, h):
        for n, ws in need(stage, h).items():
            W[n] = W.get(n, []) + ws

    h = x
    saved = []
    for i in range(2):
        t = f"l{i}"
        if i == 1:
            arrive('l1_ffn1', h)
        g1 = row(P['ffn1_norm'][i]) + (token if i == 0 else 0.0)
        h, s1 = _ffn_fwd(h, g1, W, 'ffn1_w_gu', 'ffn1_w_down', i, f"{t}_ffn1")
        arrive(f"{t}_mix", h)
        if i == 0:
            h, s2 = _even_fwd(h, row(P['mix_norm'][i]), W, conv_a_w, P['conv_a_b'], P['conv_a_ln_g'],
                              P['conv_a_ln_b'], sinks, f"{t}_even")
        else:
            h, s2 = _odd_fwd(h, row(P['mix_norm'][i]), W, sc_w, f"{t}_odd")
        h, s3 = _xa_block_fwd(h, mem, row(P['xa_norm'][i]), row(P['xa_mem_norm'][i]), W, i, f"{t}_xa")
        arrive(f"{t}_ffn2", h)
        h, s4 = _ffn_fwd(h, row(P['ffn2_norm'][i]), W, 'ffn2_w_gu', 'ffn2_w_down', i, f"{t}_ffn2")
        saved.append((s1, s2, s3, s4))

    loss, dh, d_final = _final_loss(h, row(P['final_norm']), tgt, name="final_loss")

    G = {}
    gp = {n: [None, None] for n in ('ffn1_norm', 'mix_norm', 'xa_norm', 'xa_mem_norm', 'ffn2_norm')}
    single = {}
    for i in (1, 0):
        t = f"l{i}"
        s1, s2, s3, s4 = saved[i]
        g4 = row(P['ffn2_norm'][i]) + (ready('l1', G) if i == 0 else 0.0)
        dh, gp['ffn2_norm'][i] = _ffn_bwd(dh, s4, g4, W, 'ffn2_w_gu', 'ffn2_w_down', i, G, f"{t}_ffn2")
        dh, gp['xa_norm'][i], gp['xa_mem_norm'][i] = _xa_block_bwd(
            dh, s3, mem, row(P['xa_norm'][i]), row(P['xa_mem_norm'][i]), W, i, G, f"{t}_xa")
        if i == 0:
            dh, g2 = _even_bwd(dh, s2, row(P['mix_norm'][i]), W, conv_a_w, P['conv_a_ln_g'], P['conv_a_ln_b'], sinks,
                               G, f"{t}_even")
        else:
            dh, g2 = _odd_bwd(dh, s2, row(P['mix_norm'][i]), W, sc_w, G, f"{t}_odd")
        gp['mix_norm'][i] = g2.pop('mix')
        single.update(g2)
        g1 = row(P['ffn1_norm'][i]) + (ready('l0_rest', G) if i == 0 else 0.0)
        dh, gp['ffn1_norm'][i] = _ffn_bwd(dh, s1, g1, W, 'ffn1_w_gu', 'ffn1_w_down', i, G, f"{t}_ffn1")

    small = {n: jnp.concatenate(v, axis=0) for n, v in gp.items()}
    small['conv_a_b'] = single['conv_a_b']
    small['conv_a_ln_g'] = single['conv_a_ln_g']
    small['conv_a_ln_b'] = single['conv_a_ln_b']
    small['swa_sinks'] = single['swa_sinks'][None]
    small['final_norm'] = d_final[0]
    small['conv_a_w'] = single['conv_a_w']
    small['sc_conv_w'] = single['sc_conv_w']
    return loss[0, 0], dh, G, small


def kernel(x, mem, ffn1_norm, ffn1_w_gu, ffn1_w_down, mix_norm, even_w_in, conv_a_w, conv_a_b, conv_a_ln_g, conv_a_ln_b, swa_sinks, even_w_out, odd_w_in, sc_conv_w, odd_w_out, xa_norm, xa_mem_norm, xa_wq, xa_wkv, xa_wo, ffn2_norm, ffn2_w_gu, ffn2_w_down, final_norm, loss_target, m_ffn1_norm, m_ffn1_w_gu, m_ffn1_w_down, m_mix_norm, m_even_w_in, m_conv_a_w, m_conv_a_b, m_conv_a_ln_g, m_conv_a_ln_b, m_swa_sinks, m_even_w_out, m_odd_w_in, m_sc_conv_w, m_odd_w_out, m_xa_norm, m_xa_mem_norm, m_xa_wq, m_xa_wkv, m_xa_wo, m_ffn2_norm, m_ffn2_w_gu, m_ffn2_w_down, m_final_norm, v_ffn1_norm, v_ffn1_w_gu, v_ffn1_w_down, v_mix_norm, v_even_w_in, v_conv_a_w, v_conv_a_b, v_conv_a_ln_g, v_conv_a_ln_b, v_swa_sinks, v_even_w_out, v_odd_w_in, v_sc_conv_w, v_odd_w_out, v_xa_norm, v_xa_mem_norm, v_xa_wq, v_xa_wkv, v_xa_wo, v_ffn2_norm, v_ffn2_w_gu, v_ffn2_w_down, v_final_norm):
    w = dict(zip(WEIGHT_NAMES, (ffn1_norm, ffn1_w_gu, ffn1_w_down, mix_norm, even_w_in, conv_a_w, conv_a_b, conv_a_ln_g, conv_a_ln_b, swa_sinks, even_w_out, odd_w_in, sc_conv_w, odd_w_out, xa_norm, xa_mem_norm, xa_wq, xa_wkv, xa_wo, ffn2_norm, ffn2_w_gu, ffn2_w_down, final_norm)))
    m = dict(zip(WEIGHT_NAMES, (m_ffn1_norm, m_ffn1_w_gu, m_ffn1_w_down, m_mix_norm, m_even_w_in, m_conv_a_w, m_conv_a_b, m_conv_a_ln_g, m_conv_a_ln_b, m_swa_sinks, m_even_w_out, m_odd_w_in, m_sc_conv_w, m_odd_w_out, m_xa_norm, m_xa_mem_norm, m_xa_wq, m_xa_wkv, m_xa_wo, m_ffn2_norm, m_ffn2_w_gu, m_ffn2_w_down, m_final_norm)))
    v = dict(zip(WEIGHT_NAMES, (v_ffn1_norm, v_ffn1_w_gu, v_ffn1_w_down, v_mix_norm, v_even_w_in, v_conv_a_w, v_conv_a_b, v_conv_a_ln_g, v_conv_a_ln_b, v_swa_sinks, v_even_w_out, v_odd_w_in, v_sc_conv_w, v_odd_w_out, v_xa_norm, v_xa_mem_norm, v_xa_wq, v_xa_wkv, v_xa_wo, v_ffn2_norm, v_ffn2_w_gu, v_ffn2_w_down, v_final_norm)))
    cx, cy, cc = lax.axis_index("x"), lax.axis_index("y"), lax.axis_index("c")
    chip_idx = (2 * cx + cy).astype(jnp.int32).reshape(1)

    shards = {n: w[n] for n in COMM_NAMES if n != 'tiny'}
    shards['tiny'] = _tiny_pack(conv_a_w, sc_conv_w)
    first =[('ffn1_w_gu', 0), ('ffn1_w_down', 0), ('tiny', 0)]
    stages = {
        'l0_mix': [('even_w_in', 0), ('even_w_out', 0), ('xa_wq', 0), ('xa_wkv', 0), ('xa_wo', 0)],
        'l0_ffn2': [('ffn2_w_gu', 0), ('ffn2_w_down', 0)],
        'l1_ffn1': [('ffn1_w_gu', 1), ('ffn1_w_down', 1)],
        'l1_mix': [('odd_w_in', 0), ('odd_w_out', 0), ('xa_wq', 1), ('xa_wkv', 1), ('xa_wo', 1)],
        'l1_ffn2': [('ffn2_w_gu', 1), ('ffn2_w_down', 1)],
    }
    grad_stages = {'l1': stages['l1_ffn1'] + stages['l1_mix'] + stages['l1_ffn2'],
                   'l0_rest': stages['l0_mix'] + stages['l0_ffn2'], 'l0_ffn1': first}

    def place(items, after=None):
        out = []
        for n, l in items:
            out.append(_place(shards[n], l, LAYOUT[n], chip_idx, F32 if n == 'tiny' else BF16,
                              name=f"place_{n}_{l}", after=after))
            after = out[-1] if after is not None else None
        return out

    def item_meta(items):
        return [(LAYOUT[n], 1) + shards[n].shape[1:] for n, l in items]

    def natural(items, arrays):
        out = {}
        for (n, l), a in zip(items, arrays):
            if n == 'tiny':
                continue
            if n == 'even_w_in':
                out[n] = [a.transpose(0, 2, 1, 3).reshape(1, D_MODEL, -1)]
            else:
                out[n] = [a if LAYOUT[n] == 'col' else a.reshape(1, N_CHIPS * a.shape[2], a.shape[3])]
        return out

    meta_first = item_meta(first)
    send, recv, in_flight, token = _gather_start(place(first), meta_first, chip_idx, "first_ici", _half_part_copies)
    placed, last = {}, token
    for stage, items in stages.items():
        placed[stage] = place(items, last)
        last = placed[stage][-1]
    landed = _gather_wait(send, recv, in_flight, meta_first, last, "first_ici", _half_part_copies)
    send, recv, in_flight, token = _gather_start(landed, meta_first, token, "first_d2d", _forward_copies)
    first_d2d = (send, recv, in_flight)
    gathers = {}
    for stage, items in stages.items():
        send, recv, in_flight, token = _gather_start(placed[stage], item_meta(items), token, stage)
        gathers[stage] = (send, recv, in_flight)
    first_full = _gather_wait(*first_d2d, meta_first, token, "first_d2d", _forward_copies)
    W = natural(first, first_full)
    ca, sc = _tiny_unpack(first_full[-1][0])
    conv_a_full = ca.transpose(1, 0, 2).reshape(CONV_A_WIDTH, CONV_A_CH)
    sc_full = sc.transpose(1, 0, 2).reshape(SC_WIDTH, SC_CH)

    def need(stage, h):
        send, recv, in_flight = gathers[stage]
        items = stages[stage]
        return natural(items, _gather_wait(send, recv, in_flight, item_meta(items), h, stage))

    def gathered_layout(G, item):
        n, l = item
        A, B = shards[n].shape[1:]
        g = G[item]
        if n == 'tiny':
            return g
        if n == 'even_w_in':
            return g.reshape(A, N_CHIPS, B).transpose(1, 0, 2)[None]
        return g.reshape(1, A, N_CHIPS * B) if LAYOUT[n] == 'col' else g.reshape(1, N_CHIPS, A, B)

    scatters, tokens = {}, {}

    def ready(stage, G):
        items = grad_stages[stage]
        send, recv, gs1, lands, tok = _scatter_start([gathered_layout(G, it) for it in items], item_meta(items),
                                                     chip_idx, stage)
        scatters[stage] = (send, recv, gs1, lands)
        tokens[stage] = tok
        return tok[:1, :1]

    loss_part, grad_x, G, g_small = _local_step(x[0], mem[0], loss_target[0], W, need, token[:1, :1], ready,
                                                conv_a_full, sc_full, {n: w[n] for n in SMALL_NAMES})
    G[('tiny', 0)] = _tiny_pack(g_small['conv_a_w'].reshape(CONV_A_WIDTH, N_CHIPS, 128).transpose(1, 0, 2),
                                g_small['sc_conv_w'].reshape(SC_WIDTH, N_CHIPS, 256).transpose(1, 0, 2))[None]
    loss = lax.psum(loss_part, ("x", "y", "c"))

    ready('l0_ffn1', G)
    started = tokens['l0_ffn1']

    def summed(stage, after):
        send, recv, gs1, lands = scatters[stage]
        items = grad_stages[stage]
        sent, landed = _scatter_wait(send, recv, gs1, lands, item_meta(items), after, stage)
        return items, [_chip_sum_full(g, r, m_, chip_idx, name=f"rs_chip_sum_{n}_{l}")
                       for (n, l), g, r, m_ in zip(items, sent, landed, item_meta(items))]

    def adamw(n, g1):
        if n == 'tiny':
            pk = lambda d: _tiny_pack(d['conv_a_w'], d['sc_conv_w'])
            res = [_tiny_unpack(a) for a in _adamw_layers(pk(w), pk(m), pk(v), [g1[('tiny', 0)]], name="adamw_tiny")]
            for k, nn in enumerate(('conv_a_w', 'sc_conv_w')):
                grads[nn], deltas[nn], new_m[nn], new_v[nn] = (r[k] for r in res)
        else:
            gsrc = [g1[(n, l)] for l in range(w[n].shape[0])]
            grads[n], deltas[n], new_m[n], new_v[n] = _adamw_layers(w[n], m[n], v[n], gsrc, name=f"adamw_{n}")

    grads, deltas, new_m, new_v = {}, {}, {}, {}
    sum_of = {}
    for stage in ('l1', 'l0_rest'):
        its, ss = summed(stage, started)
        sum_of.update(zip(its, ss))
    swap_groups = [['even_w_in', 'even_w_out', 'odd_w_in', 'odd_w_out', 'xa_wq', 'xa_wkv', 'xa_wo'],
                   ['ffn2_w_gu', 'ffn2_w_down'], ['ffn1_w_gu', 'ffn1_w_down']]
    swaps, after = [], started
    for gi, names in enumerate(swap_groups):
        its = [it for it in sum_of if it[0] in names]
        send, recv, own, lands, after = _swap_start([sum_of[it] for it in its], after, f"g{gi}")
        swaps.append((its, send, recv, own, lands))
    _, small_parts = _pair_swap([], _pack_small(g_small), tag="small")
    g1 = {}

    def swapped(gi, after):
        its, send, recv, own, lands = swaps[gi]
        mine, theirs = _swap_wait(send, recv, own, lands, after, f"g{gi}")
        g1.update({it: [a, b] for it, a, b in zip(its, mine, theirs)})

    for gi in (0, 1):
        swapped(gi, after)
        for n in swap_groups[gi]:
            adamw(n, g1)
        after = deltas[swap_groups[gi][-1]]

    swapped(2, after)
    its, ss = summed('l0_ffn1', after)
    sib, _ = _pair_swap(ss, None, tag="last")
    g1.update({it: [a, b] for it, a, b in zip(its, ss, sib)})
    for n in ('ffn1_w_gu', 'ffn1_w_down', 'tiny'):
        adamw(n, g1)
    rows2d = lambda d: [d[n].reshape(-1, d[n].shape[-1]) for n in SMALL_NAMES]
    for dst, arrs in zip((grads, deltas, new_m, new_v),
                         _adamw_small(rows2d(w), rows2d(m), rows2d(v), small_parts, name="adamw_small")):
        dst.update({n: a.reshape(w[n].shape) for n, a in zip(SMALL_NAMES, arrs)})

    return (loss, grad_x[None], *[grads[n] for n in WEIGHT_NAMES], *[deltas[n] for n in WEIGHT_NAMES],
            *[new_m[n] for n in WEIGHT_NAMES], *[new_v[n] for n in WEIGHT_NAMES])
```

```python
import jax
import jax.numpy as jnp
from jax import lax
from jax.experimental import pallas as pl
from jax.experimental.pallas import tpu as pltpu

F32 = jnp.float32
BF16 = jnp.bfloat16

D_MODEL = 1024
D_FF = 2816
CONV_A_CH = 512
CONV_A_WIDTH = 31
SWA_HEADS = 8
SWA_KV_HEADS = 2
SWA_GROUP = 4
HEAD_DIM = 64
WINDOW = 128
SC_CH = 1024
SC_WIDTH = 3
XA_HEADS = 4
XA_HEAD_DIM = 256
RMS_EPS = 1e-6
LN_EPS = 1e-5

ADAM_LR = 0.001
ADAM_B1 = 0.9
ADAM_B2 = 0.999
ADAM_EPS = 1e-08
ADAM_WD = 0.01
ADAM_STEP = 10
ADAM_TILE_ELEMS = 384 * 1024

N_CHIPS = 4
N_DEV = 8
NEG_BIG = -1e30
VMEM_LIMIT = 56 * 1024 * 1024
MESH = pl.DeviceIdType.MESH

INPUT_NAMES = ['x', 'mem', 'ffn1_norm', 'ffn1_w_gu', 'ffn1_w_down', 'mix_norm', 'even_w_in', 'conv_a_w', 'conv_a_b',
               'conv_a_ln_g', 'conv_a_ln_b', 'swa_sinks', 'even_w_out', 'odd_w_in', 'sc_conv_w', 'odd_w_out', 'xa_norm',
               'xa_mem_norm', 'xa_wq', 'xa_wkv', 'xa_wo', 'ffn2_norm', 'ffn2_w_gu', 'ffn2_w_down', 'final_norm']
WEIGHT_NAMES = INPUT_NAMES[2:]
BIG = [('ffn1_w_gu', 'col'), ('ffn1_w_down', 'row'), ('even_w_in', 'col'), ('conv_a_w', 'col'), ('even_w_out', 'row'),
       ('odd_w_in', 'col'), ('sc_conv_w', 'col'), ('odd_w_out', 'row'), ('xa_wq', 'row'), ('xa_wkv', 'col'),
       ('xa_wo', 'row'), ('ffn2_w_gu', 'col'), ('ffn2_w_down', 'row')]
BIG_NAMES = [n for n, _ in BIG]
SMALL_NAMES = [n for n in WEIGHT_NAMES if n not in BIG_NAMES]


def _cparams(sem=None, vmem=VMEM_LIMIT):
    kw = dict(vmem_limit_bytes=vmem)
    if sem is not None:
        kw['dimension_semantics'] = sem
    return pltpu.CompilerParams(**kw)


def _div_tile(n, want, align=8):
    if n <= want:
        return n
    t = (want // align) * align
    while t >= align:
        if n % t == 0:
            return t
        t -= align
    return n


def _mm(a, b, *, name, ta=False, tb=False, out_dtype=BF16, tm=512, tn=512, tk=512, res=None, scale=1.0,
        b_layer=None, stack=None, n_map=None):
    n_map = n_map or (lambda j: j)
    if ta:
        K, M = a.shape
    else:
        M, K = a.shape
    if tb:
        N, K2 = b.shape[-2:]
    else:
        K2, N = b.shape[-2:]
    assert K == K2, (a.shape, b.shape, ta, tb)
    tm = _div_tile(M, tm, 128 if ta else 16)
    tn = _div_tile(N, tn, 128)
    tk = _div_tile(K, tk, 16 if ta else 128)
    nk = K // tk
    a_spec = pl.BlockSpec((tk, tm), lambda i, j, k: (k, i)) if ta else pl.BlockSpec((tm, tk), lambda i, j, k: (i, k))
    if b_layer is None:
        b_spec = pl.BlockSpec((tn, tk), lambda i, j, k: (j, k)) if tb else pl.BlockSpec((tk, tn), lambda i, j, k: (k, j))
    elif tb:
        b_spec = pl.BlockSpec((None, tn, tk), lambda i, j, k: (b_layer, j, k))
    else:
        b_spec = pl.BlockSpec((None, tk, tn), lambda i, j, k: (b_layer, k, j))
    o_spec = pl.BlockSpec((tm, tn), lambda i, j, k: (i, j))
    out_shape = jax.ShapeDtypeStruct((M, N), out_dtype)
    out_spec = o_spec
    aliases = {}
    extra_specs, extra_args = [], ()
    if stack is not None:
        n_layers, layer, buf = stack[:3]
        n_total = stack[3] if len(stack) > 3 else N
        out_shape = jax.ShapeDtypeStruct((n_layers, M, n_total), out_dtype)
        out_spec = pl.BlockSpec((None, tm, tn), lambda i, j, k: (layer, i, n_map(j)))
        if buf is not None:
            extra_specs, extra_args = [pl.BlockSpec(memory_space=pl.ANY)], (buf,)
            aliases = {2 + (res is not None): 0}
    dims = (((0 if ta else 1,), (1 if tb else 0,)), ((), ()))
    has_res = res is not None
    n_extra = len(extra_args)

    def body(*refs):
        if n_extra:
            refs = refs[:2 + has_res] + refs[2 + has_res + n_extra:]
        if has_res:
            a_ref, b_ref, r_ref, o_ref, acc_ref = refs
        else:
            a_ref, b_ref, o_ref, acc_ref = refs
        k = pl.program_id(2)
        p = lax.dot_general(a_ref[...].astype(BF16), b_ref[...].astype(BF16), dims, preferred_element_type=F32)

        @pl.when(k == 0)
        def _():
            acc_ref[...] = p

        @pl.when(k > 0)
        def _():
            acc_ref[...] += p

        @pl.when(k == nk - 1)
        def _():
            r = acc_ref[...] * scale
            if has_res:
                r = r_ref[...] + r
            o_ref[...] = r.astype(out_dtype)

    in_specs = [a_spec, b_spec] + ([o_spec] if has_res else []) + extra_specs
    args = (a, b) + ((res,) if has_res else ()) + extra_args
    return pl.pallas_call(
        body, name=name, grid=(M // tm, N // tn, nk), in_specs=in_specs, out_specs=out_spec,
        out_shape=out_shape, input_output_aliases=aliases,
        scratch_shapes=[pltpu.VMEM((tm, tn), F32)],
        compiler_params=_cparams(("parallel", "parallel", "arbitrary")),
    )(*args)


def _rms_fwd(x, g, *, name):
    S, D = x.shape
    ts = _div_tile(S, 512)

    def body(x_ref, g_ref, o_ref):
        xv = x_ref[...]
        r = lax.rsqrt(jnp.mean(xv * xv, axis=-1, keepdims=True) + RMS_EPS)
        o_ref[...] = (xv * r * g_ref[...]).astype(BF16)

    return pl.pallas_call(
        body, name=name, grid=(S // ts,),
        in_specs=[pl.BlockSpec((ts, D), lambda i: (i, 0)), pl.BlockSpec((1, D), lambda i: (0, 0))],
        out_specs=pl.BlockSpec((ts, D), lambda i: (i, 0)),
        out_shape=jax.ShapeDtypeStruct((S, D), BF16),
        compiler_params=_cparams(("parallel",)),
    )(x, g)


NORM_SLAB = 256


def _norm_mm(h, g, w, layer, *, name):
    S, D = h.shape
    N = w.shape[-1]
    tm = _div_tile(S, 1024, NORM_SLAB)
    slab = min(NORM_SLAB, tm)

    def body(h_ref, g_ref, w_ref, u_ref, z_ref):
        for r0 in range(0, tm, slab):
            rows = pl.ds(r0, slab)
            xv = h_ref[rows, :]
            r = lax.rsqrt(jnp.mean(xv * xv, axis=-1, keepdims=True) + RMS_EPS)
            u = (xv * r * g_ref[...]).astype(BF16)
            u_ref[rows, :] = u
            z_ref[rows, :] = jnp.dot(u, w_ref[...], preferred_element_type=F32).astype(BF16)

    row = pl.BlockSpec((tm, D), lambda i: (i, 0))
    return pl.pallas_call(
        body, name=name, grid=(S // tm,),
        in_specs=[row, pl.BlockSpec((1, D), lambda i: (0, 0)), pl.BlockSpec((None, D, N), lambda i: (layer, 0, 0))],
        out_specs=[row, pl.BlockSpec((tm, N), lambda i: (i, 0))],
        out_shape=[jax.ShapeDtypeStruct((S, D), BF16), jax.ShapeDtypeStruct((S, N), BF16)],
        compiler_params=_cparams(("parallel",)),
    )(h, g, w)


def _mm_norm_bwd(dz, w, layer, h, g, dres, *, name, tk):
    S, K = dz.shape
    D = h.shape[1]
    tm = _div_tile(S, 1024, NORM_SLAB)
    slab = min(NORM_SLAB, tm)
    tk = _div_tile(K, tk, 128)
    nk = K // tk
    nt = (((1,), (1,)), ((), ()))

    def body(dz_ref, w_ref, h_ref, g_ref, dr_ref, dx_ref, dg_ref, acc):
        i = pl.program_id(0)
        k = pl.program_id(1)

        def norm_bwd(du_of):
            part = jnp.zeros((1, D), F32)
            for r0 in range(0, tm, slab):
                rows = pl.ds(r0, slab)
                du = du_of(rows)
                xv = h_ref[rows, :]
                r = lax.rsqrt(jnp.mean(xv * xv, axis=-1, keepdims=True) + RMS_EPS)
                xhat = xv * r
                part = part + jnp.sum(du * xhat, axis=0, keepdims=True)
                dxhat = du * g_ref[...]
                dx_ref[rows, :] = dr_ref[rows, :] + r * (
                    dxhat - xhat * jnp.mean(dxhat * xhat, axis=-1, keepdims=True))

            @pl.when(i == 0)
            def _():
                dg_ref[...] = part

            @pl.when(i > 0)
            def _():
                dg_ref[...] += part

        if nk == 1:
            norm_bwd(lambda rows: lax.dot_general(dz_ref[rows, :], w_ref[...], nt, preferred_element_type=F32))
        else:
            p = lax.dot_general(dz_ref[...], w_ref[...], nt, preferred_element_type=F32)

            @pl.when(k == 0)
            def _():
                acc[...] = p

            @pl.when(k > 0)
            def _():
                acc[...] += p

            @pl.when(k == nk - 1)
            def _():
                norm_bwd(lambda rows: acc[rows, :])

    row = pl.BlockSpec((tm, D), lambda i, k: (i, 0))
    vec = pl.BlockSpec((1, D), lambda i, k: (0, 0))
    return pl.pallas_call(
        body, name=name, grid=(S // tm, nk),
        in_specs=[pl.BlockSpec((tm, tk), lambda i, k: (i, k)), pl.BlockSpec((None, D, tk), lambda i, k: (layer, 0, k)),
                  row, vec, row],
        out_specs=[row, vec],
        out_shape=[jax.ShapeDtypeStruct((S, D), F32), jax.ShapeDtypeStruct((1, D), F32)],
        scratch_shapes=[pltpu.VMEM((tm, D), F32)],
        compiler_params=_cparams(("arbitrary", "arbitrary")),
    )(dz, w, h, g, dres)


def _rms_bwd(x, g, du, dres, *, name):
    S, D = x.shape
    ts = _div_tile(S, 512)
    has_res = dres is not None

    def body(*refs):
        if has_res:
            x_ref, g_ref, du_ref, dr_ref, dx_ref, dg_ref = refs
        else:
            x_ref, g_ref, du_ref, dg_ref = refs
        i = pl.program_id(0)
        xv = x_ref[...]
        duv = du_ref[...].astype(F32)
        r = lax.rsqrt(jnp.mean(xv * xv, axis=-1, keepdims=True) + RMS_EPS)
        xhat = xv * r
        part = jnp.sum(duv * xhat, axis=0, keepdims=True)

        @pl.when(i == 0)
        def _():
            dg_ref[...] = part

        @pl.when(i > 0)
        def _():
            dg_ref[...] += part

        if has_res:
            dxhat = duv * g_ref[...]
            dx = r * (dxhat - xhat * jnp.mean(dxhat * xhat, axis=-1, keepdims=True))
            dx_ref[...] = dr_ref[...] + dx

    row = pl.BlockSpec((ts, D), lambda i: (i, 0))
    vec = pl.BlockSpec((1, D), lambda i: (0, 0))
    if has_res:
        dx, dg = pl.pallas_call(
            body, name=name, grid=(S // ts,), in_specs=[row, vec, row, row], out_specs=[row, vec],
            out_shape=[jax.ShapeDtypeStruct((S, D), F32), jax.ShapeDtypeStruct((1, D), F32)],
            compiler_params=_cparams(("arbitrary",)),
        )(x, g, du, dres)
        return dx, dg
    dg = pl.pallas_call(
        body, name=name, grid=(S // ts,), in_specs=[row, vec, row], out_specs=vec,
        out_shape=jax.ShapeDtypeStruct((1, D), F32),
        compiler_params=_cparams(("arbitrary",)),
    )(x, g, du)
    return None, dg


def _final_loss(h, g, tgt, *, name):
    S, D = h.shape
    ts = _div_tile(S, 512)

    def body(h_ref, g_ref, t_ref, loss_ref, dh_ref, dg_ref):
        i = pl.program_id(0)
        xv = h_ref[...]
        gv = g_ref[...]
        r = lax.rsqrt(jnp.mean(xv * xv, axis=-1, keepdims=True) + RMS_EPS)
        xhat = xv * r
        err = xhat * gv - t_ref[...]
        lpart = 0.5 * jnp.sum(jnp.mean(err * err, axis=-1, keepdims=True), axis=0, keepdims=True)
        dy = err * (1.0 / D)
        gpart = jnp.sum(dy * xhat, axis=0, keepdims=True)

        @pl.when(i == 0)
        def _():
            loss_ref[...] = jnp.broadcast_to(lpart, loss_ref.shape)
            dg_ref[...] = gpart

        @pl.when(i > 0)
        def _():
            loss_ref[...] += jnp.broadcast_to(lpart, loss_ref.shape)
            dg_ref[...] += gpart

        dxhat = dy * gv
        dh_ref[...] = r * (dxhat - xhat * jnp.mean(dxhat * xhat, axis=-1, keepdims=True))

    row = pl.BlockSpec((ts, D), lambda i: (i, 0))
    vec = pl.BlockSpec((1, D), lambda i: (0, 0))
    return pl.pallas_call(
        body, name=name, grid=(S // ts,), in_specs=[row, vec, row],
        out_specs=[pl.BlockSpec((8, 128), lambda i: (0, 0)), row, vec],
        out_shape=[jax.ShapeDtypeStruct((8, 128), F32), jax.ShapeDtypeStruct((S, D), F32),
                   jax.ShapeDtypeStruct((1, D), F32)],
        compiler_params=_cparams(("arbitrary",)),
    )(h, g, tgt)


def _sigmoid(x):
    return 1.0 / (1.0 + jnp.exp(-x))


FFN_CHUNK = 1408
FFN_CHUNKS = D_FF // FFN_CHUNK
FFN_BWD_PIECE = 384
FFN_BWD_SLAB = 256


def _ffn_fwd_fused(h, g, w_gu, w_down, layer, *, name):
    S, D = h.shape
    tm = _div_tile(S, 256, 16)
    tf = FFN_CHUNK

    def body(h_ref, g_ref, wgu_ref, wd_ref, h2_ref, u_ref, gate_ref, up_ref, a_ref):
        xv = h_ref[...]
        r = lax.rsqrt(jnp.mean(xv * xv, axis=-1, keepdims=True) + RMS_EPS)
        u = (xv * r * g_ref[...]).astype(BF16)
        u_ref[...] = u
        y = None
        for c0 in range(0, D_FF, tf):
            cols = pl.ds(c0, tf)
            gate = jnp.dot(u, wgu_ref[:, cols], preferred_element_type=F32)
            up = jnp.dot(u, wgu_ref[:, pl.ds(D_FF + c0, tf)], preferred_element_type=F32)
            gate_ref[:, cols] = gate.astype(BF16)
            up_ref[:, cols] = up.astype(BF16)
            a = (gate * _sigmoid(gate) * up).astype(BF16)
            a_ref[:, cols] = a
            p = jnp.dot(a, wd_ref[cols, :], preferred_element_type=F32)
            y = p if y is None else y + p
        h2_ref[...] = xv + 0.5 * y

    row = pl.BlockSpec((tm, D), lambda i: (i, 0))
    wide = pl.BlockSpec((tm, D_FF), lambda i: (i, 0))
    hidden = jax.ShapeDtypeStruct((S, D_FF), BF16)
    return pl.pallas_call(
        body, name=name, grid=(S // tm,),
        in_specs=[row, pl.BlockSpec((1, D), lambda i: (0, 0)),
                  pl.BlockSpec((None, D, 2 * D_FF), lambda i: (layer, 0, 0)),
                  pl.BlockSpec((None, D_FF, D), lambda i: (layer, 0, 0))],
        out_specs=[row, row, wide, wide, wide],
        out_shape=[jax.ShapeDtypeStruct((S, D), F32), jax.ShapeDtypeStruct((S, D), BF16), hidden, hidden, hidden],
        compiler_params=_cparams(("parallel",), vmem=60 * 1024 * 1024),
    )(h, g, w_gu, w_down)


def _ffn_bwd_fused(dh, h, g, gate, up, w_gu, w_down, layer, *, name):
    S, D = h.shape
    tm = _div_tile(S, 512, FFN_BWD_SLAB)
    tf = FFN_CHUNK
    nj = D_FF // tf
    slab = min(FFN_BWD_SLAB, tm)
    nt = (((1,), (1,)), ((), ()))
    pieces = [(c0, min(FFN_BWD_PIECE, tf - c0)) for c0 in range(0, tf, FFN_BWD_PIECE)]

    def body(dh_ref, h_ref, g_ref, gate_ref, up_ref, wg_ref, wu_ref, wd_ref, dx_ref, dg_ref, dgate_ref, dup_ref,
             dy_s, acc):
        i = pl.program_id(0)
        j = pl.program_id(1)

        @pl.when(j == 0)
        def _():
            for r0 in range(0, tm, slab):
                rows = pl.ds(r0, slab)
                dy_s[rows, :] = (0.5 * dh_ref[rows, :]).astype(BF16)

        p = None
        for c0, cw in pieces:
            cols = pl.ds(c0, cw)
            da = lax.dot_general(dy_s[...], wd_ref[cols, :], nt, preferred_element_type=F32)
            gt = gate_ref[:, cols].astype(F32)
            sg = _sigmoid(gt)
            dgate = (da * up_ref[:, cols].astype(F32) * sg * (1.0 + gt * (1.0 - sg))).astype(BF16)
            dup = (da * gt * sg).astype(BF16)
            dgate_ref[:, cols] = dgate
            dup_ref[:, cols] = dup
            q = (lax.dot_general(dgate, wg_ref[:, cols], nt, preferred_element_type=F32)
                 + lax.dot_general(dup, wu_ref[:, cols], nt, preferred_element_type=F32))
            p = q if p is None else p + q

        @pl.when(j == 0)
        def _():
            acc[...] = p

        @pl.when(j > 0)
        def _():
            acc[...] += p

        @pl.when(j == nj - 1)
        def _():
            part = jnp.zeros((1, D), F32)
            for r0 in range(0, tm, slab):
                rows = pl.ds(r0, slab)
                xv = h_ref[rows, :]
                du = acc[rows, :]
                r = lax.rsqrt(jnp.mean(xv * xv, axis=-1, keepdims=True) + RMS_EPS)
                xhat = xv * r
                part = part + jnp.sum(du * xhat, axis=0, keepdims=True)
                dxhat = du * g_ref[...]
                dx_ref[rows, :] = dh_ref[rows, :] + r * (
                    dxhat - xhat * jnp.mean(dxhat * xhat, axis=-1, keepdims=True))

            @pl.when(i == 0)
            def _():
                dg_ref[...] = part

            @pl.when(i > 0)
            def _():
                dg_ref[...] += part

    row = pl.BlockSpec((tm, D), lambda i, j: (i, 0))
    vec = pl.BlockSpec((1, D), lambda i, j: (0, 0))
    chunk = pl.BlockSpec((tm, tf), lambda i, j: (i, j))
    hidden = jax.ShapeDtypeStruct((S, D_FF), BF16)
    return pl.pallas_call(
        body, name=name, grid=(S // tm, nj),
        in_specs=[row, row, vec, chunk, chunk,
                  pl.BlockSpec((None, D, tf), lambda i, j: (layer, 0, j)),
                  pl.BlockSpec((None, D, tf), lambda i, j: (layer, 0, nj + j)),
                  pl.BlockSpec((None, tf, D), lambda i, j: (layer, j, 0))],
        out_specs=[row, vec, chunk, chunk],
        out_shape=[jax.ShapeDtypeStruct((S, D), F32), jax.ShapeDtypeStruct((1, D), F32), hidden, hidden],
        scratch_shapes=[pltpu.VMEM((tm, D), BF16), pltpu.VMEM((tm, D), F32)],
        compiler_params=_cparams(("arbitrary", "arbitrary")),
    )(dh, h, g, gate, up, w_gu, w_gu, w_down)


CONV_HALO = 32
CONV_SUB_ROWS = 128


def _shifted_taps(win, shifted, ts):
    n = ts + CONV_HALO - 8
    for r in range(1, 8):
        shifted[r - 1] = win[pl.ds(r, n), :]

    def tap(start, rows, lanes):
        q, r = divmod(start, 8)
        if r == 0:
            return win[pl.ds(start, rows), lanes]
        return shifted[r - 1, pl.ds(8 * q, rows), lanes]

    return tap


def _conv_a_fwd(z, w, bias, ln_g, ln_b, *, name):
    S = z.shape[0]
    C = CONV_A_CH
    ts = _div_tile(S, 256, 32)

    def body(val_ref, gate_ref, w_ref, b_ref, g_ref, lb_ref, c_ref, act_ref, win, shifted):
        i = pl.program_id(0)

        @pl.when(i == 0)
        def _():
            win[pl.ds(0, CONV_HALO), :] = jnp.zeros((CONV_HALO, C), F32)

        @pl.when(i > 0)
        def _():
            win[pl.ds(0, CONV_HALO), :] = win[pl.ds(ts, CONV_HALO), :]

        a = val_ref[...].astype(F32) * _sigmoid(gate_ref[...].astype(F32))
        win[pl.ds(CONV_HALO, ts), :] = a
        tap = _shifted_taps(win, shifted, ts)
        rs = min(CONV_SUB_ROWS, ts)
        for cb in range(C // 128):
            lanes = pl.ds(128 * cb, 128)
            for rt in range(ts // rs):
                sub = jnp.broadcast_to(b_ref[:, lanes], (rs, 128))
                for k in range(CONV_A_WIDTH):
                    sub = sub + w_ref[pl.ds(k, 1), lanes] * tap(
                        CONV_HALO - (CONV_A_WIDTH - 1) + k + rs * rt, rs, lanes)
                c_ref[pl.ds(rs * rt, rs), lanes] = sub
        acc = c_ref[...]
        mu = jnp.mean(acc, axis=-1, keepdims=True)
        xc = acc - mu
        var = jnp.mean(xc * xc, axis=-1, keepdims=True)
        ln = xc * lax.rsqrt(var + LN_EPS) * g_ref[...] + lb_ref[...]
        act_ref[...] = (ln * _sigmoid(ln)).astype(BF16)

    row = lambda col: pl.BlockSpec((ts, C), lambda i, col=col: (i, col))
    vec = pl.BlockSpec((1, C), lambda i: (0, 0))
    return pl.pallas_call(
        body, name=name, grid=(S // ts,),
        in_specs=[row(0), row(1), pl.BlockSpec((32, C), lambda i: (0, 0)), vec, vec, vec],
        out_specs=[row(0), row(0)],
        out_shape=[jax.ShapeDtypeStruct((S, C), F32), jax.ShapeDtypeStruct((S, 2 * C), BF16)],
        scratch_shapes=[pltpu.VMEM((ts + CONV_HALO, C), F32), pltpu.VMEM((7, ts + CONV_HALO - 8, C), F32)],
        compiler_params=_cparams(("arbitrary",)),
    )(z, z, w, bias, ln_g, ln_b)


def _conv_a_bwd(z, c, dcat, w, ln_g, ln_b, *, name):
    S = z.shape[0]
    C = CONV_A_CH
    ts = _div_tile(S, 256, 32)
    n = S // ts

    def body(val_ref, gate_ref, c_ref, da_ref, w_ref, g_ref, lb_ref, dz_ref, small_ref, win, a_s, da_s, dw8,
             shifted):
        i = pl.program_id(0)

        @pl.when(i == 0)
        def _():
            win[pl.ds(ts, CONV_HALO), :] = jnp.zeros((CONV_HALO, C), F32)
            small_ref[...] = jnp.zeros(small_ref.shape, F32)
            dw8[...] = jnp.zeros(dw8.shape, F32)

        @pl.when(i > 0)
        def _():
            win[pl.ds(ts, CONV_HALO), :] = win[pl.ds(0, CONV_HALO), :]

        cv = c_ref[...]
        gv = g_ref[...]
        mu = jnp.mean(cv, axis=-1, keepdims=True)
        xc = cv - mu
        var = jnp.mean(xc * xc, axis=-1, keepdims=True)
        rstd = lax.rsqrt(var + LN_EPS)
        xhat = xc * rstd
        ln = xhat * gv + lb_ref[...]
        sg = _sigmoid(ln)
        dln = da_ref[...].astype(F32) * (sg * (1.0 + ln * (1.0 - sg)))
        small_ref[pl.ds(33, 1), :] += jnp.sum(dln * xhat, axis=0, keepdims=True)
        small_ref[pl.ds(34, 1), :] += jnp.sum(dln, axis=0, keepdims=True)
        dxhat = dln * gv
        dc = rstd * (dxhat - jnp.mean(dxhat, axis=-1, keepdims=True)
                     - xhat * jnp.mean(dxhat * xhat, axis=-1, keepdims=True))
        small_ref[pl.ds(32, 1), :] += jnp.sum(dc, axis=0, keepdims=True)
        win[pl.ds(0, ts), :] = dc

        val = val_ref[...].astype(F32)
        sgg = _sigmoid(gate_ref[...].astype(F32))
        a_s[...] = val * sgg
        tap = _shifted_taps(win, shifted, ts)
        rs = min(CONV_SUB_ROWS, ts)
        for cb in range(C // 128):
            lanes = pl.ds(128 * cb, 128)
            for rt in range(ts // rs):
                a_sub = a_s[pl.ds(rs * rt, rs), lanes]
                da = jnp.zeros((rs, 128), F32)
                for k in range(CONV_A_WIDTH):
                    sh = tap(CONV_A_WIDTH - 1 - k + rs * rt, rs, lanes)
                    da = da + w_ref[pl.ds(k, 1), lanes] * sh
                    prod = a_sub * sh
                    part = prod[0:8]
                    for r in range(1, rs // 8):
                        part = part + prod[8 * r:8 * r + 8]
                    dw8[pl.ds(8 * k, 8), lanes] += part
                da_s[pl.ds(rs * rt, rs), lanes] = da
        da = da_s[...]
        dz_ref[:, pl.ds(0, C)] = (da * sgg).astype(BF16)
        dz_ref[:, pl.ds(C, C)] = (da * val * sgg * (1.0 - sgg)).astype(BF16)

        @pl.when(i == n - 1)
        def _():
            for k in range(CONV_A_WIDTH):
                small_ref[pl.ds(k, 1), :] = jnp.sum(dw8[pl.ds(8 * k, 8), :], axis=0, keepdims=True)

    row = lambda col: pl.BlockSpec((ts, C), lambda i, col=col: (n - 1 - i, col))
    vec = pl.BlockSpec((1, C), lambda i: (0, 0))
    return pl.pallas_call(
        body, name=name, grid=(n,),
        in_specs=[row(0), row(1), row(0), row(0), pl.BlockSpec((32, C), lambda i: (0, 0)), vec, vec],
        out_specs=[pl.BlockSpec((ts, 2 * C), lambda i: (n - 1 - i, 0)), pl.BlockSpec((40, C), lambda i: (0, 0))],
        out_shape=[jax.ShapeDtypeStruct((S, 2 * C), BF16), jax.ShapeDtypeStruct((40, C), F32)],
        scratch_shapes=[pltpu.VMEM((ts + CONV_HALO, C), F32), pltpu.VMEM((ts, C), F32), pltpu.VMEM((ts, C), F32),
                        pltpu.VMEM((8 * 32, C), F32), pltpu.VMEM((7, ts + CONV_HALO - 8, C), F32)],
        compiler_params=_cparams(("arbitrary",)),
    )(z, z, c, dcat, w, ln_g, ln_b)


SC_HALO = 8


def _sconv_fwd(z, w, *, name):
    S = z.shape[0]
    C = SC_CH
    ts = _div_tile(S, 256, 16)

    def body(gb_ref, gc_ref, v_ref, w_ref, y_ref, cc_ref, win):
        i = pl.program_id(0)

        @pl.when(i == 0)
        def _():
            win[pl.ds(0, SC_HALO), :] = jnp.zeros((SC_HALO, C), F32)

        @pl.when(i > 0)
        def _():
            win[pl.ds(0, SC_HALO), :] = win[pl.ds(ts, SC_HALO), :]

        win[pl.ds(SC_HALO, ts), :] = gc_ref[...].astype(F32) * v_ref[...].astype(F32)
        acc = jnp.zeros((ts, C), F32)
        for k in range(SC_WIDTH):
            acc = acc + w_ref[pl.ds(k, 1), :] * win[pl.ds(SC_HALO - (SC_WIDTH - 1) + k, ts), :]
        cc_ref[...] = acc.astype(BF16)
        y_ref[...] = (gb_ref[...].astype(F32) * acc).astype(BF16)

    row = lambda col: pl.BlockSpec((ts, C), lambda i, col=col: (i, col))
    return pl.pallas_call(
        body, name=name, grid=(S // ts,),
        in_specs=[row(0), row(1), row(2), pl.BlockSpec((8, C), lambda i: (0, 0))],
        out_specs=[row(0), row(0)],
        out_shape=[jax.ShapeDtypeStruct((S, C), BF16), jax.ShapeDtypeStruct((S, C), BF16)],
        scratch_shapes=[pltpu.VMEM((ts + SC_HALO, C), F32)],
        compiler_params=_cparams(("arbitrary",)),
    )(z, z, z, w)


def _sconv_bwd(z, cc, dy, w, *, name):
    S = z.shape[0]
    C = SC_CH
    ts = _div_tile(S, 256, 16)
    n = S // ts

    def body(gb_ref, gc_ref, v_ref, cc_ref, dy_ref, w_ref, dz_ref, dw_ref, win):
        i = pl.program_id(0)

        @pl.when(i == 0)
        def _():
            win[pl.ds(ts, SC_HALO), :] = jnp.zeros((SC_HALO, C), F32)
            dw_ref[...] = jnp.zeros(dw_ref.shape, F32)

        @pl.when(i > 0)
        def _():
            win[pl.ds(ts, SC_HALO), :] = win[pl.ds(0, SC_HALO), :]

        dyv = dy_ref[...].astype(F32)
        gb = gb_ref[...].astype(F32)
        gc = gc_ref[...].astype(F32)
        val = v_ref[...].astype(F32)
        dz_ref[:, pl.ds(0, C)] = (dyv * cc_ref[...].astype(F32)).astype(BF16)
        win[pl.ds(0, ts), :] = dyv * gb
        cv = gc * val
        dcv = jnp.zeros((ts, C), F32)
        for k in range(SC_WIDTH):
            sh = win[pl.ds(SC_WIDTH - 1 - k, ts), :]
            dcv = dcv + w_ref[pl.ds(k, 1), :] * sh
            dw_ref[pl.ds(k, 1), :] += jnp.sum(cv * sh, axis=0, keepdims=True)
        dz_ref[:, pl.ds(C, C)] = (dcv * val).astype(BF16)
        dz_ref[:, pl.ds(2 * C, C)] = (dcv * gc).astype(BF16)

    row = lambda col: pl.BlockSpec((ts, C), lambda i, col=col: (n - 1 - i, col))
    return pl.pallas_call(
        body, name=name, grid=(n,),
        in_specs=[row(0), row(1), row(2), row(0), row(0), pl.BlockSpec((8, C), lambda i: (0, 0))],
        out_specs=[pl.BlockSpec((ts, 3 * C), lambda i: (n - 1 - i, 0)), pl.BlockSpec((8, C), lambda i: (0, 0))],
        out_shape=[jax.ShapeDtypeStruct((S, 3 * C), BF16), jax.ShapeDtypeStruct((8, C), F32)],
        scratch_shapes=[pltpu.VMEM((ts + SC_HALO, C), F32)],
        compiler_params=_cparams(("arbitrary",)),
    )(z, z, z, cc, dy, w)


SWA_Q_COL = 2
SWA_SLOPES = [2.0 ** (-8.0 * (h + 1) / SWA_HEADS) for h in range(SWA_HEADS)]
SWA_SCALE = HEAD_DIM ** -0.5


SWA_GROUP_ROWS = SWA_GROUP * WINDOW


def _swa_masks():
    shape = (SWA_GROUP_ROWS, 2 * WINDOW)
    ii = lax.broadcasted_iota(jnp.int32, shape, 0)
    jj = lax.broadcasted_iota(jnp.int32, shape, 1)
    dist = (ii & (WINDOW - 1)) + WINDOW - jj
    valid = (dist >= 0) & (dist < WINDOW)
    grp = lax.broadcasted_iota(jnp.int32, (SWA_GROUP_ROWS, 1), 0) // WINDOW
    return dist.astype(F32), valid, jj, grp


def _by_group(grp, vals):
    out = jnp.full(grp.shape, vals[SWA_GROUP - 1], F32)
    for g in range(SWA_GROUP - 2, -1, -1):
        out = jnp.where(grp == g, vals[g], out)
    return out


def _stack_heads(ref, rows, kv):
    return jnp.concatenate([ref[rows, pl.ds(HEAD_DIM * (kv * SWA_GROUP + g), HEAD_DIM)] for g in range(SWA_GROUP)],
                           axis=0)


def _swa_probs(qg, kk, sink, slope, distf, valid):
    s = lax.dot_general(qg, kk, (((1,), (1,)), ((), ())), preferred_element_type=F32) * SWA_SCALE
    s = s - slope * distf
    s = jnp.where(valid, s, NEG_BIG)
    m = jnp.maximum(jnp.max(s, axis=-1, keepdims=True), sink)
    p = jnp.exp(s - m)
    l = jnp.sum(p, axis=-1, keepdims=True) + jnp.exp(sink - m)
    return p, m, l


def _swa_fwd(z, kpad, vpad, sinks, cat, *, name):
    S = z.shape[0]
    tq = _div_tile(S, 256, 128)
    nblk = tq // WINDOW
    W = WINDOW

    def body(sink_ref, q_ref, k_ref, v_ref, cat_ref, o_ref):
        i = pl.program_id(0)
        distf, valid0, jj, grp = _swa_masks()
        for kv in range(SWA_KV_HEADS):
            heads = range(kv * SWA_GROUP, (kv + 1) * SWA_GROUP)
            sink = _by_group(grp, [sink_ref[h] for h in heads])
            slope = _by_group(grp, [SWA_SLOPES[h] for h in heads])
            for b in range(nblk):
                nb = i * nblk + b
                start = pl.multiple_of(nb * W, W)
                rows = pl.ds(W * b, W)
                valid = valid0 & ((jj >= W) | (nb > 0))
                kk = k_ref[pl.ds(start, 2 * W), pl.ds(HEAD_DIM * kv, HEAD_DIM)]
                vv = v_ref[pl.ds(start, 2 * W), pl.ds(HEAD_DIM * kv, HEAD_DIM)]
                p, m, l = _swa_probs(_stack_heads(q_ref, rows, kv), kk, sink, slope, distf, valid)
                o = (jnp.dot(p.astype(BF16), vv, preferred_element_type=F32) / l).astype(BF16)
                for g, h in enumerate(heads):
                    o_ref[rows, pl.ds(HEAD_DIM * h, HEAD_DIM)] = o[W * g:W * (g + 1)]

    full = pl.BlockSpec((S + W, 2 * HEAD_DIM), lambda i: (0, 0))
    return pl.pallas_call(
        body, name=name, grid=(S // tq,),
        in_specs=[pl.BlockSpec(memory_space=pltpu.SMEM), pl.BlockSpec((tq, 512), lambda i: (i, SWA_Q_COL)), full, full,
                  pl.BlockSpec(memory_space=pl.ANY)],
        out_specs=pl.BlockSpec((tq, 512), lambda i: (i, 1)),
        out_shape=jax.ShapeDtypeStruct((S, 1024), BF16), input_output_aliases={4: 0},
        compiler_params=_cparams(("parallel",)),
    )(sinks, z, kpad, vpad, cat)


def _swa_bwd(z, kpad, vpad, sinks, dcat, *, name):
    S = z.shape[0]
    tq = _div_tile(S, 256, 128)
    nblk = tq // WINDOW
    W = WINDOW

    def body(sink_ref, q_ref, k_ref, v_ref, do_ref, dq_ref, dk_ref, dv_ref, ds_ref):
        i = pl.program_id(0)

        @pl.when(i == 0)
        def _():
            dk_ref[...] = jnp.zeros(dk_ref.shape, F32)
            dv_ref[...] = jnp.zeros(dv_ref.shape, F32)
            ds_ref[...] = jnp.zeros(ds_ref.shape, F32)

        distf, valid0, jj, grp = _swa_masks()
        tn = (((0,), (0,)), ((), ()))
        for kv in range(SWA_KV_HEADS):
            heads = range(kv * SWA_GROUP, (kv + 1) * SWA_GROUP)
            sink = _by_group(grp, [sink_ref[h] for h in heads])
            slope = _by_group(grp, [SWA_SLOPES[h] for h in heads])
            for b in range(nblk):
                nb = i * nblk + b
                start = pl.multiple_of(nb * W, W)
                rows = pl.ds(W * b, W)
                valid = valid0 & ((jj >= W) | (nb > 0))
                kk = k_ref[pl.ds(start, 2 * W), pl.ds(HEAD_DIM * kv, HEAD_DIM)]
                vv = v_ref[pl.ds(start, 2 * W), pl.ds(HEAD_DIM * kv, HEAD_DIM)]
                qg = _stack_heads(q_ref, rows, kv)
                dog = _stack_heads(do_ref, rows, kv)
                p, m, l = _swa_probs(qg, kk, sink, slope, distf, valid)
                inv_l = 1.0 / l
                pn = p * inv_l
                dp = lax.dot_general(dog, vv, (((1,), (1,)), ((), ())), preferred_element_type=F32)
                delta = jnp.sum(pn * dp, axis=-1, keepdims=True)
                dsc = (pn * (dp - delta)).astype(BF16)
                dsink = jnp.exp(sink - m) * inv_l * delta
                dq = (jnp.dot(dsc, kk, preferred_element_type=F32) * SWA_SCALE).astype(BF16)
                for g, h in enumerate(heads):
                    ds_ref[pl.ds(h, 1), :] += jnp.broadcast_to(
                        -jnp.sum(dsink[W * g:W * (g + 1)], axis=0, keepdims=True), (1, 128))
                    dq_ref[rows, pl.ds(HEAD_DIM * h, HEAD_DIM)] = dq[W * g:W * (g + 1)]
                dk_ref[pl.ds(start, 2 * W), pl.ds(HEAD_DIM * kv, HEAD_DIM)] += lax.dot_general(
                    dsc, qg, tn, preferred_element_type=F32) * SWA_SCALE
                dv_ref[pl.ds(start, 2 * W), pl.ds(HEAD_DIM * kv, HEAD_DIM)] += lax.dot_general(
                    pn.astype(BF16), dog, tn, preferred_element_type=F32)

    full = pl.BlockSpec((S + W, 2 * HEAD_DIM), lambda i: (0, 0))
    return pl.pallas_call(
        body, name=name, grid=(S // tq,),
        in_specs=[pl.BlockSpec(memory_space=pltpu.SMEM), pl.BlockSpec((tq, 512), lambda i: (i, SWA_Q_COL)), full, full,
                  pl.BlockSpec((tq, 512), lambda i: (i, 1))],
        out_specs=[pl.BlockSpec((tq, 512), lambda i: (i, 0)), full, full, pl.BlockSpec((8, 128), lambda i: (0, 0))],
        out_shape=[jax.ShapeDtypeStruct((S, 512), BF16), jax.ShapeDtypeStruct((S + W, 2 * HEAD_DIM), F32),
                   jax.ShapeDtypeStruct((S + W, 2 * HEAD_DIM), F32), jax.ShapeDtypeStruct((8, 128), F32)],
        compiler_params=_cparams(("arbitrary",)),
    )(sinks, z, kpad, vpad, dcat)


XA_SCALE = XA_HEAD_DIM ** -0.5


def _xa_probs(qh, kh):
    s = lax.dot_general(qh, kh, (((1,), (1,)), ((), ())), preferred_element_type=F32) * XA_SCALE
    m = jnp.max(s, axis=-1, keepdims=True)
    p = jnp.exp(s - m)
    return p, jnp.sum(p, axis=-1, keepdims=True)


def _xa_fwd(h, g, wq, wo, layer, kv, *, name):
    S, D = h.shape
    M = kv.shape[0]
    tm = _div_tile(S, 512, NORM_SLAB)
    slab = min(NORM_SLAB, tm)
    HD = XA_HEAD_DIM

    def body(h_ref, g_ref, wq_ref, wo_ref, k_ref, v_ref, h2_ref, u_ref, q_ref, o_ref):
        for r0 in range(0, tm, slab):
            rows = pl.ds(r0, slab)
            xv = h_ref[rows, :]
            r = lax.rsqrt(jnp.mean(xv * xv, axis=-1, keepdims=True) + RMS_EPS)
            u = (xv * r * g_ref[...]).astype(BF16)
            u_ref[rows, :] = u
            q_ref[rows, :] = jnp.dot(u, wq_ref[...], preferred_element_type=F32).astype(BF16)
            for hd in range(XA_HEADS):
                cols = pl.ds(HD * hd, HD)
                p, l = _xa_probs(q_ref[rows, cols], k_ref[:, cols])
                o = jnp.dot(p.astype(BF16), v_ref[:, cols], preferred_element_type=F32) / l
                o_ref[rows, cols] = o.astype(BF16)
            h2_ref[rows, :] = xv + jnp.dot(o_ref[rows, :], wo_ref[...], preferred_element_type=F32)

    row = pl.BlockSpec((tm, D), lambda i: (i, 0))
    weight = pl.BlockSpec((None, D, D), lambda i: (layer, 0, 0))
    act = jax.ShapeDtypeStruct((S, D), BF16)
    return pl.pallas_call(
        body, name=name, grid=(S // tm,),
        in_specs=[row, pl.BlockSpec((1, D), lambda i: (0, 0)), weight, weight,
                  pl.BlockSpec((M, D), lambda i: (0, 0)), pl.BlockSpec((M, D), lambda i: (0, 1))],
        out_specs=[row, row, row, row],
        out_shape=[jax.ShapeDtypeStruct((S, D), F32), act, act, act],
        compiler_params=_cparams(("parallel",)),
    )(h, g, wq, wo, kv, kv)


def _xa_bwd(q, kv, do, *, name):
    S, D = q.shape
    M = kv.shape[0]
    ts = _div_tile(S, 512, 16)
    HD = XA_HEAD_DIM

    def body(q_ref, k_ref, v_ref, do_ref, dq_ref, dkv_ref):
        i = pl.program_id(0)

        @pl.when(i == 0)
        def _():
            dkv_ref[...] = jnp.zeros(dkv_ref.shape, F32)

        for h in range(XA_HEADS):
            qh = q_ref[:, pl.ds(HD * h, HD)]
            kh = k_ref[:, pl.ds(HD * h, HD)]
            vh = v_ref[:, pl.ds(HD * h, HD)]
            doh = do_ref[:, pl.ds(HD * h, HD)]
            p, l = _xa_probs(qh, kh)
            pn = p * (1.0 / l)
            dp = lax.dot_general(doh, vh, (((1,), (1,)), ((), ())), preferred_element_type=F32)
            delta = jnp.sum(pn * dp, axis=-1, keepdims=True)
            dsc = (pn * (dp - delta)).astype(BF16)
            dq_ref[:, pl.ds(HD * h, HD)] = (jnp.dot(dsc, kh, preferred_element_type=F32) * XA_SCALE).astype(BF16)
            dkv_ref[:, pl.ds(HD * h, HD)] += lax.dot_general(
                dsc, qh, (((0,), (0,)), ((), ())), preferred_element_type=F32) * XA_SCALE
            dkv_ref[:, pl.ds(D + HD * h, HD)] += lax.dot_general(
                pn.astype(BF16), doh, (((0,), (0,)), ((), ())), preferred_element_type=F32)

    row = pl.BlockSpec((ts, D), lambda i: (i, 0))
    return pl.pallas_call(
        body, name=name, grid=(S // ts,),
        in_specs=[row, pl.BlockSpec((M, D), lambda i: (0, 0)), pl.BlockSpec((M, D), lambda i: (0, 1)), row],
        out_specs=[row, pl.BlockSpec((M, 2 * D), lambda i: (0, 0))],
        out_shape=[jax.ShapeDtypeStruct((S, D), BF16), jax.ShapeDtypeStruct((M, 2 * D), F32)],
        compiler_params=_cparams(("arbitrary",)),
    )(q, kv, kv, do)


def _adam_math(w, g, m, v):
    m = ADAM_B1 * m + (1.0 - ADAM_B1) * g
    v = ADAM_B2 * v + (1.0 - ADAM_B2) * (g * g)
    m_hat = m / (1.0 - ADAM_B1 ** ADAM_STEP)
    v_hat = v / (1.0 - ADAM_B2 ** ADAM_STEP)
    delta = -ADAM_LR * (m_hat / (jnp.sqrt(v_hat) + ADAM_EPS) + ADAM_WD * w)
    return delta, m, v


def _adamw_layers(w, m, v, gsrc, *, name):
    L, A, B = w.shape
    tr = _div_tile(A, max(8, ADAM_TILE_ELEMS // B // 8 * 8))
    nt = A // tr
    flat = [a for srcs in gsrc for a in srcs]
    owner = [l for l, srcs in enumerate(gsrc) for _ in srcs]
    ng = len(flat)

    def body(*refs):
        w_ref, m_ref, v_ref = refs[:3]
        g_refs = refs[3:3 + ng]
        g_ref, d_ref, nm_ref, nv_ref = refs[3 + ng:]
        layer = pl.program_id(0)
        g = None
        for l in range(L):
            gl = None
            for a_ref, o in zip(g_refs, owner):
                if o == l:
                    gl = a_ref[...] if gl is None else gl + a_ref[...]
            g = gl if g is None else jnp.where(layer == l, gl, g)
        d, nm, nv = _adam_math(w_ref[...], g, m_ref[...], v_ref[...])
        g_ref[...] = g
        d_ref[...] = d
        nm_ref[...] = nm
        nv_ref[...] = nv

    def src_spec(o):
        return pl.BlockSpec((None, tr, B),
                            lambda l, i: (0, jnp.where(l == o, i, jnp.where(l > o, nt - 1, 0)), 0))

    spec = pl.BlockSpec((None, tr, B), lambda l, i: (l, i, 0))
    sds = jax.ShapeDtypeStruct((L, A, B), F32)
    return pl.pallas_call(
        body, name=name, grid=(L, nt), in_specs=[spec] * 3 + [src_spec(o) for o in owner], out_specs=[spec] * 4,
        out_shape=[sds] * 4, compiler_params=_cparams(("arbitrary", "arbitrary")),
    )(w, m, v, *flat)


def _adamw_small(ws, ms, vs, gparts, *, name):
    n = len(ws)
    R = gparts.shape[1]

    def body(*refs):
        w_refs, m_refs, v_refs = refs[:n], refs[n:2 * n], refs[2 * n:3 * n]
        gp_ref = refs[3 * n]
        outs = refs[3 * n + 1:7 * n + 1]
        packed = refs[7 * n + 1]
        g = gp_ref[0]
        for k in range(1, N_DEV):
            g = g + gp_ref[k]
        packed[...] = g
        row = 0
        for p in range(n):
            r, c = ws[p].shape
            per, lanes = max(c // 128, 1), min(c, 128)
            g_ref = outs[p]
            for i in range(r):
                for k in range(per):
                    g_ref[pl.ds(i, 1), pl.ds(128 * k, lanes)] = packed[pl.ds(row, 1), pl.ds(0, lanes)]
                    row += 1
            d, nm, nv = _adam_math(w_refs[p][...], g_ref[...], m_refs[p][...], v_refs[p][...])
            outs[n + p][...] = d
            outs[2 * n + p][...] = nm
            outs[3 * n + p][...] = nv

    shapes = [jax.ShapeDtypeStruct(a.shape, F32) for a in ws]
    res = pl.pallas_call(body, name=name, out_shape=shapes * 4, scratch_shapes=[pltpu.VMEM((R, 128), F32)],
                         compiler_params=_cparams())(*ws, *ms, *vs, gparts)
    return res[:n], res[n:2 * n], res[2 * n:3 * n], res[3 * n:]


ANY = pl.BlockSpec(memory_space=pl.ANY)


def _mesh_pos():
    return lax.axis_index("x"), lax.axis_index("y"), lax.axis_index("c")


def _other_chips(x, y):
    return [(1 - x, y), (x, 1 - y), (1 - x, 1 - y)]


LAYOUT = {'ffn1_w_gu': 'col', 'ffn1_w_down': 'stk', 'even_w_in': 'stk', 'even_w_out': 'stk', 'odd_w_in': 'col',
          'odd_w_out': 'stk', 'xa_wq': 'stk', 'xa_wkv': 'col', 'xa_wo': 'stk', 'ffn2_w_gu': 'col',
          'ffn2_w_down': 'stk', 'tiny': 'stk'}
COMM_NAMES = list(LAYOUT)
TINY_ROWS = 48


def _gathered_piece(ref, kind, L, A, h):
    if L == 2:
        return ref.at[h]
    rows = pl.ds(pl.multiple_of(h * (A // 2), 8), A // 2)
    return ref.at[0, rows] if kind == 'col' else ref.at[0, :, rows]


def _chip_part(piece, kind, B, s):
    if kind == 'col':
        return piece.at[:, pl.ds(pl.multiple_of(s * B, 128), B)]
    return piece.at[s]


def _place(shard, layer, kind, chip_idx, out_dtype, *, name, after=None):
    L, A, B = shard.shape
    ta = _div_tile(A, 256, 16)
    extra = [] if after is None else [after]

    def body(s_ref, x_ref, *rest):
        rest[-1][...] = x_ref[...].astype(out_dtype)

    if kind == 'col':
        shape = (1, A, N_CHIPS * B)
        out_spec = pl.BlockSpec((None, ta, B), lambda i, s: (0, i, s[0]))
    else:
        shape = (1, N_CHIPS, A, B)
        out_spec = pl.BlockSpec((None, None, ta, B), lambda i, s: (0, s[0], i, 0))
    grid_spec = pltpu.PrefetchScalarGridSpec(
        num_scalar_prefetch=1, grid=(A // ta,),
        in_specs=[pl.BlockSpec((None, ta, B), lambda i, s: (layer, i, 0))]
        + [pl.BlockSpec(memory_space=pl.ANY)] * len(extra), out_specs=out_spec)
    return pl.pallas_call(
        body, name=name, grid_spec=grid_spec, out_shape=jax.ShapeDtypeStruct(shape, out_dtype),
        compiler_params=_cparams(("parallel",)),
    )(chip_idx, shard, *extra)


HBM = pl.BlockSpec(memory_space=pltpu.HBM)
SEM = pl.BlockSpec(memory_space=pltpu.SEMAPHORE)
DATAFLOW = pltpu.SideEffectType.DATAFLOW_SIDE_EFFECTING


def _own_part_copies(refs, meta, send_sems, recv_sems):
    x, y, c = _mesh_pos()
    cps = []
    for k, (kind, L, A, B) in enumerate(meta):
        for j, (cx, cy) in enumerate(_other_chips(x, y)):
            part = _chip_part(refs[k].at[0], kind, B, 2 * x + y)
            cps.append(pltpu.make_async_remote_copy(
                src_ref=part, dst_ref=part, send_sem=send_sems.at[3 * k + j], recv_sem=recv_sems.at[3 * k + j],
                device_id=(cx, cy, c), device_id_type=MESH))
    return cps


def _half_part_copies(refs, meta, send_sems, recv_sems):
    x, y, c = _mesh_pos()
    cps = []
    for k, (kind, L, A, B) in enumerate(meta):
        for j, (cx, cy) in enumerate(_other_chips(x, y)):
            part = _chip_part(_gathered_piece(refs[k], kind, 1, A, c), kind, B, 2 * x + y)
            cps.append(pltpu.make_async_remote_copy(
                src_ref=part, dst_ref=part, send_sem=send_sems.at[3 * k + j], recv_sem=recv_sems.at[3 * k + j],
                device_id=(cx, cy, c), device_id_type=MESH))
    return cps


def _forward_copies(refs, meta, send_sems, recv_sems):
    x, y, c = _mesh_pos()
    cps = []
    for k, (kind, L, A, B) in enumerate(meta):
        for j, (cx, cy) in enumerate(_other_chips(x, y)):
            part = _chip_part(_gathered_piece(refs[k], kind, 1, A, c), kind, B, 2 * cx + cy)
            cps.append(pltpu.make_async_remote_copy(
                src_ref=part, dst_ref=part, send_sem=send_sems.at[3 * k + j], recv_sem=recv_sems.at[3 * k + j],
                device_id=(x, y, 1 - c), device_id_type=MESH))
    return cps


def _gather_start(fulls, meta, after, tag, copies=_own_part_copies):
    n = len(fulls)

    def body(*refs):
        send_sems, recv_sems = refs[n + 1], refs[n + 2]
        outs = refs[n + 3:2 * n + 3]
        token = refs[2 * n + 3]
        for cp in copies(outs, meta, send_sems, recv_sems):
            cp.start()
        token[...] = jnp.zeros_like(token)

    res = pl.pallas_call(
        body, name=f"ag_start_{tag}", in_specs=[HBM] * n + [pl.BlockSpec(memory_space=pl.ANY)],
        out_specs=(SEM, SEM) + (HBM,) * n + (pl.BlockSpec(memory_space=pltpu.VMEM),),
        out_shape=(pltpu.SemaphoreType.DMA((3 * n,)), pltpu.SemaphoreType.DMA((3 * n,)))
        + tuple(pltpu.HBM(f.shape, f.dtype) for f in fulls) + (jax.ShapeDtypeStruct((8, 128), F32),),
        input_output_aliases={k: 2 + k for k in range(n)},
        compiler_params=pltpu.CompilerParams(has_side_effects=DATAFLOW),
    )(*[pltpu.with_memory_space_constraint(f, pltpu.HBM) for f in fulls], after)
    return res[0], res[1], list(res[2:2 + n]), res[2 + n]


def _gather_wait(send_sems, recv_sems, fulls, meta, after, tag, copies=_own_part_copies):
    n = len(fulls)

    def body(*refs):
        f_refs = refs[:n]
        send_sems, recv_sems = refs[n], refs[n + 1]
        for cp in copies(f_refs, meta, send_sems, recv_sems):
            cp.wait_send()
            cp.wait_recv()

    return pl.pallas_call(
        body, name=f"ag_wait_{tag}", in_specs=[HBM] * n + [SEM, SEM, pl.BlockSpec(memory_space=pl.ANY)],
        out_specs=[HBM] * n, out_shape=[pltpu.HBM(f.shape, f.dtype) for f in fulls],
        input_output_aliases={k: k for k in range(n)},
        compiler_params=pltpu.CompilerParams(has_side_effects=DATAFLOW),
    )(*fulls, send_sems, recv_sems, after)


def _scatter_copies(g_refs, land_refs, meta, send_sems, recv_sems):
    x, y, c = _mesh_pos()
    cps = []
    for k, (kind, L, A, B) in enumerate(meta):
        for j, (cx, cy) in enumerate(_other_chips(x, y)):
            cps.append(pltpu.make_async_remote_copy(
                src_ref=_chip_part(g_refs[k].at[0], kind, B, 2 * cx + cy), dst_ref=land_refs[k].at[j],
                send_sem=send_sems.at[3 * k + j], recv_sem=recv_sems.at[3 * k + j], device_id=(cx, cy, c),
                device_id_type=MESH))
    return cps


def _scatter_start(gs, meta, after, tag):
    n = len(gs)

    def body(*refs):
        send_sems, recv_sems = refs[2 * n + 1], refs[2 * n + 2]
        g_out = refs[2 * n + 3:3 * n + 3]
        lands = refs[3 * n + 3:4 * n + 3]
        token = refs[4 * n + 3]
        for cp in _scatter_copies(g_out, lands, meta, send_sems, recv_sems):
            cp.start()
        token[...] = jnp.zeros_like(token)

    land_shapes = [(3, A, B) for kind, L, A, B in meta]
    lands = [pltpu.with_memory_space_constraint(lax.empty(s, g.dtype), pltpu.HBM) for s, g in zip(land_shapes, gs)]
    res = pl.pallas_call(
        body, name=f"rs_start_{tag}", in_specs=[HBM] * (2 * n) + [pl.BlockSpec(memory_space=pl.ANY)],
        out_specs=(SEM, SEM) + (HBM,) * (2 * n) + (pl.BlockSpec(memory_space=pltpu.VMEM),),
        out_shape=(pltpu.SemaphoreType.DMA((3 * n,)), pltpu.SemaphoreType.DMA((3 * n,)))
        + tuple(pltpu.HBM(g.shape, g.dtype) for g in gs)
        + tuple(pltpu.HBM(s, g.dtype) for s, g in zip(land_shapes, gs)) + (jax.ShapeDtypeStruct((8, 128), F32),),
        input_output_aliases={k: 2 + k for k in range(2 * n)},
        compiler_params=pltpu.CompilerParams(has_side_effects=DATAFLOW),
    )(*[pltpu.with_memory_space_constraint(g, pltpu.HBM) for g in gs], *lands, after)
    return res[0], res[1], list(res[2:2 + n]), list(res[2 + n:2 + 2 * n]), res[2 + 2 * n]


def _scatter_wait(send_sems, recv_sems, gs, lands, meta, after, tag):
    n = len(gs)

    def body(*refs):
        g_refs, land_refs = refs[:n], refs[n:2 * n]
        send_sems, recv_sems = refs[2 * n], refs[2 * n + 1]
        for cp in _scatter_copies(g_refs, land_refs, meta, send_sems, recv_sems):
            cp.wait_send()
            cp.wait_recv()

    both = list(gs) + list(lands)
    res = pl.pallas_call(
        body, name=f"rs_wait_{tag}", in_specs=[HBM] * (2 * n) + [SEM, SEM, pl.BlockSpec(memory_space=pl.ANY)],
        out_specs=[HBM] * (2 * n), out_shape=[pltpu.HBM(a.shape, a.dtype) for a in both],
        input_output_aliases={k: k for k in range(2 * n)},
        compiler_params=pltpu.CompilerParams(has_side_effects=DATAFLOW),
    )(*both, send_sems, recv_sems, after)
    return list(res[:n]), list(res[n:])


def _chip_sum_full(g, got, m, chip_idx, *, name):
    kind, L, A, B = m
    ta = _div_tile(A, 256, 16)

    def body(r_ref, a_ref, b_ref, o_ref):
        acc = a_ref[...].astype(F32)
        for j in range(3):
            acc = acc + b_ref[j].astype(F32)
        o_ref[...] = acc

    if kind == 'col':
        g_spec = pl.BlockSpec((None, ta, B), lambda i, r: (0, i, r[0]))
    else:
        g_spec = pl.BlockSpec((None, None, ta, B), lambda i, r: (0, r[0], i, 0))
    grid_spec = pltpu.PrefetchScalarGridSpec(
        num_scalar_prefetch=1, grid=(A // ta,),
        in_specs=[g_spec, pl.BlockSpec((3, ta, B), lambda i, r: (0, i, 0))],
        out_specs=pl.BlockSpec((None, ta, B), lambda i, r: (0, i, 0)))
    return pl.pallas_call(
        body, name=name, grid_spec=grid_spec, out_shape=jax.ShapeDtypeStruct((1, A, B), F32),
        compiler_params=_cparams(("parallel",)),
    )(chip_idx, g, got)


def _swap_copies(src_refs, land_refs, send_sems, recv_sems):
    x, y, c = _mesh_pos()
    return [pltpu.make_async_remote_copy(src_ref=s, dst_ref=d, send_sem=send_sems.at[k], recv_sem=recv_sems.at[k],
                                         device_id=(x, y, 1 - c), device_id_type=MESH)
            for k, (s, d) in enumerate(zip(src_refs, land_refs))]


def _swap_start(sums, after, tag):
    n = len(sums)

    def body(*refs):
        send_sems, recv_sems = refs[2 * n + 1], refs[2 * n + 2]
        s_out = refs[2 * n + 3:3 * n + 3]
        lands = refs[3 * n + 3:4 * n + 3]
        token = refs[4 * n + 3]
        for cp in _swap_copies(s_out, lands, send_sems, recv_sems):
            cp.start()
        token[...] = jnp.zeros_like(token)

    lands = [pltpu.with_memory_space_constraint(lax.empty(s.shape, s.dtype), pltpu.HBM) for s in sums]
    res = pl.pallas_call(
        body, name=f"rs_swap_start_{tag}", in_specs=[HBM] * (2 * n) + [pl.BlockSpec(memory_space=pl.ANY)],
        out_specs=(SEM, SEM) + (HBM,) * (2 * n) + (pl.BlockSpec(memory_space=pltpu.VMEM),),
        out_shape=(pltpu.SemaphoreType.DMA((n,)), pltpu.SemaphoreType.DMA((n,)))
        + tuple(pltpu.HBM(s.shape, s.dtype) for s in sums) * 2 + (jax.ShapeDtypeStruct((8, 128), F32),),
        input_output_aliases={k: 2 + k for k in range(2 * n)},
        compiler_params=pltpu.CompilerParams(has_side_effects=DATAFLOW),
    )(*[pltpu.with_memory_space_constraint(s, pltpu.HBM) for s in sums], *lands, after)
    return res[0], res[1], list(res[2:2 + n]), list(res[2 + n:2 + 2 * n]), res[2 + 2 * n]


def _swap_wait(send_sems, recv_sems, sums, lands, after, tag):
    n = len(sums)

    def body(*refs):
        send_sems, recv_sems = refs[2 * n], refs[2 * n + 1]
        for cp in _swap_copies(refs[:n], refs[n:2 * n], send_sems, recv_sems):
            cp.wait_send()
            cp.wait_recv()

    both = list(sums) + list(lands)
    res = pl.pallas_call(
        body, name=f"rs_swap_wait_{tag}", in_specs=[HBM] * (2 * n) + [SEM, SEM, pl.BlockSpec(memory_space=pl.ANY)],
        out_specs=[HBM] * (2 * n), out_shape=[pltpu.HBM(a.shape, a.dtype) for a in both],
        input_output_aliases={k: k for k in range(2 * n)},
        compiler_params=pltpu.CompilerParams(has_side_effects=DATAFLOW),
    )(*both, send_sems, recv_sems, after)
    return list(res[:n]), list(res[n:])


def _pair_swap(sums, small, *, tag):
    ns = len(sums)
    with_small = small is not None
    n_in = ns + with_small

    def body(*refs):
        sum_refs = refs[:ns]
        got_refs = refs[n_in:n_in + ns]
        send_sems, recv_sems = refs[2 * n_in], refs[2 * n_in + 1]
        x, y, c = _mesh_pos()
        cps = []
        for k in range(ns):
            cp = pltpu.make_async_remote_copy(
                src_ref=sum_refs[k], dst_ref=got_refs[k], send_sem=send_sems.at[k], recv_sem=recv_sems.at[k],
                device_id=(x, y, 1 - c), device_id_type=MESH)
            cp.start()
            cps.append(cp)
        if with_small:
            small_ref, sm_ref, local_sem = refs[ns], refs[n_in + ns], refs[2 * n_in + 2]
            me = 4 * x + 2 * y + c
            own = pltpu.make_async_copy(small_ref, sm_ref.at[me], local_sem)
            own.start()
            for r in range(1, N_DEV):
                fx, fy, fc = (r >> 2) & 1, (r >> 1) & 1, r & 1
                peer = (1 - x if fx else x, 1 - y if fy else y, 1 - c if fc else c)
                cp = pltpu.make_async_remote_copy(
                    src_ref=small_ref, dst_ref=sm_ref.at[me], send_sem=send_sems.at[ns + r],
                    recv_sem=recv_sems.at[ns + r], device_id=peer, device_id_type=MESH)
                cp.start()
                cps.append(cp)
        for cp in cps:
            cp.wait()
        if with_small:
            own.wait()

    out_shape = [jax.ShapeDtypeStruct(s.shape, s.dtype) for s in sums]
    scratch = [pltpu.SemaphoreType.DMA((ns + N_DEV,)), pltpu.SemaphoreType.DMA((ns + N_DEV,))]
    args = list(sums)
    if with_small:
        out_shape.append(jax.ShapeDtypeStruct((N_DEV,) + small.shape, F32))
        scratch.append(pltpu.SemaphoreType.DMA)
        args.append(small)
    res = pl.pallas_call(
        body, name=f"rs_pair_swap_{tag}", in_specs=[ANY] * n_in, out_specs=[ANY] * n_in, out_shape=out_shape,
        scratch_shapes=scratch,
    )(*args)
    return list(res[:ns]), (res[ns] if with_small else None)


def _tiny_pack(conv_a_w, sc_conv_w):
    lead = conv_a_w.shape[:-2]
    sc = sc_conv_w.reshape(lead + (2 * SC_WIDTH, 128))
    z = lambda r: jnp.zeros(lead + (r, 128), F32)
    return jnp.concatenate([conv_a_w, z(32 - CONV_A_WIDTH), sc, z(TINY_ROWS - 32 - 2 * SC_WIDTH)], axis=-2)


def _tiny_unpack(t):
    lead = t.shape[:-2]
    return t[..., :CONV_A_WIDTH, :], t[..., 32:32 + 2 * SC_WIDTH, :].reshape(lead + (SC_WIDTH, 256))


def _pack_small(d):
    parts = []
    for n in SMALL_NAMES:
        flat = d[n].astype(F32).reshape(-1)
        parts.append(jnp.pad(flat, (0, -flat.shape[0] % 128)))
    flat = jnp.concatenate(parts)
    return jnp.pad(flat, (0, -flat.shape[0] % 1024)).reshape(-1, 128)


def _ffn_fwd(h, g, W, n_gu, n_down, i, tag):
    h2, u, gate, up, a = _ffn_fwd_fused(h, g, W[n_gu][i], W[n_down][i], 0, name=f"{tag}_fwd")
    return h2, (h, u, gate, up, a)


def _ffn_bwd(dh, saved, g, W, n_gu, n_down, i, G, tag):
    h, u, gate, up, a = saved
    dh_in, dg, dgate, dup = _ffn_bwd_fused(dh, h, g, gate, up, W[n_gu][i], W[n_down][i], 0, name=f"{tag}_bwd")
    G[(n_down, i)] = _mm(a, dh, name=f"{tag}_b_wdown", ta=True, tm=1408, tn=1024, tk=1024, scale=0.5)
    tn = FFN_CHUNK
    half = _mm(u, dgate, name=f"{tag}_b_wg", ta=True, tm=1024, tn=tn, tk=2048, stack=(1, 0, None, 2 * D_FF))
    G[(n_gu, i)] = _mm(u, dup, name=f"{tag}_b_wu", ta=True, tm=1024, tn=tn, tk=2048, stack=(1, 0, half, 2 * D_FF),
                       n_map=lambda j: j + D_FF // tn)
    return dh_in, dg


def _xa_block_fwd(h, mem, g, gm, W, i, tag):
    mn = _rms_fwd(mem, gm, name=f"{tag}_mem_norm")
    kv = _mm(mn, W['xa_wkv'][i], b_layer=0, name=f"{tag}_kv", tm=256, tn=1024, tk=1024)
    h2, u, q, o = _xa_fwd(h, g, W['xa_wq'][i], W['xa_wo'][i], 0, kv, name=f"{tag}_fwd")
    return h2, (h, u, mn, q, kv, o)


def _xa_block_bwd(dh, saved, mem, g, gm, W, i, G, tag):
    h, u, mn, q, kv, o = saved
    do = _mm(dh, W['xa_wo'][i], b_layer=0, name=f"{tag}_b_do", tb=True, tm=1024, tn=1024, tk=1024)
    G[('xa_wo', i)] = _mm(o, dh, name=f"{tag}_b_wo", ta=True, tm=1024, tn=1024, tk=1024)
    dq, dkv = _xa_bwd(q, kv, do, name=f"{tag}_b_attn")
    G[('xa_wq', i)] = _mm(u, dq, name=f"{tag}_b_wq", ta=True, tm=1024, tn=1024, tk=1024)
    dh_in, dg = _mm_norm_bwd(dq, W['xa_wq'][i], 0, h, g, dh, name=f"{tag}_b_du", tk=1024)
    G[('xa_wkv', i)] = _mm(mn, dkv, name=f"{tag}_b_wkv", ta=True, tm=1024, tn=1024, tk=256)
    dmn = _mm(dkv, W['xa_wkv'][i], b_layer=0, name=f"{tag}_b_dmn", tb=True, out_dtype=F32, tm=256, tn=1024, tk=1024)
    _, dgm = _rms_bwd(mem, gm, dmn, None, name=f"{tag}_b_mem_norm")
    return dh_in, dg, dgm


def _pad_conv_w(w, rows):
    return jnp.pad(w.astype(F32), ((0, rows - w.shape[0]), (0, 0)))


def _even_fwd(h, g, W, conv_w, conv_b, ln_g, ln_b, sinks, tag):
    u, z = _norm_mm(h, g, W['even_w_in'][0], 0, name=f"{tag}_in")
    c, cat = _conv_a_fwd(z, conv_w, conv_b, ln_g, ln_b, name=f"{tag}_conv")
    kpad = jnp.pad(z[:, 1536:1664], ((WINDOW, 0), (0, 0)))
    vpad = jnp.pad(z[:, 1664:1792], ((WINDOW, 0), (0, 0)))
    cat = _swa_fwd(z, kpad, vpad, sinks, cat, name=f"{tag}_swa")
    h2 = _mm(cat, W['even_w_out'][0], b_layer=0, name=f"{tag}_out", out_dtype=F32, tm=1024, tn=1024, tk=1024, res=h)
    return h2, (h, u, z, c, kpad, vpad, cat)


def _even_bwd(dh, saved, g, W, conv_w, ln_g, ln_b, sinks, G, tag):
    h, u, z, c, kpad, vpad, cat = saved
    dcat = _mm(dh, W['even_w_out'][0], b_layer=0, name=f"{tag}_b_dcat", tb=True, tm=1024, tn=1024, tk=1024)
    G[('even_w_out', 0)] = _mm(cat, dh, name=f"{tag}_b_wout", ta=True, tm=1024, tn=1024, tk=1024)
    dz_a, small = _conv_a_bwd(z, c, dcat, conv_w, ln_g, ln_b, name=f"{tag}_b_conv")
    dq, dkp, dvp, dsinks = _swa_bwd(z, kpad, vpad, sinks, dcat, name=f"{tag}_b_swa")
    dz = jnp.concatenate([dz_a, dq, dkp[WINDOW:].astype(BF16), dvp[WINDOW:].astype(BF16)], axis=-1)
    G[('even_w_in', 0)] = _mm(u, dz, name=f"{tag}_b_win", ta=True, tm=1024, tn=1792, tk=1024)
    dh_in, dg = _mm_norm_bwd(dz, W['even_w_in'][0], 0, h, g, dh, name=f"{tag}_b_du", tk=1792)
    grads = dict(mix=dg, conv_a_w=small[:CONV_A_WIDTH], conv_a_b=small[32:33], conv_a_ln_g=small[33:34],
                 conv_a_ln_b=small[34:35], swa_sinks=dsinks[:, 0])
    return dh_in, grads


def _odd_fwd(h, g, W, conv_w, tag):
    u, z = _norm_mm(h, g, W['odd_w_in'][0], 0, name=f"{tag}_in")
    y, cc = _sconv_fwd(z, conv_w, name=f"{tag}_conv")
    h2 = _mm(y, W['odd_w_out'][0], b_layer=0, name=f"{tag}_out", out_dtype=F32, tm=1024, tn=1024, tk=1024, res=h)
    return h2, (h, u, z, y, cc)


def _odd_bwd(dh, saved, g, W, conv_w, G, tag):
    h, u, z, y, cc = saved
    dy = _mm(dh, W['odd_w_out'][0], b_layer=0, name=f"{tag}_b_dy", tb=True, tm=1024, tn=1024, tk=1024)
    G[('odd_w_out', 0)] = _mm(y, dh, name=f"{tag}_b_wout", ta=True, tm=1024, tn=1024, tk=1024)
    dz, dw = _sconv_bwd(z, cc, dy, conv_w, name=f"{tag}_b_conv")
    G[('odd_w_in', 0)] = _mm(u, dz, name=f"{tag}_b_win", ta=True, tm=1024, tn=1024, tk=1024)
    dh_in, dg = _mm_norm_bwd(dz, W['odd_w_in'][0], 0, h, g, dh, name=f"{tag}_b_du", tk=1024)
    return dh_in, dict(mix=dg, sc_conv_w=dw[:SC_WIDTH])


def _local_step(x, mem, tgt, W, need, token, ready, conv_a_w, sc_conv_w, P):
    row = lambda v: v.reshape(1, -1)
    conv_a_w = _pad_conv_w(conv_a_w, 32)
    sc_w = _pad_conv_w(sc_conv_w, 8)
    sinks = P['swa_sinks'][0]

    def arrive(stage, h):
        for n, ws in need(stage, h).items():
            W[n] = W.get(n, []) + ws

    h = x
    saved = []
    for i in range(2):
        t = f"l{i}"
        if i == 1:
            arrive('l1_ffn1', h)
        g1 = row(P['ffn1_norm'][i]) + (token if i == 0 else 0.0)
        h, s1 = _ffn_fwd(h, g1, W, 'ffn1_w_gu', 'ffn1_w_down', i, f"{t}_ffn1")
        arrive(f"{t}_mix", h)
        if i == 0:
            h, s2 = _even_fwd(h, row(P['mix_norm'][i]), W, conv_a_w, P['conv_a_b'], P['conv_a_ln_g'],
                              P['conv_a_ln_b'], sinks, f"{t}_even")
        else:
            h, s2 = _odd_fwd(h, row(P['mix_norm'][i]), W, sc_w, f"{t}_odd")
        h, s3 = _xa_block_fwd(h, mem, row(P['xa_norm'][i]), row(P['xa_mem_norm'][i]), W, i, f"{t}_xa")
        arrive(f"{t}_ffn2", h)
        h, s4 = _ffn_fwd(h, row(P['ffn2_norm'][i]), W, 'ffn2_w_gu', 'ffn2_w_down', i, f"{t}_ffn2")
        saved.append((s1, s2, s3, s4))

    loss, dh, d_final = _final_loss(h, row(P['final_norm']), tgt, name="final_loss")

    G = {}
    gp = {n: [None, None] for n in ('ffn1_norm', 'mix_norm', 'xa_norm', 'xa_mem_norm', 'ffn2_norm')}
    single = {}
    for i in (1, 0):
        t = f"l{i}"
        s1, s2, s3, s4 = saved[i]
        g4 = row(P['ffn2_norm'][i]) + (ready('l1', G) if i == 0 else 0.0)
        dh, gp['ffn2_norm'][i] = _ffn_bwd(dh, s4, g4, W, 'ffn2_w_gu', 'ffn2_w_down', i, G, f"{t}_ffn2")
        dh, gp['xa_norm'][i], gp['xa_mem_norm'][i] = _xa_block_bwd(
            dh, s3, mem, row(P['xa_norm'][i]), row(P['xa_mem_norm'][i]), W, i, G, f"{t}_xa")
        if i == 0:
            dh, g2 = _even_bwd(dh, s2, row(P['mix_norm'][i]), W, conv_a_w, P['conv_a_ln_g'], P['conv_a_ln_b'], sinks,
                               G, f"{t}_even")
        else:
            dh, g2 = _odd_bwd(dh, s2, row(P['mix_norm'][i]), W, sc_w, G, f"{t}_odd")
        gp['mix_norm'][i] = g2.pop('mix')
        single.update(g2)
        g1 = row(P['ffn1_norm'][i]) + (ready('l0_rest', G) if i == 0 else 0.0)
        dh, gp['ffn1_norm'][i] = _ffn_bwd(dh, s1, g1, W, 'ffn1_w_gu', 'ffn1_w_down', i, G, f"{t}_ffn1")

    small = {n: jnp.concatenate(v, axis=0) for n, v in gp.items()}
    small['conv_a_b'] = single['conv_a_b']
    small['conv_a_ln_g'] = single['conv_a_ln_g']
    small['conv_a_ln_b'] = single['conv_a_ln_b']
    small['swa_sinks'] = single['swa_sinks'][None]
    small['final_norm'] = d_final[0]
    small['conv_a_w'] = single['conv_a_w']
    small['sc_conv_w'] = single['sc_conv_w']
    return loss[0, 0], dh, G, small


def kernel(x, mem, ffn1_norm, ffn1_w_gu, ffn1_w_down, mix_norm, even_w_in, conv_a_w, conv_a_b, conv_a_ln_g, conv_a_ln_b, swa_sinks, even_w_out, odd_w_in, sc_conv_w, odd_w_out, xa_norm, xa_mem_norm, xa_wq, xa_wkv, xa_wo, ffn2_norm, ffn2_w_gu, ffn2_w_down, final_norm, loss_target, m_ffn1_norm, m_ffn1_w_gu, m_ffn1_w_down, m_mix_norm, m_even_w_in, m_conv_a_w, m_conv_a_b, m_conv_a_ln_g, m_conv_a_ln_b, m_swa_sinks, m_even_w_out, m_odd_w_in, m_sc_conv_w, m_odd_w_out, m_xa_norm, m_xa_mem_norm, m_xa_wq, m_xa_wkv, m_xa_wo, m_ffn2_norm, m_ffn2_w_gu, m_ffn2_w_down, m_final_norm, v_ffn1_norm, v_ffn1_w_gu, v_ffn1_w_down, v_mix_norm, v_even_w_in, v_conv_a_w, v_conv_a_b, v_conv_a_ln_g, v_conv_a_ln_b, v_swa_sinks, v_even_w_out, v_odd_w_in, v_sc_conv_w, v_odd_w_out, v_xa_norm, v_xa_mem_norm, v_xa_wq, v_xa_wkv, v_xa_wo, v_ffn2_norm, v_ffn2_w_gu, v_ffn2_w_down, v_final_norm):
    w = dict(zip(WEIGHT_NAMES, (ffn1_norm, ffn1_w_gu, ffn1_w_down, mix_norm, even_w_in, conv_a_w, conv_a_b, conv_a_ln_g, conv_a_ln_b, swa_sinks, even_w_out, odd_w_in, sc_conv_w, odd_w_out, xa_norm, xa_mem_norm, xa_wq, xa_wkv, xa_wo, ffn2_norm, ffn2_w_gu, ffn2_w_down, final_norm)))
    m = dict(zip(WEIGHT_NAMES, (m_ffn1_norm, m_ffn1_w_gu, m_ffn1_w_down, m_mix_norm, m_even_w_in, m_conv_a_w, m_conv_a_b, m_conv_a_ln_g, m_conv_a_ln_b, m_swa_sinks, m_even_w_out, m_odd_w_in, m_sc_conv_w, m_odd_w_out, m_xa_norm, m_xa_mem_norm, m_xa_wq, m_xa_wkv, m_xa_wo, m_ffn2_norm, m_ffn2_w_gu, m_ffn2_w_down, m_final_norm)))
    v = dict(zip(WEIGHT_NAMES, (v_ffn1_norm, v_ffn1_w_gu, v_ffn1_w_down, v_mix_norm, v_even_w_in, v_conv_a_w, v_conv_a_b, v_conv_a_ln_g, v_conv_a_ln_b, v_swa_sinks, v_even_w_out, v_odd_w_in, v_sc_conv_w, v_odd_w_out, v_xa_norm, v_xa_mem_norm, v_xa_wq, v_xa_wkv, v_xa_wo, v_ffn2_norm, v_ffn2_w_gu, v_ffn2_w_down, v_final_norm)))
    cx, cy, cc = lax.axis_index("x"), lax.axis_index("y"), lax.axis_index("c")
    chip_idx = (2 * cx + cy).astype(jnp.int32).reshape(1)

    shards = {n: w[n] for n in COMM_NAMES if n != 'tiny'}
    shards['tiny'] = _tiny_pack(conv_a_w, sc_conv_w)
    first =[('ffn1_w_gu', 0), ('ffn1_w_down', 0), ('tiny', 0)]
    stages = {
        'l0_mix': [('even_w_in', 0), ('even_w_out', 0), ('xa_wq', 0), ('xa_wkv', 0), ('xa_wo', 0)],
        'l0_ffn2': [('ffn2_w_gu', 0), ('ffn2_w_down', 0)],
        'l1_ffn1': [('ffn1_w_gu', 1), ('ffn1_w_down', 1)],
        'l1_mix': [('odd_w_in', 0), ('odd_w_out', 0), ('xa_wq', 1), ('xa_wkv', 1), ('xa_wo', 1)],
        'l1_ffn2': [('ffn2_w_gu', 1), ('ffn2_w_down', 1)],
    }
    grad_stages = {'l1': stages['l1_ffn1'] + stages['l1_mix'] + stages['l1_ffn2'],
                   'l0_rest': stages['l0_mix'] + stages['l0_ffn2'], 'l0_ffn1': first}

    def place(items, after=None):
        out = []
        for n, l in items:
            out.append(_place(shards[n], l, LAYOUT[n], chip_idx, F32 if n == 'tiny' else BF16,
                              name=f"place_{n}_{l}", after=after))
            after = out[-1] if after is not None else None
        return out

    def item_meta(items):
        return [(LAYOUT[n], 1) + shards[n].shape[1:] for n, l in items]

    def natural(items, arrays):
        out = {}
        for (n, l), a in zip(items, arrays):
            if n == 'tiny':
                continue
            if n == 'even_w_in':
                out[n] = [a.transpose(0, 2, 1, 3).reshape(1, D_MODEL, -1)]
            else:
                out[n] = [a if LAYOUT[n] == 'col' else a.reshape(1, N_CHIPS * a.shape[2], a.shape[3])]
        return out

    meta_first = item_meta(first)
    send, recv, in_flight, token = _gather_start(place(first), meta_first, chip_idx, "first_ici", _half_part_copies)
    placed, last = {}, token
    for stage, items in stages.items():
        placed[stage] = place(items, last)
        last = placed[stage][-1]
    landed = _gather_wait(send, recv, in_flight, meta_first, last, "first_ici", _half_part_copies)
    send, recv, in_flight, token = _gather_start(landed, meta_first, token, "first_d2d", _forward_copies)
    first_d2d = (send, recv, in_flight)
    gathers = {}
    for stage, items in stages.items():
        send, recv, in_flight, token = _gather_start(placed[stage], item_meta(items), token, stage)
        gathers[stage] = (send, recv, in_flight)
    first_full = _gather_wait(*first_d2d, meta_first, token, "first_d2d", _forward_copies)
    W = natural(first, first_full)
    ca, sc = _tiny_unpack(first_full[-1][0])
    conv_a_full = ca.transpose(1, 0, 2).reshape(CONV_A_WIDTH, CONV_A_CH)
    sc_full = sc.transpose(1, 0, 2).reshape(SC_WIDTH, SC_CH)

    def need(stage, h):
        send, recv, in_flight = gathers[stage]
        items = stages[stage]
        return natural(items, _gather_wait(send, recv, in_flight, item_meta(items), h, stage))

    def gathered_layout(G, item):
        n, l = item
        A, B = shards[n].shape[1:]
        g = G[item]
        if n == 'tiny':
            return g
        if n == 'even_w_in':
            return g.reshape(A, N_CHIPS, B).transpose(1, 0, 2)[None]
        return g.reshape(1, A, N_CHIPS * B) if LAYOUT[n] == 'col' else g.reshape(1, N_CHIPS, A, B)

    scatters, tokens = {}, {}

    def ready(stage, G):
        items = grad_stages[stage]
        send, recv, gs1, lands, tok = _scatter_start([gathered_layout(G, it) for it in items], item_meta(items),
                                                     chip_idx, stage)
        scatters[stage] = (send, recv, gs1, lands)
        tokens[stage] = tok
        return tok[:1, :1]

    loss_part, grad_x, G, g_small = _local_step(x[0], mem[0], loss_target[0], W, need, token[:1, :1], ready,
                                                conv_a_full, sc_full, {n: w[n] for n in SMALL_NAMES})
    G[('tiny', 0)] = _tiny_pack(g_small['conv_a_w'].reshape(CONV_A_WIDTH, N_CHIPS, 128).transpose(1, 0, 2),
                                g_small['sc_conv_w'].reshape(SC_WIDTH, N_CHIPS, 256).transpose(1, 0, 2))[None]
    loss = lax.psum(loss_part, ("x", "y", "c"))

    ready('l0_ffn1', G)
    started = tokens['l0_ffn1']

    def summed(stage, after):
        send, recv, gs1, lands = scatters[stage]
        items = grad_stages[stage]
        sent, landed = _scatter_wait(send, recv, gs1, lands, item_meta(items), after, stage)
        return items, [_chip_sum_full(g, r, m_, chip_idx, name=f"rs_chip_sum_{n}_{l}")
                       for (n, l), g, r, m_ in zip(items, sent, landed, item_meta(items))]

    def adamw(n, g1):
        if n == 'tiny':
            pk = lambda d: _tiny_pack(d['conv_a_w'], d['sc_conv_w'])
            res = [_tiny_unpack(a) for a in _adamw_layers(pk(w), pk(m), pk(v), [g1[('tiny', 0)]], name="adamw_tiny")]
            for k, nn in enumerate(('conv_a_w', 'sc_conv_w')):
                grads[nn], deltas[nn], new_m[nn], new_v[nn] = (r[k] for r in res)
        else:
            gsrc = [g1[(n, l)] for l in range(w[n].shape[0])]
            grads[n], deltas[n], new_m[n], new_v[n] = _adamw_layers(w[n], m[n], v[n], gsrc, name=f"adamw_{n}")

    grads, deltas, new_m, new_v = {}, {}, {}, {}
    sum_of = {}
    for stage in ('l1', 'l0_rest'):
        its, ss = summed(stage, started)
        sum_of.update(zip(its, ss))
    swap_groups = [['even_w_in', 'even_w_out', 'odd_w_in', 'odd_w_out', 'xa_wq', 'xa_wkv', 'xa_wo'],
                   ['ffn2_w_gu', 'ffn2_w_down'], ['ffn1_w_gu', 'ffn1_w_down']]
    swaps, after = [], started
    for gi, names in enumerate(swap_groups):
        its = [it for it in sum_of if it[0] in names]
        send, recv, own, lands, after = _swap_start([sum_of[it] for it in its], after, f"g{gi}")
        swaps.append((its, send, recv, own, lands))
    _, small_parts = _pair_swap([], _pack_small(g_small), tag="small")
    g1 = {}

    def swapped(gi, after):
        its, send, recv, own, lands = swaps[gi]
        mine, theirs = _swap_wait(send, recv, own, lands, after, f"g{gi}")
        g1.update({it: [a, b] for it, a, b in zip(its, mine, theirs)})

    for gi in (0, 1):
        swapped(gi, after)
        for n in swap_groups[gi]:
            adamw(n, g1)
        after = deltas[swap_groups[gi][-1]]

    swapped(2, after)
    its, ss = summed('l0_ffn1', after)
    sib, _ = _pair_swap(ss, None, tag="last")
    g1.update({it: [a, b] for it, a, b in zip(its, ss, sib)})
    for n in ('ffn1_w_gu', 'ffn1_w_down', 'tiny'):
        adamw(n, g1)
    rows2d = lambda d: [d[n].reshape(-1, d[n].shape[-1]) for n in SMALL_NAMES]
    for dst, arrs in zip((grads, deltas, new_m, new_v),
                         _adamw_small(rows2d(w), rows2d(m), rows2d(v), small_parts, name="adamw_small")):
        dst.update({n: a.reshape(w[n].shape) for n, a in zip(SMALL_NAMES, arrs)})

    return (loss, grad_x[None], *[grads[n] for n in WEIGHT_NAMES], *[deltas[n] for n in WEIGHT_NAMES],
            *[new_m[n] for n in WEIGHT_NAMES], *[new_v[n] for n in WEIGHT_NAMES])
```

```python
import jax
import jax.numpy as jnp
from jax import lax
from jax.experimental import pallas as pl
from jax.experimental.pallas import tpu as pltpu

F32 = jnp.float32
BF16 = jnp.bfloat16

D_MODEL = 1024
D_FF = 2816
CONV_A_CH = 512
CONV_A_WIDTH = 31
SWA_HEADS = 8
SWA_KV_HEADS = 2
SWA_GROUP = 4
HEAD_DIM = 64
WINDOW = 128
SC_CH = 1024
SC_WIDTH = 3
XA_HEADS = 4
XA_HEAD_DIM = 256
RMS_EPS = 1e-6
LN_EPS = 1e-5

ADAM_LR = 0.001
ADAM_B1 = 0.9
ADAM_B2 = 0.999
ADAM_EPS = 1e-08
ADAM_WD = 0.01
ADAM_STEP = 10
ADAM_TILE_ELEMS = 384 * 1024

N_CHIPS = 4
N_DEV = 8
NEG_BIG = -1e30
VMEM_LIMIT = 56 * 1024 * 1024
MESH = pl.DeviceIdType.MESH

INPUT_NAMES = ['x', 'mem', 'ffn1_norm', 'ffn1_w_gu', 'ffn1_w_down', 'mix_norm', 'even_w_in', 'conv_a_w', 'conv_a_b',
               'conv_a_ln_g', 'conv_a_ln_b', 'swa_sinks', 'even_w_out', 'odd_w_in', 'sc_conv_w', 'odd_w_out', 'xa_norm',
               'xa_mem_norm', 'xa_wq', 'xa_wkv', 'xa_wo', 'ffn2_norm', 'ffn2_w_gu', 'ffn2_w_down', 'final_norm']
WEIGHT_NAMES = INPUT_NAMES[2:]
BIG = [('ffn1_w_gu', 'col'), ('ffn1_w_down', 'row'), ('even_w_in', 'col'), ('conv_a_w', 'col'), ('even_w_out', 'row'),
       ('odd_w_in', 'col'), ('sc_conv_w', 'col'), ('odd_w_out', 'row'), ('xa_wq', 'row'), ('xa_wkv', 'col'),
       ('xa_wo', 'row'), ('ffn2_w_gu', 'col'), ('ffn2_w_down', 'row')]
BIG_NAMES = [n for n, _ in BIG]
SMALL_NAMES = [n for n in WEIGHT_NAMES if n not in BIG_NAMES]


def _cparams(sem=None, vmem=VMEM_LIMIT):
    kw = dict(vmem_limit_bytes=vmem)
    if sem is not None:
        kw['dimension_semantics'] = sem
    return pltpu.CompilerParams(**kw)


def _div_tile(n, want, align=8):
    if n <= want:
        return n
    t = (want // align) * align
    while t >= align:
        if n % t == 0:
            return t
        t -= align
    return n


def _mm(a, b, *, name, ta=False, tb=False, out_dtype=BF16, tm=512, tn=512, tk=512, res=None, scale=1.0,
        b_layer=None, stack=None, n_map=None):
    n_map = n_map or (lambda j: j)
    if ta:
        K, M = a.shape
    else:
        M, K = a.shape
    if tb:
        N, K2 = b.shape[-2:]
    else:
        K2, N = b.shape[-2:]
    assert K == K2, (a.shape, b.shape, ta, tb)
    tm = _div_tile(M, tm, 128 if ta else 16)
    tn = _div_tile(N, tn, 128)
    tk = _div_tile(K, tk, 16 if ta else 128)
    nk = K // tk
    a_spec = pl.BlockSpec((tk, tm), lambda i, j, k: (k, i)) if ta else pl.BlockSpec((tm, tk), lambda i, j, k: (i, k))
    if b_layer is None:
        b_spec = pl.BlockSpec((tn, tk), lambda i, j, k: (j, k)) if tb else pl.BlockSpec((tk, tn), lambda i, j, k: (k, j))
    elif tb:
        b_spec = pl.BlockSpec((None, tn, tk), lambda i, j, k: (b_layer, j, k))
    else:
        b_spec = pl.BlockSpec((None, tk, tn), lambda i, j, k: (b_layer, k, j))
    o_spec = pl.BlockSpec((tm, tn), lambda i, j, k: (i, j))
    out_shape = jax.ShapeDtypeStruct((M, N), out_dtype)
    out_spec = o_spec
    aliases = {}
    extra_specs, extra_args = [], ()
    if stack is not None:
        n_layers, layer, buf = stack[:3]
        n_total = stack[3] if len(stack) > 3 else N
        out_shape = jax.ShapeDtypeStruct((n_layers, M, n_total), out_dtype)
        out_spec = pl.BlockSpec((None, tm, tn), lambda i, j, k: (layer, i, n_map(j)))
        if buf is not None:
            extra_specs, extra_args = [pl.BlockSpec(memory_space=pl.ANY)], (buf,)
            aliases = {2 + (res is not None): 0}
    dims = (((0 if ta else 1,), (1 if tb else 0,)), ((), ()))
    has_res = res is not None
    n_extra = len(extra_args)

    def body(*refs):
        if n_extra:
            refs = refs[:2 + has_res] + refs[2 + has_res + n_extra:]
        if has_res:
            a_ref, b_ref, r_ref, o_ref, acc_ref = refs
        else:
            a_ref, b_ref, o_ref, acc_ref = refs
        k = pl.program_id(2)
        p = lax.dot_general(a_ref[...].astype(BF16), b_ref[...].astype(BF16), dims, preferred_element_type=F32)

        @pl.when(k == 0)
        def _():
            acc_ref[...] = p

        @pl.when(k > 0)
        def _():
            acc_ref[...] += p

        @pl.when(k == nk - 1)
        def _():
            r = acc_ref[...] * scale
            if has_res:
                r = r_ref[...] + r
            o_ref[...] = r.astype(out_dtype)

    in_specs = [a_spec, b_spec] + ([o_spec] if has_res else []) + extra_specs
    args = (a, b) + ((res,) if has_res else ()) + extra_args
    return pl.pallas_call(
        body, name=name, grid=(M // tm, N // tn, nk), in_specs=in_specs, out_specs=out_spec,
        out_shape=out_shape, input_output_aliases=aliases,
        scratch_shapes=[pltpu.VMEM((tm, tn), F32)],
        compiler_params=_cparams(("parallel", "parallel", "arbitrary")),
    )(*args)


def _rms_fwd(x, g, *, name):
    S, D = x.shape
    ts = _div_tile(S, 512)

    def body(x_ref, g_ref, o_ref):
        xv = x_ref[...]
        r = lax.rsqrt(jnp.mean(xv * xv, axis=-1, keepdims=True) + RMS_EPS)
        o_ref[...] = (xv * r * g_ref[...]).astype(BF16)

    return pl.pallas_call(
        body, name=name, grid=(S // ts,),
        in_specs=[pl.BlockSpec((ts, D), lambda i: (i, 0)), pl.BlockSpec((1, D), lambda i: (0, 0))],
        out_specs=pl.BlockSpec((ts, D), lambda i: (i, 0)),
        out_shape=jax.ShapeDtypeStruct((S, D), BF16),
        compiler_params=_cparams(("parallel",)),
    )(x, g)


NORM_SLAB = 256


def _norm_mm(h, g, w, layer, *, name):
    S, D = h.shape
    N = w.shape[-1]
    tm = _div_tile(S, 1024, NORM_SLAB)
    slab = min(NORM_SLAB, tm)

    def body(h_ref, g_ref, w_ref, u_ref, z_ref):
        for r0 in range(0, tm, slab):
            rows = pl.ds(r0, slab)
            xv = h_ref[rows, :]
            r = lax.rsqrt(jnp.mean(xv * xv, axis=-1, keepdims=True) + RMS_EPS)
            u = (xv * r * g_ref[...]).astype(BF16)
            u_ref[rows, :] = u
            z_ref[rows, :] = jnp.dot(u, w_ref[...], preferred_element_type=F32).astype(BF16)

    row = pl.BlockSpec((tm, D), lambda i: (i, 0))
    return pl.pallas_call(
        body, name=name, grid=(S // tm,),
        in_specs=[row, pl.BlockSpec((1, D), lambda i: (0, 0)), pl.BlockSpec((None, D, N), lambda i: (layer, 0, 0))],
        out_specs=[row, pl.BlockSpec((tm, N), lambda i: (i, 0))],
        out_shape=[jax.ShapeDtypeStruct((S, D), BF16), jax.ShapeDtypeStruct((S, N), BF16)],
        compiler_params=_cparams(("parallel",)),
    )(h, g, w)


def _mm_norm_bwd(dz, w, layer, h, g, dres, *, name, tk):
    S, K = dz.shape
    D = h.shape[1]
    tm = _div_tile(S, 1024, NORM_SLAB)
    slab = min(NORM_SLAB, tm)
    tk = _div_tile(K, tk, 128)
    nk = K // tk
    nt = (((1,), (1,)), ((), ()))

    def body(dz_ref, w_ref, h_ref, g_ref, dr_ref, dx_ref, dg_ref, acc):
        i = pl.program_id(0)
        k = pl.program_id(1)

        def norm_bwd(du_of):
            part = jnp.zeros((1, D), F32)
            for r0 in range(0, tm, slab):
                rows = pl.ds(r0, slab)
                du = du_of(rows)
                xv = h_ref[rows, :]
                r = lax.rsqrt(jnp.mean(xv * xv, axis=-1, keepdims=True) + RMS_EPS)
                xhat = xv * r
                part = part + jnp.sum(du * xhat, axis=0, keepdims=True)
                dxhat = du * g_ref[...]
                dx_ref[rows, :] = dr_ref[rows, :] + r * (
                    dxhat - xhat * jnp.mean(dxhat * xhat, axis=-1, keepdims=True))

            @pl.when(i == 0)
            def _():
                dg_ref[...] = part

            @pl.when(i > 0)
            def _():
                dg_ref[...] += part

        if nk == 1:
            norm_bwd(lambda rows: lax.dot_general(dz_ref[rows, :], w_ref[...], nt, preferred_element_type=F32))
        else:
            p = lax.dot_general(dz_ref[...], w_ref[...], nt, preferred_element_type=F32)

            @pl.when(k == 0)
            def _():
                acc[...] = p

            @pl.when(k > 0)
            def _():
                acc[...] += p

            @pl.when(k == nk - 1)
            def _():
                norm_bwd(lambda rows: acc[rows, :])

    row = pl.BlockSpec((tm, D), lambda i, k: (i, 0))
    vec = pl.BlockSpec((1, D), lambda i, k: (0, 0))
    return pl.pallas_call(
        body, name=name, grid=(S // tm, nk),
        in_specs=[pl.BlockSpec((tm, tk), lambda i, k: (i, k)), pl.BlockSpec((None, D, tk), lambda i, k: (layer, 0, k)),
                  row, vec, row],
        out_specs=[row, vec],
        out_shape=[jax.ShapeDtypeStruct((S, D), F32), jax.ShapeDtypeStruct((1, D), F32)],
        scratch_shapes=[pltpu.VMEM((tm, D), F32)],
        compiler_params=_cparams(("arbitrary", "arbitrary")),
    )(dz, w, h, g, dres)


def _rms_bwd(x, g, du, dres, *, name):
    S, D = x.shape
    ts = _div_tile(S, 512)
    has_res = dres is not None

    def body(*refs):
        if has_res:
            x_ref, g_ref, du_ref, dr_ref, dx_ref, dg_ref = refs
        else:
            x_ref, g_ref, du_ref, dg_ref = refs
        i = pl.program_id(0)
        xv = x_ref[...]
        duv = du_ref[...].astype(F32)
        r = lax.rsqrt(jnp.mean(xv * xv, axis=-1, keepdims=True) + RMS_EPS)
        xhat = xv * r
        part = jnp.sum(duv * xhat, axis=0, keepdims=True)

        @pl.when(i == 0)
        def _():
            dg_ref[...] = part

        @pl.when(i > 0)
        def _():
            dg_ref[...] += part

        if has_res:
            dxhat = duv * g_ref[...]
            dx = r * (dxhat - xhat * jnp.mean(dxhat * xhat, axis=-1, keepdims=True))
            dx_ref[...] = dr_ref[...] + dx

    row = pl.BlockSpec((ts, D), lambda i: (i, 0))
    vec = pl.BlockSpec((1, D), lambda i: (0, 0))
    if has_res:
        dx, dg = pl.pallas_call(
            body, name=name, grid=(S // ts,), in_specs=[row, vec, row, row], out_specs=[row, vec],
            out_shape=[jax.ShapeDtypeStruct((S, D), F32), jax.ShapeDtypeStruct((1, D), F32)],
            compiler_params=_cparams(("arbitrary",)),
        )(x, g, du, dres)
        return dx, dg
    dg = pl.pallas_call(
        body, name=name, grid=(S // ts,), in_specs=[row, vec, row], out_specs=vec,
        out_shape=jax.ShapeDtypeStruct((1, D), F32),
        compiler_params=_cparams(("arbitrary",)),
    )(x, g, du)
    return None, dg


def _final_loss(h, g, tgt, *, name):
    S, D = h.shape
    ts = _div_tile(S, 512)

    def body(h_ref, g_ref, t_ref, loss_ref, dh_ref, dg_ref):
        i = pl.program_id(0)
        xv = h_ref[...]
        gv = g_ref[...]
        r = lax.rsqrt(jnp.mean(xv * xv, axis=-1, keepdims=True) + RMS_EPS)
        xhat = xv * r
        err = xhat * gv - t_ref[...]
        lpart = 0.5 * jnp.sum(jnp.mean(err * err, axis=-1, keepdims=True), axis=0, keepdims=True)
        dy = err * (1.0 / D)
        gpart = jnp.sum(dy * xhat, axis=0, keepdims=True)

        @pl.when(i == 0)
        def _():
            loss_ref[...] = jnp.broadcast_to(lpart, loss_ref.shape)
            dg_ref[...] = gpart

        @pl.when(i > 0)
        def _():
            loss_ref[...] += jnp.broadcast_to(lpart, loss_ref.shape)
            dg_ref[...] += gpart

        dxhat = dy * gv
        dh_ref[...] = r * (dxhat - xhat * jnp.mean(dxhat * xhat, axis=-1, keepdims=True))

    row = pl.BlockSpec((ts, D), lambda i: (i, 0))
    vec = pl.BlockSpec((1, D), lambda i: (0, 0))
    return pl.pallas_call(
        body, name=name, grid=(S // ts,), in_specs=[row, vec, row],
        out_specs=[pl.BlockSpec((8, 128), lambda i: (0, 0)), row, vec],
        out_shape=[jax.ShapeDtypeStruct((8, 128), F32), jax.ShapeDtypeStruct((S, D), F32),
                   jax.ShapeDtypeStruct((1, D), F32)],
        compiler_params=_cparams(("arbitrary",)),
    )(h, g, tgt)


def _sigmoid(x):
    return 1.0 / (1.0 + jnp.exp(-x))


FFN_CHUNK = 1408
FFN_BWD_PIECE = 768


def _ffn_fwd_fused(h, g, w_gu, w_down, layer, *, name):
    S, D = h.shape
    tm = _div_tile(S, 256, 16)
    tf = FFN_CHUNK

    def body(h_ref, g_ref, wgu_ref, wd_ref, h2_ref, u_ref, gate_ref, up_ref, a_ref):
        xv = h_ref[...]
        r = lax.rsqrt(jnp.mean(xv * xv, axis=-1, keepdims=True) + RMS_EPS)
        u = (xv * r * g_ref[...]).astype(BF16)
        u_ref[...] = u
        y = None
        for c0 in range(0, D_FF, tf):
            cols = pl.ds(c0, tf)
            gate = jnp.dot(u, wgu_ref[:, cols], preferred_element_type=F32)
            up = jnp.dot(u, wgu_ref[:, pl.ds(D_FF + c0, tf)], preferred_element_type=F32)
            gate_ref[:, cols] = gate.astype(BF16)
            up_ref[:, cols] = up.astype(BF16)
            a = (gate * _sigmoid(gate) * up).astype(BF16)
            a_ref[:, cols] = a
            p = jnp.dot(a, wd_ref[cols, :], preferred_element_type=F32)
            y = p if y is None else y + p
        h2_ref[...] = xv + 0.5 * y

    row = pl.BlockSpec((tm, D), lambda i: (i, 0))
    wide = pl.BlockSpec((tm, D_FF), lambda i: (i, 0))
    hidden = jax.ShapeDtypeStruct((S, D_FF), BF16)
    return pl.pallas_call(
        body, name=name, grid=(S // tm,),
        in_specs=[row, pl.BlockSpec((1, D), lambda i: (0, 0)),
                  pl.BlockSpec((None, D, 2 * D_FF), lambda i: (layer, 0, 0)),
                  pl.BlockSpec((None, D_FF, D), lambda i: (layer, 0, 0))],
        out_specs=[row, row, wide, wide, wide],
        out_shape=[jax.ShapeDtypeStruct((S, D), F32), jax.ShapeDtypeStruct((S, D), BF16), hidden, hidden, hidden],
        compiler_params=_cparams(("parallel",), vmem=60 * 1024 * 1024),
    )(h, g, w_gu, w_down)


def _ffn_bwd_fused(dh, h, g, gate, up, w_gu, w_down, layer, *, name):
    S, D = h.shape
    tm = _div_tile(S, 256, 16)
    nt = (((1,), (1,)), ((), ()))
    pieces = [(c0, min(FFN_BWD_PIECE, D_FF - c0)) for c0 in range(0, D_FF, FFN_BWD_PIECE)]

    def body(dh_ref, h_ref, g_ref, gate_ref, up_ref, wgu_ref, wd_ref, dx_ref, dg_ref, dgate_ref, dup_ref):
        i = pl.program_id(0)
        dhv = dh_ref[...]
        dy = (0.5 * dhv).astype(BF16)
        du = None
        for c0, cw in pieces:
            cols = pl.ds(c0, cw)
            da = lax.dot_general(dy, wd_ref[cols, :], nt, preferred_element_type=F32)
            gt = gate_ref[:, cols].astype(F32)
            sg = _sigmoid(gt)
            dgate = (da * up_ref[:, cols].astype(F32) * sg * (1.0 + gt * (1.0 - sg))).astype(BF16)
            dup = (da * gt * sg).astype(BF16)
            dgate_ref[:, cols] = dgate
            dup_ref[:, cols] = dup
            q = (lax.dot_general(dgate, wgu_ref[:, cols], nt, preferred_element_type=F32)
                 + lax.dot_general(dup, wgu_ref[:, pl.ds(D_FF + c0, cw)], nt, preferred_element_type=F32))
            du = q if du is None else du + q
        xv = h_ref[...]
        r = lax.rsqrt(jnp.mean(xv * xv, axis=-1, keepdims=True) + RMS_EPS)
        xhat = xv * r
        part = jnp.sum(du * xhat, axis=0, keepdims=True)
        dxhat = du * g_ref[...]
        dx_ref[...] = dhv + r * (dxhat - xhat * jnp.mean(dxhat * xhat, axis=-1, keepdims=True))

        @pl.when(i == 0)
        def _():
            dg_ref[...] = part

        @pl.when(i > 0)
        def _():
            dg_ref[...] += part

    row = pl.BlockSpec((tm, D), lambda i: (i, 0))
    vec = pl.BlockSpec((1, D), lambda i: (0, 0))
    wide = pl.BlockSpec((tm, D_FF), lambda i: (i, 0))
    hidden = jax.ShapeDtypeStruct((S, D_FF), BF16)
    return pl.pallas_call(
        body, name=name, grid=(S // tm,),
        in_specs=[row, row, vec, wide, wide,
                  pl.BlockSpec((None, D, 2 * D_FF), lambda i: (layer, 0, 0)),
                  pl.BlockSpec((None, D_FF, D), lambda i: (layer, 0, 0))],
        out_specs=[row, vec, wide, wide],
        out_shape=[jax.ShapeDtypeStruct((S, D), F32), jax.ShapeDtypeStruct((1, D), F32), hidden, hidden],
        compiler_params=_cparams(("arbitrary",), vmem=62 * 1024 * 1024),
    )(dh, h, g, gate, up, w_gu, w_down)


CONV_HALO = 32
CONV_SUB_ROWS = 128


def _shifted_taps(win, shifted, ts):
    n = ts + CONV_HALO - 8
    for r in range(1, 8):
        shifted[r - 1] = win[pl.ds(r, n), :]

    def tap(start, rows, lanes):
        q, r = divmod(start, 8)
        if r == 0:
            return win[pl.ds(start, rows), lanes]
        return shifted[r - 1, pl.ds(8 * q, rows), lanes]

    return tap


def _conv_a_fwd(z, w, bias, ln_g, ln_b, *, name):
    S = z.shape[0]
    C = CONV_A_CH
    ts = _div_tile(S, 256, 32)

    def body(val_ref, gate_ref, w_ref, b_ref, g_ref, lb_ref, c_ref, act_ref, win, shifted):
        i = pl.program_id(0)

        @pl.when(i == 0)
        def _():
            win[pl.ds(0, CONV_HALO), :] = jnp.zeros((CONV_HALO, C), F32)

        @pl.when(i > 0)
        def _():
            win[pl.ds(0, CONV_HALO), :] = win[pl.ds(ts, CONV_HALO), :]

        a = val_ref[...].astype(F32) * _sigmoid(gate_ref[...].astype(F32))
        win[pl.ds(CONV_HALO, ts), :] = a
        tap = _shifted_taps(win, shifted, ts)
        rs = min(CONV_SUB_ROWS, ts)
        for cb in range(C // 128):
            lanes = pl.ds(128 * cb, 128)
            for rt in range(ts // rs):
                sub = jnp.broadcast_to(b_ref[:, lanes], (rs, 128))
                for k in range(CONV_A_WIDTH):
                    sub = sub + w_ref[pl.ds(k, 1), lanes] * tap(
                        CONV_HALO - (CONV_A_WIDTH - 1) + k + rs * rt, rs, lanes)
                c_ref[pl.ds(rs * rt, rs), lanes] = sub
        acc = c_ref[...]
        mu = jnp.mean(acc, axis=-1, keepdims=True)
        xc = acc - mu
        var = jnp.mean(xc * xc, axis=-1, keepdims=True)
        ln = xc * lax.rsqrt(var + LN_EPS) * g_ref[...] + lb_ref[...]
        act_ref[...] = (ln * _sigmoid(ln)).astype(BF16)

    row = lambda col: pl.BlockSpec((ts, C), lambda i, col=col: (i, col))
    vec = pl.BlockSpec((1, C), lambda i: (0, 0))
    return pl.pallas_call(
        body, name=name, grid=(S // ts,),
        in_specs=[row(0), row(1), pl.BlockSpec((32, C), lambda i: (0, 0)), vec, vec, vec],
        out_specs=[row(0), row(0)],
        out_shape=[jax.ShapeDtypeStruct((S, C), F32), jax.ShapeDtypeStruct((S, 2 * C), BF16)],
        scratch_shapes=[pltpu.VMEM((ts + CONV_HALO, C), F32), pltpu.VMEM((7, ts + CONV_HALO - 8, C), F32)],
        compiler_params=_cparams(("arbitrary",)),
    )(z, z, w, bias, ln_g, ln_b)


def _conv_a_bwd(z, c, dcat, w, ln_g, ln_b, *, name):
    S = z.shape[0]
    C = CONV_A_CH
    ts = _div_tile(S, 256, 32)
    n = S // ts

    def body(val_ref, gate_ref, c_ref, da_ref, w_ref, g_ref, lb_ref, dz_ref, small_ref, win, a_s, da_s, dw8,
             shifted):
        i = pl.program_id(0)

        @pl.when(i == 0)
        def _():
            win[pl.ds(ts, CONV_HALO), :] = jnp.zeros((CONV_HALO, C), F32)
            small_ref[...] = jnp.zeros(small_ref.shape, F32)
            dw8[...] = jnp.zeros(dw8.shape, F32)

        @pl.when(i > 0)
        def _():
            win[pl.ds(ts, CONV_HALO), :] = win[pl.ds(0, CONV_HALO), :]

        cv = c_ref[...]
        gv = g_ref[...]
        mu = jnp.mean(cv, axis=-1, keepdims=True)
        xc = cv - mu
        var = jnp.mean(xc * xc, axis=-1, keepdims=True)
        rstd = lax.rsqrt(var + LN_EPS)
        xhat = xc * rstd
        ln = xhat * gv + lb_ref[...]
        sg = _sigmoid(ln)
        dln = da_ref[...].astype(F32) * (sg * (1.0 + ln * (1.0 - sg)))
        small_ref[pl.ds(33, 1), :] += jnp.sum(dln * xhat, axis=0, keepdims=True)
        small_ref[pl.ds(34, 1), :] += jnp.sum(dln, axis=0, keepdims=True)
        dxhat = dln * gv
        dc = rstd * (dxhat - jnp.mean(dxhat, axis=-1, keepdims=True)
                     - xhat * jnp.mean(dxhat * xhat, axis=-1, keepdims=True))
        small_ref[pl.ds(32, 1), :] += jnp.sum(dc, axis=0, keepdims=True)
        win[pl.ds(0, ts), :] = dc

        val = val_ref[...].astype(F32)
        sgg = _sigmoid(gate_ref[...].astype(F32))
        a_s[...] = val * sgg
        tap = _shifted_taps(win, shifted, ts)
        rs = min(CONV_SUB_ROWS, ts)
        for cb in range(C // 128):
            lanes = pl.ds(128 * cb, 128)
            for rt in range(ts // rs):
                a_sub = a_s[pl.ds(rs * rt, rs), lanes]
                da = jnp.zeros((rs, 128), F32)
                for k in range(CONV_A_WIDTH):
                    sh = tap(CONV_A_WIDTH - 1 - k + rs * rt, rs, lanes)
                    da = da + w_ref[pl.ds(k, 1), lanes] * sh
                    prod = a_sub * sh
                    part = prod[0:8]
                    for r in range(1, rs // 8):
                        part = part + prod[8 * r:8 * r + 8]
                    dw8[pl.ds(8 * k, 8), lanes] += part
                da_s[pl.ds(rs * rt, rs), lanes] = da
        da = da_s[...]
        dz_ref[:, pl.ds(0, C)] = (da * sgg).astype(BF16)
        dz_ref[:, pl.ds(C, C)] = (da * val * sgg * (1.0 - sgg)).astype(BF16)

        @pl.when(i == n - 1)
        def _():
            for k in range(CONV_A_WIDTH):
                small_ref[pl.ds(k, 1), :] = jnp.sum(dw8[pl.ds(8 * k, 8), :], axis=0, keepdims=True)

    row = lambda col: pl.BlockSpec((ts, C), lambda i, col=col: (n - 1 - i, col))
    vec = pl.BlockSpec((1, C), lambda i: (0, 0))
    return pl.pallas_call(
        body, name=name, grid=(n,),
        in_specs=[row(0), row(1), row(0), row(0), pl.BlockSpec((32, C), lambda i: (0, 0)), vec, vec],
        out_specs=[pl.BlockSpec((ts, 2 * C), lambda i: (n - 1 - i, 0)), pl.BlockSpec((40, C), lambda i: (0, 0))],
        out_shape=[jax.ShapeDtypeStruct((S, 2 * C), BF16), jax.ShapeDtypeStruct((40, C), F32)],
        scratch_shapes=[pltpu.VMEM((ts + CONV_HALO, C), F32), pltpu.VMEM((ts, C), F32), pltpu.VMEM((ts, C), F32),
                        pltpu.VMEM((8 * 32, C), F32), pltpu.VMEM((7, ts + CONV_HALO - 8, C), F32)],
        compiler_params=_cparams(("arbitrary",)),
    )(z, z, c, dcat, w, ln_g, ln_b)


SC_HALO = 8


def _sconv_fwd(z, w, *, name):
    S = z.shape[0]
    C = SC_CH
    ts = _div_tile(S, 256, 16)

    def body(gb_ref, gc_ref, v_ref, w_ref, y_ref, cc_ref, win):
        i = pl.program_id(0)

        @pl.when(i == 0)
        def _():
            win[pl.ds(0, SC_HALO), :] = jnp.zeros((SC_HALO, C), F32)

        @pl.when(i > 0)
        def _():
            win[pl.ds(0, SC_HALO), :] = win[pl.ds(ts, SC_HALO), :]

        win[pl.ds(SC_HALO, ts), :] = gc_ref[...].astype(F32) * v_ref[...].astype(F32)
        acc = jnp.zeros((ts, C), F32)
        for k in range(SC_WIDTH):
            acc = acc + w_ref[pl.ds(k, 1), :] * win[pl.ds(SC_HALO - (SC_WIDTH - 1) + k, ts), :]
        cc_ref[...] = acc.astype(BF16)
        y_ref[...] = (gb_ref[...].astype(F32) * acc).astype(BF16)

    row = lambda col: pl.BlockSpec((ts, C), lambda i, col=col: (i, col))
    return pl.pallas_call(
        body, name=name, grid=(S // ts,),
        in_specs=[row(0), row(1), row(2), pl.BlockSpec((8, C), lambda i: (0, 0))],
        out_specs=[row(0), row(0)],
        out_shape=[jax.ShapeDtypeStruct((S, C), BF16), jax.ShapeDtypeStruct((S, C), BF16)],
        scratch_shapes=[pltpu.VMEM((ts + SC_HALO, C), F32)],
        compiler_params=_cparams(("arbitrary",)),
    )(z, z, z, w)


def _sconv_bwd(z, cc, dy, w, *, name):
    S = z.shape[0]
    C = SC_CH
    ts = _div_tile(S, 256, 16)
    n = S // ts

    def body(gb_ref, gc_ref, v_ref, cc_ref, dy_ref, w_ref, dz_ref, dw_ref, win):
        i = pl.program_id(0)

        @pl.when(i == 0)
        def _():
            win[pl.ds(ts, SC_HALO), :] = jnp.zeros((SC_HALO, C), F32)
            dw_ref[...] = jnp.zeros(dw_ref.shape, F32)

        @pl.when(i > 0)
        def _():
            win[pl.ds(ts, SC_HALO), :] = win[pl.ds(0, SC_HALO), :]

        dyv = dy_ref[...].astype(F32)
        gb = gb_ref[...].astype(F32)
        gc = gc_ref[...].astype(F32)
        val = v_ref[...].astype(F32)
        dz_ref[:, pl.ds(0, C)] = (dyv * cc_ref[...].astype(F32)).astype(BF16)
        win[pl.ds(0, ts), :] = dyv * gb
        cv = gc * val
        dcv = jnp.zeros((ts, C), F32)
        for k in range(SC_WIDTH):
            sh = win[pl.ds(SC_WIDTH - 1 - k, ts), :]
            dcv = dcv + w_ref[pl.ds(k, 1), :] * sh
            dw_ref[pl.ds(k, 1), :] += jnp.sum(cv * sh, axis=0, keepdims=True)
        dz_ref[:, pl.ds(C, C)] = (dcv * val).astype(BF16)
        dz_ref[:, pl.ds(2 * C, C)] = (dcv * gc).astype(BF16)

    row = lambda col: pl.BlockSpec((ts, C), lambda i, col=col: (n - 1 - i, col))
    return pl.pallas_call(
        body, name=name, grid=(n,),
        in_specs=[row(0), row(1), row(2), row(0), row(0), pl.BlockSpec((8, C), lambda i: (0, 0))],
        out_specs=[pl.BlockSpec((ts, 3 * C), lambda i: (n - 1 - i, 0)), pl.BlockSpec((8, C), lambda i: (0, 0))],
        out_shape=[jax.ShapeDtypeStruct((S, 3 * C), BF16), jax.ShapeDtypeStruct((8, C), F32)],
        scratch_shapes=[pltpu.VMEM((ts + SC_HALO, C), F32)],
        compiler_params=_cparams(("arbitrary",)),
    )(z, z, z, cc, dy, w)


SWA_Q_COL = 2
SWA_SLOPES = [2.0 ** (-8.0 * (h + 1) / SWA_HEADS) for h in range(SWA_HEADS)]
SWA_SCALE = HEAD_DIM ** -0.5


SWA_GROUP_ROWS = SWA_GROUP * WINDOW


def _swa_masks():
    shape = (SWA_GROUP_ROWS, 2 * WINDOW)
    ii = lax.broadcasted_iota(jnp.int32, shape, 0)
    jj = lax.broadcasted_iota(jnp.int32, shape, 1)
    dist = (ii & (WINDOW - 1)) + WINDOW - jj
    valid = (dist >= 0) & (dist < WINDOW)
    grp = lax.broadcasted_iota(jnp.int32, (SWA_GROUP_ROWS, 1), 0) // WINDOW
    return dist.astype(F32), valid, jj, grp


def _by_group(grp, vals):
    out = jnp.full(grp.shape, vals[SWA_GROUP - 1], F32)
    for g in range(SWA_GROUP - 2, -1, -1):
        out = jnp.where(grp == g, vals[g], out)
    return out


def _stack_heads(ref, rows, kv):
    return jnp.concatenate([ref[rows, pl.ds(HEAD_DIM * (kv * SWA_GROUP + g), HEAD_DIM)] for g in range(SWA_GROUP)],
                           axis=0)


def _swa_probs(qg, kk, sink, slope, distf, valid):
    s = lax.dot_general(qg, kk, (((1,), (1,)), ((), ())), preferred_element_type=F32) * SWA_SCALE
    s = s - slope * distf
    s = jnp.where(valid, s, NEG_BIG)
    m = jnp.maximum(jnp.max(s, axis=-1, keepdims=True), sink)
    p = jnp.exp(s - m)
    l = jnp.sum(p, axis=-1, keepdims=True) + jnp.exp(sink - m)
    return p, m, l


def _swa_fwd(z, kpad, vpad, sinks, cat, *, name):
    S = z.shape[0]
    tq = _div_tile(S, 256, 128)
    nblk = tq // WINDOW
    W = WINDOW

    def body(sink_ref, q_ref, k_ref, v_ref, cat_ref, o_ref):
        i = pl.program_id(0)
        distf, valid0, jj, grp = _swa_masks()
        for kv in range(SWA_KV_HEADS):
            heads = range(kv * SWA_GROUP, (kv + 1) * SWA_GROUP)
            sink = _by_group(grp, [sink_ref[h] for h in heads])
            slope = _by_group(grp, [SWA_SLOPES[h] for h in heads])
            for b in range(nblk):
                nb = i * nblk + b
                start = pl.multiple_of(nb * W, W)
                rows = pl.ds(W * b, W)
                valid = valid0 & ((jj >= W) | (nb > 0))
                kk = k_ref[pl.ds(start, 2 * W), pl.ds(HEAD_DIM * kv, HEAD_DIM)]
                vv = v_ref[pl.ds(start, 2 * W), pl.ds(HEAD_DIM * kv, HEAD_DIM)]
                p, m, l = _swa_probs(_stack_heads(q_ref, rows, kv), kk, sink, slope, distf, valid)
                o = (jnp.dot(p.astype(BF16), vv, preferred_element_type=F32) / l).astype(BF16)
                for g, h in enumerate(heads):
                    o_ref[rows, pl.ds(HEAD_DIM * h, HEAD_DIM)] = o[W * g:W * (g + 1)]

    full = pl.BlockSpec((S + W, 2 * HEAD_DIM), lambda i: (0, 0))
    return pl.pallas_call(
        body, name=name, grid=(S // tq,),
        in_specs=[pl.BlockSpec(memory_space=pltpu.SMEM), pl.BlockSpec((tq, 512), lambda i: (i, SWA_Q_COL)), full, full,
                  pl.BlockSpec(memory_space=pl.ANY)],
        out_specs=pl.BlockSpec((tq, 512), lambda i: (i, 1)),
        out_shape=jax.ShapeDtypeStruct((S, 1024), BF16), input_output_aliases={4: 0},
        compiler_params=_cparams(("parallel",)),
    )(sinks, z, kpad, vpad, cat)


def _swa_bwd(z, kpad, vpad, sinks, dcat, *, name):
    S = z.shape[0]
    tq = _div_tile(S, 256, 128)
    nblk = tq // WINDOW
    W = WINDOW

    def body(sink_ref, q_ref, k_ref, v_ref, do_ref, dq_ref, dk_ref, dv_ref, ds_ref):
        i = pl.program_id(0)

        @pl.when(i == 0)
        def _():
            dk_ref[...] = jnp.zeros(dk_ref.shape, F32)
            dv_ref[...] = jnp.zeros(dv_ref.shape, F32)
            ds_ref[...] = jnp.zeros(ds_ref.shape, F32)

        distf, valid0, jj, grp = _swa_masks()
        tn = (((0,), (0,)), ((), ()))
        for kv in range(SWA_KV_HEADS):
            heads = range(kv * SWA_GROUP, (kv + 1) * SWA_GROUP)
            sink = _by_group(grp, [sink_ref[h] for h in heads])
            slope = _by_group(grp, [SWA_SLOPES[h] for h in heads])
            for b in range(nblk):
                nb = i * nblk + b
                start = pl.multiple_of(nb * W, W)
                rows = pl.ds(W * b, W)
                valid = valid0 & ((jj >= W) | (nb > 0))
                kk = k_ref[pl.ds(start, 2 * W), pl.ds(HEAD_DIM * kv, HEAD_DIM)]
                vv = v_ref[pl.ds(start, 2 * W), pl.ds(HEAD_DIM * kv, HEAD_DIM)]
                qg = _stack_heads(q_ref, rows, kv)
                dog = _stack_heads(do_ref, rows, kv)
                p, m, l = _swa_probs(qg, kk, sink, slope, distf, valid)
                inv_l = 1.0 / l
                pn = p * inv_l
                dp = lax.dot_general(dog, vv, (((1,), (1,)), ((), ())), preferred_element_type=F32)
                delta = jnp.sum(pn * dp, axis=-1, keepdims=True)
                dsc = (pn * (dp - delta)).astype(BF16)
                dsink = jnp.exp(sink - m) * inv_l * delta
                dq = (jnp.dot(dsc, kk, preferred_element_type=F32) * SWA_SCALE).astype(BF16)
                for g, h in enumerate(heads):
                    ds_ref[pl.ds(h, 1), :] += jnp.broadcast_to(
                        -jnp.sum(dsink[W * g:W * (g + 1)], axis=0, keepdims=True), (1, 128))
                    dq_ref[rows, pl.ds(HEAD_DIM * h, HEAD_DIM)] = dq[W * g:W * (g + 1)]
                dk_ref[pl.ds(start, 2 * W), pl.ds(HEAD_DIM * kv, HEAD_DIM)] += lax.dot_general(
                    dsc, qg, tn, preferred_element_type=F32) * SWA_SCALE
                dv_ref[pl.ds(start, 2 * W), pl.ds(HEAD_DIM * kv, HEAD_DIM)] += lax.dot_general(
                    pn.astype(BF16), dog, tn, preferred_element_type=F32)

    full = pl.BlockSpec((S + W, 2 * HEAD_DIM), lambda i: (0, 0))
    return pl.pallas_call(
        body, name=name, grid=(S // tq,),
        in_specs=[pl.BlockSpec(memory_space=pltpu.SMEM), pl.BlockSpec((tq, 512), lambda i: (i, SWA_Q_COL)), full, full,
                  pl.BlockSpec((tq, 512), lambda i: (i, 1))],
        out_specs=[pl.BlockSpec((tq, 512), lambda i: (i, 0)), full, full, pl.BlockSpec((8, 128), lambda i: (0, 0))],
        out_shape=[jax.ShapeDtypeStruct((S, 512), BF16), jax.ShapeDtypeStruct((S + W, 2 * HEAD_DIM), F32),
                   jax.ShapeDtypeStruct((S + W, 2 * HEAD_DIM), F32), jax.ShapeDtypeStruct((8, 128), F32)],
        compiler_params=_cparams(("arbitrary",)),
    )(sinks, z, kpad, vpad, dcat)


XA_SCALE = XA_HEAD_DIM ** -0.5


def _xa_probs(qh, kh):
    s = lax.dot_general(qh, kh, (((1,), (1,)), ((), ())), preferred_element_type=F32) * XA_SCALE
    m = jnp.max(s, axis=-1, keepdims=True)
    p = jnp.exp(s - m)
    return p, jnp.sum(p, axis=-1, keepdims=True)


def _xa_fwd(h, g, wq, wo, layer, kv, *, name):
    S, D = h.shape
    M = kv.shape[0]
    tm = _div_tile(S, 512, NORM_SLAB)
    slab = min(NORM_SLAB, tm)
    HD = XA_HEAD_DIM

    def body(h_ref, g_ref, wq_ref, wo_ref, k_ref, v_ref, h2_ref, u_ref, q_ref, o_ref):
        for r0 in range(0, tm, slab):
            rows = pl.ds(r0, slab)
            xv = h_ref[rows, :]
            r = lax.rsqrt(jnp.mean(xv * xv, axis=-1, keepdims=True) + RMS_EPS)
            u = (xv * r * g_ref[...]).astype(BF16)
            u_ref[rows, :] = u
            q_ref[rows, :] = jnp.dot(u, wq_ref[...], preferred_element_type=F32).astype(BF16)
            for hd in range(XA_HEADS):
                cols = pl.ds(HD * hd, HD)
                p, l = _xa_probs(q_ref[rows, cols], k_ref[:, cols])
                o = jnp.dot(p.astype(BF16), v_ref[:, cols], preferred_element_type=F32) / l
                o_ref[rows, cols] = o.astype(BF16)
            h2_ref[rows, :] = xv + jnp.dot(o_ref[rows, :], wo_ref[...], preferred_element_type=F32)

    row = pl.BlockSpec((tm, D), lambda i: (i, 0))
    weight = pl.BlockSpec((None, D, D), lambda i: (layer, 0, 0))
    act = jax.ShapeDtypeStruct((S, D), BF16)
    return pl.pallas_call(
        body, name=name, grid=(S // tm,),
        in_specs=[row, pl.BlockSpec((1, D), lambda i: (0, 0)), weight, weight,
                  pl.BlockSpec((M, D), lambda i: (0, 0)), pl.BlockSpec((M, D), lambda i: (0, 1))],
        out_specs=[row, row, row, row],
        out_shape=[jax.ShapeDtypeStruct((S, D), F32), act, act, act],
        compiler_params=_cparams(("parallel",)),
    )(h, g, wq, wo, kv, kv)


def _xa_bwd(q, kv, do, *, name):
    S, D = q.shape
    M = kv.shape[0]
    ts = _div_tile(S, 512, 16)
    HD = XA_HEAD_DIM

    def body(q_ref, k_ref, v_ref, do_ref, dq_ref, dkv_ref):
        i = pl.program_id(0)

        @pl.when(i == 0)
        def _():
            dkv_ref[...] = jnp.zeros(dkv_ref.shape, F32)

        for h in range(XA_HEADS):
            qh = q_ref[:, pl.ds(HD * h, HD)]
            kh = k_ref[:, pl.ds(HD * h, HD)]
            vh = v_ref[:, pl.ds(HD * h, HD)]
            doh = do_ref[:, pl.ds(HD * h, HD)]
            p, l = _xa_probs(qh, kh)
            pn = p * (1.0 / l)
            dp = lax.dot_general(doh, vh, (((1,), (1,)), ((), ())), preferred_element_type=F32)
            delta = jnp.sum(pn * dp, axis=-1, keepdims=True)
            dsc = (pn * (dp - delta)).astype(BF16)
            dq_ref[:, pl.ds(HD * h, HD)] = (jnp.dot(dsc, kh, preferred_element_type=F32) * XA_SCALE).astype(BF16)
            dkv_ref[:, pl.ds(HD * h, HD)] += lax.dot_general(
                dsc, qh, (((0,), (0,)), ((), ())), preferred_element_type=F32) * XA_SCALE
            dkv_ref[:, pl.ds(D + HD * h, HD)] += lax.dot_general(
                pn.astype(BF16), doh, (((0,), (0,)), ((), ())), preferred_element_type=F32)

    row = pl.BlockSpec((ts, D), lambda i: (i, 0))
    return pl.pallas_call(
        body, name=name, grid=(S // ts,),
        in_specs=[row, pl.BlockSpec((M, D), lambda i: (0, 0)), pl.BlockSpec((M, D), lambda i: (0, 1)), row],
        out_specs=[row, pl.BlockSpec((M, 2 * D), lambda i: (0, 0))],
        out_shape=[jax.ShapeDtypeStruct((S, D), BF16), jax.ShapeDtypeStruct((M, 2 * D), F32)],
        compiler_params=_cparams(("arbitrary",)),
    )(q, kv, kv, do)


def _adam_math(w, g, m, v):
    m = ADAM_B1 * m + (1.0 - ADAM_B1) * g
    v = ADAM_B2 * v + (1.0 - ADAM_B2) * (g * g)
    m_hat = m / (1.0 - ADAM_B1 ** ADAM_STEP)
    v_hat = v / (1.0 - ADAM_B2 ** ADAM_STEP)
    delta = -ADAM_LR * (m_hat / (jnp.sqrt(v_hat) + ADAM_EPS) + ADAM_WD * w)
    return delta, m, v


def _adamw_layers(w, m, v, gsrc, *, name):
    L, A, B = w.shape
    tr = _div_tile(A, max(8, ADAM_TILE_ELEMS // B // 8 * 8))
    nt = A // tr
    flat = [a for srcs in gsrc for a in srcs]
    owner = [l for l, srcs in enumerate(gsrc) for _ in srcs]
    ng = len(flat)

    def body(*refs):
        w_ref, m_ref, v_ref = refs[:3]
        g_refs = refs[3:3 + ng]
        g_ref, d_ref, nm_ref, nv_ref = refs[3 + ng:]
        layer = pl.program_id(0)
        g = None
        for l in range(L):
            gl = None
            for a_ref, o in zip(g_refs, owner):
                if o == l:
                    gl = a_ref[...] if gl is None else gl + a_ref[...]
            g = gl if g is None else jnp.where(layer == l, gl, g)
        d, nm, nv = _adam_math(w_ref[...], g, m_ref[...], v_ref[...])
        g_ref[...] = g
        d_ref[...] = d
        nm_ref[...] = nm
        nv_ref[...] = nv

    def src_spec(o):
        return pl.BlockSpec((None, tr, B),
                            lambda l, i: (0, jnp.where(l == o, i, jnp.where(l > o, nt - 1, 0)), 0))

    spec = pl.BlockSpec((None, tr, B), lambda l, i: (l, i, 0))
    sds = jax.ShapeDtypeStruct((L, A, B), F32)
    return pl.pallas_call(
        body, name=name, grid=(L, nt), in_specs=[spec] * 3 + [src_spec(o) for o in owner], out_specs=[spec] * 4,
        out_shape=[sds] * 4, compiler_params=_cparams(("arbitrary", "arbitrary")),
    )(w, m, v, *flat)


def _adamw_small(ws, ms, vs, gparts, *, name):
    n = len(ws)
    R = gparts.shape[1]

    def body(*refs):
        w_refs, m_refs, v_refs = refs[:n], refs[n:2 * n], refs[2 * n:3 * n]
        gp_ref = refs[3 * n]
        outs = refs[3 * n + 1:7 * n + 1]
        packed = refs[7 * n + 1]
        g = gp_ref[0]
        for k in range(1, N_DEV):
            g = g + gp_ref[k]
        packed[...] = g
        row = 0
        for p in range(n):
            r, c = ws[p].shape
            per, lanes = max(c // 128, 1), min(c, 128)
            g_ref = outs[p]
            for i in range(r):
                for k in range(per):
                    g_ref[pl.ds(i, 1), pl.ds(128 * k, lanes)] = packed[pl.ds(row, 1), pl.ds(0, lanes)]
                    row += 1
            d, nm, nv = _adam_math(w_refs[p][...], g_ref[...], m_refs[p][...], v_refs[p][...])
            outs[n + p][...] = d
            outs[2 * n + p][...] = nm
            outs[3 * n + p][...] = nv

    shapes = [jax.ShapeDtypeStruct(a.shape, F32) for a in ws]
    res = pl.pallas_call(body, name=name, out_shape=shapes * 4, scratch_shapes=[pltpu.VMEM((R, 128), F32)],
                         compiler_params=_cparams())(*ws, *ms, *vs, gparts)
    return res[:n], res[n:2 * n], res[2 * n:3 * n], res[3 * n:]


ANY = pl.BlockSpec(memory_space=pl.ANY)


def _mesh_pos():
    return lax.axis_index("x"), lax.axis_index("y"), lax.axis_index("c")


def _other_chips(x, y):
    return [(1 - x, y), (x, 1 - y), (1 - x, 1 - y)]


LAYOUT = {'ffn1_w_gu': 'col', 'ffn1_w_down': 'stk', 'even_w_in': 'stk', 'even_w_out': 'stk', 'odd_w_in': 'col',
          'odd_w_out': 'stk', 'xa_wq': 'stk', 'xa_wkv': 'col', 'xa_wo': 'stk', 'ffn2_w_gu': 'col',
          'ffn2_w_down': 'stk', 'tiny': 'stk'}
COMM_NAMES = list(LAYOUT)
TINY_ROWS = 48


def _gathered_piece(ref, kind, L, A, h):
    if L == 2:
        return ref.at[h]
    rows = pl.ds(pl.multiple_of(h * (A // 2), 8), A // 2)
    return ref.at[0, rows] if kind == 'col' else ref.at[0, :, rows]


def _chip_part(piece, kind, B, s):
    if kind == 'col':
        return piece.at[:, pl.ds(pl.multiple_of(s * B, 128), B)]
    return piece.at[s]


def _place(shard, layer, kind, chip_idx, out_dtype, *, name, after=None):
    L, A, B = shard.shape
    ta = _div_tile(A, 256, 16)
    extra = [] if after is None else [after]

    def body(s_ref, x_ref, *rest):
        rest[-1][...] = x_ref[...].astype(out_dtype)

    if kind == 'col':
        shape = (1, A, N_CHIPS * B)
        out_spec = pl.BlockSpec((None, ta, B), lambda i, s: (0, i, s[0]))
    else:
        shape = (1, N_CHIPS, A, B)
        out_spec = pl.BlockSpec((None, None, ta, B), lambda i, s: (0, s[0], i, 0))
    grid_spec = pltpu.PrefetchScalarGridSpec(
        num_scalar_prefetch=1, grid=(A // ta,),
        in_specs=[pl.BlockSpec((None, ta, B), lambda i, s: (layer, i, 0))]
        + [pl.BlockSpec(memory_space=pl.ANY)] * len(extra), out_specs=out_spec)
    return pl.pallas_call(
        body, name=name, grid_spec=grid_spec, out_shape=jax.ShapeDtypeStruct(shape, out_dtype),
        compiler_params=_cparams(("parallel",)),
    )(chip_idx, shard, *extra)


HBM = pl.BlockSpec(memory_space=pltpu.HBM)
SEM = pl.BlockSpec(memory_space=pltpu.SEMAPHORE)
DATAFLOW = pltpu.SideEffectType.DATAFLOW_SIDE_EFFECTING


def _own_part_copies(refs, meta, send_sems, recv_sems):
    x, y, c = _mesh_pos()
    cps = []
    for k, (kind, L, A, B) in enumerate(meta):
        for j, (cx, cy) in enumerate(_other_chips(x, y)):
            part = _chip_part(refs[k].at[0], kind, B, 2 * x + y)
            cps.append(pltpu.make_async_remote_copy(
                src_ref=part, dst_ref=part, send_sem=send_sems.at[3 * k + j], recv_sem=recv_sems.at[3 * k + j],
                device_id=(cx, cy, c), device_id_type=MESH))
    return cps


def _half_part_copies(refs, meta, send_sems, recv_sems):
    x, y, c = _mesh_pos()
    cps = []
    for k, (kind, L, A, B) in enumerate(meta):
        for j, (cx, cy) in enumerate(_other_chips(x, y)):
            part = _chip_part(_gathered_piece(refs[k], kind, 1, A, c), kind, B, 2 * x + y)
            cps.append(pltpu.make_async_remote_copy(
                src_ref=part, dst_ref=part, send_sem=send_sems.at[3 * k + j], recv_sem=recv_sems.at[3 * k + j],
                device_id=(cx, cy, c), device_id_type=MESH))
    return cps


def _forward_copies(refs, meta, send_sems, recv_sems):
    x, y, c = _mesh_pos()
    cps = []
    for k, (kind, L, A, B) in enumerate(meta):
        for j, (cx, cy) in enumerate(_other_chips(x, y)):
            part = _chip_part(_gathered_piece(refs[k], kind, 1, A, c), kind, B, 2 * cx + cy)
            cps.append(pltpu.make_async_remote_copy(
                src_ref=part, dst_ref=part, send_sem=send_sems.at[3 * k + j], recv_sem=recv_sems.at[3 * k + j],
                device_id=(x, y, 1 - c), device_id_type=MESH))
    return cps


def _gather_start(fulls, meta, after, tag, copies=_own_part_copies):
    n = len(fulls)

    def body(*refs):
        send_sems, recv_sems = refs[n + 1], refs[n + 2]
        outs = refs[n + 3:2 * n + 3]
        token = refs[2 * n + 3]
        for cp in copies(outs, meta, send_sems, recv_sems):
            cp.start()
        token[...] = jnp.zeros_like(token)

    res = pl.pallas_call(
        body, name=f"ag_start_{tag}", in_specs=[HBM] * n + [pl.BlockSpec(memory_space=pl.ANY)],
        out_specs=(SEM, SEM) + (HBM,) * n + (pl.BlockSpec(memory_space=pltpu.VMEM),),
        out_shape=(pltpu.SemaphoreType.DMA((3 * n,)), pltpu.SemaphoreType.DMA((3 * n,)))
        + tuple(pltpu.HBM(f.shape, f.dtype) for f in fulls) + (jax.ShapeDtypeStruct((8, 128), F32),),
        input_output_aliases={k: 2 + k for k in range(n)},
        compiler_params=pltpu.CompilerParams(has_side_effects=DATAFLOW),
    )(*[pltpu.with_memory_space_constraint(f, pltpu.HBM) for f in fulls], after)
    return res[0], res[1], list(res[2:2 + n]), res[2 + n]


def _gather_wait(send_sems, recv_sems, fulls, meta, after, tag, copies=_own_part_copies):
    n = len(fulls)

    def body(*refs):
        f_refs = refs[:n]
        send_sems, recv_sems = refs[n], refs[n + 1]
        for cp in copies(f_refs, meta, send_sems, recv_sems):
            cp.wait_send()
            cp.wait_recv()

    return pl.pallas_call(
        body, name=f"ag_wait_{tag}", in_specs=[HBM] * n + [SEM, SEM, pl.BlockSpec(memory_space=pl.ANY)],
        out_specs=[HBM] * n, out_shape=[pltpu.HBM(f.shape, f.dtype) for f in fulls],
        input_output_aliases={k: k for k in range(n)},
        compiler_params=pltpu.CompilerParams(has_side_effects=DATAFLOW),
    )(*fulls, send_sems, recv_sems, after)


def _scatter_copies(g_refs, land_refs, meta, send_sems, recv_sems):
    x, y, c = _mesh_pos()
    cps = []
    for k, (kind, L, A, B) in enumerate(meta):
        for j, (cx, cy) in enumerate(_other_chips(x, y)):
            cps.append(pltpu.make_async_remote_copy(
                src_ref=_chip_part(g_refs[k].at[0], kind, B, 2 * cx + cy), dst_ref=land_refs[k].at[j],
                send_sem=send_sems.at[3 * k + j], recv_sem=recv_sems.at[3 * k + j], device_id=(cx, cy, c),
                device_id_type=MESH))
    return cps


def _scatter_start(gs, meta, after, tag):
    n = len(gs)

    def body(*refs):
        send_sems, recv_sems = refs[2 * n + 1], refs[2 * n + 2]
        g_out = refs[2 * n + 3:3 * n + 3]
        lands = refs[3 * n + 3:4 * n + 3]
        token = refs[4 * n + 3]
        for cp in _scatter_copies(g_out, lands, meta, send_sems, recv_sems):
            cp.start()
        token[...] = jnp.zeros_like(token)

    land_shapes = [(3, A, B) for kind, L, A, B in meta]
    lands = [pltpu.with_memory_space_constraint(lax.empty(s, g.dtype), pltpu.HBM) for s, g in zip(land_shapes, gs)]
    res = pl.pallas_call(
        body, name=f"rs_start_{tag}", in_specs=[HBM] * (2 * n) + [pl.BlockSpec(memory_space=pl.ANY)],
        out_specs=(SEM, SEM) + (HBM,) * (2 * n) + (pl.BlockSpec(memory_space=pltpu.VMEM),),
        out_shape=(pltpu.SemaphoreType.DMA((3 * n,)), pltpu.SemaphoreType.DMA((3 * n,)))
        + tuple(pltpu.HBM(g.shape, g.dtype) for g in gs)
        + tuple(pltpu.HBM(s, g.dtype) for s, g in zip(land_shapes, gs)) + (jax.ShapeDtypeStruct((8, 128), F32),),
        input_output_aliases={k: 2 + k for k in range(2 * n)},
        compiler_params=pltpu.CompilerParams(has_side_effects=DATAFLOW),
    )(*[pltpu.with_memory_space_constraint(g, pltpu.HBM) for g in gs], *lands, after)
    return res[0], res[1], list(res[2:2 + n]), list(res[2 + n:2 + 2 * n]), res[2 + 2 * n]


def _scatter_wait(send_sems, recv_sems, gs, lands, meta, after, tag):
    n = len(gs)

    def body(*refs):
        g_refs, land_refs = refs[:n], refs[n:2 * n]
        send_sems, recv_sems = refs[2 * n], refs[2 * n + 1]
        for cp in _scatter_copies(g_refs, land_refs, meta, send_sems, recv_sems):
            cp.wait_send()
            cp.wait_recv()

    both = list(gs) + list(lands)
    res = pl.pallas_call(
        body, name=f"rs_wait_{tag}", in_specs=[HBM] * (2 * n) + [SEM, SEM, pl.BlockSpec(memory_space=pl.ANY)],
        out_specs=[HBM] * (2 * n), out_shape=[pltpu.HBM(a.shape, a.dtype) for a in both],
        input_output_aliases={k: k for k in range(2 * n)},
        compiler_params=pltpu.CompilerParams(has_side_effects=DATAFLOW),
    )(*both, send_sems, recv_sems, after)
    return list(res[:n]), list(res[n:])


def _chip_sum_full(g, got, m, chip_idx, *, name):
    kind, L, A, B = m
    ta = _div_tile(A, 256, 16)

    def body(r_ref, a_ref, b_ref, o_ref):
        acc = a_ref[...].astype(F32)
        for j in range(3):
            acc = acc + b_ref[j].astype(F32)
        o_ref[...] = acc

    if kind == 'col':
        g_spec = pl.BlockSpec((None, ta, B), lambda i, r: (0, i, r[0]))
    else:
        g_spec = pl.BlockSpec((None, None, ta, B), lambda i, r: (0, r[0], i, 0))
    grid_spec = pltpu.PrefetchScalarGridSpec(
        num_scalar_prefetch=1, grid=(A // ta,),
        in_specs=[g_spec, pl.BlockSpec((3, ta, B), lambda i, r: (0, i, 0))],
        out_specs=pl.BlockSpec((None, ta, B), lambda i, r: (0, i, 0)))
    return pl.pallas_call(
        body, name=name, grid_spec=grid_spec, out_shape=jax.ShapeDtypeStruct((1, A, B), F32),
        compiler_params=_cparams(("parallel",)),
    )(chip_idx, g, got)


def _swap_copies(src_refs, land_refs, send_sems, recv_sems):
    x, y, c = _mesh_pos()
    return [pltpu.make_async_remote_copy(src_ref=s, dst_ref=d, send_sem=send_sems.at[k], recv_sem=recv_sems.at[k],
                                         device_id=(x, y, 1 - c), device_id_type=MESH)
            for k, (s, d) in enumerate(zip(src_refs, land_refs))]


def _swap_start(sums, after, tag):
    n = len(sums)

    def body(*refs):
        send_sems, recv_sems = refs[2 * n + 1], refs[2 * n + 2]
        s_out = refs[2 * n + 3:3 * n + 3]
        lands = refs[3 * n + 3:4 * n + 3]
        token = refs[4 * n + 3]
        for cp in _swap_copies(s_out, lands, send_sems, recv_sems):
            cp.start()
        token[...] = jnp.zeros_like(token)

    lands = [pltpu.with_memory_space_constraint(lax.empty(s.shape, s.dtype), pltpu.HBM) for s in sums]
    res = pl.pallas_call(
        body, name=f"rs_swap_start_{tag}", in_specs=[HBM] * (2 * n) + [pl.BlockSpec(memory_space=pl.ANY)],
        out_specs=(SEM, SEM) + (HBM,) * (2 * n) + (pl.BlockSpec(memory_space=pltpu.VMEM),),
        out_shape=(pltpu.SemaphoreType.DMA((n,)), pltpu.SemaphoreType.DMA((n,)))
        + tuple(pltpu.HBM(s.shape, s.dtype) for s in sums) * 2 + (jax.ShapeDtypeStruct((8, 128), F32),),
        input_output_aliases={k: 2 + k for k in range(2 * n)},
        compiler_params=pltpu.CompilerParams(has_side_effects=DATAFLOW),
    )(*[pltpu.with_memory_space_constraint(s, pltpu.HBM) for s in sums], *lands, after)
    return res[0], res[1], list(res[2:2 + n]), list(res[2 + n:2 + 2 * n]), res[2 + 2 * n]


def _swap_wait(send_sems, recv_sems, sums, lands, after, tag):
    n = len(sums)

    def body(*refs):
        send_sems, recv_sems = refs[2 * n], refs[2 * n + 1]
        for cp in _swap_copies(refs[:n], refs[n:2 * n], send_sems, recv_sems):
            cp.wait_send()
            cp.wait_recv()

    both = list(sums) + list(lands)
    res = pl.pallas_call(
        body, name=f"rs_swap_wait_{tag}", in_specs=[HBM] * (2 * n) + [SEM, SEM, pl.BlockSpec(memory_space=pl.ANY)],
        out_specs=[HBM] * (2 * n), out_shape=[pltpu.HBM(a.shape, a.dtype) for a in both],
        input_output_aliases={k: k for k in range(2 * n)},
        compiler_params=pltpu.CompilerParams(has_side_effects=DATAFLOW),
    )(*both, send_sems, recv_sems, after)
    return list(res[:n]), list(res[n:])


def _pair_swap(sums, small, *, tag):
    ns = len(sums)
    with_small = small is not None
    n_in = ns + with_small

    def body(*refs):
        sum_refs = refs[:ns]
        got_refs = refs[n_in:n_in + ns]
        send_sems, recv_sems = refs[2 * n_in], refs[2 * n_in + 1]
        x, y, c = _mesh_pos()
        cps = []
        for k in range(ns):
            cp = pltpu.make_async_remote_copy(
                src_ref=sum_refs[k], dst_ref=got_refs[k], send_sem=send_sems.at[k], recv_sem=recv_sems.at[k],
                device_id=(x, y, 1 - c), device_id_type=MESH)
            cp.start()
            cps.append(cp)
        if with_small:
            small_ref, sm_ref, local_sem = refs[ns], refs[n_in + ns], refs[2 * n_in + 2]
            me = 4 * x + 2 * y + c
            own = pltpu.make_async_copy(small_ref, sm_ref.at[me], local_sem)
            own.start()
            for r in range(1, N_DEV):
                fx, fy, fc = (r >> 2) & 1, (r >> 1) & 1, r & 1
                peer = (1 - x if fx else x, 1 - y if fy else y, 1 - c if fc else c)
                cp = pltpu.make_async_remote_copy(
                    src_ref=small_ref, dst_ref=sm_ref.at[me], send_sem=send_sems.at[ns + r],
                    recv_sem=recv_sems.at[ns + r], device_id=peer, device_id_type=MESH)
                cp.start()
                cps.append(cp)
        for cp in cps:
            cp.wait()
        if with_small:
            own.wait()

    out_shape = [jax.ShapeDtypeStruct(s.shape, s.dtype) for s in sums]
    scratch = [pltpu.SemaphoreType.DMA((ns + N_DEV,)), pltpu.SemaphoreType.DMA((ns + N_DEV,))]
    args = list(sums)
    if with_small:
        out_shape.append(jax.ShapeDtypeStruct((N_DEV,) + small.shape, F32))
        scratch.append(pltpu.SemaphoreType.DMA)
        args.append(small)
    res = pl.pallas_call(
        body, name=f"rs_pair_swap_{tag}", in_specs=[ANY] * n_in, out_specs=[ANY] * n_in, out_shape=out_shape,
        scratch_shapes=scratch,
    )(*args)
    return list(res[:ns]), (res[ns] if with_small else None)


def _tiny_pack(conv_a_w, sc_conv_w):
    lead = conv_a_w.shape[:-2]
    sc = sc_conv_w.reshape(lead + (2 * SC_WIDTH, 128))
    z = lambda r: jnp.zeros(lead + (r, 128), F32)
    return jnp.concatenate([conv_a_w, z(32 - CONV_A_WIDTH), sc, z(TINY_ROWS - 32 - 2 * SC_WIDTH)], axis=-2)


def _tiny_unpack(t):
    lead = t.shape[:-2]
    return t[..., :CONV_A_WIDTH, :], t[..., 32:32 + 2 * SC_WIDTH, :].reshape(lead + (SC_WIDTH, 256))


def _pack_small(d):
    parts = []
    for n in SMALL_NAMES:
        flat = d[n].astype(F32).reshape(-1)
        parts.append(jnp.pad(flat, (0, -flat.shape[0] % 128)))
    flat = jnp.concatenate(parts)
    return jnp.pad(flat, (0, -flat.shape[0] % 1024)).reshape(-1, 128)


def _ffn_fwd(h, g, W, n_gu, n_down, i, tag):
    h2, u, gate, up, a = _ffn_fwd_fused(h, g, W[n_gu][i], W[n_down][i], 0, name=f"{tag}_fwd")
    return h2, (h, u, gate, up, a)


def _ffn_bwd(dh, saved, g, W, n_gu, n_down, i, G, tag):
    h, u, gate, up, a = saved
    dh_in, dg, dgate, dup = _ffn_bwd_fused(dh, h, g, gate, up, W[n_gu][i], W[n_down][i], 0, name=f"{tag}_bwd")
    G[(n_down, i)] = _mm(a, dh, name=f"{tag}_b_wdown", ta=True, tm=1408, tn=1024, tk=1024, scale=0.5)
    tn = FFN_CHUNK
    half = _mm(u, dgate, name=f"{tag}_b_wg", ta=True, tm=1024, tn=tn, tk=2048, stack=(1, 0, None, 2 * D_FF))
    G[(n_gu, i)] = _mm(u, dup, name=f"{tag}_b_wu", ta=True, tm=1024, tn=tn, tk=2048, stack=(1, 0, half, 2 * D_FF),
                       n_map=lambda j: j + D_FF // tn)
    return dh_in, dg


def _xa_block_fwd(h, mem, g, gm, W, i, tag):
    mn = _rms_fwd(mem, gm, name=f"{tag}_mem_norm")
    kv = _mm(mn, W['xa_wkv'][i], b_layer=0, name=f"{tag}_kv", tm=256, tn=1024, tk=1024)
    h2, u, q, o = _xa_fwd(h, g, W['xa_wq'][i], W['xa_wo'][i], 0, kv, name=f"{tag}_fwd")
    return h2, (h, u, mn, q, kv, o)


def _xa_block_bwd(dh, saved, mem, g, gm, W, i, G, tag):
    h, u, mn, q, kv, o = saved
    do = _mm(dh, W['xa_wo'][i], b_layer=0, name=f"{tag}_b_do", tb=True, tm=1024, tn=1024, tk=1024)
    G[('xa_wo', i)] = _mm(o, dh, name=f"{tag}_b_wo", ta=True, tm=1024, tn=1024, tk=1024)
    dq, dkv = _xa_bwd(q, kv, do, name=f"{tag}_b_attn")
    G[('xa_wq', i)] = _mm(u, dq, name=f"{tag}_b_wq", ta=True, tm=1024, tn=1024, tk=1024)
    dh_in, dg = _mm_norm_bwd(dq, W['xa_wq'][i], 0, h, g, dh, name=f"{tag}_b_du", tk=1024)
    G[('xa_wkv', i)] = _mm(mn, dkv, name=f"{tag}_b_wkv", ta=True, tm=1024, tn=1024, tk=256)
    dmn = _mm(dkv, W['xa_wkv'][i], b_layer=0, name=f"{tag}_b_dmn", tb=True, out_dtype=F32, tm=256, tn=1024, tk=1024)
    _, dgm = _rms_bwd(mem, gm, dmn, None, name=f"{tag}_b_mem_norm")
    return dh_in, dg, dgm


def _pad_conv_w(w, rows):
    return jnp.pad(w.astype(F32), ((0, rows - w.shape[0]), (0, 0)))


def _even_fwd(h, g, W, conv_w, conv_b, ln_g, ln_b, sinks, tag):
    u, z = _norm_mm(h, g, W['even_w_in'][0], 0, name=f"{tag}_in")
    c, cat = _conv_a_fwd(z, conv_w, conv_b, ln_g, ln_b, name=f"{tag}_conv")
    kpad = jnp.pad(z[:, 1536:1664], ((WINDOW, 0), (0, 0)))
    vpad = jnp.pad(z[:, 1664:1792], ((WINDOW, 0), (0, 0)))
    cat = _swa_fwd(z, kpad, vpad, sinks, cat, name=f"{tag}_swa")
    h2 = _mm(cat, W['even_w_out'][0], b_layer=0, name=f"{tag}_out", out_dtype=F32, tm=1024, tn=1024, tk=1024, res=h)
    return h2, (h, u, z, c, kpad, vpad, cat)


def _even_bwd(dh, saved, g, W, conv_w, ln_g, ln_b, sinks, G, tag):
    h, u, z, c, kpad, vpad, cat = saved
    dcat = _mm(dh, W['even_w_out'][0], b_layer=0, name=f"{tag}_b_dcat", tb=True, tm=1024, tn=1024, tk=1024)
    G[('even_w_out', 0)] = _mm(cat, dh, name=f"{tag}_b_wout", ta=True, tm=1024, tn=1024, tk=1024)
    dz_a, small = _conv_a_bwd(z, c, dcat, conv_w, ln_g, ln_b, name=f"{tag}_b_conv")
    dq, dkp, dvp, dsinks = _swa_bwd(z, kpad, vpad, sinks, dcat, name=f"{tag}_b_swa")
    dz = jnp.concatenate([dz_a, dq, dkp[WINDOW:].astype(BF16), dvp[WINDOW:].astype(BF16)], axis=-1)
    G[('even_w_in', 0)] = _mm(u, dz, name=f"{tag}_b_win", ta=True, tm=1024, tn=1792, tk=1024)
    dh_in, dg = _mm_norm_bwd(dz, W['even_w_in'][0], 0, h, g, dh, name=f"{tag}_b_du", tk=1792)
    grads = dict(mix=dg, conv_a_w=small[:CONV_A_WIDTH], conv_a_b=small[32:33], conv_a_ln_g=small[33:34],
                 conv_a_ln_b=small[34:35], swa_sinks=dsinks[:, 0])
    return dh_in, grads


def _odd_fwd(h, g, W, conv_w, tag):
    u, z = _norm_mm(h, g, W['odd_w_in'][0], 0, name=f"{tag}_in")
    y, cc = _sconv_fwd(z, conv_w, name=f"{tag}_conv")
    h2 = _mm(y, W['odd_w_out'][0], b_layer=0, name=f"{tag}_out", out_dtype=F32, tm=1024, tn=1024, tk=1024, res=h)
    return h2, (h, u, z, y, cc)


def _odd_bwd(dh, saved, g, W, conv_w, G, tag):
    h, u, z, y, cc = saved
    dy = _mm(dh, W['odd_w_out'][0], b_layer=0, name=f"{tag}_b_dy", tb=True, tm=1024, tn=1024, tk=1024)
    G[('odd_w_out', 0)] = _mm(y, dh, name=f"{tag}_b_wout", ta=True, tm=1024, tn=1024, tk=1024)
    dz, dw = _sconv_bwd(z, cc, dy, conv_w, name=f"{tag}_b_conv")
    G[('odd_w_in', 0)] = _mm(u, dz, name=f"{tag}_b_win", ta=True, tm=1024, tn=1024, tk=1024)
    dh_in, dg = _mm_norm_bwd(dz, W['odd_w_in'][0], 0, h, g, dh, name=f"{tag}_b_du", tk=1024)
    return dh_in, dict(mix=dg, sc_conv_w=dw[:SC_WIDTH])


def _local_step(x, mem, tgt, W, need, token, ready, conv_a_w, sc_conv_w, P):
    row = lambda v: v.reshape(1, -1)
    conv_a_w = _pad_conv_w(conv_a_w, 32)
    sc_w = _pad_conv_w(sc_conv_w, 8)
    sinks = P['swa_sinks'][0]

    def arrive(stage, h):
        for n, ws in need(stage, h).items():
            W[n] = W.get(n, []) + ws

    h = x
    saved = []
    for i in range(2):
        t = f"l{i}"
        if i == 1:
            arrive('l1_ffn1', h)
        g1 = row(P['ffn1_norm'][i]) + (token if i == 0 else 0.0)
        h, s1 = _ffn_fwd(h, g1, W, 'ffn1_w_gu', 'ffn1_w_down', i, f"{t}_ffn1")
        arrive(f"{t}_mix", h)
        if i == 0:
            h, s2 = _even_fwd(h, row(P['mix_norm'][i]), W, conv_a_w, P['conv_a_b'], P['conv_a_ln_g'],
                              P['conv_a_ln_b'], sinks, f"{t}_even")
        else:
            h, s2 = _odd_fwd(h, row(P['mix_norm'][i]), W, sc_w, f"{t}_odd")
        h, s3 = _xa_block_fwd(h, mem, row(P['xa_norm'][i]), row(P['xa_mem_norm'][i]), W, i, f"{t}_xa")
        arrive(f"{t}_ffn2", h)
        h, s4 = _ffn_fwd(h, row(P['ffn2_norm'][i]), W, 'ffn2_w_gu', 'ffn2_w_down', i, f"{t}_ffn2")
        saved.append((s1, s2, s3, s4))

    loss, dh, d_final = _final_loss(h, row(P['final_norm']), tgt, name="final_loss")

    G = {}
    gp = {n: [None, None] for n in ('ffn1_norm', 'mix_norm', 'xa_norm', 'xa_mem_norm', 'ffn2_norm')}
    single = {}
    for i in (1, 0):
        t = f"l{i}"
        s1, s2, s3, s4 = saved[i]
        g4 = row(P['ffn2_norm'][i]) + (ready('l1', G) if i == 0 else 0.0)
        dh, gp['ffn2_norm'][i] = _ffn_bwd(dh, s4, g4, W, 'ffn2_w_gu', 'ffn2_w_down', i, G, f"{t}_ffn2")
        dh, gp['xa_norm'][i], gp['xa_mem_norm'][i] = _xa_block_bwd(
            dh, s3, mem, row(P['xa_norm'][i]), row(P['xa_mem_norm'][i]), W, i, G, f"{t}_xa")
        if i == 0:
            dh, g2 = _even_bwd(dh, s2, row(P['mix_norm'][i]), W, conv_a_w, P['conv_a_ln_g'], P['conv_a_ln_b'], sinks,
                               G, f"{t}_even")
        else:
            dh, g2 = _odd_bwd(dh, s2, row(P['mix_norm'][i]), W, sc_w, G, f"{t}_odd")
        gp['mix_norm'][i] = g2.pop('mix')
        single.update(g2)
        g1 = row(P['ffn1_norm'][i]) + (ready('l0_rest', G) if i == 0 else 0.0)
        dh, gp['ffn1_norm'][i] = _ffn_bwd(dh, s1, g1, W, 'ffn1_w_gu', 'ffn1_w_down', i, G, f"{t}_ffn1")

    small = {n: jnp.concatenate(v, axis=0) for n, v in gp.items()}
    small['conv_a_b'] = single['conv_a_b']
    small['conv_a_ln_g'] = single['conv_a_ln_g']
    small['conv_a_ln_b'] = single['conv_a_ln_b']
    small['swa_sinks'] = single['swa_sinks'][None]
    small['final_norm'] = d_final[0]
    small['conv_a_w'] = single['conv_a_w']
    small['sc_conv_w'] = single['sc_conv_w']
    return loss[0, 0], dh, G, small


def kernel(x, mem, ffn1_norm, ffn1_w_gu, ffn1_w_down, mix_norm, even_w_in, conv_a_w, conv_a_b, conv_a_ln_g, conv_a_ln_b, swa_sinks, even_w_out, odd_w_in, sc_conv_w, odd_w_out, xa_norm, xa_mem_norm, xa_wq, xa_wkv, xa_wo, ffn2_norm, ffn2_w_gu, ffn2_w_down, final_norm, loss_target, m_ffn1_norm, m_ffn1_w_gu, m_ffn1_w_down, m_mix_norm, m_even_w_in, m_conv_a_w, m_conv_a_b, m_conv_a_ln_g, m_conv_a_ln_b, m_swa_sinks, m_even_w_out, m_odd_w_in, m_sc_conv_w, m_odd_w_out, m_xa_norm, m_xa_mem_norm, m_xa_wq, m_xa_wkv, m_xa_wo, m_ffn2_norm, m_ffn2_w_gu, m_ffn2_w_down, m_final_norm, v_ffn1_norm, v_ffn1_w_gu, v_ffn1_w_down, v_mix_norm, v_even_w_in, v_conv_a_w, v_conv_a_b, v_conv_a_ln_g, v_conv_a_ln_b, v_swa_sinks, v_even_w_out, v_odd_w_in, v_sc_conv_w, v_odd_w_out, v_xa_norm, v_xa_mem_norm, v_xa_wq, v_xa_wkv, v_xa_wo, v_ffn2_norm, v_ffn2_w_gu, v_ffn2_w_down, v_final_norm):
    w = dict(zip(WEIGHT_NAMES, (ffn1_norm, ffn1_w_gu, ffn1_w_down, mix_norm, even_w_in, conv_a_w, conv_a_b, conv_a_ln_g, conv_a_ln_b, swa_sinks, even_w_out, odd_w_in, sc_conv_w, odd_w_out, xa_norm, xa_mem_norm, xa_wq, xa_wkv, xa_wo, ffn2_norm, ffn2_w_gu, ffn2_w_down, final_norm)))
    m = dict(zip(WEIGHT_NAMES, (m_ffn1_norm, m_ffn1_w_gu, m_ffn1_w_down, m_mix_norm, m_even_w_in, m_conv_a_w, m_conv_a_b, m_conv_a_ln_g, m_conv_a_ln_b, m_swa_sinks, m_even_w_out, m_odd_w_in, m_sc_conv_w, m_odd_w_out, m_xa_norm, m_xa_mem_norm, m_xa_wq, m_xa_wkv, m_xa_wo, m_ffn2_norm, m_ffn2_w_gu, m_ffn2_w_down, m_final_norm)))
    v = dict(zip(WEIGHT_NAMES, (v_ffn1_norm, v_ffn1_w_gu, v_ffn1_w_down, v_mix_norm, v_even_w_in, v_conv_a_w, v_conv_a_b, v_conv_a_ln_g, v_conv_a_ln_b, v_swa_sinks, v_even_w_out, v_odd_w_in, v_sc_conv_w, v_odd_w_out, v_xa_norm, v_xa_mem_norm, v_xa_wq, v_xa_wkv, v_xa_wo, v_ffn2_norm, v_ffn2_w_gu, v_ffn2_w_down, v_final_norm)))
    cx, cy, cc = lax.axis_index("x"), lax.axis_index("y"), lax.axis_index("c")
    chip_idx = (2 * cx + cy).astype(jnp.int32).reshape(1)

    shards = {n: w[n] for n in COMM_NAMES if n != 'tiny'}
    shards['tiny'] = _tiny_pack(conv_a_w, sc_conv_w)
    first =[('ffn1_w_gu', 0), ('ffn1_w_down', 0), ('tiny', 0)]
    stages = {
        'l0_mix': [('even_w_in', 0), ('even_w_out', 0), ('xa_wq', 0), ('xa_wkv', 0), ('xa_wo', 0)],
        'l0_ffn2': [('ffn2_w_gu', 0), ('ffn2_w_down', 0)],
        'l1_ffn1': [('ffn1_w_gu', 1), ('ffn1_w_down', 1)],
        'l1_mix': [('odd_w_in', 0), ('odd_w_out', 0), ('xa_wq', 1), ('xa_wkv', 1), ('xa_wo', 1)],
        'l1_ffn2': [('ffn2_w_gu', 1), ('ffn2_w_down', 1)],
    }
    grad_stages = {'l1': stages['l1_ffn1'] + stages['l1_mix'] + stages['l1_ffn2'],
                   'l0_rest': stages['l0_mix'] + stages['l0_ffn2'], 'l0_ffn1': first}

    def place(items, after=None):
        out = []
        for n, l in items:
            out.append(_place(shards[n], l, LAYOUT[n], chip_idx, F32 if n == 'tiny' else BF16,
                              name=f"place_{n}_{l}", after=after))
            after = out[-1] if after is not None else None
        return out

    def item_meta(items):
        return [(LAYOUT[n], 1) + shards[n].shape[1:] for n, l in items]

    def natural(items, arrays):
        out = {}
        for (n, l), a in zip(items, arrays):
            if n == 'tiny':
                continue
            if n == 'even_w_in':
                out[n] = [a.transpose(0, 2, 1, 3).reshape(1, D_MODEL, -1)]
            else:
                out[n] = [a if LAYOUT[n] == 'col' else a.reshape(1, N_CHIPS * a.shape[2], a.shape[3])]
        return out

    meta_first = item_meta(first)
    send, recv, in_flight, token = _gather_start(place(first), meta_first, chip_idx, "first_ici", _half_part_copies)
    placed, last = {}, token
    for stage, items in stages.items():
        placed[stage] = place(items, last)
        last = placed[stage][-1]
    landed = _gather_wait(send, recv, in_flight, meta_first, last, "first_ici", _half_part_copies)
    send, recv, in_flight, token = _gather_start(landed, meta_first, token, "first_d2d", _forward_copies)
    first_d2d = (send, recv, in_flight)
    gathers = {}
    for stage, items in stages.items():
        send, recv, in_flight, token = _gather_start(placed[stage], item_meta(items), token, stage)
        gathers[stage] = (send, recv, in_flight)
    first_full = _gather_wait(*first_d2d, meta_first, token, "first_d2d", _forward_copies)
    W = natural(first, first_full)
    ca, sc = _tiny_unpack(first_full[-1][0])
    conv_a_full = ca.transpose(1, 0, 2).reshape(CONV_A_WIDTH, CONV_A_CH)
    sc_full = sc.transpose(1, 0, 2).reshape(SC_WIDTH, SC_CH)

    def need(stage, h):
        send, recv, in_flight = gathers[stage]
        items = stages[stage]
        return natural(items, _gather_wait(send, recv, in_flight, item_meta(items), h, stage))

    def gathered_layout(G, item):
        n, l = item
        A, B = shards[n].shape[1:]
        g = G[item]
        if n == 'tiny':
            return g
        if n == 'even_w_in':
            return g.reshape(A, N_CHIPS, B).transpose(1, 0, 2)[None]
        return g.reshape(1, A, N_CHIPS * B) if LAYOUT[n] == 'col' else g.reshape(1, N_CHIPS, A, B)

    scatters, tokens = {}, {}

    def ready(stage, G):
        items = grad_stages[stage]
        send, recv, gs1, lands, tok = _scatter_start([gathered_layout(G, it) for it in items], item_meta(items),
                                                     chip_idx, stage)
        scatters[stage] = (send, recv, gs1, lands)
        tokens[stage] = tok
        return tok[:1, :1]

    loss_part, grad_x, G, g_small = _local_step(x[0], mem[0], loss_target[0], W, need, token[:1, :1], ready,
                                                conv_a_full, sc_full, {n: w[n] for n in SMALL_NAMES})
    G[('tiny', 0)] = _tiny_pack(g_small['conv_a_w'].reshape(CONV_A_WIDTH, N_CHIPS, 128).transpose(1, 0, 2),
                                g_small['sc_conv_w'].reshape(SC_WIDTH, N_CHIPS, 256).transpose(1, 0, 2))[None]
    loss = lax.psum(loss_part, ("x", "y", "c"))

    ready('l0_ffn1', G)
    started = tokens['l0_ffn1']

    def summed(stage, after):
        send, recv, gs1, lands = scatters[stage]
        items = grad_stages[stage]
        sent, landed = _scatter_wait(send, recv, gs1, lands, item_meta(items), after, stage)
        return items, [_chip_sum_full(g, r, m_, chip_idx, name=f"rs_chip_sum_{n}_{l}")
                       for (n, l), g, r, m_ in zip(items, sent, landed, item_meta(items))]

    def adamw(n, g1):
        if n == 'tiny':
            pk = lambda d: _tiny_pack(d['conv_a_w'], d['sc_conv_w'])
            res = [_tiny_unpack(a) for a in _adamw_layers(pk(w), pk(m), pk(v), [g1[('tiny', 0)]], name="adamw_tiny")]
            for k, nn in enumerate(('conv_a_w', 'sc_conv_w')):
                grads[nn], deltas[nn], new_m[nn], new_v[nn] = (r[k] for r in res)
        else:
            gsrc = [g1[(n, l)] for l in range(w[n].shape[0])]
            grads[n], deltas[n], new_m[n], new_v[n] = _adamw_layers(w[n], m[n], v[n], gsrc, name=f"adamw_{n}")

    grads, deltas, new_m, new_v = {}, {}, {}, {}
    sum_of = {}
    for stage in ('l1', 'l0_rest'):
        its, ss = summed(stage, started)
        sum_of.update(zip(its, ss))
    swap_groups = [['even_w_in', 'even_w_out', 'odd_w_in', 'odd_w_out', 'xa_wq', 'xa_wkv', 'xa_wo'],
                   ['ffn2_w_gu', 'ffn2_w_down'], ['ffn1_w_gu', 'ffn1_w_down']]
    swaps, after = [], started
    for gi, names in enumerate(swap_groups):
        its = [it for it in sum_of if it[0] in names]
        send, recv, own, lands, after = _swap_start([sum_of[it] for it in its], after, f"g{gi}")
        swaps.append((its, send, recv, own, lands))
    _, small_parts = _pair_swap([], _pack_small(g_small), tag="small")
    g1 = {}

    def swapped(gi, after):
        its, send, recv, own, lands = swaps[gi]
        mine, theirs = _swap_wait(send, recv, own, lands, after, f"g{gi}")
        g1.update({it: [a, b] for it, a, b in zip(its, mine, theirs)})

    for gi in (0, 1):
        swapped(gi, after)
        for n in swap_groups[gi]:
            adamw(n, g1)
        after = deltas[swap_groups[gi][-1]]

    swapped(2, after)
    its, ss = summed('l0_ffn1', after)
    sib, _ = _pair_swap(ss, None, tag="last")
    g1.update({it: [a, b] for it, a, b in zip(its, ss, sib)})
    for n in ('ffn1_w_gu', 'ffn1_w_down', 'tiny'):
        adamw(n, g1)
    rows2d = lambda d: [d[n].reshape(-1, d[n].shape[-1]) for n in SMALL_NAMES]
    for dst, arrs in zip((grads, deltas, new_m, new_v),
                         _adamw_small(rows2d(w), rows2d(m), rows2d(v), small_parts, name="adamw_small")):
        dst.update({n: a.reshape(w[n].shape) for n, a in zip(SMALL_NAMES, arrs)})

    return (loss, grad_x[None], *[grads[n] for n in WEIGHT_NAMES], *[deltas[n] for n in WEIGHT_NAMES],
            *[new_m[n] for n in WEIGHT_NAMES], *[new_v[n] for n in WEIGHT_NAMES])
```
